```python
import jax, jax.numpy as jnp
from jax import lax
import numpy as np

D_MODEL = 1024
BATCH = 8
SEQ = 8192
DEPTH = 1

GRID_W = 64
CTX_LEN = 256
CHUNK = 128
MIX_WIDTH = D_MODEL
A_WIDTH = MIX_WIDTH // 2
A_GROUPS = 4
A_GROUP_DIM = A_WIDTH // A_GROUPS
R_WIDTH = MIX_WIDTH - A_WIDTH
R_HEADS = 4
R_HEAD_DIM = R_WIDTH // R_HEADS
IN_COLS = 2 * A_WIDTH + 5 * R_WIDTH
D_FF = -(-8 * D_MODEL // (3 * 256)) * 256
RMS_EPS = 1e-6
ROPE_BASE = 10000.0

kernel_name = "hymba_gmlp_retention_prefix_dit_block"


def rmsnorm(x, g):
    x32 = x.astype(jnp.float32)
    y = x32 * lax.rsqrt(jnp.mean(x32 * x32, axis=-1, keepdims=True) + RMS_EPS)
    return (y * g.astype(jnp.float32)).astype(x.dtype)


def head_norm(o):
    return o * lax.rsqrt(jnp.mean(o * o, axis=-1, keepdims=True) + RMS_EPS)


def modulate(h, shift, scale):
    return h * (1.0 + scale[:, None, :]) + shift[:, None, :]


def rope_2d(t, rows, cols):
    half = t.shape[-1] // 2
    n_freq = half // 2
    inv = ROPE_BASE ** (-jnp.arange(n_freq, dtype=jnp.float32) / n_freq)

    def rot(u, pos):
        ang = pos[:, None] * inv[None, :]
        cos = jnp.cos(ang)[None, :, None, :]
        sin = jnp.sin(ang)[None, :, None, :]
        u1, u2 = u[..., :n_freq], u[..., n_freq:]
        return jnp.concatenate([u1 * cos - u2 * sin, u1 * sin + u2 * cos], axis=-1)

    return jnp.concatenate([rot(t[..., :half], rows), rot(t[..., half:], cols)], axis=-1)


def chunk_states(k, v, log_gamma, s0):
    B, L, H, dk = k.shape
    n = L // CHUNK
    kc = k.reshape(B, n, CHUNK, H, dk)
    vc = v.reshape(B, n, CHUNK, H, v.shape[-1])
    pos = jnp.arange(CHUNK, dtype=jnp.float32)
    zeta = jnp.exp(log_gamma[:, None] * (CHUNK - 1 - pos)[None, :])
    upd = jnp.einsum('bnjhd,hj,bnjhe->nbhde', kc, zeta, vc)
    decay = jnp.exp(log_gamma * CHUNK)[None, :, None, None]

    def step(s, u):
        return decay * s + u, s

    s_final, s_prev = lax.scan(step, s0, upd)
    return s_prev, s_final


def retention_out(q, k, v, log_gamma, s_prev):
    B, L, H, dk = q.shape
    n = L // CHUNK
    qc = q.reshape(B, n, CHUNK, H, dk)
    kc = k.reshape(B, n, CHUNK, H, dk)
    vc = v.reshape(B, n, CHUNK, H, v.shape[-1])
    pos = jnp.arange(CHUNK, dtype=jnp.float32)
    diff = pos[:, None] - pos[None, :]
    dmask = jnp.where(diff[None] >= 0,
                      jnp.exp(log_gamma[:, None, None] * jnp.maximum(diff, 0.0)[None]), 0.0)
    scores = jnp.einsum('bnihd,bnjhd->bnhij', qc, kc) * dmask[None, None]
    inner = jnp.einsum('bnhij,bnjhe->bnihe', scores, vc)
    xi = jnp.exp(log_gamma[:, None] * (pos + 1.0)[None, :])
    cross = jnp.einsum('bnihd,hi,nbhde->bnihe', qc, xi, s_prev)
    return (inner + cross).reshape(B, L, H, v.shape[-1])


def spatial_gating(zu, zv, sg_gain, sg_w, sg_b):
    B, L, _ = zu.shape
    n = L // CHUNK
    u = jax.nn.gelu(zu)
    v = jax.nn.gelu(zv).reshape(B, n, CHUNK, A_GROUPS, A_GROUP_DIM)
    v32 = v.astype(jnp.float32)
    v = (v32 * lax.rsqrt(jnp.mean(v32 * v32, axis=-1, keepdims=True) + RMS_EPS)
         * sg_gain.reshape(A_GROUPS, A_GROUP_DIM)).astype(zu.dtype)
    mixed = jnp.einsum('gpq,bnqgc->bnpgc', sg_w, v) + jnp.transpose(sg_b)[None, None, :, :, None]
    return u * mixed.reshape(B, L, A_WIDTH)


def split_proj(z):
    cuts = [A_WIDTH, 2 * A_WIDTH] + [2 * A_WIDTH + i * R_WIDTH for i in range(1, 5)]
    return jnp.split(z, cuts, axis=-1)


def heads(t):
    B, L, _ = t.shape
    return t.astype(jnp.float32).reshape(B, L, R_HEADS, R_HEAD_DIM)


def ffn(h, w_gate, w_up, w_down):
    return (jax.nn.silu(h @ w_gate) * (h @ w_up)) @ w_down


def _fwd_setup_inputs(seed: int = 0) -> dict:
    key = jax.random.key(seed)
    ks = jax.random.split(key, 24)
    f32 = jnp.float32
    nrm = lambda k, s: jax.random.normal(k, s, dtype=f32)
    gamma0 = 1.0 - 2.0 ** (-5.0 - np.arange(R_HEADS, dtype=np.float32))
    logit0 = jnp.asarray(np.log(gamma0) - np.log1p(-gamma0), dtype=f32)
    return {
        "x": nrm(ks[0], (BATCH, SEQ, D_MODEL)),
        "c": nrm(ks[1], (BATCH, D_MODEL)),
        "ctx": nrm(ks[2], (BATCH, CTX_LEN, D_MODEL)),
        "c_ctx": nrm(ks[3], (D_MODEL,)),
        "w_mod": nrm(ks[4], (DEPTH, D_MODEL, 6 * D_MODEL)) * (0.5 * D_MODEL ** -0.5),
        "b_mod": nrm(ks[5], (DEPTH, 6 * D_MODEL)) * 0.02,
        "norm1": 1.0 + 0.02 * nrm(ks[6], (DEPTH, D_MODEL)),
        "w_in": nrm(ks[7], (DEPTH, D_MODEL, IN_COLS)) * D_MODEL ** -0.5,
        "sg_gain": 1.0 + 0.02 * nrm(ks[8], (DEPTH, A_WIDTH)),
        "sg_w": nrm(ks[9], (DEPTH, A_GROUPS, CHUNK, CHUNK)) * CHUNK ** -0.5,
        "sg_b": 1.0 + 0.02 * nrm(ks[10], (DEPTH, A_GROUPS, CHUNK)),
        "ret_logit_f": logit0[None, :] + 0.05 * nrm(ks[11], (DEPTH, R_HEADS)),
        "ret_logit_b": logit0[None, :] + 0.05 * nrm(ks[12], (DEPTH, R_HEADS)),
        "w_out": nrm(ks[13], (DEPTH, MIX_WIDTH, D_MODEL)) * MIX_WIDTH ** -0.5,
        "norm2": 1.0 + 0.02 * nrm(ks[14], (DEPTH, D_MODEL)),
        "w_gate": nrm(ks[15], (DEPTH, D_MODEL, D_FF)) * D_MODEL ** -0.5,
        "w_up": nrm(ks[16], (DEPTH, D_MODEL, D_FF)) * D_MODEL ** -0.5,
        "w_down": nrm(ks[17], (DEPTH, D_FF, D_MODEL)) * D_FF ** -0.5,
        "norm_f": 1.0 + 0.02 * nrm(ks[18], (D_MODEL,)),
    }


def _fwd_reference(x, c, ctx, c_ctx, w_mod, b_mod, norm1, w_in, sg_gain, sg_w, sg_b,
              ret_logit_f, ret_logit_b, w_out, norm2, w_gate, w_up, w_down, norm_f):
    B, L, _ = x.shape
    rows_n = L // GRID_W
    rows = jnp.repeat(jnp.arange(rows_n, dtype=jnp.float32), GRID_W)
    cols = jnp.tile(jnp.arange(GRID_W, dtype=jnp.float32), rows_n)
    s_zero = jnp.zeros((B, R_HEADS, R_HEAD_DIM, R_HEAD_DIM), jnp.float32)
    k_scale = R_HEAD_DIM ** -0.5
    kv_lo = 2 * A_WIDTH + R_WIDTH
    kv_hi = 2 * A_WIDTH + 3 * R_WIDTH

    for l in range(DEPTH):
        last = l == DEPTH - 1
        sh1, sc1, g1, sh2, sc2, g2 = jnp.split(jax.nn.silu(c) @ w_mod[l] + b_mod[l], 6, axis=-1)
        csh1, csc1, cg1, csh2, csc2, cg2 = jnp.split(
            (jax.nn.silu(c_ctx) @ w_mod[l] + b_mod[l])[None, :], 6, axis=-1)
        lg_f = jax.nn.log_sigmoid(ret_logit_f[l].astype(jnp.float32))
        lg_b = jax.nn.log_sigmoid(ret_logit_b[l].astype(jnp.float32))

        hc = modulate(rmsnorm(ctx, norm1[l]), csh1, csc1)
        if last:
            kc_r, vc_r = jnp.split(hc @ w_in[l][:, kv_lo:kv_hi], 2, axis=-1)
        else:
            cu, cv, cq_r, kc_r, vc_r, cgf, cgb = split_proj(hc @ w_in[l])
        kc = heads(kc_r) * k_scale
        vc = heads(vc_r)
        kc_rev, vc_rev = jnp.flip(kc, 1), jnp.flip(vc, 1)
        sp_cf, s_cf = chunk_states(kc, vc, lg_f, s_zero)
        sp_cb, s_cb = chunk_states(kc_rev, vc_rev, lg_b, s_zero)
        if not last:
            qc = heads(cq_r)
            o_cf = retention_out(qc, kc, vc, lg_f, sp_cf)
            o_cb = jnp.flip(retention_out(jnp.flip(qc, 1), kc_rev, vc_rev, lg_b, sp_cb), 1)
            Lc = ctx.shape[1]
            yc_r = (jax.nn.silu(cgf.astype(jnp.float32)) * head_norm(o_cf).reshape(B, Lc, R_WIDTH)
                    + jax.nn.silu(cgb.astype(jnp.float32)) * head_norm(o_cb).reshape(B, Lc, R_WIDTH))
            yc_a = spatial_gating(cu, cv, sg_gain[l], sg_w[l], sg_b[l])
            yc = jnp.concatenate([yc_a, yc_r.astype(ctx.dtype)], axis=-1) @ w_out[l]
            ctx_new = ctx + cg1[:, None, :] * yc
            hc2 = modulate(rmsnorm(ctx_new, norm2[l]), csh2, csc2)
            ctx_next = ctx_new + cg2[:, None, :] * ffn(hc2, w_gate[l], w_up[l], w_down[l])

        hx = modulate(rmsnorm(x, norm1[l]), sh1, sc1)
        u, v, q_r, k_r, v_r, gf, gb = split_proj(hx @ w_in[l])
        y_a = spatial_gating(u, v, sg_gain[l], sg_w[l], sg_b[l])
        q = rope_2d(heads(q_r), rows, cols)
        k = rope_2d(heads(k_r), rows, cols) * k_scale
        vv = heads(v_r)
        sp_f, _ = chunk_states(k, vv, lg_f, s_cf)
        o_f = retention_out(q, k, vv, lg_f, sp_f)
        q_rev, k_rev, v_rev = jnp.flip(q, 1), jnp.flip(k, 1), jnp.flip(vv, 1)
        sp_b, _ = chunk_states(k_rev, v_rev, lg_b, s_cb)
        o_b = jnp.flip(retention_out(q_rev, k_rev, v_rev, lg_b, sp_b), 1)
        y_r = (jax.nn.silu(gf.astype(jnp.float32)) * head_norm(o_f).reshape(B, L, R_WIDTH)
               + jax.nn.silu(gb.astype(jnp.float32)) * head_norm(o_b).reshape(B, L, R_WIDTH))
        y = jnp.concatenate([y_a, y_r.astype(x.dtype)], axis=-1) @ w_out[l]
        x = x + g1[:, None, :] * y
        h2 = modulate(rmsnorm(x, norm2[l]), sh2, sc2)
        x = x + g2[:, None, :] * ffn(h2, w_gate[l], w_up[l], w_down[l])
        if not last:
            ctx = ctx_next

    return rmsnorm(x, norm_f)


import jax as _jax
import jax.numpy as _jnp

TWIN_FORMAT = 'train_step'
FWD_PARAMS = ['x', 'c', 'ctx', 'c_ctx', 'w_mod', 'b_mod', 'norm1', 'w_in', 'sg_gain', 'sg_w', 'sg_b', 'ret_logit_f', 'ret_logit_b', 'w_out', 'norm2', 'w_gate', 'w_up', 'w_down', 'norm_f']
TWIN_WEIGHTS = ['c_ctx', 'w_mod', 'b_mod', 'norm1', 'w_in', 'sg_gain', 'sg_w', 'sg_b', 'ret_logit_f', 'ret_logit_b', 'w_out', 'norm2', 'w_gate', 'w_up', 'w_down', 'norm_f']
TWIN_DIFF_INPUT = 'x'
TWIN_INPUTS = ['x', 'c', 'ctx', 'c_ctx', 'w_mod', 'b_mod', 'norm1', 'w_in', 'sg_gain', 'sg_w', 'sg_b', 'ret_logit_f', 'ret_logit_b', 'w_out', 'norm2', 'w_gate', 'w_up', 'w_down', 'norm_f', 'loss_target', 'm_c_ctx', 'm_w_mod', 'm_b_mod', 'm_norm1', 'm_w_in', 'm_sg_gain', 'm_sg_w', 'm_sg_b', 'm_ret_logit_f', 'm_ret_logit_b', 'm_w_out', 'm_norm2', 'm_w_gate', 'm_w_up', 'm_w_down', 'm_norm_f', 'v_c_ctx', 'v_w_mod', 'v_b_mod', 'v_norm1', 'v_w_in', 'v_sg_gain', 'v_sg_w', 'v_sg_b', 'v_ret_logit_f', 'v_ret_logit_b', 'v_w_out', 'v_norm2', 'v_w_gate', 'v_w_up', 'v_w_down', 'v_norm_f']
TWIN_OUTPUTS = ['loss', 'grad_x', 'grad_c_ctx', 'grad_w_mod', 'grad_b_mod', 'grad_norm1', 'grad_w_in', 'grad_sg_gain', 'grad_sg_w', 'grad_sg_b', 'grad_ret_logit_f', 'grad_ret_logit_b', 'grad_w_out', 'grad_norm2', 'grad_w_gate', 'grad_w_up', 'grad_w_down', 'grad_norm_f', 'delta_c_ctx', 'delta_w_mod', 'delta_b_mod', 'delta_norm1', 'delta_w_in', 'delta_sg_gain', 'delta_sg_w', 'delta_sg_b', 'delta_ret_logit_f', 'delta_ret_logit_b', 'delta_w_out', 'delta_norm2', 'delta_w_gate', 'delta_w_up', 'delta_w_down', 'delta_norm_f', 'new_m_c_ctx', 'new_m_w_mod', 'new_m_b_mod', 'new_m_norm1', 'new_m_w_in', 'new_m_sg_gain', 'new_m_sg_w', 'new_m_sg_b', 'new_m_ret_logit_f', 'new_m_ret_logit_b', 'new_m_w_out', 'new_m_norm2', 'new_m_w_gate', 'new_m_w_up', 'new_m_w_down', 'new_m_norm_f', 'new_v_c_ctx', 'new_v_w_mod', 'new_v_b_mod', 'new_v_norm1', 'new_v_w_in', 'new_v_sg_gain', 'new_v_sg_w', 'new_v_sg_b', 'new_v_ret_logit_f', 'new_v_ret_logit_b', 'new_v_w_out', 'new_v_norm2', 'new_v_w_gate', 'new_v_w_up', 'new_v_w_down', 'new_v_norm_f']
TWIN_LEAF_KINDS = {'loss': 'loss', 'grad_x': 'grad_x', 'grad_c_ctx': 'grad_w', 'grad_w_mod': 'grad_w', 'grad_b_mod': 'grad_w', 'grad_norm1': 'grad_w', 'grad_w_in': 'grad_w', 'grad_sg_gain': 'grad_w', 'grad_sg_w': 'grad_w', 'grad_sg_b': 'grad_w', 'grad_ret_logit_f': 'grad_w', 'grad_ret_logit_b': 'grad_w', 'grad_w_out': 'grad_w', 'grad_norm2': 'grad_w', 'grad_w_gate': 'grad_w', 'grad_w_up': 'grad_w', 'grad_w_down': 'grad_w', 'grad_norm_f': 'grad_w', 'delta_c_ctx': 'delta_w', 'delta_w_mod': 'delta_w', 'delta_b_mod': 'delta_w', 'delta_norm1': 'delta_w', 'delta_w_in': 'delta_w', 'delta_sg_gain': 'delta_w', 'delta_sg_w': 'delta_w', 'delta_sg_b': 'delta_w', 'delta_ret_logit_f': 'delta_w', 'delta_ret_logit_b': 'delta_w', 'delta_w_out': 'delta_w', 'delta_norm2': 'delta_w', 'delta_w_gate': 'delta_w', 'delta_w_up': 'delta_w', 'delta_w_down': 'delta_w', 'delta_norm_f': 'delta_w', 'new_m_c_ctx': 'new_m', 'new_m_w_mod': 'new_m', 'new_m_b_mod': 'new_m', 'new_m_norm1': 'new_m', 'new_m_w_in': 'new_m', 'new_m_sg_gain': 'new_m', 'new_m_sg_w': 'new_m', 'new_m_sg_b': 'new_m', 'new_m_ret_logit_f': 'new_m', 'new_m_ret_logit_b': 'new_m', 'new_m_w_out': 'new_m', 'new_m_norm2': 'new_m', 'new_m_w_gate': 'new_m', 'new_m_w_up': 'new_m', 'new_m_w_down': 'new_m', 'new_m_norm_f': 'new_m', 'new_v_c_ctx': 'new_v', 'new_v_w_mod': 'new_v', 'new_v_b_mod': 'new_v', 'new_v_norm1': 'new_v', 'new_v_w_in': 'new_v', 'new_v_sg_gain': 'new_v', 'new_v_sg_w': 'new_v', 'new_v_sg_b': 'new_v', 'new_v_ret_logit_f': 'new_v', 'new_v_ret_logit_b': 'new_v', 'new_v_w_out': 'new_v', 'new_v_norm2': 'new_v', 'new_v_w_gate': 'new_v', 'new_v_w_up': 'new_v', 'new_v_w_down': 'new_v', 'new_v_norm_f': 'new_v'}


def _forward(args):
    return _fwd_reference(*[args[k] for k in FWD_PARAMS])


def _output_shape():
    def fwd():
        inp = _fwd_setup_inputs(0)
        return _fwd_reference(*[inp[k] for k in FWD_PARAMS])
    out = _jax.eval_shape(fwd)
    return out.shape, out.dtype

N_MICROBATCH = 1
ADAM_LR = 0.001
ADAM_B1 = 0.9
ADAM_B2 = 0.999
ADAM_EPS = 1e-08
ADAM_WD = 0.01
ADAM_STEP = 10
PER_EXAMPLE_BATCH_AXIS = {'x': 0, 'c': 0, 'ctx': 0, 'loss_target': 0}
SHARED_INPUTS = []
_WEIGHT_DTYPES = {'c_ctx': _jnp.float32, 'w_mod': _jnp.float32, 'b_mod': _jnp.float32, 'norm1': _jnp.float32, 'w_in': _jnp.float32, 'sg_gain': _jnp.float32, 'sg_w': _jnp.float32, 'sg_b': _jnp.float32, 'ret_logit_f': _jnp.float32, 'ret_logit_b': _jnp.float32, 'w_out': _jnp.float32, 'norm2': _jnp.float32, 'w_gate': _jnp.float32, 'w_up': _jnp.float32, 'w_down': _jnp.float32, 'norm_f': _jnp.float32}
MOMENT_SCALE = {'c_ctx': 2.544605e-02, 'w_mod': 9.984257e-02, 'b_mod': 1.752043e-01, 'norm1': 1.186046e-01, 'w_in': 6.718931e-02, 'sg_gain': 5.287224e-02, 'sg_w': 5.450536e-02, 'sg_b': 6.092478e-02, 'ret_logit_f': 2.289980e-01, 'ret_logit_b': 2.394715e-01, 'w_out': 7.294074e-02, 'norm2': 7.325628e-02, 'w_gate': 3.191100e-02, 'w_up': 3.086945e-02, 'w_down': 5.114469e-02, 'norm_f': 6.408963e+01}


def _to_microbatches(a, axis):
    t = _jnp.moveaxis(a, axis, 0)
    t = t.reshape((N_MICROBATCH, t.shape[0] // N_MICROBATCH) + t.shape[1:])
    return _jnp.moveaxis(t, 1, axis + 1)


def setup_inputs(seed: int = 0) -> dict:
    inp = _fwd_setup_inputs(seed)
    key = _jax.random.fold_in(_jax.random.key(seed), 7919)
    shape, _ = _output_shape()
    out = dict(inp)
    out["loss_target"] = _jax.random.normal(_jax.random.fold_in(key, 0), shape, _jnp.float32)
    for i, name in enumerate(TWIN_WEIGHTS):
        w = inp[name].astype(_jnp.float32)
        if MOMENT_SCALE is None:
            s = _jnp.sqrt(_jnp.mean(_jnp.square(w)) + 1e-30)
        else:
            s = MOMENT_SCALE[name]
        km, kv = _jax.random.split(_jax.random.fold_in(key, i + 1))
        out[name] = w
        out["m_" + name] = s * _jax.random.normal(km, w.shape, _jnp.float32)
        out["v_" + name] = (s * s) * _jax.random.uniform(kv, w.shape, _jnp.float32, 0.5, 1.5)
    if N_MICROBATCH > 1:
        for name, axis in PER_EXAMPLE_BATCH_AXIS.items():
            out[name] = _to_microbatches(out[name], axis)
    return {'x': out['x'], 'c': out['c'], 'ctx': out['ctx'], 'c_ctx': out['c_ctx'], 'w_mod': out['w_mod'], 'b_mod': out['b_mod'], 'norm1': out['norm1'], 'w_in': out['w_in'], 'sg_gain': out['sg_gain'], 'sg_w': out['sg_w'], 'sg_b': out['sg_b'], 'ret_logit_f': out['ret_logit_f'], 'ret_logit_b': out['ret_logit_b'], 'w_out': out['w_out'], 'norm2': out['norm2'], 'w_gate': out['w_gate'], 'w_up': out['w_up'], 'w_down': out['w_down'], 'norm_f': out['norm_f'], 'loss_target': out['loss_target'], 'm_c_ctx': out['m_c_ctx'], 'm_w_mod': out['m_w_mod'], 'm_b_mod': out['m_b_mod'], 'm_norm1': out['m_norm1'], 'm_w_in': out['m_w_in'], 'm_sg_gain': out['m_sg_gain'], 'm_sg_w': out['m_sg_w'], 'm_sg_b': out['m_sg_b'], 'm_ret_logit_f': out['m_ret_logit_f'], 'm_ret_logit_b': out['m_ret_logit_b'], 'm_w_out': out['m_w_out'], 'm_norm2': out['m_norm2'], 'm_w_gate': out['m_w_gate'], 'm_w_up': out['m_w_up'], 'm_w_down': out['m_w_down'], 'm_norm_f': out['m_norm_f'], 'v_c_ctx': out['v_c_ctx'], 'v_w_mod': out['v_w_mod'], 'v_b_mod': out['v_b_mod'], 'v_norm1': out['v_norm1'], 'v_w_in': out['v_w_in'], 'v_sg_gain': out['v_sg_gain'], 'v_sg_w': out['v_sg_w'], 'v_sg_b': out['v_sg_b'], 'v_ret_logit_f': out['v_ret_logit_f'], 'v_ret_logit_b': out['v_ret_logit_b'], 'v_w_out': out['v_w_out'], 'v_norm2': out['v_norm2'], 'v_w_gate': out['v_w_gate'], 'v_w_up': out['v_w_up'], 'v_w_down': out['v_w_down'], 'v_norm_f': out['v_norm_f']}


def _loss(weights, diff, rest, loss_target):
    with _jax.named_scope("forward"):
        args = {**rest, TWIN_DIFF_INPUT: diff, **{k: w.astype(_WEIGHT_DTYPES[k]) for k, w in weights.items()}}
        y = _forward(args)
    with _jax.named_scope("loss_head"):
        err = _jnp.square(y.astype(_jnp.float32) - loss_target)
        return 0.5 * _jnp.sum(_jnp.mean(err, axis=-1)) if err.ndim else 0.5 * err


def _adamw(w, g, m, v):
    m = ADAM_B1 * m + (1.0 - ADAM_B1) * g
    v = ADAM_B2 * v + (1.0 - ADAM_B2) * _jnp.square(g)
    m_hat = m / (1.0 - ADAM_B1 ** ADAM_STEP)
    v_hat = v / (1.0 - ADAM_B2 ** ADAM_STEP)
    delta = -ADAM_LR * (m_hat / (_jnp.sqrt(v_hat) + ADAM_EPS) + ADAM_WD * w)
    return delta, m, v


def reference(x, c, ctx, c_ctx, w_mod, b_mod, norm1, w_in, sg_gain, sg_w, sg_b, ret_logit_f, ret_logit_b, w_out, norm2, w_gate, w_up, w_down, norm_f, loss_target, m_c_ctx, m_w_mod, m_b_mod, m_norm1, m_w_in, m_sg_gain, m_sg_w, m_sg_b, m_ret_logit_f, m_ret_logit_b, m_w_out, m_norm2, m_w_gate, m_w_up, m_w_down, m_norm_f, v_c_ctx, v_w_mod, v_b_mod, v_norm1, v_w_in, v_sg_gain, v_sg_w, v_sg_b, v_ret_logit_f, v_ret_logit_b, v_w_out, v_norm2, v_w_gate, v_w_up, v_w_down, v_norm_f):
    given = dict(x=x, c=c, ctx=ctx, c_ctx=c_ctx, w_mod=w_mod, b_mod=b_mod, norm1=norm1, w_in=w_in, sg_gain=sg_gain, sg_w=sg_w, sg_b=sg_b, ret_logit_f=ret_logit_f, ret_logit_b=ret_logit_b, w_out=w_out, norm2=norm2, w_gate=w_gate, w_up=w_up, w_down=w_down, norm_f=norm_f, loss_target=loss_target, m_c_ctx=m_c_ctx, m_w_mod=m_w_mod, m_b_mod=m_b_mod, m_norm1=m_norm1, m_w_in=m_w_in, m_sg_gain=m_sg_gain, m_sg_w=m_sg_w, m_sg_b=m_sg_b, m_ret_logit_f=m_ret_logit_f, m_ret_logit_b=m_ret_logit_b, m_w_out=m_w_out, m_norm2=m_norm2, m_w_gate=m_w_gate, m_w_up=m_w_up, m_w_down=m_w_down, m_norm_f=m_norm_f, v_c_ctx=v_c_ctx, v_w_mod=v_w_mod, v_b_mod=v_b_mod, v_norm1=v_norm1, v_w_in=v_w_in, v_sg_gain=v_sg_gain, v_sg_w=v_sg_w, v_sg_b=v_sg_b, v_ret_logit_f=v_ret_logit_f, v_ret_logit_b=v_ret_logit_b, v_w_out=v_w_out, v_norm2=v_norm2, v_w_gate=v_w_gate, v_w_up=v_w_up, v_w_down=v_w_down, v_norm_f=v_norm_f)
    weights = {n: given[n] for n in TWIN_WEIGHTS}
    shared = {n: given[n] for n in SHARED_INPUTS}
    per_example = {n: given[n] for n in ['x', 'c', 'ctx']}
    grad_fn = _jax.value_and_grad(_loss, argnums=(0, 1))

    def one_microbatch(ex, loss_target):
        ex = dict(ex)
        diff = ex.pop(TWIN_DIFF_INPUT)
        return grad_fn(weights, diff, {**shared, **ex}, loss_target)

    if N_MICROBATCH == 1:
        loss, (grad_w, grad_x) = one_microbatch(per_example, given["loss_target"])
    else:
        def body(carry, xs):
            loss_sum, grad_sum = carry
            l_k, (gw_k, gx_k) = one_microbatch(xs[0], xs[1])
            with _jax.named_scope("update"):
                return (loss_sum + l_k, _jax.tree.map(_jnp.add, grad_sum, gw_k)), gx_k

        init = (_jnp.zeros((), _jnp.float32), _jax.tree.map(_jnp.zeros_like, weights))
        (loss, grad_w), grad_x = _jax.lax.scan(body, init, (per_example, given["loss_target"]))
    with _jax.named_scope("update"):
        delta_w, new_m, new_v = {}, {}, {}
        for n in TWIN_WEIGHTS:
            delta_w[n], new_m[n], new_v[n] = _adamw(weights[n], grad_w[n], given["m_" + n], given["v_" + n])
    return (loss, grad_x, *[grad_w[n] for n in TWIN_WEIGHTS], *[delta_w[n] for n in TWIN_WEIGHTS],
            *[new_m[n] for n in TWIN_WEIGHTS], *[new_v[n] for n in TWIN_WEIGHTS])
```

```python
import functools

import jax
import jax.numpy as jnp
from jax import lax
from jax.experimental import pallas as pl
from jax.experimental.pallas import tpu as pltpu

F32 = jnp.float32
BF = jnp.bfloat16
MESH = pl.DeviceIdType.MESH

D = 1024
CH = 128
HD = 128
NH = 4
AW = 512
IN_COLS = 3584
DFF = 2816
NCHIP = 4
NDEV = 8
WI_C = IN_COLS // NCHIP
FF_C = DFF // NCHIP
WO_R = D // NCHIP
EPS = 1e-6
GRID_W = 64
ROPE_BASE = 10000.0
K_SCALE = HD ** -0.5
LR, B1, B2, AEPS, WD, STEP = 0.001, 0.9, 0.999, 1e-08, 0.01, 10
VMEM_MB = 1 << 20
HI = lax.Precision.HIGHEST

P_DCMOD, P_DMOD, P_N1, P_GAIN, P_SGW, P_SGB, P_LG, P_N2, P_NF, P_LOSS = 0, 48, 96, 104, 108, 620, 624, 632, 640, 648
P_ROWS = 656
Q_BMOD, Q_N1, Q_GAIN, Q_SGW, Q_SGB, Q_LG, Q_N2, Q_NF = 0, 48, 56, 60, 572, 576, 584, 592
Q_ROWS = 600


def _params(vmem_mb, sem=None):
    return pltpu.CompilerParams(vmem_limit_bytes=vmem_mb * VMEM_MB, dimension_semantics=sem)


def _const(shape):
    nd = len(shape)
    return pl.BlockSpec(shape, lambda *_: (0,) * nd, pipeline_mode=pl.Buffered(1))


def _pos():
    return lax.axis_index("x"), lax.axis_index("y"), lax.axis_index("c")


def _dot(a, b, dims):
    return lax.dot_general(a, b, (dims, ((), ())), preferred_element_type=F32)


NN = ((1,), (0,))
NT = ((1,), (1,))
TN = ((0,), (0,))


@jax.custom_vjp
def _mm(a, b):
    return _dot(a.astype(BF), b.astype(BF), NN)


def _mm_f(a, b):
    return _mm(a, b), (a.astype(BF), b.astype(BF))


def _mm_b(res, g):
    a, b = res
    gb = g.astype(BF)
    return _dot(gb, b, NT), _dot(a, gb, TN)


_mm.defvjp(_mm_f, _mm_b)


@jax.custom_vjp
def _mm_nt(a, b):
    return _dot(a.astype(BF), b.astype(BF), NT)


def _mm_nt_f(a, b):
    return _mm_nt(a, b), (a.astype(BF), b.astype(BF))


def _mm_nt_b(res, g):
    a, b = res
    gb = g.astype(BF)
    return _dot(gb, b, NN), _dot(gb, a, TN)


_mm_nt.defvjp(_mm_nt_f, _mm_nt_b)


@jax.custom_vjp
def _mm_tn(a, b):
    return _dot(a.astype(BF), b.astype(BF), TN)


def _mm_tn_f(a, b):
    return _mm_tn(a, b), (a.astype(BF), b.astype(BF))


def _mm_tn_b(res, g):
    a, b = res
    gb = g.astype(BF)
    return _dot(b, gb, NT), _dot(a, gb, NN)


_mm_tn.defvjp(_mm_tn_f, _mm_tn_b)


def _gelu(x):
    return x * (0.5 * (1.0 + jnp.tanh(0.7978845608028654 * (x + 0.044715 * (x * x * x)))))


def _silu(x):
    return x * jax.nn.sigmoid(x)


def _rms(x):
    return lax.rsqrt(jnp.mean(x * x, axis=-1, keepdims=True) + EPS)


def _swap32(t):
    lane = lax.broadcasted_iota(jnp.int32, t.shape, 1)
    first = (lane % 64) < 32
    return jnp.where(first, pltpu.roll(t, 96, 1), pltpu.roll(t, 32, 1))


def _rope(t, cos, sin):
    return t * cos + _swap32(t) * sin


def _rope_bwd(d, cos, sin):
    return d * cos + _swap32(d * sin)


def _heads(ref, off=0):
    return [ref[:, off + h * HD: off + (h + 1) * HD].astype(F32) for h in range(NH)]


def _chunk_fwd(u, v, q, k, vr, gf, gb, sf, sb, df, xf, zf, db, xb, zb, sgw, gain, bfull):
    ya, yr, uf, ub = [], [], [], []
    for g in range(NH):
        gu = _gelu(u[g])
        gv = _gelu(v[g])
        vn = gv * _rms(gv) * gain[g]
        ya.append(gu * (_mm(sgw[g], vn) + bfull[g]))
    for h in range(NH):
        a = _mm_nt(q[h], k[h])
        of = _mm(a * df[h], vr[h]) + xf[h] * _mm(q[h], sf[h])
        ob = _mm(a * db[h], vr[h]) + xb[h] * _mm(q[h], sb[h])
        yr.append(_silu(gf[h]) * (of * _rms(of)) + _silu(gb[h]) * (ob * _rms(ob)))
        uf.append(_mm_tn(k[h], zf[h] * vr[h]))
        ub.append(_mm_tn(k[h], zb[h] * vr[h]))
    return ya, yr, uf, ub


def _fwd_dir_only(sf, q, k, vr, gf, df, xf):
    out = []
    for h in range(NH):
        a = _mm_nt(q[h], k[h])
        of = _mm(a * df[h], vr[h]) + xf[h] * _mm(q[h], sf[h])
        out.append(_silu(gf[h]) * (of * _rms(of)))
    return out


def _ctx_states(ctx0, ctx1, n1, csh, csc, wk, wv, zf, zb, ef, eb):
    hc0 = (ctx0 * _rms(ctx0) * n1) * (1.0 + csc) + csh
    hc1 = (ctx1 * _rms(ctx1) * n1) * (1.0 + csc) + csh
    scf, scb = [], []
    for h in range(NH):
        k0, k1 = _mm(hc0, wk[h]) * K_SCALE, _mm(hc1, wk[h]) * K_SCALE
        v0, v1 = _mm(hc0, wv[h]), _mm(hc1, wv[h])
        scf.append(ef[h] * _mm_tn(k0, zf[h] * v0) + _mm_tn(k1, zf[h] * v1))
        scb.append(eb[h] * _mm_tn(k1, zb[h] * v1) + _mm_tn(k0, zb[h] * v0))
    return scf, scb


def _allgather_small(v, name):
    r, n = v.shape

    def body(v_ref, out_ref, send_sems, recv_sems, local_sem):
        x, y, c = _pos()
        me = 4 * x + 2 * y + c
        mine = pltpu.make_async_copy(v_ref, out_ref.at[me], local_sem)
        mine.start()
        sent = []
        for k in range(1, NDEV):
            kx, ky, kc = (k >> 2) & 1, (k >> 1) & 1, k & 1
            peer = (x ^ kx, y ^ ky, c ^ kc)
            cp = pltpu.make_async_remote_copy(src_ref=v_ref, dst_ref=out_ref.at[me], send_sem=send_sems.at[k - 1],
                                              recv_sem=recv_sems.at[k - 1], device_id=peer, device_id_type=MESH)
            cp.start()
            sent.append(cp)
        for k in range(1, NDEV):
            kx, ky, kc = (k >> 2) & 1, (k >> 1) & 1, k & 1
            peer = (x ^ kx, y ^ ky, c ^ kc)
            src = 4 * (x ^ kx) + 2 * (y ^ ky) + (c ^ kc)
            pltpu.make_async_remote_copy(src_ref=v_ref, dst_ref=out_ref.at[src], send_sem=send_sems.at[k - 1],
                                         recv_sem=recv_sems.at[k - 1], device_id=peer, device_id_type=MESH).wait_recv()
        for cp in sent:
            cp.wait_send()
        mine.wait()

    return pl.pallas_call(
        body, name=name,
        out_shape=jax.ShapeDtypeStruct((NDEV, r, n), F32),
        in_specs=[pl.BlockSpec(memory_space=pltpu.VMEM)],
        out_specs=pl.BlockSpec(memory_space=pltpu.VMEM),
        scratch_shapes=[pltpu.SemaphoreType.DMA((NDEV - 1,)), pltpu.SemaphoreType.DMA((NDEV - 1,)),
                        pltpu.SemaphoreType.DMA],
        compiler_params=_params(16),
    )(v)


def _gather_weights(shards):
    nt = len(shards)
    shapes = [s.shape for s in shards]

    def body(*refs):
        srcs, outs, stages = refs[:nt], refs[nt:2 * nt], refs[2 * nt:3 * nt]
        ici_send, ici_recv, d2d_send, d2d_recv, local_sems = refs[3 * nt:]
        x, y, c = _pos()
        chip = 2 * x + y
        for t in range(nt):
            half = shapes[t][0] // 2
            stages[t][0] = srcs[t][0:half, :].astype(BF)
            stages[t][1] = srcs[t][half:2 * half, :].astype(BF)
        local = []
        for t in range(nt):
            cp = pltpu.make_async_copy(stages[t], outs[t].at[chip], local_sems.at[t])
            cp.start()
            local.append(cp)
        sent = []
        for k in range(1, NCHIP):
            kx, ky = (k >> 1) & 1, k & 1
            for t in range(nt):
                s = (k - 1) * nt + t
                cp = pltpu.make_async_remote_copy(
                    src_ref=stages[t].at[c], dst_ref=outs[t].at[chip, c], send_sem=ici_send.at[s],
                    recv_sem=ici_recv.at[s], device_id=(x ^ kx, y ^ ky, c), device_id_type=MESH)
                cp.start()
                sent.append(cp)
        for k in range(1, NCHIP):
            kx, ky = (k >> 1) & 1, k & 1
            src_chip = 2 * (x ^ kx) + (y ^ ky)
            for t in range(nt):
                s = (k - 1) * nt + t
                pltpu.make_async_remote_copy(
                    src_ref=stages[t].at[c], dst_ref=outs[t].at[src_chip, c], send_sem=ici_send.at[s],
                    recv_sem=ici_recv.at[s], device_id=(x ^ kx, y ^ ky, c), device_id_type=MESH).wait_recv()
                cp = pltpu.make_async_remote_copy(
                    src_ref=outs[t].at[src_chip, c], dst_ref=outs[t].at[src_chip, c], send_sem=d2d_send.at[s],
                    recv_sem=d2d_recv.at[s], device_id=(x, y, 1 - c), device_id_type=MESH)
                cp.start()
                sent.append(cp)
        for k in range(1, NCHIP):
            kx, ky = (k >> 1) & 1, k & 1
            src_chip = 2 * (x ^ kx) + (y ^ ky)
            for t in range(nt):
                s = (k - 1) * nt + t
                pltpu.make_async_remote_copy(
                    src_ref=stages[t].at[c], dst_ref=outs[t].at[src_chip, 1 - c], send_sem=d2d_send.at[s],
                    recv_sem=d2d_recv.at[s], device_id=(x, y, 1 - c), device_id_type=MESH).wait_recv()
        for cp in sent:
            cp.wait_send()
        for cp in local:
            cp.wait()

    n_rem = (NCHIP - 1) * nt
    out = pl.pallas_call(
        body, name="gather_weights",
        out_shape=[jax.ShapeDtypeStruct((NCHIP, 2, r // 2, cc), BF) for r, cc in shapes],
        in_specs=[pl.BlockSpec(memory_space=pltpu.VMEM)] * nt,
        out_specs=[pl.BlockSpec(memory_space=pl.ANY)] * nt,
        scratch_shapes=[pltpu.VMEM((2, r // 2, cc), BF) for r, cc in shapes]
        + [pltpu.SemaphoreType.DMA((n_rem,))] * 4 + [pltpu.SemaphoreType.DMA((nt,))],
        compiler_params=_params(48),
    )(*shards)
    return [o.reshape(NCHIP, r, cc) for o, (r, cc) in zip(out, shapes)]


def _rs_exchange_halves(grads):
    nt = len(grads)
    shapes = [g.shape for g in grads]

    def body(*refs):
        gs, outs = refs[:nt], refs[nt:2 * nt]
        send_sems, recv_sems = refs[2 * nt:]
        x, y, c = _pos()
        sib = (x, y, 1 - c)
        sent = []
        for t in range(nt):
            for j in range(NCHIP):
                s = t * NCHIP + j
                cp = pltpu.make_async_remote_copy(src_ref=gs[t].at[j, 1 - c], dst_ref=outs[t].at[j],
                                                  send_sem=send_sems.at[s], recv_sem=recv_sems.at[s],
                                                  device_id=sib, device_id_type=MESH)
                cp.start()
                sent.append(cp)
        for cp in sent:
            cp.wait_recv()
        for cp in sent:
            cp.wait_send()

    return pl.pallas_call(
        body, name="rs_exchange_halves",
        out_shape=[jax.ShapeDtypeStruct((NCHIP, s[2], s[3]), F32) for s in shapes],
        in_specs=[pl.BlockSpec(memory_space=pl.ANY)] * nt,
        out_specs=[pl.BlockSpec(memory_space=pl.ANY)] * nt,
        scratch_shapes=[pltpu.SemaphoreType.DMA((nt * NCHIP,))] * 2,
    )(*grads)


def _rs_send_chips(parts):
    nt = len(parts)
    shapes = [p.shape for p in parts]

    def body(*refs):
        ps, outs = refs[:nt], refs[nt:2 * nt]
        send_sems, recv_sems = refs[2 * nt:]
        x, y, c = _pos()
        sent = []
        for k in range(1, NCHIP):
            kx, ky = (k >> 1) & 1, k & 1
            dst_chip = 2 * (x ^ kx) + (y ^ ky)
            for t in range(nt):
                s = (k - 1) * nt + t
                cp = pltpu.make_async_remote_copy(src_ref=ps[t].at[dst_chip], dst_ref=outs[t].at[k - 1],
                                                  send_sem=send_sems.at[s], recv_sem=recv_sems.at[s],
                                                  device_id=(x ^ kx, y ^ ky, c), device_id_type=MESH)
                cp.start()
                sent.append(cp)
        for cp in sent:
            cp.wait_recv()
        for cp in sent:
            cp.wait_send()

    return pl.pallas_call(
        body, name="rs_send_chips",
        out_shape=[jax.ShapeDtypeStruct((NCHIP - 1, s[1], s[2]), BF) for s in shapes],
        in_specs=[pl.BlockSpec(memory_space=pl.ANY)] * nt,
        out_specs=[pl.BlockSpec(memory_space=pl.ANY)] * nt,
        scratch_shapes=[pltpu.SemaphoreType.DMA((nt * (NCHIP - 1),))] * 2,
    )(*parts)


def _rs_share_final(finals):
    nt = len(finals)
    shapes = [f.shape for f in finals]

    def body(*refs):
        fs, outs = refs[:nt], refs[nt:2 * nt]
        send_sems, recv_sems, local_sems = refs[2 * nt:]
        x, y, c = _pos()
        sent, local = [], []
        for t in range(nt):
            lc = pltpu.make_async_copy(fs[t], outs[t].at[c], local_sems.at[t])
            lc.start()
            local.append(lc)
            cp = pltpu.make_async_remote_copy(src_ref=fs[t], dst_ref=outs[t].at[c], send_sem=send_sems.at[t],
                                              recv_sem=recv_sems.at[t], device_id=(x, y, 1 - c), device_id_type=MESH)
            cp.start()
            sent.append(cp)
        for t in range(nt):
            pltpu.make_async_remote_copy(src_ref=fs[t], dst_ref=outs[t].at[1 - c], send_sem=send_sems.at[t],
                                         recv_sem=recv_sems.at[t], device_id=(x, y, 1 - c),
                                         device_id_type=MESH).wait_recv()
        for cp in sent:
            cp.wait_send()
        for lc in local:
            lc.wait()

    return pl.pallas_call(
        body, name="rs_share_final",
        out_shape=[jax.ShapeDtypeStruct((2,) + s, F32) for s in shapes],
        in_specs=[pl.BlockSpec(memory_space=pl.ANY)] * nt,
        out_specs=[pl.BlockSpec(memory_space=pl.ANY)] * nt,
        scratch_shapes=[pltpu.SemaphoreType.DMA((nt,))] * 3,
    )(*finals)


def _row_tile(h, cc=D):
    for t in (512, 384, 352, 256, 176, 128, 64, 32, 16):
        if h % t == 0 and t * cc * 4 <= (5 * VMEM_MB) // 4:
            return t
    return h


def _rs_add_halves(g, recv, cidx, name):
    _, _, h, cc = g.shape
    th = _row_tile(h, cc)

    def body(c_ref, g_ref, r_ref, of_ref, ob_ref):
        s = g_ref[...] + r_ref[...]
        of_ref[...] = s
        ob_ref[...] = s.astype(BF)

    return pl.pallas_call(
        body, name=name,
        grid_spec=pltpu.PrefetchScalarGridSpec(
            num_scalar_prefetch=1, grid=(NCHIP, h // th),
            in_specs=[pl.BlockSpec((None, None, th, cc), lambda j, i, c_ref: (j, c_ref[0], i, 0)),
                      pl.BlockSpec((None, th, cc), lambda j, i, c_ref: (j, i, 0))],
            out_specs=[pl.BlockSpec((None, th, cc), lambda j, i, c_ref: (j, i, 0)),
                       pl.BlockSpec((None, th, cc), lambda j, i, c_ref: (j, i, 0))]),
        out_shape=[jax.ShapeDtypeStruct((NCHIP, h, cc), F32), jax.ShapeDtypeStruct((NCHIP, h, cc), BF)],
        compiler_params=_params(48),
    )(cidx, g, recv)


def _rs_add_chips(own, recv, chipidx, name):
    _, h, cc = own.shape
    th = _row_tile(h, cc)

    def body(j_ref, o_ref, r_ref, out_ref):
        out_ref[...] = ((o_ref[...] + r_ref[0].astype(F32)) + r_ref[1].astype(F32)) + r_ref[2].astype(F32)

    return pl.pallas_call(
        body, name=name,
        grid_spec=pltpu.PrefetchScalarGridSpec(
            num_scalar_prefetch=1, grid=(h // th,),
            in_specs=[pl.BlockSpec((None, th, cc), lambda i, j_ref: (j_ref[0], i, 0)),
                      pl.BlockSpec((NCHIP - 1, th, cc), lambda i, j_ref: (0, i, 0))],
            out_specs=pl.BlockSpec((th, cc), lambda i, j_ref: (i, 0))),
        out_shape=jax.ShapeDtypeStruct((h, cc), F32),
        compiler_params=_params(48),
    )(chipidx, own, recv)


def _adamw_math(w, g, m, v):
    m2 = B1 * m + (1.0 - B1) * g
    v2 = B2 * v + (1.0 - B2) * (g * g)
    m_hat = m2 / (1.0 - B1 ** STEP)
    v_hat = v2 / (1.0 - B2 ** STEP)
    delta = -LR * (m_hat / (jnp.sqrt(v_hat) + AEPS) + WD * w)
    return delta, m2, v2


def _adamw(w, g, m, v, name):
    r, cc = w.shape
    tr = _row_tile(r, cc)

    def body(w_ref, g_ref, m_ref, v_ref, d_ref, mo_ref, vo_ref):
        d, m2, v2 = _adamw_math(w_ref[...], g_ref[...], m_ref[...], v_ref[...])
        d_ref[...] = d
        mo_ref[...] = m2
        vo_ref[...] = v2

    spec = pl.BlockSpec((tr, cc), lambda i: (i, 0))
    return pl.pallas_call(
        body, name=name, grid=(r // tr,), in_specs=[spec] * 4, out_specs=[spec] * 3,
        out_shape=[jax.ShapeDtypeStruct((r, cc), F32)] * 3,
        compiler_params=_params(48, ("parallel",)),
    )(w, g, m, v)


def _mod_forward(ct_pad, w_mod_s):
    def body(c_ref, w_ref, o_ref):
        o_ref[...] = jnp.dot(_silu(c_ref[...]), w_ref[...], precision=HI, preferred_element_type=F32)

    return pl.pallas_call(
        body, name="mod_forward",
        out_shape=jax.ShapeDtypeStruct((16, w_mod_s.shape[1]), F32),
        in_specs=[pl.BlockSpec(memory_space=pltpu.VMEM)] * 2,
        out_specs=pl.BlockSpec(memory_space=pltpu.VMEM),
        compiler_params=_params(32),
    )(ct_pad, w_mod_s)


def _decay_exponents():
    ri = lax.broadcasted_iota(jnp.int32, (CH, CH), 0).astype(F32)
    ci = lax.broadcasted_iota(jnp.int32, (CH, CH), 1).astype(F32)
    full = jnp.full((CH, CH), float(CH), F32)
    return [[ri - ci, ri + 1.0, (CH - 1.0) - ri, full], [ci - ri, CH - ri, ri, full]]


def _decay_mats(logit_full):
    def body(l_ref, o_ref):
        ex = _decay_exponents()
        for d in range(2):
            for h in range(NH):
                lv = l_ref[d * NH + h]
                lg = jnp.minimum(lv, 0.0) - jnp.log(1.0 + jnp.exp(-jnp.abs(lv)))
                for kind in range(4):
                    m = jnp.exp(lg * ex[d][kind])
                    if kind == 0:
                        m = jnp.where(ex[d][0] >= 0.0, jnp.exp(lg * jnp.maximum(ex[d][0], 0.0)), 0.0)
                    o_ref[d, kind, h] = m

    return pl.pallas_call(
        body, name="decay_mats",
        out_shape=jax.ShapeDtypeStruct((2, 4, NH, CH, CH), F32),
        in_specs=[pl.BlockSpec(memory_space=pltpu.VMEM)],
        out_specs=pl.BlockSpec(memory_space=pltpu.VMEM),
        compiler_params=_params(32),
    )(logit_full)


def _ctx_kv_weights(wi_ref):
    def cols(g):
        return wi_ref[g // WI_C, :, g % WI_C: g % WI_C + HD].astype(F32)

    wk = [cols(3 * AW + h * HD) for h in range(NH)]
    wv = [cols(4 * AW + h * HD) for h in range(NH)]
    return wk, wv


def _ctx_forward(ctx, vecs, wi, dm):
    def body(ctx_ref, v_ref, wi_ref, dm_ref, scf_ref, scb_ref):
        wk, wv = _ctx_kv_weights(wi_ref)
        mats = [[dm_ref[d, kind, h] for h in range(NH)] for d in range(2) for kind in (2, 3)]
        scf, scb = _ctx_states(ctx_ref[0:CH, :], ctx_ref[CH:2 * CH, :], v_ref[0:1, :], v_ref[1:2, :],
                               v_ref[2:3, :], wk, wv, mats[0], mats[2], mats[1], mats[3])
        for h in range(NH):
            scf_ref[h] = scf[h]
            scb_ref[h] = scb[h]

    return pl.pallas_call(
        body, name="ctx_forward",
        out_shape=[jax.ShapeDtypeStruct((NH, HD, HD), F32)] * 2,
        in_specs=[pl.BlockSpec(memory_space=pltpu.VMEM)] * 4,
        out_specs=[pl.BlockSpec(memory_space=pltpu.VMEM)] * 2,
        compiler_params=_params(48),
    )(ctx, vecs, wi, dm)


def _ctx_backward(ctx, vecs, wi, dm, dscf, dscb):
    def body(ctx_ref, v_ref, wi_ref, dm_ref, gf_ref, gb_ref, gw_ref, gv_ref, gdm_ref):
        wk, wv = _ctx_kv_weights(wi_ref)
        mats = [[dm_ref[d, kind, h] for h in range(NH)] for d in range(2) for kind in (2, 3)]
        ctx0, ctx1 = ctx_ref[0:CH, :], ctx_ref[CH:2 * CH, :]

        def fn(n1, csh, csc, wk_, wv_, zf, zb, ef, eb):
            return _ctx_states(ctx0, ctx1, n1, csh, csc, wk_, wv_, zf, zb, ef, eb)

        _, vjp = jax.vjp(fn, v_ref[0:1, :], v_ref[1:2, :], v_ref[2:3, :], wk, wv,
                         mats[0], mats[2], mats[1], mats[3])
        cot = ([gf_ref[h] for h in range(NH)], [gb_ref[h] for h in range(NH)])
        dn1, dcsh, dcsc, dwk, dwv, dzf, dzb, def_, deb = vjp(cot)
        for h in range(NH):
            gw_ref[:, h * HD:(h + 1) * HD] = dwk[h]
            gw_ref[:, AW + h * HD:AW + (h + 1) * HD] = dwv[h]
        gv_ref[...] = jnp.zeros_like(gv_ref)
        gv_ref[0:1, :] = dn1
        gv_ref[1:2, :] = dcsh
        gv_ref[2:3, :] = dcsc
        for h in range(NH):
            gdm_ref[0, 0, h] = dzf[h]
            gdm_ref[0, 1, h] = def_[h]
            gdm_ref[1, 0, h] = dzb[h]
            gdm_ref[1, 1, h] = deb[h]

    return pl.pallas_call(
        body, name="ctx_backward",
        out_shape=[jax.ShapeDtypeStruct((D, 2 * AW), F32), jax.ShapeDtypeStruct((8, D), F32),
                   jax.ShapeDtypeStruct((2, 2, NH, CH, CH), F32)],
        in_specs=[pl.BlockSpec(memory_space=pltpu.VMEM)] * 6,
        out_specs=[pl.BlockSpec(memory_space=pltpu.VMEM)] * 3,
        compiler_params=_params(56),
    )(ctx, vecs, wi, dm, dscf, dscb)


def _in_proj(x, vecs, wi):
    ln = x.shape[0]
    t = min(512, ln)

    def body(x_ref, v_ref, wi_ref, z_ref, hx_ref):
        xv = x_ref[...]
        hx = (xv * _rms(xv) * v_ref[0:1, :]) * (1.0 + v_ref[2:3, :]) + v_ref[1:2, :]
        hb = hx.astype(BF)
        hx_ref[...] = hb
        for j in range(NCHIP):
            z_ref[:, j * WI_C:(j + 1) * WI_C] = _dot(hb, wi_ref[j], NN).astype(BF)

    return pl.pallas_call(
        body, name="in_proj", grid=(ln // t,),
        in_specs=[pl.BlockSpec((t, D), lambda i: (i, 0)), _const((8, D)), _const((NCHIP, D, WI_C))],
        out_specs=[pl.BlockSpec((t, IN_COLS), lambda i: (i, 0)), pl.BlockSpec((t, D), lambda i: (i, 0))],
        out_shape=[jax.ShapeDtypeStruct((ln, IN_COLS), BF), jax.ShapeDtypeStruct((ln, D), BF)],
        compiler_params=_params(56, ("parallel",)),
    )(x, vecs, wi)


def _in_proj_bwd(dz, x, dx1, vecs, wi):
    ln = x.shape[0]
    t = min(512, ln)

    def body(dz_ref, x_ref, dx1_ref, v_ref, wi_ref, gx_ref, acc_ref):
        @pl.when(pl.program_id(0) == 0)
        def _():
            acc_ref[...] = jnp.zeros_like(acc_ref)

        dhx = jnp.zeros((t, D), F32)
        for j in range(NCHIP):
            dhx = dhx + _dot(dz_ref[:, j * WI_C:(j + 1) * WI_C], wi_ref[j], NT)
        xv = x_ref[...]
        r = _rms(xv)
        xn = xv * r
        n1, sc = v_ref[0:1, :], v_ref[2:3, :]
        acc_ref[0:1, :] += jnp.sum(dhx * xn * (1.0 + sc), axis=0, keepdims=True)
        acc_ref[1:2, :] += jnp.sum(dhx, axis=0, keepdims=True)
        acc_ref[2:3, :] += jnp.sum(dhx * xn * n1, axis=0, keepdims=True)
        g = dhx * n1 * (1.0 + sc)
        gx_ref[...] = dx1_ref[...] + r * (g - xn * jnp.mean(g * xn, axis=-1, keepdims=True))

    return pl.pallas_call(
        body, name="in_proj_bwd", grid=(ln // t,),
        in_specs=[pl.BlockSpec((t, IN_COLS), lambda i: (i, 0)), pl.BlockSpec((t, D), lambda i: (i, 0)),
                  pl.BlockSpec((t, D), lambda i: (i, 0)), _const((8, D)), _const((NCHIP, D, WI_C))],
        out_specs=[pl.BlockSpec((t, D), lambda i: (i, 0)), pl.BlockSpec((8, D), lambda i: (0, 0))],
        out_shape=[jax.ShapeDtypeStruct((ln, D), F32), jax.ShapeDtypeStruct((8, D), F32)],
        compiler_params=_params(56, ("arbitrary",)),
    )(dz, x, dx1, vecs, wi)


def _post_mixer(x, ycat, tgt, vecs, wo, wg, wu, wd):
    ln = x.shape[0]
    t = min(256, ln)

    def body(x_ref, y_ref, t_ref, v_ref, wo_ref, wg_ref, wu_ref, wd_ref,
             dx1_ref, dyc_ref, h2_ref, dy_ref, df_ref, act_ref, da_ref, db_ref, acc_ref, a_st, b_st):
        @pl.when(pl.program_id(0) == 0)
        def _():
            acc_ref[...] = jnp.zeros_like(acc_ref)

        g1, n2, sh2, sc2 = v_ref[0:1, :], v_ref[1:2, :], v_ref[2:3, :], v_ref[3:4, :]
        g2, nf = v_ref[4:5, :], v_ref[5:6, :]
        y = _dot(y_ref[...], wo_ref[...], NN)
        x1 = x_ref[...] + g1 * y
        r2 = _rms(x1)
        xn2 = x1 * r2
        t2 = xn2 * n2
        h2b = (t2 * (1.0 + sc2) + sh2).astype(BF)
        h2_ref[...] = h2b
        f = jnp.zeros((t, D), F32)
        for j in range(NCHIP):
            a = _dot(h2b, wg_ref[j], NN)
            b = _dot(h2b, wu_ref[j], NN)
            a_st[j] = a.astype(BF)
            b_st[j] = b.astype(BF)
            act = (_silu(a) * b).astype(BF)
            act_ref[j] = act
            f = f + _dot(act, wd_ref[j], NN)
        x2 = x1 + g2 * f
        r3 = _rms(x2)
        xn3 = x2 * r3
        e = xn3 * nf - t_ref[...]
        acc_ref[6:7, :] += jnp.sum(e * e, axis=0, keepdims=True) * (0.5 / D)
        dout = e * (1.0 / D)
        acc_ref[5:6, :] += jnp.sum(dout * xn3, axis=0, keepdims=True)
        gg = dout * nf
        dx2 = r3 * (gg - xn3 * jnp.mean(gg * xn3, axis=-1, keepdims=True))
        acc_ref[4:5, :] += jnp.sum(dx2 * f, axis=0, keepdims=True)
        dfb = (g2 * dx2).astype(BF)
        df_ref[...] = dfb
        dh2 = jnp.zeros((t, D), F32)
        for j in range(NCHIP):
            dact = _dot(dfb, wd_ref[j], NT)
            a = a_st[j].astype(F32)
            b = b_st[j].astype(F32)
            s = jax.nn.sigmoid(a)
            da = (dact * b * (s * (1.0 + a * (1.0 - s)))).astype(BF)
            db = (dact * (a * s)).astype(BF)
            da_ref[j] = da
            db_ref[j] = db
            dh2 = dh2 + _dot(da, wg_ref[j], NT) + _dot(db, wu_ref[j], NT)
        acc_ref[2:3, :] += jnp.sum(dh2, axis=0, keepdims=True)
        acc_ref[3:4, :] += jnp.sum(dh2 * t2, axis=0, keepdims=True)
        acc_ref[1:2, :] += jnp.sum(dh2 * xn2 * (1.0 + sc2), axis=0, keepdims=True)
        gx = dh2 * n2 * (1.0 + sc2)
        dx1 = dx2 + r2 * (gx - xn2 * jnp.mean(gx * xn2, axis=-1, keepdims=True))
        dx1_ref[...] = dx1
        acc_ref[0:1, :] += jnp.sum(dx1 * y, axis=0, keepdims=True)
        dyb = (g1 * dx1).astype(BF)
        dy_ref[...] = dyb
        dyc_ref[...] = _dot(dyb, wo_ref[...], NT).astype(BF)

    tok = pl.BlockSpec((t, D), lambda i: (i, 0))
    ffb = pl.BlockSpec((NCHIP, t, FF_C), lambda i: (0, i, 0))
    return pl.pallas_call(
        body, name="post_mixer", grid=(ln // t,),
        in_specs=[tok, tok, tok, _const((8, D)), _const((D, D)), _const((NCHIP, D, FF_C)),
                  _const((NCHIP, D, FF_C)), _const((NCHIP, FF_C, D))],
        out_specs=[tok, tok, tok, tok, tok, ffb, ffb, ffb, pl.BlockSpec((16, D), lambda i: (0, 0))],
        out_shape=[jax.ShapeDtypeStruct((ln, D), F32)] + [jax.ShapeDtypeStruct((ln, D), BF)] * 4
        + [jax.ShapeDtypeStruct((NCHIP, ln, FF_C), BF)] * 3 + [jax.ShapeDtypeStruct((16, D), F32)],
        scratch_shapes=[pltpu.VMEM((NCHIP, t, FF_C), BF)] * 2,
        compiler_params=_params(60, ("arbitrary",)),
    )(x, ycat, tgt, vecs, wo, wg, wu, wd)


def _tn_matmul(xa, dy, name, nb, k1, n, x_batched, dy_mode, tk):
    ln = xa.shape[-2]
    tt = min(512, ln)
    nk = k1 // tk

    def body(x_ref, dy_ref, o_ref):
        @pl.when(pl.program_id(2) == 0)
        def _():
            o_ref[...] = jnp.zeros_like(o_ref)

        o_ref[...] += _dot(x_ref[...], dy_ref[...], TN)

    if x_batched:
        x_spec = pl.BlockSpec((None, tt, tk), lambda b, k, t: (b, t, k))
    elif dy_mode == "shared":
        x_spec = pl.BlockSpec((tt, tk), lambda b, k, t: (t, b * nk + k))
    else:
        x_spec = pl.BlockSpec((tt, tk), lambda b, k, t: (t, k))
    if dy_mode == "batched":
        dy_spec = pl.BlockSpec((None, tt, n), lambda b, k, t: (b, t, 0))
    elif dy_mode == "cols":
        dy_spec = pl.BlockSpec((tt, n), lambda b, k, t: (t, b))
    else:
        dy_spec = pl.BlockSpec((tt, n), lambda b, k, t: (t, 0))
    return pl.pallas_call(
        body, name=name, grid=(nb, nk, ln // tt),
        in_specs=[x_spec, dy_spec],
        out_specs=pl.BlockSpec((None, tk, n), lambda b, k, t: (b, k, 0)),
        out_shape=jax.ShapeDtypeStruct((nb, k1, n), F32),
        compiler_params=_params(56, ("parallel", "parallel", "arbitrary")),
    )(xa, dy)


def _add_ctx_cols(gwi, gwkv):
    first = 1536 // HD
    per = WI_C // HD

    def body(g_ref, a_ref, o_ref):
        o_ref[...] = g_ref[...] + a_ref[...]

    spec = pl.BlockSpec((None, D, HD), lambda i: ((first + i) // per, 0, (first + i) % per))
    return pl.pallas_call(
        body, name="add_ctx_cols", grid=(2 * AW // HD,),
        in_specs=[spec, pl.BlockSpec((D, HD), lambda i: (0, i))],
        out_specs=spec,
        out_shape=jax.ShapeDtypeStruct(gwi.shape, F32),
        input_output_aliases={0: 0},
        compiler_params=_params(32, ("arbitrary",)),
    )(gwi, gwkv)


def _mixer_fwd(z, cos_t, sin_t, dm, sgw, gain, bfull, scf, scb):
    ln = z.shape[0]
    nc = ln // CH

    def rev(p, n):
        return p * n + (1 - p) * (nc - 1 - n)

    def col(j, both):
        if both:
            return pl.BlockSpec((CH, AW), lambda p, n: (rev(p, n), j))
        return pl.BlockSpec((CH, AW), lambda p, n: (p * n, j))

    def body(u_ref, v_ref, q_ref, k_ref, vr_ref, gf_ref, gb_ref, cos_ref, sin_ref, dm_ref, sgw_ref, gain_ref,
             bfull_ref, scf_ref, scb_ref, y_ref, sf_ref, sb_ref, sb_all, st):
        p, n = pl.program_id(0), pl.program_id(1)
        cos, sin = cos_ref[...], sin_ref[...]
        k = [_rope(t, cos, sin) * K_SCALE for t in _heads(k_ref)]
        vr = _heads(vr_ref)

        @pl.when(p == 0)
        def _():
            @pl.when(n == 0)
            def _():
                st[...] = scb_ref[...]

            m = nc - 1 - n
            for h in range(NH):
                sb_all[m, h] = st[h].astype(BF)
                st[h] = dm_ref[1, 3, h] * st[h] + _mm_tn(k[h], dm_ref[1, 2, h] * vr[h])

        @pl.when(p == 1)
        def _():
            @pl.when(n == 0)
            def _():
                st[...] = scf_ref[...]

            q = [_rope(t, cos, sin) for t in _heads(q_ref)]
            sf = [st[h] for h in range(NH)]
            sb = [sb_all[n, h].astype(F32) for h in range(NH)]
            mats = [[dm_ref[d, kind, h] for h in range(NH)] for d in range(2) for kind in range(3)]
            ya, yr, uf, _ = _chunk_fwd(
                _heads(u_ref), _heads(v_ref), q, k, vr, _heads(gf_ref), _heads(gb_ref), sf, sb,
                mats[0], mats[1], mats[2], mats[3], mats[4], mats[5],
                [sgw_ref[g] for g in range(NH)], [gain_ref[:, g * HD:(g + 1) * HD] for g in range(NH)],
                [bfull_ref[g] for g in range(NH)])
            for h in range(NH):
                y_ref[:, h * HD:(h + 1) * HD] = ya[h].astype(BF)
                y_ref[:, AW + h * HD:AW + (h + 1) * HD] = yr[h].astype(BF)
                sf_ref[0, h] = sf[h].astype(BF)
                sb_ref[0, h] = sb_all[n, h]
                st[h] = dm_ref[0, 3, h] * st[h] + uf[h]

    tab = pl.BlockSpec((CH, HD), lambda p, n: (rev(p, n), 0))
    st_spec = pl.BlockSpec((1, NH, HD, HD), lambda p, n: (p * n, 0, 0, 0))
    return pl.pallas_call(
        body, name="mixer_fwd", grid=(2, nc),
        in_specs=[col(0, False), col(1, False), col(2, False), col(3, True), col(4, True), col(5, False),
                  col(6, False), tab, tab, _const((2, 4, NH, CH, CH)), _const((NH, CH, CH)), _const((1, AW)),
                  _const((NH, CH, CH)), _const((NH, HD, HD)), _const((NH, HD, HD))],
        out_specs=[pl.BlockSpec((CH, D), lambda p, n: (p * n, 0)), st_spec, st_spec],
        out_shape=[jax.ShapeDtypeStruct((ln, D), BF), jax.ShapeDtypeStruct((nc, NH, HD, HD), BF),
                   jax.ShapeDtypeStruct((nc, NH, HD, HD), BF)],
        scratch_shapes=[pltpu.VMEM((nc, NH, HD, HD), BF), pltpu.VMEM((NH, HD, HD), F32)],
        compiler_params=_params(56, ("arbitrary", "arbitrary")),
    )(z, z, z, z, z, z, z, cos_t, sin_t, dm, sgw, gain, bfull, scf, scb)


def _mixer_bwd(z, dycat, cos_t, sin_t, dm, sgw, gain, bfull, sf_all, sb_all):
    ln = z.shape[0]
    nc = ln // CH

    def rev(p, n):
        return p * n + (1 - p) * (nc - 1 - n)

    def col(j, both):
        if both:
            return pl.BlockSpec((CH, AW), lambda p, n: (rev(p, n), j))
        return pl.BlockSpec((CH, AW), lambda p, n: (p * n, j))

    def body(u_ref, v_ref, q_ref, k_ref, vr_ref, gf_ref, gb_ref, dya_ref, dyr_ref, cos_ref, sin_ref, dm_ref,
             sgw_ref, gain_ref, bfull_ref, sf_ref, sb_ref,
             dz_ref, ddm_ref, dsgw_ref, dgain_ref, dbf_ref, dscf_ref, dscb_ref, gf_all, run):
        p, n = pl.program_id(0), pl.program_id(1)
        cos, sin = cos_ref[...], sin_ref[...]
        q = [_rope(t, cos, sin) for t in _heads(q_ref)]
        k = [_rope(t, cos, sin) * K_SCALE for t in _heads(k_ref)]
        vr = _heads(vr_ref)
        gf = _heads(gf_ref)
        dyr = _heads(dyr_ref)
        sf = [sf_ref[0, h].astype(F32) for h in range(NH)]
        mats = [[dm_ref[d, kind, h] for h in range(NH)] for d in range(2) for kind in range(3)]

        @pl.when(p == 0)
        def _():
            @pl.when(n == 0)
            def _():
                run[...] = jnp.zeros_like(run)
                ddm_ref[...] = jnp.zeros_like(ddm_ref)
                dsgw_ref[...] = jnp.zeros_like(dsgw_ref)
                dgain_ref[...] = jnp.zeros_like(dgain_ref)
                dbf_ref[...] = jnp.zeros_like(dbf_ref)

            m = nc - 1 - n
            _, vjp = jax.vjp(lambda s: _fwd_dir_only(s, q, k, vr, gf, mats[0], mats[1]), sf)
            (dsf,) = vjp(dyr)
            for h in range(NH):
                gf_all[m, h] = run[h].astype(BF)
                run[h] = dsf[h] + dm_ref[0, 3, h] * run[h]

            @pl.when(n == nc - 1)
            def _():
                dscf_ref[...] = run[...]

        @pl.when(p == 1)
        def _():
            @pl.when(n == 0)
            def _():
                run[...] = jnp.zeros_like(run)

            sb = [sb_ref[0, h].astype(F32) for h in range(NH)]
            g_f = [gf_all[n, h].astype(F32) for h in range(NH)]
            g_b = [run[h] for h in range(NH)]
            args = (_heads(u_ref), _heads(v_ref), q, k, vr, gf, _heads(gb_ref), sb,
                    mats[0], mats[1], mats[2], mats[3], mats[4], mats[5],
                    [sgw_ref[g] for g in range(NH)], [gain_ref[:, g * HD:(g + 1) * HD] for g in range(NH)],
                    [bfull_ref[g] for g in range(NH)])

            def fn(u_, v_, q_, k_, vr_, gf_, gb_, sb_, df, xf, zf, db, xb, zb, sgw_, gain_, bfull_):
                return _chunk_fwd(u_, v_, q_, k_, vr_, gf_, gb_, sf, sb_, df, xf, zf, db, xb, zb, sgw_, gain_,
                                  bfull_)

            _, vjp = jax.vjp(fn, *args)
            (du, dv, dq, dk, dvr, dgf, dgb, dsb, ddf, dxf, dzf, ddb, dxb, dzb, dsgw, dgain, dbf) = vjp(
                (_heads(dya_ref), dyr, g_f, g_b))
            for h in range(NH):
                s = slice(h * HD, (h + 1) * HD)
                dz_ref[:, h * HD:(h + 1) * HD] = du[h].astype(BF)
                dz_ref[:, AW + h * HD:AW + (h + 1) * HD] = dv[h].astype(BF)
                dz_ref[:, 2 * AW + h * HD:2 * AW + (h + 1) * HD] = _rope_bwd(dq[h], cos, sin).astype(BF)
                dz_ref[:, 3 * AW + h * HD:3 * AW + (h + 1) * HD] = _rope_bwd(dk[h] * K_SCALE, cos, sin).astype(BF)
                dz_ref[:, 4 * AW + h * HD:4 * AW + (h + 1) * HD] = dvr[h].astype(BF)
                dz_ref[:, 5 * AW + h * HD:5 * AW + (h + 1) * HD] = dgf[h].astype(BF)
                dz_ref[:, 6 * AW + h * HD:6 * AW + (h + 1) * HD] = dgb[h].astype(BF)
                ddm_ref[0, 0, h] += ddf[h]
                ddm_ref[0, 1, h] += dxf[h]
                ddm_ref[0, 2, h] += dzf[h]
                ddm_ref[0, 3, h] += sf[h] * g_f[h]
                ddm_ref[1, 0, h] += ddb[h]
                ddm_ref[1, 1, h] += dxb[h]
                ddm_ref[1, 2, h] += dzb[h]
                ddm_ref[1, 3, h] += sb[h] * g_b[h]
                dsgw_ref[h] += dsgw[h]
                dgain_ref[:, s] += dgain[h]
                dbf_ref[h] += dbf[h]
                run[h] = dsb[h] + dm_ref[1, 3, h] * run[h]

            @pl.when(n == nc - 1)
            def _():
                dscb_ref[...] = run[...]

    tab = pl.BlockSpec((CH, HD), lambda p, n: (rev(p, n), 0))
    tile4 = jax.ShapeDtypeStruct((NH, CH, CH), F32)
    return pl.pallas_call(
        body, name="mixer_bwd", grid=(2, nc),
        in_specs=[col(0, False), col(1, False), col(2, True), col(3, True), col(4, True), col(5, True),
                  col(6, False),
                  pl.BlockSpec((CH, AW), lambda p, n: (p * n, 0)), pl.BlockSpec((CH, AW), lambda p, n: (rev(p, n), 1)),
                  tab, tab, _const((2, 4, NH, CH, CH)), _const((NH, CH, CH)), _const((1, AW)),
                  _const((NH, CH, CH)),
                  pl.BlockSpec((1, NH, HD, HD), lambda p, n: (rev(p, n), 0, 0, 0)),
                  pl.BlockSpec((1, NH, HD, HD), lambda p, n: (p * n, 0, 0, 0))],
        out_specs=[pl.BlockSpec((CH, IN_COLS), lambda p, n: (p * n, 0)),
                   pl.BlockSpec((2, 4, NH, CH, CH), lambda p, n: (0, 0, 0, 0, 0)),
                   pl.BlockSpec((NH, CH, CH), lambda p, n: (0, 0, 0)),
                   pl.BlockSpec((1, AW), lambda p, n: (0, 0)),
                   pl.BlockSpec((NH, CH, CH), lambda p, n: (0, 0, 0)),
                   pl.BlockSpec((NH, HD, HD), lambda p, n: (0, 0, 0)),
                   pl.BlockSpec((NH, HD, HD), lambda p, n: (0, 0, 0))],
        out_shape=[jax.ShapeDtypeStruct((ln, IN_COLS), BF), jax.ShapeDtypeStruct((2, 4, NH, CH, CH), F32),
                   tile4, jax.ShapeDtypeStruct((1, AW), F32), tile4, tile4, tile4],
        scratch_shapes=[pltpu.VMEM((nc, NH, HD, HD), BF), pltpu.VMEM((NH, HD, HD), F32)],
        compiler_params=_params(56, ("arbitrary", "arbitrary")),
    )(z, z, z, z, z, z, z, dycat, dycat, cos_t, sin_t, dm, sgw, gain, bfull, sf_all, sb_all)


def _small_reduce(ddm, ddm_ctx, dm, dbf):
    def body(ddm_ref, dctx_ref, dm_ref, dbf_ref, lg_ref, sgb_ref):
        ex = _decay_exponents()
        ones = jnp.ones((8, CH), F32)
        for d in range(2):
            for h in range(NH):
                tot = jnp.zeros((CH, CH), F32)
                for kind in range(4):
                    g = ddm_ref[d, kind, h]
                    if kind >= 2:
                        g = g + dctx_ref[d, kind - 2, h]
                    tot = tot + g * dm_ref[d, kind, h] * ex[d][kind]
                lg_ref[d * NH + h: d * NH + h + 1, :] = jnp.sum(tot, axis=0, keepdims=True)
        for g in range(NH):
            r = lax.dot_general(ones, dbf_ref[g], (NT, ((), ())), precision=HI, preferred_element_type=F32)
            sgb_ref[g:g + 1, :] = r[0:1, :]

    return pl.pallas_call(
        body, name="small_reduce",
        out_shape=[jax.ShapeDtypeStruct((8, CH), F32), jax.ShapeDtypeStruct((NH, CH), F32)],
        in_specs=[pl.BlockSpec(memory_space=pltpu.VMEM)] * 4,
        out_specs=[pl.BlockSpec(memory_space=pltpu.VMEM)] * 2,
        compiler_params=_params(32),
    )(ddm, ddm_ctx, dm, dbf)


def _mod_backward(ct_pad_t, cctx_col, dmod_pad, dcmod_cols, w_mod_s):
    def body(ct_ref, cc_ref, dm_ref, dc_ref, w_ref, gw_ref, part_ref):
        dcm = dc_ref[0:1, :]
        for d in range(1, NDEV):
            dcm = dcm + dc_ref[d:d + 1, :]
        gw_ref[...] = (jnp.dot(_silu(ct_ref[...]), dm_ref[...], precision=HI, preferred_element_type=F32)
                       + _silu(cc_ref[...]) * dcm)
        part_ref[...] = lax.dot_general(jnp.broadcast_to(dcm, (8, dcm.shape[1])), w_ref[...], (NT, ((), ())),
                                        precision=HI, preferred_element_type=F32)

    return pl.pallas_call(
        body, name="mod_backward",
        out_shape=[jax.ShapeDtypeStruct(w_mod_s.shape, F32), jax.ShapeDtypeStruct((8, D), F32)],
        in_specs=[pl.BlockSpec(memory_space=pltpu.VMEM)] * 5,
        out_specs=[pl.BlockSpec(memory_space=pltpu.VMEM)] * 2,
        compiler_params=_params(48),
    )(ct_pad_t, cctx_col, dmod_pad, dcmod_cols, w_mod_s)


def _cctx_update(parts, c_ctx, m, v):
    def body(p_ref, c_ref, m_ref, v_ref, g_ref, d_ref, mo_ref, vo_ref):
        tot = ((p_ref[0] + p_ref[2]) + p_ref[4]) + p_ref[6]
        cv = c_ref[...]
        s = jax.nn.sigmoid(cv)
        g = tot * (s * (1.0 + cv * (1.0 - s)))
        g_ref[...] = g
        d_ref[...], mo_ref[...], vo_ref[...] = _adamw_math(cv, g, m_ref[...], v_ref[...])

    return pl.pallas_call(
        body, name="cctx_update",
        out_shape=[jax.ShapeDtypeStruct((1, D), F32)] * 4,
        in_specs=[pl.BlockSpec(memory_space=pltpu.VMEM)] * 4,
        out_specs=[pl.BlockSpec(memory_space=pltpu.VMEM)] * 4,
        compiler_params=_params(16),
    )(parts, c_ctx, m, v)


def _small_update(gathered, wp, mp, vp):
    def body(g_ref, w_ref, m_ref, v_ref, go_ref, d_ref, mo_ref, vo_ref, loss_ref):
        tot = g_ref[0]
        for d in range(1, NDEV):
            tot = tot + g_ref[d]
        go_ref[Q_BMOD:Q_N1, :] = tot[P_DMOD:P_N1, :] + tot[P_DCMOD:P_DMOD, :]
        go_ref[Q_N1:Q_LG, :] = tot[P_N1:P_LG, :]
        lg = jnp.sum(tot[P_LG:P_N2, :], axis=1, keepdims=True)
        go_ref[Q_LG:Q_N2, :] = lg * jax.nn.sigmoid(-w_ref[Q_LG:Q_N2, :])
        go_ref[Q_N2:Q_ROWS, :] = tot[P_N2:P_LOSS, :]
        d_ref[...], mo_ref[...], vo_ref[...] = _adamw_math(w_ref[...], go_ref[...], m_ref[...], v_ref[...])
        ls = jnp.sum(jnp.sum(tot[P_LOSS:P_ROWS, :], axis=1, keepdims=True), axis=0, keepdims=True)
        loss_ref[...] = jnp.broadcast_to(ls, (8, CH))

    return pl.pallas_call(
        body, name="small_update",
        out_shape=[jax.ShapeDtypeStruct((Q_ROWS, CH), F32)] * 4 + [jax.ShapeDtypeStruct((8, CH), F32)],
        in_specs=[pl.BlockSpec(memory_space=pltpu.VMEM)] * 4,
        out_specs=[pl.BlockSpec(memory_space=pltpu.VMEM)] * 5,
        compiler_params=_params(32),
    )(gathered, wp, mp, vp)


def _rows(a):
    return a.reshape(-1, CH)


def _pack_small(b_mod, norm1, sg_gain, sg_w, sg_b, lf, lb, norm2, norm_f):
    lg = jnp.broadcast_to(jnp.concatenate([lf.reshape(NH), lb.reshape(NH)])[:, None], (2 * NH, CH))
    return jnp.concatenate([_rows(b_mod), _rows(norm1), _rows(sg_gain), _rows(sg_w), _rows(sg_b), lg,
                            _rows(norm2), _rows(norm_f)], axis=0)


def _unpack_small(p):
    return (p[Q_BMOD:Q_N1].reshape(1, 6 * D), p[Q_N1:Q_GAIN].reshape(1, D), p[Q_GAIN:Q_SGW].reshape(1, AW),
            p[Q_SGW:Q_SGB].reshape(1, NH, CH, CH), p[Q_SGB:Q_LG].reshape(1, NH, CH),
            p[Q_LG:Q_LG + NH, 0].reshape(1, NH), p[Q_LG + NH:Q_N2, 0].reshape(1, NH),
            p[Q_N2:Q_NF].reshape(1, D), p[Q_NF:Q_ROWS].reshape(D))


def _rope_tables(ln):
    pos = jnp.arange(ln, dtype=jnp.int32)
    rows = (pos // GRID_W).astype(F32)
    cols = (pos % GRID_W).astype(F32)
    n_freq = HD // 4
    inv = ROPE_BASE ** (-jnp.arange(n_freq, dtype=F32) / n_freq)
    ar = rows[:, None] * inv[None, :]
    ac = cols[:, None] * inv[None, :]
    cos_t = jnp.concatenate([jnp.cos(ar), jnp.cos(ar), jnp.cos(ac), jnp.cos(ac)], axis=1)
    sin_t = jnp.concatenate([-jnp.sin(ar), jnp.sin(ar), -jnp.sin(ac), jnp.sin(ac)], axis=1)
    return cos_t, sin_t


def kernel(x, c, ctx, c_ctx, w_mod, b_mod, norm1, w_in, sg_gain, sg_w, sg_b, ret_logit_f, ret_logit_b, w_out, norm2, w_gate, w_up, w_down, norm_f, loss_target, m_c_ctx, m_w_mod, m_b_mod, m_norm1, m_w_in, m_sg_gain, m_sg_w, m_sg_b, m_ret_logit_f, m_ret_logit_b, m_w_out, m_norm2, m_w_gate, m_w_up, m_w_down, m_norm_f, v_c_ctx, v_w_mod, v_b_mod, v_norm1, v_w_in, v_sg_gain, v_sg_w, v_sg_b, v_ret_logit_f, v_ret_logit_b, v_w_out, v_norm2, v_w_gate, v_w_up, v_w_down, v_norm_f):
    ln = x.shape[1]
    xi, yi, ci = _pos()
    chip = 2 * xi + yi
    me = 4 * xi + 2 * yi + ci
    x2d = x.reshape(ln, D)
    tgt = loss_target.reshape(ln, D)
    mod_c = w_mod.shape[2]

    wi, wo, wg, wu, wd = _gather_weights([w_in[0], w_out[0], w_gate[0], w_up[0], w_down[0]])
    wo = wo.reshape(D, D)

    c_all = _allgather_small(c, "gather_c").reshape(NDEV, D)
    ct_pad = jnp.concatenate([c_all, c_ctx.reshape(1, D), jnp.zeros((7, D), F32)], axis=0)
    prod = _mod_forward(ct_pad, w_mod[0])
    prod_all = _allgather_small(prod, "gather_mod")
    prod_chips = prod_all[0::2]
    mod_rows = jnp.transpose(prod_chips, (1, 0, 2)).reshape(16, NCHIP * mod_c) + b_mod
    mod = lax.dynamic_slice_in_dim(mod_rows, me, 1, axis=0)
    cmod = mod_rows[8:9]
    sh1, sc1, g1, sh2, sc2, g2 = [mod[:, i * D:(i + 1) * D] for i in range(6)]
    csh1, csc1 = cmod[:, 0:D], cmod[:, D:2 * D]
    zrow = jnp.zeros((1, D), F32)
    vec_in = jnp.concatenate([norm1, sh1, sc1] + [zrow] * 5, axis=0)
    vec_ctx = jnp.concatenate([norm1, csh1, csc1] + [zrow] * 5, axis=0)
    vec_post = jnp.concatenate([g1, norm2, sh2, sc2, g2, norm_f.reshape(1, D), zrow, zrow], axis=0)

    logits = jnp.concatenate([ret_logit_f.reshape(NH), ret_logit_b.reshape(NH)])
    dm = _decay_mats(jnp.broadcast_to(logits[:, None, None], (2 * NH, CH, CH)))
    ctx2d = ctx.reshape(ctx.shape[1], D)
    scf, scb = _ctx_forward(ctx2d, vec_ctx, wi, dm)

    cos_t, sin_t = _rope_tables(ln)
    z, hx = _in_proj(x2d, vec_in, wi)
    bfull = jnp.broadcast_to(sg_b[0][:, :, None], (NH, CH, CH))
    ycat, sf_all, sb_all = _mixer_fwd(z, cos_t, sin_t, dm, sg_w[0], sg_gain, bfull, scf, scb)

    dx1, dycat, h2, dy, df, act, da, db, acc_post = _post_mixer(x2d, ycat, tgt, vec_post, wo, wg, wu, wd)

    dz, ddm, dsgw, dgain, dbf, dscf, dscb = _mixer_bwd(z, dycat, cos_t, sin_t, dm, sg_w[0], sg_gain, bfull,
                                                       sf_all, sb_all)
    gwkv, acc_ctx, ddm_ctx = _ctx_backward(ctx2d, vec_ctx, wi, dm, dscf, dscb)
    gx, acc_in = _in_proj_bwd(dz, x2d, dx1, vec_in, wi)

    tk = min(512, D)
    g_wi = _tn_matmul(hx, dz, "grad_w_in", NCHIP, D, WI_C, False, "cols", tk)
    g_wi = _add_ctx_cols(g_wi, gwkv)
    g_wo = _tn_matmul(ycat, dy, "grad_w_out", NCHIP, WO_R, D, False, "shared", WO_R)
    g_wg = _tn_matmul(h2, da, "grad_w_gate", NCHIP, D, FF_C, False, "batched", tk)
    g_wu = _tn_matmul(h2, db, "grad_w_up", NCHIP, D, FF_C, False, "batched", tk)
    g_wd = _tn_matmul(act, df, "grad_w_down", NCHIP, FF_C, D, True, "shared", FF_C)

    full = [g_wi, g_wo, g_wg, g_wu, g_wd]
    names = ["w_in", "w_out", "w_gate", "w_up", "w_down"]
    full = [g.reshape(NCHIP, 2, g.shape[1] // 2, g.shape[2]) for g in full]
    cidx = ci.reshape(1).astype(jnp.int32)
    chipidx = chip.reshape(1).astype(jnp.int32)
    from_sib = _rs_exchange_halves(full)
    sums = [_rs_add_halves(g, r, cidx, "rs_add_halves_" + nm) for g, r, nm in zip(full, from_sib, names)]
    from_chips = _rs_send_chips([s[1] for s in sums])
    finals = [_rs_add_chips(s[0], r, chipidx, "rs_add_chips_" + nm) for s, r, nm in zip(sums, from_chips, names)]
    shared = _rs_share_final(finals)
    grads = [s.reshape(2 * s.shape[1], s.shape[2]) for s in shared]

    lg_part, dsgb = _small_reduce(ddm, ddm_ctx, dm, dbf)
    dmod = jnp.concatenate([acc_in[1:2], acc_in[2:3], acc_post[0:1], acc_post[2:3], acc_post[3:4],
                            acc_post[4:5]], axis=1)
    dcmod = jnp.concatenate([acc_ctx[1:2], acc_ctx[2:3], jnp.zeros((1, 4 * D), F32)], axis=1)
    packed = jnp.concatenate([
        _rows(dcmod), _rows(dmod), _rows(acc_in[0:1] + acc_ctx[0:1]), _rows(dgain), _rows(dsgw), dsgb, lg_part,
        _rows(acc_post[1:2]), _rows(acc_post[5:6]), _rows(acc_post[6:7])], axis=0)
    gathered = _allgather_small(packed, "gather_small")
    dmod_all = gathered[:, P_DMOD:P_N1].reshape(NDEV, 6 * D)
    dcmod_all = gathered[:, P_DCMOD:P_DMOD].reshape(NDEV, 6 * D)
    dmod_cols = lax.dynamic_slice_in_dim(dmod_all, chip * mod_c, mod_c, axis=1)
    dcmod_cols = lax.dynamic_slice_in_dim(dcmod_all, chip * mod_c, mod_c, axis=1)
    dmod_pad = jnp.concatenate([dmod_cols, jnp.zeros((CH - NDEV, mod_c), F32)], axis=0)
    ct_pad_t = jnp.concatenate([jnp.transpose(c_all), jnp.zeros((D, CH - NDEV), F32)], axis=1)
    g_wmod, cctx_part = _mod_backward(ct_pad_t, c_ctx.reshape(D, 1), dmod_pad, dcmod_cols, w_mod[0])
    parts = _allgather_small(cctx_part[0:1], "gather_cctx")
    g_cctx, d_cctx, nm_cctx, nv_cctx = _cctx_update(parts, c_ctx.reshape(1, D), m_c_ctx.reshape(1, D),
                                                    v_c_ctx.reshape(1, D))

    wp = _pack_small(b_mod, norm1, sg_gain, sg_w, sg_b, ret_logit_f, ret_logit_b, norm2, norm_f)
    mp = _pack_small(m_b_mod, m_norm1, m_sg_gain, m_sg_w, m_sg_b, m_ret_logit_f, m_ret_logit_b, m_norm2, m_norm_f)
    vp = _pack_small(v_b_mod, v_norm1, v_sg_gain, v_sg_w, v_sg_b, v_ret_logit_f, v_ret_logit_b, v_norm2, v_norm_f)
    gp, dp, mp2, vp2, loss_t = _small_update(gathered, wp, mp, vp)

    big_w = [w_mod[0], w_in[0], w_out[0], w_gate[0], w_up[0], w_down[0]]
    big_m = [m_w_mod[0], m_w_in[0], m_w_out[0], m_w_gate[0], m_w_up[0], m_w_down[0]]
    big_v = [v_w_mod[0], v_w_in[0], v_w_out[0], v_w_gate[0], v_w_up[0], v_w_down[0]]
    big_g = [g_wmod] + grads
    big = [_adamw(w, g, m, v, "adamw_" + nm) for w, g, m, v, nm in
           zip(big_w, big_g, big_m, big_v, ["w_mod"] + names)]

    def assemble(small, cctx, bigs):
        b_mod_, norm1_, gain_, sgw_, sgb_, lf_, lb_, norm2_, normf_ = _unpack_small(small)
        wm, wi_, wo_, wg_, wu_, wd_ = [b[None] for b in bigs]
        return [cctx.reshape(D), wm, b_mod_, norm1_, wi_, gain_, sgw_, sgb_, lf_, lb_, wo_, norm2_, wg_, wu_, wd_,
                normf_]

    out = [loss_t[0, 0], gx.reshape(1, ln, D)]
    out += assemble(gp, g_cctx, big_g)
    out += assemble(dp, d_cctx, [b[0] for b in big])
    out += assemble(mp2, nm_cctx, [b[1] for b in big])
    out += assemble(vp2, nv_cctx, [b[2] for b in big])
    return tuple(out)
```

```python
import functools

import jax
import jax.numpy as jnp
import numpy as np
from jax import lax
from jax.experimental import pallas as pl
from jax.experimental.pallas import tpu as pltpu

F32 = jnp.float32
BF = jnp.bfloat16
MESH = pl.DeviceIdType.MESH

D = 1024
CH = 128
HD = 128
NH = 4
AW = 512
IN_COLS = 3584
DFF = 2816
NCHIP = 4
NDEV = 8
WI_C = IN_COLS // NCHIP
FF_C = DFF // NCHIP
WO_R = D // NCHIP
EPS = 1e-6
GRID_W = 64
ROPE_BASE = 10000.0
K_SCALE = HD ** -0.5
LR, B1, B2, AEPS, WD, STEP = 0.001, 0.9, 0.999, 1e-08, 0.01, 10
VMEM_MB = 1 << 20
HI = lax.Precision.HIGHEST

P_DCMOD, P_DMOD, P_N1, P_GAIN, P_SGW, P_SGB, P_LG, P_N2, P_NF, P_LOSS = 0, 48, 96, 104, 108, 620, 624, 632, 640, 648
P_ROWS = 656
Q_BMOD, Q_N1, Q_GAIN, Q_SGW, Q_SGB, Q_LG, Q_N2, Q_NF = 0, 48, 56, 60, 572, 576, 584, 592
Q_ROWS = 600


def _params(vmem_mb, sem=None):
    return pltpu.CompilerParams(vmem_limit_bytes=vmem_mb * VMEM_MB, dimension_semantics=sem)


def _const(shape):
    nd = len(shape)
    return pl.BlockSpec(shape, lambda *_: (0,) * nd, pipeline_mode=pl.Buffered(1))


def _pos():
    return lax.axis_index("x"), lax.axis_index("y"), lax.axis_index("c")


def _dot(a, b, dims):
    return lax.dot_general(a, b, (dims, ((), ())), preferred_element_type=F32)


NN = ((1,), (0,))
NT = ((1,), (1,))
TN = ((0,), (0,))


@jax.custom_vjp
def _mm(a, b):
    return _dot(a.astype(BF), b.astype(BF), NN)


def _mm_f(a, b):
    return _mm(a, b), (a.astype(BF), b.astype(BF))


def _mm_b(res, g):
    a, b = res
    gb = g.astype(BF)
    return _dot(gb, b, NT), _dot(a, gb, TN)


_mm.defvjp(_mm_f, _mm_b)


@jax.custom_vjp
def _mm_nt(a, b):
    return _dot(a.astype(BF), b.astype(BF), NT)


def _mm_nt_f(a, b):
    return _mm_nt(a, b), (a.astype(BF), b.astype(BF))


def _mm_nt_b(res, g):
    a, b = res
    gb = g.astype(BF)
    return _dot(gb, b, NN), _dot(gb, a, TN)


_mm_nt.defvjp(_mm_nt_f, _mm_nt_b)


@jax.custom_vjp
def _mm_tn(a, b):
    return _dot(a.astype(BF), b.astype(BF), TN)


def _mm_tn_f(a, b):
    return _mm_tn(a, b), (a.astype(BF), b.astype(BF))


def _mm_tn_b(res, g):
    a, b = res
    gb = g.astype(BF)
    return _dot(b, gb, NT), _dot(a, gb, NN)


_mm_tn.defvjp(_mm_tn_f, _mm_tn_b)


def _gelu(x):
    return x * (0.5 * (1.0 + jnp.tanh(0.7978845608028654 * (x + 0.044715 * (x * x * x)))))


def _silu(x):
    return x * jax.nn.sigmoid(x)


def _rms(x):
    return lax.rsqrt(jnp.mean(x * x, axis=-1, keepdims=True) + EPS)


def _swap32(t):
    lane = lax.broadcasted_iota(jnp.int32, t.shape, 1)
    first = (lane % 64) < 32
    return jnp.where(first, pltpu.roll(t, 96, 1), pltpu.roll(t, 32, 1))


def _rope(t, cos, sin):
    return t * cos + _swap32(t) * sin


def _rope_bwd(d, cos, sin):
    return d * cos + _swap32(d * sin)


def _heads(ref, off=0):
    return [ref[:, off + h * HD: off + (h + 1) * HD].astype(F32) for h in range(NH)]


def _chunk_fwd(u, v, q, k, vr, gf, gb, sf, sb, df, xf, zf, db, xb, zb, sgw, gain, bfull):
    ya, yr, uf, ub = [], [], [], []
    for g in range(NH):
        gu = _gelu(u[g])
        gv = _gelu(v[g])
        vn = gv * _rms(gv) * gain[g]
        ya.append(gu * (_mm(sgw[g], vn) + bfull[g]))
    for h in range(NH):
        a = _mm_nt(q[h], k[h])
        of = _mm(a * df[h], vr[h]) + xf[h] * _mm(q[h], sf[h])
        ob = _mm(a * db[h], vr[h]) + xb[h] * _mm(q[h], sb[h])
        yr.append(_silu(gf[h]) * (of * _rms(of)) + _silu(gb[h]) * (ob * _rms(ob)))
        uf.append(_mm_tn(k[h], zf[h] * vr[h]))
        ub.append(_mm_tn(k[h], zb[h] * vr[h]))
    return ya, yr, uf, ub


def _fwd_dir_only(sf, q, k, vr, gf, df, xf):
    out = []
    for h in range(NH):
        a = _mm_nt(q[h], k[h])
        of = _mm(a * df[h], vr[h]) + xf[h] * _mm(q[h], sf[h])
        out.append(_silu(gf[h]) * (of * _rms(of)))
    return out


def _ctx_states(ctx0, ctx1, n1, csh, csc, wk, wv, zf, zb, ef, eb):
    hc0 = (ctx0 * _rms(ctx0) * n1) * (1.0 + csc) + csh
    hc1 = (ctx1 * _rms(ctx1) * n1) * (1.0 + csc) + csh
    scf, scb = [], []
    for h in range(NH):
        k0, k1 = _mm(hc0, wk[h]) * K_SCALE, _mm(hc1, wk[h]) * K_SCALE
        v0, v1 = _mm(hc0, wv[h]), _mm(hc1, wv[h])
        scf.append(ef[h] * _mm_tn(k0, zf[h] * v0) + _mm_tn(k1, zf[h] * v1))
        scb.append(eb[h] * _mm_tn(k1, zb[h] * v1) + _mm_tn(k0, zb[h] * v0))
    return scf, scb


def _allgather_small(v, name):
    r, n = v.shape

    def body(v_ref, out_ref, send_sems, recv_sems, local_sem):
        x, y, c = _pos()
        me = 4 * x + 2 * y + c
        mine = pltpu.make_async_copy(v_ref, out_ref.at[me], local_sem)
        mine.start()
        sent = []
        for k in range(1, NDEV):
            kx, ky, kc = (k >> 2) & 1, (k >> 1) & 1, k & 1
            peer = (x ^ kx, y ^ ky, c ^ kc)
            cp = pltpu.make_async_remote_copy(src_ref=v_ref, dst_ref=out_ref.at[me], send_sem=send_sems.at[k - 1],
                                              recv_sem=recv_sems.at[k - 1], device_id=peer, device_id_type=MESH)
            cp.start()
            sent.append(cp)
        for k in range(1, NDEV):
            kx, ky, kc = (k >> 2) & 1, (k >> 1) & 1, k & 1
            peer = (x ^ kx, y ^ ky, c ^ kc)
            src = 4 * (x ^ kx) + 2 * (y ^ ky) + (c ^ kc)
            pltpu.make_async_remote_copy(src_ref=v_ref, dst_ref=out_ref.at[src], send_sem=send_sems.at[k - 1],
                                         recv_sem=recv_sems.at[k - 1], device_id=peer, device_id_type=MESH).wait_recv()
        for cp in sent:
            cp.wait_send()
        mine.wait()

    return pl.pallas_call(
        body, name=name,
        out_shape=jax.ShapeDtypeStruct((NDEV, r, n), F32),
        in_specs=[pl.BlockSpec(memory_space=pltpu.VMEM)],
        out_specs=pl.BlockSpec(memory_space=pltpu.VMEM),
        scratch_shapes=[pltpu.SemaphoreType.DMA((NDEV - 1,)), pltpu.SemaphoreType.DMA((NDEV - 1,)),
                        pltpu.SemaphoreType.DMA],
        compiler_params=_params(16),
    )(v)


def _gather_weights(shards):
    nt = len(shards)
    shapes = [s.shape for s in shards]

    def body(*refs):
        srcs, outs, stages = refs[:nt], refs[nt:2 * nt], refs[2 * nt:3 * nt]
        ici_send, ici_recv, d2d_send, d2d_recv, local_sems = refs[3 * nt:]
        x, y, c = _pos()
        chip = 2 * x + y
        for t in range(nt):
            half = shapes[t][0] // 2
            stages[t][0] = srcs[t][0:half, :].astype(BF)
            stages[t][1] = srcs[t][half:2 * half, :].astype(BF)
        local = []
        for t in range(nt):
            cp = pltpu.make_async_copy(stages[t], outs[t].at[chip], local_sems.at[t])
            cp.start()
            local.append(cp)
        sent = []
        for k in range(1, NCHIP):
            kx, ky = (k >> 1) & 1, k & 1
            for t in range(nt):
                s = (k - 1) * nt + t
                cp = pltpu.make_async_remote_copy(
                    src_ref=stages[t].at[c], dst_ref=outs[t].at[chip, c], send_sem=ici_send.at[s],
                    recv_sem=ici_recv.at[s], device_id=(x ^ kx, y ^ ky, c), device_id_type=MESH)
                cp.start()
                sent.append(cp)
        for k in range(1, NCHIP):
            kx, ky = (k >> 1) & 1, k & 1
            src_chip = 2 * (x ^ kx) + (y ^ ky)
            for t in range(nt):
                s = (k - 1) * nt + t
                pltpu.make_async_remote_copy(
                    src_ref=stages[t].at[c], dst_ref=outs[t].at[src_chip, c], send_sem=ici_send.at[s],
                    recv_sem=ici_recv.at[s], device_id=(x ^ kx, y ^ ky, c), device_id_type=MESH).wait_recv()
                cp = pltpu.make_async_remote_copy(
                    src_ref=outs[t].at[src_chip, c], dst_ref=outs[t].at[src_chip, c], send_sem=d2d_send.at[s],
                    recv_sem=d2d_recv.at[s], device_id=(x, y, 1 - c), device_id_type=MESH)
                cp.start()
                sent.append(cp)
        for k in range(1, NCHIP):
            kx, ky = (k >> 1) & 1, k & 1
            src_chip = 2 * (x ^ kx) + (y ^ ky)
            for t in range(nt):
                s = (k - 1) * nt + t
                pltpu.make_async_remote_copy(
                    src_ref=stages[t].at[c], dst_ref=outs[t].at[src_chip, 1 - c], send_sem=d2d_send.at[s],
                    recv_sem=d2d_recv.at[s], device_id=(x, y, 1 - c), device_id_type=MESH).wait_recv()
        for cp in sent:
            cp.wait_send()
        for cp in local:
            cp.wait()

    n_rem = (NCHIP - 1) * nt
    out = pl.pallas_call(
        body, name="gather_weights",
        out_shape=[jax.ShapeDtypeStruct((NCHIP, 2, r // 2, cc), BF) for r, cc in shapes],
        in_specs=[pl.BlockSpec(memory_space=pltpu.VMEM)] * nt,
        out_specs=[pl.BlockSpec(memory_space=pl.ANY)] * nt,
        scratch_shapes=[pltpu.VMEM((2, r // 2, cc), BF) for r, cc in shapes]
        + [pltpu.SemaphoreType.DMA((n_rem,))] * 4 + [pltpu.SemaphoreType.DMA((nt,))],
        compiler_params=_params(48),
    )(*shards)
    return [o.reshape(NCHIP, r, cc) for o, (r, cc) in zip(out, shapes)]


def _rs_exchange_halves(grads):
    nt = len(grads)
    shapes = [g.shape for g in grads]

    def body(*refs):
        gs, outs = refs[:nt], refs[nt:2 * nt]
        send_sems, recv_sems = refs[2 * nt:]
        x, y, c = _pos()
        sib = (x, y, 1 - c)
        sent = []
        for t in range(nt):
            for j in range(NCHIP):
                s = t * NCHIP + j
                cp = pltpu.make_async_remote_copy(src_ref=gs[t].at[j, 1 - c], dst_ref=outs[t].at[j],
                                                  send_sem=send_sems.at[s], recv_sem=recv_sems.at[s],
                                                  device_id=sib, device_id_type=MESH)
                cp.start()
                sent.append(cp)
        for cp in sent:
            cp.wait_recv()
        for cp in sent:
            cp.wait_send()

    return pl.pallas_call(
        body, name="rs_exchange_halves",
        out_shape=[jax.ShapeDtypeStruct((NCHIP, s[2], s[3]), F32) for s in shapes],
        in_specs=[pl.BlockSpec(memory_space=pl.ANY)] * nt,
        out_specs=[pl.BlockSpec(memory_space=pl.ANY)] * nt,
        scratch_shapes=[pltpu.SemaphoreType.DMA((nt * NCHIP,))] * 2,
    )(*grads)


def _rs_send_chips(parts):
    nt = len(parts)
    shapes = [p.shape for p in parts]

    def body(*refs):
        ps, outs = refs[:nt], refs[nt:2 * nt]
        send_sems, recv_sems = refs[2 * nt:]
        x, y, c = _pos()
        sent = []
        for k in range(1, NCHIP):
            kx, ky = (k >> 1) & 1, k & 1
            dst_chip = 2 * (x ^ kx) + (y ^ ky)
            for t in range(nt):
                s = (k - 1) * nt + t
                cp = pltpu.make_async_remote_copy(src_ref=ps[t].at[dst_chip], dst_ref=outs[t].at[k - 1],
                                                  send_sem=send_sems.at[s], recv_sem=recv_sems.at[s],
                                                  device_id=(x ^ kx, y ^ ky, c), device_id_type=MESH)
                cp.start()
                sent.append(cp)
        for cp in sent:
            cp.wait_recv()
        for cp in sent:
            cp.wait_send()

    return pl.pallas_call(
        body, name="rs_send_chips",
        out_shape=[jax.ShapeDtypeStruct((NCHIP - 1, s[1], s[2]), BF) for s in shapes],
        in_specs=[pl.BlockSpec(memory_space=pl.ANY)] * nt,
        out_specs=[pl.BlockSpec(memory_space=pl.ANY)] * nt,
        scratch_shapes=[pltpu.SemaphoreType.DMA((nt * (NCHIP - 1),))] * 2,
    )(*parts)


def _rs_share_final(finals):
    nt = len(finals)
    shapes = [f.shape for f in finals]

    def body(*refs):
        fs, outs = refs[:nt], refs[nt:2 * nt]
        send_sems, recv_sems = refs[2 * nt:]
        x, y, c = _pos()
        sent = []
        for t in range(nt):
            cp = pltpu.make_async_remote_copy(src_ref=fs[t], dst_ref=outs[t], send_sem=send_sems.at[t],
                                              recv_sem=recv_sems.at[t], device_id=(x, y, 1 - c), device_id_type=MESH)
            cp.start()
            sent.append(cp)
        for cp in sent:
            cp.wait_recv()
        for cp in sent:
            cp.wait_send()

    return pl.pallas_call(
        body, name="rs_share_final",
        out_shape=[jax.ShapeDtypeStruct(s, F32) for s in shapes],
        in_specs=[pl.BlockSpec(memory_space=pl.ANY)] * nt,
        out_specs=[pl.BlockSpec(memory_space=pl.ANY)] * nt,
        scratch_shapes=[pltpu.SemaphoreType.DMA((nt,))] * 2,
    )(*finals)


def _row_tile(h, cc=D):
    for t in (512, 384, 352, 256, 176, 128, 64, 32, 16):
        if h % t == 0 and t * cc * 4 <= (5 * VMEM_MB) // 4:
            return t
    return h


def _rs_add_halves(g, recv, cidx, name):
    _, _, h, cc = g.shape
    th = _row_tile(h, cc)

    def body(c_ref, g_ref, r_ref, of_ref, ob_ref):
        s = g_ref[...] + r_ref[...]
        of_ref[...] = s
        ob_ref[...] = s.astype(BF)

    return pl.pallas_call(
        body, name=name,
        grid_spec=pltpu.PrefetchScalarGridSpec(
            num_scalar_prefetch=1, grid=(NCHIP, h // th),
            in_specs=[pl.BlockSpec((None, None, th, cc), lambda j, i, c_ref: (j, c_ref[0], i, 0)),
                      pl.BlockSpec((None, th, cc), lambda j, i, c_ref: (j, i, 0))],
            out_specs=[pl.BlockSpec((None, th, cc), lambda j, i, c_ref: (j, i, 0)),
                       pl.BlockSpec((None, th, cc), lambda j, i, c_ref: (j, i, 0))]),
        out_shape=[jax.ShapeDtypeStruct((NCHIP, h, cc), F32), jax.ShapeDtypeStruct((NCHIP, h, cc), BF)],
        compiler_params=_params(48),
    )(cidx, g, recv)


def _rs_add_chips(own, recv, chipidx, name):
    _, h, cc = own.shape
    th = _row_tile(h, cc)

    def body(j_ref, o_ref, r_ref, out_ref):
        out_ref[...] = ((o_ref[...] + r_ref[0].astype(F32)) + r_ref[1].astype(F32)) + r_ref[2].astype(F32)

    return pl.pallas_call(
        body, name=name,
        grid_spec=pltpu.PrefetchScalarGridSpec(
            num_scalar_prefetch=1, grid=(h // th,),
            in_specs=[pl.BlockSpec((None, th, cc), lambda i, j_ref: (j_ref[0], i, 0)),
                      pl.BlockSpec((NCHIP - 1, th, cc), lambda i, j_ref: (0, i, 0))],
            out_specs=pl.BlockSpec((th, cc), lambda i, j_ref: (i, 0))),
        out_shape=jax.ShapeDtypeStruct((h, cc), F32),
        compiler_params=_params(48),
    )(chipidx, own, recv)


def _adamw_math(w, g, m, v):
    m2 = B1 * m + (1.0 - B1) * g
    v2 = B2 * v + (1.0 - B2) * (g * g)
    m_hat = m2 / (1.0 - B1 ** STEP)
    v_hat = v2 / (1.0 - B2 ** STEP)
    delta = -LR * (m_hat / (jnp.sqrt(v_hat) + AEPS) + WD * w)
    return delta, m2, v2


def _adamw(w, g, m, v, name):
    r, cc = w.shape
    tr = _row_tile(r, cc)

    def body(w_ref, g_ref, m_ref, v_ref, d_ref, mo_ref, vo_ref):
        d, m2, v2 = _adamw_math(w_ref[...], g_ref[...], m_ref[...], v_ref[...])
        d_ref[...] = d
        mo_ref[...] = m2
        vo_ref[...] = v2

    spec = pl.BlockSpec((tr, cc), lambda i: (i, 0))
    return pl.pallas_call(
        body, name=name, grid=(r // tr,), in_specs=[spec] * 4, out_specs=[spec] * 3,
        out_shape=[jax.ShapeDtypeStruct((r, cc), F32)] * 3,
        compiler_params=_params(48, ("parallel",)),
    )(w, g, m, v)


def _adamw_halves(w, own, other, m, v, cidx, name):
    r, cc = w.shape
    h = r // 2
    tr = _row_tile(h, cc)
    per = h // tr

    def body(c_ref, w_ref, own_ref, oth_ref, m_ref, v_ref, g_ref, d_ref, mo_ref, vo_ref):
        mine = (pl.program_id(0) // per) == c_ref[0]
        g = jnp.where(mine, own_ref[...], oth_ref[...])
        g_ref[...] = g
        d, m2, v2 = _adamw_math(w_ref[...], g, m_ref[...], v_ref[...])
        d_ref[...] = d
        mo_ref[...] = m2
        vo_ref[...] = v2

    full = pl.BlockSpec((tr, cc), lambda i, c_ref: (i, 0))
    half = pl.BlockSpec((tr, cc), lambda i, c_ref: (i % per, 0))
    return pl.pallas_call(
        body, name=name,
        grid_spec=pltpu.PrefetchScalarGridSpec(
            num_scalar_prefetch=1, grid=(r // tr,),
            in_specs=[full, half, half, full, full], out_specs=[full] * 4),
        out_shape=[jax.ShapeDtypeStruct((r, cc), F32)] * 4,
        compiler_params=_params(48, ("parallel",)),
    )(cidx, w, own, other, m, v)


def _mod_forward(ct_pad, w_mod_s):
    def body(c_ref, w_ref, o_ref):
        o_ref[...] = jnp.dot(_silu(c_ref[...]), w_ref[...], precision=HI, preferred_element_type=F32)

    return pl.pallas_call(
        body, name="mod_forward",
        out_shape=jax.ShapeDtypeStruct((16, w_mod_s.shape[1]), F32),
        in_specs=[pl.BlockSpec(memory_space=pltpu.VMEM)] * 2,
        out_specs=pl.BlockSpec(memory_space=pltpu.VMEM),
        compiler_params=_params(32),
    )(ct_pad, w_mod_s)


def _decay_exponents():
    ri = lax.broadcasted_iota(jnp.int32, (CH, CH), 0).astype(F32)
    ci = lax.broadcasted_iota(jnp.int32, (CH, CH), 1).astype(F32)
    full = jnp.full((CH, CH), float(CH), F32)
    return [[ri - ci, ri + 1.0, (CH - 1.0) - ri, full], [ci - ri, CH - ri, ri, full]]


def _decay_mats(logit_full):
    def body(l_ref, o_ref):
        ex = _decay_exponents()
        for d in range(2):
            for h in range(NH):
                lv = l_ref[d * NH + h]
                lg = jnp.minimum(lv, 0.0) - jnp.log(1.0 + jnp.exp(-jnp.abs(lv)))
                for kind in range(4):
                    m = jnp.exp(lg * ex[d][kind])
                    if kind == 0:
                        m = jnp.where(ex[d][0] >= 0.0, jnp.exp(lg * jnp.maximum(ex[d][0], 0.0)), 0.0)
                    o_ref[d, kind, h] = m

    return pl.pallas_call(
        body, name="decay_mats",
        out_shape=jax.ShapeDtypeStruct((2, 4, NH, CH, CH), F32),
        in_specs=[pl.BlockSpec(memory_space=pltpu.VMEM)],
        out_specs=pl.BlockSpec(memory_space=pltpu.VMEM),
        compiler_params=_params(32),
    )(logit_full)


def _ctx_kv_weights(wi_ref):
    def cols(g):
        return wi_ref[g // WI_C, :, g % WI_C: g % WI_C + HD].astype(F32)

    wk = [cols(3 * AW + h * HD) for h in range(NH)]
    wv = [cols(4 * AW + h * HD) for h in range(NH)]
    return wk, wv


def _ctx_forward(ctx, vecs, wi, dm):
    def body(ctx_ref, v_ref, wi_ref, dm_ref, scf_ref, scb_ref):
        wk, wv = _ctx_kv_weights(wi_ref)
        mats = [[dm_ref[d, kind, h] for h in range(NH)] for d in range(2) for kind in (2, 3)]
        scf, scb = _ctx_states(ctx_ref[0:CH, :], ctx_ref[CH:2 * CH, :], v_ref[0:1, :], v_ref[1:2, :],
                               v_ref[2:3, :], wk, wv, mats[0], mats[2], mats[1], mats[3])
        for h in range(NH):
            scf_ref[h] = scf[h]
            scb_ref[h] = scb[h]

    return pl.pallas_call(
        body, name="ctx_forward",
        out_shape=[jax.ShapeDtypeStruct((NH, HD, HD), F32)] * 2,
        in_specs=[pl.BlockSpec(memory_space=pltpu.VMEM)] * 4,
        out_specs=[pl.BlockSpec(memory_space=pltpu.VMEM)] * 2,
        compiler_params=_params(48),
    )(ctx, vecs, wi, dm)


def _ctx_backward(ctx, vecs, wi, dm, dscf, dscb):
    def body(ctx_ref, v_ref, wi_ref, dm_ref, gf_ref, gb_ref, gw_ref, gv_ref, gdm_ref):
        wk, wv = _ctx_kv_weights(wi_ref)
        mats = [[dm_ref[d, kind, h] for h in range(NH)] for d in range(2) for kind in (2, 3)]
        ctx0, ctx1 = ctx_ref[0:CH, :], ctx_ref[CH:2 * CH, :]

        def fn(n1, csh, csc, wk_, wv_, zf, zb, ef, eb):
            return _ctx_states(ctx0, ctx1, n1, csh, csc, wk_, wv_, zf, zb, ef, eb)

        _, vjp = jax.vjp(fn, v_ref[0:1, :], v_ref[1:2, :], v_ref[2:3, :], wk, wv,
                         mats[0], mats[2], mats[1], mats[3])
        cot = ([gf_ref[h] for h in range(NH)], [gb_ref[h] for h in range(NH)])
        dn1, dcsh, dcsc, dwk, dwv, dzf, dzb, def_, deb = vjp(cot)
        for h in range(NH):
            gw_ref[:, h * HD:(h + 1) * HD] = dwk[h]
            gw_ref[:, AW + h * HD:AW + (h + 1) * HD] = dwv[h]
        gv_ref[...] = jnp.zeros_like(gv_ref)
        gv_ref[0:1, :] = dn1
        gv_ref[1:2, :] = dcsh
        gv_ref[2:3, :] = dcsc
        for h in range(NH):
            gdm_ref[0, 0, h] = dzf[h]
            gdm_ref[0, 1, h] = def_[h]
            gdm_ref[1, 0, h] = dzb[h]
            gdm_ref[1, 1, h] = deb[h]

    return pl.pallas_call(
        body, name="ctx_backward",
        out_shape=[jax.ShapeDtypeStruct((D, 2 * AW), F32), jax.ShapeDtypeStruct((8, D), F32),
                   jax.ShapeDtypeStruct((2, 2, NH, CH, CH), F32)],
        in_specs=[pl.BlockSpec(memory_space=pltpu.VMEM)] * 6,
        out_specs=[pl.BlockSpec(memory_space=pltpu.VMEM)] * 3,
        compiler_params=_params(56),
    )(ctx, vecs, wi, dm, dscf, dscb)


def _in_proj(x, vecs, wi):
    ln = x.shape[0]
    t = min(512, ln)

    def body(x_ref, v_ref, wi_ref, z_ref, hx_ref):
        xv = x_ref[...]
        hx = (xv * _rms(xv) * v_ref[0:1, :]) * (1.0 + v_ref[2:3, :]) + v_ref[1:2, :]
        hb = hx.astype(BF)
        hx_ref[...] = hb
        for j in range(NCHIP):
            z_ref[:, j * WI_C:(j + 1) * WI_C] = _dot(hb, wi_ref[j], NN).astype(BF)

    return pl.pallas_call(
        body, name="in_proj", grid=(ln // t,),
        in_specs=[pl.BlockSpec((t, D), lambda i: (i, 0)), _const((8, D)), _const((NCHIP, D, WI_C))],
        out_specs=[pl.BlockSpec((t, IN_COLS), lambda i: (i, 0)), pl.BlockSpec((t, D), lambda i: (i, 0))],
        out_shape=[jax.ShapeDtypeStruct((ln, IN_COLS), BF), jax.ShapeDtypeStruct((ln, D), BF)],
        compiler_params=_params(56, ("parallel",)),
    )(x, vecs, wi)


def _in_proj_bwd(dz, x, dx1, vecs, wi):
    ln = x.shape[0]
    t = min(512, ln)

    def body(dz_ref, x_ref, dx1_ref, v_ref, wi_ref, gx_ref, acc_ref):
        @pl.when(pl.program_id(0) == 0)
        def _():
            acc_ref[...] = jnp.zeros_like(acc_ref)

        dhx = jnp.zeros((t, D), F32)
        for j in range(NCHIP):
            dhx = dhx + _dot(dz_ref[:, j * WI_C:(j + 1) * WI_C], wi_ref[j], NT)
        xv = x_ref[...]
        r = _rms(xv)
        xn = xv * r
        n1, sc = v_ref[0:1, :], v_ref[2:3, :]
        acc_ref[0:1, :] += jnp.sum(dhx * xn * (1.0 + sc), axis=0, keepdims=True)
        acc_ref[1:2, :] += jnp.sum(dhx, axis=0, keepdims=True)
        acc_ref[2:3, :] += jnp.sum(dhx * xn * n1, axis=0, keepdims=True)
        g = dhx * n1 * (1.0 + sc)
        gx_ref[...] = dx1_ref[...] + r * (g - xn * jnp.mean(g * xn, axis=-1, keepdims=True))

    return pl.pallas_call(
        body, name="in_proj_bwd", grid=(ln // t,),
        in_specs=[pl.BlockSpec((t, IN_COLS), lambda i: (i, 0)), pl.BlockSpec((t, D), lambda i: (i, 0)),
                  pl.BlockSpec((t, D), lambda i: (i, 0)), _const((8, D)), _const((NCHIP, D, WI_C))],
        out_specs=[pl.BlockSpec((t, D), lambda i: (i, 0)), pl.BlockSpec((8, D), lambda i: (0, 0))],
        out_shape=[jax.ShapeDtypeStruct((ln, D), F32), jax.ShapeDtypeStruct((8, D), F32)],
        compiler_params=_params(56, ("arbitrary",)),
    )(dz, x, dx1, vecs, wi)


def _post_mixer(x, ycat, tgt, vecs, wo, wg, wu, wd):
    ln = x.shape[0]
    t = min(256, ln)

    def body(x_ref, y_ref, t_ref, v_ref, wo_ref, wg_ref, wu_ref, wd_ref,
             dx1_ref, dyc_ref, h2_ref, dy_ref, df_ref, act_ref, da_ref, db_ref, acc_ref, a_st, b_st):
        @pl.when(pl.program_id(0) == 0)
        def _():
            acc_ref[...] = jnp.zeros_like(acc_ref)

        g1, n2, sh2, sc2 = v_ref[0:1, :], v_ref[1:2, :], v_ref[2:3, :], v_ref[3:4, :]
        g2, nf = v_ref[4:5, :], v_ref[5:6, :]
        y = _dot(y_ref[...], wo_ref[...], NN)
        x1 = x_ref[...] + g1 * y
        r2 = _rms(x1)
        xn2 = x1 * r2
        t2 = xn2 * n2
        h2b = (t2 * (1.0 + sc2) + sh2).astype(BF)
        h2_ref[...] = h2b
        f = jnp.zeros((t, D), F32)
        for j in range(NCHIP):
            a = _dot(h2b, wg_ref[j], NN)
            b = _dot(h2b, wu_ref[j], NN)
            a_st[j] = a.astype(BF)
            b_st[j] = b.astype(BF)
            act = (_silu(a) * b).astype(BF)
            act_ref[j] = act
            f = f + _dot(act, wd_ref[j], NN)
        x2 = x1 + g2 * f
        r3 = _rms(x2)
        xn3 = x2 * r3
        e = xn3 * nf - t_ref[...]
        acc_ref[6:7, :] += jnp.sum(e * e, axis=0, keepdims=True) * (0.5 / D)
        dout = e * (1.0 / D)
        acc_ref[5:6, :] += jnp.sum(dout * xn3, axis=0, keepdims=True)
        gg = dout * nf
        dx2 = r3 * (gg - xn3 * jnp.mean(gg * xn3, axis=-1, keepdims=True))
        acc_ref[4:5, :] += jnp.sum(dx2 * f, axis=0, keepdims=True)
        dfb = (g2 * dx2).astype(BF)
        df_ref[...] = dfb
        dh2 = jnp.zeros((t, D), F32)
        for j in range(NCHIP):
            dact = _dot(dfb, wd_ref[j], NT)
            a = a_st[j].astype(F32)
            b = b_st[j].astype(F32)
            s = jax.nn.sigmoid(a)
            da = (dact * b * (s * (1.0 + a * (1.0 - s)))).astype(BF)
            db = (dact * (a * s)).astype(BF)
            da_ref[j] = da
            db_ref[j] = db
            dh2 = dh2 + _dot(da, wg_ref[j], NT) + _dot(db, wu_ref[j], NT)
        acc_ref[2:3, :] += jnp.sum(dh2, axis=0, keepdims=True)
        acc_ref[3:4, :] += jnp.sum(dh2 * t2, axis=0, keepdims=True)
        acc_ref[1:2, :] += jnp.sum(dh2 * xn2 * (1.0 + sc2), axis=0, keepdims=True)
        gx = dh2 * n2 * (1.0 + sc2)
        dx1 = dx2 + r2 * (gx - xn2 * jnp.mean(gx * xn2, axis=-1, keepdims=True))
        dx1_ref[...] = dx1
        acc_ref[0:1, :] += jnp.sum(dx1 * y, axis=0, keepdims=True)
        dyb = (g1 * dx1).astype(BF)
        dy_ref[...] = dyb
        dyc_ref[...] = _dot(dyb, wo_ref[...], NT).astype(BF)

    tok = pl.BlockSpec((t, D), lambda i: (i, 0))
    ffb = pl.BlockSpec((NCHIP, t, FF_C), lambda i: (0, i, 0))
    return pl.pallas_call(
        body, name="post_mixer", grid=(ln // t,),
        in_specs=[tok, tok, tok, _const((8, D)), _const((D, D)), _const((NCHIP, D, FF_C)),
                  _const((NCHIP, D, FF_C)), _const((NCHIP, FF_C, D))],
        out_specs=[tok, tok, tok, tok, tok, ffb, ffb, ffb, pl.BlockSpec((16, D), lambda i: (0, 0))],
        out_shape=[jax.ShapeDtypeStruct((ln, D), F32)] + [jax.ShapeDtypeStruct((ln, D), BF)] * 4
        + [jax.ShapeDtypeStruct((NCHIP, ln, FF_C), BF)] * 3 + [jax.ShapeDtypeStruct((16, D), F32)],
        scratch_shapes=[pltpu.VMEM((NCHIP, t, FF_C), BF)] * 2,
        compiler_params=_params(60, ("arbitrary",)),
    )(x, ycat, tgt, vecs, wo, wg, wu, wd)


def _tn_matmul(xa, dy, name, nb, k1, n, x_batched, dy_mode, tt):
    ln = xa.shape[-2]
    tt = min(tt, ln)

    def body(x_ref, dy_ref, o_ref):
        @pl.when(pl.program_id(0) == 0)
        def _():
            o_ref[...] = jnp.zeros_like(o_ref)

        xt = None if x_batched else jnp.transpose(x_ref[...])
        for b in range(nb):
            lhs = jnp.transpose(x_ref[b]) if x_batched else xt
            if dy_mode == "batched":
                rhs = dy_ref[b]
            elif dy_mode == "cols":
                rhs = dy_ref[:, b * n:(b + 1) * n]
            else:
                rhs = dy_ref[...]
            o_ref[b] += _dot(lhs, rhs, NN)

    x_spec = (pl.BlockSpec((nb, tt, k1), lambda t: (0, t, 0)) if x_batched
              else pl.BlockSpec((tt, k1), lambda t: (t, 0)))
    if dy_mode == "batched":
        dy_spec = pl.BlockSpec((nb, tt, n), lambda t: (0, t, 0))
    elif dy_mode == "cols":
        dy_spec = pl.BlockSpec((tt, nb * n), lambda t: (t, 0))
    else:
        dy_spec = pl.BlockSpec((tt, n), lambda t: (t, 0))
    return pl.pallas_call(
        body, name=name, grid=(ln // tt,),
        in_specs=[x_spec, dy_spec],
        out_specs=pl.BlockSpec((nb, k1, n), lambda t: (0, 0, 0)),
        out_shape=jax.ShapeDtypeStruct((nb, k1, n), F32),
        compiler_params=_params(60, ("arbitrary",)),
    )(xa, dy)


def _add_ctx_cols(gwi, gwkv):
    first = 1536 // HD
    per = WI_C // HD

    def body(g_ref, a_ref, o_ref):
        o_ref[...] = g_ref[...] + a_ref[...]

    spec = pl.BlockSpec((None, D, HD), lambda i: ((first + i) // per, 0, (first + i) % per))
    return pl.pallas_call(
        body, name="add_ctx_cols", grid=(2 * AW // HD,),
        in_specs=[spec, pl.BlockSpec((D, HD), lambda i: (0, i))],
        out_specs=spec,
        out_shape=jax.ShapeDtypeStruct(gwi.shape, F32),
        input_output_aliases={0: 0},
        compiler_params=_params(32, ("arbitrary",)),
    )(gwi, gwkv)


def _mixer_fwd(z, cos_t, sin_t, dm, sgw, gain, bfull, scf, scb):
    ln = z.shape[0]
    nc = ln // CH

    def rev(p, n):
        return p * n + (1 - p) * (nc - 1 - n)

    def col(j, both):
        if both:
            return pl.BlockSpec((CH, AW), lambda p, n: (rev(p, n), j))
        return pl.BlockSpec((CH, AW), lambda p, n: (p * n, j))

    def body(u_ref, v_ref, q_ref, k_ref, vr_ref, gf_ref, gb_ref, cos_ref, sin_ref, dm_ref, sgw_ref, gain_ref,
             bfull_ref, scf_ref, scb_ref, y_ref, sf_ref, sb_ref, sb_all, st):
        p, n = pl.program_id(0), pl.program_id(1)
        cos, sin = cos_ref[...], sin_ref[...]
        k = [_rope(t, cos, sin) * K_SCALE for t in _heads(k_ref)]
        vr = _heads(vr_ref)

        @pl.when(p == 0)
        def _():
            @pl.when(n == 0)
            def _():
                st[...] = scb_ref[...]

            m = nc - 1 - n
            for h in range(NH):
                sb_all[m, h] = st[h].astype(BF)
                st[h] = dm_ref[1, 3, h] * st[h] + _mm_tn(k[h], dm_ref[1, 2, h] * vr[h])

        @pl.when(p == 1)
        def _():
            @pl.when(n == 0)
            def _():
                st[...] = scf_ref[...]

            q = [_rope(t, cos, sin) for t in _heads(q_ref)]
            sf = [st[h] for h in range(NH)]
            sb = [sb_all[n, h].astype(F32) for h in range(NH)]
            mats = [[dm_ref[d, kind, h] for h in range(NH)] for d in range(2) for kind in range(3)]
            ya, yr, uf, _ = _chunk_fwd(
                _heads(u_ref), _heads(v_ref), q, k, vr, _heads(gf_ref), _heads(gb_ref), sf, sb,
                mats[0], mats[1], mats[2], mats[3], mats[4], mats[5],
                [sgw_ref[g] for g in range(NH)], [gain_ref[:, g * HD:(g + 1) * HD] for g in range(NH)],
                [bfull_ref[g] for g in range(NH)])
            for h in range(NH):
                y_ref[:, h * HD:(h + 1) * HD] = ya[h].astype(BF)
                y_ref[:, AW + h * HD:AW + (h + 1) * HD] = yr[h].astype(BF)
                sf_ref[0, h] = sf[h].astype(BF)
                sb_ref[0, h] = sb_all[n, h]
                st[h] = dm_ref[0, 3, h] * st[h] + uf[h]

    tab = pl.BlockSpec((CH, HD), lambda p, n: (rev(p, n), 0))
    st_spec = pl.BlockSpec((1, NH, HD, HD), lambda p, n: (p * n, 0, 0, 0))
    return pl.pallas_call(
        body, name="mixer_fwd", grid=(2, nc),
        in_specs=[col(0, False), col(1, False), col(2, False), col(3, True), col(4, True), col(5, False),
                  col(6, False), tab, tab, _const((2, 4, NH, CH, CH)), _const((NH, CH, CH)), _const((1, AW)),
                  _const((NH, CH, CH)), _const((NH, HD, HD)), _const((NH, HD, HD))],
        out_specs=[pl.BlockSpec((CH, D), lambda p, n: (p * n, 0)), st_spec, st_spec],
        out_shape=[jax.ShapeDtypeStruct((ln, D), BF), jax.ShapeDtypeStruct((nc, NH, HD, HD), BF),
                   jax.ShapeDtypeStruct((nc, NH, HD, HD), BF)],
        scratch_shapes=[pltpu.VMEM((nc, NH, HD, HD), BF), pltpu.VMEM((NH, HD, HD), F32)],
        compiler_params=_params(56, ("arbitrary", "arbitrary")),
    )(z, z, z, z, z, z, z, cos_t, sin_t, dm, sgw, gain, bfull, scf, scb)


def _mixer_bwd(z, dycat, cos_t, sin_t, dm, sgw, gain, bfull, sf_all, sb_all):
    ln = z.shape[0]
    nc = ln // CH

    def rev(p, n):
        return p * n + (1 - p) * (nc - 1 - n)

    def col(j, both):
        if both:
            return pl.BlockSpec((CH, AW), lambda p, n: (rev(p, n), j))
        return pl.BlockSpec((CH, AW), lambda p, n: (p * n, j))

    def body(u_ref, v_ref, q_ref, k_ref, vr_ref, gf_ref, gb_ref, dya_ref, dyr_ref, cos_ref, sin_ref, dm_ref,
             sgw_ref, gain_ref, bfull_ref, sf_ref, sb_ref,
             dz_ref, ddm_ref, dsgw_ref, dgain_ref, dbf_ref, dscf_ref, dscb_ref, gf_all, run):
        p, n = pl.program_id(0), pl.program_id(1)
        cos, sin = cos_ref[...], sin_ref[...]
        q = [_rope(t, cos, sin) for t in _heads(q_ref)]
        k = [_rope(t, cos, sin) * K_SCALE for t in _heads(k_ref)]
        vr = _heads(vr_ref)
        gf = _heads(gf_ref)
        dyr = _heads(dyr_ref)
        sf = [sf_ref[0, h].astype(F32) for h in range(NH)]
        mats = [[dm_ref[d, kind, h] for h in range(NH)] for d in range(2) for kind in range(3)]

        @pl.when(p == 0)
        def _():
            @pl.when(n == 0)
            def _():
                run[...] = jnp.zeros_like(run)
                ddm_ref[...] = jnp.zeros_like(ddm_ref)
                dsgw_ref[...] = jnp.zeros_like(dsgw_ref)
                dgain_ref[...] = jnp.zeros_like(dgain_ref)
                dbf_ref[...] = jnp.zeros_like(dbf_ref)

            m = nc - 1 - n
            _, vjp = jax.vjp(lambda s: _fwd_dir_only(s, q, k, vr, gf, mats[0], mats[1]), sf)
            (dsf,) = vjp(dyr)
            for h in range(NH):
                gf_all[m, h] = run[h].astype(BF)
                run[h] = dsf[h] + dm_ref[0, 3, h] * run[h]

            @pl.when(n == nc - 1)
            def _():
                dscf_ref[...] = run[...]

        @pl.when(p == 1)
        def _():
            @pl.when(n == 0)
            def _():
                run[...] = jnp.zeros_like(run)

            sb = [sb_ref[0, h].astype(F32) for h in range(NH)]
            g_f = [gf_all[n, h].astype(F32) for h in range(NH)]
            g_b = [run[h] for h in range(NH)]
            args = (_heads(u_ref), _heads(v_ref), q, k, vr, gf, _heads(gb_ref), sb,
                    mats[0], mats[1], mats[2], mats[3], mats[4], mats[5],
                    [sgw_ref[g] for g in range(NH)], [gain_ref[:, g * HD:(g + 1) * HD] for g in range(NH)],
                    [bfull_ref[g] for g in range(NH)])

            def fn(u_, v_, q_, k_, vr_, gf_, gb_, sb_, df, xf, zf, db, xb, zb, sgw_, gain_, bfull_):
                return _chunk_fwd(u_, v_, q_, k_, vr_, gf_, gb_, sf, sb_, df, xf, zf, db, xb, zb, sgw_, gain_,
                                  bfull_)

            _, vjp = jax.vjp(fn, *args)
            (du, dv, dq, dk, dvr, dgf, dgb, dsb, ddf, dxf, dzf, ddb, dxb, dzb, dsgw, dgain, dbf) = vjp(
                (_heads(dya_ref), dyr, g_f, g_b))
            for h in range(NH):
                s = slice(h * HD, (h + 1) * HD)
                dz_ref[:, h * HD:(h + 1) * HD] = du[h].astype(BF)
                dz_ref[:, AW + h * HD:AW + (h + 1) * HD] = dv[h].astype(BF)
                dz_ref[:, 2 * AW + h * HD:2 * AW + (h + 1) * HD] = _rope_bwd(dq[h], cos, sin).astype(BF)
                dz_ref[:, 3 * AW + h * HD:3 * AW + (h + 1) * HD] = _rope_bwd(dk[h] * K_SCALE, cos, sin).astype(BF)
                dz_ref[:, 4 * AW + h * HD:4 * AW + (h + 1) * HD] = dvr[h].astype(BF)
                dz_ref[:, 5 * AW + h * HD:5 * AW + (h + 1) * HD] = dgf[h].astype(BF)
                dz_ref[:, 6 * AW + h * HD:6 * AW + (h + 1) * HD] = dgb[h].astype(BF)
                ddm_ref[0, 0, h] += ddf[h]
                ddm_ref[0, 1, h] += dxf[h]
                ddm_ref[0, 2, h] += dzf[h]
                ddm_ref[0, 3, h] += sf[h] * g_f[h]
                ddm_ref[1, 0, h] += ddb[h]
                ddm_ref[1, 1, h] += dxb[h]
                ddm_ref[1, 2, h] += dzb[h]
                ddm_ref[1, 3, h] += sb[h] * g_b[h]
                dsgw_ref[h] += dsgw[h]
                dgain_ref[:, s] += dgain[h]
                dbf_ref[h] += dbf[h]
                run[h] = dsb[h] + dm_ref[1, 3, h] * run[h]

            @pl.when(n == nc - 1)
            def _():
                dscb_ref[...] = run[...]

    tab = pl.BlockSpec((CH, HD), lambda p, n: (rev(p, n), 0))
    tile4 = jax.ShapeDtypeStruct((NH, CH, CH), F32)
    return pl.pallas_call(
        body, name="mixer_bwd", grid=(2, nc),
        in_specs=[col(0, False), col(1, False), col(2, True), col(3, True), col(4, True), col(5, True),
                  col(6, False),
                  pl.BlockSpec((CH, AW), lambda p, n: (p * n, 0)), pl.BlockSpec((CH, AW), lambda p, n: (rev(p, n), 1)),
                  tab, tab, _const((2, 4, NH, CH, CH)), _const((NH, CH, CH)), _const((1, AW)),
                  _const((NH, CH, CH)),
                  pl.BlockSpec((1, NH, HD, HD), lambda p, n: (rev(p, n), 0, 0, 0)),
                  pl.BlockSpec((1, NH, HD, HD), lambda p, n: (p * n, 0, 0, 0))],
        out_specs=[pl.BlockSpec((CH, IN_COLS), lambda p, n: (p * n, 0)),
                   pl.BlockSpec((2, 4, NH, CH, CH), lambda p, n: (0, 0, 0, 0, 0)),
                   pl.BlockSpec((NH, CH, CH), lambda p, n: (0, 0, 0)),
                   pl.BlockSpec((1, AW), lambda p, n: (0, 0)),
                   pl.BlockSpec((NH, CH, CH), lambda p, n: (0, 0, 0)),
                   pl.BlockSpec((NH, HD, HD), lambda p, n: (0, 0, 0)),
                   pl.BlockSpec((NH, HD, HD), lambda p, n: (0, 0, 0))],
        out_shape=[jax.ShapeDtypeStruct((ln, IN_COLS), BF), jax.ShapeDtypeStruct((2, 4, NH, CH, CH), F32),
                   tile4, jax.ShapeDtypeStruct((1, AW), F32), tile4, tile4, tile4],
        scratch_shapes=[pltpu.VMEM((nc, NH, HD, HD), BF), pltpu.VMEM((NH, HD, HD), F32)],
        compiler_params=_params(56, ("arbitrary", "arbitrary")),
    )(z, z, z, z, z, z, z, dycat, dycat, cos_t, sin_t, dm, sgw, gain, bfull, sf_all, sb_all)


def _small_reduce(ddm, ddm_ctx, dm, dbf):
    def body(ddm_ref, dctx_ref, dm_ref, dbf_ref, lg_ref, sgb_ref):
        ex = _decay_exponents()
        ones = jnp.ones((8, CH), F32)
        for d in range(2):
            for h in range(NH):
                tot = jnp.zeros((CH, CH), F32)
                for kind in range(4):
                    g = ddm_ref[d, kind, h]
                    if kind >= 2:
                        g = g + dctx_ref[d, kind - 2, h]
                    tot = tot + g * dm_ref[d, kind, h] * ex[d][kind]
                lg_ref[d * NH + h: d * NH + h + 1, :] = jnp.sum(tot, axis=0, keepdims=True)
        for g in range(NH):
            r = lax.dot_general(ones, dbf_ref[g], (NT, ((), ())), precision=HI, preferred_element_type=F32)
            sgb_ref[g:g + 1, :] = r[0:1, :]

    return pl.pallas_call(
        body, name="small_reduce",
        out_shape=[jax.ShapeDtypeStruct((8, CH), F32), jax.ShapeDtypeStruct((NH, CH), F32)],
        in_specs=[pl.BlockSpec(memory_space=pltpu.VMEM)] * 4,
        out_specs=[pl.BlockSpec(memory_space=pltpu.VMEM)] * 2,
        compiler_params=_params(32),
    )(ddm, ddm_ctx, dm, dbf)


def _mod_backward(ct_pad_t, cctx_col, dmod_pad, dcmod_cols, w_mod_s):
    def body(ct_ref, cc_ref, dm_ref, dc_ref, w_ref, gw_ref, part_ref):
        dcm = dc_ref[0:1, :]
        for d in range(1, NDEV):
            dcm = dcm + dc_ref[d:d + 1, :]
        gw_ref[...] = (jnp.dot(_silu(ct_ref[...]), dm_ref[...], precision=HI, preferred_element_type=F32)
                       + _silu(cc_ref[...]) * dcm)
        part_ref[...] = lax.dot_general(jnp.broadcast_to(dcm, (8, dcm.shape[1])), w_ref[...], (NT, ((), ())),
                                        precision=HI, preferred_element_type=F32)

    return pl.pallas_call(
        body, name="mod_backward",
        out_shape=[jax.ShapeDtypeStruct(w_mod_s.shape, F32), jax.ShapeDtypeStruct((8, D), F32)],
        in_specs=[pl.BlockSpec(memory_space=pltpu.VMEM)] * 5,
        out_specs=[pl.BlockSpec(memory_space=pltpu.VMEM)] * 2,
        compiler_params=_params(48),
    )(ct_pad_t, cctx_col, dmod_pad, dcmod_cols, w_mod_s)


def _cctx_update(parts, c_ctx, m, v):
    def body(p_ref, c_ref, m_ref, v_ref, g_ref, d_ref, mo_ref, vo_ref):
        tot = ((p_ref[0] + p_ref[2]) + p_ref[4]) + p_ref[6]
        cv = c_ref[...]
        s = jax.nn.sigmoid(cv)
        g = tot * (s * (1.0 + cv * (1.0 - s)))
        g_ref[...] = g
        d_ref[...], mo_ref[...], vo_ref[...] = _adamw_math(cv, g, m_ref[...], v_ref[...])

    return pl.pallas_call(
        body, name="cctx_update",
        out_shape=[jax.ShapeDtypeStruct((1, D), F32)] * 4,
        in_specs=[pl.BlockSpec(memory_space=pltpu.VMEM)] * 4,
        out_specs=[pl.BlockSpec(memory_space=pltpu.VMEM)] * 4,
        compiler_params=_params(16),
    )(parts, c_ctx, m, v)


def _small_update(gathered, wp, mp, vp):
    def body(g_ref, w_ref, m_ref, v_ref, go_ref, d_ref, mo_ref, vo_ref, loss_ref):
        tot = g_ref[0]
        for d in range(1, NDEV):
            tot = tot + g_ref[d]
        go_ref[Q_BMOD:Q_N1, :] = tot[P_DMOD:P_N1, :] + tot[P_DCMOD:P_DMOD, :]
        go_ref[Q_N1:Q_LG, :] = tot[P_N1:P_LG, :]
        lg = jnp.sum(tot[P_LG:P_N2, :], axis=1, keepdims=True)
        go_ref[Q_LG:Q_N2, :] = lg * jax.nn.sigmoid(-w_ref[Q_LG:Q_N2, :])
        go_ref[Q_N2:Q_ROWS, :] = tot[P_N2:P_LOSS, :]
        d_ref[...], mo_ref[...], vo_ref[...] = _adamw_math(w_ref[...], go_ref[...], m_ref[...], v_ref[...])
        ls = jnp.sum(jnp.sum(tot[P_LOSS:P_ROWS, :], axis=1, keepdims=True), axis=0, keepdims=True)
        loss_ref[...] = jnp.broadcast_to(ls, (8, CH))

    return pl.pallas_call(
        body, name="small_update",
        out_shape=[jax.ShapeDtypeStruct((Q_ROWS, CH), F32)] * 4 + [jax.ShapeDtypeStruct((8, CH), F32)],
        in_specs=[pl.BlockSpec(memory_space=pltpu.VMEM)] * 4,
        out_specs=[pl.BlockSpec(memory_space=pltpu.VMEM)] * 5,
        compiler_params=_params(32),
    )(gathered, wp, mp, vp)


def _rows(a):
    return a.reshape(-1, CH)


def _pack_small(b_mod, norm1, sg_gain, sg_w, sg_b, lf, lb, norm2, norm_f):
    lg = jnp.broadcast_to(jnp.concatenate([lf.reshape(NH), lb.reshape(NH)])[:, None], (2 * NH, CH))
    return jnp.concatenate([_rows(b_mod), _rows(norm1), _rows(sg_gain), _rows(sg_w), _rows(sg_b), lg,
                            _rows(norm2), _rows(norm_f)], axis=0)


def _unpack_small(p):
    return (p[Q_BMOD:Q_N1].reshape(1, 6 * D), p[Q_N1:Q_GAIN].reshape(1, D), p[Q_GAIN:Q_SGW].reshape(1, AW),
            p[Q_SGW:Q_SGB].reshape(1, NH, CH, CH), p[Q_SGB:Q_LG].reshape(1, NH, CH),
            p[Q_LG:Q_LG + NH, 0].reshape(1, NH), p[Q_LG + NH:Q_N2, 0].reshape(1, NH),
            p[Q_N2:Q_NF].reshape(1, D), p[Q_NF:Q_ROWS].reshape(D))


def _rope_tables(ln):
    pos = np.arange(ln)
    rows = (pos // GRID_W).astype(np.float32)
    cols = (pos % GRID_W).astype(np.float32)
    n_freq = HD // 4
    inv = (np.float32(ROPE_BASE) ** (-np.arange(n_freq, dtype=np.float32) / np.float32(n_freq))).astype(np.float32)
    ar = rows[:, None] * inv[None, :]
    ac = cols[:, None] * inv[None, :]
    cos_t = np.concatenate([np.cos(ar), np.cos(ar), np.cos(ac), np.cos(ac)], axis=1).astype(np.float32)
    sin_t = np.concatenate([-np.sin(ar), np.sin(ar), -np.sin(ac), np.sin(ac)], axis=1).astype(np.float32)
    return jnp.asarray(cos_t), jnp.asarray(sin_t)


def kernel(x, c, ctx, c_ctx, w_mod, b_mod, norm1, w_in, sg_gain, sg_w, sg_b, ret_logit_f, ret_logit_b, w_out, norm2, w_gate, w_up, w_down, norm_f, loss_target, m_c_ctx, m_w_mod, m_b_mod, m_norm1, m_w_in, m_sg_gain, m_sg_w, m_sg_b, m_ret_logit_f, m_ret_logit_b, m_w_out, m_norm2, m_w_gate, m_w_up, m_w_down, m_norm_f, v_c_ctx, v_w_mod, v_b_mod, v_norm1, v_w_in, v_sg_gain, v_sg_w, v_sg_b, v_ret_logit_f, v_ret_logit_b, v_w_out, v_norm2, v_w_gate, v_w_up, v_w_down, v_norm_f):
    ln = x.shape[1]
    xi, yi, ci = _pos()
    chip = 2 * xi + yi
    me = 4 * xi + 2 * yi + ci
    x2d = x.reshape(ln, D)
    tgt = loss_target.reshape(ln, D)
    mod_c = w_mod.shape[2]

    wi, wo, wg, wu, wd = _gather_weights([w_in[0], w_out[0], w_gate[0], w_up[0], w_down[0]])
    wo = wo.reshape(D, D)

    c_all = _allgather_small(c, "gather_c").reshape(NDEV, D)
    ct_pad = jnp.concatenate([c_all, c_ctx.reshape(1, D), jnp.zeros((7, D), F32)], axis=0)
    prod = _mod_forward(ct_pad, w_mod[0])
    prod_all = _allgather_small(prod, "gather_mod")
    prod_chips = prod_all[0::2]
    mod_rows = jnp.transpose(prod_chips, (1, 0, 2)).reshape(16, NCHIP * mod_c) + b_mod
    mod = lax.dynamic_slice_in_dim(mod_rows, me, 1, axis=0)
    cmod = mod_rows[8:9]
    sh1, sc1, g1, sh2, sc2, g2 = [mod[:, i * D:(i + 1) * D] for i in range(6)]
    csh1, csc1 = cmod[:, 0:D], cmod[:, D:2 * D]
    zrow = jnp.zeros((1, D), F32)
    vec_in = jnp.concatenate([norm1, sh1, sc1] + [zrow] * 5, axis=0)
    vec_ctx = jnp.concatenate([norm1, csh1, csc1] + [zrow] * 5, axis=0)
    vec_post = jnp.concatenate([g1, norm2, sh2, sc2, g2, norm_f.reshape(1, D), zrow, zrow], axis=0)

    logits = jnp.concatenate([ret_logit_f.reshape(NH), ret_logit_b.reshape(NH)])
    dm = _decay_mats(jnp.broadcast_to(logits[:, None, None], (2 * NH, CH, CH)))
    ctx2d = ctx.reshape(ctx.shape[1], D)
    scf, scb = _ctx_forward(ctx2d, vec_ctx, wi, dm)

    cos_t, sin_t = _rope_tables(ln)
    z, hx = _in_proj(x2d, vec_in, wi)
    bfull = jnp.broadcast_to(sg_b[0][:, :, None], (NH, CH, CH))
    ycat, sf_all, sb_all = _mixer_fwd(z, cos_t, sin_t, dm, sg_w[0], sg_gain, bfull, scf, scb)

    dx1, dycat, h2, dy, df, act, da, db, acc_post = _post_mixer(x2d, ycat, tgt, vec_post, wo, wg, wu, wd)

    dz, ddm, dsgw, dgain, dbf, dscf, dscb = _mixer_bwd(z, dycat, cos_t, sin_t, dm, sg_w[0], sg_gain, bfull,
                                                       sf_all, sb_all)
    gwkv, acc_ctx, ddm_ctx = _ctx_backward(ctx2d, vec_ctx, wi, dm, dscf, dscb)
    gx, acc_in = _in_proj_bwd(dz, x2d, dx1, vec_in, wi)

    g_wi = _tn_matmul(hx, dz, "grad_w_in", NCHIP, D, WI_C, False, "cols", 512)
    g_wi = _add_ctx_cols(g_wi, gwkv)
    g_wo = _tn_matmul(ycat, dy, "grad_w_out", 1, D, D, False, "shared", 1024).reshape(NCHIP, WO_R, D)
    g_wg = _tn_matmul(h2, da, "grad_w_gate", NCHIP, D, FF_C, False, "batched", 1024)
    g_wu = _tn_matmul(h2, db, "grad_w_up", NCHIP, D, FF_C, False, "batched", 1024)
    g_wd = _tn_matmul(act, df, "grad_w_down", NCHIP, FF_C, D, True, "shared", 1024)

    full = [g_wi, g_wo, g_wg, g_wu, g_wd]
    names = ["w_in", "w_out", "w_gate", "w_up", "w_down"]
    full = [g.reshape(NCHIP, 2, g.shape[1] // 2, g.shape[2]) for g in full]
    cidx = ci.reshape(1).astype(jnp.int32)
    chipidx = chip.reshape(1).astype(jnp.int32)
    from_sib = _rs_exchange_halves(full)
    sums = [_rs_add_halves(g, r, cidx, "rs_add_halves_" + nm) for g, r, nm in zip(full, from_sib, names)]
    from_chips = _rs_send_chips([s[1] for s in sums])
    finals = [_rs_add_chips(s[0], r, chipidx, "rs_add_chips_" + nm) for s, r, nm in zip(sums, from_chips, names)]
    others = _rs_share_final(finals)

    lg_part, dsgb = _small_reduce(ddm, ddm_ctx, dm, dbf)
    dmod = jnp.concatenate([acc_in[1:2], acc_in[2:3], acc_post[0:1], acc_post[2:3], acc_post[3:4],
                            acc_post[4:5]], axis=1)
    dcmod = jnp.concatenate([acc_ctx[1:2], acc_ctx[2:3], jnp.zeros((1, 4 * D), F32)], axis=1)
    packed = jnp.concatenate([
        _rows(dcmod), _rows(dmod), _rows(acc_in[0:1] + acc_ctx[0:1]), _rows(dgain), _rows(dsgw), dsgb, lg_part,
        _rows(acc_post[1:2]), _rows(acc_post[5:6]), _rows(acc_post[6:7])], axis=0)
    gathered = _allgather_small(packed, "gather_small")
    dmod_all = gathered[:, P_DMOD:P_N1].reshape(NDEV, 6 * D)
    dcmod_all = gathered[:, P_DCMOD:P_DMOD].reshape(NDEV, 6 * D)
    dmod_cols = lax.dynamic_slice_in_dim(dmod_all, chip * mod_c, mod_c, axis=1)
    dcmod_cols = lax.dynamic_slice_in_dim(dcmod_all, chip * mod_c, mod_c, axis=1)
    dmod_pad = jnp.concatenate([dmod_cols, jnp.zeros((CH - NDEV, mod_c), F32)], axis=0)
    ct_pad_t = jnp.concatenate([jnp.transpose(c_all), jnp.zeros((D, CH - NDEV), F32)], axis=1)
    g_wmod, cctx_part = _mod_backward(ct_pad_t, c_ctx.reshape(D, 1), dmod_pad, dcmod_cols, w_mod[0])
    parts = _allgather_small(cctx_part[0:1], "gather_cctx")
    g_cctx, d_cctx, nm_cctx, nv_cctx = _cctx_update(parts, c_ctx.reshape(1, D), m_c_ctx.reshape(1, D),
                                                    v_c_ctx.reshape(1, D))

    wp = _pack_small(b_mod, norm1, sg_gain, sg_w, sg_b, ret_logit_f, ret_logit_b, norm2, norm_f)
    mp = _pack_small(m_b_mod, m_norm1, m_sg_gain, m_sg_w, m_sg_b, m_ret_logit_f, m_ret_logit_b, m_norm2, m_norm_f)
    vp = _pack_small(v_b_mod, v_norm1, v_sg_gain, v_sg_w, v_sg_b, v_ret_logit_f, v_ret_logit_b, v_norm2, v_norm_f)
    gp, dp, mp2, vp2, loss_t = _small_update(gathered, wp, mp, vp)

    big_w = [w_in[0], w_out[0], w_gate[0], w_up[0], w_down[0]]
    big_m = [m_w_in[0], m_w_out[0], m_w_gate[0], m_w_up[0], m_w_down[0]]
    big_v = [v_w_in[0], v_w_out[0], v_w_gate[0], v_w_up[0], v_w_down[0]]
    upd = [_adamw_halves(w, own, oth, m, v, cidx, "adamw_" + nm) for w, own, oth, m, v, nm in
           zip(big_w, finals, others, big_m, big_v, names)]
    big_g = [g_wmod] + [u[0] for u in upd]
    big = [_adamw(w_mod[0], g_wmod, m_w_mod[0], v_w_mod[0], "adamw_w_mod")] + [u[1:] for u in upd]

    def assemble(small, cctx, bigs):
        b_mod_, norm1_, gain_, sgw_, sgb_, lf_, lb_, norm2_, normf_ = _unpack_small(small)
        wm, wi_, wo_, wg_, wu_, wd_ = [b[None] for b in bigs]
        return [cctx.reshape(D), wm, b_mod_, norm1_, wi_, gain_, sgw_, sgb_, lf_, lb_, wo_, norm2_, wg_, wu_, wd_,
                normf_]

    out = [loss_t[0, 0], gx.reshape(1, ln, D)]
    out += assemble(gp, g_cctx, big_g)
    out += assemble(dp, d_cctx, [b[0] for b in big])
    out += assemble(mp2, nm_cctx, [b[1] for b in big])
    out += assemble(vp2, nv_cctx, [b[2] for b in big])
    return tuple(out)
```

```python
import functools

import jax
import jax.numpy as jnp
import numpy as np
from jax import lax
from jax.experimental import pallas as pl
from jax.experimental.pallas import tpu as pltpu

F32 = jnp.float32
BF = jnp.bfloat16
MESH = pl.DeviceIdType.MESH

D = 1024
CH = 128
HD = 128
NH = 4
AW = 512
IN_COLS = 3584
DFF = 2816
NCHIP = 4
NDEV = 8
WI_C = IN_COLS // NCHIP
FF_C = DFF // NCHIP
WO_R = D // NCHIP
EPS = 1e-6
GRID_W = 64
ROPE_BASE = 10000.0
K_SCALE = HD ** -0.5
LR, B1, B2, AEPS, WD, STEP = 0.001, 0.9, 0.999, 1e-08, 0.01, 10
VMEM_MB = 1 << 20
HI = lax.Precision.HIGHEST

P_DCMOD, P_DMOD, P_N1, P_GAIN, P_SGW, P_SGB, P_LG, P_N2, P_NF, P_LOSS = 0, 48, 96, 104, 108, 620, 624, 632, 640, 648
P_ROWS = 656
Q_BMOD, Q_N1, Q_GAIN, Q_SGW, Q_SGB, Q_LG, Q_N2, Q_NF = 0, 48, 56, 60, 572, 576, 584, 592
Q_ROWS = 600


def _params(vmem_mb, sem=None):
    return pltpu.CompilerParams(vmem_limit_bytes=vmem_mb * VMEM_MB, dimension_semantics=sem)


def _const(shape):
    nd = len(shape)
    return pl.BlockSpec(shape, lambda *_: (0,) * nd, pipeline_mode=pl.Buffered(1))


def _pos():
    return lax.axis_index("x"), lax.axis_index("y"), lax.axis_index("c")


def _dot(a, b, dims):
    return lax.dot_general(a, b, (dims, ((), ())), preferred_element_type=F32)


NN = ((1,), (0,))
NT = ((1,), (1,))
TN = ((0,), (0,))


@jax.custom_vjp
def _mm(a, b):
    return _dot(a.astype(BF), b.astype(BF), NN)


def _mm_f(a, b):
    return _mm(a, b), (a.astype(BF), b.astype(BF))


def _mm_b(res, g):
    a, b = res
    gb = g.astype(BF)
    return _dot(gb, b, NT), _dot(a, gb, TN)


_mm.defvjp(_mm_f, _mm_b)


@jax.custom_vjp
def _mm_nt(a, b):
    return _dot(a.astype(BF), b.astype(BF), NT)


def _mm_nt_f(a, b):
    return _mm_nt(a, b), (a.astype(BF), b.astype(BF))


def _mm_nt_b(res, g):
    a, b = res
    gb = g.astype(BF)
    return _dot(gb, b, NN), _dot(gb, a, TN)


_mm_nt.defvjp(_mm_nt_f, _mm_nt_b)


@jax.custom_vjp
def _mm_tn(a, b):
    return _dot(a.astype(BF), b.astype(BF), TN)


def _mm_tn_f(a, b):
    return _mm_tn(a, b), (a.astype(BF), b.astype(BF))


def _mm_tn_b(res, g):
    a, b = res
    gb = g.astype(BF)
    return _dot(b, gb, NT), _dot(a, gb, NN)


_mm_tn.defvjp(_mm_tn_f, _mm_tn_b)


def _gelu(x):
    return x * (0.5 * (1.0 + jnp.tanh(0.7978845608028654 * (x + 0.044715 * (x * x * x)))))


def _silu(x):
    return x * jax.nn.sigmoid(x)


def _rms(x):
    return lax.rsqrt(jnp.mean(x * x, axis=-1, keepdims=True) + EPS)


def _swap32(t):
    lane = lax.broadcasted_iota(jnp.int32, t.shape, 1)
    first = (lane % 64) < 32
    return jnp.where(first, pltpu.roll(t, 96, 1), pltpu.roll(t, 32, 1))


def _rope(t, cos, sin):
    return t * cos + _swap32(t) * sin


def _rope_bwd(d, cos, sin):
    return d * cos + _swap32(d * sin)


def _heads(ref, off=0):
    return [ref[:, off + h * HD: off + (h + 1) * HD].astype(F32) for h in range(NH)]


def _chunk_fwd(u, v, q, k, vr, gf, gb, sf, sb, df, xf, zf, db, xb, zb, sgw, gain, bfull):
    ya, yr, uf, ub = [], [], [], []
    for g in range(NH):
        gu = _gelu(u[g])
        gv = _gelu(v[g])
        vn = gv * _rms(gv) * gain[g]
        ya.append(gu * (_mm(sgw[g], vn) + bfull[g]))
    for h in range(NH):
        a = _mm_nt(q[h], k[h])
        of = _mm(a * df[h], vr[h]) + xf[h] * _mm(q[h], sf[h])
        ob = _mm(a * db[h], vr[h]) + xb[h] * _mm(q[h], sb[h])
        yr.append(_silu(gf[h]) * (of * _rms(of)) + _silu(gb[h]) * (ob * _rms(ob)))
        uf.append(_mm_tn(k[h], zf[h] * vr[h]))
        ub.append(_mm_tn(k[h], zb[h] * vr[h]))
    return ya, yr, uf, ub


def _fwd_dir_only(sf, q, k, vr, gf, df, xf):
    out = []
    for h in range(NH):
        a = _mm_nt(q[h], k[h])
        of = _mm(a * df[h], vr[h]) + xf[h] * _mm(q[h], sf[h])
        out.append(_silu(gf[h]) * (of * _rms(of)))
    return out


def _ctx_states(ctx0, ctx1, n1, csh, csc, wk, wv, zf, zb, ef, eb):
    hc0 = (ctx0 * _rms(ctx0) * n1) * (1.0 + csc) + csh
    hc1 = (ctx1 * _rms(ctx1) * n1) * (1.0 + csc) + csh
    scf, scb = [], []
    for h in range(NH):
        k0, k1 = _mm(hc0, wk[h]) * K_SCALE, _mm(hc1, wk[h]) * K_SCALE
        v0, v1 = _mm(hc0, wv[h]), _mm(hc1, wv[h])
        scf.append(ef[h] * _mm_tn(k0, zf[h] * v0) + _mm_tn(k1, zf[h] * v1))
        scb.append(eb[h] * _mm_tn(k1, zb[h] * v1) + _mm_tn(k0, zb[h] * v0))
    return scf, scb


def _allgather_small(v, name):
    r, n = v.shape

    def body(v_ref, out_ref, send_sems, recv_sems, local_sem):
        x, y, c = _pos()
        me = 4 * x + 2 * y + c
        mine = pltpu.make_async_copy(v_ref, out_ref.at[me], local_sem)
        mine.start()
        sent = []
        for k in range(1, NDEV):
            kx, ky, kc = (k >> 2) & 1, (k >> 1) & 1, k & 1
            peer = (x ^ kx, y ^ ky, c ^ kc)
            cp = pltpu.make_async_remote_copy(src_ref=v_ref, dst_ref=out_ref.at[me], send_sem=send_sems.at[k - 1],
                                              recv_sem=recv_sems.at[k - 1], device_id=peer, device_id_type=MESH)
            cp.start()
            sent.append(cp)
        for k in range(1, NDEV):
            kx, ky, kc = (k >> 2) & 1, (k >> 1) & 1, k & 1
            peer = (x ^ kx, y ^ ky, c ^ kc)
            src = 4 * (x ^ kx) + 2 * (y ^ ky) + (c ^ kc)
            pltpu.make_async_remote_copy(src_ref=v_ref, dst_ref=out_ref.at[src], send_sem=send_sems.at[k - 1],
                                         recv_sem=recv_sems.at[k - 1], device_id=peer, device_id_type=MESH).wait_recv()
        for cp in sent:
            cp.wait_send()
        mine.wait()

    return pl.pallas_call(
        body, name=name,
        out_shape=jax.ShapeDtypeStruct((NDEV, r, n), F32),
        in_specs=[pl.BlockSpec(memory_space=pltpu.VMEM)],
        out_specs=pl.BlockSpec(memory_space=pltpu.VMEM),
        scratch_shapes=[pltpu.SemaphoreType.DMA((NDEV - 1,)), pltpu.SemaphoreType.DMA((NDEV - 1,)),
                        pltpu.SemaphoreType.DMA],
        compiler_params=_params(16),
    )(v)


def _gather_weights(shards):
    nt = len(shards)
    shapes = [s.shape for s in shards]

    def body(*refs):
        srcs, outs, stages = refs[:nt], refs[nt:2 * nt], refs[2 * nt:3 * nt]
        ici_send, ici_recv, d2d_send, d2d_recv, local_sems = refs[3 * nt:]
        x, y, c = _pos()
        chip = 2 * x + y
        for t in range(nt):
            half = shapes[t][0] // 2
            stages[t][0] = srcs[t][0:half, :].astype(BF)
            stages[t][1] = srcs[t][half:2 * half, :].astype(BF)
        local = []
        for t in range(nt):
            cp = pltpu.make_async_copy(stages[t], outs[t].at[chip], local_sems.at[t])
            cp.start()
            local.append(cp)
        sent = []
        for k in range(1, NCHIP):
            kx, ky = (k >> 1) & 1, k & 1
            for t in range(nt):
                s = (k - 1) * nt + t
                cp = pltpu.make_async_remote_copy(
                    src_ref=stages[t].at[c], dst_ref=outs[t].at[chip, c], send_sem=ici_send.at[s],
                    recv_sem=ici_recv.at[s], device_id=(x ^ kx, y ^ ky, c), device_id_type=MESH)
                cp.start()
                sent.append(cp)
        for k in range(1, NCHIP):
            kx, ky = (k >> 1) & 1, k & 1
            src_chip = 2 * (x ^ kx) + (y ^ ky)
            for t in range(nt):
                s = (k - 1) * nt + t
                pltpu.make_async_remote_copy(
                    src_ref=stages[t].at[c], dst_ref=outs[t].at[src_chip, c], send_sem=ici_send.at[s],
                    recv_sem=ici_recv.at[s], device_id=(x ^ kx, y ^ ky, c), device_id_type=MESH).wait_recv()
                cp = pltpu.make_async_remote_copy(
                    src_ref=outs[t].at[src_chip, c], dst_ref=outs[t].at[src_chip, c], send_sem=d2d_send.at[s],
                    recv_sem=d2d_recv.at[s], device_id=(x, y, 1 - c), device_id_type=MESH)
                cp.start()
                sent.append(cp)
        for k in range(1, NCHIP):
            kx, ky = (k >> 1) & 1, k & 1
            src_chip = 2 * (x ^ kx) + (y ^ ky)
            for t in range(nt):
                s = (k - 1) * nt + t
                pltpu.make_async_remote_copy(
                    src_ref=stages[t].at[c], dst_ref=outs[t].at[src_chip, 1 - c], send_sem=d2d_send.at[s],
                    recv_sem=d2d_recv.at[s], device_id=(x, y, 1 - c), device_id_type=MESH).wait_recv()
        for cp in sent:
            cp.wait_send()
        for cp in local:
            cp.wait()

    n_rem = (NCHIP - 1) * nt
    out = pl.pallas_call(
        body, name="gather_weights",
        out_shape=[jax.ShapeDtypeStruct((NCHIP, 2, r // 2, cc), BF) for r, cc in shapes],
        in_specs=[pl.BlockSpec(memory_space=pltpu.VMEM)] * nt,
        out_specs=[pl.BlockSpec(memory_space=pl.ANY)] * nt,
        scratch_shapes=[pltpu.VMEM((2, r // 2, cc), BF) for r, cc in shapes]
        + [pltpu.SemaphoreType.DMA((n_rem,))] * 4 + [pltpu.SemaphoreType.DMA((nt,))],
        compiler_params=_params(48),
    )(*shards)
    return [o.reshape(NCHIP, r, cc) for o, (r, cc) in zip(out, shapes)]


def _chip_offsets():
    return [((k >> 1) & 1, k & 1) for k in range(1, NCHIP)]


def _prologue(c, c_ctx, w_mod_s, shards):
    nt = len(shards)
    shapes = [s.shape for s in shards]
    mod_c = w_mod_s.shape[1]

    def body(*refs):
        c_ref, cc_ref, wm_ref = refs[:3]
        srcs = refs[3:3 + nt]
        outs = refs[3 + nt:3 + 2 * nt]
        call_ref, prod_ref = refs[3 + 2 * nt:5 + 2 * nt]
        stages = refs[5 + 2 * nt:5 + 3 * nt]
        ct = refs[5 + 3 * nt]
        c_send, c_recv, p_send, p_recv, ici_send, ici_recv, d2d_send, d2d_recv, local_sems = refs[6 + 3 * nt:]
        x, y, c = _pos()
        chip = 2 * x + y
        me = 4 * x + 2 * y + c
        sib = (x, y, 1 - c)
        pending = []
        for t in range(nt):
            half = shapes[t][0] // 2
            stages[t][0] = srcs[t][0:half, :].astype(BF)
            stages[t][1] = srcs[t][half:2 * half, :].astype(BF)
            cp = pltpu.make_async_copy(stages[t], outs[t].at[chip], local_sems.at[t])
            cp.start()
            pending.append(cp)
        sends = []
        for k, (kx, ky) in enumerate(_chip_offsets()):
            cp = pltpu.make_async_remote_copy(src_ref=stages[0].at[c], dst_ref=outs[0].at[chip, c],
                                              send_sem=ici_send.at[k], recv_sem=ici_recv.at[k],
                                              device_id=(x ^ kx, y ^ ky, c), device_id_type=MESH)
            cp.start()
            sends.append(cp)

        def to_all(src, dst_of, send_sems, recv_sems):
            for k in range(1, NDEV):
                kx, ky, kc = (k >> 2) & 1, (k >> 1) & 1, k & 1
                cp = pltpu.make_async_remote_copy(src_ref=src, dst_ref=dst_of(me), send_sem=send_sems.at[k - 1],
                                                  recv_sem=recv_sems.at[k - 1], device_id=(x ^ kx, y ^ ky, c ^ kc),
                                                  device_id_type=MESH)
                cp.start()
                sends.append(cp)
            for k in range(1, NDEV):
                kx, ky, kc = (k >> 2) & 1, (k >> 1) & 1, k & 1
                frm = 4 * (x ^ kx) + 2 * (y ^ ky) + (c ^ kc)
                pltpu.make_async_remote_copy(src_ref=src, dst_ref=dst_of(frm), send_sem=send_sems.at[k - 1],
                                             recv_sem=recv_sems.at[k - 1], device_id=(x ^ kx, y ^ ky, c ^ kc),
                                             device_id_type=MESH).wait_recv()

        call_ref[me] = c_ref[...]
        to_all(c_ref, lambda d: call_ref.at[d], c_send, c_recv)
        ct[...] = jnp.zeros_like(ct)
        for d in range(NDEV):
            ct[d:d + 1, :] = call_ref[d]
        ct[NDEV:NDEV + 1, :] = cc_ref[...]
        prod_ref[me] = jnp.dot(_silu(ct[...]), wm_ref[...], precision=HI, preferred_element_type=F32)
        to_all(prod_ref.at[me], lambda d: prod_ref.at[d], p_send, p_recv)

        for k, (kx, ky) in enumerate(_chip_offsets()):
            frm = 2 * (x ^ kx) + (y ^ ky)
            pltpu.make_async_remote_copy(src_ref=stages[0].at[c], dst_ref=outs[0].at[frm, c],
                                         send_sem=ici_send.at[k], recv_sem=ici_recv.at[k],
                                         device_id=(x ^ kx, y ^ ky, c), device_id_type=MESH).wait_recv()
            cp = pltpu.make_async_remote_copy(src_ref=outs[0].at[frm, c], dst_ref=outs[0].at[frm, c],
                                              send_sem=d2d_send.at[k], recv_sem=d2d_recv.at[k],
                                              device_id=sib, device_id_type=MESH)
            cp.start()
            sends.append(cp)
        for k, (kx, ky) in enumerate(_chip_offsets()):
            frm = 2 * (x ^ kx) + (y ^ ky)
            pltpu.make_async_remote_copy(src_ref=stages[0].at[c], dst_ref=outs[0].at[frm, 1 - c],
                                         send_sem=d2d_send.at[k], recv_sem=d2d_recv.at[k],
                                         device_id=sib, device_id_type=MESH).wait_recv()
        for cp in sends:
            cp.wait_send()
        for cp in pending:
            cp.wait()

    vm = pl.BlockSpec(memory_space=pltpu.VMEM)
    out = pl.pallas_call(
        body, name="prologue",
        out_shape=[jax.ShapeDtypeStruct((NCHIP, 2, r // 2, cc), BF) for r, cc in shapes]
        + [jax.ShapeDtypeStruct((NDEV, 1, D), F32), jax.ShapeDtypeStruct((NDEV, 16, mod_c), F32)],
        in_specs=[vm] * (3 + nt),
        out_specs=[pl.BlockSpec(memory_space=pl.ANY)] * nt + [vm, vm],
        scratch_shapes=[pltpu.VMEM((2, r // 2, cc), BF) for r, cc in shapes] + [pltpu.VMEM((16, D), F32)]
        + [pltpu.SemaphoreType.DMA((NDEV - 1,))] * 4 + [pltpu.SemaphoreType.DMA((NCHIP - 1,))] * 4
        + [pltpu.SemaphoreType.DMA((nt,))],
        compiler_params=_params(56),
    )(c, c_ctx, w_mod_s, *shards)
    return out[:nt], out[nt], out[nt + 1]


def _gather_ici_copies(bufs, send_sems, recv_sems):
    x, y, c = _pos()
    chip = 2 * x + y
    nt = len(bufs)
    out_cp, in_cp = [], []
    for k, (kx, ky) in enumerate(_chip_offsets()):
        frm = 2 * (x ^ kx) + (y ^ ky)
        for t in range(nt):
            s = k * nt + t
            peer = (x ^ kx, y ^ ky, c)
            out_cp.append(pltpu.make_async_remote_copy(
                src_ref=bufs[t].at[chip, c], dst_ref=bufs[t].at[chip, c], send_sem=send_sems.at[s],
                recv_sem=recv_sems.at[s], device_id=peer, device_id_type=MESH))
            in_cp.append(pltpu.make_async_remote_copy(
                src_ref=bufs[t].at[chip, c], dst_ref=bufs[t].at[frm, c], send_sem=send_sems.at[s],
                recv_sem=recv_sems.at[s], device_id=peer, device_id_type=MESH))
    return out_cp, in_cp


def _gather_d2d_copies(bufs, send_sems, recv_sems):
    x, y, c = _pos()
    nt = len(bufs)
    out_cp, in_cp = [], []
    for k, (kx, ky) in enumerate(_chip_offsets()):
        frm = 2 * (x ^ kx) + (y ^ ky)
        for t in range(nt):
            s = k * nt + t
            out_cp.append(pltpu.make_async_remote_copy(
                src_ref=bufs[t].at[frm, c], dst_ref=bufs[t].at[frm, c], send_sem=send_sems.at[s],
                recv_sem=recv_sems.at[s], device_id=(x, y, 1 - c), device_id_type=MESH))
            in_cp.append(pltpu.make_async_remote_copy(
                src_ref=bufs[t].at[frm, c], dst_ref=bufs[t].at[frm, 1 - c], send_sem=send_sems.at[s],
                recv_sem=recv_sems.at[s], device_id=(x, y, 1 - c), device_id_type=MESH))
    return out_cp, in_cp


def _scatter_ici_copies(parts, outs, send_sems, recv_sems):
    x, y, c = _pos()
    nt = len(parts)
    cps = []
    for k, (kx, ky) in enumerate(_chip_offsets()):
        dst_chip = 2 * (x ^ kx) + (y ^ ky)
        for t in range(nt):
            s = k * nt + t
            cps.append(pltpu.make_async_remote_copy(
                src_ref=parts[t].at[dst_chip], dst_ref=outs[t].at[k], send_sem=send_sems.at[s],
                recv_sem=recv_sems.at[s], device_id=(x ^ kx, y ^ ky, c), device_id_type=MESH))
    return cps


def _rs_exchange_halves(grads, name):
    nt = len(grads)
    shapes = [g.shape for g in grads]

    def body(*refs):
        gs, outs = refs[:nt], refs[nt:2 * nt]
        send_sems, recv_sems = refs[2 * nt:]
        x, y, c = _pos()
        sib = (x, y, 1 - c)
        sent = []
        for t in range(nt):
            for j in range(NCHIP):
                s = t * NCHIP + j
                cp = pltpu.make_async_remote_copy(src_ref=gs[t].at[j, 1 - c], dst_ref=outs[t].at[j],
                                                  send_sem=send_sems.at[s], recv_sem=recv_sems.at[s],
                                                  device_id=sib, device_id_type=MESH)
                cp.start()
                sent.append(cp)
        for cp in sent:
            cp.wait_recv()
        for cp in sent:
            cp.wait_send()

    return pl.pallas_call(
        body, name=name,
        out_shape=[jax.ShapeDtypeStruct((NCHIP, s[2], s[3]), F32) for s in shapes],
        in_specs=[pl.BlockSpec(memory_space=pl.ANY)] * nt,
        out_specs=[pl.BlockSpec(memory_space=pl.ANY)] * nt,
        scratch_shapes=[pltpu.SemaphoreType.DMA((nt * NCHIP,))] * 2,
    )(*grads)


def _rs_send_chips(parts):
    nt = len(parts)
    shapes = [p.shape for p in parts]

    def body(*refs):
        ps, outs = refs[:nt], refs[nt:2 * nt]
        send_sems, recv_sems = refs[2 * nt:]
        x, y, c = _pos()
        sent = []
        for k in range(1, NCHIP):
            kx, ky = (k >> 1) & 1, k & 1
            dst_chip = 2 * (x ^ kx) + (y ^ ky)
            for t in range(nt):
                s = (k - 1) * nt + t
                cp = pltpu.make_async_remote_copy(src_ref=ps[t].at[dst_chip], dst_ref=outs[t].at[k - 1],
                                                  send_sem=send_sems.at[s], recv_sem=recv_sems.at[s],
                                                  device_id=(x ^ kx, y ^ ky, c), device_id_type=MESH)
                cp.start()
                sent.append(cp)
        for cp in sent:
            cp.wait_recv()
        for cp in sent:
            cp.wait_send()

    return pl.pallas_call(
        body, name="rs_send_chips",
        out_shape=[jax.ShapeDtypeStruct((NCHIP - 1, s[1], s[2]), BF) for s in shapes],
        in_specs=[pl.BlockSpec(memory_space=pl.ANY)] * nt,
        out_specs=[pl.BlockSpec(memory_space=pl.ANY)] * nt,
        scratch_shapes=[pltpu.SemaphoreType.DMA((nt * (NCHIP - 1),))] * 2,
    )(*parts)


def _rs_share_final(finals):
    nt = len(finals)
    shapes = [f.shape for f in finals]

    def body(*refs):
        fs, outs = refs[:nt], refs[nt:2 * nt]
        send_sems, recv_sems = refs[2 * nt:]
        x, y, c = _pos()
        sent = []
        for t in range(nt):
            cp = pltpu.make_async_remote_copy(src_ref=fs[t], dst_ref=outs[t], send_sem=send_sems.at[t],
                                              recv_sem=recv_sems.at[t], device_id=(x, y, 1 - c), device_id_type=MESH)
            cp.start()
            sent.append(cp)
        for cp in sent:
            cp.wait_recv()
        for cp in sent:
            cp.wait_send()

    return pl.pallas_call(
        body, name="rs_share_final",
        out_shape=[jax.ShapeDtypeStruct(s, F32) for s in shapes],
        in_specs=[pl.BlockSpec(memory_space=pl.ANY)] * nt,
        out_specs=[pl.BlockSpec(memory_space=pl.ANY)] * nt,
        scratch_shapes=[pltpu.SemaphoreType.DMA((nt,))] * 2,
    )(*finals)


def _row_tile(h, cc=D):
    for t in (512, 384, 352, 256, 176, 128, 64, 32, 16):
        if h % t == 0 and t * cc * 4 <= (5 * VMEM_MB) // 4:
            return t
    return h


def _rs_add_halves(g, recv, cidx, name):
    _, _, h, cc = g.shape
    th = _row_tile(h, cc)

    def body(c_ref, g_ref, r_ref, of_ref, ob_ref):
        s = g_ref[...] + r_ref[...]
        of_ref[...] = s
        ob_ref[...] = s.astype(BF)

    return pl.pallas_call(
        body, name=name,
        grid_spec=pltpu.PrefetchScalarGridSpec(
            num_scalar_prefetch=1, grid=(NCHIP, h // th),
            in_specs=[pl.BlockSpec((None, None, th, cc), lambda j, i, c_ref: (j, c_ref[0], i, 0)),
                      pl.BlockSpec((None, th, cc), lambda j, i, c_ref: (j, i, 0))],
            out_specs=[pl.BlockSpec((None, th, cc), lambda j, i, c_ref: (j, i, 0)),
                       pl.BlockSpec((None, th, cc), lambda j, i, c_ref: (j, i, 0))]),
        out_shape=[jax.ShapeDtypeStruct((NCHIP, h, cc), F32), jax.ShapeDtypeStruct((NCHIP, h, cc), BF)],
        compiler_params=_params(48),
    )(cidx, g, recv)


def _rs_add_chips(own, recv, chipidx, name):
    _, h, cc = own.shape
    th = _row_tile(h, cc)

    def body(j_ref, o_ref, r_ref, out_ref):
        out_ref[...] = ((o_ref[...] + r_ref[0].astype(F32)) + r_ref[1].astype(F32)) + r_ref[2].astype(F32)

    return pl.pallas_call(
        body, name=name,
        grid_spec=pltpu.PrefetchScalarGridSpec(
            num_scalar_prefetch=1, grid=(h // th,),
            in_specs=[pl.BlockSpec((None, th, cc), lambda i, j_ref: (j_ref[0], i, 0)),
                      pl.BlockSpec((NCHIP - 1, th, cc), lambda i, j_ref: (0, i, 0))],
            out_specs=pl.BlockSpec((th, cc), lambda i, j_ref: (i, 0))),
        out_shape=jax.ShapeDtypeStruct((h, cc), F32),
        compiler_params=_params(48),
    )(chipidx, own, recv)


def _adamw_math(w, g, m, v):
    m2 = B1 * m + (1.0 - B1) * g
    v2 = B2 * v + (1.0 - B2) * (g * g)
    m_hat = m2 / (1.0 - B1 ** STEP)
    v_hat = v2 / (1.0 - B2 ** STEP)
    delta = -LR * (m_hat / (jnp.sqrt(v_hat) + AEPS) + WD * w)
    return delta, m2, v2


def _adamw(w, g, m, v, name):
    r, cc = w.shape
    tr = _row_tile(r, cc)

    def body(w_ref, g_ref, m_ref, v_ref, d_ref, mo_ref, vo_ref):
        d, m2, v2 = _adamw_math(w_ref[...], g_ref[...], m_ref[...], v_ref[...])
        d_ref[...] = d
        mo_ref[...] = m2
        vo_ref[...] = v2

    spec = pl.BlockSpec((tr, cc), lambda i: (i, 0))
    return pl.pallas_call(
        body, name=name, grid=(r // tr,), in_specs=[spec] * 4, out_specs=[spec] * 3,
        out_shape=[jax.ShapeDtypeStruct((r, cc), F32)] * 3,
        compiler_params=_params(48, ("parallel",)),
    )(w, g, m, v)


def _adamw_halves(w, own, other, m, v, cidx, name):
    r, cc = w.shape
    h = r // 2
    tr = _row_tile(h, cc)
    per = h // tr

    def body(c_ref, w_ref, own_ref, oth_ref, m_ref, v_ref, g_ref, d_ref, mo_ref, vo_ref):
        mine = (pl.program_id(0) // per) == c_ref[0]
        g = jnp.where(mine, own_ref[...], oth_ref[...])
        g_ref[...] = g
        d, m2, v2 = _adamw_math(w_ref[...], g, m_ref[...], v_ref[...])
        d_ref[...] = d
        mo_ref[...] = m2
        vo_ref[...] = v2

    full = pl.BlockSpec((tr, cc), lambda i, c_ref: (i, 0))
    half = pl.BlockSpec((tr, cc), lambda i, c_ref: (i % per, 0))
    return pl.pallas_call(
        body, name=name,
        grid_spec=pltpu.PrefetchScalarGridSpec(
            num_scalar_prefetch=1, grid=(r // tr,),
            in_specs=[full, half, half, full, full], out_specs=[full] * 4),
        out_shape=[jax.ShapeDtypeStruct((r, cc), F32)] * 4,
        compiler_params=_params(48, ("parallel",)),
    )(cidx, w, own, other, m, v)


def _mod_forward(ct_pad, w_mod_s):
    def body(c_ref, w_ref, o_ref):
        o_ref[...] = jnp.dot(_silu(c_ref[...]), w_ref[...], precision=HI, preferred_element_type=F32)

    return pl.pallas_call(
        body, name="mod_forward",
        out_shape=jax.ShapeDtypeStruct((16, w_mod_s.shape[1]), F32),
        in_specs=[pl.BlockSpec(memory_space=pltpu.VMEM)] * 2,
        out_specs=pl.BlockSpec(memory_space=pltpu.VMEM),
        compiler_params=_params(32),
    )(ct_pad, w_mod_s)


def _decay_exponents():
    ri = lax.broadcasted_iota(jnp.int32, (CH, CH), 0).astype(F32)
    ci = lax.broadcasted_iota(jnp.int32, (CH, CH), 1).astype(F32)
    full = jnp.full((CH, CH), float(CH), F32)
    return [[ri - ci, ri + 1.0, (CH - 1.0) - ri, full], [ci - ri, CH - ri, ri, full]]


def _decay_mats(logit_full):
    def body(l_ref, o_ref):
        ex = _decay_exponents()
        for d in range(2):
            for h in range(NH):
                lv = l_ref[d * NH + h]
                lg = jnp.minimum(lv, 0.0) - jnp.log(1.0 + jnp.exp(-jnp.abs(lv)))
                for kind in range(4):
                    m = jnp.exp(lg * ex[d][kind])
                    if kind == 0:
                        m = jnp.where(ex[d][0] >= 0.0, jnp.exp(lg * jnp.maximum(ex[d][0], 0.0)), 0.0)
                    o_ref[d, kind, h] = m

    return pl.pallas_call(
        body, name="decay_mats",
        out_shape=jax.ShapeDtypeStruct((2, 4, NH, CH, CH), F32),
        in_specs=[pl.BlockSpec(memory_space=pltpu.VMEM)],
        out_specs=pl.BlockSpec(memory_space=pltpu.VMEM),
        compiler_params=_params(32),
    )(logit_full)


def _ctx_kv_weights(wi_ref):
    def cols(g):
        return wi_ref[g // WI_C, :, g % WI_C: g % WI_C + HD].astype(F32)

    wk = [cols(3 * AW + h * HD) for h in range(NH)]
    wv = [cols(4 * AW + h * HD) for h in range(NH)]
    return wk, wv


def _ctx_forward(ctx, vecs, wi, dm):
    def body(ctx_ref, v_ref, wi_ref, dm_ref, scf_ref, scb_ref):
        wk, wv = _ctx_kv_weights(wi_ref)
        mats = [[dm_ref[d, kind, h] for h in range(NH)] for d in range(2) for kind in (2, 3)]
        scf, scb = _ctx_states(ctx_ref[0:CH, :], ctx_ref[CH:2 * CH, :], v_ref[0:1, :], v_ref[1:2, :],
                               v_ref[2:3, :], wk, wv, mats[0], mats[2], mats[1], mats[3])
        for h in range(NH):
            scf_ref[h] = scf[h]
            scb_ref[h] = scb[h]

    return pl.pallas_call(
        body, name="ctx_forward",
        out_shape=[jax.ShapeDtypeStruct((NH, HD, HD), F32)] * 2,
        in_specs=[pl.BlockSpec(memory_space=pltpu.VMEM)] * 4,
        out_specs=[pl.BlockSpec(memory_space=pltpu.VMEM)] * 2,
        compiler_params=_params(48),
    )(ctx, vecs, wi, dm)


def _ctx_backward(ctx, vecs, wi, dm, dscf, dscb):
    def body(ctx_ref, v_ref, wi_ref, dm_ref, gf_ref, gb_ref, gw_ref, gv_ref, gdm_ref):
        wk, wv = _ctx_kv_weights(wi_ref)
        mats = [[dm_ref[d, kind, h] for h in range(NH)] for d in range(2) for kind in (2, 3)]
        ctx0, ctx1 = ctx_ref[0:CH, :], ctx_ref[CH:2 * CH, :]

        def fn(n1, csh, csc, wk_, wv_, zf, zb, ef, eb):
            return _ctx_states(ctx0, ctx1, n1, csh, csc, wk_, wv_, zf, zb, ef, eb)

        _, vjp = jax.vjp(fn, v_ref[0:1, :], v_ref[1:2, :], v_ref[2:3, :], wk, wv,
                         mats[0], mats[2], mats[1], mats[3])
        cot = ([gf_ref[h] for h in range(NH)], [gb_ref[h] for h in range(NH)])
        dn1, dcsh, dcsc, dwk, dwv, dzf, dzb, def_, deb = vjp(cot)
        for h in range(NH):
            gw_ref[:, h * HD:(h + 1) * HD] = dwk[h]
            gw_ref[:, AW + h * HD:AW + (h + 1) * HD] = dwv[h]
        gv_ref[...] = jnp.zeros_like(gv_ref)
        gv_ref[0:1, :] = dn1
        gv_ref[1:2, :] = dcsh
        gv_ref[2:3, :] = dcsc
        for h in range(NH):
            gdm_ref[0, 0, h] = dzf[h]
            gdm_ref[0, 1, h] = def_[h]
            gdm_ref[1, 0, h] = dzb[h]
            gdm_ref[1, 1, h] = deb[h]

    return pl.pallas_call(
        body, name="ctx_backward",
        out_shape=[jax.ShapeDtypeStruct((D, 2 * AW), F32), jax.ShapeDtypeStruct((8, D), F32),
                   jax.ShapeDtypeStruct((2, 2, NH, CH, CH), F32)],
        in_specs=[pl.BlockSpec(memory_space=pltpu.VMEM)] * 6,
        out_specs=[pl.BlockSpec(memory_space=pltpu.VMEM)] * 3,
        compiler_params=_params(56),
    )(ctx, vecs, wi, dm, dscf, dscb)


def _in_proj(x, vecs, wi, gbufs):
    ln = x.shape[0]
    t = min(512, ln)
    nt = len(gbufs)
    steps = ln // t

    def body(x_ref, v_ref, wi_ref, *refs):
        z_ref, hx_ref = refs[nt:nt + 2]
        bufs = refs[nt + 2:2 * nt + 2]
        send_sems, recv_sems = refs[2 * nt + 2:]
        i = pl.program_id(0)

        @pl.when(i == 0)
        def _():
            for cp in _gather_ici_copies(bufs, send_sems, recv_sems)[0]:
                cp.start()

        xv = x_ref[...]
        hx = (xv * _rms(xv) * v_ref[0:1, :]) * (1.0 + v_ref[2:3, :]) + v_ref[1:2, :]
        hb = hx.astype(BF)
        hx_ref[...] = hb
        for j in range(NCHIP):
            z_ref[:, j * WI_C:(j + 1) * WI_C] = _dot(hb, wi_ref[j], NN)

        @pl.when(i == steps - 1)
        def _():
            out_cp, in_cp = _gather_ici_copies(bufs, send_sems, recv_sems)
            for cp in in_cp:
                cp.wait_recv()
            for cp in out_cp:
                cp.wait_send()

    hbm = pl.BlockSpec(memory_space=pl.ANY)
    out = pl.pallas_call(
        body, name="in_proj", grid=(steps,),
        in_specs=[pl.BlockSpec((t, D), lambda i: (i, 0)), _const((8, D)), _const((NCHIP, D, WI_C))] + [hbm] * nt,
        out_specs=[pl.BlockSpec((t, IN_COLS), lambda i: (i, 0)), pl.BlockSpec((t, D), lambda i: (i, 0))] + [hbm] * nt,
        out_shape=[jax.ShapeDtypeStruct((ln, IN_COLS), F32), jax.ShapeDtypeStruct((ln, D), BF)]
        + [jax.ShapeDtypeStruct(g.shape, g.dtype) for g in gbufs],
        input_output_aliases={3 + k: 2 + k for k in range(nt)},
        scratch_shapes=[pltpu.SemaphoreType.DMA(((NCHIP - 1) * nt,))] * 2,
        compiler_params=_params(56, ("arbitrary",)),
    )(x, vecs, wi, *gbufs)
    return out[0], out[1], out[2:]


def _in_proj_bwd(dz, x, dx1, vecs, wi, parts):
    ln = x.shape[0]
    t = min(512, ln)
    nt = len(parts)
    steps = ln // t

    def body(dz_ref, x_ref, dx1_ref, v_ref, wi_ref, *refs):
        ps = refs[:nt]
        gx_ref, acc_ref = refs[nt:nt + 2]
        got = refs[nt + 2:2 * nt + 2]
        send_sems, recv_sems = refs[2 * nt + 2:]

        @pl.when(pl.program_id(0) == 0)
        def _():
            acc_ref[...] = jnp.zeros_like(acc_ref)
            for cp in _scatter_ici_copies(ps, got, send_sems, recv_sems):
                cp.start()

        dhx = jnp.zeros((t, D), F32)
        for j in range(NCHIP):
            dhx = dhx + _dot(dz_ref[:, j * WI_C:(j + 1) * WI_C], wi_ref[j], NT)
        xv = x_ref[...]
        r = _rms(xv)
        xn = xv * r
        n1, sc = v_ref[0:1, :], v_ref[2:3, :]
        acc_ref[0:1, :] += jnp.sum(dhx * xn * (1.0 + sc), axis=0, keepdims=True)
        acc_ref[1:2, :] += jnp.sum(dhx, axis=0, keepdims=True)
        acc_ref[2:3, :] += jnp.sum(dhx * xn * n1, axis=0, keepdims=True)
        g = dhx * n1 * (1.0 + sc)
        gx_ref[...] = dx1_ref[...] + r * (g - xn * jnp.mean(g * xn, axis=-1, keepdims=True))

        @pl.when(pl.program_id(0) == steps - 1)
        def _():
            cps = _scatter_ici_copies(ps, got, send_sems, recv_sems)
            for cp in cps:
                cp.wait_recv()
            for cp in cps:
                cp.wait_send()

    hbm = pl.BlockSpec(memory_space=pl.ANY)
    out = pl.pallas_call(
        body, name="in_proj_bwd", grid=(steps,),
        in_specs=[pl.BlockSpec((t, IN_COLS), lambda i: (i, 0)), pl.BlockSpec((t, D), lambda i: (i, 0)),
                  pl.BlockSpec((t, D), lambda i: (i, 0)), _const((8, D)), _const((NCHIP, D, WI_C))] + [hbm] * nt,
        out_specs=[pl.BlockSpec((t, D), lambda i: (i, 0)), pl.BlockSpec((8, D), lambda i: (0, 0))] + [hbm] * nt,
        out_shape=[jax.ShapeDtypeStruct((ln, D), F32), jax.ShapeDtypeStruct((8, D), F32)]
        + [jax.ShapeDtypeStruct((NCHIP - 1,) + p.shape[1:], BF) for p in parts],
        scratch_shapes=[pltpu.SemaphoreType.DMA(((NCHIP - 1) * nt,))] * 2,
        compiler_params=_params(56, ("arbitrary",)),
    )(dz, x, dx1, vecs, wi, *parts)
    return out[0], out[1], out[2:]


def _post_mixer(x, ycat, tgt, vecs, wo, wg, wu, wd):
    ln = x.shape[0]
    t = min(256, ln)

    def body(x_ref, y_ref, t_ref, v_ref, wo_ref, wg_ref, wu_ref, wd_ref,
             dx1_ref, dyc_ref, h2_ref, dy_ref, df_ref, act_ref, da_ref, db_ref, acc_ref, a_st, b_st):
        @pl.when(pl.program_id(0) == 0)
        def _():
            acc_ref[...] = jnp.zeros_like(acc_ref)

        g1, n2, sh2, sc2 = v_ref[0:1, :], v_ref[1:2, :], v_ref[2:3, :], v_ref[3:4, :]
        g2, nf = v_ref[4:5, :], v_ref[5:6, :]
        y = _dot(y_ref[...], wo_ref[...], NN)
        x1 = x_ref[...] + g1 * y
        r2 = _rms(x1)
        xn2 = x1 * r2
        t2 = xn2 * n2
        h2b = (t2 * (1.0 + sc2) + sh2).astype(BF)
        h2_ref[...] = h2b
        f = jnp.zeros((t, D), F32)
        for j in range(NCHIP):
            a = _dot(h2b, wg_ref[j], NN)
            b = _dot(h2b, wu_ref[j], NN)
            a_st[j] = a
            b_st[j] = b
            act = (_silu(a) * b).astype(BF)
            act_ref[j] = act
            f = f + _dot(act, wd_ref[j], NN)
        x2 = x1 + g2 * f
        r3 = _rms(x2)
        xn3 = x2 * r3
        e = xn3 * nf - t_ref[...]
        acc_ref[6:7, :] += jnp.sum(e * e, axis=0, keepdims=True) * (0.5 / D)
        dout = e * (1.0 / D)
        acc_ref[5:6, :] += jnp.sum(dout * xn3, axis=0, keepdims=True)
        gg = dout * nf
        dx2 = r3 * (gg - xn3 * jnp.mean(gg * xn3, axis=-1, keepdims=True))
        acc_ref[4:5, :] += jnp.sum(dx2 * f, axis=0, keepdims=True)
        dfb = (g2 * dx2).astype(BF)
        df_ref[...] = dfb
        dh2 = jnp.zeros((t, D), F32)
        for j in range(NCHIP):
            dact = _dot(dfb, wd_ref[j], NT)
            a = a_st[j]
            b = b_st[j]
            s = jax.nn.sigmoid(a)
            da = (dact * b * (s * (1.0 + a * (1.0 - s)))).astype(BF)
            db = (dact * (a * s)).astype(BF)
            da_ref[j] = da
            db_ref[j] = db
            dh2 = dh2 + _dot(da, wg_ref[j], NT) + _dot(db, wu_ref[j], NT)
        acc_ref[2:3, :] += jnp.sum(dh2, axis=0, keepdims=True)
        acc_ref[3:4, :] += jnp.sum(dh2 * t2, axis=0, keepdims=True)
        acc_ref[1:2, :] += jnp.sum(dh2 * xn2 * (1.0 + sc2), axis=0, keepdims=True)
        gx = dh2 * n2 * (1.0 + sc2)
        dx1 = dx2 + r2 * (gx - xn2 * jnp.mean(gx * xn2, axis=-1, keepdims=True))
        dx1_ref[...] = dx1
        acc_ref[0:1, :] += jnp.sum(dx1 * y, axis=0, keepdims=True)
        dyb = (g1 * dx1).astype(BF)
        dy_ref[...] = dyb
        dyc_ref[...] = _dot(dyb, wo_ref[...], NT)

    tok = pl.BlockSpec((t, D), lambda i: (i, 0))
    ffb = pl.BlockSpec((NCHIP, t, FF_C), lambda i: (0, i, 0))
    return pl.pallas_call(
        body, name="post_mixer", grid=(ln // t,),
        in_specs=[tok, tok, tok, _const((8, D)), _const((D, D)), _const((NCHIP, D, FF_C)),
                  _const((NCHIP, D, FF_C)), _const((NCHIP, FF_C, D))],
        out_specs=[tok, tok, tok, tok, tok, ffb, ffb, ffb, pl.BlockSpec((16, D), lambda i: (0, 0))],
        out_shape=[jax.ShapeDtypeStruct((ln, D), F32)] * 2 + [jax.ShapeDtypeStruct((ln, D), BF)] * 3
        + [jax.ShapeDtypeStruct((NCHIP, ln, FF_C), BF)] * 3 + [jax.ShapeDtypeStruct((16, D), F32)],
        scratch_shapes=[pltpu.VMEM((NCHIP, t, FF_C), F32)] * 2,
        compiler_params=_params(60, ("arbitrary",)),
    )(x, ycat, tgt, vecs, wo, wg, wu, wd)


def _tn_matmul(xa, dy, name, nb, k1, n, x_batched, dy_mode, tt):
    ln = xa.shape[-2]
    tt = min(tt, ln)

    def body(x_ref, dy_ref, o_ref):
        @pl.when(pl.program_id(0) == 0)
        def _():
            o_ref[...] = jnp.zeros_like(o_ref)

        xt = None if x_batched else jnp.transpose(x_ref[...])
        for b in range(nb):
            lhs = jnp.transpose(x_ref[b]) if x_batched else xt
            if dy_mode == "batched":
                rhs = dy_ref[b]
            elif dy_mode == "cols":
                rhs = dy_ref[:, b * n:(b + 1) * n]
            else:
                rhs = dy_ref[...]
            o_ref[b] += _dot(lhs, rhs, NN)

    x_spec = (pl.BlockSpec((nb, tt, k1), lambda t: (0, t, 0)) if x_batched
              else pl.BlockSpec((tt, k1), lambda t: (t, 0)))
    if dy_mode == "batched":
        dy_spec = pl.BlockSpec((nb, tt, n), lambda t: (0, t, 0))
    elif dy_mode == "cols":
        dy_spec = pl.BlockSpec((tt, nb * n), lambda t: (t, 0))
    else:
        dy_spec = pl.BlockSpec((tt, n), lambda t: (t, 0))
    return pl.pallas_call(
        body, name=name, grid=(ln // tt,),
        in_specs=[x_spec, dy_spec],
        out_specs=pl.BlockSpec((nb, k1, n), lambda t: (0, 0, 0)),
        out_shape=jax.ShapeDtypeStruct((nb, k1, n), F32),
        compiler_params=_params(60, ("arbitrary",)),
    )(xa, dy)


def _add_ctx_cols(gwi, gwkv):
    first = 1536 // HD
    per = WI_C // HD

    def body(g_ref, a_ref, o_ref):
        o_ref[...] = g_ref[...] + a_ref[...]

    spec = pl.BlockSpec((None, D, HD), lambda i: ((first + i) // per, 0, (first + i) % per))
    return pl.pallas_call(
        body, name="add_ctx_cols", grid=(2 * AW // HD,),
        in_specs=[spec, pl.BlockSpec((D, HD), lambda i: (0, i))],
        out_specs=spec,
        out_shape=jax.ShapeDtypeStruct(gwi.shape, F32),
        input_output_aliases={0: 0},
        compiler_params=_params(32, ("arbitrary",)),
    )(gwi, gwkv)


def _mixer_fwd(z, cos_t, sin_t, dm, sgw, gain, bfull, scf, scb, gbufs):
    ln = z.shape[0]
    nc = ln // CH
    nt = len(gbufs)

    def rev(p, n):
        return p * n + (1 - p) * (nc - 1 - n)

    def col(j, both):
        if both:
            return pl.BlockSpec((CH, AW), lambda p, n: (rev(p, n), j))
        return pl.BlockSpec((CH, AW), lambda p, n: (p * n, j))

    def body(u_ref, v_ref, q_ref, k_ref, vr_ref, gf_ref, gb_ref, cos_ref, sin_ref, dm_ref, sgw_ref, gain_ref,
             bfull_ref, scf_ref, scb_ref, *refs):
        y_ref, sf_ref, sb_ref = refs[nt:nt + 3]
        bufs = refs[nt + 3:2 * nt + 3]
        sb_all, st, send_sems, recv_sems = refs[2 * nt + 3:]
        p, n = pl.program_id(0), pl.program_id(1)

        @pl.when((p == 0) & (n == 0))
        def _():
            for cp in _gather_d2d_copies(bufs, send_sems, recv_sems)[0]:
                cp.start()

        cos, sin = cos_ref[...], sin_ref[...]
        k = [_rope(t, cos, sin) * K_SCALE for t in _heads(k_ref)]
        vr = _heads(vr_ref)

        @pl.when(p == 0)
        def _():
            @pl.when(n == 0)
            def _():
                st[...] = scb_ref[...]

            m = nc - 1 - n
            for h in range(NH):
                sb_all[m, h] = st[h]
                st[h] = dm_ref[1, 3, h] * st[h] + _mm_tn(k[h], dm_ref[1, 2, h] * vr[h])

        @pl.when(p == 1)
        def _():
            @pl.when(n == 0)
            def _():
                st[...] = scf_ref[...]

            q = [_rope(t, cos, sin) for t in _heads(q_ref)]
            sf = [st[h] for h in range(NH)]
            sb = [sb_all[n, h].astype(F32) for h in range(NH)]
            mats = [[dm_ref[d, kind, h] for h in range(NH)] for d in range(2) for kind in range(3)]
            ya, yr, uf, _ = _chunk_fwd(
                _heads(u_ref), _heads(v_ref), q, k, vr, _heads(gf_ref), _heads(gb_ref), sf, sb,
                mats[0], mats[1], mats[2], mats[3], mats[4], mats[5],
                [sgw_ref[g] for g in range(NH)], [gain_ref[:, g * HD:(g + 1) * HD] for g in range(NH)],
                [bfull_ref[g] for g in range(NH)])
            for h in range(NH):
                y_ref[:, h * HD:(h + 1) * HD] = ya[h].astype(BF)
                y_ref[:, AW + h * HD:AW + (h + 1) * HD] = yr[h].astype(BF)
                sf_ref[0, h] = sf[h]
                sb_ref[0, h] = sb_all[n, h]
                st[h] = dm_ref[0, 3, h] * st[h] + uf[h]

        @pl.when((p == 1) & (n == nc - 1))
        def _():
            out_cp, in_cp = _gather_d2d_copies(bufs, send_sems, recv_sems)
            for cp in in_cp:
                cp.wait_recv()
            for cp in out_cp:
                cp.wait_send()

    hbm = pl.BlockSpec(memory_space=pl.ANY)
    tab = pl.BlockSpec((CH, HD), lambda p, n: (rev(p, n), 0))
    st_spec = pl.BlockSpec((1, NH, HD, HD), lambda p, n: (p * n, 0, 0, 0))
    out = pl.pallas_call(
        body, name="mixer_fwd", grid=(2, nc),
        in_specs=[col(0, False), col(1, False), col(2, False), col(3, True), col(4, True), col(5, False),
                  col(6, False), tab, tab, _const((2, 4, NH, CH, CH)), _const((NH, CH, CH)), _const((1, AW)),
                  _const((NH, CH, CH)), _const((NH, HD, HD)), _const((NH, HD, HD))] + [hbm] * nt,
        out_specs=[pl.BlockSpec((CH, D), lambda p, n: (p * n, 0)), st_spec, st_spec] + [hbm] * nt,
        out_shape=[jax.ShapeDtypeStruct((ln, D), BF), jax.ShapeDtypeStruct((nc, NH, HD, HD), F32),
                   jax.ShapeDtypeStruct((nc, NH, HD, HD), F32)]
        + [jax.ShapeDtypeStruct(g.shape, g.dtype) for g in gbufs],
        input_output_aliases={15 + k: 3 + k for k in range(nt)},
        scratch_shapes=[pltpu.VMEM((nc, NH, HD, HD), F32), pltpu.VMEM((NH, HD, HD), F32)]
        + [pltpu.SemaphoreType.DMA(((NCHIP - 1) * nt,))] * 2,
        compiler_params=_params(56, ("arbitrary", "arbitrary")),
    )(z, z, z, z, z, z, z, cos_t, sin_t, dm, sgw, gain, bfull, scf, scb, *gbufs)
    return out[0], out[1], out[2], out[3:]


def _mixer_bwd(z, dycat, cos_t, sin_t, dm, sgw, gain, bfull, sf_all, sb_all, parts):
    ln = z.shape[0]
    nc = ln // CH

    def rev(p, n):
        return p * n + (1 - p) * (nc - 1 - n)

    def col(j, both):
        if both:
            return pl.BlockSpec((CH, AW), lambda p, n: (rev(p, n), j))
        return pl.BlockSpec((CH, AW), lambda p, n: (p * n, j))

    nt = len(parts)

    def body(u_ref, v_ref, q_ref, k_ref, vr_ref, gf_ref, gb_ref, dya_ref, dyr_ref, cos_ref, sin_ref, dm_ref,
             sgw_ref, gain_ref, bfull_ref, sf_ref, sb_ref, *refs):
        ps = refs[:nt]
        dz_ref, ddm_ref, dsgw_ref, dgain_ref, dbf_ref, dscf_ref, dscb_ref = refs[nt:nt + 7]
        got = refs[nt + 7:2 * nt + 7]
        gf_all, run, send_sems, recv_sems = refs[2 * nt + 7:]
        p, n = pl.program_id(0), pl.program_id(1)

        @pl.when((p == 0) & (n == 0))
        def _():
            for cp in _scatter_ici_copies(ps, got, send_sems, recv_sems):
                cp.start()

        cos, sin = cos_ref[...], sin_ref[...]
        q = [_rope(t, cos, sin) for t in _heads(q_ref)]
        k = [_rope(t, cos, sin) * K_SCALE for t in _heads(k_ref)]
        vr = _heads(vr_ref)
        gf = _heads(gf_ref)
        dyr = _heads(dyr_ref)
        sf = [sf_ref[0, h].astype(F32) for h in range(NH)]
        mats = [[dm_ref[d, kind, h] for h in range(NH)] for d in range(2) for kind in range(3)]

        @pl.when(p == 0)
        def _():
            @pl.when(n == 0)
            def _():
                run[...] = jnp.zeros_like(run)
                ddm_ref[...] = jnp.zeros_like(ddm_ref)
                dsgw_ref[...] = jnp.zeros_like(dsgw_ref)
                dgain_ref[...] = jnp.zeros_like(dgain_ref)
                dbf_ref[...] = jnp.zeros_like(dbf_ref)

            m = nc - 1 - n
            _, vjp = jax.vjp(lambda s: _fwd_dir_only(s, q, k, vr, gf, mats[0], mats[1]), sf)
            (dsf,) = vjp(dyr)
            for h in range(NH):
                gf_all[m, h] = run[h]
                run[h] = dsf[h] + dm_ref[0, 3, h] * run[h]

            @pl.when(n == nc - 1)
            def _():
                dscf_ref[...] = run[...]

        @pl.when(p == 1)
        def _():
            @pl.when(n == 0)
            def _():
                run[...] = jnp.zeros_like(run)

            sb = [sb_ref[0, h].astype(F32) for h in range(NH)]
            g_f = [gf_all[n, h].astype(F32) for h in range(NH)]
            g_b = [run[h] for h in range(NH)]
            args = (_heads(u_ref), _heads(v_ref), q, k, vr, gf, _heads(gb_ref), sb,
                    mats[0], mats[1], mats[2], mats[3], mats[4], mats[5],
                    [sgw_ref[g] for g in range(NH)], [gain_ref[:, g * HD:(g + 1) * HD] for g in range(NH)],
                    [bfull_ref[g] for g in range(NH)])

            def fn(u_, v_, q_, k_, vr_, gf_, gb_, sb_, df, xf, zf, db, xb, zb, sgw_, gain_, bfull_):
                return _chunk_fwd(u_, v_, q_, k_, vr_, gf_, gb_, sf, sb_, df, xf, zf, db, xb, zb, sgw_, gain_,
                                  bfull_)

            _, vjp = jax.vjp(fn, *args)
            (du, dv, dq, dk, dvr, dgf, dgb, dsb, ddf, dxf, dzf, ddb, dxb, dzb, dsgw, dgain, dbf) = vjp(
                (_heads(dya_ref), dyr, g_f, g_b))
            for h in range(NH):
                s = slice(h * HD, (h + 1) * HD)
                dz_ref[:, h * HD:(h + 1) * HD] = du[h].astype(BF)
                dz_ref[:, AW + h * HD:AW + (h + 1) * HD] = dv[h].astype(BF)
                dz_ref[:, 2 * AW + h * HD:2 * AW + (h + 1) * HD] = _rope_bwd(dq[h], cos, sin).astype(BF)
                dz_ref[:, 3 * AW + h * HD:3 * AW + (h + 1) * HD] = _rope_bwd(dk[h] * K_SCALE, cos, sin).astype(BF)
                dz_ref[:, 4 * AW + h * HD:4 * AW + (h + 1) * HD] = dvr[h].astype(BF)
                dz_ref[:, 5 * AW + h * HD:5 * AW + (h + 1) * HD] = dgf[h].astype(BF)
                dz_ref[:, 6 * AW + h * HD:6 * AW + (h + 1) * HD] = dgb[h].astype(BF)
                ddm_ref[0, 0, h] += ddf[h]
                ddm_ref[0, 1, h] += dxf[h]
                ddm_ref[0, 2, h] += dzf[h]
                ddm_ref[0, 3, h] += sf[h] * g_f[h]
                ddm_ref[1, 0, h] += ddb[h]
                ddm_ref[1, 1, h] += dxb[h]
                ddm_ref[1, 2, h] += dzb[h]
                ddm_ref[1, 3, h] += sb[h] * g_b[h]
                dsgw_ref[h] += dsgw[h]
                dgain_ref[:, s] += dgain[h]
                dbf_ref[h] += dbf[h]
                run[h] = dsb[h] + dm_ref[1, 3, h] * run[h]

            @pl.when(n == nc - 1)
            def _():
                dscb_ref[...] = run[...]

        @pl.when((p == 1) & (n == nc - 1))
        def _():
            cps = _scatter_ici_copies(ps, got, send_sems, recv_sems)
            for cp in cps:
                cp.wait_recv()
            for cp in cps:
                cp.wait_send()

    hbm = pl.BlockSpec(memory_space=pl.ANY)
    tab = pl.BlockSpec((CH, HD), lambda p, n: (rev(p, n), 0))
    tile4 = jax.ShapeDtypeStruct((NH, CH, CH), F32)
    out = pl.pallas_call(
        body, name="mixer_bwd", grid=(2, nc),
        in_specs=[col(0, False), col(1, False), col(2, True), col(3, True), col(4, True), col(5, True),
                  col(6, False),
                  pl.BlockSpec((CH, AW), lambda p, n: (p * n, 0)), pl.BlockSpec((CH, AW), lambda p, n: (rev(p, n), 1)),
                  tab, tab, _const((2, 4, NH, CH, CH)), _const((NH, CH, CH)), _const((1, AW)),
                  _const((NH, CH, CH)),
                  pl.BlockSpec((1, NH, HD, HD), lambda p, n: (rev(p, n), 0, 0, 0)),
                  pl.BlockSpec((1, NH, HD, HD), lambda p, n: (p * n, 0, 0, 0))] + [hbm] * nt,
        out_specs=[pl.BlockSpec((CH, IN_COLS), lambda p, n: (p * n, 0)),
                   pl.BlockSpec((2, 4, NH, CH, CH), lambda p, n: (0, 0, 0, 0, 0)),
                   pl.BlockSpec((NH, CH, CH), lambda p, n: (0, 0, 0)),
                   pl.BlockSpec((1, AW), lambda p, n: (0, 0)),
                   pl.BlockSpec((NH, CH, CH), lambda p, n: (0, 0, 0)),
                   pl.BlockSpec((NH, HD, HD), lambda p, n: (0, 0, 0)),
                   pl.BlockSpec((NH, HD, HD), lambda p, n: (0, 0, 0))] + [hbm] * nt,
        out_shape=[jax.ShapeDtypeStruct((ln, IN_COLS), BF), jax.ShapeDtypeStruct((2, 4, NH, CH, CH), F32),
                   tile4, jax.ShapeDtypeStruct((1, AW), F32), tile4, tile4, tile4]
        + [jax.ShapeDtypeStruct((NCHIP - 1,) + p.shape[1:], BF) for p in parts],
        scratch_shapes=[pltpu.VMEM((nc, NH, HD, HD), F32), pltpu.VMEM((NH, HD, HD), F32)]
        + [pltpu.SemaphoreType.DMA(((NCHIP - 1) * nt,))] * 2,
        compiler_params=_params(56, ("arbitrary", "arbitrary")),
    )(z, z, z, z, z, z, z, dycat, dycat, cos_t, sin_t, dm, sgw, gain, bfull, sf_all, sb_all, *parts)
    return out[:7], out[7:]


def _small_reduce(ddm, ddm_ctx, dm, dbf):
    def body(ddm_ref, dctx_ref, dm_ref, dbf_ref, lg_ref, sgb_ref):
        ex = _decay_exponents()
        ones = jnp.ones((8, CH), F32)
        for d in range(2):
            for h in range(NH):
                tot = jnp.zeros((CH, CH), F32)
                for kind in range(4):
                    g = ddm_ref[d, kind, h]
                    if kind >= 2:
                        g = g + dctx_ref[d, kind - 2, h]
                    tot = tot + g * dm_ref[d, kind, h] * ex[d][kind]
                lg_ref[d * NH + h: d * NH + h + 1, :] = jnp.sum(tot, axis=0, keepdims=True)
        for g in range(NH):
            r = lax.dot_general(ones, dbf_ref[g], (NT, ((), ())), precision=HI, preferred_element_type=F32)
            sgb_ref[g:g + 1, :] = r[0:1, :]

    return pl.pallas_call(
        body, name="small_reduce",
        out_shape=[jax.ShapeDtypeStruct((8, CH), F32), jax.ShapeDtypeStruct((NH, CH), F32)],
        in_specs=[pl.BlockSpec(memory_space=pltpu.VMEM)] * 4,
        out_specs=[pl.BlockSpec(memory_space=pltpu.VMEM)] * 2,
        compiler_params=_params(32),
    )(ddm, ddm_ctx, dm, dbf)


def _mod_backward(ct_pad_t, cctx_col, dmod_pad, dcmod_cols, w_mod_s):
    def body(ct_ref, cc_ref, dm_ref, dc_ref, w_ref, gw_ref, part_ref):
        dcm = dc_ref[0:1, :]
        for d in range(1, NDEV):
            dcm = dcm + dc_ref[d:d + 1, :]
        gw_ref[...] = (jnp.dot(_silu(ct_ref[...]), dm_ref[...], precision=HI, preferred_element_type=F32)
                       + _silu(cc_ref[...]) * dcm)
        part_ref[...] = lax.dot_general(jnp.broadcast_to(dcm, (8, dcm.shape[1])), w_ref[...], (NT, ((), ())),
                                        precision=HI, preferred_element_type=F32)

    return pl.pallas_call(
        body, name="mod_backward",
        out_shape=[jax.ShapeDtypeStruct(w_mod_s.shape, F32), jax.ShapeDtypeStruct((8, D), F32)],
        in_specs=[pl.BlockSpec(memory_space=pltpu.VMEM)] * 5,
        out_specs=[pl.BlockSpec(memory_space=pltpu.VMEM)] * 2,
        compiler_params=_params(48),
    )(ct_pad_t, cctx_col, dmod_pad, dcmod_cols, w_mod_s)


def _cctx_update(parts, c_ctx, m, v):
    def body(p_ref, c_ref, m_ref, v_ref, g_ref, d_ref, mo_ref, vo_ref):
        tot = ((p_ref[0] + p_ref[2]) + p_ref[4]) + p_ref[6]
        cv = c_ref[...]
        s = jax.nn.sigmoid(cv)
        g = tot * (s * (1.0 + cv * (1.0 - s)))
        g_ref[...] = g
        d_ref[...], mo_ref[...], vo_ref[...] = _adamw_math(cv, g, m_ref[...], v_ref[...])

    return pl.pallas_call(
        body, name="cctx_update",
        out_shape=[jax.ShapeDtypeStruct((1, D), F32)] * 4,
        in_specs=[pl.BlockSpec(memory_space=pltpu.VMEM)] * 4,
        out_specs=[pl.BlockSpec(memory_space=pltpu.VMEM)] * 4,
        compiler_params=_params(16),
    )(parts, c_ctx, m, v)


def _small_update(gathered, wp, mp, vp):
    def body(g_ref, w_ref, m_ref, v_ref, go_ref, d_ref, mo_ref, vo_ref, loss_ref):
        tot = g_ref[0]
        for d in range(1, NDEV):
            tot = tot + g_ref[d]
        go_ref[Q_BMOD:Q_N1, :] = tot[P_DMOD:P_N1, :] + tot[P_DCMOD:P_DMOD, :]
        go_ref[Q_N1:Q_LG, :] = tot[P_N1:P_LG, :]
        lg = jnp.sum(tot[P_LG:P_N2, :], axis=1, keepdims=True)
        go_ref[Q_LG:Q_N2, :] = lg * jax.nn.sigmoid(-w_ref[Q_LG:Q_N2, :])
        go_ref[Q_N2:Q_ROWS, :] = tot[P_N2:P_LOSS, :]
        d_ref[...], mo_ref[...], vo_ref[...] = _adamw_math(w_ref[...], go_ref[...], m_ref[...], v_ref[...])
        ls = jnp.sum(jnp.sum(tot[P_LOSS:P_ROWS, :], axis=1, keepdims=True), axis=0, keepdims=True)
        loss_ref[...] = jnp.broadcast_to(ls, (8, CH))

    return pl.pallas_call(
        body, name="small_update",
        out_shape=[jax.ShapeDtypeStruct((Q_ROWS, CH), F32)] * 4 + [jax.ShapeDtypeStruct((8, CH), F32)],
        in_specs=[pl.BlockSpec(memory_space=pltpu.VMEM)] * 4,
        out_specs=[pl.BlockSpec(memory_space=pltpu.VMEM)] * 5,
        compiler_params=_params(32),
    )(gathered, wp, mp, vp)


def _rows(a):
    return a.reshape(-1, CH)


def _pack_small(b_mod, norm1, sg_gain, sg_w, sg_b, lf, lb, norm2, norm_f):
    lg = jnp.broadcast_to(jnp.concatenate([lf.reshape(NH), lb.reshape(NH)])[:, None], (2 * NH, CH))
    return jnp.concatenate([_rows(b_mod), _rows(norm1), _rows(sg_gain), _rows(sg_w), _rows(sg_b), lg,
                            _rows(norm2), _rows(norm_f)], axis=0)


def _unpack_small(p):
    return (p[Q_BMOD:Q_N1].reshape(1, 6 * D), p[Q_N1:Q_GAIN].reshape(1, D), p[Q_GAIN:Q_SGW].reshape(1, AW),
            p[Q_SGW:Q_SGB].reshape(1, NH, CH, CH), p[Q_SGB:Q_LG].reshape(1, NH, CH),
            p[Q_LG:Q_LG + NH, 0].reshape(1, NH), p[Q_LG + NH:Q_N2, 0].reshape(1, NH),
            p[Q_N2:Q_NF].reshape(1, D), p[Q_NF:Q_ROWS].reshape(D))


def _rope_tables(ln):
    pos = np.arange(ln)
    rows = (pos // GRID_W).astype(np.float32)
    cols = (pos % GRID_W).astype(np.float32)
    n_freq = HD // 4
    inv = (np.float32(ROPE_BASE) ** (-np.arange(n_freq, dtype=np.float32) / np.float32(n_freq))).astype(np.float32)
    ar = rows[:, None] * inv[None, :]
    ac = cols[:, None] * inv[None, :]
    cos_t = np.concatenate([np.cos(ar), np.cos(ar), np.cos(ac), np.cos(ac)], axis=1).astype(np.float32)
    sin_t = np.concatenate([-np.sin(ar), np.sin(ar), -np.sin(ac), np.sin(ac)], axis=1).astype(np.float32)
    return jnp.asarray(cos_t), jnp.asarray(sin_t)


def kernel(x, c, ctx, c_ctx, w_mod, b_mod, norm1, w_in, sg_gain, sg_w, sg_b, ret_logit_f, ret_logit_b, w_out, norm2, w_gate, w_up, w_down, norm_f, loss_target, m_c_ctx, m_w_mod, m_b_mod, m_norm1, m_w_in, m_sg_gain, m_sg_w, m_sg_b, m_ret_logit_f, m_ret_logit_b, m_w_out, m_norm2, m_w_gate, m_w_up, m_w_down, m_norm_f, v_c_ctx, v_w_mod, v_b_mod, v_norm1, v_w_in, v_sg_gain, v_sg_w, v_sg_b, v_ret_logit_f, v_ret_logit_b, v_w_out, v_norm2, v_w_gate, v_w_up, v_w_down, v_norm_f):
    ln = x.shape[1]
    xi, yi, ci = _pos()
    chip = 2 * xi + yi
    me = 4 * xi + 2 * yi + ci
    x2d = x.reshape(ln, D)
    tgt = loss_target.reshape(ln, D)
    mod_c = w_mod.shape[2]

    gbufs, c_all, prod_all = _prologue(c, c_ctx.reshape(1, D), w_mod[0],
                                       [w_in[0], w_out[0], w_gate[0], w_up[0], w_down[0]])
    wi = gbufs[0].reshape(NCHIP, D, WI_C)
    gbufs = gbufs[1:]
    c_all = c_all.reshape(NDEV, D)
    prod_chips = prod_all[0::2]
    mod_rows = jnp.transpose(prod_chips, (1, 0, 2)).reshape(16, NCHIP * mod_c) + b_mod
    mod = lax.dynamic_slice_in_dim(mod_rows, me, 1, axis=0)
    cmod = mod_rows[8:9]
    sh1, sc1, g1, sh2, sc2, g2 = [mod[:, i * D:(i + 1) * D] for i in range(6)]
    csh1, csc1 = cmod[:, 0:D], cmod[:, D:2 * D]
    zrow = jnp.zeros((1, D), F32)
    vec_in = jnp.concatenate([norm1, sh1, sc1] + [zrow] * 5, axis=0)
    vec_ctx = jnp.concatenate([norm1, csh1, csc1] + [zrow] * 5, axis=0)
    vec_post = jnp.concatenate([g1, norm2, sh2, sc2, g2, norm_f.reshape(1, D), zrow, zrow], axis=0)

    logits = jnp.concatenate([ret_logit_f.reshape(NH), ret_logit_b.reshape(NH)])
    dm = _decay_mats(jnp.broadcast_to(logits[:, None, None], (2 * NH, CH, CH)))
    ctx2d = ctx.reshape(ctx.shape[1], D)
    scf, scb = _ctx_forward(ctx2d, vec_ctx, wi, dm)

    cos_t, sin_t = _rope_tables(ln)
    z, hx, gbufs = _in_proj(x2d, vec_in, wi, gbufs)
    bfull = jnp.broadcast_to(sg_b[0][:, :, None], (NH, CH, CH))
    ycat, sf_all, sb_all, gbufs = _mixer_fwd(z, cos_t, sin_t, dm, sg_w[0], sg_gain, bfull, scf, scb, gbufs)
    wo, wg, wu, wd = [g.reshape(NCHIP, 2 * g.shape[2], g.shape[3]) for g in gbufs]
    wo = wo.reshape(D, D)

    dx1, dycat, h2, dy, df, act, da, db, acc_post = _post_mixer(x2d, ycat, tgt, vec_post, wo, wg, wu, wd)

    cidx = ci.reshape(1).astype(jnp.int32)
    chipidx = chip.reshape(1).astype(jnp.int32)

    def halves_summed(full, names):
        full = [g.reshape(NCHIP, 2, g.shape[1] // 2, g.shape[2]) for g in full]
        from_sib = _rs_exchange_halves(full, "rs_exchange_" + names[0])
        return [_rs_add_halves(g, r, cidx, "rs_add_halves_" + nm) for g, r, nm in zip(full, from_sib, names)]

    g_wo = _tn_matmul(ycat, dy, "grad_w_out", 1, D, D, False, "shared", 1024).reshape(NCHIP, WO_R, D)
    g_wg = _tn_matmul(h2, da, "grad_w_gate", NCHIP, D, FF_C, False, "batched", 1024)
    g_wu = _tn_matmul(h2, db, "grad_w_up", NCHIP, D, FF_C, False, "batched", 1024)
    g_wd = _tn_matmul(act, df, "grad_w_down", NCHIP, FF_C, D, True, "shared", 1024)
    names = ["w_in", "w_out", "w_gate", "w_up", "w_down"]
    sums_b = halves_summed([g_wo, g_wg, g_wu, g_wd], names[1:])

    (dz, ddm, dsgw, dgain, dbf, dscf, dscb), from_chips_b = _mixer_bwd(
        z, dycat, cos_t, sin_t, dm, sg_w[0], sg_gain, bfull, sf_all, sb_all, [s[1] for s in sums_b])
    gwkv, acc_ctx, ddm_ctx = _ctx_backward(ctx2d, vec_ctx, wi, dm, dscf, dscb)
    g_wi = _tn_matmul(hx, dz, "grad_w_in", NCHIP, D, WI_C, False, "cols", 512)
    g_wi = _add_ctx_cols(g_wi, gwkv)
    sums_a = halves_summed([g_wi], names[:1])
    gx, acc_in, from_chips_a = _in_proj_bwd(dz, x2d, dx1, vec_in, wi, [s[1] for s in sums_a])

    sums = sums_a + sums_b
    from_chips = list(from_chips_a) + list(from_chips_b)
    finals = [_rs_add_chips(s[0], r, chipidx, "rs_add_chips_" + nm) for s, r, nm in zip(sums, from_chips, names)]
    others = _rs_share_final(finals)

    lg_part, dsgb = _small_reduce(ddm, ddm_ctx, dm, dbf)
    dmod = jnp.concatenate([acc_in[1:2], acc_in[2:3], acc_post[0:1], acc_post[2:3], acc_post[3:4],
                            acc_post[4:5]], axis=1)
    dcmod = jnp.concatenate([acc_ctx[1:2], acc_ctx[2:3], jnp.zeros((1, 4 * D), F32)], axis=1)
    packed = jnp.concatenate([
        _rows(dcmod), _rows(dmod), _rows(acc_in[0:1] + acc_ctx[0:1]), _rows(dgain), _rows(dsgw), dsgb, lg_part,
        _rows(acc_post[1:2]), _rows(acc_post[5:6]), _rows(acc_post[6:7])], axis=0)
    gathered = _allgather_small(packed, "gather_small")
    dmod_all = gathered[:, P_DMOD:P_N1].reshape(NDEV, 6 * D)
    dcmod_all = gathered[:, P_DCMOD:P_DMOD].reshape(NDEV, 6 * D)
    dmod_cols = lax.dynamic_slice_in_dim(dmod_all, chip * mod_c, mod_c, axis=1)
    dcmod_cols = lax.dynamic_slice_in_dim(dcmod_all, chip * mod_c, mod_c, axis=1)
    dmod_pad = jnp.concatenate([dmod_cols, jnp.zeros((CH - NDEV, mod_c), F32)], axis=0)
    ct_pad_t = jnp.concatenate([jnp.transpose(c_all), jnp.zeros((D, CH - NDEV), F32)], axis=1)
    g_wmod, cctx_part = _mod_backward(ct_pad_t, c_ctx.reshape(D, 1), dmod_pad, dcmod_cols, w_mod[0])
    parts = _allgather_small(cctx_part[0:1], "gather_cctx")
    g_cctx, d_cctx, nm_cctx, nv_cctx = _cctx_update(parts, c_ctx.reshape(1, D), m_c_ctx.reshape(1, D),
                                                    v_c_ctx.reshape(1, D))

    wp = _pack_small(b_mod, norm1, sg_gain, sg_w, sg_b, ret_logit_f, ret_logit_b, norm2, norm_f)
    mp = _pack_small(m_b_mod, m_norm1, m_sg_gain, m_sg_w, m_sg_b, m_ret_logit_f, m_ret_logit_b, m_norm2, m_norm_f)
    vp = _pack_small(v_b_mod, v_norm1, v_sg_gain, v_sg_w, v_sg_b, v_ret_logit_f, v_ret_logit_b, v_norm2, v_norm_f)
    gp, dp, mp2, vp2, loss_t = _small_update(gathered, wp, mp, vp)

    big_w = [w_in[0], w_out[0], w_gate[0], w_up[0], w_down[0]]
    big_m = [m_w_in[0], m_w_out[0], m_w_gate[0], m_w_up[0], m_w_down[0]]
    big_v = [v_w_in[0], v_w_out[0], v_w_gate[0], v_w_up[0], v_w_down[0]]
    upd = [_adamw_halves(w, own, oth, m, v, cidx, "adamw_" + nm) for w, own, oth, m, v, nm in
           zip(big_w, finals, others, big_m, big_v, names)]
    big_g = [g_wmod] + [u[0] for u in upd]
    big = [_adamw(w_mod[0], g_wmod, m_w_mod[0], v_w_mod[0], "adamw_w_mod")] + [u[1:] for u in upd]

    def assemble(small, cctx, bigs):
        b_mod_, norm1_, gain_, sgw_, sgb_, lf_, lb_, norm2_, normf_ = _unpack_small(small)
        wm, wi_, wo_, wg_, wu_, wd_ = [b[None] for b in bigs]
        return [cctx.reshape(D), wm, b_mod_, norm1_, wi_, gain_, sgw_, sgb_, lf_, lb_, wo_, norm2_, wg_, wu_, wd_,
                normf_]

    out = [loss_t[0, 0], gx.reshape(1, ln, D)]
    out += assemble(gp, g_cctx, big_g)
    out += assemble(dp, d_cctx, [b[0] for b in big])
    out += assemble(mp2, nm_cctx, [b[1] for b in big])
    out += assemble(vp2, nv_cctx, [b[2] for b in big])
    return tuple(out)
```

```python
import functools

import jax
import jax.numpy as jnp
import numpy as np
from jax import lax
from jax.experimental import pallas as pl
from jax.experimental.pallas import tpu as pltpu

F32 = jnp.float32
BF = jnp.bfloat16
MESH = pl.DeviceIdType.MESH

D = 1024
CH = 128
HD = 128
NH = 4
AW = 512
IN_COLS = 3584
DFF = 2816
NCHIP = 4
NDEV = 8
WI_C = IN_COLS // NCHIP
FF_C = DFF // NCHIP
WO_R = D // NCHIP
EPS = 1e-6
GRID_W = 64
ROPE_BASE = 10000.0
K_SCALE = HD ** -0.5
LR, B1, B2, AEPS, WD, STEP = 0.001, 0.9, 0.999, 1e-08, 0.01, 10
VMEM_MB = 1 << 20
HI = lax.Precision.HIGHEST

P_DCMOD, P_DMOD, P_N1, P_GAIN, P_SGW, P_SGB, P_LG, P_N2, P_NF, P_LOSS = 0, 48, 96, 104, 108, 620, 624, 632, 640, 648
P_ROWS = 656
Q_BMOD, Q_N1, Q_GAIN, Q_SGW, Q_SGB, Q_LG, Q_N2, Q_NF = 0, 48, 56, 60, 572, 576, 584, 592
Q_ROWS = 600


def _params(vmem_mb, sem=None):
    return pltpu.CompilerParams(vmem_limit_bytes=vmem_mb * VMEM_MB, dimension_semantics=sem)


def _const(shape):
    nd = len(shape)
    return pl.BlockSpec(shape, lambda *_: (0,) * nd, pipeline_mode=pl.Buffered(1))


def _pos():
    return lax.axis_index("x"), lax.axis_index("y"), lax.axis_index("c")


def _dot(a, b, dims):
    return lax.dot_general(a, b, (dims, ((), ())), preferred_element_type=F32)


NN = ((1,), (0,))
NT = ((1,), (1,))
TN = ((0,), (0,))


@jax.custom_vjp
def _mm(a, b):
    return _dot(a.astype(BF), b.astype(BF), NN)


def _mm_f(a, b):
    return _mm(a, b), (a.astype(BF), b.astype(BF))


def _mm_b(res, g):
    a, b = res
    gb = g.astype(BF)
    return _dot(gb, b, NT), _dot(a, gb, TN)


_mm.defvjp(_mm_f, _mm_b)


@jax.custom_vjp
def _mm_nt(a, b):
    return _dot(a.astype(BF), b.astype(BF), NT)


def _mm_nt_f(a, b):
    return _mm_nt(a, b), (a.astype(BF), b.astype(BF))


def _mm_nt_b(res, g):
    a, b = res
    gb = g.astype(BF)
    return _dot(gb, b, NN), _dot(gb, a, TN)


_mm_nt.defvjp(_mm_nt_f, _mm_nt_b)


@jax.custom_vjp
def _mm_tn(a, b):
    return _dot(a.astype(BF), b.astype(BF), TN)


def _mm_tn_f(a, b):
    return _mm_tn(a, b), (a.astype(BF), b.astype(BF))


def _mm_tn_b(res, g):
    a, b = res
    gb = g.astype(BF)
    return _dot(b, gb, NT), _dot(a, gb, NN)


_mm_tn.defvjp(_mm_tn_f, _mm_tn_b)


def _gelu(x):
    return x * (0.5 * (1.0 + jnp.tanh(0.7978845608028654 * (x + 0.044715 * (x * x * x)))))


def _silu(x):
    return x * jax.nn.sigmoid(x)


def _rms(x):
    return lax.rsqrt(jnp.mean(x * x, axis=-1, keepdims=True) + EPS)


def _swap32(t):
    lane = lax.broadcasted_iota(jnp.int32, t.shape, 1)
    first = (lane % 64) < 32
    return jnp.where(first, pltpu.roll(t, 96, 1), pltpu.roll(t, 32, 1))


def _rope(t, cos, sin):
    return t * cos + _swap32(t) * sin


def _rope_bwd(d, cos, sin):
    return d * cos + _swap32(d * sin)


def _heads(ref, off=0):
    return [ref[:, off + h * HD: off + (h + 1) * HD].astype(F32) for h in range(NH)]


def _chunk_fwd(u, v, q, k, vr, gf, gb, sf, sb, df, xf, zf, db, xb, zb, sgw, gain, bfull):
    ya, yr, uf, ub = [], [], [], []
    for g in range(NH):
        gu = _gelu(u[g])
        gv = _gelu(v[g])
        vn = gv * _rms(gv) * gain[g]
        ya.append(gu * (_mm(sgw[g], vn) + bfull[g]))
    for h in range(NH):
        a = _mm_nt(q[h], k[h])
        of = _mm(a * df[h], vr[h]) + xf[h] * _mm(q[h], sf[h])
        ob = _mm(a * db[h], vr[h]) + xb[h] * _mm(q[h], sb[h])
        yr.append(_silu(gf[h]) * (of * _rms(of)) + _silu(gb[h]) * (ob * _rms(ob)))
        uf.append(_mm_tn(k[h], zf[h] * vr[h]))
        ub.append(_mm_tn(k[h], zb[h] * vr[h]))
    return ya, yr, uf, ub


def _fwd_dir_only(sf, q, k, vr, gf, df, xf):
    out = []
    for h in range(NH):
        a = _mm_nt(q[h], k[h])
        of = _mm(a * df[h], vr[h]) + xf[h] * _mm(q[h], sf[h])
        out.append(_silu(gf[h]) * (of * _rms(of)))
    return out


def _ctx_states(ctx0, ctx1, n1, csh, csc, wk, wv, zf, zb, ef, eb):
    hc0 = (ctx0 * _rms(ctx0) * n1) * (1.0 + csc) + csh
    hc1 = (ctx1 * _rms(ctx1) * n1) * (1.0 + csc) + csh
    scf, scb = [], []
    for h in range(NH):
        k0, k1 = _mm(hc0, wk[h]) * K_SCALE, _mm(hc1, wk[h]) * K_SCALE
        v0, v1 = _mm(hc0, wv[h]), _mm(hc1, wv[h])
        scf.append(ef[h] * _mm_tn(k0, zf[h] * v0) + _mm_tn(k1, zf[h] * v1))
        scb.append(eb[h] * _mm_tn(k1, zb[h] * v1) + _mm_tn(k0, zb[h] * v0))
    return scf, scb


def _allgather_small(v, name):
    r, n = v.shape

    def body(v_ref, out_ref, send_sems, recv_sems, local_sem):
        x, y, c = _pos()
        me = 4 * x + 2 * y + c
        mine = pltpu.make_async_copy(v_ref, out_ref.at[me], local_sem)
        mine.start()
        sent = []
        for k in range(1, NDEV):
            kx, ky, kc = (k >> 2) & 1, (k >> 1) & 1, k & 1
            peer = (x ^ kx, y ^ ky, c ^ kc)
            cp = pltpu.make_async_remote_copy(src_ref=v_ref, dst_ref=out_ref.at[me], send_sem=send_sems.at[k - 1],
                                              recv_sem=recv_sems.at[k - 1], device_id=peer, device_id_type=MESH)
            cp.start()
            sent.append(cp)
        for k in range(1, NDEV):
            kx, ky, kc = (k >> 2) & 1, (k >> 1) & 1, k & 1
            peer = (x ^ kx, y ^ ky, c ^ kc)
            src = 4 * (x ^ kx) + 2 * (y ^ ky) + (c ^ kc)
            pltpu.make_async_remote_copy(src_ref=v_ref, dst_ref=out_ref.at[src], send_sem=send_sems.at[k - 1],
                                         recv_sem=recv_sems.at[k - 1], device_id=peer, device_id_type=MESH).wait_recv()
        for cp in sent:
            cp.wait_send()
        mine.wait()

    return pl.pallas_call(
        body, name=name,
        out_shape=jax.ShapeDtypeStruct((NDEV, r, n), F32),
        in_specs=[pl.BlockSpec(memory_space=pltpu.VMEM)],
        out_specs=pl.BlockSpec(memory_space=pltpu.VMEM),
        scratch_shapes=[pltpu.SemaphoreType.DMA((NDEV - 1,)), pltpu.SemaphoreType.DMA((NDEV - 1,)),
                        pltpu.SemaphoreType.DMA],
        compiler_params=_params(16),
    )(v)


def _gather_weights(shards):
    nt = len(shards)
    shapes = [s.shape for s in shards]

    def body(*refs):
        srcs, outs, stages = refs[:nt], refs[nt:2 * nt], refs[2 * nt:3 * nt]
        ici_send, ici_recv, d2d_send, d2d_recv, local_sems = refs[3 * nt:]
        x, y, c = _pos()
        chip = 2 * x + y
        for t in range(nt):
            half = shapes[t][0] // 2
            stages[t][0] = srcs[t][0:half, :].astype(BF)
            stages[t][1] = srcs[t][half:2 * half, :].astype(BF)
        local = []
        for t in range(nt):
            cp = pltpu.make_async_copy(stages[t], outs[t].at[chip], local_sems.at[t])
            cp.start()
            local.append(cp)
        sent = []
        for k in range(1, NCHIP):
            kx, ky = (k >> 1) & 1, k & 1
            for t in range(nt):
                s = (k - 1) * nt + t
                cp = pltpu.make_async_remote_copy(
                    src_ref=stages[t].at[c], dst_ref=outs[t].at[chip, c], send_sem=ici_send.at[s],
                    recv_sem=ici_recv.at[s], device_id=(x ^ kx, y ^ ky, c), device_id_type=MESH)
                cp.start()
                sent.append(cp)
        for k in range(1, NCHIP):
            kx, ky = (k >> 1) & 1, k & 1
            src_chip = 2 * (x ^ kx) + (y ^ ky)
            for t in range(nt):
                s = (k - 1) * nt + t
                pltpu.make_async_remote_copy(
                    src_ref=stages[t].at[c], dst_ref=outs[t].at[src_chip, c], send_sem=ici_send.at[s],
                    recv_sem=ici_recv.at[s], device_id=(x ^ kx, y ^ ky, c), device_id_type=MESH).wait_recv()
                cp = pltpu.make_async_remote_copy(
                    src_ref=outs[t].at[src_chip, c], dst_ref=outs[t].at[src_chip, c], send_sem=d2d_send.at[s],
                    recv_sem=d2d_recv.at[s], device_id=(x, y, 1 - c), device_id_type=MESH)
                cp.start()
                sent.append(cp)
        for k in range(1, NCHIP):
            kx, ky = (k >> 1) & 1, k & 1
            src_chip = 2 * (x ^ kx) + (y ^ ky)
            for t in range(nt):
                s = (k - 1) * nt + t
                pltpu.make_async_remote_copy(
                    src_ref=stages[t].at[c], dst_ref=outs[t].at[src_chip, 1 - c], send_sem=d2d_send.at[s],
                    recv_sem=d2d_recv.at[s], device_id=(x, y, 1 - c), device_id_type=MESH).wait_recv()
        for cp in sent:
            cp.wait_send()
        for cp in local:
            cp.wait()

    n_rem = (NCHIP - 1) * nt
    out = pl.pallas_call(
        body, name="gather_weights",
        out_shape=[jax.ShapeDtypeStruct((NCHIP, 2, r // 2, cc), BF) for r, cc in shapes],
        in_specs=[pl.BlockSpec(memory_space=pltpu.VMEM)] * nt,
        out_specs=[pl.BlockSpec(memory_space=pl.ANY)] * nt,
        scratch_shapes=[pltpu.VMEM((2, r // 2, cc), BF) for r, cc in shapes]
        + [pltpu.SemaphoreType.DMA((n_rem,))] * 4 + [pltpu.SemaphoreType.DMA((nt,))],
        compiler_params=_params(48),
    )(*shards)
    return [o.reshape(NCHIP, r, cc) for o, (r, cc) in zip(out, shapes)]


def _chip_offsets():
    return [((k >> 1) & 1, k & 1) for k in range(1, NCHIP)]


def _prologue(c, c_ctx, w_mod_s, shards):
    nt = len(shards)
    shapes = [s.shape for s in shards]
    mod_c = w_mod_s.shape[1]

    def body(*refs):
        c_ref, cc_ref, wm_ref = refs[:3]
        srcs = refs[3:3 + nt]
        outs = refs[3 + nt:3 + 2 * nt]
        call_ref, prod_ref = refs[3 + 2 * nt:5 + 2 * nt]
        stages = refs[5 + 2 * nt:5 + 3 * nt]
        ct = refs[5 + 3 * nt]
        c_send, c_recv, p_send, p_recv, ici_send, ici_recv, d2d_send, d2d_recv, local_sems = refs[6 + 3 * nt:]
        x, y, c = _pos()
        chip = 2 * x + y
        me = 4 * x + 2 * y + c
        sib = (x, y, 1 - c)
        pending = []
        for t in range(nt):
            half = shapes[t][0] // 2
            stages[t][0] = srcs[t][0:half, :].astype(BF)
            stages[t][1] = srcs[t][half:2 * half, :].astype(BF)
            cp = pltpu.make_async_copy(stages[t], outs[t].at[chip], local_sems.at[t])
            cp.start()
            pending.append(cp)
        sends = []
        for k, (kx, ky) in enumerate(_chip_offsets()):
            cp = pltpu.make_async_remote_copy(src_ref=stages[0].at[c], dst_ref=outs[0].at[chip, c],
                                              send_sem=ici_send.at[k], recv_sem=ici_recv.at[k],
                                              device_id=(x ^ kx, y ^ ky, c), device_id_type=MESH)
            cp.start()
            sends.append(cp)

        def to_all(src, dst_of, send_sems, recv_sems):
            for k in range(1, NDEV):
                kx, ky, kc = (k >> 2) & 1, (k >> 1) & 1, k & 1
                cp = pltpu.make_async_remote_copy(src_ref=src, dst_ref=dst_of(me), send_sem=send_sems.at[k - 1],
                                                  recv_sem=recv_sems.at[k - 1], device_id=(x ^ kx, y ^ ky, c ^ kc),
                                                  device_id_type=MESH)
                cp.start()
                sends.append(cp)
            for k in range(1, NDEV):
                kx, ky, kc = (k >> 2) & 1, (k >> 1) & 1, k & 1
                frm = 4 * (x ^ kx) + 2 * (y ^ ky) + (c ^ kc)
                pltpu.make_async_remote_copy(src_ref=src, dst_ref=dst_of(frm), send_sem=send_sems.at[k - 1],
                                             recv_sem=recv_sems.at[k - 1], device_id=(x ^ kx, y ^ ky, c ^ kc),
                                             device_id_type=MESH).wait_recv()

        call_ref[me] = c_ref[...]
        to_all(c_ref, lambda d: call_ref.at[d], c_send, c_recv)
        ct[...] = jnp.zeros_like(ct)
        for d in range(NDEV):
            ct[d:d + 1, :] = call_ref[d]
        ct[NDEV:NDEV + 1, :] = cc_ref[...]
        prod_ref[me] = jnp.dot(_silu(ct[...]), wm_ref[...], precision=HI, preferred_element_type=F32)
        to_all(prod_ref.at[me], lambda d: prod_ref.at[d], p_send, p_recv)

        for k, (kx, ky) in enumerate(_chip_offsets()):
            frm = 2 * (x ^ kx) + (y ^ ky)
            pltpu.make_async_remote_copy(src_ref=stages[0].at[c], dst_ref=outs[0].at[frm, c],
                                         send_sem=ici_send.at[k], recv_sem=ici_recv.at[k],
                                         device_id=(x ^ kx, y ^ ky, c), device_id_type=MESH).wait_recv()
            cp = pltpu.make_async_remote_copy(src_ref=outs[0].at[frm, c], dst_ref=outs[0].at[frm, c],
                                              send_sem=d2d_send.at[k], recv_sem=d2d_recv.at[k],
                                              device_id=sib, device_id_type=MESH)
            cp.start()
            sends.append(cp)
        for k, (kx, ky) in enumerate(_chip_offsets()):
            frm = 2 * (x ^ kx) + (y ^ ky)
            pltpu.make_async_remote_copy(src_ref=stages[0].at[c], dst_ref=outs[0].at[frm, 1 - c],
                                         send_sem=d2d_send.at[k], recv_sem=d2d_recv.at[k],
                                         device_id=sib, device_id_type=MESH).wait_recv()
        for cp in sends:
            cp.wait_send()
        for cp in pending:
            cp.wait()

    vm = pl.BlockSpec(memory_space=pltpu.VMEM)
    out = pl.pallas_call(
        body, name="prologue",
        out_shape=[jax.ShapeDtypeStruct((NCHIP, 2, r // 2, cc), BF) for r, cc in shapes]
        + [jax.ShapeDtypeStruct((NDEV, 1, D), F32), jax.ShapeDtypeStruct((NDEV, 16, mod_c), F32)],
        in_specs=[vm] * (3 + nt),
        out_specs=[pl.BlockSpec(memory_space=pl.ANY)] * nt + [vm, vm],
        scratch_shapes=[pltpu.VMEM((2, r // 2, cc), BF) for r, cc in shapes] + [pltpu.VMEM((16, D), F32)]
        + [pltpu.SemaphoreType.DMA((NDEV - 1,))] * 4 + [pltpu.SemaphoreType.DMA((NCHIP - 1,))] * 4
        + [pltpu.SemaphoreType.DMA((nt,))],
        compiler_params=_params(56),
    )(c, c_ctx, w_mod_s, *shards)
    return out[:nt], out[nt], out[nt + 1]


def _gather_ici_copies(bufs, send_sems, recv_sems):
    x, y, c = _pos()
    chip = 2 * x + y
    nt = len(bufs)
    out_cp, in_cp = [], []
    for k, (kx, ky) in enumerate(_chip_offsets()):
        frm = 2 * (x ^ kx) + (y ^ ky)
        for t in range(nt):
            s = k * nt + t
            peer = (x ^ kx, y ^ ky, c)
            out_cp.append(pltpu.make_async_remote_copy(
                src_ref=bufs[t].at[chip, c], dst_ref=bufs[t].at[chip, c], send_sem=send_sems.at[s],
                recv_sem=recv_sems.at[s], device_id=peer, device_id_type=MESH))
            in_cp.append(pltpu.make_async_remote_copy(
                src_ref=bufs[t].at[chip, c], dst_ref=bufs[t].at[frm, c], send_sem=send_sems.at[s],
                recv_sem=recv_sems.at[s], device_id=peer, device_id_type=MESH))
    return out_cp, in_cp


def _gather_d2d_copies(bufs, send_sems, recv_sems):
    x, y, c = _pos()
    nt = len(bufs)
    out_cp, in_cp = [], []
    for k, (kx, ky) in enumerate(_chip_offsets()):
        frm = 2 * (x ^ kx) + (y ^ ky)
        for t in range(nt):
            s = k * nt + t
            out_cp.append(pltpu.make_async_remote_copy(
                src_ref=bufs[t].at[frm, c], dst_ref=bufs[t].at[frm, c], send_sem=send_sems.at[s],
                recv_sem=recv_sems.at[s], device_id=(x, y, 1 - c), device_id_type=MESH))
            in_cp.append(pltpu.make_async_remote_copy(
                src_ref=bufs[t].at[frm, c], dst_ref=bufs[t].at[frm, 1 - c], send_sem=send_sems.at[s],
                recv_sem=recv_sems.at[s], device_id=(x, y, 1 - c), device_id_type=MESH))
    return out_cp, in_cp


def _scatter_ici_copies(parts, outs, send_sems, recv_sems):
    x, y, c = _pos()
    nt = len(parts)
    cps = []
    for k, (kx, ky) in enumerate(_chip_offsets()):
        dst_chip = 2 * (x ^ kx) + (y ^ ky)
        for t in range(nt):
            s = k * nt + t
            cps.append(pltpu.make_async_remote_copy(
                src_ref=parts[t].at[dst_chip], dst_ref=outs[t].at[k], send_sem=send_sems.at[s],
                recv_sem=recv_sems.at[s], device_id=(x ^ kx, y ^ ky, c), device_id_type=MESH))
    return cps


def _rs_exchange_halves(grads, name):
    nt = len(grads)
    shapes = [g.shape for g in grads]

    def body(*refs):
        gs, outs = refs[:nt], refs[nt:2 * nt]
        send_sems, recv_sems = refs[2 * nt:]
        x, y, c = _pos()
        sib = (x, y, 1 - c)
        sent = []
        for t in range(nt):
            for j in range(NCHIP):
                s = t * NCHIP + j
                cp = pltpu.make_async_remote_copy(src_ref=gs[t].at[j, 1 - c], dst_ref=outs[t].at[j],
                                                  send_sem=send_sems.at[s], recv_sem=recv_sems.at[s],
                                                  device_id=sib, device_id_type=MESH)
                cp.start()
                sent.append(cp)
        for cp in sent:
            cp.wait_recv()
        for cp in sent:
            cp.wait_send()

    return pl.pallas_call(
        body, name=name,
        out_shape=[jax.ShapeDtypeStruct((NCHIP, s[2], s[3]), F32) for s in shapes],
        in_specs=[pl.BlockSpec(memory_space=pl.ANY)] * nt,
        out_specs=[pl.BlockSpec(memory_space=pl.ANY)] * nt,
        scratch_shapes=[pltpu.SemaphoreType.DMA((nt * NCHIP,))] * 2,
    )(*grads)


def _rs_send_chips(parts):
    nt = len(parts)
    shapes = [p.shape for p in parts]

    def body(*refs):
        ps, outs = refs[:nt], refs[nt:2 * nt]
        send_sems, recv_sems = refs[2 * nt:]
        x, y, c = _pos()
        sent = []
        for k in range(1, NCHIP):
            kx, ky = (k >> 1) & 1, k & 1
            dst_chip = 2 * (x ^ kx) + (y ^ ky)
            for t in range(nt):
                s = (k - 1) * nt + t
                cp = pltpu.make_async_remote_copy(src_ref=ps[t].at[dst_chip], dst_ref=outs[t].at[k - 1],
                                                  send_sem=send_sems.at[s], recv_sem=recv_sems.at[s],
                                                  device_id=(x ^ kx, y ^ ky, c), device_id_type=MESH)
                cp.start()
                sent.append(cp)
        for cp in sent:
            cp.wait_recv()
        for cp in sent:
            cp.wait_send()

    return pl.pallas_call(
        body, name="rs_send_chips",
        out_shape=[jax.ShapeDtypeStruct((NCHIP - 1, s[1], s[2]), BF) for s in shapes],
        in_specs=[pl.BlockSpec(memory_space=pl.ANY)] * nt,
        out_specs=[pl.BlockSpec(memory_space=pl.ANY)] * nt,
        scratch_shapes=[pltpu.SemaphoreType.DMA((nt * (NCHIP - 1),))] * 2,
    )(*parts)


def _rs_share_final(finals):
    nt = len(finals)
    shapes = [f.shape for f in finals]

    def body(*refs):
        fs, outs = refs[:nt], refs[nt:2 * nt]
        send_sems, recv_sems = refs[2 * nt:]
        x, y, c = _pos()
        sent = []
        for t in range(nt):
            cp = pltpu.make_async_remote_copy(src_ref=fs[t], dst_ref=outs[t], send_sem=send_sems.at[t],
                                              recv_sem=recv_sems.at[t], device_id=(x, y, 1 - c), device_id_type=MESH)
            cp.start()
            sent.append(cp)
        for cp in sent:
            cp.wait_recv()
        for cp in sent:
            cp.wait_send()

    return pl.pallas_call(
        body, name="rs_share_final",
        out_shape=[jax.ShapeDtypeStruct(s, F32) for s in shapes],
        in_specs=[pl.BlockSpec(memory_space=pl.ANY)] * nt,
        out_specs=[pl.BlockSpec(memory_space=pl.ANY)] * nt,
        scratch_shapes=[pltpu.SemaphoreType.DMA((nt,))] * 2,
    )(*finals)


def _row_tile(h, cc=D):
    for t in (512, 384, 352, 256, 176, 128, 64, 32, 16):
        if h % t == 0 and t * cc * 4 <= (5 * VMEM_MB) // 4:
            return t
    return h


def _rs_add_halves(g, recv, cidx, name):
    _, _, h, cc = g.shape
    th = _row_tile(h, cc)

    def body(c_ref, g_ref, r_ref, of_ref, ob_ref):
        s = g_ref[...] + r_ref[...]
        of_ref[...] = s
        ob_ref[...] = s.astype(BF)

    return pl.pallas_call(
        body, name=name,
        grid_spec=pltpu.PrefetchScalarGridSpec(
            num_scalar_prefetch=1, grid=(NCHIP, h // th),
            in_specs=[pl.BlockSpec((None, None, th, cc), lambda j, i, c_ref: (j, c_ref[0], i, 0)),
                      pl.BlockSpec((None, th, cc), lambda j, i, c_ref: (j, i, 0))],
            out_specs=[pl.BlockSpec((None, th, cc), lambda j, i, c_ref: (j, i, 0)),
                       pl.BlockSpec((None, th, cc), lambda j, i, c_ref: (j, i, 0))]),
        out_shape=[jax.ShapeDtypeStruct((NCHIP, h, cc), F32), jax.ShapeDtypeStruct((NCHIP, h, cc), BF)],
        compiler_params=_params(48),
    )(cidx, g, recv)


def _rs_add_chips(own, recv, chipidx, name):
    _, h, cc = own.shape
    th = _row_tile(h, cc)

    def body(j_ref, o_ref, r_ref, out_ref):
        out_ref[...] = ((o_ref[...] + r_ref[0].astype(F32)) + r_ref[1].astype(F32)) + r_ref[2].astype(F32)

    return pl.pallas_call(
        body, name=name,
        grid_spec=pltpu.PrefetchScalarGridSpec(
            num_scalar_prefetch=1, grid=(h // th,),
            in_specs=[pl.BlockSpec((None, th, cc), lambda i, j_ref: (j_ref[0], i, 0)),
                      pl.BlockSpec((NCHIP - 1, th, cc), lambda i, j_ref: (0, i, 0))],
            out_specs=pl.BlockSpec((th, cc), lambda i, j_ref: (i, 0))),
        out_shape=jax.ShapeDtypeStruct((h, cc), F32),
        compiler_params=_params(48),
    )(chipidx, own, recv)


def _adamw_math(w, g, m, v):
    m2 = B1 * m + (1.0 - B1) * g
    v2 = B2 * v + (1.0 - B2) * (g * g)
    m_hat = m2 / (1.0 - B1 ** STEP)
    v_hat = v2 / (1.0 - B2 ** STEP)
    delta = -LR * (m_hat / (jnp.sqrt(v_hat) + AEPS) + WD * w)
    return delta, m2, v2


def _adamw(w, g, m, v, name):
    r, cc = w.shape
    tr = _row_tile(r, cc)

    def body(w_ref, g_ref, m_ref, v_ref, d_ref, mo_ref, vo_ref):
        d, m2, v2 = _adamw_math(w_ref[...], g_ref[...], m_ref[...], v_ref[...])
        d_ref[...] = d
        mo_ref[...] = m2
        vo_ref[...] = v2

    spec = pl.BlockSpec((tr, cc), lambda i: (i, 0))
    return pl.pallas_call(
        body, name=name, grid=(r // tr,), in_specs=[spec] * 4, out_specs=[spec] * 3,
        out_shape=[jax.ShapeDtypeStruct((r, cc), F32)] * 3,
        compiler_params=_params(48, ("parallel",)),
    )(w, g, m, v)


def _adamw_halves(w, own, other, m, v, cidx, name):
    r, cc = w.shape
    h = r // 2
    tr = _row_tile(h, cc)
    per = h // tr

    def body(c_ref, w_ref, own_ref, oth_ref, m_ref, v_ref, g_ref, d_ref, mo_ref, vo_ref):
        mine = (pl.program_id(0) // per) == c_ref[0]
        g = jnp.where(mine, own_ref[...], oth_ref[...])
        g_ref[...] = g
        d, m2, v2 = _adamw_math(w_ref[...], g, m_ref[...], v_ref[...])
        d_ref[...] = d
        mo_ref[...] = m2
        vo_ref[...] = v2

    full = pl.BlockSpec((tr, cc), lambda i, c_ref: (i, 0))
    half = pl.BlockSpec((tr, cc), lambda i, c_ref: (i % per, 0))
    return pl.pallas_call(
        body, name=name,
        grid_spec=pltpu.PrefetchScalarGridSpec(
            num_scalar_prefetch=1, grid=(r // tr,),
            in_specs=[full, half, half, full, full], out_specs=[full] * 4),
        out_shape=[jax.ShapeDtypeStruct((r, cc), F32)] * 4,
        compiler_params=_params(48, ("parallel",)),
    )(cidx, w, own, other, m, v)


def _mod_forward(ct_pad, w_mod_s):
    def body(c_ref, w_ref, o_ref):
        o_ref[...] = jnp.dot(_silu(c_ref[...]), w_ref[...], precision=HI, preferred_element_type=F32)

    return pl.pallas_call(
        body, name="mod_forward",
        out_shape=jax.ShapeDtypeStruct((16, w_mod_s.shape[1]), F32),
        in_specs=[pl.BlockSpec(memory_space=pltpu.VMEM)] * 2,
        out_specs=pl.BlockSpec(memory_space=pltpu.VMEM),
        compiler_params=_params(32),
    )(ct_pad, w_mod_s)


def _decay_exponents():
    ri = lax.broadcasted_iota(jnp.int32, (CH, CH), 0).astype(F32)
    ci = lax.broadcasted_iota(jnp.int32, (CH, CH), 1).astype(F32)
    full = jnp.full((CH, CH), float(CH), F32)
    return [[ri - ci, ri + 1.0, (CH - 1.0) - ri, full], [ci - ri, CH - ri, ri, full]]


def _decay_mats(logit_full):
    def body(l_ref, o_ref):
        ex = _decay_exponents()
        for d in range(2):
            for h in range(NH):
                lv = l_ref[d * NH + h]
                lg = jnp.minimum(lv, 0.0) - jnp.log(1.0 + jnp.exp(-jnp.abs(lv)))
                for kind in range(4):
                    m = jnp.exp(lg * ex[d][kind])
                    if kind == 0:
                        m = jnp.where(ex[d][0] >= 0.0, jnp.exp(lg * jnp.maximum(ex[d][0], 0.0)), 0.0)
                    o_ref[d, kind, h] = m

    return pl.pallas_call(
        body, name="decay_mats",
        out_shape=jax.ShapeDtypeStruct((2, 4, NH, CH, CH), F32),
        in_specs=[pl.BlockSpec(memory_space=pltpu.VMEM)],
        out_specs=pl.BlockSpec(memory_space=pltpu.VMEM),
        compiler_params=_params(32),
    )(logit_full)


def _ctx_kv_weights(wi_ref):
    def cols(g):
        return wi_ref[g // WI_C, :, g % WI_C: g % WI_C + HD].astype(F32)

    wk = [cols(3 * AW + h * HD) for h in range(NH)]
    wv = [cols(4 * AW + h * HD) for h in range(NH)]
    return wk, wv


def _ctx_forward(ctx, vecs, wi, dm):
    def body(ctx_ref, v_ref, wi_ref, dm_ref, scf_ref, scb_ref):
        wk, wv = _ctx_kv_weights(wi_ref)
        mats = [[dm_ref[d, kind, h] for h in range(NH)] for d in range(2) for kind in (2, 3)]
        scf, scb = _ctx_states(ctx_ref[0:CH, :], ctx_ref[CH:2 * CH, :], v_ref[0:1, :], v_ref[1:2, :],
                               v_ref[2:3, :], wk, wv, mats[0], mats[2], mats[1], mats[3])
        for h in range(NH):
            scf_ref[h] = scf[h]
            scb_ref[h] = scb[h]

    return pl.pallas_call(
        body, name="ctx_forward",
        out_shape=[jax.ShapeDtypeStruct((NH, HD, HD), F32)] * 2,
        in_specs=[pl.BlockSpec(memory_space=pltpu.VMEM)] * 4,
        out_specs=[pl.BlockSpec(memory_space=pltpu.VMEM)] * 2,
        compiler_params=_params(48),
    )(ctx, vecs, wi, dm)


def _ctx_backward(ctx, vecs, wi, dm, dscf, dscb):
    def body(ctx_ref, v_ref, wi_ref, dm_ref, gf_ref, gb_ref, gw_ref, gv_ref, gdm_ref):
        wk, wv = _ctx_kv_weights(wi_ref)
        mats = [[dm_ref[d, kind, h] for h in range(NH)] for d in range(2) for kind in (2, 3)]
        ctx0, ctx1 = ctx_ref[0:CH, :], ctx_ref[CH:2 * CH, :]

        def fn(n1, csh, csc, wk_, wv_, zf, zb, ef, eb):
            return _ctx_states(ctx0, ctx1, n1, csh, csc, wk_, wv_, zf, zb, ef, eb)

        _, vjp = jax.vjp(fn, v_ref[0:1, :], v_ref[1:2, :], v_ref[2:3, :], wk, wv,
                         mats[0], mats[2], mats[1], mats[3])
        cot = ([gf_ref[h] for h in range(NH)], [gb_ref[h] for h in range(NH)])
        dn1, dcsh, dcsc, dwk, dwv, dzf, dzb, def_, deb = vjp(cot)
        for h in range(NH):
            gw_ref[:, h * HD:(h + 1) * HD] = dwk[h]
            gw_ref[:, AW + h * HD:AW + (h + 1) * HD] = dwv[h]
        gv_ref[...] = jnp.zeros_like(gv_ref)
        gv_ref[0:1, :] = dn1
        gv_ref[1:2, :] = dcsh
        gv_ref[2:3, :] = dcsc
        for h in range(NH):
            gdm_ref[0, 0, h] = dzf[h]
            gdm_ref[0, 1, h] = def_[h]
            gdm_ref[1, 0, h] = dzb[h]
            gdm_ref[1, 1, h] = deb[h]

    return pl.pallas_call(
        body, name="ctx_backward",
        out_shape=[jax.ShapeDtypeStruct((D, 2 * AW), F32), jax.ShapeDtypeStruct((8, D), F32),
                   jax.ShapeDtypeStruct((2, 2, NH, CH, CH), F32)],
        in_specs=[pl.BlockSpec(memory_space=pltpu.VMEM)] * 6,
        out_specs=[pl.BlockSpec(memory_space=pltpu.VMEM)] * 3,
        compiler_params=_params(56),
    )(ctx, vecs, wi, dm, dscf, dscb)


def _in_proj(x, vecs, wi, gbufs):
    ln = x.shape[0]
    t = min(512, ln)
    nt = len(gbufs)
    steps = ln // t

    def body(x_ref, v_ref, wi_ref, *refs):
        z_ref, hx_ref = refs[nt:nt + 2]
        bufs = refs[nt + 2:2 * nt + 2]
        send_sems, recv_sems = refs[2 * nt + 2:]
        i = pl.program_id(0)

        @pl.when(i == 0)
        def _():
            for cp in _gather_ici_copies(bufs, send_sems, recv_sems)[0]:
                cp.start()

        xv = x_ref[...]
        hx = (xv * _rms(xv) * v_ref[0:1, :]) * (1.0 + v_ref[2:3, :]) + v_ref[1:2, :]
        hb = hx.astype(BF)
        hx_ref[...] = hb
        for j in range(NCHIP):
            z_ref[:, j * WI_C:(j + 1) * WI_C] = _dot(hb, wi_ref[j], NN)

        @pl.when(i == steps - 1)
        def _():
            out_cp, in_cp = _gather_ici_copies(bufs, send_sems, recv_sems)
            for cp in in_cp:
                cp.wait_recv()
            for cp in out_cp:
                cp.wait_send()

    hbm = pl.BlockSpec(memory_space=pl.ANY)
    out = pl.pallas_call(
        body, name="in_proj", grid=(steps,),
        in_specs=[pl.BlockSpec((t, D), lambda i: (i, 0)), _const((8, D)), _const((NCHIP, D, WI_C))] + [hbm] * nt,
        out_specs=[pl.BlockSpec((t, IN_COLS), lambda i: (i, 0)), pl.BlockSpec((t, D), lambda i: (i, 0))] + [hbm] * nt,
        out_shape=[jax.ShapeDtypeStruct((ln, IN_COLS), F32), jax.ShapeDtypeStruct((ln, D), BF)]
        + [jax.ShapeDtypeStruct(g.shape, g.dtype) for g in gbufs],
        input_output_aliases={3 + k: 2 + k for k in range(nt)},
        scratch_shapes=[pltpu.SemaphoreType.DMA(((NCHIP - 1) * nt,))] * 2,
        compiler_params=_params(56, ("arbitrary",)),
    )(x, vecs, wi, *gbufs)
    return out[0], out[1], out[2:]


def _in_proj_bwd(dz, x, dx1, vecs, wi, parts):
    ln = x.shape[0]
    t = min(512, ln)
    nt = len(parts)
    steps = ln // t

    def body(dz_ref, x_ref, dx1_ref, v_ref, wi_ref, *refs):
        ps = refs[:nt]
        gx_ref, acc_ref = refs[nt:nt + 2]
        got = refs[nt + 2:2 * nt + 2]
        send_sems, recv_sems = refs[2 * nt + 2:]

        @pl.when(pl.program_id(0) == 0)
        def _():
            acc_ref[...] = jnp.zeros_like(acc_ref)
            for cp in _scatter_ici_copies(ps, got, send_sems, recv_sems):
                cp.start()

        dhx = jnp.zeros((t, D), F32)
        for j in range(NCHIP):
            dhx = dhx + _dot(dz_ref[:, j * WI_C:(j + 1) * WI_C], wi_ref[j], NT)
        xv = x_ref[...]
        r = _rms(xv)
        xn = xv * r
        n1, sc = v_ref[0:1, :], v_ref[2:3, :]
        acc_ref[0:1, :] += jnp.sum(dhx * xn * (1.0 + sc), axis=0, keepdims=True)
        acc_ref[1:2, :] += jnp.sum(dhx, axis=0, keepdims=True)
        acc_ref[2:3, :] += jnp.sum(dhx * xn * n1, axis=0, keepdims=True)
        g = dhx * n1 * (1.0 + sc)
        gx_ref[...] = dx1_ref[...] + r * (g - xn * jnp.mean(g * xn, axis=-1, keepdims=True))

        @pl.when(pl.program_id(0) == steps - 1)
        def _():
            cps = _scatter_ici_copies(ps, got, send_sems, recv_sems)
            for cp in cps:
                cp.wait_recv()
            for cp in cps:
                cp.wait_send()

    hbm = pl.BlockSpec(memory_space=pl.ANY)
    out = pl.pallas_call(
        body, name="in_proj_bwd", grid=(steps,),
        in_specs=[pl.BlockSpec((t, IN_COLS), lambda i: (i, 0)), pl.BlockSpec((t, D), lambda i: (i, 0)),
                  pl.BlockSpec((t, D), lambda i: (i, 0)), _const((8, D)), _const((NCHIP, D, WI_C))] + [hbm] * nt,
        out_specs=[pl.BlockSpec((t, D), lambda i: (i, 0)), pl.BlockSpec((8, D), lambda i: (0, 0))] + [hbm] * nt,
        out_shape=[jax.ShapeDtypeStruct((ln, D), F32), jax.ShapeDtypeStruct((8, D), F32)]
        + [jax.ShapeDtypeStruct((NCHIP - 1,) + p.shape[1:], BF) for p in parts],
        scratch_shapes=[pltpu.SemaphoreType.DMA(((NCHIP - 1) * nt,))] * 2,
        compiler_params=_params(56, ("arbitrary",)),
    )(dz, x, dx1, vecs, wi, *parts)
    return out[0], out[1], out[2:]


def _post_mixer(x, ycat, tgt, vecs, wo, wg, wu, wd):
    ln = x.shape[0]
    t = min(256, ln)

    def body(x_ref, y_ref, t_ref, v_ref, wo_ref, wg_ref, wu_ref, wd_ref,
             dx1_ref, dyc_ref, h2_ref, dy_ref, df_ref, act_ref, da_ref, db_ref, acc_ref, a_st, b_st):
        @pl.when(pl.program_id(0) == 0)
        def _():
            acc_ref[...] = jnp.zeros_like(acc_ref)

        g1, n2, sh2, sc2 = v_ref[0:1, :], v_ref[1:2, :], v_ref[2:3, :], v_ref[3:4, :]
        g2, nf = v_ref[4:5, :], v_ref[5:6, :]
        y = _dot(y_ref[...], wo_ref[...], NN)
        x1 = x_ref[...] + g1 * y
        r2 = _rms(x1)
        xn2 = x1 * r2
        t2 = xn2 * n2
        h2b = (t2 * (1.0 + sc2) + sh2).astype(BF)
        h2_ref[...] = h2b
        f = jnp.zeros((t, D), F32)
        for j in range(NCHIP):
            a = _dot(h2b, wg_ref[j], NT)
            b = _dot(h2b, wu_ref[j], NT)
            a_st[j] = a
            b_st[j] = b
            act = (_silu(a) * b).astype(BF)
            act_ref[j] = act
            f = f + _dot(act, wd_ref[j], NN)
        x2 = x1 + g2 * f
        r3 = _rms(x2)
        xn3 = x2 * r3
        e = xn3 * nf - t_ref[...]
        acc_ref[6:7, :] += jnp.sum(e * e, axis=0, keepdims=True) * (0.5 / D)
        dout = e * (1.0 / D)
        acc_ref[5:6, :] += jnp.sum(dout * xn3, axis=0, keepdims=True)
        gg = dout * nf
        dx2 = r3 * (gg - xn3 * jnp.mean(gg * xn3, axis=-1, keepdims=True))
        acc_ref[4:5, :] += jnp.sum(dx2 * f, axis=0, keepdims=True)
        dfb = (g2 * dx2).astype(BF)
        df_ref[...] = dfb
        dh2 = jnp.zeros((t, D), F32)
        for j in range(NCHIP):
            dact = _dot(dfb, wd_ref[j], NT)
            a = a_st[j]
            b = b_st[j]
            s = jax.nn.sigmoid(a)
            da = (dact * b * (s * (1.0 + a * (1.0 - s)))).astype(BF)
            db = (dact * (a * s)).astype(BF)
            da_ref[j] = da
            db_ref[j] = db
            dh2 = dh2 + _dot(da, wg_ref[j], NN) + _dot(db, wu_ref[j], NN)
        acc_ref[2:3, :] += jnp.sum(dh2, axis=0, keepdims=True)
        acc_ref[3:4, :] += jnp.sum(dh2 * t2, axis=0, keepdims=True)
        acc_ref[1:2, :] += jnp.sum(dh2 * xn2 * (1.0 + sc2), axis=0, keepdims=True)
        gx = dh2 * n2 * (1.0 + sc2)
        dx1 = dx2 + r2 * (gx - xn2 * jnp.mean(gx * xn2, axis=-1, keepdims=True))
        dx1_ref[...] = dx1
        acc_ref[0:1, :] += jnp.sum(dx1 * y, axis=0, keepdims=True)
        dyb = (g1 * dx1).astype(BF)
        dy_ref[...] = dyb
        dyc_ref[...] = _dot(dyb, wo_ref[...], NT)

    tok = pl.BlockSpec((t, D), lambda i: (i, 0))
    ffb = pl.BlockSpec((NCHIP, t, FF_C), lambda i: (0, i, 0))
    return pl.pallas_call(
        body, name="post_mixer", grid=(ln // t,),
        in_specs=[tok, tok, tok, _const((8, D)), _const((D, D)), _const((NCHIP, FF_C, D)),
                  _const((NCHIP, FF_C, D)), _const((NCHIP, FF_C, D))],
        out_specs=[tok, tok, tok, tok, tok, ffb, ffb, ffb, pl.BlockSpec((16, D), lambda i: (0, 0))],
        out_shape=[jax.ShapeDtypeStruct((ln, D), F32)] * 2 + [jax.ShapeDtypeStruct((ln, D), BF)] * 3
        + [jax.ShapeDtypeStruct((NCHIP, ln, FF_C), BF)] * 3 + [jax.ShapeDtypeStruct((16, D), F32)],
        scratch_shapes=[pltpu.VMEM((NCHIP, t, FF_C), F32)] * 2,
        compiler_params=_params(60, ("arbitrary",)),
    )(x, ycat, tgt, vecs, wo, wg, wu, wd)


def _tn_matmul(xa, dy, name, nb, k1, n, x_batched, dy_mode, tt, ctx_kv=None):
    ln = xa.shape[-2]
    tt = min(tt, ln)

    def body(x_ref, dy_ref, *refs):
        o_ref = refs[-1]

        @pl.when(pl.program_id(0) == 0)
        def _():
            o_ref[...] = jnp.zeros_like(o_ref)
            if ctx_kv is not None:
                for g in range(0, 2 * AW, HD):
                    col = 3 * AW + g
                    o_ref[col // n, :, col % n: col % n + HD] = refs[0][:, g:g + HD]

        xt = None if x_batched else jnp.transpose(x_ref[...])
        for b in range(nb):
            lhs = jnp.transpose(x_ref[b]) if x_batched else xt
            if dy_mode == "batched":
                rhs = dy_ref[b]
            elif dy_mode == "cols":
                rhs = dy_ref[:, b * n:(b + 1) * n]
            else:
                rhs = dy_ref[...]
            o_ref[b] += _dot(lhs, rhs, NN)

    x_spec = (pl.BlockSpec((nb, tt, k1), lambda t: (0, t, 0)) if x_batched
              else pl.BlockSpec((tt, k1), lambda t: (t, 0)))
    if dy_mode == "batched":
        dy_spec = pl.BlockSpec((nb, tt, n), lambda t: (0, t, 0))
    elif dy_mode == "cols":
        dy_spec = pl.BlockSpec((tt, nb * n), lambda t: (t, 0))
    else:
        dy_spec = pl.BlockSpec((tt, n), lambda t: (t, 0))
    extra = [] if ctx_kv is None else [ctx_kv]
    return pl.pallas_call(
        body, name=name, grid=(ln // tt,),
        in_specs=[x_spec, dy_spec] + [_const(e.shape) for e in extra],
        out_specs=pl.BlockSpec((nb, k1, n), lambda t: (0, 0, 0)),
        out_shape=jax.ShapeDtypeStruct((nb, k1, n), F32),
        compiler_params=_params(60, ("arbitrary",)),
    )(xa, dy, *extra)


def _add_ctx_cols(gwi, gwkv):
    first = 1536 // HD
    per = WI_C // HD

    def body(g_ref, a_ref, o_ref):
        o_ref[...] = g_ref[...] + a_ref[...]

    spec = pl.BlockSpec((None, D, HD), lambda i: ((first + i) // per, 0, (first + i) % per))
    return pl.pallas_call(
        body, name="add_ctx_cols", grid=(2 * AW // HD,),
        in_specs=[spec, pl.BlockSpec((D, HD), lambda i: (0, i))],
        out_specs=spec,
        out_shape=jax.ShapeDtypeStruct(gwi.shape, F32),
        input_output_aliases={0: 0},
        compiler_params=_params(32, ("arbitrary",)),
    )(gwi, gwkv)


def _mixer_fwd(z, cos_t, sin_t, dm, sgw, gain, bfull, scf, scb, gbufs_a, gbufs_b):
    ln = z.shape[0]
    nc = ln // CH
    na = len(gbufs_a)
    gbufs = list(gbufs_a) + list(gbufs_b)
    nt = len(gbufs)
    mid = nc // 2

    def rev(p, n):
        return p * n + (1 - p) * (nc - 1 - n)

    def col(j, both):
        if both:
            return pl.BlockSpec((CH, AW), lambda p, n: (rev(p, n), j))
        return pl.BlockSpec((CH, AW), lambda p, n: (p * n, j))

    def body(u_ref, v_ref, q_ref, k_ref, vr_ref, gf_ref, gb_ref, cos_ref, sin_ref, dm_ref, sgw_ref, gain_ref,
             bfull_ref, scf_ref, scb_ref, *refs):
        y_ref, sf_ref, sb_ref = refs[nt:nt + 3]
        bufs = refs[nt + 3:2 * nt + 3]
        bufs_a, bufs_b = bufs[:na], bufs[na:]
        sb_all, st, a_send, a_recv, bi_send, bi_recv, bd_send, bd_recv = refs[2 * nt + 3:]
        p, n = pl.program_id(0), pl.program_id(1)

        @pl.when((p == 0) & (n == 0))
        def _():
            for cp in _gather_d2d_copies(bufs_a, a_send, a_recv)[0]:
                cp.start()
            for cp in _gather_ici_copies(bufs_b, bi_send, bi_recv)[0]:
                cp.start()

        @pl.when((p == 1) & (n == mid))
        def _():
            for cp in _gather_ici_copies(bufs_b, bi_send, bi_recv)[1]:
                cp.wait_recv()
            for cp in _gather_d2d_copies(bufs_b, bd_send, bd_recv)[0]:
                cp.start()

        cos, sin = cos_ref[...], sin_ref[...]
        k = [_rope(t, cos, sin) * K_SCALE for t in _heads(k_ref)]
        vr = _heads(vr_ref)

        @pl.when(p == 0)
        def _():
            @pl.when(n == 0)
            def _():
                st[...] = scb_ref[...]

            m = nc - 1 - n
            for h in range(NH):
                sb_all[m, h] = st[h]
                st[h] = dm_ref[1, 3, h] * st[h] + _mm_tn(k[h], dm_ref[1, 2, h] * vr[h])

        @pl.when(p == 1)
        def _():
            @pl.when(n == 0)
            def _():
                st[...] = scf_ref[...]

            q = [_rope(t, cos, sin) for t in _heads(q_ref)]
            sf = [st[h] for h in range(NH)]
            sb = [sb_all[n, h].astype(F32) for h in range(NH)]
            mats = [[dm_ref[d, kind, h] for h in range(NH)] for d in range(2) for kind in range(3)]
            ya, yr, uf, _ = _chunk_fwd(
                _heads(u_ref), _heads(v_ref), q, k, vr, _heads(gf_ref), _heads(gb_ref), sf, sb,
                mats[0], mats[1], mats[2], mats[3], mats[4], mats[5],
                [sgw_ref[g] for g in range(NH)], [gain_ref[:, g * HD:(g + 1) * HD] for g in range(NH)],
                [bfull_ref[g] for g in range(NH)])
            for h in range(NH):
                y_ref[:, h * HD:(h + 1) * HD] = ya[h].astype(BF)
                y_ref[:, AW + h * HD:AW + (h + 1) * HD] = yr[h].astype(BF)
                sf_ref[0, h] = sf[h]
                sb_ref[0, h] = sb_all[n, h]
                st[h] = dm_ref[0, 3, h] * st[h] + uf[h]

        @pl.when((p == 1) & (n == nc - 1))
        def _():
            a_out, a_in = _gather_d2d_copies(bufs_a, a_send, a_recv)
            b_out, b_in = _gather_d2d_copies(bufs_b, bd_send, bd_recv)
            for cp in a_in + b_in:
                cp.wait_recv()
            for cp in a_out + b_out + _gather_ici_copies(bufs_b, bi_send, bi_recv)[0]:
                cp.wait_send()

    hbm = pl.BlockSpec(memory_space=pl.ANY)
    tab = pl.BlockSpec((CH, HD), lambda p, n: (rev(p, n), 0))
    st_spec = pl.BlockSpec((1, NH, HD, HD), lambda p, n: (p * n, 0, 0, 0))
    out = pl.pallas_call(
        body, name="mixer_fwd", grid=(2, nc),
        in_specs=[col(0, False), col(1, False), col(2, False), col(3, True), col(4, True), col(5, False),
                  col(6, False), tab, tab, _const((2, 4, NH, CH, CH)), _const((NH, CH, CH)), _const((1, AW)),
                  _const((NH, CH, CH)), _const((NH, HD, HD)), _const((NH, HD, HD))] + [hbm] * nt,
        out_specs=[pl.BlockSpec((CH, D), lambda p, n: (p * n, 0)), st_spec, st_spec] + [hbm] * nt,
        out_shape=[jax.ShapeDtypeStruct((ln, D), BF), jax.ShapeDtypeStruct((nc, NH, HD, HD), F32),
                   jax.ShapeDtypeStruct((nc, NH, HD, HD), F32)]
        + [jax.ShapeDtypeStruct(g.shape, g.dtype) for g in gbufs],
        input_output_aliases={15 + k: 3 + k for k in range(nt)},
        scratch_shapes=[pltpu.VMEM((nc, NH, HD, HD), F32), pltpu.VMEM((NH, HD, HD), F32)]
        + [pltpu.SemaphoreType.DMA(((NCHIP - 1) * na,))] * 2
        + [pltpu.SemaphoreType.DMA(((NCHIP - 1) * (nt - na),))] * 4,
        compiler_params=_params(56, ("arbitrary", "arbitrary")),
    )(z, z, z, z, z, z, z, cos_t, sin_t, dm, sgw, gain, bfull, scf, scb, *gbufs)
    return out[0], out[1], out[2], out[3:]


def _mixer_bwd(z, dycat, cos_t, sin_t, dm, sgw, gain, bfull, sf_all, sb_all, parts):
    ln = z.shape[0]
    nc = ln // CH

    def rev(p, n):
        return p * n + (1 - p) * (nc - 1 - n)

    def col(j, both):
        if both:
            return pl.BlockSpec((CH, AW), lambda p, n: (rev(p, n), j))
        return pl.BlockSpec((CH, AW), lambda p, n: (p * n, j))

    nt = len(parts)

    def body(u_ref, v_ref, q_ref, k_ref, vr_ref, gf_ref, gb_ref, dya_ref, dyr_ref, cos_ref, sin_ref, dm_ref,
             sgw_ref, gain_ref, bfull_ref, sf_ref, sb_ref, *refs):
        ps = refs[:nt]
        dz_ref, ddm_ref, dsgw_ref, dgain_ref, dbf_ref, dscf_ref, dscb_ref = refs[nt:nt + 7]
        got = refs[nt + 7:2 * nt + 7]
        gf_all, run, send_sems, recv_sems = refs[2 * nt + 7:]
        p, n = pl.program_id(0), pl.program_id(1)

        @pl.when((p == 0) & (n == 0))
        def _():
            for cp in _scatter_ici_copies(ps, got, send_sems, recv_sems):
                cp.start()

        cos, sin = cos_ref[...], sin_ref[...]
        q = [_rope(t, cos, sin) for t in _heads(q_ref)]
        k = [_rope(t, cos, sin) * K_SCALE for t in _heads(k_ref)]
        vr = _heads(vr_ref)
        gf = _heads(gf_ref)
        dyr = _heads(dyr_ref)
        sf = [sf_ref[0, h].astype(F32) for h in range(NH)]
        mats = [[dm_ref[d, kind, h] for h in range(NH)] for d in range(2) for kind in range(3)]

        @pl.when(p == 0)
        def _():
            @pl.when(n == 0)
            def _():
                run[...] = jnp.zeros_like(run)
                ddm_ref[...] = jnp.zeros_like(ddm_ref)
                dsgw_ref[...] = jnp.zeros_like(dsgw_ref)
                dgain_ref[...] = jnp.zeros_like(dgain_ref)
                dbf_ref[...] = jnp.zeros_like(dbf_ref)

            m = nc - 1 - n
            _, vjp = jax.vjp(lambda s: _fwd_dir_only(s, q, k, vr, gf, mats[0], mats[1]), sf)
            (dsf,) = vjp(dyr)
            for h in range(NH):
                gf_all[m, h] = run[h]
                run[h] = dsf[h] + dm_ref[0, 3, h] * run[h]

            @pl.when(n == nc - 1)
            def _():
                dscf_ref[...] = run[...]

        @pl.when(p == 1)
        def _():
            @pl.when(n == 0)
            def _():
                run[...] = jnp.zeros_like(run)

            sb = [sb_ref[0, h].astype(F32) for h in range(NH)]
            g_f = [gf_all[n, h].astype(F32) for h in range(NH)]
            g_b = [run[h] for h in range(NH)]
            args = (_heads(u_ref), _heads(v_ref), q, k, vr, gf, _heads(gb_ref), sb,
                    mats[0], mats[1], mats[2], mats[3], mats[4], mats[5],
                    [sgw_ref[g] for g in range(NH)], [gain_ref[:, g * HD:(g + 1) * HD] for g in range(NH)],
                    [bfull_ref[g] for g in range(NH)])

            def fn(u_, v_, q_, k_, vr_, gf_, gb_, sb_, df, xf, zf, db, xb, zb, sgw_, gain_, bfull_):
                return _chunk_fwd(u_, v_, q_, k_, vr_, gf_, gb_, sf, sb_, df, xf, zf, db, xb, zb, sgw_, gain_,
                                  bfull_)

            _, vjp = jax.vjp(fn, *args)
            (du, dv, dq, dk, dvr, dgf, dgb, dsb, ddf, dxf, dzf, ddb, dxb, dzb, dsgw, dgain, dbf) = vjp(
                (_heads(dya_ref), dyr, g_f, g_b))
            for h in range(NH):
                s = slice(h * HD, (h + 1) * HD)
                dz_ref[:, h * HD:(h + 1) * HD] = du[h].astype(BF)
                dz_ref[:, AW + h * HD:AW + (h + 1) * HD] = dv[h].astype(BF)
                dz_ref[:, 2 * AW + h * HD:2 * AW + (h + 1) * HD] = _rope_bwd(dq[h], cos, sin).astype(BF)
                dz_ref[:, 3 * AW + h * HD:3 * AW + (h + 1) * HD] = _rope_bwd(dk[h] * K_SCALE, cos, sin).astype(BF)
                dz_ref[:, 4 * AW + h * HD:4 * AW + (h + 1) * HD] = dvr[h].astype(BF)
                dz_ref[:, 5 * AW + h * HD:5 * AW + (h + 1) * HD] = dgf[h].astype(BF)
                dz_ref[:, 6 * AW + h * HD:6 * AW + (h + 1) * HD] = dgb[h].astype(BF)
                ddm_ref[0, 0, h] += ddf[h]
                ddm_ref[0, 1, h] += dxf[h]
                ddm_ref[0, 2, h] += dzf[h]
                ddm_ref[0, 3, h] += sf[h] * g_f[h]
                ddm_ref[1, 0, h] += ddb[h]
                ddm_ref[1, 1, h] += dxb[h]
                ddm_ref[1, 2, h] += dzb[h]
                ddm_ref[1, 3, h] += sb[h] * g_b[h]
                dsgw_ref[h] += dsgw[h]
                dgain_ref[:, s] += dgain[h]
                dbf_ref[h] += dbf[h]
                run[h] = dsb[h] + dm_ref[1, 3, h] * run[h]

            @pl.when(n == nc - 1)
            def _():
                dscb_ref[...] = run[...]

        @pl.when((p == 1) & (n == nc - 1))
        def _():
            cps = _scatter_ici_copies(ps, got, send_sems, recv_sems)
            for cp in cps:
                cp.wait_recv()
            for cp in cps:
                cp.wait_send()

    hbm = pl.BlockSpec(memory_space=pl.ANY)
    tab = pl.BlockSpec((CH, HD), lambda p, n: (rev(p, n), 0))
    tile4 = jax.ShapeDtypeStruct((NH, CH, CH), F32)
    out = pl.pallas_call(
        body, name="mixer_bwd", grid=(2, nc),
        in_specs=[col(0, False), col(1, False), col(2, True), col(3, True), col(4, True), col(5, True),
                  col(6, False),
                  pl.BlockSpec((CH, AW), lambda p, n: (p * n, 0)), pl.BlockSpec((CH, AW), lambda p, n: (rev(p, n), 1)),
                  tab, tab, _const((2, 4, NH, CH, CH)), _const((NH, CH, CH)), _const((1, AW)),
                  _const((NH, CH, CH)),
                  pl.BlockSpec((1, NH, HD, HD), lambda p, n: (rev(p, n), 0, 0, 0)),
                  pl.BlockSpec((1, NH, HD, HD), lambda p, n: (p * n, 0, 0, 0))] + [hbm] * nt,
        out_specs=[pl.BlockSpec((CH, IN_COLS), lambda p, n: (p * n, 0)),
                   pl.BlockSpec((2, 4, NH, CH, CH), lambda p, n: (0, 0, 0, 0, 0)),
                   pl.BlockSpec((NH, CH, CH), lambda p, n: (0, 0, 0)),
                   pl.BlockSpec((1, AW), lambda p, n: (0, 0)),
                   pl.BlockSpec((NH, CH, CH), lambda p, n: (0, 0, 0)),
                   pl.BlockSpec((NH, HD, HD), lambda p, n: (0, 0, 0)),
                   pl.BlockSpec((NH, HD, HD), lambda p, n: (0, 0, 0))] + [hbm] * nt,
        out_shape=[jax.ShapeDtypeStruct((ln, IN_COLS), BF), jax.ShapeDtypeStruct((2, 4, NH, CH, CH), F32),
                   tile4, jax.ShapeDtypeStruct((1, AW), F32), tile4, tile4, tile4]
        + [jax.ShapeDtypeStruct((NCHIP - 1,) + p.shape[1:], BF) for p in parts],
        scratch_shapes=[pltpu.VMEM((nc, NH, HD, HD), F32), pltpu.VMEM((NH, HD, HD), F32)]
        + [pltpu.SemaphoreType.DMA(((NCHIP - 1) * nt,))] * 2,
        compiler_params=_params(56, ("arbitrary", "arbitrary")),
    )(z, z, z, z, z, z, z, dycat, dycat, cos_t, sin_t, dm, sgw, gain, bfull, sf_all, sb_all, *parts)
    return out[:7], out[7:]


def _small_reduce(ddm, ddm_ctx, dm, dbf):
    def body(ddm_ref, dctx_ref, dm_ref, dbf_ref, lg_ref, sgb_ref):
        ex = _decay_exponents()
        ones = jnp.ones((8, CH), F32)
        for d in range(2):
            for h in range(NH):
                tot = jnp.zeros((CH, CH), F32)
                for kind in range(4):
                    g = ddm_ref[d, kind, h]
                    if kind >= 2:
                        g = g + dctx_ref[d, kind - 2, h]
                    tot = tot + g * dm_ref[d, kind, h] * ex[d][kind]
                lg_ref[d * NH + h: d * NH + h + 1, :] = jnp.sum(tot, axis=0, keepdims=True)
        for g in range(NH):
            r = lax.dot_general(ones, dbf_ref[g], (NT, ((), ())), precision=HI, preferred_element_type=F32)
            sgb_ref[g:g + 1, :] = r[0:1, :]

    return pl.pallas_call(
        body, name="small_reduce",
        out_shape=[jax.ShapeDtypeStruct((8, CH), F32), jax.ShapeDtypeStruct((NH, CH), F32)],
        in_specs=[pl.BlockSpec(memory_space=pltpu.VMEM)] * 4,
        out_specs=[pl.BlockSpec(memory_space=pltpu.VMEM)] * 2,
        compiler_params=_params(32),
    )(ddm, ddm_ctx, dm, dbf)


def _mod_backward(ct_pad_t, cctx_col, dmod_pad, dcmod_cols, w_mod_s):
    def body(ct_ref, cc_ref, dm_ref, dc_ref, w_ref, gw_ref, part_ref):
        dcm = dc_ref[0:1, :]
        for d in range(1, NDEV):
            dcm = dcm + dc_ref[d:d + 1, :]
        gw_ref[...] = (jnp.dot(_silu(ct_ref[...]), dm_ref[...], precision=HI, preferred_element_type=F32)
                       + _silu(cc_ref[...]) * dcm)
        part_ref[...] = lax.dot_general(jnp.broadcast_to(dcm, (8, dcm.shape[1])), w_ref[...], (NT, ((), ())),
                                        precision=HI, preferred_element_type=F32)

    return pl.pallas_call(
        body, name="mod_backward",
        out_shape=[jax.ShapeDtypeStruct(w_mod_s.shape, F32), jax.ShapeDtypeStruct((8, D), F32)],
        in_specs=[pl.BlockSpec(memory_space=pltpu.VMEM)] * 5,
        out_specs=[pl.BlockSpec(memory_space=pltpu.VMEM)] * 2,
        compiler_params=_params(48),
    )(ct_pad_t, cctx_col, dmod_pad, dcmod_cols, w_mod_s)


def _cctx_update(parts, c_ctx, m, v):
    def body(p_ref, c_ref, m_ref, v_ref, g_ref, d_ref, mo_ref, vo_ref):
        tot = ((p_ref[0] + p_ref[2]) + p_ref[4]) + p_ref[6]
        cv = c_ref[...]
        s = jax.nn.sigmoid(cv)
        g = tot * (s * (1.0 + cv * (1.0 - s)))
        g_ref[...] = g
        d_ref[...], mo_ref[...], vo_ref[...] = _adamw_math(cv, g, m_ref[...], v_ref[...])

    return pl.pallas_call(
        body, name="cctx_update",
        out_shape=[jax.ShapeDtypeStruct((1, D), F32)] * 4,
        in_specs=[pl.BlockSpec(memory_space=pltpu.VMEM)] * 4,
        out_specs=[pl.BlockSpec(memory_space=pltpu.VMEM)] * 4,
        compiler_params=_params(16),
    )(parts, c_ctx, m, v)


def _small_update(gathered, wp, mp, vp):
    def body(g_ref, w_ref, m_ref, v_ref, go_ref, d_ref, mo_ref, vo_ref, loss_ref):
        tot = g_ref[0]
        for d in range(1, NDEV):
            tot = tot + g_ref[d]
        go_ref[Q_BMOD:Q_N1, :] = tot[P_DMOD:P_N1, :] + tot[P_DCMOD:P_DMOD, :]
        go_ref[Q_N1:Q_LG, :] = tot[P_N1:P_LG, :]
        lg = jnp.sum(tot[P_LG:P_N2, :], axis=1, keepdims=True)
        go_ref[Q_LG:Q_N2, :] = lg * jax.nn.sigmoid(-w_ref[Q_LG:Q_N2, :])
        go_ref[Q_N2:Q_ROWS, :] = tot[P_N2:P_LOSS, :]
        d_ref[...], mo_ref[...], vo_ref[...] = _adamw_math(w_ref[...], go_ref[...], m_ref[...], v_ref[...])
        ls = jnp.sum(jnp.sum(tot[P_LOSS:P_ROWS, :], axis=1, keepdims=True), axis=0, keepdims=True)
        loss_ref[...] = jnp.broadcast_to(ls, (8, CH))

    return pl.pallas_call(
        body, name="small_update",
        out_shape=[jax.ShapeDtypeStruct((Q_ROWS, CH), F32)] * 4 + [jax.ShapeDtypeStruct((8, CH), F32)],
        in_specs=[pl.BlockSpec(memory_space=pltpu.VMEM)] * 4,
        out_specs=[pl.BlockSpec(memory_space=pltpu.VMEM)] * 5,
        compiler_params=_params(32),
    )(gathered, wp, mp, vp)


def _rows(a):
    return a.reshape(-1, CH)


def _pack_small(b_mod, norm1, sg_gain, sg_w, sg_b, lf, lb, norm2, norm_f):
    lg = jnp.broadcast_to(jnp.concatenate([lf.reshape(NH), lb.reshape(NH)])[:, None], (2 * NH, CH))
    return jnp.concatenate([_rows(b_mod), _rows(norm1), _rows(sg_gain), _rows(sg_w), _rows(sg_b), lg,
                            _rows(norm2), _rows(norm_f)], axis=0)


def _unpack_small(p):
    return (p[Q_BMOD:Q_N1].reshape(1, 6 * D), p[Q_N1:Q_GAIN].reshape(1, D), p[Q_GAIN:Q_SGW].reshape(1, AW),
            p[Q_SGW:Q_SGB].reshape(1, NH, CH, CH), p[Q_SGB:Q_LG].reshape(1, NH, CH),
            p[Q_LG:Q_LG + NH, 0].reshape(1, NH), p[Q_LG + NH:Q_N2, 0].reshape(1, NH),
            p[Q_N2:Q_NF].reshape(1, D), p[Q_NF:Q_ROWS].reshape(D))


def _rope_tables(ln):
    pos = np.arange(ln)
    rows = (pos // GRID_W).astype(np.float32)
    cols = (pos % GRID_W).astype(np.float32)
    n_freq = HD // 4
    inv = (np.float32(ROPE_BASE) ** (-np.arange(n_freq, dtype=np.float32) / np.float32(n_freq))).astype(np.float32)
    ar = rows[:, None] * inv[None, :]
    ac = cols[:, None] * inv[None, :]
    cos_t = np.concatenate([np.cos(ar), np.cos(ar), np.cos(ac), np.cos(ac)], axis=1).astype(np.float32)
    sin_t = np.concatenate([-np.sin(ar), np.sin(ar), -np.sin(ac), np.sin(ac)], axis=1).astype(np.float32)
    return jnp.asarray(cos_t), jnp.asarray(sin_t)


def kernel(x, c, ctx, c_ctx, w_mod, b_mod, norm1, w_in, sg_gain, sg_w, sg_b, ret_logit_f, ret_logit_b, w_out, norm2, w_gate, w_up, w_down, norm_f, loss_target, m_c_ctx, m_w_mod, m_b_mod, m_norm1, m_w_in, m_sg_gain, m_sg_w, m_sg_b, m_ret_logit_f, m_ret_logit_b, m_w_out, m_norm2, m_w_gate, m_w_up, m_w_down, m_norm_f, v_c_ctx, v_w_mod, v_b_mod, v_norm1, v_w_in, v_sg_gain, v_sg_w, v_sg_b, v_ret_logit_f, v_ret_logit_b, v_w_out, v_norm2, v_w_gate, v_w_up, v_w_down, v_norm_f):
    ln = x.shape[1]
    xi, yi, ci = _pos()
    chip = 2 * xi + yi
    me = 4 * xi + 2 * yi + ci
    x2d = x.reshape(ln, D)
    tgt = loss_target.reshape(ln, D)
    mod_c = w_mod.shape[2]

    tr = lambda a: jnp.swapaxes(a[0], 0, 1)
    gbufs, c_all, prod_all = _prologue(c, c_ctx.reshape(1, D), w_mod[0],
                                       [w_in[0], w_out[0], tr(w_gate), tr(w_up), w_down[0]])
    wi = gbufs[0].reshape(NCHIP, D, WI_C)
    gbufs_a, gbufs_b = gbufs[1:3], gbufs[3:5]
    c_all = c_all.reshape(NDEV, D)
    prod_chips = prod_all[0::2]
    mod_rows = jnp.transpose(prod_chips, (1, 0, 2)).reshape(16, NCHIP * mod_c) + b_mod
    mod = lax.dynamic_slice_in_dim(mod_rows, me, 1, axis=0)
    cmod = mod_rows[8:9]
    sh1, sc1, g1, sh2, sc2, g2 = [mod[:, i * D:(i + 1) * D] for i in range(6)]
    csh1, csc1 = cmod[:, 0:D], cmod[:, D:2 * D]
    zrow = jnp.zeros((1, D), F32)
    vec_in = jnp.concatenate([norm1, sh1, sc1] + [zrow] * 5, axis=0)
    vec_ctx = jnp.concatenate([norm1, csh1, csc1] + [zrow] * 5, axis=0)
    vec_post = jnp.concatenate([g1, norm2, sh2, sc2, g2, norm_f.reshape(1, D), zrow, zrow], axis=0)

    logits = jnp.concatenate([ret_logit_f.reshape(NH), ret_logit_b.reshape(NH)])
    dm = _decay_mats(jnp.broadcast_to(logits[:, None, None], (2 * NH, CH, CH)))
    ctx2d = ctx.reshape(ctx.shape[1], D)
    scf, scb = _ctx_forward(ctx2d, vec_ctx, wi, dm)

    cos_t, sin_t = _rope_tables(ln)
    z, hx, gbufs_a = _in_proj(x2d, vec_in, wi, gbufs_a)
    bfull = jnp.broadcast_to(sg_b[0][:, :, None], (NH, CH, CH))
    ycat, sf_all, sb_all, gbufs = _mixer_fwd(z, cos_t, sin_t, dm, sg_w[0], sg_gain, bfull, scf, scb,
                                             gbufs_a, gbufs_b)
    wo, wg_t, wu_t, wd = [g.reshape(NCHIP, 2 * g.shape[2], g.shape[3]) for g in gbufs]
    wo = wo.reshape(D, D)

    dx1, dycat, h2, dy, df, act, da, db, acc_post = _post_mixer(x2d, ycat, tgt, vec_post, wo, wg_t, wu_t, wd)

    cidx = ci.reshape(1).astype(jnp.int32)
    chipidx = chip.reshape(1).astype(jnp.int32)

    def halves_summed(full, names):
        full = [g.reshape(NCHIP, 2, g.shape[1] // 2, g.shape[2]) for g in full]
        from_sib = _rs_exchange_halves(full, "rs_exchange_" + names[0])
        return [_rs_add_halves(g, r, cidx, "rs_add_halves_" + nm) for g, r, nm in zip(full, from_sib, names)]

    g_wo = _tn_matmul(ycat, dy, "grad_w_out", 1, D, D, False, "shared", 1024).reshape(NCHIP, WO_R, D)
    g_wg = _tn_matmul(da, h2, "grad_w_gate", NCHIP, FF_C, D, True, "shared", 1024)
    g_wu = _tn_matmul(db, h2, "grad_w_up", NCHIP, FF_C, D, True, "shared", 1024)
    g_wd = _tn_matmul(act, df, "grad_w_down", NCHIP, FF_C, D, True, "shared", 1024)
    names = ["w_in", "w_out", "w_gate", "w_up", "w_down"]
    sums_b = halves_summed([g_wo, g_wg, g_wu, g_wd], names[1:])

    (dz, ddm, dsgw, dgain, dbf, dscf, dscb), from_chips_b = _mixer_bwd(
        z, dycat, cos_t, sin_t, dm, sg_w[0], sg_gain, bfull, sf_all, sb_all, [s[1] for s in sums_b])
    gwkv, acc_ctx, ddm_ctx = _ctx_backward(ctx2d, vec_ctx, wi, dm, dscf, dscb)
    g_wi = _tn_matmul(hx, dz, "grad_w_in", NCHIP, D, WI_C, False, "cols", 512, ctx_kv=gwkv)
    sums_a = halves_summed([g_wi], names[:1])
    gx, acc_in, from_chips_a = _in_proj_bwd(dz, x2d, dx1, vec_in, wi, [s[1] for s in sums_a])

    sums = sums_a + sums_b
    from_chips = list(from_chips_a) + list(from_chips_b)
    finals = [_rs_add_chips(s[0], r, chipidx, "rs_add_chips_" + nm) for s, r, nm in zip(sums, from_chips, names)]
    others = _rs_share_final(finals)

    lg_part, dsgb = _small_reduce(ddm, ddm_ctx, dm, dbf)
    dmod = jnp.concatenate([acc_in[1:2], acc_in[2:3], acc_post[0:1], acc_post[2:3], acc_post[3:4],
                            acc_post[4:5]], axis=1)
    dcmod = jnp.concatenate([acc_ctx[1:2], acc_ctx[2:3], jnp.zeros((1, 4 * D), F32)], axis=1)
    packed = jnp.concatenate([
        _rows(dcmod), _rows(dmod), _rows(acc_in[0:1] + acc_ctx[0:1]), _rows(dgain), _rows(dsgw), dsgb, lg_part,
        _rows(acc_post[1:2]), _rows(acc_post[5:6]), _rows(acc_post[6:7])], axis=0)
    gathered = _allgather_small(packed, "gather_small")
    dmod_all = gathered[:, P_DMOD:P_N1].reshape(NDEV, 6 * D)
    dcmod_all = gathered[:, P_DCMOD:P_DMOD].reshape(NDEV, 6 * D)
    dmod_cols = lax.dynamic_slice_in_dim(dmod_all, chip * mod_c, mod_c, axis=1)
    dcmod_cols = lax.dynamic_slice_in_dim(dcmod_all, chip * mod_c, mod_c, axis=1)
    dmod_pad = jnp.concatenate([dmod_cols, jnp.zeros((CH - NDEV, mod_c), F32)], axis=0)
    ct_pad_t = jnp.concatenate([jnp.transpose(c_all), jnp.zeros((D, CH - NDEV), F32)], axis=1)
    g_wmod, cctx_part = _mod_backward(ct_pad_t, c_ctx.reshape(D, 1), dmod_pad, dcmod_cols, w_mod[0])
    parts = _allgather_small(cctx_part[0:1], "gather_cctx")
    g_cctx, d_cctx, nm_cctx, nv_cctx = _cctx_update(parts, c_ctx.reshape(1, D), m_c_ctx.reshape(1, D),
                                                    v_c_ctx.reshape(1, D))

    wp = _pack_small(b_mod, norm1, sg_gain, sg_w, sg_b, ret_logit_f, ret_logit_b, norm2, norm_f)
    mp = _pack_small(m_b_mod, m_norm1, m_sg_gain, m_sg_w, m_sg_b, m_ret_logit_f, m_ret_logit_b, m_norm2, m_norm_f)
    vp = _pack_small(v_b_mod, v_norm1, v_sg_gain, v_sg_w, v_sg_b, v_ret_logit_f, v_ret_logit_b, v_norm2, v_norm_f)
    gp, dp, mp2, vp2, loss_t = _small_update(gathered, wp, mp, vp)

    big_w = [w_in[0], w_out[0], tr(w_gate), tr(w_up), w_down[0]]
    big_m = [m_w_in[0], m_w_out[0], tr(m_w_gate), tr(m_w_up), m_w_down[0]]
    big_v = [v_w_in[0], v_w_out[0], tr(v_w_gate), tr(v_w_up), v_w_down[0]]
    upd = [_adamw_halves(w, own, oth, m, v, cidx, "adamw_" + nm) for w, own, oth, m, v, nm in
           zip(big_w, finals, others, big_m, big_v, names)]
    big_g = [g_wmod] + [u[0] for u in upd]
    big = [_adamw(w_mod[0], g_wmod, m_w_mod[0], v_w_mod[0], "adamw_w_mod")] + [u[1:] for u in upd]

    def assemble(small, cctx, bigs):
        b_mod_, norm1_, gain_, sgw_, sgb_, lf_, lb_, norm2_, normf_ = _unpack_small(small)
        wm, wi_, wo_, wg_, wu_, wd_ = [b[None] for b in bigs]
        wg_, wu_ = jnp.swapaxes(wg_, 1, 2), jnp.swapaxes(wu_, 1, 2)
        return [cctx.reshape(D), wm, b_mod_, norm1_, wi_, gain_, sgw_, sgb_, lf_, lb_, wo_, norm2_, wg_, wu_, wd_,
                normf_]

    out = [loss_t[0, 0], gx.reshape(1, ln, D)]
    out += assemble(gp, g_cctx, big_g)
    out += assemble(dp, d_cctx, [b[0] for b in big])
    out += assemble(mp2, nm_cctx, [b[1] for b in big])
    out += assemble(vp2, nv_cctx, [b[2] for b in big])
    return tuple(out)
```

```python
import functools

import jax
import jax.numpy as jnp
import numpy as np
from jax import lax
from jax.experimental import pallas as pl
from jax.experimental.pallas import tpu as pltpu

F32 = jnp.float32
BF = jnp.bfloat16
MESH = pl.DeviceIdType.MESH

D = 1024
CH = 128
HD = 128
NH = 4
AW = 512
IN_COLS = 3584
DFF = 2816
NCHIP = 4
NDEV = 8
WI_C = IN_COLS // NCHIP
FF_C = DFF // NCHIP
WO_R = D // NCHIP
EPS = 1e-6
GRID_W = 64
ROPE_BASE = 10000.0
K_SCALE = HD ** -0.5
LR, B1, B2, AEPS, WD, STEP = 0.001, 0.9, 0.999, 1e-08, 0.01, 10
VMEM_MB = 1 << 20
HI = lax.Precision.HIGHEST

P_DCMOD, P_DMOD, P_N1, P_GAIN, P_SGW, P_SGB, P_LG, P_N2, P_NF, P_LOSS = 0, 48, 96, 104, 108, 620, 624, 632, 640, 648
P_ROWS = 656
Q_BMOD, Q_N1, Q_GAIN, Q_SGW, Q_SGB, Q_LG, Q_N2, Q_NF = 0, 48, 56, 60, 572, 576, 584, 592
Q_ROWS = 600


def _params(vmem_mb, sem=None):
    return pltpu.CompilerParams(vmem_limit_bytes=vmem_mb * VMEM_MB, dimension_semantics=sem)


def _const(shape):
    nd = len(shape)
    return pl.BlockSpec(shape, lambda *_: (0,) * nd, pipeline_mode=pl.Buffered(1))


def _pos():
    return lax.axis_index("x"), lax.axis_index("y"), lax.axis_index("c")


def _dot(a, b, dims):
    return lax.dot_general(a, b, (dims, ((), ())), preferred_element_type=F32)


NN = ((1,), (0,))
NT = ((1,), (1,))
TN = ((0,), (0,))


@jax.custom_vjp
def _mm(a, b):
    return _dot(a.astype(BF), b.astype(BF), NN)


def _mm_f(a, b):
    return _mm(a, b), (a.astype(BF), b.astype(BF))


def _mm_b(res, g):
    a, b = res
    gb = g.astype(BF)
    return _dot(gb, b, NT), _dot(a, gb, TN)


_mm.defvjp(_mm_f, _mm_b)


@jax.custom_vjp
def _mm_nt(a, b):
    return _dot(a.astype(BF), b.astype(BF), NT)


def _mm_nt_f(a, b):
    return _mm_nt(a, b), (a.astype(BF), b.astype(BF))


def _mm_nt_b(res, g):
    a, b = res
    gb = g.astype(BF)
    return _dot(gb, b, NN), _dot(gb, a, TN)


_mm_nt.defvjp(_mm_nt_f, _mm_nt_b)


@jax.custom_vjp
def _mm_tn(a, b):
    return _dot(a.astype(BF), b.astype(BF), TN)


def _mm_tn_f(a, b):
    return _mm_tn(a, b), (a.astype(BF), b.astype(BF))


def _mm_tn_b(res, g):
    a, b = res
    gb = g.astype(BF)
    return _dot(b, gb, NT), _dot(a, gb, NN)


_mm_tn.defvjp(_mm_tn_f, _mm_tn_b)


def _gelu(x):
    return x * (0.5 * (1.0 + jnp.tanh(0.7978845608028654 * (x + 0.044715 * (x * x * x)))))


def _silu(x):
    return x * jax.nn.sigmoid(x)


def _rms(x):
    return lax.rsqrt(jnp.mean(x * x, axis=-1, keepdims=True) + EPS)


def _swap32(t):
    lane = lax.broadcasted_iota(jnp.int32, t.shape, 1)
    first = (lane % 64) < 32
    return jnp.where(first, pltpu.roll(t, 96, 1), pltpu.roll(t, 32, 1))


def _rope(t, cos, sin):
    return t * cos + _swap32(t) * sin


def _rope_bwd(d, cos, sin):
    return d * cos + _swap32(d * sin)


def _heads(ref, r0=0):
    return [ref[r0:r0 + CH, h * HD:(h + 1) * HD].astype(F32) for h in range(NH)]


def _chunk_fwd(u, v, q, k, vr, gf, gb, sf, sb, df, xf, zf, db, xb, zb, sgw, gain, bfull):
    ya, yr, uf, ub = [], [], [], []
    for g in range(NH):
        gu = _gelu(u[g])
        gv = _gelu(v[g])
        vn = gv * _rms(gv) * gain[g]
        ya.append(gu * (_mm(sgw[g], vn) + bfull[g]))
    for h in range(NH):
        a = _mm_nt(q[h], k[h])
        of = _mm(a * df[h], vr[h]) + xf[h] * _mm(q[h], sf[h])
        ob = _mm(a * db[h], vr[h]) + xb[h] * _mm(q[h], sb[h])
        yr.append(_silu(gf[h]) * (of * _rms(of)) + _silu(gb[h]) * (ob * _rms(ob)))
        uf.append(_mm_tn(k[h], zf[h] * vr[h]))
        ub.append(_mm_tn(k[h], zb[h] * vr[h]))
    return ya, yr, uf, ub


def _fwd_dir_only(sf, q, k, vr, gf, df, xf):
    out = []
    for h in range(NH):
        a = _mm_nt(q[h], k[h])
        of = _mm(a * df[h], vr[h]) + xf[h] * _mm(q[h], sf[h])
        out.append(_silu(gf[h]) * (of * _rms(of)))
    return out


def _ctx_states(ctx0, ctx1, n1, csh, csc, wk, wv, zf, zb, ef, eb):
    hc0 = (ctx0 * _rms(ctx0) * n1) * (1.0 + csc) + csh
    hc1 = (ctx1 * _rms(ctx1) * n1) * (1.0 + csc) + csh
    scf, scb = [], []
    for h in range(NH):
        k0, k1 = _mm(hc0, wk[h]) * K_SCALE, _mm(hc1, wk[h]) * K_SCALE
        v0, v1 = _mm(hc0, wv[h]), _mm(hc1, wv[h])
        scf.append(ef[h] * _mm_tn(k0, zf[h] * v0) + _mm_tn(k1, zf[h] * v1))
        scb.append(eb[h] * _mm_tn(k1, zb[h] * v1) + _mm_tn(k0, zb[h] * v0))
    return scf, scb


def _allgather_small(v, name):
    r, n = v.shape

    def body(v_ref, out_ref, send_sems, recv_sems, local_sem):
        x, y, c = _pos()
        me = 4 * x + 2 * y + c
        mine = pltpu.make_async_copy(v_ref, out_ref.at[me], local_sem)
        mine.start()
        sent = []
        for k in range(1, NDEV):
            kx, ky, kc = (k >> 2) & 1, (k >> 1) & 1, k & 1
            peer = (x ^ kx, y ^ ky, c ^ kc)
            cp = pltpu.make_async_remote_copy(src_ref=v_ref, dst_ref=out_ref.at[me], send_sem=send_sems.at[k - 1],
                                              recv_sem=recv_sems.at[k - 1], device_id=peer, device_id_type=MESH)
            cp.start()
            sent.append(cp)
        for k in range(1, NDEV):
            kx, ky, kc = (k >> 2) & 1, (k >> 1) & 1, k & 1
            peer = (x ^ kx, y ^ ky, c ^ kc)
            src = 4 * (x ^ kx) + 2 * (y ^ ky) + (c ^ kc)
            pltpu.make_async_remote_copy(src_ref=v_ref, dst_ref=out_ref.at[src], send_sem=send_sems.at[k - 1],
                                         recv_sem=recv_sems.at[k - 1], device_id=peer, device_id_type=MESH).wait_recv()
        for cp in sent:
            cp.wait_send()
        mine.wait()

    return pl.pallas_call(
        body, name=name,
        out_shape=jax.ShapeDtypeStruct((NDEV, r, n), F32),
        in_specs=[pl.BlockSpec(memory_space=pltpu.VMEM)],
        out_specs=pl.BlockSpec(memory_space=pltpu.VMEM),
        scratch_shapes=[pltpu.SemaphoreType.DMA((NDEV - 1,)), pltpu.SemaphoreType.DMA((NDEV - 1,)),
                        pltpu.SemaphoreType.DMA],
        compiler_params=_params(16),
    )(v)


def _gather_weights(shards):
    nt = len(shards)
    shapes = [s.shape for s in shards]

    def body(*refs):
        srcs, outs, stages = refs[:nt], refs[nt:2 * nt], refs[2 * nt:3 * nt]
        ici_send, ici_recv, d2d_send, d2d_recv, local_sems = refs[3 * nt:]
        x, y, c = _pos()
        chip = 2 * x + y
        for t in range(nt):
            half = shapes[t][0] // 2
            stages[t][0] = srcs[t][0:half, :].astype(BF)
            stages[t][1] = srcs[t][half:2 * half, :].astype(BF)
        local = []
        for t in range(nt):
            cp = pltpu.make_async_copy(stages[t], outs[t].at[chip], local_sems.at[t])
            cp.start()
            local.append(cp)
        sent = []
        for k in range(1, NCHIP):
            kx, ky = (k >> 1) & 1, k & 1
            for t in range(nt):
                s = (k - 1) * nt + t
                cp = pltpu.make_async_remote_copy(
                    src_ref=stages[t].at[c], dst_ref=outs[t].at[chip, c], send_sem=ici_send.at[s],
                    recv_sem=ici_recv.at[s], device_id=(x ^ kx, y ^ ky, c), device_id_type=MESH)
                cp.start()
                sent.append(cp)
        for k in range(1, NCHIP):
            kx, ky = (k >> 1) & 1, k & 1
            src_chip = 2 * (x ^ kx) + (y ^ ky)
            for t in range(nt):
                s = (k - 1) * nt + t
                pltpu.make_async_remote_copy(
                    src_ref=stages[t].at[c], dst_ref=outs[t].at[src_chip, c], send_sem=ici_send.at[s],
                    recv_sem=ici_recv.at[s], device_id=(x ^ kx, y ^ ky, c), device_id_type=MESH).wait_recv()
                cp = pltpu.make_async_remote_copy(
                    src_ref=outs[t].at[src_chip, c], dst_ref=outs[t].at[src_chip, c], send_sem=d2d_send.at[s],
                    recv_sem=d2d_recv.at[s], device_id=(x, y, 1 - c), device_id_type=MESH)
                cp.start()
                sent.append(cp)
        for k in range(1, NCHIP):
            kx, ky = (k >> 1) & 1, k & 1
            src_chip = 2 * (x ^ kx) + (y ^ ky)
            for t in range(nt):
                s = (k - 1) * nt + t
                pltpu.make_async_remote_copy(
                    src_ref=stages[t].at[c], dst_ref=outs[t].at[src_chip, 1 - c], send_sem=d2d_send.at[s],
                    recv_sem=d2d_recv.at[s], device_id=(x, y, 1 - c), device_id_type=MESH).wait_recv()
        for cp in sent:
            cp.wait_send()
        for cp in local:
            cp.wait()

    n_rem = (NCHIP - 1) * nt
    out = pl.pallas_call(
        body, name="gather_weights",
        out_shape=[jax.ShapeDtypeStruct((NCHIP, 2, r // 2, cc), BF) for r, cc in shapes],
        in_specs=[pl.BlockSpec(memory_space=pltpu.VMEM)] * nt,
        out_specs=[pl.BlockSpec(memory_space=pl.ANY)] * nt,
        scratch_shapes=[pltpu.VMEM((2, r // 2, cc), BF) for r, cc in shapes]
        + [pltpu.SemaphoreType.DMA((n_rem,))] * 4 + [pltpu.SemaphoreType.DMA((nt,))],
        compiler_params=_params(48),
    )(*shards)
    return [o.reshape(NCHIP, r, cc) for o, (r, cc) in zip(out, shapes)]


def _chip_offsets():
    return [((k >> 1) & 1, k & 1) for k in range(1, NCHIP)]


def _prologue(c, c_ctx, w_mod_s, shards):
    nt = len(shards)
    shapes = [s.shape for s in shards]
    mod_c = w_mod_s.shape[1]

    def body(*refs):
        c_ref, cc_ref, wm_ref = refs[:3]
        srcs = refs[3:3 + nt]
        outs = refs[3 + nt:3 + 2 * nt]
        call_ref, prod_ref = refs[3 + 2 * nt:5 + 2 * nt]
        stages = refs[5 + 2 * nt:5 + 3 * nt]
        ct = refs[5 + 3 * nt]
        c_send, c_recv, p_send, p_recv, ici_send, ici_recv, d2d_send, d2d_recv, local_sems = refs[6 + 3 * nt:]
        x, y, c = _pos()
        chip = 2 * x + y
        me = 4 * x + 2 * y + c
        sib = (x, y, 1 - c)
        pending = []
        for t in range(nt):
            half = shapes[t][0] // 2
            stages[t][0] = srcs[t][0:half, :].astype(BF)
            stages[t][1] = srcs[t][half:2 * half, :].astype(BF)
            cp = pltpu.make_async_copy(stages[t], outs[t].at[chip], local_sems.at[t])
            cp.start()
            pending.append(cp)
        sends = []
        for k, (kx, ky) in enumerate(_chip_offsets()):
            cp = pltpu.make_async_remote_copy(src_ref=stages[0].at[c], dst_ref=outs[0].at[chip, c],
                                              send_sem=ici_send.at[k], recv_sem=ici_recv.at[k],
                                              device_id=(x ^ kx, y ^ ky, c), device_id_type=MESH)
            cp.start()
            sends.append(cp)

        def to_all(src, dst_of, send_sems, recv_sems):
            for k in range(1, NDEV):
                kx, ky, kc = (k >> 2) & 1, (k >> 1) & 1, k & 1
                cp = pltpu.make_async_remote_copy(src_ref=src, dst_ref=dst_of(me), send_sem=send_sems.at[k - 1],
                                                  recv_sem=recv_sems.at[k - 1], device_id=(x ^ kx, y ^ ky, c ^ kc),
                                                  device_id_type=MESH)
                cp.start()
                sends.append(cp)
            for k in range(1, NDEV):
                kx, ky, kc = (k >> 2) & 1, (k >> 1) & 1, k & 1
                frm = 4 * (x ^ kx) + 2 * (y ^ ky) + (c ^ kc)
                pltpu.make_async_remote_copy(src_ref=src, dst_ref=dst_of(frm), send_sem=send_sems.at[k - 1],
                                             recv_sem=recv_sems.at[k - 1], device_id=(x ^ kx, y ^ ky, c ^ kc),
                                             device_id_type=MESH).wait_recv()

        call_ref[me] = c_ref[...]
        to_all(c_ref, lambda d: call_ref.at[d], c_send, c_recv)
        ct[...] = jnp.zeros_like(ct)
        for d in range(NDEV):
            ct[d:d + 1, :] = call_ref[d]
        ct[NDEV:NDEV + 1, :] = cc_ref[...]
        prod_ref[me] = jnp.dot(_silu(ct[...]), wm_ref[...], precision=HI, preferred_element_type=F32)
        to_all(prod_ref.at[me], lambda d: prod_ref.at[d], p_send, p_recv)

        for k, (kx, ky) in enumerate(_chip_offsets()):
            frm = 2 * (x ^ kx) + (y ^ ky)
            pltpu.make_async_remote_copy(src_ref=stages[0].at[c], dst_ref=outs[0].at[frm, c],
                                         send_sem=ici_send.at[k], recv_sem=ici_recv.at[k],
                                         device_id=(x ^ kx, y ^ ky, c), device_id_type=MESH).wait_recv()
            cp = pltpu.make_async_remote_copy(src_ref=outs[0].at[frm, c], dst_ref=outs[0].at[frm, c],
                                              send_sem=d2d_send.at[k], recv_sem=d2d_recv.at[k],
                                              device_id=sib, device_id_type=MESH)
            cp.start()
            sends.append(cp)
        for k, (kx, ky) in enumerate(_chip_offsets()):
            frm = 2 * (x ^ kx) + (y ^ ky)
            pltpu.make_async_remote_copy(src_ref=stages[0].at[c], dst_ref=outs[0].at[frm, 1 - c],
                                         send_sem=d2d_send.at[k], recv_sem=d2d_recv.at[k],
                                         device_id=sib, device_id_type=MESH).wait_recv()
        for cp in sends:
            cp.wait_send()
        for cp in pending:
            cp.wait()

    vm = pl.BlockSpec(memory_space=pltpu.VMEM)
    out = pl.pallas_call(
        body, name="prologue",
        out_shape=[jax.ShapeDtypeStruct((NCHIP, 2, r // 2, cc), BF) for r, cc in shapes]
        + [jax.ShapeDtypeStruct((NDEV, 1, D), F32), jax.ShapeDtypeStruct((NDEV, 16, mod_c), F32)],
        in_specs=[vm] * (3 + nt),
        out_specs=[pl.BlockSpec(memory_space=pl.ANY)] * nt + [vm, vm],
        scratch_shapes=[pltpu.VMEM((2, r // 2, cc), BF) for r, cc in shapes] + [pltpu.VMEM((16, D), F32)]
        + [pltpu.SemaphoreType.DMA((NDEV - 1,))] * 4 + [pltpu.SemaphoreType.DMA((NCHIP - 1,))] * 4
        + [pltpu.SemaphoreType.DMA((nt,))],
        compiler_params=_params(56),
    )(c, c_ctx, w_mod_s, *shards)
    return out[:nt], out[nt], out[nt + 1]


def _gather_ici_copies(bufs, send_sems, recv_sems):
    x, y, c = _pos()
    chip = 2 * x + y
    nt = len(bufs)
    out_cp, in_cp = [], []
    for k, (kx, ky) in enumerate(_chip_offsets()):
        frm = 2 * (x ^ kx) + (y ^ ky)
        for t in range(nt):
            s = k * nt + t
            peer = (x ^ kx, y ^ ky, c)
            out_cp.append(pltpu.make_async_remote_copy(
                src_ref=bufs[t].at[chip, c], dst_ref=bufs[t].at[chip, c], send_sem=send_sems.at[s],
                recv_sem=recv_sems.at[s], device_id=peer, device_id_type=MESH))
            in_cp.append(pltpu.make_async_remote_copy(
                src_ref=bufs[t].at[chip, c], dst_ref=bufs[t].at[frm, c], send_sem=send_sems.at[s],
                recv_sem=recv_sems.at[s], device_id=peer, device_id_type=MESH))
    return out_cp, in_cp


def _gather_d2d_copies(bufs, send_sems, recv_sems):
    x, y, c = _pos()
    nt = len(bufs)
    out_cp, in_cp = [], []
    for k, (kx, ky) in enumerate(_chip_offsets()):
        frm = 2 * (x ^ kx) + (y ^ ky)
        for t in range(nt):
            s = k * nt + t
            out_cp.append(pltpu.make_async_remote_copy(
                src_ref=bufs[t].at[frm, c], dst_ref=bufs[t].at[frm, c], send_sem=send_sems.at[s],
                recv_sem=recv_sems.at[s], device_id=(x, y, 1 - c), device_id_type=MESH))
            in_cp.append(pltpu.make_async_remote_copy(
                src_ref=bufs[t].at[frm, c], dst_ref=bufs[t].at[frm, 1 - c], send_sem=send_sems.at[s],
                recv_sem=recv_sems.at[s], device_id=(x, y, 1 - c), device_id_type=MESH))
    return out_cp, in_cp


def _scatter_ici_copies(parts, outs, send_sems, recv_sems):
    x, y, c = _pos()
    nt = len(parts)
    cps = []
    for k, (kx, ky) in enumerate(_chip_offsets()):
        dst_chip = 2 * (x ^ kx) + (y ^ ky)
        for t in range(nt):
            s = k * nt + t
            cps.append(pltpu.make_async_remote_copy(
                src_ref=parts[t].at[dst_chip], dst_ref=outs[t].at[k], send_sem=send_sems.at[s],
                recv_sem=recv_sems.at[s], device_id=(x ^ kx, y ^ ky, c), device_id_type=MESH))
    return cps


def _rs_exchange_halves(grads, name):
    nt = len(grads)
    shapes = [g.shape for g in grads]

    def body(*refs):
        gs, outs = refs[:nt], refs[nt:2 * nt]
        send_sems, recv_sems = refs[2 * nt:]
        x, y, c = _pos()
        sib = (x, y, 1 - c)
        sent = []
        for t in range(nt):
            for j in range(NCHIP):
                s = t * NCHIP + j
                cp = pltpu.make_async_remote_copy(src_ref=gs[t].at[j, 1 - c], dst_ref=outs[t].at[j],
                                                  send_sem=send_sems.at[s], recv_sem=recv_sems.at[s],
                                                  device_id=sib, device_id_type=MESH)
                cp.start()
                sent.append(cp)
        for cp in sent:
            cp.wait_recv()
        for cp in sent:
            cp.wait_send()

    return pl.pallas_call(
        body, name=name,
        out_shape=[jax.ShapeDtypeStruct((NCHIP, s[2], s[3]), F32) for s in shapes],
        in_specs=[pl.BlockSpec(memory_space=pl.ANY)] * nt,
        out_specs=[pl.BlockSpec(memory_space=pl.ANY)] * nt,
        scratch_shapes=[pltpu.SemaphoreType.DMA((nt * NCHIP,))] * 2,
    )(*grads)


def _rs_send_chips(parts):
    nt = len(parts)
    shapes = [p.shape for p in parts]

    def body(*refs):
        ps, outs = refs[:nt], refs[nt:2 * nt]
        send_sems, recv_sems = refs[2 * nt:]
        x, y, c = _pos()
        sent = []
        for k in range(1, NCHIP):
            kx, ky = (k >> 1) & 1, k & 1
            dst_chip = 2 * (x ^ kx) + (y ^ ky)
            for t in range(nt):
                s = (k - 1) * nt + t
                cp = pltpu.make_async_remote_copy(src_ref=ps[t].at[dst_chip], dst_ref=outs[t].at[k - 1],
                                                  send_sem=send_sems.at[s], recv_sem=recv_sems.at[s],
                                                  device_id=(x ^ kx, y ^ ky, c), device_id_type=MESH)
                cp.start()
                sent.append(cp)
        for cp in sent:
            cp.wait_recv()
        for cp in sent:
            cp.wait_send()

    return pl.pallas_call(
        body, name="rs_send_chips",
        out_shape=[jax.ShapeDtypeStruct((NCHIP - 1, s[1], s[2]), BF) for s in shapes],
        in_specs=[pl.BlockSpec(memory_space=pl.ANY)] * nt,
        out_specs=[pl.BlockSpec(memory_space=pl.ANY)] * nt,
        scratch_shapes=[pltpu.SemaphoreType.DMA((nt * (NCHIP - 1),))] * 2,
    )(*parts)


def _rs_share_final(finals):
    nt = len(finals)
    shapes = [f.shape for f in finals]

    def body(*refs):
        fs, outs = refs[:nt], refs[nt:2 * nt]
        send_sems, recv_sems = refs[2 * nt:]
        x, y, c = _pos()
        sent = []
        for t in range(nt):
            cp = pltpu.make_async_remote_copy(src_ref=fs[t], dst_ref=outs[t], send_sem=send_sems.at[t],
                                              recv_sem=recv_sems.at[t], device_id=(x, y, 1 - c), device_id_type=MESH)
            cp.start()
            sent.append(cp)
        for cp in sent:
            cp.wait_recv()
        for cp in sent:
            cp.wait_send()

    return pl.pallas_call(
        body, name="rs_share_final",
        out_shape=[jax.ShapeDtypeStruct(s, F32) for s in shapes],
        in_specs=[pl.BlockSpec(memory_space=pl.ANY)] * nt,
        out_specs=[pl.BlockSpec(memory_space=pl.ANY)] * nt,
        scratch_shapes=[pltpu.SemaphoreType.DMA((nt,))] * 2,
    )(*finals)


def _row_tile(h, cc=D):
    for t in (512, 384, 352, 256, 176, 128, 64, 32, 16):
        if h % t == 0 and t * cc * 4 <= (5 * VMEM_MB) // 4:
            return t
    return h


def _rs_add_halves(g, recv, cidx, name):
    _, _, h, cc = g.shape
    th = _row_tile(h, cc)

    def body(c_ref, g_ref, r_ref, of_ref, ob_ref):
        s = g_ref[...] + r_ref[...]
        of_ref[...] = s
        ob_ref[...] = s.astype(BF)

    return pl.pallas_call(
        body, name=name,
        grid_spec=pltpu.PrefetchScalarGridSpec(
            num_scalar_prefetch=1, grid=(NCHIP, h // th),
            in_specs=[pl.BlockSpec((None, None, th, cc), lambda j, i, c_ref: (j, c_ref[0], i, 0)),
                      pl.BlockSpec((None, th, cc), lambda j, i, c_ref: (j, i, 0))],
            out_specs=[pl.BlockSpec((None, th, cc), lambda j, i, c_ref: (j, i, 0)),
                       pl.BlockSpec((None, th, cc), lambda j, i, c_ref: (j, i, 0))]),
        out_shape=[jax.ShapeDtypeStruct((NCHIP, h, cc), F32), jax.ShapeDtypeStruct((NCHIP, h, cc), BF)],
        compiler_params=_params(48),
    )(cidx, g, recv)


def _rs_add_chips(own, recv, chipidx, name):
    _, h, cc = own.shape
    th = _row_tile(h, cc)

    def body(j_ref, o_ref, r_ref, out_ref):
        out_ref[...] = ((o_ref[...] + r_ref[0].astype(F32)) + r_ref[1].astype(F32)) + r_ref[2].astype(F32)

    return pl.pallas_call(
        body, name=name,
        grid_spec=pltpu.PrefetchScalarGridSpec(
            num_scalar_prefetch=1, grid=(h // th,),
            in_specs=[pl.BlockSpec((None, th, cc), lambda i, j_ref: (j_ref[0], i, 0)),
                      pl.BlockSpec((NCHIP - 1, th, cc), lambda i, j_ref: (0, i, 0))],
            out_specs=pl.BlockSpec((th, cc), lambda i, j_ref: (i, 0))),
        out_shape=jax.ShapeDtypeStruct((h, cc), F32),
        compiler_params=_params(48),
    )(chipidx, own, recv)


def _adamw_math(w, g, m, v):
    m2 = B1 * m + (1.0 - B1) * g
    v2 = B2 * v + (1.0 - B2) * (g * g)
    m_hat = m2 / (1.0 - B1 ** STEP)
    v_hat = v2 / (1.0 - B2 ** STEP)
    delta = -LR * (m_hat / (jnp.sqrt(v_hat) + AEPS) + WD * w)
    return delta, m2, v2


def _adamw(w, g, m, v, name):
    r, cc = w.shape
    tr = _row_tile(r, cc)

    def body(w_ref, g_ref, m_ref, v_ref, d_ref, mo_ref, vo_ref):
        d, m2, v2 = _adamw_math(w_ref[...], g_ref[...], m_ref[...], v_ref[...])
        d_ref[...] = d
        mo_ref[...] = m2
        vo_ref[...] = v2

    spec = pl.BlockSpec((tr, cc), lambda i: (i, 0))
    return pl.pallas_call(
        body, name=name, grid=(r // tr,), in_specs=[spec] * 4, out_specs=[spec] * 3,
        out_shape=[jax.ShapeDtypeStruct((r, cc), F32)] * 3,
        compiler_params=_params(48, ("parallel",)),
    )(w, g, m, v)


def _adamw_halves(w, own, other, m, v, cidx, name):
    r, cc = w.shape
    h = r // 2
    tr = _row_tile(h, cc)
    per = h // tr

    def body(c_ref, w_ref, own_ref, oth_ref, m_ref, v_ref, g_ref, d_ref, mo_ref, vo_ref):
        mine = (pl.program_id(0) // per) == c_ref[0]
        g = jnp.where(mine, own_ref[...], oth_ref[...])
        g_ref[...] = g
        d, m2, v2 = _adamw_math(w_ref[...], g, m_ref[...], v_ref[...])
        d_ref[...] = d
        mo_ref[...] = m2
        vo_ref[...] = v2

    full = pl.BlockSpec((tr, cc), lambda i, c_ref: (i, 0))
    half = pl.BlockSpec((tr, cc), lambda i, c_ref: (i % per, 0))
    return pl.pallas_call(
        body, name=name,
        grid_spec=pltpu.PrefetchScalarGridSpec(
            num_scalar_prefetch=1, grid=(r // tr,),
            in_specs=[full, half, half, full, full], out_specs=[full] * 4),
        out_shape=[jax.ShapeDtypeStruct((r, cc), F32)] * 4,
        compiler_params=_params(48, ("parallel",)),
    )(cidx, w, own, other, m, v)


def _mod_forward(ct_pad, w_mod_s):
    def body(c_ref, w_ref, o_ref):
        o_ref[...] = jnp.dot(_silu(c_ref[...]), w_ref[...], precision=HI, preferred_element_type=F32)

    return pl.pallas_call(
        body, name="mod_forward",
        out_shape=jax.ShapeDtypeStruct((16, w_mod_s.shape[1]), F32),
        in_specs=[pl.BlockSpec(memory_space=pltpu.VMEM)] * 2,
        out_specs=pl.BlockSpec(memory_space=pltpu.VMEM),
        compiler_params=_params(32),
    )(ct_pad, w_mod_s)


def _decay_exponents():
    ri = lax.broadcasted_iota(jnp.int32, (CH, CH), 0).astype(F32)
    ci = lax.broadcasted_iota(jnp.int32, (CH, CH), 1).astype(F32)
    full = jnp.full((CH, CH), float(CH), F32)
    return [[ri - ci, ri + 1.0, (CH - 1.0) - ri, full], [ci - ri, CH - ri, ri, full]]


def _decay_mats(logit_full):
    def body(l_ref, o_ref):
        ex = _decay_exponents()
        for d in range(2):
            for h in range(NH):
                lv = l_ref[d * NH + h]
                lg = jnp.minimum(lv, 0.0) - jnp.log(1.0 + jnp.exp(-jnp.abs(lv)))
                for kind in range(4):
                    m = jnp.exp(lg * ex[d][kind])
                    if kind == 0:
                        m = jnp.where(ex[d][0] >= 0.0, jnp.exp(lg * jnp.maximum(ex[d][0], 0.0)), 0.0)
                    o_ref[d, kind, h] = m

    return pl.pallas_call(
        body, name="decay_mats",
        out_shape=jax.ShapeDtypeStruct((2, 4, NH, CH, CH), F32),
        in_specs=[pl.BlockSpec(memory_space=pltpu.VMEM)],
        out_specs=pl.BlockSpec(memory_space=pltpu.VMEM),
        compiler_params=_params(32),
    )(logit_full)


def _ctx_kv_weights(wi_ref):
    def cols(g):
        return wi_ref[g // WI_C, :, g % WI_C: g % WI_C + HD].astype(F32)

    wk = [cols(3 * AW + h * HD) for h in range(NH)]
    wv = [cols(4 * AW + h * HD) for h in range(NH)]
    return wk, wv


def _ctx_forward(ctx, vecs, wi, dm):
    def body(ctx_ref, v_ref, wi_ref, dm_ref, scf_ref, scb_ref):
        wk, wv = _ctx_kv_weights(wi_ref)
        mats = [[dm_ref[d, kind, h] for h in range(NH)] for d in range(2) for kind in (2, 3)]
        scf, scb = _ctx_states(ctx_ref[0:CH, :], ctx_ref[CH:2 * CH, :], v_ref[0:1, :], v_ref[1:2, :],
                               v_ref[2:3, :], wk, wv, mats[0], mats[2], mats[1], mats[3])
        for h in range(NH):
            scf_ref[h] = scf[h]
            scb_ref[h] = scb[h]

    return pl.pallas_call(
        body, name="ctx_forward",
        out_shape=[jax.ShapeDtypeStruct((NH, HD, HD), F32)] * 2,
        in_specs=[pl.BlockSpec(memory_space=pltpu.VMEM)] * 4,
        out_specs=[pl.BlockSpec(memory_space=pltpu.VMEM)] * 2,
        compiler_params=_params(48),
    )(ctx, vecs, wi, dm)


def _ctx_backward(ctx, vecs, wi, dm, dscf, dscb):
    def body(ctx_ref, v_ref, wi_ref, dm_ref, gf_ref, gb_ref, gw_ref, gv_ref, gdm_ref):
        wk, wv = _ctx_kv_weights(wi_ref)
        mats = [[dm_ref[d, kind, h] for h in range(NH)] for d in range(2) for kind in (2, 3)]
        ctx0, ctx1 = ctx_ref[0:CH, :], ctx_ref[CH:2 * CH, :]

        def fn(n1, csh, csc, wk_, wv_, zf, zb, ef, eb):
            return _ctx_states(ctx0, ctx1, n1, csh, csc, wk_, wv_, zf, zb, ef, eb)

        _, vjp = jax.vjp(fn, v_ref[0:1, :], v_ref[1:2, :], v_ref[2:3, :], wk, wv,
                         mats[0], mats[2], mats[1], mats[3])
        cot = ([gf_ref[h] for h in range(NH)], [gb_ref[h] for h in range(NH)])
        dn1, dcsh, dcsc, dwk, dwv, dzf, dzb, def_, deb = vjp(cot)
        for h in range(NH):
            gw_ref[:, h * HD:(h + 1) * HD] = dwk[h]
            gw_ref[:, AW + h * HD:AW + (h + 1) * HD] = dwv[h]
        gv_ref[...] = jnp.zeros_like(gv_ref)
        gv_ref[0:1, :] = dn1
        gv_ref[1:2, :] = dcsh
        gv_ref[2:3, :] = dcsc
        for h in range(NH):
            gdm_ref[0, 0, h] = dzf[h]
            gdm_ref[0, 1, h] = def_[h]
            gdm_ref[1, 0, h] = dzb[h]
            gdm_ref[1, 1, h] = deb[h]

    return pl.pallas_call(
        body, name="ctx_backward",
        out_shape=[jax.ShapeDtypeStruct((D, 2 * AW), F32), jax.ShapeDtypeStruct((8, D), F32),
                   jax.ShapeDtypeStruct((2, 2, NH, CH, CH), F32)],
        in_specs=[pl.BlockSpec(memory_space=pltpu.VMEM)] * 6,
        out_specs=[pl.BlockSpec(memory_space=pltpu.VMEM)] * 3,
        compiler_params=_params(56),
    )(ctx, vecs, wi, dm, dscf, dscb)


def _in_proj(x, vecs, wi, gbufs):
    ln = x.shape[0]
    t = min(512, ln)
    nt = len(gbufs)
    steps = ln // t

    def body(x_ref, v_ref, wi_ref, *refs):
        z_ref, hx_ref = refs[nt:nt + 2]
        bufs = refs[nt + 2:2 * nt + 2]
        send_sems, recv_sems = refs[2 * nt + 2:]
        i = pl.program_id(0)

        @pl.when(i == 0)
        def _():
            for cp in _gather_ici_copies(bufs, send_sems, recv_sems)[0]:
                cp.start()

        xv = x_ref[...]
        hx = (xv * _rms(xv) * v_ref[0:1, :]) * (1.0 + v_ref[2:3, :]) + v_ref[1:2, :]
        hb = hx.astype(BF)
        hx_ref[...] = hb
        for j in range(NCHIP):
            z_ref[:, j * WI_C:(j + 1) * WI_C] = _dot(hb, wi_ref[j], NN)

        @pl.when(i == steps - 1)
        def _():
            out_cp, in_cp = _gather_ici_copies(bufs, send_sems, recv_sems)
            for cp in in_cp:
                cp.wait_recv()
            for cp in out_cp:
                cp.wait_send()

    hbm = pl.BlockSpec(memory_space=pl.ANY)
    out = pl.pallas_call(
        body, name="in_proj", grid=(steps,),
        in_specs=[pl.BlockSpec((t, D), lambda i: (i, 0)), _const((8, D)), _const((NCHIP, D, WI_C))] + [hbm] * nt,
        out_specs=[pl.BlockSpec((t, IN_COLS), lambda i: (i, 0)), pl.BlockSpec((t, D), lambda i: (i, 0))] + [hbm] * nt,
        out_shape=[jax.ShapeDtypeStruct((ln, IN_COLS), F32), jax.ShapeDtypeStruct((ln, D), BF)]
        + [jax.ShapeDtypeStruct(g.shape, g.dtype) for g in gbufs],
        input_output_aliases={3 + k: 2 + k for k in range(nt)},
        scratch_shapes=[pltpu.SemaphoreType.DMA(((NCHIP - 1) * nt,))] * 2,
        compiler_params=_params(56, ("arbitrary",)),
    )(x, vecs, wi, *gbufs)
    return out[0], out[1], out[2:]


def _in_proj_bwd(dz, x, dx1, vecs, wi, parts):
    ln = x.shape[0]
    t = min(512, ln)
    nt = len(parts)
    steps = ln // t

    def body(dz_ref, x_ref, dx1_ref, v_ref, wi_ref, *refs):
        ps = refs[:nt]
        gx_ref, acc_ref = refs[nt:nt + 2]
        got = refs[nt + 2:2 * nt + 2]
        send_sems, recv_sems = refs[2 * nt + 2:]

        @pl.when(pl.program_id(0) == 0)
        def _():
            acc_ref[...] = jnp.zeros_like(acc_ref)
            for cp in _scatter_ici_copies(ps, got, send_sems, recv_sems):
                cp.start()

        dhx = jnp.zeros((t, D), F32)
        for j in range(NCHIP):
            dhx = dhx + _dot(dz_ref[:, j * WI_C:(j + 1) * WI_C], wi_ref[j], NT)
        xv = x_ref[...]
        r = _rms(xv)
        xn = xv * r
        n1, sc = v_ref[0:1, :], v_ref[2:3, :]
        acc_ref[0:1, :] += jnp.sum(dhx * xn * (1.0 + sc), axis=0, keepdims=True)
        acc_ref[1:2, :] += jnp.sum(dhx, axis=0, keepdims=True)
        acc_ref[2:3, :] += jnp.sum(dhx * xn * n1, axis=0, keepdims=True)
        g = dhx * n1 * (1.0 + sc)
        gx_ref[...] = dx1_ref[...] + r * (g - xn * jnp.mean(g * xn, axis=-1, keepdims=True))

        @pl.when(pl.program_id(0) == steps - 1)
        def _():
            cps = _scatter_ici_copies(ps, got, send_sems, recv_sems)
            for cp in cps:
                cp.wait_recv()
            for cp in cps:
                cp.wait_send()

    hbm = pl.BlockSpec(memory_space=pl.ANY)
    out = pl.pallas_call(
        body, name="in_proj_bwd", grid=(steps,),
        in_specs=[pl.BlockSpec((t, IN_COLS), lambda i: (i, 0)), pl.BlockSpec((t, D), lambda i: (i, 0)),
                  pl.BlockSpec((t, D), lambda i: (i, 0)), _const((8, D)), _const((NCHIP, D, WI_C))] + [hbm] * nt,
        out_specs=[pl.BlockSpec((t, D), lambda i: (i, 0)), pl.BlockSpec((8, D), lambda i: (0, 0))] + [hbm] * nt,
        out_shape=[jax.ShapeDtypeStruct((ln, D), F32), jax.ShapeDtypeStruct((8, D), F32)]
        + [jax.ShapeDtypeStruct((NCHIP - 1,) + p.shape[1:], BF) for p in parts],
        scratch_shapes=[pltpu.SemaphoreType.DMA(((NCHIP - 1) * nt,))] * 2,
        compiler_params=_params(56, ("arbitrary",)),
    )(dz, x, dx1, vecs, wi, *parts)
    return out[0], out[1], out[2:]


def _post_mixer(x, ycat, tgt, vecs, wo, wg, wu, wd):
    ln = x.shape[0]
    t = min(256, ln)

    def body(x_ref, y_ref, t_ref, v_ref, wo_ref, wg_ref, wu_ref, wd_ref,
             dx1_ref, dyc_ref, h2_ref, dy_ref, df_ref, act_ref, da_ref, db_ref, acc_ref, a_st, b_st):
        @pl.when(pl.program_id(0) == 0)
        def _():
            acc_ref[...] = jnp.zeros_like(acc_ref)

        g1, n2, sh2, sc2 = v_ref[0:1, :], v_ref[1:2, :], v_ref[2:3, :], v_ref[3:4, :]
        g2, nf = v_ref[4:5, :], v_ref[5:6, :]
        y = _dot(y_ref[...], wo_ref[...], NN)
        x1 = x_ref[...] + g1 * y
        r2 = _rms(x1)
        xn2 = x1 * r2
        t2 = xn2 * n2
        h2b = (t2 * (1.0 + sc2) + sh2).astype(BF)
        h2_ref[...] = h2b
        f = jnp.zeros((t, D), F32)
        for j in range(NCHIP):
            a = _dot(h2b, wg_ref[j], NT)
            b = _dot(h2b, wu_ref[j], NT)
            a_st[j] = a
            b_st[j] = b
            act = (_silu(a) * b).astype(BF)
            act_ref[j] = act
            f = f + _dot(act, wd_ref[j], NN)
        x2 = x1 + g2 * f
        r3 = _rms(x2)
        xn3 = x2 * r3
        e = xn3 * nf - t_ref[...]
        acc_ref[6:7, :] += jnp.sum(e * e, axis=0, keepdims=True) * (0.5 / D)
        dout = e * (1.0 / D)
        acc_ref[5:6, :] += jnp.sum(dout * xn3, axis=0, keepdims=True)
        gg = dout * nf
        dx2 = r3 * (gg - xn3 * jnp.mean(gg * xn3, axis=-1, keepdims=True))
        acc_ref[4:5, :] += jnp.sum(dx2 * f, axis=0, keepdims=True)
        dfb = (g2 * dx2).astype(BF)
        df_ref[...] = dfb
        dh2 = jnp.zeros((t, D), F32)
        for j in range(NCHIP):
            dact = _dot(dfb, wd_ref[j], NT)
            a = a_st[j]
            b = b_st[j]
            s = jax.nn.sigmoid(a)
            da = (dact * b * (s * (1.0 + a * (1.0 - s)))).astype(BF)
            db = (dact * (a * s)).astype(BF)
            da_ref[j] = da
            db_ref[j] = db
            dh2 = dh2 + _dot(da, wg_ref[j], NN) + _dot(db, wu_ref[j], NN)
        acc_ref[2:3, :] += jnp.sum(dh2, axis=0, keepdims=True)
        acc_ref[3:4, :] += jnp.sum(dh2 * t2, axis=0, keepdims=True)
        acc_ref[1:2, :] += jnp.sum(dh2 * xn2 * (1.0 + sc2), axis=0, keepdims=True)
        gx = dh2 * n2 * (1.0 + sc2)
        dx1 = dx2 + r2 * (gx - xn2 * jnp.mean(gx * xn2, axis=-1, keepdims=True))
        dx1_ref[...] = dx1
        acc_ref[0:1, :] += jnp.sum(dx1 * y, axis=0, keepdims=True)
        dyb = (g1 * dx1).astype(BF)
        dy_ref[...] = dyb
        dyc_ref[...] = _dot(dyb, wo_ref[...], NT)

    tok = pl.BlockSpec((t, D), lambda i: (i, 0))
    ffb = pl.BlockSpec((NCHIP, t, FF_C), lambda i: (0, i, 0))
    return pl.pallas_call(
        body, name="post_mixer", grid=(ln // t,),
        in_specs=[tok, tok, tok, _const((8, D)), _const((D, D)), _const((NCHIP, FF_C, D)),
                  _const((NCHIP, FF_C, D)), _const((NCHIP, FF_C, D))],
        out_specs=[tok, tok, tok, tok, tok, ffb, ffb, ffb, pl.BlockSpec((16, D), lambda i: (0, 0))],
        out_shape=[jax.ShapeDtypeStruct((ln, D), F32)] * 2 + [jax.ShapeDtypeStruct((ln, D), BF)] * 3
        + [jax.ShapeDtypeStruct((NCHIP, ln, FF_C), BF)] * 3 + [jax.ShapeDtypeStruct((16, D), F32)],
        scratch_shapes=[pltpu.VMEM((NCHIP, t, FF_C), F32)] * 2,
        compiler_params=_params(60, ("arbitrary",)),
    )(x, ycat, tgt, vecs, wo, wg, wu, wd)


def _tn_matmul(xa, dy, name, nb, k1, n, x_batched, dy_mode, tt, ctx_kv=None):
    ln = xa.shape[-2]
    tt = min(tt, ln)

    def body(x_ref, dy_ref, *refs):
        o_ref = refs[-1]

        @pl.when(pl.program_id(0) == 0)
        def _():
            o_ref[...] = jnp.zeros_like(o_ref)
            if ctx_kv is not None:
                for g in range(0, 2 * AW, HD):
                    col = 3 * AW + g
                    o_ref[col // n, :, col % n: col % n + HD] = refs[0][:, g:g + HD]

        xt = None if x_batched else jnp.transpose(x_ref[...])
        for b in range(nb):
            lhs = jnp.transpose(x_ref[b]) if x_batched else xt
            if dy_mode == "batched":
                rhs = dy_ref[b]
            elif dy_mode == "cols":
                rhs = dy_ref[:, b * n:(b + 1) * n]
            else:
                rhs = dy_ref[...]
            o_ref[b] += _dot(lhs, rhs, NN)

    x_spec = (pl.BlockSpec((nb, tt, k1), lambda t: (0, t, 0)) if x_batched
              else pl.BlockSpec((tt, k1), lambda t: (t, 0)))
    if dy_mode == "batched":
        dy_spec = pl.BlockSpec((nb, tt, n), lambda t: (0, t, 0))
    elif dy_mode == "cols":
        dy_spec = pl.BlockSpec((tt, nb * n), lambda t: (t, 0))
    else:
        dy_spec = pl.BlockSpec((tt, n), lambda t: (t, 0))
    extra = [] if ctx_kv is None else [ctx_kv]
    return pl.pallas_call(
        body, name=name, grid=(ln // tt,),
        in_specs=[x_spec, dy_spec] + [_const(e.shape) for e in extra],
        out_specs=pl.BlockSpec((nb, k1, n), lambda t: (0, 0, 0)),
        out_shape=jax.ShapeDtypeStruct((nb, k1, n), F32),
        compiler_params=_params(60, ("arbitrary",)),
    )(xa, dy, *extra)


def _add_ctx_cols(gwi, gwkv):
    first = 1536 // HD
    per = WI_C // HD

    def body(g_ref, a_ref, o_ref):
        o_ref[...] = g_ref[...] + a_ref[...]

    spec = pl.BlockSpec((None, D, HD), lambda i: ((first + i) // per, 0, (first + i) % per))
    return pl.pallas_call(
        body, name="add_ctx_cols", grid=(2 * AW // HD,),
        in_specs=[spec, pl.BlockSpec((D, HD), lambda i: (0, i))],
        out_specs=spec,
        out_shape=jax.ShapeDtypeStruct(gwi.shape, F32),
        input_output_aliases={0: 0},
        compiler_params=_params(32, ("arbitrary",)),
    )(gwi, gwkv)


FWD_CHUNKS_PER_STEP = 4
BWD_CHUNKS_PER_STEP = 2


def _chunks_per_step(nc, want):
    return want if nc % want == 0 else 1


def _mixer_fwd(z, cos_t, sin_t, dm, sgw, gain, bfull, scf, scb, gbufs_a, gbufs_b):
    ln = z.shape[0]
    nc = ln // CH
    na = len(gbufs_a)
    gbufs = list(gbufs_a) + list(gbufs_b)
    nt = len(gbufs)
    cps = _chunks_per_step(nc, FWD_CHUNKS_PER_STEP)
    nb = nc // cps
    rows = cps * CH
    mid = nb // 2

    def rev(p, n):
        return p * n + (1 - p) * (nb - 1 - n)

    def col(j, both):
        if both:
            return pl.BlockSpec((rows, AW), lambda p, n: (rev(p, n), j))
        return pl.BlockSpec((rows, AW), lambda p, n: (p * n, j))

    def body(u_ref, v_ref, q_ref, k_ref, vr_ref, gf_ref, gb_ref, cos_ref, sin_ref, dm_ref, sgw_ref, gain_ref,
             bfull_ref, scf_ref, scb_ref, *refs):
        y_ref, sf_ref, sb_ref = refs[nt:nt + 3]
        bufs = refs[nt + 3:2 * nt + 3]
        bufs_a, bufs_b = bufs[:na], bufs[na:]
        sb_all, st, a_send, a_recv, bi_send, bi_recv, bd_send, bd_recv = refs[2 * nt + 3:]
        p, n = pl.program_id(0), pl.program_id(1)

        @pl.when((p == 0) & (n == 0))
        def _():
            for cp in _gather_d2d_copies(bufs_a, a_send, a_recv)[0]:
                cp.start()
            for cp in _gather_ici_copies(bufs_b, bi_send, bi_recv)[0]:
                cp.start()

        @pl.when((p == 1) & (n == mid))
        def _():
            for cp in _gather_ici_copies(bufs_b, bi_send, bi_recv)[1]:
                cp.wait_recv()
            for cp in _gather_d2d_copies(bufs_b, bd_send, bd_recv)[0]:
                cp.start()

        def roped_k(r0):
            cos, sin = cos_ref[r0:r0 + CH, :], sin_ref[r0:r0 + CH, :]
            return [_rope(t, cos, sin) * K_SCALE for t in _heads(k_ref, r0)]

        @pl.when(p == 0)
        def _():
            @pl.when(n == 0)
            def _():
                st[...] = scb_ref[...]

            for s in reversed(range(cps)):
                m = (nb - 1 - n) * cps + s
                k, vr = roped_k(s * CH), _heads(vr_ref, s * CH)
                for h in range(NH):
                    sb_all[m, h] = st[h]
                    st[h] = dm_ref[1, 3, h] * st[h] + _mm_tn(k[h], dm_ref[1, 2, h] * vr[h])

        @pl.when(p == 1)
        def _():
            @pl.when(n == 0)
            def _():
                st[...] = scf_ref[...]

            mats = [[dm_ref[d, kind, h] for h in range(NH)] for d in range(2) for kind in range(3)]
            for s in range(cps):
                r0 = s * CH
                m = n * cps + s
                cos, sin = cos_ref[r0:r0 + CH, :], sin_ref[r0:r0 + CH, :]
                q = [_rope(t, cos, sin) for t in _heads(q_ref, r0)]
                sf = [st[h] for h in range(NH)]
                sb = [sb_all[m, h] for h in range(NH)]
                ya, yr, uf, _ = _chunk_fwd(
                    _heads(u_ref, r0), _heads(v_ref, r0), q, roped_k(r0), _heads(vr_ref, r0), _heads(gf_ref, r0),
                    _heads(gb_ref, r0), sf, sb, mats[0], mats[1], mats[2], mats[3], mats[4], mats[5],
                    [sgw_ref[g] for g in range(NH)], [gain_ref[:, g * HD:(g + 1) * HD] for g in range(NH)],
                    [bfull_ref[g] for g in range(NH)])
                for h in range(NH):
                    y_ref[r0:r0 + CH, h * HD:(h + 1) * HD] = ya[h].astype(BF)
                    y_ref[r0:r0 + CH, AW + h * HD:AW + (h + 1) * HD] = yr[h].astype(BF)
                    sf_ref[s, h] = sf[h]
                    sb_ref[s, h] = sb[h]
                    st[h] = dm_ref[0, 3, h] * st[h] + uf[h]

        @pl.when((p == 1) & (n == nb - 1))
        def _():
            a_out, a_in = _gather_d2d_copies(bufs_a, a_send, a_recv)
            b_out, b_in = _gather_d2d_copies(bufs_b, bd_send, bd_recv)
            for cp in a_in + b_in:
                cp.wait_recv()
            for cp in a_out + b_out + _gather_ici_copies(bufs_b, bi_send, bi_recv)[0]:
                cp.wait_send()

    hbm = pl.BlockSpec(memory_space=pl.ANY)
    tab = pl.BlockSpec((rows, HD), lambda p, n: (rev(p, n), 0))
    st_spec = pl.BlockSpec((cps, NH, HD, HD), lambda p, n: (p * n, 0, 0, 0))
    out = pl.pallas_call(
        body, name="mixer_fwd", grid=(2, nb),
        in_specs=[col(0, False), col(1, False), col(2, False), col(3, True), col(4, True), col(5, False),
                  col(6, False), tab, tab, _const((2, 4, NH, CH, CH)), _const((NH, CH, CH)), _const((1, AW)),
                  _const((NH, CH, CH)), _const((NH, HD, HD)), _const((NH, HD, HD))] + [hbm] * nt,
        out_specs=[pl.BlockSpec((rows, D), lambda p, n: (p * n, 0)), st_spec, st_spec] + [hbm] * nt,
        out_shape=[jax.ShapeDtypeStruct((ln, D), BF), jax.ShapeDtypeStruct((nc, NH, HD, HD), F32),
                   jax.ShapeDtypeStruct((nc, NH, HD, HD), F32)]
        + [jax.ShapeDtypeStruct(g.shape, g.dtype) for g in gbufs],
        input_output_aliases={15 + k: 3 + k for k in range(nt)},
        scratch_shapes=[pltpu.VMEM((nc, NH, HD, HD), F32), pltpu.VMEM((NH, HD, HD), F32)]
        + [pltpu.SemaphoreType.DMA(((NCHIP - 1) * na,))] * 2
        + [pltpu.SemaphoreType.DMA(((NCHIP - 1) * (nt - na),))] * 4,
        compiler_params=_params(56, ("arbitrary", "arbitrary")),
    )(z, z, z, z, z, z, z, cos_t, sin_t, dm, sgw, gain, bfull, scf, scb, *gbufs)
    return out[0], out[1], out[2], out[3:]


def _mixer_bwd(z, dycat, cos_t, sin_t, dm, sgw, gain, bfull, sf_all, sb_all, parts):
    ln = z.shape[0]
    nc = ln // CH
    cps = _chunks_per_step(nc, BWD_CHUNKS_PER_STEP)
    nb = nc // cps
    rows = cps * CH

    def rev(p, n):
        return p * n + (1 - p) * (nb - 1 - n)

    def col(j, both):
        if both:
            return pl.BlockSpec((rows, AW), lambda p, n: (rev(p, n), j))
        return pl.BlockSpec((rows, AW), lambda p, n: (p * n, j))

    nt = len(parts)

    def body(u_ref, v_ref, q_ref, k_ref, vr_ref, gf_ref, gb_ref, dya_ref, dyr_ref, cos_ref, sin_ref, dm_ref,
             sgw_ref, gain_ref, bfull_ref, sf_ref, sb_ref, *refs):
        ps = refs[:nt]
        dz_ref, ddm_ref, dsgw_ref, dgain_ref, dbf_ref, dscf_ref, dscb_ref = refs[nt:nt + 7]
        got = refs[nt + 7:2 * nt + 7]
        gf_all, run, send_sems, recv_sems = refs[2 * nt + 7:]
        p, n = pl.program_id(0), pl.program_id(1)

        @pl.when((p == 0) & (n == 0))
        def _():
            for cp in _scatter_ici_copies(ps, got, send_sems, recv_sems):
                cp.start()

        mats = [[dm_ref[d, kind, h] for h in range(NH)] for d in range(2) for kind in range(3)]

        def chunk_inputs(s):
            r0 = s * CH
            cos, sin = cos_ref[r0:r0 + CH, :], sin_ref[r0:r0 + CH, :]
            q = [_rope(t, cos, sin) for t in _heads(q_ref, r0)]
            k = [_rope(t, cos, sin) * K_SCALE for t in _heads(k_ref, r0)]
            sf = [sf_ref[s, h] for h in range(NH)]
            return cos, sin, q, k, _heads(vr_ref, r0), _heads(gf_ref, r0), _heads(dyr_ref, r0), sf

        @pl.when(p == 0)
        def _():
            @pl.when(n == 0)
            def _():
                run[...] = jnp.zeros_like(run)
                ddm_ref[...] = jnp.zeros_like(ddm_ref)
                dsgw_ref[...] = jnp.zeros_like(dsgw_ref)
                dgain_ref[...] = jnp.zeros_like(dgain_ref)
                dbf_ref[...] = jnp.zeros_like(dbf_ref)

            for s in reversed(range(cps)):
                m = (nb - 1 - n) * cps + s
                _, _, q, k, vr, gf, dyr, sf = chunk_inputs(s)
                _, vjp = jax.vjp(lambda st: _fwd_dir_only(st, q, k, vr, gf, mats[0], mats[1]), sf)
                (dsf,) = vjp(dyr)
                for h in range(NH):
                    gf_all[m, h] = run[h]
                    run[h] = dsf[h] + dm_ref[0, 3, h] * run[h]

            @pl.when(n == nb - 1)
            def _():
                dscf_ref[...] = run[...]

        @pl.when(p == 1)
        def _():
            @pl.when(n == 0)
            def _():
                run[...] = jnp.zeros_like(run)

            for s in range(cps):
                r0 = s * CH
                m = n * cps + s
                cos, sin, q, k, vr, gf, dyr, sf = chunk_inputs(s)
                sb = [sb_ref[s, h] for h in range(NH)]
                g_f = [gf_all[m, h] for h in range(NH)]
                g_b = [run[h] for h in range(NH)]
                args = (_heads(u_ref, r0), _heads(v_ref, r0), q, k, vr, gf, _heads(gb_ref, r0), sb,
                        mats[0], mats[1], mats[2], mats[3], mats[4], mats[5],
                        [sgw_ref[g] for g in range(NH)], [gain_ref[:, g * HD:(g + 1) * HD] for g in range(NH)],
                        [bfull_ref[g] for g in range(NH)])

                def fn(u_, v_, q_, k_, vr_, gf_, gb_, sb_, df, xf, zf, db, xb, zb, sgw_, gain_, bfull_, sf=sf):
                    return _chunk_fwd(u_, v_, q_, k_, vr_, gf_, gb_, sf, sb_, df, xf, zf, db, xb, zb, sgw_,
                                      gain_, bfull_)

                _, vjp = jax.vjp(fn, *args)
                (du, dv, dq, dk, dvr, dgf, dgb, dsb, ddf, dxf, dzf, ddb, dxb, dzb, dsgw, dgain, dbf) = vjp(
                    (_heads(dya_ref, r0), dyr, g_f, g_b))
                rw = slice(r0, r0 + CH)
                for h in range(NH):
                    cs = slice(h * HD, (h + 1) * HD)
                    dz_ref[rw, h * HD:(h + 1) * HD] = du[h].astype(BF)
                    dz_ref[rw, AW + h * HD:AW + (h + 1) * HD] = dv[h].astype(BF)
                    dz_ref[rw, 2 * AW + h * HD:2 * AW + (h + 1) * HD] = _rope_bwd(dq[h], cos, sin).astype(BF)
                    dz_ref[rw, 3 * AW + h * HD:3 * AW + (h + 1) * HD] = _rope_bwd(dk[h] * K_SCALE, cos,
                                                                                  sin).astype(BF)
                    dz_ref[rw, 4 * AW + h * HD:4 * AW + (h + 1) * HD] = dvr[h].astype(BF)
                    dz_ref[rw, 5 * AW + h * HD:5 * AW + (h + 1) * HD] = dgf[h].astype(BF)
                    dz_ref[rw, 6 * AW + h * HD:6 * AW + (h + 1) * HD] = dgb[h].astype(BF)
                    ddm_ref[0, 0, h] += ddf[h]
                    ddm_ref[0, 1, h] += dxf[h]
                    ddm_ref[0, 2, h] += dzf[h]
                    ddm_ref[0, 3, h] += sf[h] * g_f[h]
                    ddm_ref[1, 0, h] += ddb[h]
                    ddm_ref[1, 1, h] += dxb[h]
                    ddm_ref[1, 2, h] += dzb[h]
                    ddm_ref[1, 3, h] += sb[h] * g_b[h]
                    dsgw_ref[h] += dsgw[h]
                    dgain_ref[:, cs] += dgain[h]
                    dbf_ref[h] += dbf[h]
                    run[h] = dsb[h] + dm_ref[1, 3, h] * run[h]

            @pl.when(n == nb - 1)
            def _():
                dscb_ref[...] = run[...]

        @pl.when((p == 1) & (n == nb - 1))
        def _():
            cps_ = _scatter_ici_copies(ps, got, send_sems, recv_sems)
            for cp in cps_:
                cp.wait_recv()
            for cp in cps_:
                cp.wait_send()

    hbm = pl.BlockSpec(memory_space=pl.ANY)
    tab = pl.BlockSpec((rows, HD), lambda p, n: (rev(p, n), 0))
    tile4 = jax.ShapeDtypeStruct((NH, CH, CH), F32)
    out = pl.pallas_call(
        body, name="mixer_bwd", grid=(2, nb),
        in_specs=[col(0, False), col(1, False), col(2, True), col(3, True), col(4, True), col(5, True),
                  col(6, False),
                  pl.BlockSpec((rows, AW), lambda p, n: (p * n, 0)),
                  pl.BlockSpec((rows, AW), lambda p, n: (rev(p, n), 1)),
                  tab, tab, _const((2, 4, NH, CH, CH)), _const((NH, CH, CH)), _const((1, AW)),
                  _const((NH, CH, CH)),
                  pl.BlockSpec((cps, NH, HD, HD), lambda p, n: (rev(p, n), 0, 0, 0)),
                  pl.BlockSpec((cps, NH, HD, HD), lambda p, n: (p * n, 0, 0, 0))] + [hbm] * nt,
        out_specs=[pl.BlockSpec((rows, IN_COLS), lambda p, n: (p * n, 0)),
                   pl.BlockSpec((2, 4, NH, CH, CH), lambda p, n: (0, 0, 0, 0, 0)),
                   pl.BlockSpec((NH, CH, CH), lambda p, n: (0, 0, 0)),
                   pl.BlockSpec((1, AW), lambda p, n: (0, 0)),
                   pl.BlockSpec((NH, CH, CH), lambda p, n: (0, 0, 0)),
                   pl.BlockSpec((NH, HD, HD), lambda p, n: (0, 0, 0)),
                   pl.BlockSpec((NH, HD, HD), lambda p, n: (0, 0, 0))] + [hbm] * nt,
        out_shape=[jax.ShapeDtypeStruct((ln, IN_COLS), BF), jax.ShapeDtypeStruct((2, 4, NH, CH, CH), F32),
                   tile4, jax.ShapeDtypeStruct((1, AW), F32), tile4, tile4, tile4]
        + [jax.ShapeDtypeStruct((NCHIP - 1,) + p.shape[1:], BF) for p in parts],
        scratch_shapes=[pltpu.VMEM((nc, NH, HD, HD), F32), pltpu.VMEM((NH, HD, HD), F32)]
        + [pltpu.SemaphoreType.DMA(((NCHIP - 1) * nt,))] * 2,
        compiler_params=_params(56, ("arbitrary", "arbitrary")),
    )(z, z, z, z, z, z, z, dycat, dycat, cos_t, sin_t, dm, sgw, gain, bfull, sf_all, sb_all, *parts)
    return out[:7], out[7:]


def _small_reduce(ddm, ddm_ctx, dm, dbf):
    def body(ddm_ref, dctx_ref, dm_ref, dbf_ref, lg_ref, sgb_ref):
        ex = _decay_exponents()
        ones = jnp.ones((8, CH), F32)
        for d in range(2):
            for h in range(NH):
                tot = jnp.zeros((CH, CH), F32)
                for kind in range(4):
                    g = ddm_ref[d, kind, h]
                    if kind >= 2:
                        g = g + dctx_ref[d, kind - 2, h]
                    tot = tot + g * dm_ref[d, kind, h] * ex[d][kind]
                lg_ref[d * NH + h: d * NH + h + 1, :] = jnp.sum(tot, axis=0, keepdims=True)
        for g in range(NH):
            r = lax.dot_general(ones, dbf_ref[g], (NT, ((), ())), precision=HI, preferred_element_type=F32)
            sgb_ref[g:g + 1, :] = r[0:1, :]

    return pl.pallas_call(
        body, name="small_reduce",
        out_shape=[jax.ShapeDtypeStruct((8, CH), F32), jax.ShapeDtypeStruct((NH, CH), F32)],
        in_specs=[pl.BlockSpec(memory_space=pltpu.VMEM)] * 4,
        out_specs=[pl.BlockSpec(memory_space=pltpu.VMEM)] * 2,
        compiler_params=_params(32),
    )(ddm, ddm_ctx, dm, dbf)


def _mod_backward(ct_pad_t, cctx_col, dmod_pad, dcmod_cols, w_mod_s):
    def body(ct_ref, cc_ref, dm_ref, dc_ref, w_ref, gw_ref, part_ref):
        dcm = dc_ref[0:1, :]
        for d in range(1, NDEV):
            dcm = dcm + dc_ref[d:d + 1, :]
        gw_ref[...] = (jnp.dot(_silu(ct_ref[...]), dm_ref[...], precision=HI, preferred_element_type=F32)
                       + _silu(cc_ref[...]) * dcm)
        part_ref[...] = lax.dot_general(jnp.broadcast_to(dcm, (8, dcm.shape[1])), w_ref[...], (NT, ((), ())),
                                        precision=HI, preferred_element_type=F32)

    return pl.pallas_call(
        body, name="mod_backward",
        out_shape=[jax.ShapeDtypeStruct(w_mod_s.shape, F32), jax.ShapeDtypeStruct((8, D), F32)],
        in_specs=[pl.BlockSpec(memory_space=pltpu.VMEM)] * 5,
        out_specs=[pl.BlockSpec(memory_space=pltpu.VMEM)] * 2,
        compiler_params=_params(48),
    )(ct_pad_t, cctx_col, dmod_pad, dcmod_cols, w_mod_s)


def _cctx_update(parts, c_ctx, m, v):
    def body(p_ref, c_ref, m_ref, v_ref, g_ref, d_ref, mo_ref, vo_ref):
        tot = ((p_ref[0] + p_ref[2]) + p_ref[4]) + p_ref[6]
        cv = c_ref[...]
        s = jax.nn.sigmoid(cv)
        g = tot * (s * (1.0 + cv * (1.0 - s)))
        g_ref[...] = g
        d_ref[...], mo_ref[...], vo_ref[...] = _adamw_math(cv, g, m_ref[...], v_ref[...])

    return pl.pallas_call(
        body, name="cctx_update",
        out_shape=[jax.ShapeDtypeStruct((1, D), F32)] * 4,
        in_specs=[pl.BlockSpec(memory_space=pltpu.VMEM)] * 4,
        out_specs=[pl.BlockSpec(memory_space=pltpu.VMEM)] * 4,
        compiler_params=_params(16),
    )(parts, c_ctx, m, v)


def _small_update(gathered, wp, mp, vp):
    def body(g_ref, w_ref, m_ref, v_ref, go_ref, d_ref, mo_ref, vo_ref, loss_ref):
        tot = g_ref[0]
        for d in range(1, NDEV):
            tot = tot + g_ref[d]
        go_ref[Q_BMOD:Q_N1, :] = tot[P_DMOD:P_N1, :] + tot[P_DCMOD:P_DMOD, :]
        go_ref[Q_N1:Q_LG, :] = tot[P_N1:P_LG, :]
        lg = jnp.sum(tot[P_LG:P_N2, :], axis=1, keepdims=True)
        go_ref[Q_LG:Q_N2, :] = lg * jax.nn.sigmoid(-w_ref[Q_LG:Q_N2, :])
        go_ref[Q_N2:Q_ROWS, :] = tot[P_N2:P_LOSS, :]
        d_ref[...], mo_ref[...], vo_ref[...] = _adamw_math(w_ref[...], go_ref[...], m_ref[...], v_ref[...])
        ls = jnp.sum(jnp.sum(tot[P_LOSS:P_ROWS, :], axis=1, keepdims=True), axis=0, keepdims=True)
        loss_ref[...] = jnp.broadcast_to(ls, (8, CH))

    return pl.pallas_call(
        body, name="small_update",
        out_shape=[jax.ShapeDtypeStruct((Q_ROWS, CH), F32)] * 4 + [jax.ShapeDtypeStruct((8, CH), F32)],
        in_specs=[pl.BlockSpec(memory_space=pltpu.VMEM)] * 4,
        out_specs=[pl.BlockSpec(memory_space=pltpu.VMEM)] * 5,
        compiler_params=_params(32),
    )(gathered, wp, mp, vp)


def _rows(a):
    return a.reshape(-1, CH)


def _pack_small(b_mod, norm1, sg_gain, sg_w, sg_b, lf, lb, norm2, norm_f):
    lg = jnp.broadcast_to(jnp.concatenate([lf.reshape(NH), lb.reshape(NH)])[:, None], (2 * NH, CH))
    return jnp.concatenate([_rows(b_mod), _rows(norm1), _rows(sg_gain), _rows(sg_w), _rows(sg_b), lg,
                            _rows(norm2), _rows(norm_f)], axis=0)


def _unpack_small(p):
    return (p[Q_BMOD:Q_N1].reshape(1, 6 * D), p[Q_N1:Q_GAIN].reshape(1, D), p[Q_GAIN:Q_SGW].reshape(1, AW),
            p[Q_SGW:Q_SGB].reshape(1, NH, CH, CH), p[Q_SGB:Q_LG].reshape(1, NH, CH),
            p[Q_LG:Q_LG + NH, 0].reshape(1, NH), p[Q_LG + NH:Q_N2, 0].reshape(1, NH),
            p[Q_N2:Q_NF].reshape(1, D), p[Q_NF:Q_ROWS].reshape(D))


def _rope_tables(ln):
    pos = np.arange(ln)
    rows = (pos // GRID_W).astype(np.float32)
    cols = (pos % GRID_W).astype(np.float32)
    n_freq = HD // 4
    inv = (np.float32(ROPE_BASE) ** (-np.arange(n_freq, dtype=np.float32) / np.float32(n_freq))).astype(np.float32)
    ar = rows[:, None] * inv[None, :]
    ac = cols[:, None] * inv[None, :]
    cos_t = np.concatenate([np.cos(ar), np.cos(ar), np.cos(ac), np.cos(ac)], axis=1).astype(np.float32)
    sin_t = np.concatenate([-np.sin(ar), np.sin(ar), -np.sin(ac), np.sin(ac)], axis=1).astype(np.float32)
    return jnp.asarray(cos_t), jnp.asarray(sin_t)


def kernel(x, c, ctx, c_ctx, w_mod, b_mod, norm1, w_in, sg_gain, sg_w, sg_b, ret_logit_f, ret_logit_b, w_out, norm2, w_gate, w_up, w_down, norm_f, loss_target, m_c_ctx, m_w_mod, m_b_mod, m_norm1, m_w_in, m_sg_gain, m_sg_w, m_sg_b, m_ret_logit_f, m_ret_logit_b, m_w_out, m_norm2, m_w_gate, m_w_up, m_w_down, m_norm_f, v_c_ctx, v_w_mod, v_b_mod, v_norm1, v_w_in, v_sg_gain, v_sg_w, v_sg_b, v_ret_logit_f, v_ret_logit_b, v_w_out, v_norm2, v_w_gate, v_w_up, v_w_down, v_norm_f):
    ln = x.shape[1]
    xi, yi, ci = _pos()
    chip = 2 * xi + yi
    me = 4 * xi + 2 * yi + ci
    x2d = x.reshape(ln, D)
    tgt = loss_target.reshape(ln, D)
    mod_c = w_mod.shape[2]

    tr = lambda a: jnp.swapaxes(a[0], 0, 1)
    gbufs, c_all, prod_all = _prologue(c, c_ctx.reshape(1, D), w_mod[0],
                                       [w_in[0], w_out[0], tr(w_gate), tr(w_up), w_down[0]])
    wi = gbufs[0].reshape(NCHIP, D, WI_C)
    gbufs_a, gbufs_b = gbufs[1:3], gbufs[3:5]
    c_all = c_all.reshape(NDEV, D)
    prod_chips = prod_all[0::2]
    mod_rows = jnp.transpose(prod_chips, (1, 0, 2)).reshape(16, NCHIP * mod_c) + b_mod
    mod = lax.dynamic_slice_in_dim(mod_rows, me, 1, axis=0)
    cmod = mod_rows[8:9]
    sh1, sc1, g1, sh2, sc2, g2 = [mod[:, i * D:(i + 1) * D] for i in range(6)]
    csh1, csc1 = cmod[:, 0:D], cmod[:, D:2 * D]
    zrow = jnp.zeros((1, D), F32)
    vec_in = jnp.concatenate([norm1, sh1, sc1] + [zrow] * 5, axis=0)
    vec_ctx = jnp.concatenate([norm1, csh1, csc1] + [zrow] * 5, axis=0)
    vec_post = jnp.concatenate([g1, norm2, sh2, sc2, g2, norm_f.reshape(1, D), zrow, zrow], axis=0)

    logits = jnp.concatenate([ret_logit_f.reshape(NH), ret_logit_b.reshape(NH)])
    dm = _decay_mats(jnp.broadcast_to(logits[:, None, None], (2 * NH, CH, CH)))
    ctx2d = ctx.reshape(ctx.shape[1], D)
    scf, scb = _ctx_forward(ctx2d, vec_ctx, wi, dm)

    cos_t, sin_t = _rope_tables(ln)
    z, hx, gbufs_a = _in_proj(x2d, vec_in, wi, gbufs_a)
    bfull = jnp.broadcast_to(sg_b[0][:, :, None], (NH, CH, CH))
    ycat, sf_all, sb_all, gbufs = _mixer_fwd(z, cos_t, sin_t, dm, sg_w[0], sg_gain, bfull, scf, scb,
                                             gbufs_a, gbufs_b)
    wo, wg_t, wu_t, wd = [g.reshape(NCHIP, 2 * g.shape[2], g.shape[3]) for g in gbufs]
    wo = wo.reshape(D, D)

    dx1, dycat, h2, dy, df, act, da, db, acc_post = _post_mixer(x2d, ycat, tgt, vec_post, wo, wg_t, wu_t, wd)

    cidx = ci.reshape(1).astype(jnp.int32)
    chipidx = chip.reshape(1).astype(jnp.int32)

    def halves_summed(full, names):
        full = [g.reshape(NCHIP, 2, g.shape[1] // 2, g.shape[2]) for g in full]
        from_sib = _rs_exchange_halves(full, "rs_exchange_" + names[0])
        return [_rs_add_halves(g, r, cidx, "rs_add_halves_" + nm) for g, r, nm in zip(full, from_sib, names)]

    g_wo = _tn_matmul(ycat, dy, "grad_w_out", 1, D, D, False, "shared", 1024).reshape(NCHIP, WO_R, D)
    g_wg = _tn_matmul(da, h2, "grad_w_gate", NCHIP, FF_C, D, True, "shared", 1024)
    g_wu = _tn_matmul(db, h2, "grad_w_up", NCHIP, FF_C, D, True, "shared", 1024)
    g_wd = _tn_matmul(act, df, "grad_w_down", NCHIP, FF_C, D, True, "shared", 1024)
    names = ["w_in", "w_out", "w_gate", "w_up", "w_down"]
    sums_b = halves_summed([g_wo, g_wg, g_wu, g_wd], names[1:])

    (dz, ddm, dsgw, dgain, dbf, dscf, dscb), from_chips_b = _mixer_bwd(
        z, dycat, cos_t, sin_t, dm, sg_w[0], sg_gain, bfull, sf_all, sb_all, [s[1] for s in sums_b])
    gwkv, acc_ctx, ddm_ctx = _ctx_backward(ctx2d, vec_ctx, wi, dm, dscf, dscb)
    g_wi = _tn_matmul(hx, dz, "grad_w_in", NCHIP, D, WI_C, False, "cols", 512, ctx_kv=gwkv)
    sums_a = halves_summed([g_wi], names[:1])
    gx, acc_in, from_chips_a = _in_proj_bwd(dz, x2d, dx1, vec_in, wi, [s[1] for s in sums_a])

    sums = sums_a + sums_b
    from_chips = list(from_chips_a) + list(from_chips_b)
    finals = [_rs_add_chips(s[0], r, chipidx, "rs_add_chips_" + nm) for s, r, nm in zip(sums, from_chips, names)]
    others = _rs_share_final(finals)

    lg_part, dsgb = _small_reduce(ddm, ddm_ctx, dm, dbf)
    dmod = jnp.concatenate([acc_in[1:2], acc_in[2:3], acc_post[0:1], acc_post[2:3], acc_post[3:4],
                            acc_post[4:5]], axis=1)
    dcmod = jnp.concatenate([acc_ctx[1:2], acc_ctx[2:3], jnp.zeros((1, 4 * D), F32)], axis=1)
    packed = jnp.concatenate([
        _rows(dcmod), _rows(dmod), _rows(acc_in[0:1] + acc_ctx[0:1]), _rows(dgain), _rows(dsgw), dsgb, lg_part,
        _rows(acc_post[1:2]), _rows(acc_post[5:6]), _rows(acc_post[6:7])], axis=0)
    gathered = _allgather_small(packed, "gather_small")
    dmod_all = gathered[:, P_DMOD:P_N1].reshape(NDEV, 6 * D)
    dcmod_all = gathered[:, P_DCMOD:P_DMOD].reshape(NDEV, 6 * D)
    dmod_cols = lax.dynamic_slice_in_dim(dmod_all, chip * mod_c, mod_c, axis=1)
    dcmod_cols = lax.dynamic_slice_in_dim(dcmod_all, chip * mod_c, mod_c, axis=1)
    dmod_pad = jnp.concatenate([dmod_cols, jnp.zeros((CH - NDEV, mod_c), F32)], axis=0)
    ct_pad_t = jnp.concatenate([jnp.transpose(c_all), jnp.zeros((D, CH - NDEV), F32)], axis=1)
    g_wmod, cctx_part = _mod_backward(ct_pad_t, c_ctx.reshape(D, 1), dmod_pad, dcmod_cols, w_mod[0])
    parts = _allgather_small(cctx_part[0:1], "gather_cctx")
    g_cctx, d_cctx, nm_cctx, nv_cctx = _cctx_update(parts, c_ctx.reshape(1, D), m_c_ctx.reshape(1, D),
                                                    v_c_ctx.reshape(1, D))

    wp = _pack_small(b_mod, norm1, sg_gain, sg_w, sg_b, ret_logit_f, ret_logit_b, norm2, norm_f)
    mp = _pack_small(m_b_mod, m_norm1, m_sg_gain, m_sg_w, m_sg_b, m_ret_logit_f, m_ret_logit_b, m_norm2, m_norm_f)
    vp = _pack_small(v_b_mod, v_norm1, v_sg_gain, v_sg_w, v_sg_b, v_ret_logit_f, v_ret_logit_b, v_norm2, v_norm_f)
    gp, dp, mp2, vp2, loss_t = _small_update(gathered, wp, mp, vp)

    big_w = [w_in[0], w_out[0], tr(w_gate), tr(w_up), w_down[0]]
    big_m = [m_w_in[0], m_w_out[0], tr(m_w_gate), tr(m_w_up), m_w_down[0]]
    big_v = [v_w_in[0], v_w_out[0], tr(v_w_gate), tr(v_w_up), v_w_down[0]]
    upd = [_adamw_halves(w, own, oth, m, v, cidx, "adamw_" + nm) for w, own, oth, m, v, nm in
           zip(big_w, finals, others, big_m, big_v, names)]
    big_g = [g_wmod] + [u[0] for u in upd]
    big = [_adamw(w_mod[0], g_wmod, m_w_mod[0], v_w_mod[0], "adamw_w_mod")] + [u[1:] for u in upd]

    def assemble(small, cctx, bigs):
        b_mod_, norm1_, gain_, sgw_, sgb_, lf_, lb_, norm2_, normf_ = _unpack_small(small)
        wm, wi_, wo_, wg_, wu_, wd_ = [b[None] for b in bigs]
        wg_, wu_ = jnp.swapaxes(wg_, 1, 2), jnp.swapaxes(wu_, 1, 2)
        return [cctx.reshape(D), wm, b_mod_, norm1_, wi_, gain_, sgw_, sgb_, lf_, lb_, wo_, norm2_, wg_, wu_, wd_,
                normf_]

    out = [loss_t[0, 0], gx.reshape(1, ln, D)]
    out += assemble(gp, g_cctx, big_g)
    out += assemble(dp, d_cctx, [b[0] for b in big])
    out += assemble(mp2, nm_cctx, [b[1] for b in big])
    out += assemble(vp2, nv_cctx, [b[2] for b in big])
    return tuple(out)
```

```python
import functools

import jax
import jax.numpy as jnp
import numpy as np
from jax import lax
from jax.experimental import pallas as pl
from jax.experimental.pallas import tpu as pltpu

F32 = jnp.float32
BF = jnp.bfloat16
MESH = pl.DeviceIdType.MESH

D = 1024
CH = 128
HD = 128
NH = 4
AW = 512
IN_COLS = 3584
DFF = 2816
NCHIP = 4
NDEV = 8
WI_C = IN_COLS // NCHIP
FF_C = DFF // NCHIP
WO_R = D // NCHIP
EPS = 1e-6
GRID_W = 64
ROPE_BASE = 10000.0
K_SCALE = HD ** -0.5
LR, B1, B2, AEPS, WD, STEP = 0.001, 0.9, 0.999, 1e-08, 0.01, 10
VMEM_MB = 1 << 20
HI = lax.Precision.HIGHEST

P_DCMOD, P_DMOD, P_N1, P_GAIN, P_SGW, P_SGB, P_LG, P_N2, P_NF, P_LOSS = 0, 48, 96, 104, 108, 620, 624, 632, 640, 648
P_ROWS = 656
Q_BMOD, Q_N1, Q_GAIN, Q_SGW, Q_SGB, Q_LG, Q_N2, Q_NF = 0, 48, 56, 60, 572, 576, 584, 592
Q_ROWS = 600


def _params(vmem_mb, sem=None):
    return pltpu.CompilerParams(vmem_limit_bytes=vmem_mb * VMEM_MB, dimension_semantics=sem)


def _const(shape):
    nd = len(shape)
    return pl.BlockSpec(shape, lambda *_: (0,) * nd, pipeline_mode=pl.Buffered(1))


def _pos():
    return lax.axis_index("x"), lax.axis_index("y"), lax.axis_index("c")


def _dot(a, b, dims):
    return lax.dot_general(a, b, (dims, ((), ())), preferred_element_type=F32)


NN = ((1,), (0,))
NT = ((1,), (1,))
TN = ((0,), (0,))


@jax.custom_vjp
def _mm(a, b):
    return _dot(a.astype(BF), b.astype(BF), NN)


def _mm_f(a, b):
    return _mm(a, b), (a.astype(BF), b.astype(BF))


def _mm_b(res, g):
    a, b = res
    gb = g.astype(BF)
    return _dot(gb, b, NT), _dot(a, gb, TN)


_mm.defvjp(_mm_f, _mm_b)


@jax.custom_vjp
def _mm_nt(a, b):
    return _dot(a.astype(BF), b.astype(BF), NT)


def _mm_nt_f(a, b):
    return _mm_nt(a, b), (a.astype(BF), b.astype(BF))


def _mm_nt_b(res, g):
    a, b = res
    gb = g.astype(BF)
    return _dot(gb, b, NN), _dot(gb, a, TN)


_mm_nt.defvjp(_mm_nt_f, _mm_nt_b)


@jax.custom_vjp
def _mm_tn(a, b):
    return _dot(a.astype(BF), b.astype(BF), TN)


def _mm_tn_f(a, b):
    return _mm_tn(a, b), (a.astype(BF), b.astype(BF))


def _mm_tn_b(res, g):
    a, b = res
    gb = g.astype(BF)
    return _dot(b, gb, NT), _dot(a, gb, NN)


_mm_tn.defvjp(_mm_tn_f, _mm_tn_b)


def _gelu(x):
    return x * (0.5 * (1.0 + jnp.tanh(0.7978845608028654 * (x + 0.044715 * (x * x * x)))))


def _silu(x):
    return x * jax.nn.sigmoid(x)


def _rms(x):
    return lax.rsqrt(jnp.mean(x * x, axis=-1, keepdims=True) + EPS)


def _swap32(t):
    lane = lax.broadcasted_iota(jnp.int32, t.shape, 1)
    first = (lane % 64) < 32
    return jnp.where(first, pltpu.roll(t, 96, 1), pltpu.roll(t, 32, 1))


def _rope(t, cos, sin):
    return t * cos + _swap32(t) * sin


def _rope_bwd(d, cos, sin):
    return d * cos + _swap32(d * sin)


def _heads(ref, r0=0):
    return [ref[r0:r0 + CH, h * HD:(h + 1) * HD].astype(F32) for h in range(NH)]


def _chunk_fwd(u, v, q, k, vr, gf, gb, sf, sb, df, xf, zf, db, xb, zb, sgw, gain, bfull):
    ya, yr, uf, ub = [], [], [], []
    for g in range(NH):
        gu = _gelu(u[g])
        gv = _gelu(v[g])
        vn = gv * _rms(gv) * gain[g]
        ya.append(gu * (_mm(sgw[g], vn) + bfull[g]))
    for h in range(NH):
        a = _mm_nt(q[h], k[h])
        of = _mm(a * df[h], vr[h]) + xf[h] * _mm(q[h], sf[h])
        ob = _mm(a * db[h], vr[h]) + xb[h] * _mm(q[h], sb[h])
        yr.append(_silu(gf[h]) * (of * _rms(of)) + _silu(gb[h]) * (ob * _rms(ob)))
        uf.append(_mm_tn(k[h], zf[h] * vr[h]))
        ub.append(_mm_tn(k[h], zb[h] * vr[h]))
    return ya, yr, uf, ub


def _fwd_dir_only(sf, q, k, vr, gf, df, xf):
    out = []
    for h in range(NH):
        a = _mm_nt(q[h], k[h])
        of = _mm(a * df[h], vr[h]) + xf[h] * _mm(q[h], sf[h])
        out.append(_silu(gf[h]) * (of * _rms(of)))
    return out


def _ctx_states(ctx0, ctx1, n1, csh, csc, wk, wv, zf, zb, ef, eb):
    hc0 = (ctx0 * _rms(ctx0) * n1) * (1.0 + csc) + csh
    hc1 = (ctx1 * _rms(ctx1) * n1) * (1.0 + csc) + csh
    scf, scb = [], []
    for h in range(NH):
        k0, k1 = _mm(hc0, wk[h]) * K_SCALE, _mm(hc1, wk[h]) * K_SCALE
        v0, v1 = _mm(hc0, wv[h]), _mm(hc1, wv[h])
        scf.append(ef[h] * _mm_tn(k0, zf[h] * v0) + _mm_tn(k1, zf[h] * v1))
        scb.append(eb[h] * _mm_tn(k1, zb[h] * v1) + _mm_tn(k0, zb[h] * v0))
    return scf, scb


def _allgather_small(v, name):
    r, n = v.shape

    def body(v_ref, out_ref, send_sems, recv_sems, local_sem):
        x, y, c = _pos()
        me = 4 * x + 2 * y + c
        mine = pltpu.make_async_copy(v_ref, out_ref.at[me], local_sem)
        mine.start()
        sent = []
        for k in range(1, NDEV):
            kx, ky, kc = (k >> 2) & 1, (k >> 1) & 1, k & 1
            peer = (x ^ kx, y ^ ky, c ^ kc)
            cp = pltpu.make_async_remote_copy(src_ref=v_ref, dst_ref=out_ref.at[me], send_sem=send_sems.at[k - 1],
                                              recv_sem=recv_sems.at[k - 1], device_id=peer, device_id_type=MESH)
            cp.start()
            sent.append(cp)
        for k in range(1, NDEV):
            kx, ky, kc = (k >> 2) & 1, (k >> 1) & 1, k & 1
            peer = (x ^ kx, y ^ ky, c ^ kc)
            src = 4 * (x ^ kx) + 2 * (y ^ ky) + (c ^ kc)
            pltpu.make_async_remote_copy(src_ref=v_ref, dst_ref=out_ref.at[src], send_sem=send_sems.at[k - 1],
                                         recv_sem=recv_sems.at[k - 1], device_id=peer, device_id_type=MESH).wait_recv()
        for cp in sent:
            cp.wait_send()
        mine.wait()

    return pl.pallas_call(
        body, name=name,
        out_shape=jax.ShapeDtypeStruct((NDEV, r, n), F32),
        in_specs=[pl.BlockSpec(memory_space=pltpu.VMEM)],
        out_specs=pl.BlockSpec(memory_space=pltpu.VMEM),
        scratch_shapes=[pltpu.SemaphoreType.DMA((NDEV - 1,)), pltpu.SemaphoreType.DMA((NDEV - 1,)),
                        pltpu.SemaphoreType.DMA],
        compiler_params=_params(16),
    )(v)


def _gather_weights(shards):
    nt = len(shards)
    shapes = [s.shape for s in shards]

    def body(*refs):
        srcs, outs, stages = refs[:nt], refs[nt:2 * nt], refs[2 * nt:3 * nt]
        ici_send, ici_recv, d2d_send, d2d_recv, local_sems = refs[3 * nt:]
        x, y, c = _pos()
        chip = 2 * x + y
        for t in range(nt):
            half = shapes[t][0] // 2
            stages[t][0] = srcs[t][0:half, :].astype(BF)
            stages[t][1] = srcs[t][half:2 * half, :].astype(BF)
        local = []
        for t in range(nt):
            cp = pltpu.make_async_copy(stages[t], outs[t].at[chip], local_sems.at[t])
            cp.start()
            local.append(cp)
        sent = []
        for k in range(1, NCHIP):
            kx, ky = (k >> 1) & 1, k & 1
            for t in range(nt):
                s = (k - 1) * nt + t
                cp = pltpu.make_async_remote_copy(
                    src_ref=stages[t].at[c], dst_ref=outs[t].at[chip, c], send_sem=ici_send.at[s],
                    recv_sem=ici_recv.at[s], device_id=(x ^ kx, y ^ ky, c), device_id_type=MESH)
                cp.start()
                sent.append(cp)
        for k in range(1, NCHIP):
            kx, ky = (k >> 1) & 1, k & 1
            src_chip = 2 * (x ^ kx) + (y ^ ky)
            for t in range(nt):
                s = (k - 1) * nt + t
                pltpu.make_async_remote_copy(
                    src_ref=stages[t].at[c], dst_ref=outs[t].at[src_chip, c], send_sem=ici_send.at[s],
                    recv_sem=ici_recv.at[s], device_id=(x ^ kx, y ^ ky, c), device_id_type=MESH).wait_recv()
                cp = pltpu.make_async_remote_copy(
                    src_ref=outs[t].at[src_chip, c], dst_ref=outs[t].at[src_chip, c], send_sem=d2d_send.at[s],
                    recv_sem=d2d_recv.at[s], device_id=(x, y, 1 - c), device_id_type=MESH)
                cp.start()
                sent.append(cp)
        for k in range(1, NCHIP):
            kx, ky = (k >> 1) & 1, k & 1
            src_chip = 2 * (x ^ kx) + (y ^ ky)
            for t in range(nt):
                s = (k - 1) * nt + t
                pltpu.make_async_remote_copy(
                    src_ref=stages[t].at[c], dst_ref=outs[t].at[src_chip, 1 - c], send_sem=d2d_send.at[s],
                    recv_sem=d2d_recv.at[s], device_id=(x, y, 1 - c), device_id_type=MESH).wait_recv()
        for cp in sent:
            cp.wait_send()
        for cp in local:
            cp.wait()

    n_rem = (NCHIP - 1) * nt
    out = pl.pallas_call(
        body, name="gather_weights",
        out_shape=[jax.ShapeDtypeStruct((NCHIP, 2, r // 2, cc), BF) for r, cc in shapes],
        in_specs=[pl.BlockSpec(memory_space=pltpu.VMEM)] * nt,
        out_specs=[pl.BlockSpec(memory_space=pl.ANY)] * nt,
        scratch_shapes=[pltpu.VMEM((2, r // 2, cc), BF) for r, cc in shapes]
        + [pltpu.SemaphoreType.DMA((n_rem,))] * 4 + [pltpu.SemaphoreType.DMA((nt,))],
        compiler_params=_params(48),
    )(*shards)
    return [o.reshape(NCHIP, r, cc) for o, (r, cc) in zip(out, shapes)]


def _chip_offsets():
    return [((k >> 1) & 1, k & 1) for k in range(1, NCHIP)]


def _prologue(c, c_ctx, w_mod_s, shards):
    nt = len(shards)
    shapes = [s.shape for s in shards]
    mod_c = w_mod_s.shape[1]

    def body(*refs):
        c_ref, cc_ref, wm_ref = refs[:3]
        srcs = refs[3:3 + nt]
        outs = refs[3 + nt:3 + 2 * nt]
        call_ref, prod_ref = refs[3 + 2 * nt:5 + 2 * nt]
        stages = refs[5 + 2 * nt:5 + 3 * nt]
        ct = refs[5 + 3 * nt]
        c_send, c_recv, p_send, p_recv, ici_send, ici_recv, d2d_send, d2d_recv, local_sems = refs[6 + 3 * nt:]
        x, y, c = _pos()
        chip = 2 * x + y
        me = 4 * x + 2 * y + c
        sib = (x, y, 1 - c)
        pending = []
        for t in range(nt):
            half = shapes[t][0] // 2
            stages[t][0] = srcs[t][0:half, :].astype(BF)
            stages[t][1] = srcs[t][half:2 * half, :].astype(BF)
            cp = pltpu.make_async_copy(stages[t], outs[t].at[chip], local_sems.at[t])
            cp.start()
            pending.append(cp)
        sends = []
        for k, (kx, ky) in enumerate(_chip_offsets()):
            cp = pltpu.make_async_remote_copy(src_ref=stages[0].at[c], dst_ref=outs[0].at[chip, c],
                                              send_sem=ici_send.at[k], recv_sem=ici_recv.at[k],
                                              device_id=(x ^ kx, y ^ ky, c), device_id_type=MESH)
            cp.start()
            sends.append(cp)

        def to_all(src, dst_of, send_sems, recv_sems):
            for k in range(1, NDEV):
                kx, ky, kc = (k >> 2) & 1, (k >> 1) & 1, k & 1
                cp = pltpu.make_async_remote_copy(src_ref=src, dst_ref=dst_of(me), send_sem=send_sems.at[k - 1],
                                                  recv_sem=recv_sems.at[k - 1], device_id=(x ^ kx, y ^ ky, c ^ kc),
                                                  device_id_type=MESH)
                cp.start()
                sends.append(cp)
            for k in range(1, NDEV):
                kx, ky, kc = (k >> 2) & 1, (k >> 1) & 1, k & 1
                frm = 4 * (x ^ kx) + 2 * (y ^ ky) + (c ^ kc)
                pltpu.make_async_remote_copy(src_ref=src, dst_ref=dst_of(frm), send_sem=send_sems.at[k - 1],
                                             recv_sem=recv_sems.at[k - 1], device_id=(x ^ kx, y ^ ky, c ^ kc),
                                             device_id_type=MESH).wait_recv()

        call_ref[me] = c_ref[...]
        to_all(c_ref, lambda d: call_ref.at[d], c_send, c_recv)
        ct[...] = jnp.zeros_like(ct)
        for d in range(NDEV):
            ct[d:d + 1, :] = call_ref[d]
        ct[NDEV:NDEV + 1, :] = cc_ref[...]
        prod_ref[me] = jnp.dot(_silu(ct[...]), wm_ref[...], precision=HI, preferred_element_type=F32)
        to_all(prod_ref.at[me], lambda d: prod_ref.at[d], p_send, p_recv)

        for k, (kx, ky) in enumerate(_chip_offsets()):
            frm = 2 * (x ^ kx) + (y ^ ky)
            pltpu.make_async_remote_copy(src_ref=stages[0].at[c], dst_ref=outs[0].at[frm, c],
                                         send_sem=ici_send.at[k], recv_sem=ici_recv.at[k],
                                         device_id=(x ^ kx, y ^ ky, c), device_id_type=MESH).wait_recv()
            cp = pltpu.make_async_remote_copy(src_ref=outs[0].at[frm, c], dst_ref=outs[0].at[frm, c],
                                              send_sem=d2d_send.at[k], recv_sem=d2d_recv.at[k],
                                              device_id=sib, device_id_type=MESH)
            cp.start()
            sends.append(cp)
        for k, (kx, ky) in enumerate(_chip_offsets()):
            frm = 2 * (x ^ kx) + (y ^ ky)
            pltpu.make_async_remote_copy(src_ref=stages[0].at[c], dst_ref=outs[0].at[frm, 1 - c],
                                         send_sem=d2d_send.at[k], recv_sem=d2d_recv.at[k],
                                         device_id=sib, device_id_type=MESH).wait_recv()
        for cp in sends:
            cp.wait_send()
        for cp in pending:
            cp.wait()

    vm = pl.BlockSpec(memory_space=pltpu.VMEM)
    out = pl.pallas_call(
        body, name="prologue",
        out_shape=[jax.ShapeDtypeStruct((NCHIP, 2, r // 2, cc), BF) for r, cc in shapes]
        + [jax.ShapeDtypeStruct((NDEV, 1, D), F32), jax.ShapeDtypeStruct((NDEV, 16, mod_c), F32)],
        in_specs=[vm] * (3 + nt),
        out_specs=[pl.BlockSpec(memory_space=pl.ANY)] * nt + [vm, vm],
        scratch_shapes=[pltpu.VMEM((2, r // 2, cc), BF) for r, cc in shapes] + [pltpu.VMEM((16, D), F32)]
        + [pltpu.SemaphoreType.DMA((NDEV - 1,))] * 4 + [pltpu.SemaphoreType.DMA((NCHIP - 1,))] * 4
        + [pltpu.SemaphoreType.DMA((nt,))],
        compiler_params=_params(56),
    )(c, c_ctx, w_mod_s, *shards)
    return out[:nt], out[nt], out[nt + 1]


def _gather_ici_copies(bufs, send_sems, recv_sems):
    x, y, c = _pos()
    chip = 2 * x + y
    nt = len(bufs)
    out_cp, in_cp = [], []
    for k, (kx, ky) in enumerate(_chip_offsets()):
        frm = 2 * (x ^ kx) + (y ^ ky)
        for t in range(nt):
            s = k * nt + t
            peer = (x ^ kx, y ^ ky, c)
            out_cp.append(pltpu.make_async_remote_copy(
                src_ref=bufs[t].at[chip, c], dst_ref=bufs[t].at[chip, c], send_sem=send_sems.at[s],
                recv_sem=recv_sems.at[s], device_id=peer, device_id_type=MESH))
            in_cp.append(pltpu.make_async_remote_copy(
                src_ref=bufs[t].at[chip, c], dst_ref=bufs[t].at[frm, c], send_sem=send_sems.at[s],
                recv_sem=recv_sems.at[s], device_id=peer, device_id_type=MESH))
    return out_cp, in_cp


def _gather_d2d_copies(bufs, send_sems, recv_sems):
    x, y, c = _pos()
    nt = len(bufs)
    out_cp, in_cp = [], []
    for k, (kx, ky) in enumerate(_chip_offsets()):
        frm = 2 * (x ^ kx) + (y ^ ky)
        for t in range(nt):
            s = k * nt + t
            out_cp.append(pltpu.make_async_remote_copy(
                src_ref=bufs[t].at[frm, c], dst_ref=bufs[t].at[frm, c], send_sem=send_sems.at[s],
                recv_sem=recv_sems.at[s], device_id=(x, y, 1 - c), device_id_type=MESH))
            in_cp.append(pltpu.make_async_remote_copy(
                src_ref=bufs[t].at[frm, c], dst_ref=bufs[t].at[frm, 1 - c], send_sem=send_sems.at[s],
                recv_sem=recv_sems.at[s], device_id=(x, y, 1 - c), device_id_type=MESH))
    return out_cp, in_cp


def _scatter_ici_copies(parts, outs, send_sems, recv_sems):
    x, y, c = _pos()
    nt = len(parts)
    cps = []
    for k, (kx, ky) in enumerate(_chip_offsets()):
        dst_chip = 2 * (x ^ kx) + (y ^ ky)
        for t in range(nt):
            s = k * nt + t
            cps.append(pltpu.make_async_remote_copy(
                src_ref=parts[t].at[dst_chip], dst_ref=outs[t].at[k], send_sem=send_sems.at[s],
                recv_sem=recv_sems.at[s], device_id=(x ^ kx, y ^ ky, c), device_id_type=MESH))
    return cps


def _rs_exchange_halves(grads, name):
    nt = len(grads)
    shapes = [g.shape for g in grads]

    def body(*refs):
        gs, outs = refs[:nt], refs[nt:2 * nt]
        send_sems, recv_sems = refs[2 * nt:]
        x, y, c = _pos()
        sib = (x, y, 1 - c)
        sent = []
        for t in range(nt):
            for j in range(NCHIP):
                s = t * NCHIP + j
                cp = pltpu.make_async_remote_copy(src_ref=gs[t].at[j, 1 - c], dst_ref=outs[t].at[j],
                                                  send_sem=send_sems.at[s], recv_sem=recv_sems.at[s],
                                                  device_id=sib, device_id_type=MESH)
                cp.start()
                sent.append(cp)
        for cp in sent:
            cp.wait_recv()
        for cp in sent:
            cp.wait_send()

    return pl.pallas_call(
        body, name=name,
        out_shape=[jax.ShapeDtypeStruct((NCHIP, s[2], s[3]), F32) for s in shapes],
        in_specs=[pl.BlockSpec(memory_space=pl.ANY)] * nt,
        out_specs=[pl.BlockSpec(memory_space=pl.ANY)] * nt,
        scratch_shapes=[pltpu.SemaphoreType.DMA((nt * NCHIP,))] * 2,
    )(*grads)


def _rs_send_chips(parts):
    nt = len(parts)
    shapes = [p.shape for p in parts]

    def body(*refs):
        ps, outs = refs[:nt], refs[nt:2 * nt]
        send_sems, recv_sems = refs[2 * nt:]
        x, y, c = _pos()
        sent = []
        for k in range(1, NCHIP):
            kx, ky = (k >> 1) & 1, k & 1
            dst_chip = 2 * (x ^ kx) + (y ^ ky)
            for t in range(nt):
                s = (k - 1) * nt + t
                cp = pltpu.make_async_remote_copy(src_ref=ps[t].at[dst_chip], dst_ref=outs[t].at[k - 1],
                                                  send_sem=send_sems.at[s], recv_sem=recv_sems.at[s],
                                                  device_id=(x ^ kx, y ^ ky, c), device_id_type=MESH)
                cp.start()
                sent.append(cp)
        for cp in sent:
            cp.wait_recv()
        for cp in sent:
            cp.wait_send()

    return pl.pallas_call(
        body, name="rs_send_chips",
        out_shape=[jax.ShapeDtypeStruct((NCHIP - 1, s[1], s[2]), BF) for s in shapes],
        in_specs=[pl.BlockSpec(memory_space=pl.ANY)] * nt,
        out_specs=[pl.BlockSpec(memory_space=pl.ANY)] * nt,
        scratch_shapes=[pltpu.SemaphoreType.DMA((nt * (NCHIP - 1),))] * 2,
    )(*parts)


def _rs_share_final(finals):
    nt = len(finals)
    shapes = [f.shape for f in finals]

    def body(*refs):
        fs, outs = refs[:nt], refs[nt:2 * nt]
        send_sems, recv_sems = refs[2 * nt:]
        x, y, c = _pos()
        sent = []
        for t in range(nt):
            cp = pltpu.make_async_remote_copy(src_ref=fs[t], dst_ref=outs[t], send_sem=send_sems.at[t],
                                              recv_sem=recv_sems.at[t], device_id=(x, y, 1 - c), device_id_type=MESH)
            cp.start()
            sent.append(cp)
        for cp in sent:
            cp.wait_recv()
        for cp in sent:
            cp.wait_send()

    return pl.pallas_call(
        body, name="rs_share_final",
        out_shape=[jax.ShapeDtypeStruct(s, F32) for s in shapes],
        in_specs=[pl.BlockSpec(memory_space=pl.ANY)] * nt,
        out_specs=[pl.BlockSpec(memory_space=pl.ANY)] * nt,
        scratch_shapes=[pltpu.SemaphoreType.DMA((nt,))] * 2,
    )(*finals)


def _row_tile(h, cc=D):
    for t in (512, 384, 352, 256, 176, 128, 64, 32, 16):
        if h % t == 0 and t * cc * 4 <= (5 * VMEM_MB) // 4:
            return t
    return h


def _rs_add_halves(g, recv, cidx, name):
    _, _, h, cc = g.shape
    th = _row_tile(h, cc)

    def body(c_ref, g_ref, r_ref, of_ref, ob_ref):
        s = g_ref[...] + r_ref[...]
        of_ref[...] = s
        ob_ref[...] = s.astype(BF)

    return pl.pallas_call(
        body, name=name,
        grid_spec=pltpu.PrefetchScalarGridSpec(
            num_scalar_prefetch=1, grid=(NCHIP, h // th),
            in_specs=[pl.BlockSpec((None, None, th, cc), lambda j, i, c_ref: (j, c_ref[0], i, 0)),
                      pl.BlockSpec((None, th, cc), lambda j, i, c_ref: (j, i, 0))],
            out_specs=[pl.BlockSpec((None, th, cc), lambda j, i, c_ref: (j, i, 0)),
                       pl.BlockSpec((None, th, cc), lambda j, i, c_ref: (j, i, 0))]),
        out_shape=[jax.ShapeDtypeStruct((NCHIP, h, cc), F32), jax.ShapeDtypeStruct((NCHIP, h, cc), BF)],
        compiler_params=_params(48),
    )(cidx, g, recv)


def _rs_add_chips(own, recv, chipidx, name):
    _, h, cc = own.shape
    th = _row_tile(h, cc)

    def body(j_ref, o_ref, r_ref, out_ref):
        out_ref[...] = ((o_ref[...] + r_ref[0].astype(F32)) + r_ref[1].astype(F32)) + r_ref[2].astype(F32)

    return pl.pallas_call(
        body, name=name,
        grid_spec=pltpu.PrefetchScalarGridSpec(
            num_scalar_prefetch=1, grid=(h // th,),
            in_specs=[pl.BlockSpec((None, th, cc), lambda i, j_ref: (j_ref[0], i, 0)),
                      pl.BlockSpec((NCHIP - 1, th, cc), lambda i, j_ref: (0, i, 0))],
            out_specs=pl.BlockSpec((th, cc), lambda i, j_ref: (i, 0))),
        out_shape=jax.ShapeDtypeStruct((h, cc), F32),
        compiler_params=_params(48),
    )(chipidx, own, recv)


def _adamw_math(w, g, m, v):
    m2 = B1 * m + (1.0 - B1) * g
    v2 = B2 * v + (1.0 - B2) * (g * g)
    m_hat = m2 / (1.0 - B1 ** STEP)
    v_hat = v2 / (1.0 - B2 ** STEP)
    delta = -LR * (m_hat / (jnp.sqrt(v_hat) + AEPS) + WD * w)
    return delta, m2, v2


def _adamw(w, g, m, v, name):
    r, cc = w.shape
    tr = _row_tile(r, cc)

    def body(w_ref, g_ref, m_ref, v_ref, d_ref, mo_ref, vo_ref):
        d, m2, v2 = _adamw_math(w_ref[...], g_ref[...], m_ref[...], v_ref[...])
        d_ref[...] = d
        mo_ref[...] = m2
        vo_ref[...] = v2

    spec = pl.BlockSpec((tr, cc), lambda i: (i, 0))
    return pl.pallas_call(
        body, name=name, grid=(r // tr,), in_specs=[spec] * 4, out_specs=[spec] * 3,
        out_shape=[jax.ShapeDtypeStruct((r, cc), F32)] * 3,
        compiler_params=_params(48, ("parallel",)),
    )(w, g, m, v)


def _adamw_halves(w, own, other, m, v, cidx, name):
    r, cc = w.shape
    h = r // 2
    tr = _row_tile(h, cc)
    per = h // tr

    def body(c_ref, w_ref, own_ref, oth_ref, m_ref, v_ref, g_ref, d_ref, mo_ref, vo_ref):
        mine = (pl.program_id(0) // per) == c_ref[0]
        g = jnp.where(mine, own_ref[...], oth_ref[...])
        g_ref[...] = g
        d, m2, v2 = _adamw_math(w_ref[...], g, m_ref[...], v_ref[...])
        d_ref[...] = d
        mo_ref[...] = m2
        vo_ref[...] = v2

    full = pl.BlockSpec((tr, cc), lambda i, c_ref: (i, 0))
    half = pl.BlockSpec((tr, cc), lambda i, c_ref: (i % per, 0))
    return pl.pallas_call(
        body, name=name,
        grid_spec=pltpu.PrefetchScalarGridSpec(
            num_scalar_prefetch=1, grid=(r // tr,),
            in_specs=[full, half, half, full, full], out_specs=[full] * 4),
        out_shape=[jax.ShapeDtypeStruct((r, cc), F32)] * 4,
        compiler_params=_params(48, ("parallel",)),
    )(cidx, w, own, other, m, v)


def _mod_forward(ct_pad, w_mod_s):
    def body(c_ref, w_ref, o_ref):
        o_ref[...] = jnp.dot(_silu(c_ref[...]), w_ref[...], precision=HI, preferred_element_type=F32)

    return pl.pallas_call(
        body, name="mod_forward",
        out_shape=jax.ShapeDtypeStruct((16, w_mod_s.shape[1]), F32),
        in_specs=[pl.BlockSpec(memory_space=pltpu.VMEM)] * 2,
        out_specs=pl.BlockSpec(memory_space=pltpu.VMEM),
        compiler_params=_params(32),
    )(ct_pad, w_mod_s)


def _decay_exponents():
    ri = lax.broadcasted_iota(jnp.int32, (CH, CH), 0).astype(F32)
    ci = lax.broadcasted_iota(jnp.int32, (CH, CH), 1).astype(F32)
    full = jnp.full((CH, CH), float(CH), F32)
    return [[ri - ci, ri + 1.0, (CH - 1.0) - ri, full], [ci - ri, CH - ri, ri, full]]


def _decay_mats(logit_full):
    def body(l_ref, o_ref):
        ex = _decay_exponents()
        for d in range(2):
            for h in range(NH):
                lv = l_ref[d * NH + h]
                lg = jnp.minimum(lv, 0.0) - jnp.log(1.0 + jnp.exp(-jnp.abs(lv)))
                for kind in range(4):
                    m = jnp.exp(lg * ex[d][kind])
                    if kind == 0:
                        m = jnp.where(ex[d][0] >= 0.0, jnp.exp(lg * jnp.maximum(ex[d][0], 0.0)), 0.0)
                    o_ref[d, kind, h] = m

    return pl.pallas_call(
        body, name="decay_mats",
        out_shape=jax.ShapeDtypeStruct((2, 4, NH, CH, CH), F32),
        in_specs=[pl.BlockSpec(memory_space=pltpu.VMEM)],
        out_specs=pl.BlockSpec(memory_space=pltpu.VMEM),
        compiler_params=_params(32),
    )(logit_full)


def _ctx_kv_weights(wi_ref):
    def cols(g):
        return wi_ref[g // WI_C, :, g % WI_C: g % WI_C + HD].astype(F32)

    wk = [cols(3 * AW + h * HD) for h in range(NH)]
    wv = [cols(4 * AW + h * HD) for h in range(NH)]
    return wk, wv


def _ctx_forward(ctx, vecs, wi, dm):
    def body(ctx_ref, v_ref, wi_ref, dm_ref, scf_ref, scb_ref):
        wk, wv = _ctx_kv_weights(wi_ref)
        mats = [[dm_ref[d, kind, h] for h in range(NH)] for d in range(2) for kind in (2, 3)]
        scf, scb = _ctx_states(ctx_ref[0:CH, :], ctx_ref[CH:2 * CH, :], v_ref[0:1, :], v_ref[1:2, :],
                               v_ref[2:3, :], wk, wv, mats[0], mats[2], mats[1], mats[3])
        for h in range(NH):
            scf_ref[h] = scf[h]
            scb_ref[h] = scb[h]

    return pl.pallas_call(
        body, name="ctx_forward",
        out_shape=[jax.ShapeDtypeStruct((NH, HD, HD), F32)] * 2,
        in_specs=[pl.BlockSpec(memory_space=pltpu.VMEM)] * 4,
        out_specs=[pl.BlockSpec(memory_space=pltpu.VMEM)] * 2,
        compiler_params=_params(48),
    )(ctx, vecs, wi, dm)


def _ctx_backward(ctx, vecs, wi, dm, dscf, dscb):
    def body(ctx_ref, v_ref, wi_ref, dm_ref, gf_ref, gb_ref, gw_ref, gv_ref, gdm_ref):
        wk, wv = _ctx_kv_weights(wi_ref)
        mats = [[dm_ref[d, kind, h] for h in range(NH)] for d in range(2) for kind in (2, 3)]
        ctx0, ctx1 = ctx_ref[0:CH, :], ctx_ref[CH:2 * CH, :]

        def fn(n1, csh, csc, wk_, wv_, zf, zb, ef, eb):
            return _ctx_states(ctx0, ctx1, n1, csh, csc, wk_, wv_, zf, zb, ef, eb)

        _, vjp = jax.vjp(fn, v_ref[0:1, :], v_ref[1:2, :], v_ref[2:3, :], wk, wv,
                         mats[0], mats[2], mats[1], mats[3])
        cot = ([gf_ref[h] for h in range(NH)], [gb_ref[h] for h in range(NH)])
        dn1, dcsh, dcsc, dwk, dwv, dzf, dzb, def_, deb = vjp(cot)
        for h in range(NH):
            gw_ref[:, h * HD:(h + 1) * HD] = dwk[h]
            gw_ref[:, AW + h * HD:AW + (h + 1) * HD] = dwv[h]
        gv_ref[...] = jnp.zeros_like(gv_ref)
        gv_ref[0:1, :] = dn1
        gv_ref[1:2, :] = dcsh
        gv_ref[2:3, :] = dcsc
        for h in range(NH):
            gdm_ref[0, 0, h] = dzf[h]
            gdm_ref[0, 1, h] = def_[h]
            gdm_ref[1, 0, h] = dzb[h]
            gdm_ref[1, 1, h] = deb[h]

    return pl.pallas_call(
        body, name="ctx_backward",
        out_shape=[jax.ShapeDtypeStruct((D, 2 * AW), F32), jax.ShapeDtypeStruct((8, D), F32),
                   jax.ShapeDtypeStruct((2, 2, NH, CH, CH), F32)],
        in_specs=[pl.BlockSpec(memory_space=pltpu.VMEM)] * 6,
        out_specs=[pl.BlockSpec(memory_space=pltpu.VMEM)] * 3,
        compiler_params=_params(56),
    )(ctx, vecs, wi, dm, dscf, dscb)


def _in_proj(x, vecs, wi, gbufs):
    ln = x.shape[0]
    t = min(512, ln)
    nt = len(gbufs)
    steps = ln // t

    def body(x_ref, v_ref, wi_ref, *refs):
        z_ref, hx_ref = refs[nt:nt + 2]
        bufs = refs[nt + 2:2 * nt + 2]
        send_sems, recv_sems = refs[2 * nt + 2:]
        i = pl.program_id(0)

        @pl.when(i == 0)
        def _():
            for cp in _gather_ici_copies(bufs, send_sems, recv_sems)[0]:
                cp.start()

        xv = x_ref[...]
        hx = (xv * _rms(xv) * v_ref[0:1, :]) * (1.0 + v_ref[2:3, :]) + v_ref[1:2, :]
        hb = hx.astype(BF)
        hx_ref[...] = hb
        for j in range(NCHIP):
            z_ref[:, j * WI_C:(j + 1) * WI_C] = _dot(hb, wi_ref[j], NN)

        @pl.when(i == steps - 1)
        def _():
            out_cp, in_cp = _gather_ici_copies(bufs, send_sems, recv_sems)
            for cp in in_cp:
                cp.wait_recv()
            for cp in out_cp:
                cp.wait_send()

    hbm = pl.BlockSpec(memory_space=pl.ANY)
    out = pl.pallas_call(
        body, name="in_proj", grid=(steps,),
        in_specs=[pl.BlockSpec((t, D), lambda i: (i, 0)), _const((8, D)), _const((NCHIP, D, WI_C))] + [hbm] * nt,
        out_specs=[pl.BlockSpec((t, IN_COLS), lambda i: (i, 0)), pl.BlockSpec((t, D), lambda i: (i, 0))] + [hbm] * nt,
        out_shape=[jax.ShapeDtypeStruct((ln, IN_COLS), F32), jax.ShapeDtypeStruct((ln, D), BF)]
        + [jax.ShapeDtypeStruct(g.shape, g.dtype) for g in gbufs],
        input_output_aliases={3 + k: 2 + k for k in range(nt)},
        scratch_shapes=[pltpu.SemaphoreType.DMA(((NCHIP - 1) * nt,))] * 2,
        compiler_params=_params(56, ("arbitrary",)),
    )(x, vecs, wi, *gbufs)
    return out[0], out[1], out[2:]


def _in_proj_bwd(dz, x, dx1, vecs, wi, parts):
    ln = x.shape[0]
    t = min(512, ln)
    nt = len(parts)
    steps = ln // t

    def body(dz_ref, x_ref, dx1_ref, v_ref, wi_ref, *refs):
        ps = refs[:nt]
        gx_ref, acc_ref = refs[nt:nt + 2]
        got = refs[nt + 2:2 * nt + 2]
        send_sems, recv_sems = refs[2 * nt + 2:]

        @pl.when(pl.program_id(0) == 0)
        def _():
            acc_ref[...] = jnp.zeros_like(acc_ref)
            for cp in _scatter_ici_copies(ps, got, send_sems, recv_sems):
                cp.start()

        dhx = jnp.zeros((t, D), F32)
        for j in range(NCHIP):
            dhx = dhx + _dot(dz_ref[:, j * WI_C:(j + 1) * WI_C], wi_ref[j], NT)
        xv = x_ref[...]
        r = _rms(xv)
        xn = xv * r
        n1, sc = v_ref[0:1, :], v_ref[2:3, :]
        acc_ref[0:1, :] += jnp.sum(dhx * xn * (1.0 + sc), axis=0, keepdims=True)
        acc_ref[1:2, :] += jnp.sum(dhx, axis=0, keepdims=True)
        acc_ref[2:3, :] += jnp.sum(dhx * xn * n1, axis=0, keepdims=True)
        g = dhx * n1 * (1.0 + sc)
        gx_ref[...] = dx1_ref[...] + r * (g - xn * jnp.mean(g * xn, axis=-1, keepdims=True))

        @pl.when(pl.program_id(0) == steps - 1)
        def _():
            cps = _scatter_ici_copies(ps, got, send_sems, recv_sems)
            for cp in cps:
                cp.wait_recv()
            for cp in cps:
                cp.wait_send()

    hbm = pl.BlockSpec(memory_space=pl.ANY)
    out = pl.pallas_call(
        body, name="in_proj_bwd", grid=(steps,),
        in_specs=[pl.BlockSpec((t, IN_COLS), lambda i: (i, 0)), pl.BlockSpec((t, D), lambda i: (i, 0)),
                  pl.BlockSpec((t, D), lambda i: (i, 0)), _const((8, D)), _const((NCHIP, D, WI_C))] + [hbm] * nt,
        out_specs=[pl.BlockSpec((t, D), lambda i: (i, 0)), pl.BlockSpec((8, D), lambda i: (0, 0))] + [hbm] * nt,
        out_shape=[jax.ShapeDtypeStruct((ln, D), F32), jax.ShapeDtypeStruct((8, D), F32)]
        + [jax.ShapeDtypeStruct((NCHIP - 1,) + p.shape[1:], BF) for p in parts],
        scratch_shapes=[pltpu.SemaphoreType.DMA(((NCHIP - 1) * nt,))] * 2,
        compiler_params=_params(56, ("arbitrary",)),
    )(dz, x, dx1, vecs, wi, *parts)
    return out[0], out[1], out[2:]


def _post_mixer(x, ycat, tgt, vecs, wo, wg, wu, wd):
    ln = x.shape[0]
    t = min(256, ln)

    def body(x_ref, y_ref, t_ref, v_ref, wo_ref, wg_ref, wu_ref, wd_ref,
             dx1_ref, dyc_ref, h2_ref, dy_ref, df_ref, act_ref, da_ref, db_ref, acc_ref, a_st, b_st):
        @pl.when(pl.program_id(0) == 0)
        def _():
            acc_ref[...] = jnp.zeros_like(acc_ref)

        g1, n2, sh2, sc2 = v_ref[0:1, :], v_ref[1:2, :], v_ref[2:3, :], v_ref[3:4, :]
        g2, nf = v_ref[4:5, :], v_ref[5:6, :]
        y = _dot(y_ref[...], wo_ref[...], NN)
        x1 = x_ref[...] + g1 * y
        r2 = _rms(x1)
        xn2 = x1 * r2
        t2 = xn2 * n2
        h2b = (t2 * (1.0 + sc2) + sh2).astype(BF)
        h2_ref[...] = h2b
        f = jnp.zeros((t, D), F32)
        for j in range(NCHIP):
            a = _dot(h2b, wg_ref[j], NT)
            b = _dot(h2b, wu_ref[j], NT)
            a_st[j] = a
            b_st[j] = b
            act = (_silu(a) * b).astype(BF)
            act_ref[j] = act
            f = f + _dot(act, wd_ref[j], NN)
        x2 = x1 + g2 * f
        r3 = _rms(x2)
        xn3 = x2 * r3
        e = xn3 * nf - t_ref[...]
        acc_ref[6:7, :] += jnp.sum(e * e, axis=0, keepdims=True) * (0.5 / D)
        dout = e * (1.0 / D)
        acc_ref[5:6, :] += jnp.sum(dout * xn3, axis=0, keepdims=True)
        gg = dout * nf
        dx2 = r3 * (gg - xn3 * jnp.mean(gg * xn3, axis=-1, keepdims=True))
        acc_ref[4:5, :] += jnp.sum(dx2 * f, axis=0, keepdims=True)
        dfb = (g2 * dx2).astype(BF)
        df_ref[...] = dfb
        dh2 = jnp.zeros((t, D), F32)
        for j in range(NCHIP):
            dact = _dot(dfb, wd_ref[j], NT)
            a = a_st[j]
            b = b_st[j]
            s = jax.nn.sigmoid(a)
            da = (dact * b * (s * (1.0 + a * (1.0 - s)))).astype(BF)
            db = (dact * (a * s)).astype(BF)
            da_ref[j] = da
            db_ref[j] = db
            dh2 = dh2 + _dot(da, wg_ref[j], NN) + _dot(db, wu_ref[j], NN)
        acc_ref[2:3, :] += jnp.sum(dh2, axis=0, keepdims=True)
        acc_ref[3:4, :] += jnp.sum(dh2 * t2, axis=0, keepdims=True)
        acc_ref[1:2, :] += jnp.sum(dh2 * xn2 * (1.0 + sc2), axis=0, keepdims=True)
        gx = dh2 * n2 * (1.0 + sc2)
        dx1 = dx2 + r2 * (gx - xn2 * jnp.mean(gx * xn2, axis=-1, keepdims=True))
        dx1_ref[...] = dx1
        acc_ref[0:1, :] += jnp.sum(dx1 * y, axis=0, keepdims=True)
        dyb = (g1 * dx1).astype(BF)
        dy_ref[...] = dyb
        dyc_ref[...] = _dot(dyb, wo_ref[...], NT)

    tok = pl.BlockSpec((t, D), lambda i: (i, 0))
    ffb = pl.BlockSpec((NCHIP, t, FF_C), lambda i: (0, i, 0))
    return pl.pallas_call(
        body, name="post_mixer", grid=(ln // t,),
        in_specs=[tok, tok, tok, _const((8, D)), _const((D, D)), _const((NCHIP, FF_C, D)),
                  _const((NCHIP, FF_C, D)), _const((NCHIP, FF_C, D))],
        out_specs=[tok, tok, tok, tok, tok, ffb, ffb, ffb, pl.BlockSpec((16, D), lambda i: (0, 0))],
        out_shape=[jax.ShapeDtypeStruct((ln, D), F32)] * 2 + [jax.ShapeDtypeStruct((ln, D), BF)] * 3
        + [jax.ShapeDtypeStruct((NCHIP, ln, FF_C), BF)] * 3 + [jax.ShapeDtypeStruct((16, D), F32)],
        scratch_shapes=[pltpu.VMEM((NCHIP, t, FF_C), F32)] * 2,
        compiler_params=_params(60, ("arbitrary",)),
    )(x, ycat, tgt, vecs, wo, wg, wu, wd)


def _exchange_copies(g, out, send_sems, recv_sems):
    x, y, c = _pos()
    return [pltpu.make_async_remote_copy(src_ref=g.at[j, 1 - c], dst_ref=out.at[j], send_sem=send_sems.at[j],
                                         recv_sem=recv_sems.at[j], device_id=(x, y, 1 - c), device_id_type=MESH)
            for j in range(NCHIP)]


def _tn_matmul(xa, dy, name, nb, k1, n, x_batched, dy_mode, tt, ctx_kv=None, carry=None):
    ln = xa.shape[-2]
    tt = min(tt, ln)
    steps = ln // tt
    n_in = 2 + (ctx_kv is not None) + (carry is not None)

    def body(x_ref, dy_ref, *refs):
        o_ref = refs[n_in - 2]
        if carry is not None:
            g_ref, got_ref = refs[n_in - 3], refs[n_in - 1]
            send_sems, recv_sems = refs[n_in:]

        @pl.when(pl.program_id(0) == 0)
        def _():
            if carry is not None:
                for cp in _exchange_copies(g_ref, got_ref, send_sems, recv_sems):
                    cp.start()
            o_ref[...] = jnp.zeros_like(o_ref)
            if ctx_kv is not None:
                for g in range(0, 2 * AW, HD):
                    col = 3 * AW + g
                    o_ref[col // n, :, col % n: col % n + HD] = refs[0][:, g:g + HD]

        xt = None if x_batched else jnp.transpose(x_ref[...])
        for b in range(nb):
            lhs = jnp.transpose(x_ref[b]) if x_batched else xt
            if dy_mode == "batched":
                rhs = dy_ref[b]
            elif dy_mode == "cols":
                rhs = dy_ref[:, b * n:(b + 1) * n]
            else:
                rhs = dy_ref[...]
            o_ref[b] += _dot(lhs, rhs, NN)

        if carry is not None:
            @pl.when(pl.program_id(0) == steps - 1)
            def _():
                cps = _exchange_copies(g_ref, got_ref, send_sems, recv_sems)
                for cp in cps:
                    cp.wait_recv()
                for cp in cps:
                    cp.wait_send()

    x_spec = (pl.BlockSpec((nb, tt, k1), lambda t: (0, t, 0)) if x_batched
              else pl.BlockSpec((tt, k1), lambda t: (t, 0)))
    if dy_mode == "batched":
        dy_spec = pl.BlockSpec((nb, tt, n), lambda t: (0, t, 0))
    elif dy_mode == "cols":
        dy_spec = pl.BlockSpec((tt, nb * n), lambda t: (t, 0))
    else:
        dy_spec = pl.BlockSpec((tt, n), lambda t: (t, 0))
    hbm = pl.BlockSpec(memory_space=pl.ANY)
    extra = [] if ctx_kv is None else [ctx_kv]
    in_specs = [x_spec, dy_spec] + [_const(e.shape) for e in extra]
    out_specs = [pl.BlockSpec((nb, k1, n), lambda t: (0, 0, 0))]
    out_shape = [jax.ShapeDtypeStruct((nb, k1, n), F32)]
    scratch = []
    if carry is not None:
        extra = extra + [carry]
        in_specs.append(hbm)
        out_specs.append(hbm)
        out_shape.append(jax.ShapeDtypeStruct((NCHIP,) + carry.shape[2:], F32))
        scratch = [pltpu.SemaphoreType.DMA((NCHIP,))] * 2
    out = pl.pallas_call(
        body, name=name, grid=(steps,),
        in_specs=in_specs, out_specs=out_specs, out_shape=out_shape, scratch_shapes=scratch,
        compiler_params=_params(60, ("arbitrary",)),
    )(xa, dy, *extra)
    return out[0] if carry is None else (out[0], out[1])


def _add_ctx_cols(gwi, gwkv):
    first = 1536 // HD
    per = WI_C // HD

    def body(g_ref, a_ref, o_ref):
        o_ref[...] = g_ref[...] + a_ref[...]

    spec = pl.BlockSpec((None, D, HD), lambda i: ((first + i) // per, 0, (first + i) % per))
    return pl.pallas_call(
        body, name="add_ctx_cols", grid=(2 * AW // HD,),
        in_specs=[spec, pl.BlockSpec((D, HD), lambda i: (0, i))],
        out_specs=spec,
        out_shape=jax.ShapeDtypeStruct(gwi.shape, F32),
        input_output_aliases={0: 0},
        compiler_params=_params(32, ("arbitrary",)),
    )(gwi, gwkv)


FWD_CHUNKS_PER_STEP = 4
BWD_CHUNKS_PER_STEP = 4


def _chunks_per_step(nc, want):
    return want if nc % want == 0 else 1


def _mixer_fwd(z, cos_t, sin_t, dm, sgw, gain, bfull, scf, scb, gbufs_a, gbufs_b):
    ln = z.shape[0]
    nc = ln // CH
    na = len(gbufs_a)
    gbufs = list(gbufs_a) + list(gbufs_b)
    nt = len(gbufs)
    cps = _chunks_per_step(nc, FWD_CHUNKS_PER_STEP)
    nb = nc // cps
    rows = cps * CH
    mid = nb // 2

    def rev(p, n):
        return p * n + (1 - p) * (nb - 1 - n)

    def col(j, both):
        if both:
            return pl.BlockSpec((rows, AW), lambda p, n: (rev(p, n), j))
        return pl.BlockSpec((rows, AW), lambda p, n: (p * n, j))

    def body(u_ref, v_ref, q_ref, k_ref, vr_ref, gf_ref, gb_ref, cos_ref, sin_ref, dm_ref, sgw_ref, gain_ref,
             bfull_ref, scf_ref, scb_ref, *refs):
        y_ref, sf_ref, sb_ref = refs[nt:nt + 3]
        bufs = refs[nt + 3:2 * nt + 3]
        bufs_a, bufs_b = bufs[:na], bufs[na:]
        sb_all, st, a_send, a_recv, bi_send, bi_recv, bd_send, bd_recv = refs[2 * nt + 3:]
        p, n = pl.program_id(0), pl.program_id(1)

        @pl.when((p == 0) & (n == 0))
        def _():
            for cp in _gather_d2d_copies(bufs_a, a_send, a_recv)[0]:
                cp.start()
            for cp in _gather_ici_copies(bufs_b, bi_send, bi_recv)[0]:
                cp.start()

        @pl.when((p == 1) & (n == mid))
        def _():
            for cp in _gather_ici_copies(bufs_b, bi_send, bi_recv)[1]:
                cp.wait_recv()
            for cp in _gather_d2d_copies(bufs_b, bd_send, bd_recv)[0]:
                cp.start()

        def roped_k(r0):
            cos, sin = cos_ref[r0:r0 + CH, :], sin_ref[r0:r0 + CH, :]
            return [_rope(t, cos, sin) * K_SCALE for t in _heads(k_ref, r0)]

        @pl.when(p == 0)
        def _():
            @pl.when(n == 0)
            def _():
                st[...] = scb_ref[...]

            for s in reversed(range(cps)):
                m = (nb - 1 - n) * cps + s
                k, vr = roped_k(s * CH), _heads(vr_ref, s * CH)
                for h in range(NH):
                    sb_all[m, h] = st[h]
                    st[h] = dm_ref[1, 3, h] * st[h] + _mm_tn(k[h], dm_ref[1, 2, h] * vr[h])

        @pl.when(p == 1)
        def _():
            @pl.when(n == 0)
            def _():
                st[...] = scf_ref[...]

            mats = [[dm_ref[d, kind, h] for h in range(NH)] for d in range(2) for kind in range(3)]
            for s in range(cps):
                r0 = s * CH
                m = n * cps + s
                cos, sin = cos_ref[r0:r0 + CH, :], sin_ref[r0:r0 + CH, :]
                q = [_rope(t, cos, sin) for t in _heads(q_ref, r0)]
                sf = [st[h] for h in range(NH)]
                sb = [sb_all[m, h] for h in range(NH)]
                ya, yr, uf, _ = _chunk_fwd(
                    _heads(u_ref, r0), _heads(v_ref, r0), q, roped_k(r0), _heads(vr_ref, r0), _heads(gf_ref, r0),
                    _heads(gb_ref, r0), sf, sb, mats[0], mats[1], mats[2], mats[3], mats[4], mats[5],
                    [sgw_ref[g] for g in range(NH)], [gain_ref[:, g * HD:(g + 1) * HD] for g in range(NH)],
                    [bfull_ref[g] for g in range(NH)])
                for h in range(NH):
                    y_ref[r0:r0 + CH, h * HD:(h + 1) * HD] = ya[h].astype(BF)
                    y_ref[r0:r0 + CH, AW + h * HD:AW + (h + 1) * HD] = yr[h].astype(BF)
                    sf_ref[s, h] = sf[h]
                    sb_ref[s, h] = sb[h]
                    st[h] = dm_ref[0, 3, h] * st[h] + uf[h]

        @pl.when((p == 1) & (n == nb - 1))
        def _():
            a_out, a_in = _gather_d2d_copies(bufs_a, a_send, a_recv)
            b_out, b_in = _gather_d2d_copies(bufs_b, bd_send, bd_recv)
            for cp in a_in + b_in:
                cp.wait_recv()
            for cp in a_out + b_out + _gather_ici_copies(bufs_b, bi_send, bi_recv)[0]:
                cp.wait_send()

    hbm = pl.BlockSpec(memory_space=pl.ANY)
    tab = pl.BlockSpec((rows, HD), lambda p, n: (rev(p, n), 0))
    st_spec = pl.BlockSpec((cps, NH, HD, HD), lambda p, n: (p * n, 0, 0, 0))
    out = pl.pallas_call(
        body, name="mixer_fwd", grid=(2, nb),
        in_specs=[col(0, False), col(1, False), col(2, False), col(3, True), col(4, True), col(5, False),
                  col(6, False), tab, tab, _const((2, 4, NH, CH, CH)), _const((NH, CH, CH)), _const((1, AW)),
                  _const((NH, CH, CH)), _const((NH, HD, HD)), _const((NH, HD, HD))] + [hbm] * nt,
        out_specs=[pl.BlockSpec((rows, D), lambda p, n: (p * n, 0)), st_spec, st_spec] + [hbm] * nt,
        out_shape=[jax.ShapeDtypeStruct((ln, D), BF), jax.ShapeDtypeStruct((nc, NH, HD, HD), F32),
                   jax.ShapeDtypeStruct((nc, NH, HD, HD), F32)]
        + [jax.ShapeDtypeStruct(g.shape, g.dtype) for g in gbufs],
        input_output_aliases={15 + k: 3 + k for k in range(nt)},
        scratch_shapes=[pltpu.VMEM((nc, NH, HD, HD), F32), pltpu.VMEM((NH, HD, HD), F32)]
        + [pltpu.SemaphoreType.DMA(((NCHIP - 1) * na,))] * 2
        + [pltpu.SemaphoreType.DMA(((NCHIP - 1) * (nt - na),))] * 4,
        compiler_params=_params(56, ("arbitrary", "arbitrary")),
    )(z, z, z, z, z, z, z, cos_t, sin_t, dm, sgw, gain, bfull, scf, scb, *gbufs)
    return out[0], out[1], out[2], out[3:]


def _mixer_bwd(z, dycat, cos_t, sin_t, dm, sgw, gain, bfull, sf_all, sb_all, parts):
    ln = z.shape[0]
    nc = ln // CH
    cps = _chunks_per_step(nc, BWD_CHUNKS_PER_STEP)
    nb = nc // cps
    rows = cps * CH

    def rev(p, n):
        return p * n + (1 - p) * (nb - 1 - n)

    def col(j, both):
        if both:
            return pl.BlockSpec((rows, AW), lambda p, n: (rev(p, n), j))
        return pl.BlockSpec((rows, AW), lambda p, n: (p * n, j))

    nt = len(parts)

    def body(u_ref, v_ref, q_ref, k_ref, vr_ref, gf_ref, gb_ref, dya_ref, dyr_ref, cos_ref, sin_ref, dm_ref,
             sgw_ref, gain_ref, bfull_ref, sf_ref, sb_ref, *refs):
        ps = refs[:nt]
        dz_ref, ddm_ref, dsgw_ref, dgain_ref, dbf_ref, dscf_ref, dscb_ref = refs[nt:nt + 7]
        got = refs[nt + 7:2 * nt + 7]
        gf_all, run, send_sems, recv_sems = refs[2 * nt + 7:]
        p, n = pl.program_id(0), pl.program_id(1)

        @pl.when((p == 0) & (n == 0))
        def _():
            for cp in _scatter_ici_copies(ps, got, send_sems, recv_sems):
                cp.start()

        mats = [[dm_ref[d, kind, h] for h in range(NH)] for d in range(2) for kind in range(3)]

        def chunk_inputs(s):
            r0 = s * CH
            cos, sin = cos_ref[r0:r0 + CH, :], sin_ref[r0:r0 + CH, :]
            q = [_rope(t, cos, sin) for t in _heads(q_ref, r0)]
            k = [_rope(t, cos, sin) * K_SCALE for t in _heads(k_ref, r0)]
            sf = [sf_ref[s, h] for h in range(NH)]
            return cos, sin, q, k, _heads(vr_ref, r0), _heads(gf_ref, r0), _heads(dyr_ref, r0), sf

        @pl.when(p == 0)
        def _():
            @pl.when(n == 0)
            def _():
                run[...] = jnp.zeros_like(run)
                ddm_ref[...] = jnp.zeros_like(ddm_ref)
                dsgw_ref[...] = jnp.zeros_like(dsgw_ref)
                dgain_ref[...] = jnp.zeros_like(dgain_ref)
                dbf_ref[...] = jnp.zeros_like(dbf_ref)

            for s in reversed(range(cps)):
                m = (nb - 1 - n) * cps + s
                _, _, q, k, vr, gf, dyr, sf = chunk_inputs(s)
                _, vjp = jax.vjp(lambda st: _fwd_dir_only(st, q, k, vr, gf, mats[0], mats[1]), sf)
                (dsf,) = vjp(dyr)
                for h in range(NH):
                    g_next = run[h]
                    gf_all[m, h] = g_next.astype(BF)
                    ddm_ref[0, 3, h] += sf[h] * g_next
                    run[h] = dsf[h] + dm_ref[0, 3, h] * g_next

            @pl.when(n == nb - 1)
            def _():
                dscf_ref[...] = run[...]

        @pl.when(p == 1)
        def _():
            @pl.when(n == 0)
            def _():
                run[...] = jnp.zeros_like(run)

            for s in range(cps):
                r0 = s * CH
                m = n * cps + s
                cos, sin, q, k, vr, gf, dyr, sf = chunk_inputs(s)
                sb = [sb_ref[s, h] for h in range(NH)]
                g_f = [gf_all[m, h].astype(F32) for h in range(NH)]
                g_b = [run[h] for h in range(NH)]
                args = (_heads(u_ref, r0), _heads(v_ref, r0), q, k, vr, gf, _heads(gb_ref, r0), sb,
                        mats[0], mats[1], mats[2], mats[3], mats[4], mats[5],
                        [sgw_ref[g] for g in range(NH)], [gain_ref[:, g * HD:(g + 1) * HD] for g in range(NH)],
                        [bfull_ref[g] for g in range(NH)])

                def fn(u_, v_, q_, k_, vr_, gf_, gb_, sb_, df, xf, zf, db, xb, zb, sgw_, gain_, bfull_, sf=sf):
                    return _chunk_fwd(u_, v_, q_, k_, vr_, gf_, gb_, sf, sb_, df, xf, zf, db, xb, zb, sgw_,
                                      gain_, bfull_)

                _, vjp = jax.vjp(fn, *args)
                (du, dv, dq, dk, dvr, dgf, dgb, dsb, ddf, dxf, dzf, ddb, dxb, dzb, dsgw, dgain, dbf) = vjp(
                    (_heads(dya_ref, r0), dyr, g_f, g_b))
                rw = slice(r0, r0 + CH)
                for h in range(NH):
                    cs = slice(h * HD, (h + 1) * HD)
                    dz_ref[rw, h * HD:(h + 1) * HD] = du[h].astype(BF)
                    dz_ref[rw, AW + h * HD:AW + (h + 1) * HD] = dv[h].astype(BF)
                    dz_ref[rw, 2 * AW + h * HD:2 * AW + (h + 1) * HD] = _rope_bwd(dq[h], cos, sin).astype(BF)
                    dz_ref[rw, 3 * AW + h * HD:3 * AW + (h + 1) * HD] = _rope_bwd(dk[h] * K_SCALE, cos,
                                                                                  sin).astype(BF)
                    dz_ref[rw, 4 * AW + h * HD:4 * AW + (h + 1) * HD] = dvr[h].astype(BF)
                    dz_ref[rw, 5 * AW + h * HD:5 * AW + (h + 1) * HD] = dgf[h].astype(BF)
                    dz_ref[rw, 6 * AW + h * HD:6 * AW + (h + 1) * HD] = dgb[h].astype(BF)
                    ddm_ref[0, 0, h] += ddf[h]
                    ddm_ref[0, 1, h] += dxf[h]
                    ddm_ref[0, 2, h] += dzf[h]
                    ddm_ref[1, 0, h] += ddb[h]
                    ddm_ref[1, 1, h] += dxb[h]
                    ddm_ref[1, 2, h] += dzb[h]
                    ddm_ref[1, 3, h] += sb[h] * g_b[h]
                    dsgw_ref[h] += dsgw[h]
                    dgain_ref[:, cs] += dgain[h]
                    dbf_ref[h] += dbf[h]
                    run[h] = dsb[h] + dm_ref[1, 3, h] * run[h]

            @pl.when(n == nb - 1)
            def _():
                dscb_ref[...] = run[...]

        @pl.when((p == 1) & (n == nb - 1))
        def _():
            cps_ = _scatter_ici_copies(ps, got, send_sems, recv_sems)
            for cp in cps_:
                cp.wait_recv()
            for cp in cps_:
                cp.wait_send()

    hbm = pl.BlockSpec(memory_space=pl.ANY)
    tab = pl.BlockSpec((rows, HD), lambda p, n: (rev(p, n), 0))
    tile4 = jax.ShapeDtypeStruct((NH, CH, CH), F32)
    out = pl.pallas_call(
        body, name="mixer_bwd", grid=(2, nb),
        in_specs=[col(0, False), col(1, False), col(2, True), col(3, True), col(4, True), col(5, True),
                  col(6, False),
                  pl.BlockSpec((rows, AW), lambda p, n: (p * n, 0)),
                  pl.BlockSpec((rows, AW), lambda p, n: (rev(p, n), 1)),
                  tab, tab, _const((2, 4, NH, CH, CH)), _const((NH, CH, CH)), _const((1, AW)),
                  _const((NH, CH, CH)),
                  pl.BlockSpec((cps, NH, HD, HD), lambda p, n: (rev(p, n), 0, 0, 0)),
                  pl.BlockSpec((cps, NH, HD, HD), lambda p, n: (p * n, 0, 0, 0))] + [hbm] * nt,
        out_specs=[pl.BlockSpec((rows, IN_COLS), lambda p, n: (p * n, 0)),
                   pl.BlockSpec((2, 4, NH, CH, CH), lambda p, n: (0, 0, 0, 0, 0)),
                   pl.BlockSpec((NH, CH, CH), lambda p, n: (0, 0, 0)),
                   pl.BlockSpec((1, AW), lambda p, n: (0, 0)),
                   pl.BlockSpec((NH, CH, CH), lambda p, n: (0, 0, 0)),
                   pl.BlockSpec((NH, HD, HD), lambda p, n: (0, 0, 0)),
                   pl.BlockSpec((NH, HD, HD), lambda p, n: (0, 0, 0))] + [hbm] * nt,
        out_shape=[jax.ShapeDtypeStruct((ln, IN_COLS), BF), jax.ShapeDtypeStruct((2, 4, NH, CH, CH), F32),
                   tile4, jax.ShapeDtypeStruct((1, AW), F32), tile4, tile4, tile4]
        + [jax.ShapeDtypeStruct((NCHIP - 1,) + p.shape[1:], BF) for p in parts],
        scratch_shapes=[pltpu.VMEM((nc, NH, HD, HD), BF), pltpu.VMEM((NH, HD, HD), F32)]
        + [pltpu.SemaphoreType.DMA(((NCHIP - 1) * nt,))] * 2,
        compiler_params=_params(60, ("arbitrary", "arbitrary")),
    )(z, z, z, z, z, z, z, dycat, dycat, cos_t, sin_t, dm, sgw, gain, bfull, sf_all, sb_all, *parts)
    return out[:7], out[7:]


def _small_reduce(ddm, ddm_ctx, dm, dbf):
    def body(ddm_ref, dctx_ref, dm_ref, dbf_ref, lg_ref, sgb_ref):
        ex = _decay_exponents()
        ones = jnp.ones((8, CH), F32)
        for d in range(2):
            for h in range(NH):
                tot = jnp.zeros((CH, CH), F32)
                for kind in range(4):
                    g = ddm_ref[d, kind, h]
                    if kind >= 2:
                        g = g + dctx_ref[d, kind - 2, h]
                    tot = tot + g * dm_ref[d, kind, h] * ex[d][kind]
                lg_ref[d * NH + h: d * NH + h + 1, :] = jnp.sum(tot, axis=0, keepdims=True)
        for g in range(NH):
            r = lax.dot_general(ones, dbf_ref[g], (NT, ((), ())), precision=HI, preferred_element_type=F32)
            sgb_ref[g:g + 1, :] = r[0:1, :]

    return pl.pallas_call(
        body, name="small_reduce",
        out_shape=[jax.ShapeDtypeStruct((8, CH), F32), jax.ShapeDtypeStruct((NH, CH), F32)],
        in_specs=[pl.BlockSpec(memory_space=pltpu.VMEM)] * 4,
        out_specs=[pl.BlockSpec(memory_space=pltpu.VMEM)] * 2,
        compiler_params=_params(32),
    )(ddm, ddm_ctx, dm, dbf)


def _mod_backward(ct_pad_t, cctx_col, dmod_pad, dcmod_cols, w_mod_s):
    def body(ct_ref, cc_ref, dm_ref, dc_ref, w_ref, gw_ref, part_ref):
        dcm = dc_ref[0:1, :]
        for d in range(1, NDEV):
            dcm = dcm + dc_ref[d:d + 1, :]
        gw_ref[...] = (jnp.dot(_silu(ct_ref[...]), dm_ref[...], precision=HI, preferred_element_type=F32)
                       + _silu(cc_ref[...]) * dcm)
        part_ref[...] = lax.dot_general(jnp.broadcast_to(dcm, (8, dcm.shape[1])), w_ref[...], (NT, ((), ())),
                                        precision=HI, preferred_element_type=F32)

    return pl.pallas_call(
        body, name="mod_backward",
        out_shape=[jax.ShapeDtypeStruct(w_mod_s.shape, F32), jax.ShapeDtypeStruct((8, D), F32)],
        in_specs=[pl.BlockSpec(memory_space=pltpu.VMEM)] * 5,
        out_specs=[pl.BlockSpec(memory_space=pltpu.VMEM)] * 2,
        compiler_params=_params(48),
    )(ct_pad_t, cctx_col, dmod_pad, dcmod_cols, w_mod_s)


def _cctx_update(parts, c_ctx, m, v):
    def body(p_ref, c_ref, m_ref, v_ref, g_ref, d_ref, mo_ref, vo_ref):
        tot = ((p_ref[0] + p_ref[2]) + p_ref[4]) + p_ref[6]
        cv = c_ref[...]
        s = jax.nn.sigmoid(cv)
        g = tot * (s * (1.0 + cv * (1.0 - s)))
        g_ref[...] = g
        d_ref[...], mo_ref[...], vo_ref[...] = _adamw_math(cv, g, m_ref[...], v_ref[...])

    return pl.pallas_call(
        body, name="cctx_update",
        out_shape=[jax.ShapeDtypeStruct((1, D), F32)] * 4,
        in_specs=[pl.BlockSpec(memory_space=pltpu.VMEM)] * 4,
        out_specs=[pl.BlockSpec(memory_space=pltpu.VMEM)] * 4,
        compiler_params=_params(16),
    )(parts, c_ctx, m, v)


def _small_update(gathered, wp, mp, vp):
    def body(g_ref, w_ref, m_ref, v_ref, go_ref, d_ref, mo_ref, vo_ref, loss_ref):
        tot = g_ref[0]
        for d in range(1, NDEV):
            tot = tot + g_ref[d]
        go_ref[Q_BMOD:Q_N1, :] = tot[P_DMOD:P_N1, :] + tot[P_DCMOD:P_DMOD, :]
        go_ref[Q_N1:Q_LG, :] = tot[P_N1:P_LG, :]
        lg = jnp.sum(tot[P_LG:P_N2, :], axis=1, keepdims=True)
        go_ref[Q_LG:Q_N2, :] = lg * jax.nn.sigmoid(-w_ref[Q_LG:Q_N2, :])
        go_ref[Q_N2:Q_ROWS, :] = tot[P_N2:P_LOSS, :]
        d_ref[...], mo_ref[...], vo_ref[...] = _adamw_math(w_ref[...], go_ref[...], m_ref[...], v_ref[...])
        ls = jnp.sum(jnp.sum(tot[P_LOSS:P_ROWS, :], axis=1, keepdims=True), axis=0, keepdims=True)
        loss_ref[...] = jnp.broadcast_to(ls, (8, CH))

    return pl.pallas_call(
        body, name="small_update",
        out_shape=[jax.ShapeDtypeStruct((Q_ROWS, CH), F32)] * 4 + [jax.ShapeDtypeStruct((8, CH), F32)],
        in_specs=[pl.BlockSpec(memory_space=pltpu.VMEM)] * 4,
        out_specs=[pl.BlockSpec(memory_space=pltpu.VMEM)] * 5,
        compiler_params=_params(32),
    )(gathered, wp, mp, vp)


def _rows(a):
    return a.reshape(-1, CH)


def _pack_small(b_mod, norm1, sg_gain, sg_w, sg_b, lf, lb, norm2, norm_f):
    lg = jnp.broadcast_to(jnp.concatenate([lf.reshape(NH), lb.reshape(NH)])[:, None], (2 * NH, CH))
    return jnp.concatenate([_rows(b_mod), _rows(norm1), _rows(sg_gain), _rows(sg_w), _rows(sg_b), lg,
                            _rows(norm2), _rows(norm_f)], axis=0)


def _unpack_small(p):
    return (p[Q_BMOD:Q_N1].reshape(1, 6 * D), p[Q_N1:Q_GAIN].reshape(1, D), p[Q_GAIN:Q_SGW].reshape(1, AW),
            p[Q_SGW:Q_SGB].reshape(1, NH, CH, CH), p[Q_SGB:Q_LG].reshape(1, NH, CH),
            p[Q_LG:Q_LG + NH, 0].reshape(1, NH), p[Q_LG + NH:Q_N2, 0].reshape(1, NH),
            p[Q_N2:Q_NF].reshape(1, D), p[Q_NF:Q_ROWS].reshape(D))


def _rope_tables(ln):
    pos = np.arange(ln)
    rows = (pos // GRID_W).astype(np.float32)
    cols = (pos % GRID_W).astype(np.float32)
    n_freq = HD // 4
    inv = (np.float32(ROPE_BASE) ** (-np.arange(n_freq, dtype=np.float32) / np.float32(n_freq))).astype(np.float32)
    ar = rows[:, None] * inv[None, :]
    ac = cols[:, None] * inv[None, :]
    cos_t = np.concatenate([np.cos(ar), np.cos(ar), np.cos(ac), np.cos(ac)], axis=1).astype(np.float32)
    sin_t = np.concatenate([-np.sin(ar), np.sin(ar), -np.sin(ac), np.sin(ac)], axis=1).astype(np.float32)
    return jnp.asarray(cos_t), jnp.asarray(sin_t)


def kernel(x, c, ctx, c_ctx, w_mod, b_mod, norm1, w_in, sg_gain, sg_w, sg_b, ret_logit_f, ret_logit_b, w_out, norm2, w_gate, w_up, w_down, norm_f, loss_target, m_c_ctx, m_w_mod, m_b_mod, m_norm1, m_w_in, m_sg_gain, m_sg_w, m_sg_b, m_ret_logit_f, m_ret_logit_b, m_w_out, m_norm2, m_w_gate, m_w_up, m_w_down, m_norm_f, v_c_ctx, v_w_mod, v_b_mod, v_norm1, v_w_in, v_sg_gain, v_sg_w, v_sg_b, v_ret_logit_f, v_ret_logit_b, v_w_out, v_norm2, v_w_gate, v_w_up, v_w_down, v_norm_f):
    ln = x.shape[1]
    xi, yi, ci = _pos()
    chip = 2 * xi + yi
    me = 4 * xi + 2 * yi + ci
    x2d = x.reshape(ln, D)
    tgt = loss_target.reshape(ln, D)
    mod_c = w_mod.shape[2]

    tr = lambda a: jnp.swapaxes(a[0], 0, 1)
    gbufs, c_all, prod_all = _prologue(c, c_ctx.reshape(1, D), w_mod[0],
                                       [w_in[0], w_out[0], tr(w_gate), tr(w_up), w_down[0]])
    wi = gbufs[0].reshape(NCHIP, D, WI_C)
    gbufs_a, gbufs_b = gbufs[1:3], gbufs[3:5]
    c_all = c_all.reshape(NDEV, D)
    prod_chips = prod_all[0::2]
    mod_rows = jnp.transpose(prod_chips, (1, 0, 2)).reshape(16, NCHIP * mod_c) + b_mod
    mod = lax.dynamic_slice_in_dim(mod_rows, me, 1, axis=0)
    cmod = mod_rows[8:9]
    sh1, sc1, g1, sh2, sc2, g2 = [mod[:, i * D:(i + 1) * D] for i in range(6)]
    csh1, csc1 = cmod[:, 0:D], cmod[:, D:2 * D]
    zrow = jnp.zeros((1, D), F32)
    vec_in = jnp.concatenate([norm1, sh1, sc1] + [zrow] * 5, axis=0)
    vec_ctx = jnp.concatenate([norm1, csh1, csc1] + [zrow] * 5, axis=0)
    vec_post = jnp.concatenate([g1, norm2, sh2, sc2, g2, norm_f.reshape(1, D), zrow, zrow], axis=0)

    logits = jnp.concatenate([ret_logit_f.reshape(NH), ret_logit_b.reshape(NH)])
    dm = _decay_mats(jnp.broadcast_to(logits[:, None, None], (2 * NH, CH, CH)))
    ctx2d = ctx.reshape(ctx.shape[1], D)
    scf, scb = _ctx_forward(ctx2d, vec_ctx, wi, dm)

    cos_t, sin_t = _rope_tables(ln)
    z, hx, gbufs_a = _in_proj(x2d, vec_in, wi, gbufs_a)
    bfull = jnp.broadcast_to(sg_b[0][:, :, None], (NH, CH, CH))
    ycat, sf_all, sb_all, gbufs = _mixer_fwd(z, cos_t, sin_t, dm, sg_w[0], sg_gain, bfull, scf, scb,
                                             gbufs_a, gbufs_b)
    wo, wg_t, wu_t, wd = [g.reshape(NCHIP, 2 * g.shape[2], g.shape[3]) for g in gbufs]
    wo = wo.reshape(D, D)

    dx1, dycat, h2, dy, df, act, da, db, acc_post = _post_mixer(x2d, ycat, tgt, vec_post, wo, wg_t, wu_t, wd)

    cidx = ci.reshape(1).astype(jnp.int32)
    chipidx = chip.reshape(1).astype(jnp.int32)

    def halves_summed(full, names):
        full = [g.reshape(NCHIP, 2, g.shape[1] // 2, g.shape[2]) for g in full]
        from_sib = _rs_exchange_halves(full, "rs_exchange_" + names[0])
        return [_rs_add_halves(g, r, cidx, "rs_add_halves_" + nm) for g, r, nm in zip(full, from_sib, names)]

    def split(g):
        return g.reshape(NCHIP, 2, g.shape[1] // 2, g.shape[2])

    g_wd = split(_tn_matmul(act, df, "grad_w_down", NCHIP, FF_C, D, True, "shared", 1024))
    g_wu, x_wd = _tn_matmul(db, h2, "grad_w_up", NCHIP, FF_C, D, True, "shared", 1024, carry=g_wd)
    g_wu = split(g_wu)
    g_wg, x_wu = _tn_matmul(da, h2, "grad_w_gate", NCHIP, FF_C, D, True, "shared", 1024, carry=g_wu)
    g_wg = split(g_wg)
    g_wo, x_wg = _tn_matmul(ycat, dy, "grad_w_out", 1, D, D, False, "shared", 1024, carry=g_wg)
    g_wo = split(g_wo.reshape(NCHIP, WO_R, D))
    x_wo = _rs_exchange_halves([g_wo], "rs_exchange_w_out")[0]
    names = ["w_in", "w_out", "w_gate", "w_up", "w_down"]
    sums_b = [_rs_add_halves(g, r, cidx, "rs_add_halves_" + nm)
              for g, r, nm in zip([g_wo, g_wg, g_wu, g_wd], [x_wo, x_wg, x_wu, x_wd], names[1:])]

    (dz, ddm, dsgw, dgain, dbf, dscf, dscb), from_chips_b = _mixer_bwd(
        z, dycat, cos_t, sin_t, dm, sg_w[0], sg_gain, bfull, sf_all, sb_all, [s[1] for s in sums_b])
    gwkv, acc_ctx, ddm_ctx = _ctx_backward(ctx2d, vec_ctx, wi, dm, dscf, dscb)
    g_wi = _tn_matmul(hx, dz, "grad_w_in", NCHIP, D, WI_C, False, "cols", 512, ctx_kv=gwkv)
    sums_a = halves_summed([g_wi], names[:1])
    gx, acc_in, from_chips_a = _in_proj_bwd(dz, x2d, dx1, vec_in, wi, [s[1] for s in sums_a])

    sums = sums_a + sums_b
    from_chips = list(from_chips_a) + list(from_chips_b)
    finals = [_rs_add_chips(s[0], r, chipidx, "rs_add_chips_" + nm) for s, r, nm in zip(sums, from_chips, names)]
    others = _rs_share_final(finals)

    lg_part, dsgb = _small_reduce(ddm, ddm_ctx, dm, dbf)
    dmod = jnp.concatenate([acc_in[1:2], acc_in[2:3], acc_post[0:1], acc_post[2:3], acc_post[3:4],
                            acc_post[4:5]], axis=1)
    dcmod = jnp.concatenate([acc_ctx[1:2], acc_ctx[2:3], jnp.zeros((1, 4 * D), F32)], axis=1)
    packed = jnp.concatenate([
        _rows(dcmod), _rows(dmod), _rows(acc_in[0:1] + acc_ctx[0:1]), _rows(dgain), _rows(dsgw), dsgb, lg_part,
        _rows(acc_post[1:2]), _rows(acc_post[5:6]), _rows(acc_post[6:7])], axis=0)
    gathered = _allgather_small(packed, "gather_small")
    dmod_all = gathered[:, P_DMOD:P_N1].reshape(NDEV, 6 * D)
    dcmod_all = gathered[:, P_DCMOD:P_DMOD].reshape(NDEV, 6 * D)
    dmod_cols = lax.dynamic_slice_in_dim(dmod_all, chip * mod_c, mod_c, axis=1)
    dcmod_cols = lax.dynamic_slice_in_dim(dcmod_all, chip * mod_c, mod_c, axis=1)
    dmod_pad = jnp.concatenate([dmod_cols, jnp.zeros((CH - NDEV, mod_c), F32)], axis=0)
    ct_pad_t = jnp.concatenate([jnp.transpose(c_all), jnp.zeros((D, CH - NDEV), F32)], axis=1)
    g_wmod, cctx_part = _mod_backward(ct_pad_t, c_ctx.reshape(D, 1), dmod_pad, dcmod_cols, w_mod[0])
    parts = _allgather_small(cctx_part[0:1], "gather_cctx")
    g_cctx, d_cctx, nm_cctx, nv_cctx = _cctx_update(parts, c_ctx.reshape(1, D), m_c_ctx.reshape(1, D),
                                                    v_c_ctx.reshape(1, D))

    wp = _pack_small(b_mod, norm1, sg_gain, sg_w, sg_b, ret_logit_f, ret_logit_b, norm2, norm_f)
    mp = _pack_small(m_b_mod, m_norm1, m_sg_gain, m_sg_w, m_sg_b, m_ret_logit_f, m_ret_logit_b, m_norm2, m_norm_f)
    vp = _pack_small(v_b_mod, v_norm1, v_sg_gain, v_sg_w, v_sg_b, v_ret_logit_f, v_ret_logit_b, v_norm2, v_norm_f)
    gp, dp, mp2, vp2, loss_t = _small_update(gathered, wp, mp, vp)

    big_w = [w_in[0], w_out[0], tr(w_gate), tr(w_up), w_down[0]]
    big_m = [m_w_in[0], m_w_out[0], tr(m_w_gate), tr(m_w_up), m_w_down[0]]
    big_v = [v_w_in[0], v_w_out[0], tr(v_w_gate), tr(v_w_up), v_w_down[0]]
    upd = [_adamw_halves(w, own, oth, m, v, cidx, "adamw_" + nm) for w, own, oth, m, v, nm in
           zip(big_w, finals, others, big_m, big_v, names)]
    big_g = [g_wmod] + [u[0] for u in upd]
    big = [_adamw(w_mod[0], g_wmod, m_w_mod[0], v_w_mod[0], "adamw_w_mod")] + [u[1:] for u in upd]

    def assemble(small, cctx, bigs):
        b_mod_, norm1_, gain_, sgw_, sgb_, lf_, lb_, norm2_, normf_ = _unpack_small(small)
        wm, wi_, wo_, wg_, wu_, wd_ = [b[None] for b in bigs]
        wg_, wu_ = jnp.swapaxes(wg_, 1, 2), jnp.swapaxes(wu_, 1, 2)
        return [cctx.reshape(D), wm, b_mod_, norm1_, wi_, gain_, sgw_, sgb_, lf_, lb_, wo_, norm2_, wg_, wu_, wd_,
                normf_]

    out = [loss_t[0, 0], gx.reshape(1, ln, D)]
    out += assemble(gp, g_cctx, big_g)
    out += assemble(dp, d_cctx, [b[0] for b in big])
    out += assemble(mp2, nm_cctx, [b[1] for b in big])
    out += assemble(vp2, nv_cctx, [b[2] for b in big])
    return tuple(out)
```

```python
import functools

import jax
import jax.numpy as jnp
import numpy as np
from jax import lax
from jax.experimental import pallas as pl
from jax.experimental.pallas import tpu as pltpu

F32 = jnp.float32
BF = jnp.bfloat16
MESH = pl.DeviceIdType.MESH

D = 1024
CH = 128
HD = 128
NH = 4
AW = 512
IN_COLS = 3584
DFF = 2816
NCHIP = 4
NDEV = 8
WI_C = IN_COLS // NCHIP
FF_C = DFF // NCHIP
WO_R = D // NCHIP
EPS = 1e-6
GRID_W = 64
ROPE_BASE = 10000.0
K_SCALE = HD ** -0.5
LR, B1, B2, AEPS, WD, STEP = 0.001, 0.9, 0.999, 1e-08, 0.01, 10
VMEM_MB = 1 << 20
HI = lax.Precision.HIGHEST

P_DCMOD, P_DMOD, P_N1, P_GAIN, P_SGW, P_SGB, P_LG, P_N2, P_NF, P_LOSS = 0, 48, 96, 104, 112, 624, 632, 640, 648, 656
P_ROWS = 664
Q_BMOD, Q_N1, Q_GAIN, Q_SGW, Q_SGB, Q_LG, Q_N2, Q_NF = 0, 48, 56, 64, 576, 584, 592, 600
Q_ROWS = 608


def _params(vmem_mb, sem=None):
    return pltpu.CompilerParams(vmem_limit_bytes=vmem_mb * VMEM_MB, dimension_semantics=sem)


def _const(shape):
    nd = len(shape)
    return pl.BlockSpec(shape, lambda *_: (0,) * nd, pipeline_mode=pl.Buffered(1))


def _pos():
    return lax.axis_index("x"), lax.axis_index("y"), lax.axis_index("c")


def _dot(a, b, dims):
    return lax.dot_general(a, b, (dims, ((), ())), preferred_element_type=F32)


NN = ((1,), (0,))
NT = ((1,), (1,))
TN = ((0,), (0,))


@jax.custom_vjp
def _mm(a, b):
    return _dot(a.astype(BF), b.astype(BF), NN)


def _mm_f(a, b):
    return _mm(a, b), (a.astype(BF), b.astype(BF))


def _mm_b(res, g):
    a, b = res
    gb = g.astype(BF)
    return _dot(gb, b, NT), _dot(a, gb, TN)


_mm.defvjp(_mm_f, _mm_b)


@jax.custom_vjp
def _mm_nt(a, b):
    return _dot(a.astype(BF), b.astype(BF), NT)


def _mm_nt_f(a, b):
    return _mm_nt(a, b), (a.astype(BF), b.astype(BF))


def _mm_nt_b(res, g):
    a, b = res
    gb = g.astype(BF)
    return _dot(gb, b, NN), _dot(gb, a, TN)


_mm_nt.defvjp(_mm_nt_f, _mm_nt_b)


@jax.custom_vjp
def _mm_tn(a, b):
    return _dot(a.astype(BF), b.astype(BF), TN)


def _mm_tn_f(a, b):
    return _mm_tn(a, b), (a.astype(BF), b.astype(BF))


def _mm_tn_b(res, g):
    a, b = res
    gb = g.astype(BF)
    return _dot(b, gb, NT), _dot(a, gb, NN)


_mm_tn.defvjp(_mm_tn_f, _mm_tn_b)


def _gelu(x):
    return x * (0.5 * (1.0 + jnp.tanh(0.7978845608028654 * (x + 0.044715 * (x * x * x)))))


def _silu(x):
    return x * jax.nn.sigmoid(x)


def _rms(x):
    return lax.rsqrt(jnp.mean(x * x, axis=-1, keepdims=True) + EPS)


def _swap32(t):
    lane = lax.broadcasted_iota(jnp.int32, t.shape, 1)
    first = (lane % 64) < 32
    return jnp.where(first, pltpu.roll(t, 96, 1), pltpu.roll(t, 32, 1))


def _rope(t, cos, sin):
    return t * cos + _swap32(t) * sin


def _rope_bwd(d, cos, sin):
    return d * cos + _swap32(d * sin)


def _heads(ref, r0=0):
    return [ref[r0:r0 + CH, h * HD:(h + 1) * HD].astype(F32) for h in range(NH)]


def _chunk_fwd(u, v, q, k, vr, gf, gb, sf, sb, df, xf, zf, db, xb, zb, sgw, gain, bfull):
    ya, yr, uf, ub = [], [], [], []
    for g in range(NH):
        gu = _gelu(u[g])
        gv = _gelu(v[g])
        vn = gv * _rms(gv) * gain[g]
        ya.append(gu * (_mm(sgw[g], vn) + bfull[g]))
    for h in range(NH):
        a = _mm_nt(q[h], k[h])
        of = _mm(a * df[h], vr[h]) + xf[h] * _mm(q[h], sf[h])
        ob = _mm(a * db[h], vr[h]) + xb[h] * _mm(q[h], sb[h])
        yr.append(_silu(gf[h]) * (of * _rms(of)) + _silu(gb[h]) * (ob * _rms(ob)))
        uf.append(_mm_tn(k[h], zf[h] * vr[h]))
        ub.append(_mm_tn(k[h], zb[h] * vr[h]))
    return ya, yr, uf, ub


def _fwd_dir_only(sf, q, k, vr, gf, df, xf):
    out = []
    for h in range(NH):
        a = _mm_nt(q[h], k[h])
        of = _mm(a * df[h], vr[h]) + xf[h] * _mm(q[h], sf[h])
        out.append(_silu(gf[h]) * (of * _rms(of)))
    return out


def _ctx_states(ctx0, ctx1, n1, csh, csc, wk, wv, zf, zb, ef, eb):
    hc0 = (ctx0 * _rms(ctx0) * n1) * (1.0 + csc) + csh
    hc1 = (ctx1 * _rms(ctx1) * n1) * (1.0 + csc) + csh
    scf, scb = [], []
    for h in range(NH):
        k0, k1 = _mm(hc0, wk[h]) * K_SCALE, _mm(hc1, wk[h]) * K_SCALE
        v0, v1 = _mm(hc0, wv[h]), _mm(hc1, wv[h])
        scf.append(ef[h] * _mm_tn(k0, zf[h] * v0) + _mm_tn(k1, zf[h] * v1))
        scb.append(eb[h] * _mm_tn(k1, zb[h] * v1) + _mm_tn(k0, zb[h] * v0))
    return scf, scb


def _allgather_small(v, name):
    r, n = v.shape

    def body(v_ref, out_ref, send_sems, recv_sems, local_sem):
        x, y, c = _pos()
        me = 4 * x + 2 * y + c
        mine = pltpu.make_async_copy(v_ref, out_ref.at[me], local_sem)
        mine.start()
        sent = []
        for k in range(1, NDEV):
            kx, ky, kc = (k >> 2) & 1, (k >> 1) & 1, k & 1
            peer = (x ^ kx, y ^ ky, c ^ kc)
            cp = pltpu.make_async_remote_copy(src_ref=v_ref, dst_ref=out_ref.at[me], send_sem=send_sems.at[k - 1],
                                              recv_sem=recv_sems.at[k - 1], device_id=peer, device_id_type=MESH)
            cp.start()
            sent.append(cp)
        for k in range(1, NDEV):
            kx, ky, kc = (k >> 2) & 1, (k >> 1) & 1, k & 1
            peer = (x ^ kx, y ^ ky, c ^ kc)
            src = 4 * (x ^ kx) + 2 * (y ^ ky) + (c ^ kc)
            pltpu.make_async_remote_copy(src_ref=v_ref, dst_ref=out_ref.at[src], send_sem=send_sems.at[k - 1],
                                         recv_sem=recv_sems.at[k - 1], device_id=peer, device_id_type=MESH).wait_recv()
        for cp in sent:
            cp.wait_send()
        mine.wait()

    return pl.pallas_call(
        body, name=name,
        out_shape=jax.ShapeDtypeStruct((NDEV, r, n), F32),
        in_specs=[pl.BlockSpec(memory_space=pltpu.VMEM)],
        out_specs=pl.BlockSpec(memory_space=pltpu.VMEM),
        scratch_shapes=[pltpu.SemaphoreType.DMA((NDEV - 1,)), pltpu.SemaphoreType.DMA((NDEV - 1,)),
                        pltpu.SemaphoreType.DMA],
        compiler_params=_params(16),
    )(v)


def _gather_weights(shards):
    nt = len(shards)
    shapes = [s.shape for s in shards]

    def body(*refs):
        srcs, outs, stages = refs[:nt], refs[nt:2 * nt], refs[2 * nt:3 * nt]
        ici_send, ici_recv, d2d_send, d2d_recv, local_sems = refs[3 * nt:]
        x, y, c = _pos()
        chip = 2 * x + y
        for t in range(nt):
            half = shapes[t][0] // 2
            stages[t][0] = srcs[t][0:half, :].astype(BF)
            stages[t][1] = srcs[t][half:2 * half, :].astype(BF)
        local = []
        for t in range(nt):
            cp = pltpu.make_async_copy(stages[t], outs[t].at[chip], local_sems.at[t])
            cp.start()
            local.append(cp)
        sent = []
        for k in range(1, NCHIP):
            kx, ky = (k >> 1) & 1, k & 1
            for t in range(nt):
                s = (k - 1) * nt + t
                cp = pltpu.make_async_remote_copy(
                    src_ref=stages[t].at[c], dst_ref=outs[t].at[chip, c], send_sem=ici_send.at[s],
                    recv_sem=ici_recv.at[s], device_id=(x ^ kx, y ^ ky, c), device_id_type=MESH)
                cp.start()
                sent.append(cp)
        for k in range(1, NCHIP):
            kx, ky = (k >> 1) & 1, k & 1
            src_chip = 2 * (x ^ kx) + (y ^ ky)
            for t in range(nt):
                s = (k - 1) * nt + t
                pltpu.make_async_remote_copy(
                    src_ref=stages[t].at[c], dst_ref=outs[t].at[src_chip, c], send_sem=ici_send.at[s],
                    recv_sem=ici_recv.at[s], device_id=(x ^ kx, y ^ ky, c), device_id_type=MESH).wait_recv()
                cp = pltpu.make_async_remote_copy(
                    src_ref=outs[t].at[src_chip, c], dst_ref=outs[t].at[src_chip, c], send_sem=d2d_send.at[s],
                    recv_sem=d2d_recv.at[s], device_id=(x, y, 1 - c), device_id_type=MESH)
                cp.start()
                sent.append(cp)
        for k in range(1, NCHIP):
            kx, ky = (k >> 1) & 1, k & 1
            src_chip = 2 * (x ^ kx) + (y ^ ky)
            for t in range(nt):
                s = (k - 1) * nt + t
                pltpu.make_async_remote_copy(
                    src_ref=stages[t].at[c], dst_ref=outs[t].at[src_chip, 1 - c], send_sem=d2d_send.at[s],
                    recv_sem=d2d_recv.at[s], device_id=(x, y, 1 - c), device_id_type=MESH).wait_recv()
        for cp in sent:
            cp.wait_send()
        for cp in local:
            cp.wait()

    n_rem = (NCHIP - 1) * nt
    out = pl.pallas_call(
        body, name="gather_weights",
        out_shape=[jax.ShapeDtypeStruct((NCHIP, 2, r // 2, cc), BF) for r, cc in shapes],
        in_specs=[pl.BlockSpec(memory_space=pltpu.VMEM)] * nt,
        out_specs=[pl.BlockSpec(memory_space=pl.ANY)] * nt,
        scratch_shapes=[pltpu.VMEM((2, r // 2, cc), BF) for r, cc in shapes]
        + [pltpu.SemaphoreType.DMA((n_rem,))] * 4 + [pltpu.SemaphoreType.DMA((nt,))],
        compiler_params=_params(48),
    )(*shards)
    return [o.reshape(NCHIP, r, cc) for o, (r, cc) in zip(out, shapes)]


def _chip_offsets():
    return [((k >> 1) & 1, k & 1) for k in range(1, NCHIP)]


def _prologue(c, c_ctx, w_mod_s, shards):
    nt = len(shards)
    shapes = [s.shape for s in shards]
    mod_c = w_mod_s.shape[1]

    def body(*refs):
        c_ref, cc_ref, wm_ref = refs[:3]
        srcs = refs[3:3 + nt]
        outs = refs[3 + nt:3 + 2 * nt]
        call_ref, prod_ref = refs[3 + 2 * nt:5 + 2 * nt]
        stages = refs[5 + 2 * nt:5 + 3 * nt]
        ct = refs[5 + 3 * nt]
        c_send, c_recv, p_send, p_recv, ici_send, ici_recv, d2d_send, d2d_recv, local_sems = refs[6 + 3 * nt:]
        x, y, c = _pos()
        chip = 2 * x + y
        me = 4 * x + 2 * y + c
        sib = (x, y, 1 - c)
        pending = []
        for t in range(nt):
            half = shapes[t][0] // 2
            stages[t][0] = srcs[t][0:half, :].astype(BF)
            stages[t][1] = srcs[t][half:2 * half, :].astype(BF)
            cp = pltpu.make_async_copy(stages[t], outs[t].at[chip], local_sems.at[t])
            cp.start()
            pending.append(cp)
        sends = []
        for k, (kx, ky) in enumerate(_chip_offsets()):
            cp = pltpu.make_async_remote_copy(src_ref=stages[0].at[c], dst_ref=outs[0].at[chip, c],
                                              send_sem=ici_send.at[k], recv_sem=ici_recv.at[k],
                                              device_id=(x ^ kx, y ^ ky, c), device_id_type=MESH)
            cp.start()
            sends.append(cp)

        def to_all(src, dst_of, send_sems, recv_sems):
            for k in range(1, NDEV):
                kx, ky, kc = (k >> 2) & 1, (k >> 1) & 1, k & 1
                cp = pltpu.make_async_remote_copy(src_ref=src, dst_ref=dst_of(me), send_sem=send_sems.at[k - 1],
                                                  recv_sem=recv_sems.at[k - 1], device_id=(x ^ kx, y ^ ky, c ^ kc),
                                                  device_id_type=MESH)
                cp.start()
                sends.append(cp)
            for k in range(1, NDEV):
                kx, ky, kc = (k >> 2) & 1, (k >> 1) & 1, k & 1
                frm = 4 * (x ^ kx) + 2 * (y ^ ky) + (c ^ kc)
                pltpu.make_async_remote_copy(src_ref=src, dst_ref=dst_of(frm), send_sem=send_sems.at[k - 1],
                                             recv_sem=recv_sems.at[k - 1], device_id=(x ^ kx, y ^ ky, c ^ kc),
                                             device_id_type=MESH).wait_recv()

        call_ref[me] = c_ref[...]
        to_all(c_ref, lambda d: call_ref.at[d], c_send, c_recv)
        ct[...] = jnp.zeros_like(ct)
        for d in range(NDEV):
            ct[d:d + 1, :] = call_ref[d]
        ct[NDEV:NDEV + 1, :] = cc_ref[...]
        prod_ref[me] = jnp.dot(_silu(ct[...]), wm_ref[...], precision=HI, preferred_element_type=F32)
        to_all(prod_ref.at[me], lambda d: prod_ref.at[d], p_send, p_recv)

        for k, (kx, ky) in enumerate(_chip_offsets()):
            frm = 2 * (x ^ kx) + (y ^ ky)
            pltpu.make_async_remote_copy(src_ref=stages[0].at[c], dst_ref=outs[0].at[frm, c],
                                         send_sem=ici_send.at[k], recv_sem=ici_recv.at[k],
                                         device_id=(x ^ kx, y ^ ky, c), device_id_type=MESH).wait_recv()
            cp = pltpu.make_async_remote_copy(src_ref=outs[0].at[frm, c], dst_ref=outs[0].at[frm, c],
                                              send_sem=d2d_send.at[k], recv_sem=d2d_recv.at[k],
                                              device_id=sib, device_id_type=MESH)
            cp.start()
            sends.append(cp)
        for k, (kx, ky) in enumerate(_chip_offsets()):
            frm = 2 * (x ^ kx) + (y ^ ky)
            pltpu.make_async_remote_copy(src_ref=stages[0].at[c], dst_ref=outs[0].at[frm, 1 - c],
                                         send_sem=d2d_send.at[k], recv_sem=d2d_recv.at[k],
                                         device_id=sib, device_id_type=MESH).wait_recv()
        for cp in sends:
            cp.wait_send()
        for cp in pending:
            cp.wait()

    vm = pl.BlockSpec(memory_space=pltpu.VMEM)
    out = pl.pallas_call(
        body, name="prologue",
        out_shape=[jax.ShapeDtypeStruct((NCHIP, 2, r // 2, cc), BF) for r, cc in shapes]
        + [jax.ShapeDtypeStruct((NDEV, 1, D), F32), jax.ShapeDtypeStruct((NDEV, 16, mod_c), F32)],
        in_specs=[vm] * (3 + nt),
        out_specs=[pl.BlockSpec(memory_space=pl.ANY)] * nt + [vm, vm],
        scratch_shapes=[pltpu.VMEM((2, r // 2, cc), BF) for r, cc in shapes] + [pltpu.VMEM((16, D), F32)]
        + [pltpu.SemaphoreType.DMA((NDEV - 1,))] * 4 + [pltpu.SemaphoreType.DMA((NCHIP - 1,))] * 4
        + [pltpu.SemaphoreType.DMA((nt,))],
        compiler_params=_params(56),
    )(c, c_ctx, w_mod_s, *shards)
    return out[:nt], out[nt], out[nt + 1]


def _gather_ici_copies(bufs, send_sems, recv_sems):
    x, y, c = _pos()
    chip = 2 * x + y
    nt = len(bufs)
    out_cp, in_cp = [], []
    for k, (kx, ky) in enumerate(_chip_offsets()):
        frm = 2 * (x ^ kx) + (y ^ ky)
        for t in range(nt):
            s = k * nt + t
            peer = (x ^ kx, y ^ ky, c)
            out_cp.append(pltpu.make_async_remote_copy(
                src_ref=bufs[t].at[chip, c], dst_ref=bufs[t].at[chip, c], send_sem=send_sems.at[s],
                recv_sem=recv_sems.at[s], device_id=peer, device_id_type=MESH))
            in_cp.append(pltpu.make_async_remote_copy(
                src_ref=bufs[t].at[chip, c], dst_ref=bufs[t].at[frm, c], send_sem=send_sems.at[s],
                recv_sem=recv_sems.at[s], device_id=peer, device_id_type=MESH))
    return out_cp, in_cp


def _gather_d2d_copies(bufs, send_sems, recv_sems):
    x, y, c = _pos()
    nt = len(bufs)
    out_cp, in_cp = [], []
    for k, (kx, ky) in enumerate(_chip_offsets()):
        frm = 2 * (x ^ kx) + (y ^ ky)
        for t in range(nt):
            s = k * nt + t
            out_cp.append(pltpu.make_async_remote_copy(
                src_ref=bufs[t].at[frm, c], dst_ref=bufs[t].at[frm, c], send_sem=send_sems.at[s],
                recv_sem=recv_sems.at[s], device_id=(x, y, 1 - c), device_id_type=MESH))
            in_cp.append(pltpu.make_async_remote_copy(
                src_ref=bufs[t].at[frm, c], dst_ref=bufs[t].at[frm, 1 - c], send_sem=send_sems.at[s],
                recv_sem=recv_sems.at[s], device_id=(x, y, 1 - c), device_id_type=MESH))
    return out_cp, in_cp


def _scatter_ici_copies(parts, outs, send_sems, recv_sems):
    x, y, c = _pos()
    nt = len(parts)
    cps = []
    for k, (kx, ky) in enumerate(_chip_offsets()):
        dst_chip = 2 * (x ^ kx) + (y ^ ky)
        for t in range(nt):
            s = k * nt + t
            cps.append(pltpu.make_async_remote_copy(
                src_ref=parts[t].at[dst_chip], dst_ref=outs[t].at[k], send_sem=send_sems.at[s],
                recv_sem=recv_sems.at[s], device_id=(x ^ kx, y ^ ky, c), device_id_type=MESH))
    return cps


def _rs_exchange_halves(grads, name):
    nt = len(grads)
    shapes = [g.shape for g in grads]

    def body(*refs):
        gs, outs = refs[:nt], refs[nt:2 * nt]
        send_sems, recv_sems = refs[2 * nt:]
        x, y, c = _pos()
        sib = (x, y, 1 - c)
        sent = []
        for t in range(nt):
            for j in range(NCHIP):
                s = t * NCHIP + j
                cp = pltpu.make_async_remote_copy(src_ref=gs[t].at[j, 1 - c], dst_ref=outs[t].at[j],
                                                  send_sem=send_sems.at[s], recv_sem=recv_sems.at[s],
                                                  device_id=sib, device_id_type=MESH)
                cp.start()
                sent.append(cp)
        for cp in sent:
            cp.wait_recv()
        for cp in sent:
            cp.wait_send()

    return pl.pallas_call(
        body, name=name,
        out_shape=[jax.ShapeDtypeStruct((NCHIP, s[2], s[3]), F32) for s in shapes],
        in_specs=[pl.BlockSpec(memory_space=pl.ANY)] * nt,
        out_specs=[pl.BlockSpec(memory_space=pl.ANY)] * nt,
        scratch_shapes=[pltpu.SemaphoreType.DMA((nt * NCHIP,))] * 2,
    )(*grads)


def _rs_send_chips(parts):
    nt = len(parts)
    shapes = [p.shape for p in parts]

    def body(*refs):
        ps, outs = refs[:nt], refs[nt:2 * nt]
        send_sems, recv_sems = refs[2 * nt:]
        x, y, c = _pos()
        sent = []
        for k in range(1, NCHIP):
            kx, ky = (k >> 1) & 1, k & 1
            dst_chip = 2 * (x ^ kx) + (y ^ ky)
            for t in range(nt):
                s = (k - 1) * nt + t
                cp = pltpu.make_async_remote_copy(src_ref=ps[t].at[dst_chip], dst_ref=outs[t].at[k - 1],
                                                  send_sem=send_sems.at[s], recv_sem=recv_sems.at[s],
                                                  device_id=(x ^ kx, y ^ ky, c), device_id_type=MESH)
                cp.start()
                sent.append(cp)
        for cp in sent:
            cp.wait_recv()
        for cp in sent:
            cp.wait_send()

    return pl.pallas_call(
        body, name="rs_send_chips",
        out_shape=[jax.ShapeDtypeStruct((NCHIP - 1, s[1], s[2]), BF) for s in shapes],
        in_specs=[pl.BlockSpec(memory_space=pl.ANY)] * nt,
        out_specs=[pl.BlockSpec(memory_space=pl.ANY)] * nt,
        scratch_shapes=[pltpu.SemaphoreType.DMA((nt * (NCHIP - 1),))] * 2,
    )(*parts)


def _rs_share_final(finals):
    nt = len(finals)
    shapes = [f.shape for f in finals]

    def body(*refs):
        fs, outs = refs[:nt], refs[nt:2 * nt]
        send_sems, recv_sems = refs[2 * nt:]
        x, y, c = _pos()
        sent = []
        for t in range(nt):
            cp = pltpu.make_async_remote_copy(src_ref=fs[t], dst_ref=outs[t], send_sem=send_sems.at[t],
                                              recv_sem=recv_sems.at[t], device_id=(x, y, 1 - c), device_id_type=MESH)
            cp.start()
            sent.append(cp)
        for cp in sent:
            cp.wait_recv()
        for cp in sent:
            cp.wait_send()

    return pl.pallas_call(
        body, name="rs_share_final",
        out_shape=[jax.ShapeDtypeStruct(s, F32) for s in shapes],
        in_specs=[pl.BlockSpec(memory_space=pl.ANY)] * nt,
        out_specs=[pl.BlockSpec(memory_space=pl.ANY)] * nt,
        scratch_shapes=[pltpu.SemaphoreType.DMA((nt,))] * 2,
    )(*finals)


def _row_tile(h, cc=D):
    for t in (512, 384, 352, 256, 176, 128, 64, 32, 16):
        if h % t == 0 and t * cc * 4 <= (5 * VMEM_MB) // 4:
            return t
    return h


def _rs_add_halves(g, recv, cidx, name):
    _, _, h, cc = g.shape
    th = _row_tile(h, cc)

    def body(c_ref, g_ref, r_ref, of_ref, ob_ref):
        s = g_ref[...] + r_ref[...]
        of_ref[...] = s
        ob_ref[...] = s.astype(BF)

    return pl.pallas_call(
        body, name=name,
        grid_spec=pltpu.PrefetchScalarGridSpec(
            num_scalar_prefetch=1, grid=(NCHIP, h // th),
            in_specs=[pl.BlockSpec((None, None, th, cc), lambda j, i, c_ref: (j, c_ref[0], i, 0)),
                      pl.BlockSpec((None, th, cc), lambda j, i, c_ref: (j, i, 0))],
            out_specs=[pl.BlockSpec((None, th, cc), lambda j, i, c_ref: (j, i, 0)),
                       pl.BlockSpec((None, th, cc), lambda j, i, c_ref: (j, i, 0))]),
        out_shape=[jax.ShapeDtypeStruct((NCHIP, h, cc), F32), jax.ShapeDtypeStruct((NCHIP, h, cc), BF)],
        compiler_params=_params(48),
    )(cidx, g, recv)


def _rs_add_chips(own, recv, chipidx, name):
    _, h, cc = own.shape
    th = _row_tile(h, cc)

    def body(j_ref, o_ref, r_ref, out_ref):
        out_ref[...] = ((o_ref[...] + r_ref[0].astype(F32)) + r_ref[1].astype(F32)) + r_ref[2].astype(F32)

    return pl.pallas_call(
        body, name=name,
        grid_spec=pltpu.PrefetchScalarGridSpec(
            num_scalar_prefetch=1, grid=(h // th,),
            in_specs=[pl.BlockSpec((None, th, cc), lambda i, j_ref: (j_ref[0], i, 0)),
                      pl.BlockSpec((NCHIP - 1, th, cc), lambda i, j_ref: (0, i, 0))],
            out_specs=pl.BlockSpec((th, cc), lambda i, j_ref: (i, 0))),
        out_shape=jax.ShapeDtypeStruct((h, cc), F32),
        compiler_params=_params(48),
    )(chipidx, own, recv)


def _adamw_math(w, g, m, v):
    m2 = B1 * m + (1.0 - B1) * g
    v2 = B2 * v + (1.0 - B2) * (g * g)
    m_hat = m2 / (1.0 - B1 ** STEP)
    v_hat = v2 / (1.0 - B2 ** STEP)
    delta = -LR * (m_hat / (jnp.sqrt(v_hat) + AEPS) + WD * w)
    return delta, m2, v2


def _adamw(w, g, m, v, name):
    r, cc = w.shape
    tr = _row_tile(r, cc)

    def body(w_ref, g_ref, m_ref, v_ref, d_ref, mo_ref, vo_ref):
        d, m2, v2 = _adamw_math(w_ref[...], g_ref[...], m_ref[...], v_ref[...])
        d_ref[...] = d
        mo_ref[...] = m2
        vo_ref[...] = v2

    spec = pl.BlockSpec((tr, cc), lambda i: (i, 0))
    return pl.pallas_call(
        body, name=name, grid=(r // tr,), in_specs=[spec] * 4, out_specs=[spec] * 3,
        out_shape=[jax.ShapeDtypeStruct((r, cc), F32)] * 3,
        compiler_params=_params(48, ("parallel",)),
    )(w, g, m, v)


def _adamw_halves(w, own, other, m, v, cidx, name):
    r, cc = w.shape
    h = r // 2
    tr = _row_tile(h, cc)
    per = h // tr

    def body(c_ref, w_ref, own_ref, oth_ref, m_ref, v_ref, g_ref, d_ref, mo_ref, vo_ref):
        mine = (pl.program_id(0) // per) == c_ref[0]
        g = jnp.where(mine, own_ref[...], oth_ref[...])
        g_ref[...] = g
        d, m2, v2 = _adamw_math(w_ref[...], g, m_ref[...], v_ref[...])
        d_ref[...] = d
        mo_ref[...] = m2
        vo_ref[...] = v2

    full = pl.BlockSpec((tr, cc), lambda i, c_ref: (i, 0))
    half = pl.BlockSpec((tr, cc), lambda i, c_ref: (i % per, 0))
    return pl.pallas_call(
        body, name=name,
        grid_spec=pltpu.PrefetchScalarGridSpec(
            num_scalar_prefetch=1, grid=(r // tr,),
            in_specs=[full, half, half, full, full], out_specs=[full] * 4),
        out_shape=[jax.ShapeDtypeStruct((r, cc), F32)] * 4,
        compiler_params=_params(48, ("parallel",)),
    )(cidx, w, own, other, m, v)


def _mod_forward(ct_pad, w_mod_s):
    def body(c_ref, w_ref, o_ref):
        o_ref[...] = jnp.dot(_silu(c_ref[...]), w_ref[...], precision=HI, preferred_element_type=F32)

    return pl.pallas_call(
        body, name="mod_forward",
        out_shape=jax.ShapeDtypeStruct((16, w_mod_s.shape[1]), F32),
        in_specs=[pl.BlockSpec(memory_space=pltpu.VMEM)] * 2,
        out_specs=pl.BlockSpec(memory_space=pltpu.VMEM),
        compiler_params=_params(32),
    )(ct_pad, w_mod_s)


def _decay_exponents():
    ri = lax.broadcasted_iota(jnp.int32, (CH, CH), 0).astype(F32)
    ci = lax.broadcasted_iota(jnp.int32, (CH, CH), 1).astype(F32)
    full = jnp.full((CH, CH), float(CH), F32)
    return [[ri - ci, ri + 1.0, (CH - 1.0) - ri, full], [ci - ri, CH - ri, ri, full]]


def _decay_mats(logit_full):
    def body(l_ref, o_ref):
        ex = _decay_exponents()
        for d in range(2):
            for h in range(NH):
                lv = l_ref[d * NH + h]
                lg = jnp.minimum(lv, 0.0) - jnp.log(1.0 + jnp.exp(-jnp.abs(lv)))
                for kind in range(4):
                    m = jnp.exp(lg * ex[d][kind])
                    if kind == 0:
                        m = jnp.where(ex[d][0] >= 0.0, jnp.exp(lg * jnp.maximum(ex[d][0], 0.0)), 0.0)
                    o_ref[d, kind, h] = m

    return pl.pallas_call(
        body, name="decay_mats",
        out_shape=jax.ShapeDtypeStruct((2, 4, NH, CH, CH), F32),
        in_specs=[pl.BlockSpec(memory_space=pltpu.VMEM)],
        out_specs=pl.BlockSpec(memory_space=pltpu.VMEM),
        compiler_params=_params(32),
    )(logit_full)


def _ctx_kv_weights(wi_ref):
    def cols(g):
        return wi_ref[g // WI_C, :, g % WI_C: g % WI_C + HD].astype(F32)

    wk = [cols(3 * AW + h * HD) for h in range(NH)]
    wv = [cols(4 * AW + h * HD) for h in range(NH)]
    return wk, wv


def _ctx_forward(ctx, vecs, wi, dm):
    def body(ctx_ref, v_ref, wi_ref, dm_ref, scf_ref, scb_ref):
        wk, wv = _ctx_kv_weights(wi_ref)
        mats = [[dm_ref[d, kind, h] for h in range(NH)] for d in range(2) for kind in (2, 3)]
        scf, scb = _ctx_states(ctx_ref[0:CH, :], ctx_ref[CH:2 * CH, :], v_ref[0:1, :], v_ref[1:2, :],
                               v_ref[2:3, :], wk, wv, mats[0], mats[2], mats[1], mats[3])
        for h in range(NH):
            scf_ref[h] = scf[h]
            scb_ref[h] = scb[h]

    return pl.pallas_call(
        body, name="ctx_forward",
        out_shape=[jax.ShapeDtypeStruct((NH, HD, HD), F32)] * 2,
        in_specs=[pl.BlockSpec(memory_space=pltpu.VMEM)] * 4,
        out_specs=[pl.BlockSpec(memory_space=pltpu.VMEM)] * 2,
        compiler_params=_params(48),
    )(ctx, vecs, wi, dm)


def _ctx_backward(ctx, vecs, wi, dm, dscf, dscb):
    def body(ctx_ref, v_ref, wi_ref, dm_ref, gf_ref, gb_ref, gw_ref, gv_ref, gdm_ref):
        wk, wv = _ctx_kv_weights(wi_ref)
        mats = [[dm_ref[d, kind, h] for h in range(NH)] for d in range(2) for kind in (2, 3)]
        ctx0, ctx1 = ctx_ref[0:CH, :], ctx_ref[CH:2 * CH, :]

        def fn(n1, csh, csc, wk_, wv_, zf, zb, ef, eb):
            return _ctx_states(ctx0, ctx1, n1, csh, csc, wk_, wv_, zf, zb, ef, eb)

        _, vjp = jax.vjp(fn, v_ref[0:1, :], v_ref[1:2, :], v_ref[2:3, :], wk, wv,
                         mats[0], mats[2], mats[1], mats[3])
        cot = ([gf_ref[h] for h in range(NH)], [gb_ref[h] for h in range(NH)])
        dn1, dcsh, dcsc, dwk, dwv, dzf, dzb, def_, deb = vjp(cot)
        for h in range(NH):
            gw_ref[:, h * HD:(h + 1) * HD] = dwk[h]
            gw_ref[:, AW + h * HD:AW + (h + 1) * HD] = dwv[h]
        gv_ref[...] = jnp.zeros_like(gv_ref)
        gv_ref[0:1, :] = dn1
        gv_ref[1:2, :] = dcsh
        gv_ref[2:3, :] = dcsc
        for h in range(NH):
            gdm_ref[0, 0, h] = dzf[h]
            gdm_ref[0, 1, h] = def_[h]
            gdm_ref[1, 0, h] = dzb[h]
            gdm_ref[1, 1, h] = deb[h]

    return pl.pallas_call(
        body, name="ctx_backward",
        out_shape=[jax.ShapeDtypeStruct((D, 2 * AW), F32), jax.ShapeDtypeStruct((8, D), F32),
                   jax.ShapeDtypeStruct((2, 2, NH, CH, CH), F32)],
        in_specs=[pl.BlockSpec(memory_space=pltpu.VMEM)] * 6,
        out_specs=[pl.BlockSpec(memory_space=pltpu.VMEM)] * 3,
        compiler_params=_params(56),
    )(ctx, vecs, wi, dm, dscf, dscb)


def _in_proj(x, vecs, wi, gbufs):
    ln = x.shape[0]
    t = min(512, ln)
    nt = len(gbufs)
    steps = ln // t

    def body(x_ref, v_ref, wi_ref, *refs):
        z_ref, hx_ref = refs[nt:nt + 2]
        bufs = refs[nt + 2:2 * nt + 2]
        send_sems, recv_sems = refs[2 * nt + 2:]
        i = pl.program_id(0)

        @pl.when(i == 0)
        def _():
            for cp in _gather_ici_copies(bufs, send_sems, recv_sems)[0]:
                cp.start()

        xv = x_ref[...]
        hx = (xv * _rms(xv) * v_ref[0:1, :]) * (1.0 + v_ref[2:3, :]) + v_ref[1:2, :]
        hb = hx.astype(BF)
        hx_ref[...] = hb
        for j in range(NCHIP):
            z_ref[:, j * WI_C:(j + 1) * WI_C] = _dot(hb, wi_ref[j], NN)

        @pl.when(i == steps - 1)
        def _():
            out_cp, in_cp = _gather_ici_copies(bufs, send_sems, recv_sems)
            for cp in in_cp:
                cp.wait_recv()
            for cp in out_cp:
                cp.wait_send()

    hbm = pl.BlockSpec(memory_space=pl.ANY)
    out = pl.pallas_call(
        body, name="in_proj", grid=(steps,),
        in_specs=[pl.BlockSpec((t, D), lambda i: (i, 0)), _const((8, D)), _const((NCHIP, D, WI_C))] + [hbm] * nt,
        out_specs=[pl.BlockSpec((t, IN_COLS), lambda i: (i, 0)), pl.BlockSpec((t, D), lambda i: (i, 0))] + [hbm] * nt,
        out_shape=[jax.ShapeDtypeStruct((ln, IN_COLS), F32), jax.ShapeDtypeStruct((ln, D), BF)]
        + [jax.ShapeDtypeStruct(g.shape, g.dtype) for g in gbufs],
        input_output_aliases={3 + k: 2 + k for k in range(nt)},
        scratch_shapes=[pltpu.SemaphoreType.DMA(((NCHIP - 1) * nt,))] * 2,
        compiler_params=_params(56, ("arbitrary",)),
    )(x, vecs, wi, *gbufs)
    return out[0], out[1], out[2:]


def _allgather_copies(src, out, send_sems, recv_sems, local_sem):
    x, y, c = _pos()
    me = 4 * x + 2 * y + c
    sends, recvs = [], []
    for k in range(1, NDEV):
        kx, ky, kc = (k >> 2) & 1, (k >> 1) & 1, k & 1
        peer = (x ^ kx, y ^ ky, c ^ kc)
        frm = 4 * (x ^ kx) + 2 * (y ^ ky) + (c ^ kc)
        sends.append(pltpu.make_async_remote_copy(src_ref=src, dst_ref=out.at[me], send_sem=send_sems.at[k - 1],
                                                  recv_sem=recv_sems.at[k - 1], device_id=peer, device_id_type=MESH))
        recvs.append(pltpu.make_async_remote_copy(src_ref=src, dst_ref=out.at[frm], send_sem=send_sems.at[k - 1],
                                                  recv_sem=recv_sems.at[k - 1], device_id=peer, device_id_type=MESH))
    return sends, recvs, pltpu.make_async_copy(src, out.at[me], local_sem)


def _in_proj_bwd(dz, x, dx1, vecs, wi, parts, early):
    ln = x.shape[0]
    t = min(512, ln)
    nt = len(parts)
    steps = ln // t

    def body(dz_ref, x_ref, dx1_ref, v_ref, wi_ref, *refs):
        ps = refs[:nt]
        early_ref = refs[nt]
        gx_ref, acc_ref = refs[nt + 1:nt + 3]
        got = refs[nt + 3:2 * nt + 3]
        early_all = refs[2 * nt + 3]
        send_sems, recv_sems, ag_send, ag_recv, ag_local = refs[2 * nt + 4:]

        @pl.when(pl.program_id(0) == 0)
        def _():
            acc_ref[...] = jnp.zeros_like(acc_ref)
            for cp in _scatter_ici_copies(ps, got, send_sems, recv_sems):
                cp.start()
            sends, _, own = _allgather_copies(early_ref, early_all, ag_send, ag_recv, ag_local)
            own.start()
            for cp in sends:
                cp.start()

        dhx = jnp.zeros((t, D), F32)
        for j in range(NCHIP):
            dhx = dhx + _dot(dz_ref[:, j * WI_C:(j + 1) * WI_C], wi_ref[j], NT)
        xv = x_ref[...]
        r = _rms(xv)
        xn = xv * r
        n1, sc = v_ref[0:1, :], v_ref[2:3, :]
        acc_ref[0:1, :] += jnp.sum(dhx * xn * (1.0 + sc), axis=0, keepdims=True)
        acc_ref[1:2, :] += jnp.sum(dhx, axis=0, keepdims=True)
        acc_ref[2:3, :] += jnp.sum(dhx * xn * n1, axis=0, keepdims=True)
        g = dhx * n1 * (1.0 + sc)
        gx_ref[...] = dx1_ref[...] + r * (g - xn * jnp.mean(g * xn, axis=-1, keepdims=True))

        @pl.when(pl.program_id(0) == steps - 1)
        def _():
            cps = _scatter_ici_copies(ps, got, send_sems, recv_sems)
            sends, recvs, own = _allgather_copies(early_ref, early_all, ag_send, ag_recv, ag_local)
            for cp in cps + recvs:
                cp.wait_recv()
            for cp in cps + sends:
                cp.wait_send()
            own.wait()

    hbm = pl.BlockSpec(memory_space=pl.ANY)
    out = pl.pallas_call(
        body, name="in_proj_bwd", grid=(steps,),
        in_specs=[pl.BlockSpec((t, IN_COLS), lambda i: (i, 0)), pl.BlockSpec((t, D), lambda i: (i, 0)),
                  pl.BlockSpec((t, D), lambda i: (i, 0)), _const((8, D)), _const((NCHIP, D, WI_C))]
        + [hbm] * (nt + 1),
        out_specs=[pl.BlockSpec((t, D), lambda i: (i, 0)), pl.BlockSpec((8, D), lambda i: (0, 0))]
        + [hbm] * (nt + 1),
        out_shape=[jax.ShapeDtypeStruct((ln, D), F32), jax.ShapeDtypeStruct((8, D), F32)]
        + [jax.ShapeDtypeStruct((NCHIP - 1,) + p.shape[1:], BF) for p in parts]
        + [jax.ShapeDtypeStruct((NDEV,) + early.shape, F32)],
        scratch_shapes=[pltpu.SemaphoreType.DMA(((NCHIP - 1) * nt,))] * 2
        + [pltpu.SemaphoreType.DMA((NDEV - 1,))] * 2 + [pltpu.SemaphoreType.DMA],
        compiler_params=_params(56, ("arbitrary",)),
    )(dz, x, dx1, vecs, wi, *parts, early)
    return out[0], out[1], out[2:2 + nt], out[2 + nt]


def _post_mixer(x, ycat, tgt, vecs, wo, wg, wu, wd):
    ln = x.shape[0]
    t = min(256, ln)

    def body(x_ref, y_ref, t_ref, v_ref, wo_ref, wg_ref, wu_ref, wd_ref,
             dx1_ref, dyc_ref, h2_ref, dy_ref, df_ref, act_ref, da_ref, db_ref, acc_ref, a_st, b_st):
        @pl.when(pl.program_id(0) == 0)
        def _():
            acc_ref[...] = jnp.zeros_like(acc_ref)

        g1, n2, sh2, sc2 = v_ref[0:1, :], v_ref[1:2, :], v_ref[2:3, :], v_ref[3:4, :]
        g2, nf = v_ref[4:5, :], v_ref[5:6, :]
        y = _dot(y_ref[...], wo_ref[...], NN)
        x1 = x_ref[...] + g1 * y
        r2 = _rms(x1)
        xn2 = x1 * r2
        t2 = xn2 * n2
        h2b = (t2 * (1.0 + sc2) + sh2).astype(BF)
        h2_ref[...] = h2b
        f = jnp.zeros((t, D), F32)
        for j in range(NCHIP):
            a = _dot(h2b, wg_ref[j], NT)
            b = _dot(h2b, wu_ref[j], NT)
            a_st[j] = a
            b_st[j] = b
            act = (_silu(a) * b).astype(BF)
            act_ref[j] = act
            f = f + _dot(act, wd_ref[j], NN)
        x2 = x1 + g2 * f
        r3 = _rms(x2)
        xn3 = x2 * r3
        e = xn3 * nf - t_ref[...]
        acc_ref[6:7, :] += jnp.sum(e * e, axis=0, keepdims=True) * (0.5 / D)
        dout = e * (1.0 / D)
        acc_ref[5:6, :] += jnp.sum(dout * xn3, axis=0, keepdims=True)
        gg = dout * nf
        dx2 = r3 * (gg - xn3 * jnp.mean(gg * xn3, axis=-1, keepdims=True))
        acc_ref[4:5, :] += jnp.sum(dx2 * f, axis=0, keepdims=True)
        dfb = (g2 * dx2).astype(BF)
        df_ref[...] = dfb
        dh2 = jnp.zeros((t, D), F32)
        for j in range(NCHIP):
            dact = _dot(dfb, wd_ref[j], NT)
            a = a_st[j]
            b = b_st[j]
            s = jax.nn.sigmoid(a)
            da = (dact * b * (s * (1.0 + a * (1.0 - s)))).astype(BF)
            db = (dact * (a * s)).astype(BF)
            da_ref[j] = da
            db_ref[j] = db
            dh2 = dh2 + _dot(da, wg_ref[j], NN) + _dot(db, wu_ref[j], NN)
        acc_ref[2:3, :] += jnp.sum(dh2, axis=0, keepdims=True)
        acc_ref[3:4, :] += jnp.sum(dh2 * t2, axis=0, keepdims=True)
        acc_ref[1:2, :] += jnp.sum(dh2 * xn2 * (1.0 + sc2), axis=0, keepdims=True)
        gx = dh2 * n2 * (1.0 + sc2)
        dx1 = dx2 + r2 * (gx - xn2 * jnp.mean(gx * xn2, axis=-1, keepdims=True))
        dx1_ref[...] = dx1
        acc_ref[0:1, :] += jnp.sum(dx1 * y, axis=0, keepdims=True)
        dyb = (g1 * dx1).astype(BF)
        dy_ref[...] = dyb
        dyc_ref[...] = _dot(dyb, wo_ref[...], NT)

    tok = pl.BlockSpec((t, D), lambda i: (i, 0))
    ffb = pl.BlockSpec((NCHIP, t, FF_C), lambda i: (0, i, 0))
    return pl.pallas_call(
        body, name="post_mixer", grid=(ln // t,),
        in_specs=[tok, tok, tok, _const((8, D)), _const((D, D)), _const((NCHIP, FF_C, D)),
                  _const((NCHIP, FF_C, D)), _const((NCHIP, FF_C, D))],
        out_specs=[tok, tok, tok, tok, tok, ffb, ffb, ffb, pl.BlockSpec((16, D), lambda i: (0, 0))],
        out_shape=[jax.ShapeDtypeStruct((ln, D), F32)] * 2 + [jax.ShapeDtypeStruct((ln, D), BF)] * 3
        + [jax.ShapeDtypeStruct((NCHIP, ln, FF_C), BF)] * 3 + [jax.ShapeDtypeStruct((16, D), F32)],
        scratch_shapes=[pltpu.VMEM((NCHIP, t, FF_C), F32)] * 2,
        compiler_params=_params(60, ("arbitrary",)),
    )(x, ycat, tgt, vecs, wo, wg, wu, wd)


def _exchange_copies(g, out, send_sems, recv_sems):
    x, y, c = _pos()
    return [pltpu.make_async_remote_copy(src_ref=g.at[j, 1 - c], dst_ref=out.at[j], send_sem=send_sems.at[j],
                                         recv_sem=recv_sems.at[j], device_id=(x, y, 1 - c), device_id_type=MESH)
            for j in range(NCHIP)]


def _tn_matmul(xa, dy, name, nb, k1, n, x_batched, dy_mode, tt, ctx_kv=None, carry=None):
    ln = xa.shape[-2]
    tt = min(tt, ln)
    steps = ln // tt
    n_in = 2 + (ctx_kv is not None) + (carry is not None)

    def body(x_ref, dy_ref, *refs):
        o_ref = refs[n_in - 2]
        if carry is not None:
            g_ref, got_ref = refs[n_in - 3], refs[n_in - 1]
            send_sems, recv_sems = refs[n_in:]

        @pl.when(pl.program_id(0) == 0)
        def _():
            if carry is not None:
                for cp in _exchange_copies(g_ref, got_ref, send_sems, recv_sems):
                    cp.start()
            o_ref[...] = jnp.zeros_like(o_ref)
            if ctx_kv is not None:
                for g in range(0, 2 * AW, HD):
                    col = 3 * AW + g
                    o_ref[col // n, :, col % n: col % n + HD] = refs[0][:, g:g + HD]

        xt = None if x_batched else jnp.transpose(x_ref[...])
        for b in range(nb):
            lhs = jnp.transpose(x_ref[b]) if x_batched else xt
            if dy_mode == "batched":
                rhs = dy_ref[b]
            elif dy_mode == "cols":
                rhs = dy_ref[:, b * n:(b + 1) * n]
            else:
                rhs = dy_ref[...]
            o_ref[b] += _dot(lhs, rhs, NN)

        if carry is not None:
            @pl.when(pl.program_id(0) == steps - 1)
            def _():
                cps = _exchange_copies(g_ref, got_ref, send_sems, recv_sems)
                for cp in cps:
                    cp.wait_recv()
                for cp in cps:
                    cp.wait_send()

    x_spec = (pl.BlockSpec((nb, tt, k1), lambda t: (0, t, 0)) if x_batched
              else pl.BlockSpec((tt, k1), lambda t: (t, 0)))
    if dy_mode == "batched":
        dy_spec = pl.BlockSpec((nb, tt, n), lambda t: (0, t, 0))
    elif dy_mode == "cols":
        dy_spec = pl.BlockSpec((tt, nb * n), lambda t: (t, 0))
    else:
        dy_spec = pl.BlockSpec((tt, n), lambda t: (t, 0))
    hbm = pl.BlockSpec(memory_space=pl.ANY)
    extra = [] if ctx_kv is None else [ctx_kv]
    in_specs = [x_spec, dy_spec] + [_const(e.shape) for e in extra]
    out_specs = [pl.BlockSpec((nb, k1, n), lambda t: (0, 0, 0))]
    out_shape = [jax.ShapeDtypeStruct((nb, k1, n), F32)]
    scratch = []
    if carry is not None:
        extra = extra + [carry]
        in_specs.append(hbm)
        out_specs.append(hbm)
        out_shape.append(jax.ShapeDtypeStruct((NCHIP,) + carry.shape[2:], F32))
        scratch = [pltpu.SemaphoreType.DMA((NCHIP,))] * 2
    out = pl.pallas_call(
        body, name=name, grid=(steps,),
        in_specs=in_specs, out_specs=out_specs, out_shape=out_shape, scratch_shapes=scratch,
        compiler_params=_params(60, ("arbitrary",)),
    )(xa, dy, *extra)
    return out[0] if carry is None else (out[0], out[1])


def _add_ctx_cols(gwi, gwkv):
    first = 1536 // HD
    per = WI_C // HD

    def body(g_ref, a_ref, o_ref):
        o_ref[...] = g_ref[...] + a_ref[...]

    spec = pl.BlockSpec((None, D, HD), lambda i: ((first + i) // per, 0, (first + i) % per))
    return pl.pallas_call(
        body, name="add_ctx_cols", grid=(2 * AW // HD,),
        in_specs=[spec, pl.BlockSpec((D, HD), lambda i: (0, i))],
        out_specs=spec,
        out_shape=jax.ShapeDtypeStruct(gwi.shape, F32),
        input_output_aliases={0: 0},
        compiler_params=_params(32, ("arbitrary",)),
    )(gwi, gwkv)


FWD_CHUNKS_PER_STEP = 4
BWD_CHUNKS_PER_STEP = 4


def _chunks_per_step(nc, want):
    return want if nc % want == 0 else 1


def _mixer_fwd(z, cos_t, sin_t, dm, sgw, gain, bfull, scf, scb, gbufs_a, gbufs_b):
    ln = z.shape[0]
    nc = ln // CH
    na = len(gbufs_a)
    gbufs = list(gbufs_a) + list(gbufs_b)
    nt = len(gbufs)
    cps = _chunks_per_step(nc, FWD_CHUNKS_PER_STEP)
    nb = nc // cps
    rows = cps * CH
    mid = nb // 2

    def rev(p, n):
        return p * n + (1 - p) * (nb - 1 - n)

    def col(j, both):
        if both:
            return pl.BlockSpec((rows, AW), lambda p, n: (rev(p, n), j))
        return pl.BlockSpec((rows, AW), lambda p, n: (p * n, j))

    def body(u_ref, v_ref, q_ref, k_ref, vr_ref, gf_ref, gb_ref, cos_ref, sin_ref, dm_ref, sgw_ref, gain_ref,
             bfull_ref, scf_ref, scb_ref, *refs):
        y_ref, sf_ref, sb_ref = refs[nt:nt + 3]
        bufs = refs[nt + 3:2 * nt + 3]
        bufs_a, bufs_b = bufs[:na], bufs[na:]
        sb_all, st, a_send, a_recv, bi_send, bi_recv, bd_send, bd_recv = refs[2 * nt + 3:]
        p, n = pl.program_id(0), pl.program_id(1)

        @pl.when((p == 0) & (n == 0))
        def _():
            for cp in _gather_d2d_copies(bufs_a, a_send, a_recv)[0]:
                cp.start()
            for cp in _gather_ici_copies(bufs_b, bi_send, bi_recv)[0]:
                cp.start()

        @pl.when((p == 1) & (n == mid))
        def _():
            for cp in _gather_ici_copies(bufs_b, bi_send, bi_recv)[1]:
                cp.wait_recv()
            for cp in _gather_d2d_copies(bufs_b, bd_send, bd_recv)[0]:
                cp.start()

        def roped_k(r0):
            cos, sin = cos_ref[r0:r0 + CH, :], sin_ref[r0:r0 + CH, :]
            return [_rope(t, cos, sin) * K_SCALE for t in _heads(k_ref, r0)]

        @pl.when(p == 0)
        def _():
            @pl.when(n == 0)
            def _():
                st[...] = scb_ref[...]

            for s in reversed(range(cps)):
                m = (nb - 1 - n) * cps + s
                k, vr = roped_k(s * CH), _heads(vr_ref, s * CH)
                for h in range(NH):
                    sb_all[m, h] = st[h]
                    st[h] = dm_ref[1, 3, h] * st[h] + _mm_tn(k[h], dm_ref[1, 2, h] * vr[h])

        @pl.when(p == 1)
        def _():
            @pl.when(n == 0)
            def _():
                st[...] = scf_ref[...]

            mats = [[dm_ref[d, kind, h] for h in range(NH)] for d in range(2) for kind in range(3)]
            for s in range(cps):
                r0 = s * CH
                m = n * cps + s
                cos, sin = cos_ref[r0:r0 + CH, :], sin_ref[r0:r0 + CH, :]
                q = [_rope(t, cos, sin) for t in _heads(q_ref, r0)]
                sf = [st[h] for h in range(NH)]
                sb = [sb_all[m, h] for h in range(NH)]
                ya, yr, uf, _ = _chunk_fwd(
                    _heads(u_ref, r0), _heads(v_ref, r0), q, roped_k(r0), _heads(vr_ref, r0), _heads(gf_ref, r0),
                    _heads(gb_ref, r0), sf, sb, mats[0], mats[1], mats[2], mats[3], mats[4], mats[5],
                    [sgw_ref[g] for g in range(NH)], [gain_ref[:, g * HD:(g + 1) * HD] for g in range(NH)],
                    [bfull_ref[g] for g in range(NH)])
                for h in range(NH):
                    y_ref[r0:r0 + CH, h * HD:(h + 1) * HD] = ya[h].astype(BF)
                    y_ref[r0:r0 + CH, AW + h * HD:AW + (h + 1) * HD] = yr[h].astype(BF)
                    sf_ref[s, h] = sf[h]
                    sb_ref[s, h] = sb[h]
                    st[h] = dm_ref[0, 3, h] * st[h] + uf[h]

        @pl.when((p == 1) & (n == nb - 1))
        def _():
            a_out, a_in = _gather_d2d_copies(bufs_a, a_send, a_recv)
            b_out, b_in = _gather_d2d_copies(bufs_b, bd_send, bd_recv)
            for cp in a_in + b_in:
                cp.wait_recv()
            for cp in a_out + b_out + _gather_ici_copies(bufs_b, bi_send, bi_recv)[0]:
                cp.wait_send()

    hbm = pl.BlockSpec(memory_space=pl.ANY)
    tab = pl.BlockSpec((rows, HD), lambda p, n: (rev(p, n), 0))
    st_spec = pl.BlockSpec((cps, NH, HD, HD), lambda p, n: (p * n, 0, 0, 0))
    out = pl.pallas_call(
        body, name="mixer_fwd", grid=(2, nb),
        in_specs=[col(0, False), col(1, False), col(2, False), col(3, True), col(4, True), col(5, False),
                  col(6, False), tab, tab, _const((2, 4, NH, CH, CH)), _const((NH, CH, CH)), _const((1, AW)),
                  _const((NH, CH, CH)), _const((NH, HD, HD)), _const((NH, HD, HD))] + [hbm] * nt,
        out_specs=[pl.BlockSpec((rows, D), lambda p, n: (p * n, 0)), st_spec, st_spec] + [hbm] * nt,
        out_shape=[jax.ShapeDtypeStruct((ln, D), BF), jax.ShapeDtypeStruct((nc, NH, HD, HD), F32),
                   jax.ShapeDtypeStruct((nc, NH, HD, HD), F32)]
        + [jax.ShapeDtypeStruct(g.shape, g.dtype) for g in gbufs],
        input_output_aliases={15 + k: 3 + k for k in range(nt)},
        scratch_shapes=[pltpu.VMEM((nc, NH, HD, HD), F32), pltpu.VMEM((NH, HD, HD), F32)]
        + [pltpu.SemaphoreType.DMA(((NCHIP - 1) * na,))] * 2
        + [pltpu.SemaphoreType.DMA(((NCHIP - 1) * (nt - na),))] * 4,
        compiler_params=_params(56, ("arbitrary", "arbitrary")),
    )(z, z, z, z, z, z, z, cos_t, sin_t, dm, sgw, gain, bfull, scf, scb, *gbufs)
    return out[0], out[1], out[2], out[3:]


def _mixer_bwd(z, dycat, cos_t, sin_t, dm, sgw, gain, bfull, sf_all, sb_all, parts):
    ln = z.shape[0]
    nc = ln // CH
    cps = _chunks_per_step(nc, BWD_CHUNKS_PER_STEP)
    nb = nc // cps
    rows = cps * CH

    def rev(p, n):
        return p * n + (1 - p) * (nb - 1 - n)

    def col(j, both):
        if both:
            return pl.BlockSpec((rows, AW), lambda p, n: (rev(p, n), j))
        return pl.BlockSpec((rows, AW), lambda p, n: (p * n, j))

    nt = len(parts)

    def body(u_ref, v_ref, q_ref, k_ref, vr_ref, gf_ref, gb_ref, dya_ref, dyr_ref, cos_ref, sin_ref, dm_ref,
             sgw_ref, gain_ref, bfull_ref, sf_ref, sb_ref, *refs):
        ps = refs[:nt]
        dz_ref, ddm_ref, dsgw_ref, dgain_ref, dbf_ref, dscf_ref, dscb_ref = refs[nt:nt + 7]
        got = refs[nt + 7:2 * nt + 7]
        gf_all, run, send_sems, recv_sems = refs[2 * nt + 7:]
        p, n = pl.program_id(0), pl.program_id(1)

        @pl.when((p == 0) & (n == 0))
        def _():
            for cp in _scatter_ici_copies(ps, got, send_sems, recv_sems):
                cp.start()

        mats = [[dm_ref[d, kind, h] for h in range(NH)] for d in range(2) for kind in range(3)]

        def chunk_inputs(s):
            r0 = s * CH
            cos, sin = cos_ref[r0:r0 + CH, :], sin_ref[r0:r0 + CH, :]
            q = [_rope(t, cos, sin) for t in _heads(q_ref, r0)]
            k = [_rope(t, cos, sin) * K_SCALE for t in _heads(k_ref, r0)]
            sf = [sf_ref[s, h] for h in range(NH)]
            return cos, sin, q, k, _heads(vr_ref, r0), _heads(gf_ref, r0), _heads(dyr_ref, r0), sf

        @pl.when(p == 0)
        def _():
            @pl.when(n == 0)
            def _():
                run[...] = jnp.zeros_like(run)
                ddm_ref[...] = jnp.zeros_like(ddm_ref)
                dsgw_ref[...] = jnp.zeros_like(dsgw_ref)
                dgain_ref[...] = jnp.zeros_like(dgain_ref)
                dbf_ref[...] = jnp.zeros_like(dbf_ref)

            for s in reversed(range(cps)):
                m = (nb - 1 - n) * cps + s
                _, _, q, k, vr, gf, dyr, sf = chunk_inputs(s)
                _, vjp = jax.vjp(lambda st: _fwd_dir_only(st, q, k, vr, gf, mats[0], mats[1]), sf)
                (dsf,) = vjp(dyr)
                for h in range(NH):
                    g_next = run[h]
                    gf_all[m, h] = g_next.astype(BF)
                    ddm_ref[0, 3, h] += sf[h] * g_next
                    run[h] = dsf[h] + dm_ref[0, 3, h] * g_next

            @pl.when(n == nb - 1)
            def _():
                dscf_ref[...] = run[...]

        @pl.when(p == 1)
        def _():
            @pl.when(n == 0)
            def _():
                run[...] = jnp.zeros_like(run)

            for s in range(cps):
                r0 = s * CH
                m = n * cps + s
                cos, sin, q, k, vr, gf, dyr, sf = chunk_inputs(s)
                sb = [sb_ref[s, h] for h in range(NH)]
                g_f = [gf_all[m, h].astype(F32) for h in range(NH)]
                g_b = [run[h] for h in range(NH)]
                args = (_heads(u_ref, r0), _heads(v_ref, r0), q, k, vr, gf, _heads(gb_ref, r0), sb,
                        mats[0], mats[1], mats[2], mats[3], mats[4], mats[5],
                        [sgw_ref[g] for g in range(NH)], [gain_ref[:, g * HD:(g + 1) * HD] for g in range(NH)],
                        [bfull_ref[g] for g in range(NH)])

                def fn(u_, v_, q_, k_, vr_, gf_, gb_, sb_, df, xf, zf, db, xb, zb, sgw_, gain_, bfull_, sf=sf):
                    return _chunk_fwd(u_, v_, q_, k_, vr_, gf_, gb_, sf, sb_, df, xf, zf, db, xb, zb, sgw_,
                                      gain_, bfull_)

                _, vjp = jax.vjp(fn, *args)
                (du, dv, dq, dk, dvr, dgf, dgb, dsb, ddf, dxf, dzf, ddb, dxb, dzb, dsgw, dgain, dbf) = vjp(
                    (_heads(dya_ref, r0), dyr, g_f, g_b))
                rw = slice(r0, r0 + CH)
                for h in range(NH):
                    cs = slice(h * HD, (h + 1) * HD)
                    dz_ref[rw, h * HD:(h + 1) * HD] = du[h].astype(BF)
                    dz_ref[rw, AW + h * HD:AW + (h + 1) * HD] = dv[h].astype(BF)
                    dz_ref[rw, 2 * AW + h * HD:2 * AW + (h + 1) * HD] = _rope_bwd(dq[h], cos, sin).astype(BF)
                    dz_ref[rw, 3 * AW + h * HD:3 * AW + (h + 1) * HD] = _rope_bwd(dk[h] * K_SCALE, cos,
                                                                                  sin).astype(BF)
                    dz_ref[rw, 4 * AW + h * HD:4 * AW + (h + 1) * HD] = dvr[h].astype(BF)
                    dz_ref[rw, 5 * AW + h * HD:5 * AW + (h + 1) * HD] = dgf[h].astype(BF)
                    dz_ref[rw, 6 * AW + h * HD:6 * AW + (h + 1) * HD] = dgb[h].astype(BF)
                    ddm_ref[0, 0, h] += ddf[h]
                    ddm_ref[0, 1, h] += dxf[h]
                    ddm_ref[0, 2, h] += dzf[h]
                    ddm_ref[1, 0, h] += ddb[h]
                    ddm_ref[1, 1, h] += dxb[h]
                    ddm_ref[1, 2, h] += dzb[h]
                    ddm_ref[1, 3, h] += sb[h] * g_b[h]
                    dsgw_ref[h] += dsgw[h]
                    dgain_ref[:, cs] += dgain[h]
                    dbf_ref[h] += dbf[h]
                    run[h] = dsb[h] + dm_ref[1, 3, h] * run[h]

            @pl.when(n == nb - 1)
            def _():
                dscb_ref[...] = run[...]

        @pl.when((p == 1) & (n == nb - 1))
        def _():
            cps_ = _scatter_ici_copies(ps, got, send_sems, recv_sems)
            for cp in cps_:
                cp.wait_recv()
            for cp in cps_:
                cp.wait_send()

    hbm = pl.BlockSpec(memory_space=pl.ANY)
    tab = pl.BlockSpec((rows, HD), lambda p, n: (rev(p, n), 0))
    tile4 = jax.ShapeDtypeStruct((NH, CH, CH), F32)
    out = pl.pallas_call(
        body, name="mixer_bwd", grid=(2, nb),
        in_specs=[col(0, False), col(1, False), col(2, True), col(3, True), col(4, True), col(5, True),
                  col(6, False),
                  pl.BlockSpec((rows, AW), lambda p, n: (p * n, 0)),
                  pl.BlockSpec((rows, AW), lambda p, n: (rev(p, n), 1)),
                  tab, tab, _const((2, 4, NH, CH, CH)), _const((NH, CH, CH)), _const((1, AW)),
                  _const((NH, CH, CH)),
                  pl.BlockSpec((cps, NH, HD, HD), lambda p, n: (rev(p, n), 0, 0, 0)),
                  pl.BlockSpec((cps, NH, HD, HD), lambda p, n: (p * n, 0, 0, 0))] + [hbm] * nt,
        out_specs=[pl.BlockSpec((rows, IN_COLS), lambda p, n: (p * n, 0)),
                   pl.BlockSpec((2, 4, NH, CH, CH), lambda p, n: (0, 0, 0, 0, 0)),
                   pl.BlockSpec((NH, CH, CH), lambda p, n: (0, 0, 0)),
                   pl.BlockSpec((1, AW), lambda p, n: (0, 0)),
                   pl.BlockSpec((NH, CH, CH), lambda p, n: (0, 0, 0)),
                   pl.BlockSpec((NH, HD, HD), lambda p, n: (0, 0, 0)),
                   pl.BlockSpec((NH, HD, HD), lambda p, n: (0, 0, 0))] + [hbm] * nt,
        out_shape=[jax.ShapeDtypeStruct((ln, IN_COLS), BF), jax.ShapeDtypeStruct((2, 4, NH, CH, CH), F32),
                   tile4, jax.ShapeDtypeStruct((1, AW), F32), tile4, tile4, tile4]
        + [jax.ShapeDtypeStruct((NCHIP - 1,) + p.shape[1:], BF) for p in parts],
        scratch_shapes=[pltpu.VMEM((nc, NH, HD, HD), BF), pltpu.VMEM((NH, HD, HD), F32)]
        + [pltpu.SemaphoreType.DMA(((NCHIP - 1) * nt,))] * 2,
        compiler_params=_params(60, ("arbitrary", "arbitrary")),
    )(z, z, z, z, z, z, z, dycat, dycat, cos_t, sin_t, dm, sgw, gain, bfull, sf_all, sb_all, *parts)
    return out[:7], out[7:]


def _small_reduce(ddm, ddm_ctx, dm, dbf):
    def body(ddm_ref, dctx_ref, dm_ref, dbf_ref, lg_ref, sgb_ref):
        ex = _decay_exponents()
        ones = jnp.ones((8, CH), F32)
        for d in range(2):
            for h in range(NH):
                tot = jnp.zeros((CH, CH), F32)
                for kind in range(4):
                    g = ddm_ref[d, kind, h]
                    if kind >= 2:
                        g = g + dctx_ref[d, kind - 2, h]
                    tot = tot + g * dm_ref[d, kind, h] * ex[d][kind]
                lg_ref[d * NH + h: d * NH + h + 1, :] = jnp.sum(tot, axis=0, keepdims=True)
        sgb_ref[...] = jnp.zeros_like(sgb_ref)
        for g in range(NH):
            r = lax.dot_general(ones, dbf_ref[g], (NT, ((), ())), precision=HI, preferred_element_type=F32)
            sgb_ref[g:g + 1, :] = r[0:1, :]

    return pl.pallas_call(
        body, name="small_reduce",
        out_shape=[jax.ShapeDtypeStruct((8, CH), F32), jax.ShapeDtypeStruct((8, CH), F32)],
        in_specs=[pl.BlockSpec(memory_space=pltpu.VMEM)] * 4,
        out_specs=[pl.BlockSpec(memory_space=pltpu.VMEM)] * 2,
        compiler_params=_params(32),
    )(ddm, ddm_ctx, dm, dbf)


def _mod_backward(ct_pad_t, cctx_col, dmod_pad, dcmod_cols, w_mod_s):
    def body(ct_ref, cc_ref, dm_ref, dc_ref, w_ref, gw_ref, part_ref):
        dcm = dc_ref[0:1, :]
        for d in range(1, NDEV):
            dcm = dcm + dc_ref[d:d + 1, :]
        gw_ref[...] = (jnp.dot(_silu(ct_ref[...]), dm_ref[...], precision=HI, preferred_element_type=F32)
                       + _silu(cc_ref[...]) * dcm)
        part_ref[...] = lax.dot_general(jnp.broadcast_to(dcm, (8, dcm.shape[1])), w_ref[...], (NT, ((), ())),
                                        precision=HI, preferred_element_type=F32)

    return pl.pallas_call(
        body, name="mod_backward",
        out_shape=[jax.ShapeDtypeStruct(w_mod_s.shape, F32), jax.ShapeDtypeStruct((8, D), F32)],
        in_specs=[pl.BlockSpec(memory_space=pltpu.VMEM)] * 5,
        out_specs=[pl.BlockSpec(memory_space=pltpu.VMEM)] * 2,
        compiler_params=_params(48),
    )(ct_pad_t, cctx_col, dmod_pad, dcmod_cols, w_mod_s)


def _cctx_update(parts, c_ctx, m, v):
    def body(p_ref, c_ref, m_ref, v_ref, g_ref, d_ref, mo_ref, vo_ref):
        tot = ((p_ref[0] + p_ref[2]) + p_ref[4]) + p_ref[6]
        cv = c_ref[...]
        s = jax.nn.sigmoid(cv)
        g = tot * (s * (1.0 + cv * (1.0 - s)))
        g_ref[...] = g
        d_ref[...], mo_ref[...], vo_ref[...] = _adamw_math(cv, g, m_ref[...], v_ref[...])

    return pl.pallas_call(
        body, name="cctx_update",
        out_shape=[jax.ShapeDtypeStruct((1, D), F32)] * 4,
        in_specs=[pl.BlockSpec(memory_space=pltpu.VMEM)] * 4,
        out_specs=[pl.BlockSpec(memory_space=pltpu.VMEM)] * 4,
        compiler_params=_params(16),
    )(parts, c_ctx, m, v)


def _small_update(gathered, wp, mp, vp):
    def body(g_ref, w_ref, m_ref, v_ref, go_ref, d_ref, mo_ref, vo_ref, loss_ref):
        tot = g_ref[0]
        for d in range(1, NDEV):
            tot = tot + g_ref[d]
        go_ref[Q_BMOD:Q_N1, :] = tot[P_DMOD:P_N1, :] + tot[P_DCMOD:P_DMOD, :]
        go_ref[Q_N1:Q_LG, :] = tot[P_N1:P_LG, :]
        lg = jnp.sum(tot[P_LG:P_N2, :], axis=1, keepdims=True)
        go_ref[Q_LG:Q_N2, :] = lg * jax.nn.sigmoid(-w_ref[Q_LG:Q_N2, :])
        go_ref[Q_N2:Q_ROWS, :] = tot[P_N2:P_LOSS, :]
        d_ref[...], mo_ref[...], vo_ref[...] = _adamw_math(w_ref[...], go_ref[...], m_ref[...], v_ref[...])
        ls = jnp.sum(jnp.sum(tot[P_LOSS:P_ROWS, :], axis=1, keepdims=True), axis=0, keepdims=True)
        loss_ref[...] = jnp.broadcast_to(ls, (8, CH))

    return pl.pallas_call(
        body, name="small_update",
        out_shape=[jax.ShapeDtypeStruct((Q_ROWS, CH), F32)] * 4 + [jax.ShapeDtypeStruct((8, CH), F32)],
        in_specs=[pl.BlockSpec(memory_space=pltpu.VMEM)] * 4,
        out_specs=[pl.BlockSpec(memory_space=pltpu.VMEM)] * 5,
        compiler_params=_params(32),
    )(gathered, wp, mp, vp)


def _rows(a):
    r = a.reshape(-1, CH)
    return jnp.pad(r, ((0, -r.shape[0] % 8), (0, 0)))


def _pack_small(b_mod, norm1, sg_gain, sg_w, sg_b, lf, lb, norm2, norm_f):
    lg = jnp.broadcast_to(jnp.concatenate([lf.reshape(NH), lb.reshape(NH)])[:, None], (2 * NH, CH))
    return jnp.concatenate([_rows(b_mod), _rows(norm1), _rows(sg_gain), _rows(sg_w), _rows(sg_b), lg,
                            _rows(norm2), _rows(norm_f)], axis=0)


def _unpack_small(p):
    return (p[Q_BMOD:Q_N1].reshape(1, 6 * D), p[Q_N1:Q_GAIN].reshape(1, D), p[Q_GAIN:Q_GAIN + NH].reshape(1, AW),
            p[Q_SGW:Q_SGB].reshape(1, NH, CH, CH), p[Q_SGB:Q_SGB + NH].reshape(1, NH, CH),
            p[Q_LG:Q_LG + NH, 0].reshape(1, NH), p[Q_LG + NH:Q_N2, 0].reshape(1, NH),
            p[Q_N2:Q_NF].reshape(1, D), p[Q_NF:Q_ROWS].reshape(D))


def _rope_tables(ln):
    pos = np.arange(ln)
    rows = (pos // GRID_W).astype(np.float32)
    cols = (pos % GRID_W).astype(np.float32)
    n_freq = HD // 4
    inv = (np.float32(ROPE_BASE) ** (-np.arange(n_freq, dtype=np.float32) / np.float32(n_freq))).astype(np.float32)
    ar = rows[:, None] * inv[None, :]
    ac = cols[:, None] * inv[None, :]
    cos_t = np.concatenate([np.cos(ar), np.cos(ar), np.cos(ac), np.cos(ac)], axis=1).astype(np.float32)
    sin_t = np.concatenate([-np.sin(ar), np.sin(ar), -np.sin(ac), np.sin(ac)], axis=1).astype(np.float32)
    return jnp.asarray(cos_t), jnp.asarray(sin_t)


def kernel(x, c, ctx, c_ctx, w_mod, b_mod, norm1, w_in, sg_gain, sg_w, sg_b, ret_logit_f, ret_logit_b, w_out, norm2, w_gate, w_up, w_down, norm_f, loss_target, m_c_ctx, m_w_mod, m_b_mod, m_norm1, m_w_in, m_sg_gain, m_sg_w, m_sg_b, m_ret_logit_f, m_ret_logit_b, m_w_out, m_norm2, m_w_gate, m_w_up, m_w_down, m_norm_f, v_c_ctx, v_w_mod, v_b_mod, v_norm1, v_w_in, v_sg_gain, v_sg_w, v_sg_b, v_ret_logit_f, v_ret_logit_b, v_w_out, v_norm2, v_w_gate, v_w_up, v_w_down, v_norm_f):
    ln = x.shape[1]
    xi, yi, ci = _pos()
    chip = 2 * xi + yi
    me = 4 * xi + 2 * yi + ci
    x2d = x.reshape(ln, D)
    tgt = loss_target.reshape(ln, D)
    mod_c = w_mod.shape[2]

    tr = lambda a: jnp.swapaxes(a[0], 0, 1)
    gbufs, c_all, prod_all = _prologue(c, c_ctx.reshape(1, D), w_mod[0],
                                       [w_in[0], w_out[0], tr(w_gate), tr(w_up), w_down[0]])
    wi = gbufs[0].reshape(NCHIP, D, WI_C)
    gbufs_a, gbufs_b = gbufs[1:3], gbufs[3:5]
    c_all = c_all.reshape(NDEV, D)
    prod_chips = prod_all[0::2]
    mod_rows = jnp.transpose(prod_chips, (1, 0, 2)).reshape(16, NCHIP * mod_c) + b_mod
    mod = lax.dynamic_slice_in_dim(mod_rows, me, 1, axis=0)
    cmod = mod_rows[8:9]
    sh1, sc1, g1, sh2, sc2, g2 = [mod[:, i * D:(i + 1) * D] for i in range(6)]
    csh1, csc1 = cmod[:, 0:D], cmod[:, D:2 * D]
    zrow = jnp.zeros((1, D), F32)
    vec_in = jnp.concatenate([norm1, sh1, sc1] + [zrow] * 5, axis=0)
    vec_ctx = jnp.concatenate([norm1, csh1, csc1] + [zrow] * 5, axis=0)
    vec_post = jnp.concatenate([g1, norm2, sh2, sc2, g2, norm_f.reshape(1, D), zrow, zrow], axis=0)

    logits = jnp.concatenate([ret_logit_f.reshape(NH), ret_logit_b.reshape(NH)])
    dm = _decay_mats(jnp.broadcast_to(logits[:, None, None], (2 * NH, CH, CH)))
    ctx2d = ctx.reshape(ctx.shape[1], D)
    scf, scb = _ctx_forward(ctx2d, vec_ctx, wi, dm)

    cos_t, sin_t = _rope_tables(ln)
    z, hx, gbufs_a = _in_proj(x2d, vec_in, wi, gbufs_a)
    bfull = jnp.broadcast_to(sg_b[0][:, :, None], (NH, CH, CH))
    ycat, sf_all, sb_all, gbufs = _mixer_fwd(z, cos_t, sin_t, dm, sg_w[0], sg_gain, bfull, scf, scb,
                                             gbufs_a, gbufs_b)
    wo, wg_t, wu_t, wd = [g.reshape(NCHIP, 2 * g.shape[2], g.shape[3]) for g in gbufs]
    wo = wo.reshape(D, D)

    dx1, dycat, h2, dy, df, act, da, db, acc_post = _post_mixer(x2d, ycat, tgt, vec_post, wo, wg_t, wu_t, wd)

    cidx = ci.reshape(1).astype(jnp.int32)
    chipidx = chip.reshape(1).astype(jnp.int32)

    def halves_summed(full, names):
        full = [g.reshape(NCHIP, 2, g.shape[1] // 2, g.shape[2]) for g in full]
        from_sib = _rs_exchange_halves(full, "rs_exchange_" + names[0])
        return [_rs_add_halves(g, r, cidx, "rs_add_halves_" + nm) for g, r, nm in zip(full, from_sib, names)]

    def split(g):
        return g.reshape(NCHIP, 2, g.shape[1] // 2, g.shape[2])

    g_wd = split(_tn_matmul(act, df, "grad_w_down", NCHIP, FF_C, D, True, "shared", 1024))
    g_wu, x_wd = _tn_matmul(db, h2, "grad_w_up", NCHIP, FF_C, D, True, "shared", 1024, carry=g_wd)
    g_wu = split(g_wu)
    g_wg, x_wu = _tn_matmul(da, h2, "grad_w_gate", NCHIP, FF_C, D, True, "shared", 1024, carry=g_wu)
    g_wg = split(g_wg)
    g_wo, x_wg = _tn_matmul(ycat, dy, "grad_w_out", 1, D, D, False, "shared", 1024, carry=g_wg)
    g_wo = split(g_wo.reshape(NCHIP, WO_R, D))
    x_wo = _rs_exchange_halves([g_wo], "rs_exchange_w_out")[0]
    names = ["w_in", "w_out", "w_gate", "w_up", "w_down"]
    sums_b = [_rs_add_halves(g, r, cidx, "rs_add_halves_" + nm)
              for g, r, nm in zip([g_wo, g_wg, g_wu, g_wd], [x_wo, x_wg, x_wu, x_wd], names[1:])]

    (dz, ddm, dsgw, dgain, dbf, dscf, dscb), from_chips_b = _mixer_bwd(
        z, dycat, cos_t, sin_t, dm, sg_w[0], sg_gain, bfull, sf_all, sb_all, [s[1] for s in sums_b])
    gwkv, acc_ctx, ddm_ctx = _ctx_backward(ctx2d, vec_ctx, wi, dm, dscf, dscb)
    g_wi = _tn_matmul(hx, dz, "grad_w_in", NCHIP, D, WI_C, False, "cols", 512, ctx_kv=gwkv)
    sums_a = halves_summed([g_wi], names[:1])
    lg_part, dsgb = _small_reduce(ddm, ddm_ctx, dm, dbf)
    dcmod = jnp.concatenate([acc_ctx[1:2], acc_ctx[2:3], jnp.zeros((1, 4 * D), F32)], axis=1)
    dmod_rest = jnp.concatenate([acc_post[0:1], acc_post[2:3], acc_post[3:4], acc_post[4:5]], axis=1)
    early = jnp.concatenate([_rows(dcmod), _rows(dmod_rest), _rows(dgain), _rows(dsgw), dsgb, lg_part,
                             _rows(acc_post[1:2]), _rows(acc_post[5:6]), _rows(acc_post[6:7])], axis=0)
    gx, acc_in, from_chips_a, early_all = _in_proj_bwd(dz, x2d, dx1, vec_in, wi, [s[1] for s in sums_a], early)

    sums = sums_a + sums_b
    from_chips = list(from_chips_a) + list(from_chips_b)
    finals = [_rs_add_chips(s[0], r, chipidx, "rs_add_chips_" + nm) for s, r, nm in zip(sums, from_chips, names)]
    others = _rs_share_final(finals)

    late = jnp.concatenate([_rows(acc_in[1:2]), _rows(acc_in[2:3]), _rows(acc_in[0:1] + acc_ctx[0:1])], axis=0)
    late_all = _allgather_small(late, "gather_small")
    n_dc, n_l = P_DMOD - P_DCMOD, 16
    gathered = jnp.concatenate([early_all[:, :n_dc], late_all[:, :n_l], early_all[:, n_dc:n_dc + 32],
                                late_all[:, n_l:], early_all[:, n_dc + 32:]], axis=1)
    dmod_all = gathered[:, P_DMOD:P_N1].reshape(NDEV, 6 * D)
    dcmod_all = gathered[:, P_DCMOD:P_DMOD].reshape(NDEV, 6 * D)
    dmod_cols = lax.dynamic_slice_in_dim(dmod_all, chip * mod_c, mod_c, axis=1)
    dcmod_cols = lax.dynamic_slice_in_dim(dcmod_all, chip * mod_c, mod_c, axis=1)
    dmod_pad = jnp.concatenate([dmod_cols, jnp.zeros((CH - NDEV, mod_c), F32)], axis=0)
    ct_pad_t = jnp.concatenate([jnp.transpose(c_all), jnp.zeros((D, CH - NDEV), F32)], axis=1)
    g_wmod, cctx_part = _mod_backward(ct_pad_t, c_ctx.reshape(D, 1), dmod_pad, dcmod_cols, w_mod[0])
    parts = _allgather_small(cctx_part[0:1], "gather_cctx")
    g_cctx, d_cctx, nm_cctx, nv_cctx = _cctx_update(parts, c_ctx.reshape(1, D), m_c_ctx.reshape(1, D),
                                                    v_c_ctx.reshape(1, D))

    wp = _pack_small(b_mod, norm1, sg_gain, sg_w, sg_b, ret_logit_f, ret_logit_b, norm2, norm_f)
    mp = _pack_small(m_b_mod, m_norm1, m_sg_gain, m_sg_w, m_sg_b, m_ret_logit_f, m_ret_logit_b, m_norm2, m_norm_f)
    vp = _pack_small(v_b_mod, v_norm1, v_sg_gain, v_sg_w, v_sg_b, v_ret_logit_f, v_ret_logit_b, v_norm2, v_norm_f)
    gp, dp, mp2, vp2, loss_t = _small_update(gathered, wp, mp, vp)

    big_w = [w_in[0], w_out[0], tr(w_gate), tr(w_up), w_down[0]]
    big_m = [m_w_in[0], m_w_out[0], tr(m_w_gate), tr(m_w_up), m_w_down[0]]
    big_v = [v_w_in[0], v_w_out[0], tr(v_w_gate), tr(v_w_up), v_w_down[0]]
    upd = [_adamw_halves(w, own, oth, m, v, cidx, "adamw_" + nm) for w, own, oth, m, v, nm in
           zip(big_w, finals, others, big_m, big_v, names)]
    big_g = [g_wmod] + [u[0] for u in upd]
    big = [_adamw(w_mod[0], g_wmod, m_w_mod[0], v_w_mod[0], "adamw_w_mod")] + [u[1:] for u in upd]

    def assemble(small, cctx, bigs):
        b_mod_, norm1_, gain_, sgw_, sgb_, lf_, lb_, norm2_, normf_ = _unpack_small(small)
        wm, wi_, wo_, wg_, wu_, wd_ = [b[None] for b in bigs]
        wg_, wu_ = jnp.swapaxes(wg_, 1, 2), jnp.swapaxes(wu_, 1, 2)
        return [cctx.reshape(D), wm, b_mod_, norm1_, wi_, gain_, sgw_, sgb_, lf_, lb_, wo_, norm2_, wg_, wu_, wd_,
                normf_]

    out = [loss_t[0, 0], gx.reshape(1, ln, D)]
    out += assemble(gp, g_cctx, big_g)
    out += assemble(dp, d_cctx, [b[0] for b in big])
    out += assemble(mp2, nm_cctx, [b[1] for b in big])
    out += assemble(vp2, nv_cctx, [b[2] for b in big])
    return tuple(out)
```

```python
import functools

import jax
import jax.numpy as jnp
import numpy as np
from jax import lax
from jax.experimental import pallas as pl
from jax.experimental.pallas import tpu as pltpu

F32 = jnp.float32
BF = jnp.bfloat16
MESH = pl.DeviceIdType.MESH

D = 1024
CH = 128
HD = 128
NH = 4
AW = 512
IN_COLS = 3584
DFF = 2816
NCHIP = 4
NDEV = 8
WI_C = IN_COLS // NCHIP
FF_C = DFF // NCHIP
WO_R = D // NCHIP
EPS = 1e-6
GRID_W = 64
ROPE_BASE = 10000.0
K_SCALE = HD ** -0.5
LR, B1, B2, AEPS, WD, STEP = 0.001, 0.9, 0.999, 1e-08, 0.01, 10
VMEM_MB = 1 << 20
HI = lax.Precision.HIGHEST

P_DCMOD, P_DMOD, P_N1, P_GAIN, P_SGW, P_SGB, P_LG, P_N2, P_NF, P_LOSS = 0, 48, 96, 104, 112, 624, 632, 640, 648, 656
P_ROWS = 664
Q_BMOD, Q_N1, Q_GAIN, Q_SGW, Q_SGB, Q_LG, Q_N2, Q_NF = 0, 48, 56, 64, 576, 584, 592, 600
Q_ROWS = 608


def _params(vmem_mb, sem=None):
    return pltpu.CompilerParams(vmem_limit_bytes=vmem_mb * VMEM_MB, dimension_semantics=sem)


def _const(shape):
    nd = len(shape)
    return pl.BlockSpec(shape, lambda *_: (0,) * nd, pipeline_mode=pl.Buffered(1))


def _pos():
    return lax.axis_index("x"), lax.axis_index("y"), lax.axis_index("c")


def _dot(a, b, dims):
    return lax.dot_general(a, b, (dims, ((), ())), preferred_element_type=F32)


NN = ((1,), (0,))
NT = ((1,), (1,))
TN = ((0,), (0,))


@jax.custom_vjp
def _mm(a, b):
    return _dot(a.astype(BF), b.astype(BF), NN)


def _mm_f(a, b):
    return _mm(a, b), (a.astype(BF), b.astype(BF))


def _mm_b(res, g):
    a, b = res
    gb = g.astype(BF)
    return _dot(gb, b, NT), _dot(a, gb, TN)


_mm.defvjp(_mm_f, _mm_b)


@jax.custom_vjp
def _mm_nt(a, b):
    return _dot(a.astype(BF), b.astype(BF), NT)


def _mm_nt_f(a, b):
    return _mm_nt(a, b), (a.astype(BF), b.astype(BF))


def _mm_nt_b(res, g):
    a, b = res
    gb = g.astype(BF)
    return _dot(gb, b, NN), _dot(gb, a, TN)


_mm_nt.defvjp(_mm_nt_f, _mm_nt_b)


@jax.custom_vjp
def _mm_tn(a, b):
    return _dot(a.astype(BF), b.astype(BF), TN)


def _mm_tn_f(a, b):
    return _mm_tn(a, b), (a.astype(BF), b.astype(BF))


def _mm_tn_b(res, g):
    a, b = res
    gb = g.astype(BF)
    return _dot(b, gb, NT), _dot(a, gb, NN)


_mm_tn.defvjp(_mm_tn_f, _mm_tn_b)


def _gelu(x):
    return x * (0.5 * (1.0 + jnp.tanh(0.7978845608028654 * (x + 0.044715 * (x * x * x)))))


def _silu(x):
    return x * jax.nn.sigmoid(x)


def _rms(x):
    return lax.rsqrt(jnp.mean(x * x, axis=-1, keepdims=True) + EPS)


def _swap32(t):
    lane = lax.broadcasted_iota(jnp.int32, t.shape, 1)
    first = (lane % 64) < 32
    return jnp.where(first, pltpu.roll(t, 96, 1), pltpu.roll(t, 32, 1))


def _rope(t, cos, sin):
    return t * cos + _swap32(t) * sin


def _rope_bwd(d, cos, sin):
    return d * cos + _swap32(d * sin)


def _heads(ref, r0=0):
    return [ref[r0:r0 + CH, h * HD:(h + 1) * HD].astype(F32) for h in range(NH)]


def _chunk_fwd(u, v, q, k, vr, gf, gb, sf, sb, df, xf, zf, db, xb, zb, sgw, gain, bfull):
    ya, yr, uf, ub = [], [], [], []
    for g in range(NH):
        gu = _gelu(u[g])
        gv = _gelu(v[g])
        vn = gv * _rms(gv) * gain[g]
        ya.append(gu * (_mm(sgw[g], vn) + bfull[g]))
    for h in range(NH):
        a = _mm_nt(q[h], k[h])
        of = _mm(a * df[h], vr[h]) + xf[h] * _mm(q[h], sf[h])
        ob = _mm(a * db[h], vr[h]) + xb[h] * _mm(q[h], sb[h])
        yr.append(_silu(gf[h]) * (of * _rms(of)) + _silu(gb[h]) * (ob * _rms(ob)))
        uf.append(_mm_tn(k[h], zf[h] * vr[h]))
        ub.append(_mm_tn(k[h], zb[h] * vr[h]))
    return ya, yr, uf, ub


def _fwd_dir_only(sf, q, k, vr, gf, df, xf):
    out = []
    for h in range(NH):
        a = _mm_nt(q[h], k[h])
        of = _mm(a * df[h], vr[h]) + xf[h] * _mm(q[h], sf[h])
        out.append(_silu(gf[h]) * (of * _rms(of)))
    return out


def _ctx_states(ctx0, ctx1, n1, csh, csc, wk, wv, zf, zb, ef, eb):
    hc0 = (ctx0 * _rms(ctx0) * n1) * (1.0 + csc) + csh
    hc1 = (ctx1 * _rms(ctx1) * n1) * (1.0 + csc) + csh
    scf, scb = [], []
    for h in range(NH):
        k0, k1 = _mm(hc0, wk[h]) * K_SCALE, _mm(hc1, wk[h]) * K_SCALE
        v0, v1 = _mm(hc0, wv[h]), _mm(hc1, wv[h])
        scf.append(ef[h] * _mm_tn(k0, zf[h] * v0) + _mm_tn(k1, zf[h] * v1))
        scb.append(eb[h] * _mm_tn(k1, zb[h] * v1) + _mm_tn(k0, zb[h] * v0))
    return scf, scb


def _allgather_small(v, name):
    r, n = v.shape

    def body(v_ref, out_ref, send_sems, recv_sems, local_sem):
        x, y, c = _pos()
        me = 4 * x + 2 * y + c
        mine = pltpu.make_async_copy(v_ref, out_ref.at[me], local_sem)
        mine.start()
        sent = []
        for k in range(1, NDEV):
            kx, ky, kc = (k >> 2) & 1, (k >> 1) & 1, k & 1
            peer = (x ^ kx, y ^ ky, c ^ kc)
            cp = pltpu.make_async_remote_copy(src_ref=v_ref, dst_ref=out_ref.at[me], send_sem=send_sems.at[k - 1],
                                              recv_sem=recv_sems.at[k - 1], device_id=peer, device_id_type=MESH)
            cp.start()
            sent.append(cp)
        for k in range(1, NDEV):
            kx, ky, kc = (k >> 2) & 1, (k >> 1) & 1, k & 1
            peer = (x ^ kx, y ^ ky, c ^ kc)
            src = 4 * (x ^ kx) + 2 * (y ^ ky) + (c ^ kc)
            pltpu.make_async_remote_copy(src_ref=v_ref, dst_ref=out_ref.at[src], send_sem=send_sems.at[k - 1],
                                         recv_sem=recv_sems.at[k - 1], device_id=peer, device_id_type=MESH).wait_recv()
        for cp in sent:
            cp.wait_send()
        mine.wait()

    return pl.pallas_call(
        body, name=name,
        out_shape=jax.ShapeDtypeStruct((NDEV, r, n), F32),
        in_specs=[pl.BlockSpec(memory_space=pltpu.VMEM)],
        out_specs=pl.BlockSpec(memory_space=pltpu.VMEM),
        scratch_shapes=[pltpu.SemaphoreType.DMA((NDEV - 1,)), pltpu.SemaphoreType.DMA((NDEV - 1,)),
                        pltpu.SemaphoreType.DMA],
        compiler_params=_params(16),
    )(v)


def _gather_weights(shards):
    nt = len(shards)
    shapes = [s.shape for s in shards]

    def body(*refs):
        srcs, outs, stages = refs[:nt], refs[nt:2 * nt], refs[2 * nt:3 * nt]
        ici_send, ici_recv, d2d_send, d2d_recv, local_sems = refs[3 * nt:]
        x, y, c = _pos()
        chip = 2 * x + y
        for t in range(nt):
            half = shapes[t][0] // 2
            stages[t][0] = srcs[t][0:half, :].astype(BF)
            stages[t][1] = srcs[t][half:2 * half, :].astype(BF)
        local = []
        for t in range(nt):
            cp = pltpu.make_async_copy(stages[t], outs[t].at[chip], local_sems.at[t])
            cp.start()
            local.append(cp)
        sent = []
        for k in range(1, NCHIP):
            kx, ky = (k >> 1) & 1, k & 1
            for t in range(nt):
                s = (k - 1) * nt + t
                cp = pltpu.make_async_remote_copy(
                    src_ref=stages[t].at[c], dst_ref=outs[t].at[chip, c], send_sem=ici_send.at[s],
                    recv_sem=ici_recv.at[s], device_id=(x ^ kx, y ^ ky, c), device_id_type=MESH)
                cp.start()
                sent.append(cp)
        for k in range(1, NCHIP):
            kx, ky = (k >> 1) & 1, k & 1
            src_chip = 2 * (x ^ kx) + (y ^ ky)
            for t in range(nt):
                s = (k - 1) * nt + t
                pltpu.make_async_remote_copy(
                    src_ref=stages[t].at[c], dst_ref=outs[t].at[src_chip, c], send_sem=ici_send.at[s],
                    recv_sem=ici_recv.at[s], device_id=(x ^ kx, y ^ ky, c), device_id_type=MESH).wait_recv()
                cp = pltpu.make_async_remote_copy(
                    src_ref=outs[t].at[src_chip, c], dst_ref=outs[t].at[src_chip, c], send_sem=d2d_send.at[s],
                    recv_sem=d2d_recv.at[s], device_id=(x, y, 1 - c), device_id_type=MESH)
                cp.start()
                sent.append(cp)
        for k in range(1, NCHIP):
            kx, ky = (k >> 1) & 1, k & 1
            src_chip = 2 * (x ^ kx) + (y ^ ky)
            for t in range(nt):
                s = (k - 1) * nt + t
                pltpu.make_async_remote_copy(
                    src_ref=stages[t].at[c], dst_ref=outs[t].at[src_chip, 1 - c], send_sem=d2d_send.at[s],
                    recv_sem=d2d_recv.at[s], device_id=(x, y, 1 - c), device_id_type=MESH).wait_recv()
        for cp in sent:
            cp.wait_send()
        for cp in local:
            cp.wait()

    n_rem = (NCHIP - 1) * nt
    out = pl.pallas_call(
        body, name="gather_weights",
        out_shape=[jax.ShapeDtypeStruct((NCHIP, 2, r // 2, cc), BF) for r, cc in shapes],
        in_specs=[pl.BlockSpec(memory_space=pltpu.VMEM)] * nt,
        out_specs=[pl.BlockSpec(memory_space=pl.ANY)] * nt,
        scratch_shapes=[pltpu.VMEM((2, r // 2, cc), BF) for r, cc in shapes]
        + [pltpu.SemaphoreType.DMA((n_rem,))] * 4 + [pltpu.SemaphoreType.DMA((nt,))],
        compiler_params=_params(48),
    )(*shards)
    return [o.reshape(NCHIP, r, cc) for o, (r, cc) in zip(out, shapes)]


def _chip_offsets():
    return [((k >> 1) & 1, k & 1) for k in range(1, NCHIP)]


def _prologue(c, c_ctx, w_mod_s, shards):
    nt = len(shards)
    shapes = [s.shape for s in shards]
    mod_c = w_mod_s.shape[1]

    def body(*refs):
        c_ref, cc_ref, wm_ref = refs[:3]
        srcs = refs[3:3 + nt]
        outs = refs[3 + nt:3 + 2 * nt]
        call_ref, prod_ref = refs[3 + 2 * nt:5 + 2 * nt]
        stages = refs[5 + 2 * nt:5 + 3 * nt]
        ct = refs[5 + 3 * nt]
        c_send, c_recv, p_send, p_recv, ici_send, ici_recv, d2d_send, d2d_recv, local_sems = refs[6 + 3 * nt:]
        x, y, c = _pos()
        chip = 2 * x + y
        me = 4 * x + 2 * y + c
        sib = (x, y, 1 - c)
        pending = []
        for t in range(nt):
            half = shapes[t][0] // 2
            stages[t][0] = srcs[t][0:half, :].astype(BF)
            stages[t][1] = srcs[t][half:2 * half, :].astype(BF)
            cp = pltpu.make_async_copy(stages[t], outs[t].at[chip], local_sems.at[t])
            cp.start()
            pending.append(cp)
        sends = []
        for k, (kx, ky) in enumerate(_chip_offsets()):
            cp = pltpu.make_async_remote_copy(src_ref=stages[0].at[c], dst_ref=outs[0].at[chip, c],
                                              send_sem=ici_send.at[k], recv_sem=ici_recv.at[k],
                                              device_id=(x ^ kx, y ^ ky, c), device_id_type=MESH)
            cp.start()
            sends.append(cp)

        def to_all(src, dst_of, send_sems, recv_sems):
            for k in range(1, NDEV):
                kx, ky, kc = (k >> 2) & 1, (k >> 1) & 1, k & 1
                cp = pltpu.make_async_remote_copy(src_ref=src, dst_ref=dst_of(me), send_sem=send_sems.at[k - 1],
                                                  recv_sem=recv_sems.at[k - 1], device_id=(x ^ kx, y ^ ky, c ^ kc),
                                                  device_id_type=MESH)
                cp.start()
                sends.append(cp)
            for k in range(1, NDEV):
                kx, ky, kc = (k >> 2) & 1, (k >> 1) & 1, k & 1
                frm = 4 * (x ^ kx) + 2 * (y ^ ky) + (c ^ kc)
                pltpu.make_async_remote_copy(src_ref=src, dst_ref=dst_of(frm), send_sem=send_sems.at[k - 1],
                                             recv_sem=recv_sems.at[k - 1], device_id=(x ^ kx, y ^ ky, c ^ kc),
                                             device_id_type=MESH).wait_recv()

        call_ref[me] = c_ref[...]
        to_all(c_ref, lambda d: call_ref.at[d], c_send, c_recv)
        ct[...] = jnp.zeros_like(ct)
        for d in range(NDEV):
            ct[d:d + 1, :] = call_ref[d]
        ct[NDEV:NDEV + 1, :] = cc_ref[...]
        prod_ref[me] = jnp.dot(_silu(ct[...]), wm_ref[...], precision=HI, preferred_element_type=F32)
        to_all(prod_ref.at[me], lambda d: prod_ref.at[d], p_send, p_recv)

        for k, (kx, ky) in enumerate(_chip_offsets()):
            frm = 2 * (x ^ kx) + (y ^ ky)
            pltpu.make_async_remote_copy(src_ref=stages[0].at[c], dst_ref=outs[0].at[frm, c],
                                         send_sem=ici_send.at[k], recv_sem=ici_recv.at[k],
                                         device_id=(x ^ kx, y ^ ky, c), device_id_type=MESH).wait_recv()
            cp = pltpu.make_async_remote_copy(src_ref=outs[0].at[frm, c], dst_ref=outs[0].at[frm, c],
                                              send_sem=d2d_send.at[k], recv_sem=d2d_recv.at[k],
                                              device_id=sib, device_id_type=MESH)
            cp.start()
            sends.append(cp)
        for k, (kx, ky) in enumerate(_chip_offsets()):
            frm = 2 * (x ^ kx) + (y ^ ky)
            pltpu.make_async_remote_copy(src_ref=stages[0].at[c], dst_ref=outs[0].at[frm, 1 - c],
                                         send_sem=d2d_send.at[k], recv_sem=d2d_recv.at[k],
                                         device_id=sib, device_id_type=MESH).wait_recv()
        for cp in sends:
            cp.wait_send()
        for cp in pending:
            cp.wait()

    vm = pl.BlockSpec(memory_space=pltpu.VMEM)
    out = pl.pallas_call(
        body, name="prologue",
        out_shape=[jax.ShapeDtypeStruct((NCHIP, 2, r // 2, cc), BF) for r, cc in shapes]
        + [jax.ShapeDtypeStruct((NDEV, 1, D), F32), jax.ShapeDtypeStruct((NDEV, 16, mod_c), F32)],
        in_specs=[vm] * (3 + nt),
        out_specs=[pl.BlockSpec(memory_space=pl.ANY)] * nt + [vm, vm],
        scratch_shapes=[pltpu.VMEM((2, r // 2, cc), BF) for r, cc in shapes] + [pltpu.VMEM((16, D), F32)]
        + [pltpu.SemaphoreType.DMA((NDEV - 1,))] * 4 + [pltpu.SemaphoreType.DMA((NCHIP - 1,))] * 4
        + [pltpu.SemaphoreType.DMA((nt,))],
        compiler_params=_params(56),
    )(c, c_ctx, w_mod_s, *shards)
    return out[:nt], out[nt], out[nt + 1]


def _gather_ici_copies(bufs, send_sems, recv_sems):
    x, y, c = _pos()
    chip = 2 * x + y
    nt = len(bufs)
    out_cp, in_cp = [], []
    for k, (kx, ky) in enumerate(_chip_offsets()):
        frm = 2 * (x ^ kx) + (y ^ ky)
        for t in range(nt):
            s = k * nt + t
            peer = (x ^ kx, y ^ ky, c)
            out_cp.append(pltpu.make_async_remote_copy(
                src_ref=bufs[t].at[chip, c], dst_ref=bufs[t].at[chip, c], send_sem=send_sems.at[s],
                recv_sem=recv_sems.at[s], device_id=peer, device_id_type=MESH))
            in_cp.append(pltpu.make_async_remote_copy(
                src_ref=bufs[t].at[chip, c], dst_ref=bufs[t].at[frm, c], send_sem=send_sems.at[s],
                recv_sem=recv_sems.at[s], device_id=peer, device_id_type=MESH))
    return out_cp, in_cp


def _gather_d2d_copies(bufs, send_sems, recv_sems):
    x, y, c = _pos()
    nt = len(bufs)
    out_cp, in_cp = [], []
    for k, (kx, ky) in enumerate(_chip_offsets()):
        frm = 2 * (x ^ kx) + (y ^ ky)
        for t in range(nt):
            s = k * nt + t
            out_cp.append(pltpu.make_async_remote_copy(
                src_ref=bufs[t].at[frm, c], dst_ref=bufs[t].at[frm, c], send_sem=send_sems.at[s],
                recv_sem=recv_sems.at[s], device_id=(x, y, 1 - c), device_id_type=MESH))
            in_cp.append(pltpu.make_async_remote_copy(
                src_ref=bufs[t].at[frm, c], dst_ref=bufs[t].at[frm, 1 - c], send_sem=send_sems.at[s],
                recv_sem=recv_sems.at[s], device_id=(x, y, 1 - c), device_id_type=MESH))
    return out_cp, in_cp


def _scatter_ici_copies(parts, outs, send_sems, recv_sems):
    x, y, c = _pos()
    nt = len(parts)
    cps = []
    for k, (kx, ky) in enumerate(_chip_offsets()):
        dst_chip = 2 * (x ^ kx) + (y ^ ky)
        for t in range(nt):
            s = k * nt + t
            cps.append(pltpu.make_async_remote_copy(
                src_ref=parts[t].at[dst_chip], dst_ref=outs[t].at[k], send_sem=send_sems.at[s],
                recv_sem=recv_sems.at[s], device_id=(x ^ kx, y ^ ky, c), device_id_type=MESH))
    return cps


def _rs_exchange_halves(grads, name):
    nt = len(grads)
    shapes = [g.shape for g in grads]

    def body(*refs):
        gs, outs = refs[:nt], refs[nt:2 * nt]
        send_sems, recv_sems = refs[2 * nt:]
        x, y, c = _pos()
        sib = (x, y, 1 - c)
        sent = []
        for t in range(nt):
            for j in range(NCHIP):
                s = t * NCHIP + j
                cp = pltpu.make_async_remote_copy(src_ref=gs[t].at[j, 1 - c], dst_ref=outs[t].at[j],
                                                  send_sem=send_sems.at[s], recv_sem=recv_sems.at[s],
                                                  device_id=sib, device_id_type=MESH)
                cp.start()
                sent.append(cp)
        for cp in sent:
            cp.wait_recv()
        for cp in sent:
            cp.wait_send()

    return pl.pallas_call(
        body, name=name,
        out_shape=[jax.ShapeDtypeStruct((NCHIP, s[2], s[3]), F32) for s in shapes],
        in_specs=[pl.BlockSpec(memory_space=pl.ANY)] * nt,
        out_specs=[pl.BlockSpec(memory_space=pl.ANY)] * nt,
        scratch_shapes=[pltpu.SemaphoreType.DMA((nt * NCHIP,))] * 2,
    )(*grads)


def _rs_send_chips(parts):
    nt = len(parts)
    shapes = [p.shape for p in parts]

    def body(*refs):
        ps, outs = refs[:nt], refs[nt:2 * nt]
        send_sems, recv_sems = refs[2 * nt:]
        x, y, c = _pos()
        sent = []
        for k in range(1, NCHIP):
            kx, ky = (k >> 1) & 1, k & 1
            dst_chip = 2 * (x ^ kx) + (y ^ ky)
            for t in range(nt):
                s = (k - 1) * nt + t
                cp = pltpu.make_async_remote_copy(src_ref=ps[t].at[dst_chip], dst_ref=outs[t].at[k - 1],
                                                  send_sem=send_sems.at[s], recv_sem=recv_sems.at[s],
                                                  device_id=(x ^ kx, y ^ ky, c), device_id_type=MESH)
                cp.start()
                sent.append(cp)
        for cp in sent:
            cp.wait_recv()
        for cp in sent:
            cp.wait_send()

    return pl.pallas_call(
        body, name="rs_send_chips",
        out_shape=[jax.ShapeDtypeStruct((NCHIP - 1, s[1], s[2]), BF) for s in shapes],
        in_specs=[pl.BlockSpec(memory_space=pl.ANY)] * nt,
        out_specs=[pl.BlockSpec(memory_space=pl.ANY)] * nt,
        scratch_shapes=[pltpu.SemaphoreType.DMA((nt * (NCHIP - 1),))] * 2,
    )(*parts)


def _rs_share_final(finals):
    nt = len(finals)
    shapes = [f.shape for f in finals]

    def body(*refs):
        fs, outs = refs[:nt], refs[nt:2 * nt]
        send_sems, recv_sems = refs[2 * nt:]
        x, y, c = _pos()
        sent = []
        for t in range(nt):
            cp = pltpu.make_async_remote_copy(src_ref=fs[t], dst_ref=outs[t], send_sem=send_sems.at[t],
                                              recv_sem=recv_sems.at[t], device_id=(x, y, 1 - c), device_id_type=MESH)
            cp.start()
            sent.append(cp)
        for cp in sent:
            cp.wait_recv()
        for cp in sent:
            cp.wait_send()

    return pl.pallas_call(
        body, name="rs_share_final",
        out_shape=[jax.ShapeDtypeStruct(s, F32) for s in shapes],
        in_specs=[pl.BlockSpec(memory_space=pl.ANY)] * nt,
        out_specs=[pl.BlockSpec(memory_space=pl.ANY)] * nt,
        scratch_shapes=[pltpu.SemaphoreType.DMA((nt,))] * 2,
    )(*finals)


def _row_tile(h, cc=D):
    for t in (512, 384, 352, 256, 176, 128, 64, 32, 16):
        if h % t == 0 and t * cc * 4 <= (5 * VMEM_MB) // 4:
            return t
    return h


def _rs_add_halves(g, recv, cidx, name):
    _, _, h, cc = g.shape
    th = _row_tile(h, cc)

    def body(c_ref, g_ref, r_ref, of_ref, ob_ref):
        s = g_ref[...] + r_ref[...]
        of_ref[...] = s
        ob_ref[...] = s.astype(BF)

    return pl.pallas_call(
        body, name=name,
        grid_spec=pltpu.PrefetchScalarGridSpec(
            num_scalar_prefetch=1, grid=(NCHIP, h // th),
            in_specs=[pl.BlockSpec((None, None, th, cc), lambda j, i, c_ref: (j, c_ref[0], i, 0)),
                      pl.BlockSpec((None, th, cc), lambda j, i, c_ref: (j, i, 0))],
            out_specs=[pl.BlockSpec((None, th, cc), lambda j, i, c_ref: (j, i, 0)),
                       pl.BlockSpec((None, th, cc), lambda j, i, c_ref: (j, i, 0))]),
        out_shape=[jax.ShapeDtypeStruct((NCHIP, h, cc), F32), jax.ShapeDtypeStruct((NCHIP, h, cc), BF)],
        compiler_params=_params(48),
    )(cidx, g, recv)


def _rs_add_chips(own, recv, chipidx, name):
    _, h, cc = own.shape
    th = _row_tile(h, cc)

    def body(j_ref, o_ref, r_ref, out_ref):
        out_ref[...] = ((o_ref[...] + r_ref[0].astype(F32)) + r_ref[1].astype(F32)) + r_ref[2].astype(F32)

    return pl.pallas_call(
        body, name=name,
        grid_spec=pltpu.PrefetchScalarGridSpec(
            num_scalar_prefetch=1, grid=(h // th,),
            in_specs=[pl.BlockSpec((None, th, cc), lambda i, j_ref: (j_ref[0], i, 0)),
                      pl.BlockSpec((NCHIP - 1, th, cc), lambda i, j_ref: (0, i, 0))],
            out_specs=pl.BlockSpec((th, cc), lambda i, j_ref: (i, 0))),
        out_shape=jax.ShapeDtypeStruct((h, cc), F32),
        compiler_params=_params(48),
    )(chipidx, own, recv)


def _adamw_math(w, g, m, v):
    m2 = B1 * m + (1.0 - B1) * g
    v2 = B2 * v + (1.0 - B2) * (g * g)
    m_hat = m2 / (1.0 - B1 ** STEP)
    v_hat = v2 / (1.0 - B2 ** STEP)
    delta = -LR * (m_hat / (jnp.sqrt(v_hat) + AEPS) + WD * w)
    return delta, m2, v2


def _adamw(w, g, m, v, name):
    r, cc = w.shape
    tr = _row_tile(r, cc)

    def body(w_ref, g_ref, m_ref, v_ref, d_ref, mo_ref, vo_ref):
        d, m2, v2 = _adamw_math(w_ref[...], g_ref[...], m_ref[...], v_ref[...])
        d_ref[...] = d
        mo_ref[...] = m2
        vo_ref[...] = v2

    spec = pl.BlockSpec((tr, cc), lambda i: (i, 0))
    return pl.pallas_call(
        body, name=name, grid=(r // tr,), in_specs=[spec] * 4, out_specs=[spec] * 3,
        out_shape=[jax.ShapeDtypeStruct((r, cc), F32)] * 3,
        compiler_params=_params(48, ("parallel",)),
    )(w, g, m, v)


def _adamw_halves(w, own, other, m, v, cidx, name):
    r, cc = w.shape
    h = r // 2
    tr = _row_tile(h, cc)
    per = h // tr

    def body(c_ref, w_ref, own_ref, oth_ref, m_ref, v_ref, g_ref, d_ref, mo_ref, vo_ref):
        mine = (pl.program_id(0) // per) == c_ref[0]
        g = jnp.where(mine, own_ref[...], oth_ref[...])
        g_ref[...] = g
        d, m2, v2 = _adamw_math(w_ref[...], g, m_ref[...], v_ref[...])
        d_ref[...] = d
        mo_ref[...] = m2
        vo_ref[...] = v2

    full = pl.BlockSpec((tr, cc), lambda i, c_ref: (i, 0))
    half = pl.BlockSpec((tr, cc), lambda i, c_ref: (i % per, 0))
    return pl.pallas_call(
        body, name=name,
        grid_spec=pltpu.PrefetchScalarGridSpec(
            num_scalar_prefetch=1, grid=(r // tr,),
            in_specs=[full, half, half, full, full], out_specs=[full] * 4),
        out_shape=[jax.ShapeDtypeStruct((r, cc), F32)] * 4,
        compiler_params=_params(48, ("parallel",)),
    )(cidx, w, own, other, m, v)


def _mod_forward(ct_pad, w_mod_s):
    def body(c_ref, w_ref, o_ref):
        o_ref[...] = jnp.dot(_silu(c_ref[...]), w_ref[...], precision=HI, preferred_element_type=F32)

    return pl.pallas_call(
        body, name="mod_forward",
        out_shape=jax.ShapeDtypeStruct((16, w_mod_s.shape[1]), F32),
        in_specs=[pl.BlockSpec(memory_space=pltpu.VMEM)] * 2,
        out_specs=pl.BlockSpec(memory_space=pltpu.VMEM),
        compiler_params=_params(32),
    )(ct_pad, w_mod_s)


def _decay_exponents():
    ri = lax.broadcasted_iota(jnp.int32, (CH, CH), 0).astype(F32)
    ci = lax.broadcasted_iota(jnp.int32, (CH, CH), 1).astype(F32)
    full = jnp.full((CH, CH), float(CH), F32)
    return [[ri - ci, ri + 1.0, (CH - 1.0) - ri, full], [ci - ri, CH - ri, ri, full]]


def _decay_mats(logit_full):
    def body(l_ref, o_ref):
        ex = _decay_exponents()
        for d in range(2):
            for h in range(NH):
                lv = l_ref[d * NH + h]
                lg = jnp.minimum(lv, 0.0) - jnp.log(1.0 + jnp.exp(-jnp.abs(lv)))
                for kind in range(4):
                    m = jnp.exp(lg * ex[d][kind])
                    if kind == 0:
                        m = jnp.where(ex[d][0] >= 0.0, jnp.exp(lg * jnp.maximum(ex[d][0], 0.0)), 0.0)
                    o_ref[d, kind, h] = m

    return pl.pallas_call(
        body, name="decay_mats",
        out_shape=jax.ShapeDtypeStruct((2, 4, NH, CH, CH), F32),
        in_specs=[pl.BlockSpec(memory_space=pltpu.VMEM)],
        out_specs=pl.BlockSpec(memory_space=pltpu.VMEM),
        compiler_params=_params(32),
    )(logit_full)


def _ctx_kv_weights(wi_ref):
    def cols(g):
        return wi_ref[g // WI_C, :, g % WI_C: g % WI_C + HD].astype(F32)

    wk = [cols(3 * AW + h * HD) for h in range(NH)]
    wv = [cols(4 * AW + h * HD) for h in range(NH)]
    return wk, wv


def _ctx_forward(ctx, vecs, wi, dm):
    def body(ctx_ref, v_ref, wi_ref, dm_ref, scf_ref, scb_ref):
        wk, wv = _ctx_kv_weights(wi_ref)
        mats = [[dm_ref[d, kind, h] for h in range(NH)] for d in range(2) for kind in (2, 3)]
        scf, scb = _ctx_states(ctx_ref[0:CH, :], ctx_ref[CH:2 * CH, :], v_ref[0:1, :], v_ref[1:2, :],
                               v_ref[2:3, :], wk, wv, mats[0], mats[2], mats[1], mats[3])
        for h in range(NH):
            scf_ref[h] = scf[h]
            scb_ref[h] = scb[h]

    return pl.pallas_call(
        body, name="ctx_forward",
        out_shape=[jax.ShapeDtypeStruct((NH, HD, HD), F32)] * 2,
        in_specs=[pl.BlockSpec(memory_space=pltpu.VMEM)] * 4,
        out_specs=[pl.BlockSpec(memory_space=pltpu.VMEM)] * 2,
        compiler_params=_params(48),
    )(ctx, vecs, wi, dm)


def _ctx_backward(ctx, vecs, wi, dm, dscf, dscb):
    def body(ctx_ref, v_ref, wi_ref, dm_ref, gf_ref, gb_ref, gw_ref, gv_ref, gdm_ref):
        wk, wv = _ctx_kv_weights(wi_ref)
        mats = [[dm_ref[d, kind, h] for h in range(NH)] for d in range(2) for kind in (2, 3)]
        ctx0, ctx1 = ctx_ref[0:CH, :], ctx_ref[CH:2 * CH, :]

        def fn(n1, csh, csc, wk_, wv_, zf, zb, ef, eb):
            return _ctx_states(ctx0, ctx1, n1, csh, csc, wk_, wv_, zf, zb, ef, eb)

        _, vjp = jax.vjp(fn, v_ref[0:1, :], v_ref[1:2, :], v_ref[2:3, :], wk, wv,
                         mats[0], mats[2], mats[1], mats[3])
        cot = ([gf_ref[h] for h in range(NH)], [gb_ref[h] for h in range(NH)])
        dn1, dcsh, dcsc, dwk, dwv, dzf, dzb, def_, deb = vjp(cot)
        for h in range(NH):
            gw_ref[:, h * HD:(h + 1) * HD] = dwk[h]
            gw_ref[:, AW + h * HD:AW + (h + 1) * HD] = dwv[h]
        gv_ref[...] = jnp.zeros_like(gv_ref)
        gv_ref[0:1, :] = dn1
        gv_ref[1:2, :] = dcsh
        gv_ref[2:3, :] = dcsc
        for h in range(NH):
            gdm_ref[0, 0, h] = dzf[h]
            gdm_ref[0, 1, h] = def_[h]
            gdm_ref[1, 0, h] = dzb[h]
            gdm_ref[1, 1, h] = deb[h]

    return pl.pallas_call(
        body, name="ctx_backward",
        out_shape=[jax.ShapeDtypeStruct((D, 2 * AW), F32), jax.ShapeDtypeStruct((8, D), F32),
                   jax.ShapeDtypeStruct((2, 2, NH, CH, CH), F32)],
        in_specs=[pl.BlockSpec(memory_space=pltpu.VMEM)] * 6,
        out_specs=[pl.BlockSpec(memory_space=pltpu.VMEM)] * 3,
        compiler_params=_params(56),
    )(ctx, vecs, wi, dm, dscf, dscb)


def _in_proj(x, vecs, wi, gbufs):
    ln = x.shape[0]
    t = min(512, ln)
    nt = len(gbufs)
    steps = ln // t

    def body(x_ref, v_ref, wi_ref, *refs):
        z_ref, hx_ref = refs[nt:nt + 2]
        bufs = refs[nt + 2:2 * nt + 2]
        send_sems, recv_sems = refs[2 * nt + 2:]
        i = pl.program_id(0)

        @pl.when(i == 0)
        def _():
            for cp in _gather_ici_copies(bufs, send_sems, recv_sems)[0]:
                cp.start()

        xv = x_ref[...]
        hx = (xv * _rms(xv) * v_ref[0:1, :]) * (1.0 + v_ref[2:3, :]) + v_ref[1:2, :]
        hb = hx.astype(BF)
        hx_ref[...] = hb
        for j in range(NCHIP):
            z_ref[:, j * WI_C:(j + 1) * WI_C] = _dot(hb, wi_ref[j], NN)

        @pl.when(i == steps - 1)
        def _():
            out_cp, in_cp = _gather_ici_copies(bufs, send_sems, recv_sems)
            for cp in in_cp:
                cp.wait_recv()
            for cp in out_cp:
                cp.wait_send()

    hbm = pl.BlockSpec(memory_space=pl.ANY)
    out = pl.pallas_call(
        body, name="in_proj", grid=(steps,),
        in_specs=[pl.BlockSpec((t, D), lambda i: (i, 0)), _const((8, D)), _const((NCHIP, D, WI_C))] + [hbm] * nt,
        out_specs=[pl.BlockSpec((t, IN_COLS), lambda i: (i, 0)), pl.BlockSpec((t, D), lambda i: (i, 0))] + [hbm] * nt,
        out_shape=[jax.ShapeDtypeStruct((ln, IN_COLS), F32), jax.ShapeDtypeStruct((ln, D), BF)]
        + [jax.ShapeDtypeStruct(g.shape, g.dtype) for g in gbufs],
        input_output_aliases={3 + k: 2 + k for k in range(nt)},
        scratch_shapes=[pltpu.SemaphoreType.DMA(((NCHIP - 1) * nt,))] * 2,
        compiler_params=_params(56, ("arbitrary",)),
    )(x, vecs, wi, *gbufs)
    return out[0], out[1], out[2:]


def _allgather_copies(src, out, send_sems, recv_sems, local_sem):
    x, y, c = _pos()
    me = 4 * x + 2 * y + c
    sends, recvs = [], []
    for k in range(1, NDEV):
        kx, ky, kc = (k >> 2) & 1, (k >> 1) & 1, k & 1
        peer = (x ^ kx, y ^ ky, c ^ kc)
        frm = 4 * (x ^ kx) + 2 * (y ^ ky) + (c ^ kc)
        sends.append(pltpu.make_async_remote_copy(src_ref=src, dst_ref=out.at[me], send_sem=send_sems.at[k - 1],
                                                  recv_sem=recv_sems.at[k - 1], device_id=peer, device_id_type=MESH))
        recvs.append(pltpu.make_async_remote_copy(src_ref=src, dst_ref=out.at[frm], send_sem=send_sems.at[k - 1],
                                                  recv_sem=recv_sems.at[k - 1], device_id=peer, device_id_type=MESH))
    return sends, recvs, pltpu.make_async_copy(src, out.at[me], local_sem)


def _in_proj_bwd(dz, x, dx1, vecs, wi, parts, early):
    ln = x.shape[0]
    t = min(512, ln)
    nt = len(parts)
    steps = ln // t

    def body(dz_ref, x_ref, dx1_ref, v_ref, wi_ref, *refs):
        ps = refs[:nt]
        early_ref = refs[nt]
        gx_ref, acc_ref = refs[nt + 1:nt + 3]
        got = refs[nt + 3:2 * nt + 3]
        early_all = refs[2 * nt + 3]
        send_sems, recv_sems, ag_send, ag_recv, ag_local = refs[2 * nt + 4:]

        @pl.when(pl.program_id(0) == 0)
        def _():
            acc_ref[...] = jnp.zeros_like(acc_ref)
            for cp in _scatter_ici_copies(ps, got, send_sems, recv_sems):
                cp.start()
            sends, _, own = _allgather_copies(early_ref, early_all, ag_send, ag_recv, ag_local)
            own.start()
            for cp in sends:
                cp.start()

        dhx = jnp.zeros((t, D), F32)
        for j in range(NCHIP):
            dhx = dhx + _dot(dz_ref[:, j * WI_C:(j + 1) * WI_C], wi_ref[j], NT)
        xv = x_ref[...]
        r = _rms(xv)
        xn = xv * r
        n1, sc = v_ref[0:1, :], v_ref[2:3, :]
        acc_ref[0:1, :] += jnp.sum(dhx * xn * (1.0 + sc), axis=0, keepdims=True)
        acc_ref[1:2, :] += jnp.sum(dhx, axis=0, keepdims=True)
        acc_ref[2:3, :] += jnp.sum(dhx * xn * n1, axis=0, keepdims=True)
        g = dhx * n1 * (1.0 + sc)
        gx_ref[...] = dx1_ref[...] + r * (g - xn * jnp.mean(g * xn, axis=-1, keepdims=True))

        @pl.when(pl.program_id(0) == steps - 1)
        def _():
            cps = _scatter_ici_copies(ps, got, send_sems, recv_sems)
            sends, recvs, own = _allgather_copies(early_ref, early_all, ag_send, ag_recv, ag_local)
            for cp in cps + recvs:
                cp.wait_recv()
            for cp in cps + sends:
                cp.wait_send()
            own.wait()

    hbm = pl.BlockSpec(memory_space=pl.ANY)
    out = pl.pallas_call(
        body, name="in_proj_bwd", grid=(steps,),
        in_specs=[pl.BlockSpec((t, IN_COLS), lambda i: (i, 0)), pl.BlockSpec((t, D), lambda i: (i, 0)),
                  pl.BlockSpec((t, D), lambda i: (i, 0)), _const((8, D)), _const((NCHIP, D, WI_C))]
        + [hbm] * (nt + 1),
        out_specs=[pl.BlockSpec((t, D), lambda i: (i, 0)), pl.BlockSpec((8, D), lambda i: (0, 0))]
        + [hbm] * (nt + 1),
        out_shape=[jax.ShapeDtypeStruct((ln, D), F32), jax.ShapeDtypeStruct((8, D), F32)]
        + [jax.ShapeDtypeStruct((NCHIP - 1,) + p.shape[1:], BF) for p in parts]
        + [jax.ShapeDtypeStruct((NDEV,) + early.shape, F32)],
        scratch_shapes=[pltpu.SemaphoreType.DMA(((NCHIP - 1) * nt,))] * 2
        + [pltpu.SemaphoreType.DMA((NDEV - 1,))] * 2 + [pltpu.SemaphoreType.DMA],
        compiler_params=_params(56, ("arbitrary",)),
    )(dz, x, dx1, vecs, wi, *parts, early)
    return out[0], out[1], out[2:2 + nt], out[2 + nt]


def _post_mixer(x, ycat, tgt, vecs, wo, wg, wu, wd):
    ln = x.shape[0]
    t = min(256, ln)

    def body(x_ref, y_ref, t_ref, v_ref, wo_ref, wg_ref, wu_ref, wd_ref,
             dx1_ref, dyc_ref, h2_ref, dy_ref, df_ref, act_ref, da_ref, db_ref, acc_ref, a_st, b_st):
        @pl.when(pl.program_id(0) == 0)
        def _():
            acc_ref[...] = jnp.zeros_like(acc_ref)

        g1, n2, sh2, sc2 = v_ref[0:1, :], v_ref[1:2, :], v_ref[2:3, :], v_ref[3:4, :]
        g2, nf = v_ref[4:5, :], v_ref[5:6, :]
        y = _dot(y_ref[...], wo_ref[...], NN)
        x1 = x_ref[...] + g1 * y
        r2 = _rms(x1)
        xn2 = x1 * r2
        t2 = xn2 * n2
        h2b = (t2 * (1.0 + sc2) + sh2).astype(BF)
        h2_ref[...] = h2b
        a = _dot(h2b, wg_ref[...], NT)
        b = _dot(h2b, wu_ref[...], NT)
        a_st[...] = a
        b_st[...] = b
        act = (_silu(a) * b).astype(BF)
        act_ref[...] = act
        f = _dot(act, wd_ref[...], NN)
        x2 = x1 + g2 * f
        r3 = _rms(x2)
        xn3 = x2 * r3
        e = xn3 * nf - t_ref[...]
        acc_ref[6:7, :] += jnp.sum(e * e, axis=0, keepdims=True) * (0.5 / D)
        dout = e * (1.0 / D)
        acc_ref[5:6, :] += jnp.sum(dout * xn3, axis=0, keepdims=True)
        gg = dout * nf
        dx2 = r3 * (gg - xn3 * jnp.mean(gg * xn3, axis=-1, keepdims=True))
        acc_ref[4:5, :] += jnp.sum(dx2 * f, axis=0, keepdims=True)
        dfb = (g2 * dx2).astype(BF)
        df_ref[...] = dfb
        dact = _dot(dfb, wd_ref[...], NT)
        a = a_st[...]
        b = b_st[...]
        s = jax.nn.sigmoid(a)
        da = (dact * b * (s * (1.0 + a * (1.0 - s)))).astype(BF)
        db = (dact * (a * s)).astype(BF)
        da_ref[...] = da
        db_ref[...] = db
        dh2 = _dot(da, wg_ref[...], NN) + _dot(db, wu_ref[...], NN)
        acc_ref[2:3, :] += jnp.sum(dh2, axis=0, keepdims=True)
        acc_ref[3:4, :] += jnp.sum(dh2 * t2, axis=0, keepdims=True)
        acc_ref[1:2, :] += jnp.sum(dh2 * xn2 * (1.0 + sc2), axis=0, keepdims=True)
        gx = dh2 * n2 * (1.0 + sc2)
        dx1 = dx2 + r2 * (gx - xn2 * jnp.mean(gx * xn2, axis=-1, keepdims=True))
        dx1_ref[...] = dx1
        acc_ref[0:1, :] += jnp.sum(dx1 * y, axis=0, keepdims=True)
        dyb = (g1 * dx1).astype(BF)
        dy_ref[...] = dyb
        dyc_ref[...] = _dot(dyb, wo_ref[...], NT)

    tok = pl.BlockSpec((t, D), lambda i: (i, 0))
    ffb = pl.BlockSpec((t, DFF), lambda i: (i, 0))
    return pl.pallas_call(
        body, name="post_mixer", grid=(ln // t,),
        in_specs=[tok, tok, tok, _const((8, D)), _const((D, D)), _const((DFF, D)), _const((DFF, D)),
                  _const((DFF, D))],
        out_specs=[tok, tok, tok, tok, tok, ffb, ffb, ffb, pl.BlockSpec((16, D), lambda i: (0, 0))],
        out_shape=[jax.ShapeDtypeStruct((ln, D), F32)] * 2 + [jax.ShapeDtypeStruct((ln, D), BF)] * 3
        + [jax.ShapeDtypeStruct((ln, DFF), BF)] * 3 + [jax.ShapeDtypeStruct((16, D), F32)],
        scratch_shapes=[pltpu.VMEM((t, DFF), F32)] * 2,
        compiler_params=_params(60, ("arbitrary",)),
    )(x, ycat, tgt, vecs, wo, wg, wu, wd)


def _exchange_copies(g, out, send_sems, recv_sems):
    x, y, c = _pos()
    return [pltpu.make_async_remote_copy(src_ref=g.at[j, 1 - c], dst_ref=out.at[j], send_sem=send_sems.at[j],
                                         recv_sem=recv_sems.at[j], device_id=(x, y, 1 - c), device_id_type=MESH)
            for j in range(NCHIP)]


def _tn_matmul(xa, dy, name, nb, k1, n, x_batched, dy_mode, tt, ctx_kv=None, carry=None):
    ln = xa.shape[-2]
    tt = min(tt, ln)
    steps = ln // tt
    n_in = 2 + (ctx_kv is not None) + (carry is not None)

    def body(x_ref, dy_ref, *refs):
        o_ref = refs[n_in - 2]
        if carry is not None:
            g_ref, got_ref = refs[n_in - 3], refs[n_in - 1]
            send_sems, recv_sems = refs[n_in:]

        @pl.when(pl.program_id(0) == 0)
        def _():
            if carry is not None:
                for cp in _exchange_copies(g_ref, got_ref, send_sems, recv_sems):
                    cp.start()
            o_ref[...] = jnp.zeros_like(o_ref)
            if ctx_kv is not None:
                for g in range(0, 2 * AW, HD):
                    col = 3 * AW + g
                    o_ref[col // n, :, col % n: col % n + HD] = refs[0][:, g:g + HD]

        xt = None if x_batched else jnp.transpose(x_ref[...])
        for b in range(nb):
            lhs = jnp.transpose(x_ref[b]) if x_batched else xt
            if dy_mode == "batched":
                rhs = dy_ref[b]
            elif dy_mode == "cols":
                rhs = dy_ref[:, b * n:(b + 1) * n]
            else:
                rhs = dy_ref[...]
            o_ref[b] += _dot(lhs, rhs, NN)

        if carry is not None:
            @pl.when(pl.program_id(0) == steps - 1)
            def _():
                cps = _exchange_copies(g_ref, got_ref, send_sems, recv_sems)
                for cp in cps:
                    cp.wait_recv()
                for cp in cps:
                    cp.wait_send()

    x_spec = (pl.BlockSpec((nb, tt, k1), lambda t: (0, t, 0)) if x_batched
              else pl.BlockSpec((tt, k1), lambda t: (t, 0)))
    if dy_mode == "batched":
        dy_spec = pl.BlockSpec((nb, tt, n), lambda t: (0, t, 0))
    elif dy_mode == "cols":
        dy_spec = pl.BlockSpec((tt, nb * n), lambda t: (t, 0))
    else:
        dy_spec = pl.BlockSpec((tt, n), lambda t: (t, 0))
    hbm = pl.BlockSpec(memory_space=pl.ANY)
    extra = [] if ctx_kv is None else [ctx_kv]
    in_specs = [x_spec, dy_spec] + [_const(e.shape) for e in extra]
    out_specs = [pl.BlockSpec((nb, k1, n), lambda t: (0, 0, 0))]
    out_shape = [jax.ShapeDtypeStruct((nb, k1, n), F32)]
    scratch = []
    if carry is not None:
        extra = extra + [carry]
        in_specs.append(hbm)
        out_specs.append(hbm)
        out_shape.append(jax.ShapeDtypeStruct((NCHIP,) + carry.shape[2:], F32))
        scratch = [pltpu.SemaphoreType.DMA((NCHIP,))] * 2
    out = pl.pallas_call(
        body, name=name, grid=(steps,),
        in_specs=in_specs, out_specs=out_specs, out_shape=out_shape, scratch_shapes=scratch,
        compiler_params=_params(60, ("arbitrary",)),
    )(xa, dy, *extra)
    return out[0] if carry is None else (out[0], out[1])


def _add_ctx_cols(gwi, gwkv):
    first = 1536 // HD
    per = WI_C // HD

    def body(g_ref, a_ref, o_ref):
        o_ref[...] = g_ref[...] + a_ref[...]

    spec = pl.BlockSpec((None, D, HD), lambda i: ((first + i) // per, 0, (first + i) % per))
    return pl.pallas_call(
        body, name="add_ctx_cols", grid=(2 * AW // HD,),
        in_specs=[spec, pl.BlockSpec((D, HD), lambda i: (0, i))],
        out_specs=spec,
        out_shape=jax.ShapeDtypeStruct(gwi.shape, F32),
        input_output_aliases={0: 0},
        compiler_params=_params(32, ("arbitrary",)),
    )(gwi, gwkv)


FWD_CHUNKS_PER_STEP = 4
BWD_CHUNKS_PER_STEP = 4


def _chunks_per_step(nc, want):
    return want if nc % want == 0 else 1


def _mixer_fwd(z, cos_t, sin_t, dm, sgw, gain, bfull, scf, scb, gbufs_a, gbufs_b):
    ln = z.shape[0]
    nc = ln // CH
    na = len(gbufs_a)
    gbufs = list(gbufs_a) + list(gbufs_b)
    nt = len(gbufs)
    cps = _chunks_per_step(nc, FWD_CHUNKS_PER_STEP)
    nb = nc // cps
    rows = cps * CH
    mid = nb // 2

    def rev(p, n):
        return p * n + (1 - p) * (nb - 1 - n)

    def col(j, both):
        if both:
            return pl.BlockSpec((rows, AW), lambda p, n: (rev(p, n), j))
        return pl.BlockSpec((rows, AW), lambda p, n: (p * n, j))

    def body(u_ref, v_ref, q_ref, k_ref, vr_ref, gf_ref, gb_ref, cos_ref, sin_ref, dm_ref, sgw_ref, gain_ref,
             bfull_ref, scf_ref, scb_ref, *refs):
        y_ref, sf_ref, sb_ref = refs[nt:nt + 3]
        bufs = refs[nt + 3:2 * nt + 3]
        bufs_a, bufs_b = bufs[:na], bufs[na:]
        sb_all, st, a_send, a_recv, bi_send, bi_recv, bd_send, bd_recv = refs[2 * nt + 3:]
        p, n = pl.program_id(0), pl.program_id(1)

        @pl.when((p == 0) & (n == 0))
        def _():
            for cp in _gather_d2d_copies(bufs_a, a_send, a_recv)[0]:
                cp.start()
            for cp in _gather_ici_copies(bufs_b, bi_send, bi_recv)[0]:
                cp.start()

        @pl.when((p == 1) & (n == mid))
        def _():
            for cp in _gather_ici_copies(bufs_b, bi_send, bi_recv)[1]:
                cp.wait_recv()
            for cp in _gather_d2d_copies(bufs_b, bd_send, bd_recv)[0]:
                cp.start()

        def roped_k(r0):
            cos, sin = cos_ref[r0:r0 + CH, :], sin_ref[r0:r0 + CH, :]
            return [_rope(t, cos, sin) * K_SCALE for t in _heads(k_ref, r0)]

        @pl.when(p == 0)
        def _():
            @pl.when(n == 0)
            def _():
                st[...] = scb_ref[...]

            for s in reversed(range(cps)):
                m = (nb - 1 - n) * cps + s
                k, vr = roped_k(s * CH), _heads(vr_ref, s * CH)
                for h in range(NH):
                    sb_all[m, h] = st[h]
                    st[h] = dm_ref[1, 3, h] * st[h] + _mm_tn(k[h], dm_ref[1, 2, h] * vr[h])

        @pl.when(p == 1)
        def _():
            @pl.when(n == 0)
            def _():
                st[...] = scf_ref[...]

            mats = [[dm_ref[d, kind, h] for h in range(NH)] for d in range(2) for kind in range(3)]
            for s in range(cps):
                r0 = s * CH
                m = n * cps + s
                cos, sin = cos_ref[r0:r0 + CH, :], sin_ref[r0:r0 + CH, :]
                q = [_rope(t, cos, sin) for t in _heads(q_ref, r0)]
                sf = [st[h] for h in range(NH)]
                sb = [sb_all[m, h] for h in range(NH)]
                ya, yr, uf, _ = _chunk_fwd(
                    _heads(u_ref, r0), _heads(v_ref, r0), q, roped_k(r0), _heads(vr_ref, r0), _heads(gf_ref, r0),
                    _heads(gb_ref, r0), sf, sb, mats[0], mats[1], mats[2], mats[3], mats[4], mats[5],
                    [sgw_ref[g] for g in range(NH)], [gain_ref[:, g * HD:(g + 1) * HD] for g in range(NH)],
                    [bfull_ref[g] for g in range(NH)])
                for h in range(NH):
                    y_ref[r0:r0 + CH, h * HD:(h + 1) * HD] = ya[h].astype(BF)
                    y_ref[r0:r0 + CH, AW + h * HD:AW + (h + 1) * HD] = yr[h].astype(BF)
                    sf_ref[s, h] = sf[h]
                    sb_ref[s, h] = sb[h]
                    st[h] = dm_ref[0, 3, h] * st[h] + uf[h]

        @pl.when((p == 1) & (n == nb - 1))
        def _():
            a_out, a_in = _gather_d2d_copies(bufs_a, a_send, a_recv)
            b_out, b_in = _gather_d2d_copies(bufs_b, bd_send, bd_recv)
            for cp in a_in + b_in:
                cp.wait_recv()
            for cp in a_out + b_out + _gather_ici_copies(bufs_b, bi_send, bi_recv)[0]:
                cp.wait_send()

    hbm = pl.BlockSpec(memory_space=pl.ANY)
    tab = pl.BlockSpec((rows, HD), lambda p, n: (rev(p, n), 0))
    st_spec = pl.BlockSpec((cps, NH, HD, HD), lambda p, n: (p * n, 0, 0, 0))
    out = pl.pallas_call(
        body, name="mixer_fwd", grid=(2, nb),
        in_specs=[col(0, False), col(1, False), col(2, False), col(3, True), col(4, True), col(5, False),
                  col(6, False), tab, tab, _const((2, 4, NH, CH, CH)), _const((NH, CH, CH)), _const((1, AW)),
                  _const((NH, CH, CH)), _const((NH, HD, HD)), _const((NH, HD, HD))] + [hbm] * nt,
        out_specs=[pl.BlockSpec((rows, D), lambda p, n: (p * n, 0)), st_spec, st_spec] + [hbm] * nt,
        out_shape=[jax.ShapeDtypeStruct((ln, D), BF), jax.ShapeDtypeStruct((nc, NH, HD, HD), F32),
                   jax.ShapeDtypeStruct((nc, NH, HD, HD), F32)]
        + [jax.ShapeDtypeStruct(g.shape, g.dtype) for g in gbufs],
        input_output_aliases={15 + k: 3 + k for k in range(nt)},
        scratch_shapes=[pltpu.VMEM((nc, NH, HD, HD), F32), pltpu.VMEM((NH, HD, HD), F32)]
        + [pltpu.SemaphoreType.DMA(((NCHIP - 1) * na,))] * 2
        + [pltpu.SemaphoreType.DMA(((NCHIP - 1) * (nt - na),))] * 4,
        compiler_params=_params(56, ("arbitrary", "arbitrary")),
    )(z, z, z, z, z, z, z, cos_t, sin_t, dm, sgw, gain, bfull, scf, scb, *gbufs)
    return out[0], out[1], out[2], out[3:]


def _mixer_bwd(z, dycat, cos_t, sin_t, dm, sgw, gain, bfull, sf_all, sb_all, parts):
    ln = z.shape[0]
    nc = ln // CH
    cps = _chunks_per_step(nc, BWD_CHUNKS_PER_STEP)
    nb = nc // cps
    rows = cps * CH

    def rev(p, n):
        return p * n + (1 - p) * (nb - 1 - n)

    def col(j, both):
        if both:
            return pl.BlockSpec((rows, AW), lambda p, n: (rev(p, n), j))
        return pl.BlockSpec((rows, AW), lambda p, n: (p * n, j))

    nt = len(parts)

    def body(u_ref, v_ref, q_ref, k_ref, vr_ref, gf_ref, gb_ref, dya_ref, dyr_ref, cos_ref, sin_ref, dm_ref,
             sgw_ref, gain_ref, bfull_ref, sf_ref, sb_ref, *refs):
        ps = refs[:nt]
        dz_ref, ddm_ref, dsgw_ref, dgain_ref, dbf_ref, dscf_ref, dscb_ref = refs[nt:nt + 7]
        got = refs[nt + 7:2 * nt + 7]
        gf_all, run, send_sems, recv_sems = refs[2 * nt + 7:]
        p, n = pl.program_id(0), pl.program_id(1)

        @pl.when((p == 0) & (n == 0))
        def _():
            for cp in _scatter_ici_copies(ps, got, send_sems, recv_sems):
                cp.start()

        mats = [[dm_ref[d, kind, h] for h in range(NH)] for d in range(2) for kind in range(3)]

        def chunk_inputs(s):
            r0 = s * CH
            cos, sin = cos_ref[r0:r0 + CH, :], sin_ref[r0:r0 + CH, :]
            q = [_rope(t, cos, sin) for t in _heads(q_ref, r0)]
            k = [_rope(t, cos, sin) * K_SCALE for t in _heads(k_ref, r0)]
            sf = [sf_ref[s, h] for h in range(NH)]
            return cos, sin, q, k, _heads(vr_ref, r0), _heads(gf_ref, r0), _heads(dyr_ref, r0), sf

        @pl.when(p == 0)
        def _():
            @pl.when(n == 0)
            def _():
                run[...] = jnp.zeros_like(run)
                ddm_ref[...] = jnp.zeros_like(ddm_ref)
                dsgw_ref[...] = jnp.zeros_like(dsgw_ref)
                dgain_ref[...] = jnp.zeros_like(dgain_ref)
                dbf_ref[...] = jnp.zeros_like(dbf_ref)

            for s in reversed(range(cps)):
                m = (nb - 1 - n) * cps + s
                _, _, q, k, vr, gf, dyr, sf = chunk_inputs(s)
                _, vjp = jax.vjp(lambda st: _fwd_dir_only(st, q, k, vr, gf, mats[0], mats[1]), sf)
                (dsf,) = vjp(dyr)
                for h in range(NH):
                    g_next = run[h]
                    gf_all[m, h] = g_next.astype(BF)
                    ddm_ref[0, 3, h] += sf[h] * g_next
                    run[h] = dsf[h] + dm_ref[0, 3, h] * g_next

            @pl.when(n == nb - 1)
            def _():
                dscf_ref[...] = run[...]

        @pl.when(p == 1)
        def _():
            @pl.when(n == 0)
            def _():
                run[...] = jnp.zeros_like(run)

            for s in range(cps):
                r0 = s * CH
                m = n * cps + s
                cos, sin, q, k, vr, gf, dyr, sf = chunk_inputs(s)
                sb = [sb_ref[s, h] for h in range(NH)]
                g_f = [gf_all[m, h].astype(F32) for h in range(NH)]
                g_b = [run[h] for h in range(NH)]
                args = (_heads(u_ref, r0), _heads(v_ref, r0), q, k, vr, gf, _heads(gb_ref, r0), sb,
                        mats[0], mats[1], mats[2], mats[3], mats[4], mats[5],
                        [sgw_ref[g] for g in range(NH)], [gain_ref[:, g * HD:(g + 1) * HD] for g in range(NH)],
                        [bfull_ref[g] for g in range(NH)])

                def fn(u_, v_, q_, k_, vr_, gf_, gb_, sb_, df, xf, zf, db, xb, zb, sgw_, gain_, bfull_, sf=sf):
                    return _chunk_fwd(u_, v_, q_, k_, vr_, gf_, gb_, sf, sb_, df, xf, zf, db, xb, zb, sgw_,
                                      gain_, bfull_)

                _, vjp = jax.vjp(fn, *args)
                (du, dv, dq, dk, dvr, dgf, dgb, dsb, ddf, dxf, dzf, ddb, dxb, dzb, dsgw, dgain, dbf) = vjp(
                    (_heads(dya_ref, r0), dyr, g_f, g_b))
                rw = slice(r0, r0 + CH)
                for h in range(NH):
                    cs = slice(h * HD, (h + 1) * HD)
                    dz_ref[rw, h * HD:(h + 1) * HD] = du[h].astype(BF)
                    dz_ref[rw, AW + h * HD:AW + (h + 1) * HD] = dv[h].astype(BF)
                    dz_ref[rw, 2 * AW + h * HD:2 * AW + (h + 1) * HD] = _rope_bwd(dq[h], cos, sin).astype(BF)
                    dz_ref[rw, 3 * AW + h * HD:3 * AW + (h + 1) * HD] = _rope_bwd(dk[h] * K_SCALE, cos,
                                                                                  sin).astype(BF)
                    dz_ref[rw, 4 * AW + h * HD:4 * AW + (h + 1) * HD] = dvr[h].astype(BF)
                    dz_ref[rw, 5 * AW + h * HD:5 * AW + (h + 1) * HD] = dgf[h].astype(BF)
                    dz_ref[rw, 6 * AW + h * HD:6 * AW + (h + 1) * HD] = dgb[h].astype(BF)
                    ddm_ref[0, 0, h] += ddf[h]
                    ddm_ref[0, 1, h] += dxf[h]
                    ddm_ref[0, 2, h] += dzf[h]
                    ddm_ref[1, 0, h] += ddb[h]
                    ddm_ref[1, 1, h] += dxb[h]
                    ddm_ref[1, 2, h] += dzb[h]
                    ddm_ref[1, 3, h] += sb[h] * g_b[h]
                    dsgw_ref[h] += dsgw[h]
                    dgain_ref[:, cs] += dgain[h]
                    dbf_ref[h] += dbf[h]
                    run[h] = dsb[h] + dm_ref[1, 3, h] * run[h]

            @pl.when(n == nb - 1)
            def _():
                dscb_ref[...] = run[...]

        @pl.when((p == 1) & (n == nb - 1))
        def _():
            cps_ = _scatter_ici_copies(ps, got, send_sems, recv_sems)
            for cp in cps_:
                cp.wait_recv()
            for cp in cps_:
                cp.wait_send()

    hbm = pl.BlockSpec(memory_space=pl.ANY)
    tab = pl.BlockSpec((rows, HD), lambda p, n: (rev(p, n), 0))
    tile4 = jax.ShapeDtypeStruct((NH, CH, CH), F32)
    out = pl.pallas_call(
        body, name="mixer_bwd", grid=(2, nb),
        in_specs=[col(0, False), col(1, False), col(2, True), col(3, True), col(4, True), col(5, True),
                  col(6, False),
                  pl.BlockSpec((rows, AW), lambda p, n: (p * n, 0)),
                  pl.BlockSpec((rows, AW), lambda p, n: (rev(p, n), 1)),
                  tab, tab, _const((2, 4, NH, CH, CH)), _const((NH, CH, CH)), _const((1, AW)),
                  _const((NH, CH, CH)),
                  pl.BlockSpec((cps, NH, HD, HD), lambda p, n: (rev(p, n), 0, 0, 0)),
                  pl.BlockSpec((cps, NH, HD, HD), lambda p, n: (p * n, 0, 0, 0))] + [hbm] * nt,
        out_specs=[pl.BlockSpec((rows, IN_COLS), lambda p, n: (p * n, 0)),
                   pl.BlockSpec((2, 4, NH, CH, CH), lambda p, n: (0, 0, 0, 0, 0)),
                   pl.BlockSpec((NH, CH, CH), lambda p, n: (0, 0, 0)),
                   pl.BlockSpec((1, AW), lambda p, n: (0, 0)),
                   pl.BlockSpec((NH, CH, CH), lambda p, n: (0, 0, 0)),
                   pl.BlockSpec((NH, HD, HD), lambda p, n: (0, 0, 0)),
                   pl.BlockSpec((NH, HD, HD), lambda p, n: (0, 0, 0))] + [hbm] * nt,
        out_shape=[jax.ShapeDtypeStruct((ln, IN_COLS), BF), jax.ShapeDtypeStruct((2, 4, NH, CH, CH), F32),
                   tile4, jax.ShapeDtypeStruct((1, AW), F32), tile4, tile4, tile4]
        + [jax.ShapeDtypeStruct((NCHIP - 1,) + p.shape[1:], BF) for p in parts],
        scratch_shapes=[pltpu.VMEM((nc, NH, HD, HD), BF), pltpu.VMEM((NH, HD, HD), F32)]
        + [pltpu.SemaphoreType.DMA(((NCHIP - 1) * nt,))] * 2,
        compiler_params=_params(60, ("arbitrary", "arbitrary")),
    )(z, z, z, z, z, z, z, dycat, dycat, cos_t, sin_t, dm, sgw, gain, bfull, sf_all, sb_all, *parts)
    return out[:7], out[7:]


def _small_reduce(ddm, ddm_ctx, dm, dbf):
    def body(ddm_ref, dctx_ref, dm_ref, dbf_ref, lg_ref, sgb_ref):
        ex = _decay_exponents()
        ones = jnp.ones((8, CH), F32)
        for d in range(2):
            for h in range(NH):
                tot = jnp.zeros((CH, CH), F32)
                for kind in range(4):
                    g = ddm_ref[d, kind, h]
                    if kind >= 2:
                        g = g + dctx_ref[d, kind - 2, h]
                    tot = tot + g * dm_ref[d, kind, h] * ex[d][kind]
                lg_ref[d * NH + h: d * NH + h + 1, :] = jnp.sum(tot, axis=0, keepdims=True)
        sgb_ref[...] = jnp.zeros_like(sgb_ref)
        for g in range(NH):
            r = lax.dot_general(ones, dbf_ref[g], (NT, ((), ())), precision=HI, preferred_element_type=F32)
            sgb_ref[g:g + 1, :] = r[0:1, :]

    return pl.pallas_call(
        body, name="small_reduce",
        out_shape=[jax.ShapeDtypeStruct((8, CH), F32), jax.ShapeDtypeStruct((8, CH), F32)],
        in_specs=[pl.BlockSpec(memory_space=pltpu.VMEM)] * 4,
        out_specs=[pl.BlockSpec(memory_space=pltpu.VMEM)] * 2,
        compiler_params=_params(32),
    )(ddm, ddm_ctx, dm, dbf)


def _mod_backward(ct_pad_t, cctx_col, dmod_pad, dcmod_cols, w_mod_s):
    def body(ct_ref, cc_ref, dm_ref, dc_ref, w_ref, gw_ref, part_ref):
        dcm = dc_ref[0:1, :]
        for d in range(1, NDEV):
            dcm = dcm + dc_ref[d:d + 1, :]
        gw_ref[...] = (jnp.dot(_silu(ct_ref[...]), dm_ref[...], precision=HI, preferred_element_type=F32)
                       + _silu(cc_ref[...]) * dcm)
        part_ref[...] = lax.dot_general(jnp.broadcast_to(dcm, (8, dcm.shape[1])), w_ref[...], (NT, ((), ())),
                                        precision=HI, preferred_element_type=F32)

    return pl.pallas_call(
        body, name="mod_backward",
        out_shape=[jax.ShapeDtypeStruct(w_mod_s.shape, F32), jax.ShapeDtypeStruct((8, D), F32)],
        in_specs=[pl.BlockSpec(memory_space=pltpu.VMEM)] * 5,
        out_specs=[pl.BlockSpec(memory_space=pltpu.VMEM)] * 2,
        compiler_params=_params(48),
    )(ct_pad_t, cctx_col, dmod_pad, dcmod_cols, w_mod_s)


def _cctx_update(parts, c_ctx, m, v):
    def body(p_ref, c_ref, m_ref, v_ref, g_ref, d_ref, mo_ref, vo_ref):
        tot = ((p_ref[0] + p_ref[2]) + p_ref[4]) + p_ref[6]
        cv = c_ref[...]
        s = jax.nn.sigmoid(cv)
        g = tot * (s * (1.0 + cv * (1.0 - s)))
        g_ref[...] = g
        d_ref[...], mo_ref[...], vo_ref[...] = _adamw_math(cv, g, m_ref[...], v_ref[...])

    return pl.pallas_call(
        body, name="cctx_update",
        out_shape=[jax.ShapeDtypeStruct((1, D), F32)] * 4,
        in_specs=[pl.BlockSpec(memory_space=pltpu.VMEM)] * 4,
        out_specs=[pl.BlockSpec(memory_space=pltpu.VMEM)] * 4,
        compiler_params=_params(16),
    )(parts, c_ctx, m, v)


def _small_update(gathered, wp, mp, vp):
    def body(g_ref, w_ref, m_ref, v_ref, go_ref, d_ref, mo_ref, vo_ref, loss_ref):
        tot = g_ref[0]
        for d in range(1, NDEV):
            tot = tot + g_ref[d]
        go_ref[Q_BMOD:Q_N1, :] = tot[P_DMOD:P_N1, :] + tot[P_DCMOD:P_DMOD, :]
        go_ref[Q_N1:Q_LG, :] = tot[P_N1:P_LG, :]
        lg = jnp.sum(tot[P_LG:P_N2, :], axis=1, keepdims=True)
        go_ref[Q_LG:Q_N2, :] = lg * jax.nn.sigmoid(-w_ref[Q_LG:Q_N2, :])
        go_ref[Q_N2:Q_ROWS, :] = tot[P_N2:P_LOSS, :]
        d_ref[...], mo_ref[...], vo_ref[...] = _adamw_math(w_ref[...], go_ref[...], m_ref[...], v_ref[...])
        ls = jnp.sum(jnp.sum(tot[P_LOSS:P_ROWS, :], axis=1, keepdims=True), axis=0, keepdims=True)
        loss_ref[...] = jnp.broadcast_to(ls, (8, CH))

    return pl.pallas_call(
        body, name="small_update",
        out_shape=[jax.ShapeDtypeStruct((Q_ROWS, CH), F32)] * 4 + [jax.ShapeDtypeStruct((8, CH), F32)],
        in_specs=[pl.BlockSpec(memory_space=pltpu.VMEM)] * 4,
        out_specs=[pl.BlockSpec(memory_space=pltpu.VMEM)] * 5,
        compiler_params=_params(32),
    )(gathered, wp, mp, vp)


def _rows(a):
    r = a.reshape(-1, CH)
    return jnp.pad(r, ((0, -r.shape[0] % 8), (0, 0)))


def _pack_small(b_mod, norm1, sg_gain, sg_w, sg_b, lf, lb, norm2, norm_f):
    lg = jnp.broadcast_to(jnp.concatenate([lf.reshape(NH), lb.reshape(NH)])[:, None], (2 * NH, CH))
    return jnp.concatenate([_rows(b_mod), _rows(norm1), _rows(sg_gain), _rows(sg_w), _rows(sg_b), lg,
                            _rows(norm2), _rows(norm_f)], axis=0)


def _unpack_small(p):
    return (p[Q_BMOD:Q_N1].reshape(1, 6 * D), p[Q_N1:Q_GAIN].reshape(1, D), p[Q_GAIN:Q_GAIN + NH].reshape(1, AW),
            p[Q_SGW:Q_SGB].reshape(1, NH, CH, CH), p[Q_SGB:Q_SGB + NH].reshape(1, NH, CH),
            p[Q_LG:Q_LG + NH, 0].reshape(1, NH), p[Q_LG + NH:Q_N2, 0].reshape(1, NH),
            p[Q_N2:Q_NF].reshape(1, D), p[Q_NF:Q_ROWS].reshape(D))


def _rope_tables(ln):
    pos = np.arange(ln)
    rows = (pos // GRID_W).astype(np.float32)
    cols = (pos % GRID_W).astype(np.float32)
    n_freq = HD // 4
    inv = (np.float32(ROPE_BASE) ** (-np.arange(n_freq, dtype=np.float32) / np.float32(n_freq))).astype(np.float32)
    ar = rows[:, None] * inv[None, :]
    ac = cols[:, None] * inv[None, :]
    cos_t = np.concatenate([np.cos(ar), np.cos(ar), np.cos(ac), np.cos(ac)], axis=1).astype(np.float32)
    sin_t = np.concatenate([-np.sin(ar), np.sin(ar), -np.sin(ac), np.sin(ac)], axis=1).astype(np.float32)
    return jnp.asarray(cos_t), jnp.asarray(sin_t)


def kernel(x, c, ctx, c_ctx, w_mod, b_mod, norm1, w_in, sg_gain, sg_w, sg_b, ret_logit_f, ret_logit_b, w_out, norm2, w_gate, w_up, w_down, norm_f, loss_target, m_c_ctx, m_w_mod, m_b_mod, m_norm1, m_w_in, m_sg_gain, m_sg_w, m_sg_b, m_ret_logit_f, m_ret_logit_b, m_w_out, m_norm2, m_w_gate, m_w_up, m_w_down, m_norm_f, v_c_ctx, v_w_mod, v_b_mod, v_norm1, v_w_in, v_sg_gain, v_sg_w, v_sg_b, v_ret_logit_f, v_ret_logit_b, v_w_out, v_norm2, v_w_gate, v_w_up, v_w_down, v_norm_f):
    ln = x.shape[1]
    xi, yi, ci = _pos()
    chip = 2 * xi + yi
    me = 4 * xi + 2 * yi + ci
    x2d = x.reshape(ln, D)
    tgt = loss_target.reshape(ln, D)
    mod_c = w_mod.shape[2]

    tr = lambda a: jnp.swapaxes(a[0], 0, 1)
    gbufs, c_all, prod_all = _prologue(c, c_ctx.reshape(1, D), w_mod[0],
                                       [w_in[0], w_out[0], tr(w_gate), tr(w_up), w_down[0]])
    wi = gbufs[0].reshape(NCHIP, D, WI_C)
    gbufs_a, gbufs_b = gbufs[1:3], gbufs[3:5]
    c_all = c_all.reshape(NDEV, D)
    prod_chips = prod_all[0::2]
    mod_rows = jnp.transpose(prod_chips, (1, 0, 2)).reshape(16, NCHIP * mod_c) + b_mod
    mod = lax.dynamic_slice_in_dim(mod_rows, me, 1, axis=0)
    cmod = mod_rows[8:9]
    sh1, sc1, g1, sh2, sc2, g2 = [mod[:, i * D:(i + 1) * D] for i in range(6)]
    csh1, csc1 = cmod[:, 0:D], cmod[:, D:2 * D]
    zrow = jnp.zeros((1, D), F32)
    vec_in = jnp.concatenate([norm1, sh1, sc1] + [zrow] * 5, axis=0)
    vec_ctx = jnp.concatenate([norm1, csh1, csc1] + [zrow] * 5, axis=0)
    vec_post = jnp.concatenate([g1, norm2, sh2, sc2, g2, norm_f.reshape(1, D), zrow, zrow], axis=0)

    logits = jnp.concatenate([ret_logit_f.reshape(NH), ret_logit_b.reshape(NH)])
    dm = _decay_mats(jnp.broadcast_to(logits[:, None, None], (2 * NH, CH, CH)))
    ctx2d = ctx.reshape(ctx.shape[1], D)
    scf, scb = _ctx_forward(ctx2d, vec_ctx, wi, dm)

    cos_t, sin_t = _rope_tables(ln)
    z, hx, gbufs_a = _in_proj(x2d, vec_in, wi, gbufs_a)
    bfull = jnp.broadcast_to(sg_b[0][:, :, None], (NH, CH, CH))
    ycat, sf_all, sb_all, gbufs = _mixer_fwd(z, cos_t, sin_t, dm, sg_w[0], sg_gain, bfull, scf, scb,
                                             gbufs_a, gbufs_b)
    wo, wg_t, wu_t, wd = [g.reshape(-1, D) for g in gbufs]

    dx1, dycat, h2, dy, df, act, da, db, acc_post = _post_mixer(x2d, ycat, tgt, vec_post, wo, wg_t, wu_t, wd)

    cidx = ci.reshape(1).astype(jnp.int32)
    chipidx = chip.reshape(1).astype(jnp.int32)

    def halves_summed(full, names):
        full = [g.reshape(NCHIP, 2, g.shape[1] // 2, g.shape[2]) for g in full]
        from_sib = _rs_exchange_halves(full, "rs_exchange_" + names[0])
        return [_rs_add_halves(g, r, cidx, "rs_add_halves_" + nm) for g, r, nm in zip(full, from_sib, names)]

    def split(g):
        return g.reshape(NCHIP, 2, g.shape[1] // (2 * NCHIP), g.shape[2])

    g_wd = split(_tn_matmul(act, df, "grad_w_down", 1, DFF, D, False, "shared", 1024))
    g_wu, x_wd = _tn_matmul(db, h2, "grad_w_up", 1, DFF, D, False, "shared", 1024, carry=g_wd)
    g_wu = split(g_wu)
    g_wg, x_wu = _tn_matmul(da, h2, "grad_w_gate", 1, DFF, D, False, "shared", 1024, carry=g_wu)
    g_wg = split(g_wg)
    g_wo, x_wg = _tn_matmul(ycat, dy, "grad_w_out", 1, D, D, False, "shared", 1024, carry=g_wg)
    g_wo = split(g_wo)
    x_wo = _rs_exchange_halves([g_wo], "rs_exchange_w_out")[0]
    names = ["w_in", "w_out", "w_gate", "w_up", "w_down"]
    sums_b = [_rs_add_halves(g, r, cidx, "rs_add_halves_" + nm)
              for g, r, nm in zip([g_wo, g_wg, g_wu, g_wd], [x_wo, x_wg, x_wu, x_wd], names[1:])]

    (dz, ddm, dsgw, dgain, dbf, dscf, dscb), from_chips_b = _mixer_bwd(
        z, dycat, cos_t, sin_t, dm, sg_w[0], sg_gain, bfull, sf_all, sb_all, [s[1] for s in sums_b])
    gwkv, acc_ctx, ddm_ctx = _ctx_backward(ctx2d, vec_ctx, wi, dm, dscf, dscb)
    g_wi = _tn_matmul(hx, dz, "grad_w_in", NCHIP, D, WI_C, False, "cols", 512, ctx_kv=gwkv)
    sums_a = halves_summed([g_wi], names[:1])
    lg_part, dsgb = _small_reduce(ddm, ddm_ctx, dm, dbf)
    dcmod = jnp.concatenate([acc_ctx[1:2], acc_ctx[2:3], jnp.zeros((1, 4 * D), F32)], axis=1)
    dmod_rest = jnp.concatenate([acc_post[0:1], acc_post[2:3], acc_post[3:4], acc_post[4:5]], axis=1)
    early = jnp.concatenate([_rows(dcmod), _rows(dmod_rest), _rows(dgain), _rows(dsgw), dsgb, lg_part,
                             _rows(acc_post[1:2]), _rows(acc_post[5:6]), _rows(acc_post[6:7])], axis=0)
    gx, acc_in, from_chips_a, early_all = _in_proj_bwd(dz, x2d, dx1, vec_in, wi, [s[1] for s in sums_a], early)

    sums = sums_a + sums_b
    from_chips = list(from_chips_a) + list(from_chips_b)
    finals = [_rs_add_chips(s[0], r, chipidx, "rs_add_chips_" + nm) for s, r, nm in zip(sums, from_chips, names)]
    others = _rs_share_final(finals)

    late = jnp.concatenate([_rows(acc_in[1:2]), _rows(acc_in[2:3]), _rows(acc_in[0:1] + acc_ctx[0:1])], axis=0)
    late_all = _allgather_small(late, "gather_small")
    n_dc, n_l = P_DMOD - P_DCMOD, 16
    gathered = jnp.concatenate([early_all[:, :n_dc], late_all[:, :n_l], early_all[:, n_dc:n_dc + 32],
                                late_all[:, n_l:], early_all[:, n_dc + 32:]], axis=1)
    dmod_all = gathered[:, P_DMOD:P_N1].reshape(NDEV, 6 * D)
    dcmod_all = gathered[:, P_DCMOD:P_DMOD].reshape(NDEV, 6 * D)
    dmod_cols = lax.dynamic_slice_in_dim(dmod_all, chip * mod_c, mod_c, axis=1)
    dcmod_cols = lax.dynamic_slice_in_dim(dcmod_all, chip * mod_c, mod_c, axis=1)
    dmod_pad = jnp.concatenate([dmod_cols, jnp.zeros((CH - NDEV, mod_c), F32)], axis=0)
    ct_pad_t = jnp.concatenate([jnp.transpose(c_all), jnp.zeros((D, CH - NDEV), F32)], axis=1)
    g_wmod, cctx_part = _mod_backward(ct_pad_t, c_ctx.reshape(D, 1), dmod_pad, dcmod_cols, w_mod[0])
    parts = _allgather_small(cctx_part[0:1], "gather_cctx")
    g_cctx, d_cctx, nm_cctx, nv_cctx = _cctx_update(parts, c_ctx.reshape(1, D), m_c_ctx.reshape(1, D),
                                                    v_c_ctx.reshape(1, D))

    wp = _pack_small(b_mod, norm1, sg_gain, sg_w, sg_b, ret_logit_f, ret_logit_b, norm2, norm_f)
    mp = _pack_small(m_b_mod, m_norm1, m_sg_gain, m_sg_w, m_sg_b, m_ret_logit_f, m_ret_logit_b, m_norm2, m_norm_f)
    vp = _pack_small(v_b_mod, v_norm1, v_sg_gain, v_sg_w, v_sg_b, v_ret_logit_f, v_ret_logit_b, v_norm2, v_norm_f)
    gp, dp, mp2, vp2, loss_t = _small_update(gathered, wp, mp, vp)

    big_w = [w_in[0], w_out[0], tr(w_gate), tr(w_up), w_down[0]]
    big_m = [m_w_in[0], m_w_out[0], tr(m_w_gate), tr(m_w_up), m_w_down[0]]
    big_v = [v_w_in[0], v_w_out[0], tr(v_w_gate), tr(v_w_up), v_w_down[0]]
    upd = [_adamw_halves(w, own, oth, m, v, cidx, "adamw_" + nm) for w, own, oth, m, v, nm in
           zip(big_w, finals, others, big_m, big_v, names)]
    big_g = [g_wmod] + [u[0] for u in upd]
    big = [_adamw(w_mod[0], g_wmod, m_w_mod[0], v_w_mod[0], "adamw_w_mod")] + [u[1:] for u in upd]

    def assemble(small, cctx, bigs):
        b_mod_, norm1_, gain_, sgw_, sgb_, lf_, lb_, norm2_, normf_ = _unpack_small(small)
        wm, wi_, wo_, wg_, wu_, wd_ = [b[None] for b in bigs]
        wg_, wu_ = jnp.swapaxes(wg_, 1, 2), jnp.swapaxes(wu_, 1, 2)
        return [cctx.reshape(D), wm, b_mod_, norm1_, wi_, gain_, sgw_, sgb_, lf_, lb_, wo_, norm2_, wg_, wu_, wd_,
                normf_]

    out = [loss_t[0, 0], gx.reshape(1, ln, D)]
    out += assemble(gp, g_cctx, big_g)
    out += assemble(dp, d_cctx, [b[0] for b in big])
    out += assemble(mp2, nm_cctx, [b[1] for b in big])
    out += assemble(vp2, nv_cctx, [b[2] for b in big])
    return tuple(out)
```

```python
import functools

import jax
import jax.numpy as jnp
import numpy as np
from jax import lax
from jax.experimental import pallas as pl
from jax.experimental.pallas import tpu as pltpu

F32 = jnp.float32
BF = jnp.bfloat16
MESH = pl.DeviceIdType.MESH

D = 1024
CH = 128
HD = 128
NH = 4
AW = 512
IN_COLS = 3584
DFF = 2816
NCHIP = 4
NDEV = 8
WI_C = IN_COLS // NCHIP
FF_C = DFF // NCHIP
WO_R = D // NCHIP
EPS = 1e-6
GRID_W = 64
ROPE_BASE = 10000.0
K_SCALE = HD ** -0.5
LR, B1, B2, AEPS, WD, STEP = 0.001, 0.9, 0.999, 1e-08, 0.01, 10
VMEM_MB = 1 << 20
HI = lax.Precision.HIGHEST

P_DCMOD, P_DMOD, P_N1, P_GAIN, P_SGW, P_SGB, P_LG, P_N2, P_NF, P_LOSS = 0, 48, 96, 104, 112, 624, 632, 640, 648, 656
P_ROWS = 664
Q_BMOD, Q_N1, Q_GAIN, Q_SGW, Q_SGB, Q_LG, Q_N2, Q_NF = 0, 48, 56, 64, 576, 584, 592, 600
Q_ROWS = 608


def _params(vmem_mb, sem=None):
    return pltpu.CompilerParams(vmem_limit_bytes=vmem_mb * VMEM_MB, dimension_semantics=sem)


def _const(shape):
    nd = len(shape)
    return pl.BlockSpec(shape, lambda *_: (0,) * nd, pipeline_mode=pl.Buffered(1))


def _pos():
    return lax.axis_index("x"), lax.axis_index("y"), lax.axis_index("c")


def _dot(a, b, dims):
    return lax.dot_general(a, b, (dims, ((), ())), preferred_element_type=F32)


NN = ((1,), (0,))
NT = ((1,), (1,))
TN = ((0,), (0,))


@jax.custom_vjp
def _mm(a, b):
    return _dot(a.astype(BF), b.astype(BF), NN)


def _mm_f(a, b):
    return _mm(a, b), (a.astype(BF), b.astype(BF))


def _mm_b(res, g):
    a, b = res
    gb = g.astype(BF)
    return _dot(gb, b, NT), _dot(a, gb, TN)


_mm.defvjp(_mm_f, _mm_b)


@jax.custom_vjp
def _mm_nt(a, b):
    return _dot(a.astype(BF), b.astype(BF), NT)


def _mm_nt_f(a, b):
    return _mm_nt(a, b), (a.astype(BF), b.astype(BF))


def _mm_nt_b(res, g):
    a, b = res
    gb = g.astype(BF)
    return _dot(gb, b, NN), _dot(gb, a, TN)


_mm_nt.defvjp(_mm_nt_f, _mm_nt_b)


@jax.custom_vjp
def _mm_tn(a, b):
    return _dot(a.astype(BF), b.astype(BF), TN)


def _mm_tn_f(a, b):
    return _mm_tn(a, b), (a.astype(BF), b.astype(BF))


def _mm_tn_b(res, g):
    a, b = res
    gb = g.astype(BF)
    return _dot(b, gb, NT), _dot(a, gb, NN)


_mm_tn.defvjp(_mm_tn_f, _mm_tn_b)


def _gelu(x):
    return x * (0.5 * (1.0 + jnp.tanh(0.7978845608028654 * (x + 0.044715 * (x * x * x)))))


def _silu(x):
    return x * jax.nn.sigmoid(x)


def _rms(x):
    return lax.rsqrt(jnp.mean(x * x, axis=-1, keepdims=True) + EPS)


def _swap32(t):
    lane = lax.broadcasted_iota(jnp.int32, t.shape, 1)
    first = (lane % 64) < 32
    return jnp.where(first, pltpu.roll(t, 96, 1), pltpu.roll(t, 32, 1))


def _rope(t, cos, sin):
    return t * cos + _swap32(t) * sin


def _rope_bwd(d, cos, sin):
    return d * cos + _swap32(d * sin)


def _heads(ref, r0=0):
    return [ref[r0:r0 + CH, h * HD:(h + 1) * HD].astype(F32) for h in range(NH)]


def _gate_group(u, v, sgw, gain, bfull):
    gv = _gelu(v)
    return _gelu(u) * (_mm(sgw, gv * _rms(gv) * gain) + bfull)


def _gated_norm(gate, o):
    return _silu(gate) * (o * _rms(o))


def _ret_head(q, k, vr, gf, gb, sf, sb, df, xf, zf, db, xb, zb):
    a = _mm_nt(q, k)
    of = _mm(a * df, vr) + xf * _mm(q, sf)
    ob = _mm(a * db, vr) + xb * _mm(q, sb)
    return _gated_norm(gf, of) + _gated_norm(gb, ob), _mm_tn(k, zf * vr), _mm_tn(k, zb * vr), of


def _ctx_states(ctx0, ctx1, n1, csh, csc, wk, wv, zf, zb, ef, eb):
    hc0 = (ctx0 * _rms(ctx0) * n1) * (1.0 + csc) + csh
    hc1 = (ctx1 * _rms(ctx1) * n1) * (1.0 + csc) + csh
    scf, scb = [], []
    for h in range(NH):
        k0, k1 = _mm(hc0, wk[h]) * K_SCALE, _mm(hc1, wk[h]) * K_SCALE
        v0, v1 = _mm(hc0, wv[h]), _mm(hc1, wv[h])
        scf.append(ef[h] * _mm_tn(k0, zf[h] * v0) + _mm_tn(k1, zf[h] * v1))
        scb.append(eb[h] * _mm_tn(k1, zb[h] * v1) + _mm_tn(k0, zb[h] * v0))
    return scf, scb


def _allgather_small(v, name):
    r, n = v.shape

    def body(v_ref, out_ref, send_sems, recv_sems, local_sem):
        x, y, c = _pos()
        me = 4 * x + 2 * y + c
        mine = pltpu.make_async_copy(v_ref, out_ref.at[me], local_sem)
        mine.start()
        sent = []
        for k in range(1, NDEV):
            kx, ky, kc = (k >> 2) & 1, (k >> 1) & 1, k & 1
            peer = (x ^ kx, y ^ ky, c ^ kc)
            cp = pltpu.make_async_remote_copy(src_ref=v_ref, dst_ref=out_ref.at[me], send_sem=send_sems.at[k - 1],
                                              recv_sem=recv_sems.at[k - 1], device_id=peer, device_id_type=MESH)
            cp.start()
            sent.append(cp)
        for k in range(1, NDEV):
            kx, ky, kc = (k >> 2) & 1, (k >> 1) & 1, k & 1
            peer = (x ^ kx, y ^ ky, c ^ kc)
            src = 4 * (x ^ kx) + 2 * (y ^ ky) + (c ^ kc)
            pltpu.make_async_remote_copy(src_ref=v_ref, dst_ref=out_ref.at[src], send_sem=send_sems.at[k - 1],
                                         recv_sem=recv_sems.at[k - 1], device_id=peer, device_id_type=MESH).wait_recv()
        for cp in sent:
            cp.wait_send()
        mine.wait()

    return pl.pallas_call(
        body, name=name,
        out_shape=jax.ShapeDtypeStruct((NDEV, r, n), F32),
        in_specs=[pl.BlockSpec(memory_space=pltpu.VMEM)],
        out_specs=pl.BlockSpec(memory_space=pltpu.VMEM),
        scratch_shapes=[pltpu.SemaphoreType.DMA((NDEV - 1,)), pltpu.SemaphoreType.DMA((NDEV - 1,)),
                        pltpu.SemaphoreType.DMA],
        compiler_params=_params(16),
    )(v)


def _gather_weights(shards):
    nt = len(shards)
    shapes = [s.shape for s in shards]

    def body(*refs):
        srcs, outs, stages = refs[:nt], refs[nt:2 * nt], refs[2 * nt:3 * nt]
        ici_send, ici_recv, d2d_send, d2d_recv, local_sems = refs[3 * nt:]
        x, y, c = _pos()
        chip = 2 * x + y
        for t in range(nt):
            half = shapes[t][0] // 2
            stages[t][0] = srcs[t][0:half, :].astype(BF)
            stages[t][1] = srcs[t][half:2 * half, :].astype(BF)
        local = []
        for t in range(nt):
            cp = pltpu.make_async_copy(stages[t], outs[t].at[chip], local_sems.at[t])
            cp.start()
            local.append(cp)
        sent = []
        for k in range(1, NCHIP):
            kx, ky = (k >> 1) & 1, k & 1
            for t in range(nt):
                s = (k - 1) * nt + t
                cp = pltpu.make_async_remote_copy(
                    src_ref=stages[t].at[c], dst_ref=outs[t].at[chip, c], send_sem=ici_send.at[s],
                    recv_sem=ici_recv.at[s], device_id=(x ^ kx, y ^ ky, c), device_id_type=MESH)
                cp.start()
                sent.append(cp)
        for k in range(1, NCHIP):
            kx, ky = (k >> 1) & 1, k & 1
            src_chip = 2 * (x ^ kx) + (y ^ ky)
            for t in range(nt):
                s = (k - 1) * nt + t
                pltpu.make_async_remote_copy(
                    src_ref=stages[t].at[c], dst_ref=outs[t].at[src_chip, c], send_sem=ici_send.at[s],
                    recv_sem=ici_recv.at[s], device_id=(x ^ kx, y ^ ky, c), device_id_type=MESH).wait_recv()
                cp = pltpu.make_async_remote_copy(
                    src_ref=outs[t].at[src_chip, c], dst_ref=outs[t].at[src_chip, c], send_sem=d2d_send.at[s],
                    recv_sem=d2d_recv.at[s], device_id=(x, y, 1 - c), device_id_type=MESH)
                cp.start()
                sent.append(cp)
        for k in range(1, NCHIP):
            kx, ky = (k >> 1) & 1, k & 1
            src_chip = 2 * (x ^ kx) + (y ^ ky)
            for t in range(nt):
                s = (k - 1) * nt + t
                pltpu.make_async_remote_copy(
                    src_ref=stages[t].at[c], dst_ref=outs[t].at[src_chip, 1 - c], send_sem=d2d_send.at[s],
                    recv_sem=d2d_recv.at[s], device_id=(x, y, 1 - c), device_id_type=MESH).wait_recv()
        for cp in sent:
            cp.wait_send()
        for cp in local:
            cp.wait()

    n_rem = (NCHIP - 1) * nt
    out = pl.pallas_call(
        body, name="gather_weights",
        out_shape=[jax.ShapeDtypeStruct((NCHIP, 2, r // 2, cc), BF) for r, cc in shapes],
        in_specs=[pl.BlockSpec(memory_space=pltpu.VMEM)] * nt,
        out_specs=[pl.BlockSpec(memory_space=pl.ANY)] * nt,
        scratch_shapes=[pltpu.VMEM((2, r // 2, cc), BF) for r, cc in shapes]
        + [pltpu.SemaphoreType.DMA((n_rem,))] * 4 + [pltpu.SemaphoreType.DMA((nt,))],
        compiler_params=_params(48),
    )(*shards)
    return [o.reshape(NCHIP, r, cc) for o, (r, cc) in zip(out, shapes)]


def _chip_offsets():
    return [((k >> 1) & 1, k & 1) for k in range(1, NCHIP)]


def _prologue(c, c_ctx, w_mod_s, shards):
    nt = len(shards)
    shapes = [s.shape for s in shards]
    mod_c = w_mod_s.shape[1]

    def body(*refs):
        c_ref, cc_ref, wm_ref = refs[:3]
        srcs = refs[3:3 + nt]
        outs = refs[3 + nt:3 + 2 * nt]
        call_ref, prod_ref = refs[3 + 2 * nt:5 + 2 * nt]
        stages = refs[5 + 2 * nt:5 + 3 * nt]
        ct = refs[5 + 3 * nt]
        c_send, c_recv, p_send, p_recv, ici_send, ici_recv, d2d_send, d2d_recv, local_sems = refs[6 + 3 * nt:]
        x, y, c = _pos()
        chip = 2 * x + y
        me = 4 * x + 2 * y + c
        sib = (x, y, 1 - c)
        pending = []
        for t in range(nt):
            half = shapes[t][0] // 2
            stages[t][0] = srcs[t][0:half, :].astype(BF)
            stages[t][1] = srcs[t][half:2 * half, :].astype(BF)
            cp = pltpu.make_async_copy(stages[t], outs[t].at[chip], local_sems.at[t])
            cp.start()
            pending.append(cp)
        sends = []
        for k, (kx, ky) in enumerate(_chip_offsets()):
            cp = pltpu.make_async_remote_copy(src_ref=stages[0].at[c], dst_ref=outs[0].at[chip, c],
                                              send_sem=ici_send.at[k], recv_sem=ici_recv.at[k],
                                              device_id=(x ^ kx, y ^ ky, c), device_id_type=MESH)
            cp.start()
            sends.append(cp)

        def to_all(src, dst_of, send_sems, recv_sems):
            for k in range(1, NDEV):
                kx, ky, kc = (k >> 2) & 1, (k >> 1) & 1, k & 1
                cp = pltpu.make_async_remote_copy(src_ref=src, dst_ref=dst_of(me), send_sem=send_sems.at[k - 1],
                                                  recv_sem=recv_sems.at[k - 1], device_id=(x ^ kx, y ^ ky, c ^ kc),
                                                  device_id_type=MESH)
                cp.start()
                sends.append(cp)
            for k in range(1, NDEV):
                kx, ky, kc = (k >> 2) & 1, (k >> 1) & 1, k & 1
                frm = 4 * (x ^ kx) + 2 * (y ^ ky) + (c ^ kc)
                pltpu.make_async_remote_copy(src_ref=src, dst_ref=dst_of(frm), send_sem=send_sems.at[k - 1],
                                             recv_sem=recv_sems.at[k - 1], device_id=(x ^ kx, y ^ ky, c ^ kc),
                                             device_id_type=MESH).wait_recv()

        call_ref[me] = c_ref[...]
        to_all(c_ref, lambda d: call_ref.at[d], c_send, c_recv)
        ct[...] = jnp.zeros_like(ct)
        for d in range(NDEV):
            ct[d:d + 1, :] = call_ref[d]
        ct[NDEV:NDEV + 1, :] = cc_ref[...]
        prod_ref[me] = jnp.dot(_silu(ct[...]), wm_ref[...], precision=HI, preferred_element_type=F32)
        to_all(prod_ref.at[me], lambda d: prod_ref.at[d], p_send, p_recv)

        for k, (kx, ky) in enumerate(_chip_offsets()):
            frm = 2 * (x ^ kx) + (y ^ ky)
            pltpu.make_async_remote_copy(src_ref=stages[0].at[c], dst_ref=outs[0].at[frm, c],
                                         send_sem=ici_send.at[k], recv_sem=ici_recv.at[k],
                                         device_id=(x ^ kx, y ^ ky, c), device_id_type=MESH).wait_recv()
            cp = pltpu.make_async_remote_copy(src_ref=outs[0].at[frm, c], dst_ref=outs[0].at[frm, c],
                                              send_sem=d2d_send.at[k], recv_sem=d2d_recv.at[k],
                                              device_id=sib, device_id_type=MESH)
            cp.start()
            sends.append(cp)
        for k, (kx, ky) in enumerate(_chip_offsets()):
            frm = 2 * (x ^ kx) + (y ^ ky)
            pltpu.make_async_remote_copy(src_ref=stages[0].at[c], dst_ref=outs[0].at[frm, 1 - c],
                                         send_sem=d2d_send.at[k], recv_sem=d2d_recv.at[k],
                                         device_id=sib, device_id_type=MESH).wait_recv()
        for cp in sends:
            cp.wait_send()
        for cp in pending:
            cp.wait()

    vm = pl.BlockSpec(memory_space=pltpu.VMEM)
    out = pl.pallas_call(
        body, name="prologue",
        out_shape=[jax.ShapeDtypeStruct((NCHIP, 2, r // 2, cc), BF) for r, cc in shapes]
        + [jax.ShapeDtypeStruct((NDEV, 1, D), F32), jax.ShapeDtypeStruct((NDEV, 16, mod_c), F32)],
        in_specs=[vm] * (3 + nt),
        out_specs=[pl.BlockSpec(memory_space=pl.ANY)] * nt + [vm, vm],
        scratch_shapes=[pltpu.VMEM((2, r // 2, cc), BF) for r, cc in shapes] + [pltpu.VMEM((16, D), F32)]
        + [pltpu.SemaphoreType.DMA((NDEV - 1,))] * 4 + [pltpu.SemaphoreType.DMA((NCHIP - 1,))] * 4
        + [pltpu.SemaphoreType.DMA((nt,))],
        compiler_params=_params(56),
    )(c, c_ctx, w_mod_s, *shards)
    return out[:nt], out[nt], out[nt + 1]


def _gather_ici_copies(bufs, send_sems, recv_sems):
    x, y, c = _pos()
    chip = 2 * x + y
    nt = len(bufs)
    out_cp, in_cp = [], []
    for k, (kx, ky) in enumerate(_chip_offsets()):
        frm = 2 * (x ^ kx) + (y ^ ky)
        for t in range(nt):
            s = k * nt + t
            peer = (x ^ kx, y ^ ky, c)
            out_cp.append(pltpu.make_async_remote_copy(
                src_ref=bufs[t].at[chip, c], dst_ref=bufs[t].at[chip, c], send_sem=send_sems.at[s],
                recv_sem=recv_sems.at[s], device_id=peer, device_id_type=MESH))
            in_cp.append(pltpu.make_async_remote_copy(
                src_ref=bufs[t].at[chip, c], dst_ref=bufs[t].at[frm, c], send_sem=send_sems.at[s],
                recv_sem=recv_sems.at[s], device_id=peer, device_id_type=MESH))
    return out_cp, in_cp


def _gather_d2d_copies(bufs, send_sems, recv_sems):
    x, y, c = _pos()
    nt = len(bufs)
    out_cp, in_cp = [], []
    for k, (kx, ky) in enumerate(_chip_offsets()):
        frm = 2 * (x ^ kx) + (y ^ ky)
        for t in range(nt):
            s = k * nt + t
            out_cp.append(pltpu.make_async_remote_copy(
                src_ref=bufs[t].at[frm, c], dst_ref=bufs[t].at[frm, c], send_sem=send_sems.at[s],
                recv_sem=recv_sems.at[s], device_id=(x, y, 1 - c), device_id_type=MESH))
            in_cp.append(pltpu.make_async_remote_copy(
                src_ref=bufs[t].at[frm, c], dst_ref=bufs[t].at[frm, 1 - c], send_sem=send_sems.at[s],
                recv_sem=recv_sems.at[s], device_id=(x, y, 1 - c), device_id_type=MESH))
    return out_cp, in_cp


def _scatter_ici_copies(parts, outs, send_sems, recv_sems):
    x, y, c = _pos()
    nt = len(parts)
    cps = []
    for k, (kx, ky) in enumerate(_chip_offsets()):
        dst_chip = 2 * (x ^ kx) + (y ^ ky)
        for t in range(nt):
            s = k * nt + t
            cps.append(pltpu.make_async_remote_copy(
                src_ref=parts[t].at[dst_chip], dst_ref=outs[t].at[k], send_sem=send_sems.at[s],
                recv_sem=recv_sems.at[s], device_id=(x ^ kx, y ^ ky, c), device_id_type=MESH))
    return cps


def _rs_exchange_halves(grads, name):
    nt = len(grads)
    shapes = [g.shape for g in grads]

    def body(*refs):
        gs, outs = refs[:nt], refs[nt:2 * nt]
        send_sems, recv_sems = refs[2 * nt:]
        x, y, c = _pos()
        sib = (x, y, 1 - c)
        sent = []
        for t in range(nt):
            for j in range(NCHIP):
                s = t * NCHIP + j
                cp = pltpu.make_async_remote_copy(src_ref=gs[t].at[j, 1 - c], dst_ref=outs[t].at[j],
                                                  send_sem=send_sems.at[s], recv_sem=recv_sems.at[s],
                                                  device_id=sib, device_id_type=MESH)
                cp.start()
                sent.append(cp)
        for cp in sent:
            cp.wait_recv()
        for cp in sent:
            cp.wait_send()

    return pl.pallas_call(
        body, name=name,
        out_shape=[jax.ShapeDtypeStruct((NCHIP, s[2], s[3]), F32) for s in shapes],
        in_specs=[pl.BlockSpec(memory_space=pl.ANY)] * nt,
        out_specs=[pl.BlockSpec(memory_space=pl.ANY)] * nt,
        scratch_shapes=[pltpu.SemaphoreType.DMA((nt * NCHIP,))] * 2,
    )(*grads)


def _rs_send_chips(parts):
    nt = len(parts)
    shapes = [p.shape for p in parts]

    def body(*refs):
        ps, outs = refs[:nt], refs[nt:2 * nt]
        send_sems, recv_sems = refs[2 * nt:]
        x, y, c = _pos()
        sent = []
        for k in range(1, NCHIP):
            kx, ky = (k >> 1) & 1, k & 1
            dst_chip = 2 * (x ^ kx) + (y ^ ky)
            for t in range(nt):
                s = (k - 1) * nt + t
                cp = pltpu.make_async_remote_copy(src_ref=ps[t].at[dst_chip], dst_ref=outs[t].at[k - 1],
                                                  send_sem=send_sems.at[s], recv_sem=recv_sems.at[s],
                                                  device_id=(x ^ kx, y ^ ky, c), device_id_type=MESH)
                cp.start()
                sent.append(cp)
        for cp in sent:
            cp.wait_recv()
        for cp in sent:
            cp.wait_send()

    return pl.pallas_call(
        body, name="rs_send_chips",
        out_shape=[jax.ShapeDtypeStruct((NCHIP - 1, s[1], s[2]), BF) for s in shapes],
        in_specs=[pl.BlockSpec(memory_space=pl.ANY)] * nt,
        out_specs=[pl.BlockSpec(memory_space=pl.ANY)] * nt,
        scratch_shapes=[pltpu.SemaphoreType.DMA((nt * (NCHIP - 1),))] * 2,
    )(*parts)


def _rs_share_final(finals):
    nt = len(finals)
    shapes = [f.shape for f in finals]

    def body(*refs):
        fs, outs = refs[:nt], refs[nt:2 * nt]
        send_sems, recv_sems = refs[2 * nt:]
        x, y, c = _pos()
        sent = []
        for t in range(nt):
            cp = pltpu.make_async_remote_copy(src_ref=fs[t], dst_ref=outs[t], send_sem=send_sems.at[t],
                                              recv_sem=recv_sems.at[t], device_id=(x, y, 1 - c), device_id_type=MESH)
            cp.start()
            sent.append(cp)
        for cp in sent:
            cp.wait_recv()
        for cp in sent:
            cp.wait_send()

    return pl.pallas_call(
        body, name="rs_share_final",
        out_shape=[jax.ShapeDtypeStruct(s, F32) for s in shapes],
        in_specs=[pl.BlockSpec(memory_space=pl.ANY)] * nt,
        out_specs=[pl.BlockSpec(memory_space=pl.ANY)] * nt,
        scratch_shapes=[pltpu.SemaphoreType.DMA((nt,))] * 2,
    )(*finals)


def _row_tile(h, cc=D):
    for t in (512, 384, 352, 256, 176, 128, 64, 32, 16):
        if h % t == 0 and t * cc * 4 <= (5 * VMEM_MB) // 4:
            return t
    return h


def _rs_add_halves(g, recv, cidx, name):
    _, _, h, cc = g.shape
    th = _row_tile(h, cc)

    def body(c_ref, g_ref, r_ref, of_ref, ob_ref):
        s = g_ref[...] + r_ref[...]
        of_ref[...] = s
        ob_ref[...] = s.astype(BF)

    return pl.pallas_call(
        body, name=name,
        grid_spec=pltpu.PrefetchScalarGridSpec(
            num_scalar_prefetch=1, grid=(NCHIP, h // th),
            in_specs=[pl.BlockSpec((None, None, th, cc), lambda j, i, c_ref: (j, c_ref[0], i, 0)),
                      pl.BlockSpec((None, th, cc), lambda j, i, c_ref: (j, i, 0))],
            out_specs=[pl.BlockSpec((None, th, cc), lambda j, i, c_ref: (j, i, 0)),
                       pl.BlockSpec((None, th, cc), lambda j, i, c_ref: (j, i, 0))]),
        out_shape=[jax.ShapeDtypeStruct((NCHIP, h, cc), F32), jax.ShapeDtypeStruct((NCHIP, h, cc), BF)],
        compiler_params=_params(48),
    )(cidx, g, recv)


def _rs_add_chips(own, recv, chipidx, name):
    _, h, cc = own.shape
    th = _row_tile(h, cc)

    def body(j_ref, o_ref, r_ref, out_ref):
        out_ref[...] = ((o_ref[...] + r_ref[0].astype(F32)) + r_ref[1].astype(F32)) + r_ref[2].astype(F32)

    return pl.pallas_call(
        body, name=name,
        grid_spec=pltpu.PrefetchScalarGridSpec(
            num_scalar_prefetch=1, grid=(h // th,),
            in_specs=[pl.BlockSpec((None, th, cc), lambda i, j_ref: (j_ref[0], i, 0)),
                      pl.BlockSpec((NCHIP - 1, th, cc), lambda i, j_ref: (0, i, 0))],
            out_specs=pl.BlockSpec((th, cc), lambda i, j_ref: (i, 0))),
        out_shape=jax.ShapeDtypeStruct((h, cc), F32),
        compiler_params=_params(48),
    )(chipidx, own, recv)


def _adamw_math(w, g, m, v):
    m2 = B1 * m + (1.0 - B1) * g
    v2 = B2 * v + (1.0 - B2) * (g * g)
    m_hat = m2 / (1.0 - B1 ** STEP)
    v_hat = v2 / (1.0 - B2 ** STEP)
    delta = -LR * (m_hat / (jnp.sqrt(v_hat) + AEPS) + WD * w)
    return delta, m2, v2


def _adamw(w, g, m, v, name):
    r, cc = w.shape
    tr = _row_tile(r, cc)

    def body(w_ref, g_ref, m_ref, v_ref, d_ref, mo_ref, vo_ref):
        d, m2, v2 = _adamw_math(w_ref[...], g_ref[...], m_ref[...], v_ref[...])
        d_ref[...] = d
        mo_ref[...] = m2
        vo_ref[...] = v2

    spec = pl.BlockSpec((tr, cc), lambda i: (i, 0))
    return pl.pallas_call(
        body, name=name, grid=(r // tr,), in_specs=[spec] * 4, out_specs=[spec] * 3,
        out_shape=[jax.ShapeDtypeStruct((r, cc), F32)] * 3,
        compiler_params=_params(48, ("parallel",)),
    )(w, g, m, v)


def _adamw_halves(w, own, other, m, v, cidx, name):
    r, cc = w.shape
    h = r // 2
    tr = _row_tile(h, cc)
    per = h // tr

    def body(c_ref, w_ref, own_ref, oth_ref, m_ref, v_ref, g_ref, d_ref, mo_ref, vo_ref):
        mine = (pl.program_id(0) // per) == c_ref[0]
        g = jnp.where(mine, own_ref[...], oth_ref[...])
        g_ref[...] = g
        d, m2, v2 = _adamw_math(w_ref[...], g, m_ref[...], v_ref[...])
        d_ref[...] = d
        mo_ref[...] = m2
        vo_ref[...] = v2

    full = pl.BlockSpec((tr, cc), lambda i, c_ref: (i, 0))
    half = pl.BlockSpec((tr, cc), lambda i, c_ref: (i % per, 0))
    return pl.pallas_call(
        body, name=name,
        grid_spec=pltpu.PrefetchScalarGridSpec(
            num_scalar_prefetch=1, grid=(r // tr,),
            in_specs=[full, half, half, full, full], out_specs=[full] * 4),
        out_shape=[jax.ShapeDtypeStruct((r, cc), F32)] * 4,
        compiler_params=_params(48, ("parallel",)),
    )(cidx, w, own, other, m, v)


def _mod_forward(ct_pad, w_mod_s):
    def body(c_ref, w_ref, o_ref):
        o_ref[...] = jnp.dot(_silu(c_ref[...]), w_ref[...], precision=HI, preferred_element_type=F32)

    return pl.pallas_call(
        body, name="mod_forward",
        out_shape=jax.ShapeDtypeStruct((16, w_mod_s.shape[1]), F32),
        in_specs=[pl.BlockSpec(memory_space=pltpu.VMEM)] * 2,
        out_specs=pl.BlockSpec(memory_space=pltpu.VMEM),
        compiler_params=_params(32),
    )(ct_pad, w_mod_s)


def _decay_exponents():
    ri = lax.broadcasted_iota(jnp.int32, (CH, CH), 0).astype(F32)
    ci = lax.broadcasted_iota(jnp.int32, (CH, CH), 1).astype(F32)
    full = jnp.full((CH, CH), float(CH), F32)
    return [[ri - ci, ri + 1.0, (CH - 1.0) - ri, full], [ci - ri, CH - ri, ri, full]]


def _decay_mats(logit_full):
    def body(l_ref, o_ref):
        ex = _decay_exponents()
        for d in range(2):
            for h in range(NH):
                lv = l_ref[d * NH + h]
                lg = jnp.minimum(lv, 0.0) - jnp.log(1.0 + jnp.exp(-jnp.abs(lv)))
                for kind in range(4):
                    m = jnp.exp(lg * ex[d][kind])
                    if kind == 0:
                        m = jnp.where(ex[d][0] >= 0.0, jnp.exp(lg * jnp.maximum(ex[d][0], 0.0)), 0.0)
                    o_ref[d, kind, h] = m

    return pl.pallas_call(
        body, name="decay_mats",
        out_shape=jax.ShapeDtypeStruct((2, 4, NH, CH, CH), F32),
        in_specs=[pl.BlockSpec(memory_space=pltpu.VMEM)],
        out_specs=pl.BlockSpec(memory_space=pltpu.VMEM),
        compiler_params=_params(32),
    )(logit_full)


def _ctx_kv_weights(wi_ref):
    def cols(g):
        return wi_ref[g // WI_C, :, g % WI_C: g % WI_C + HD].astype(F32)

    wk = [cols(3 * AW + h * HD) for h in range(NH)]
    wv = [cols(4 * AW + h * HD) for h in range(NH)]
    return wk, wv


def _ctx_forward(ctx, vecs, wi, dm):
    def body(ctx_ref, v_ref, wi_ref, dm_ref, scf_ref, scb_ref):
        wk, wv = _ctx_kv_weights(wi_ref)
        mats = [[dm_ref[d, kind, h] for h in range(NH)] for d in range(2) for kind in (2, 3)]
        scf, scb = _ctx_states(ctx_ref[0:CH, :], ctx_ref[CH:2 * CH, :], v_ref[0:1, :], v_ref[1:2, :],
                               v_ref[2:3, :], wk, wv, mats[0], mats[2], mats[1], mats[3])
        for h in range(NH):
            scf_ref[h] = scf[h]
            scb_ref[h] = scb[h]

    return pl.pallas_call(
        body, name="ctx_forward",
        out_shape=[jax.ShapeDtypeStruct((NH, HD, HD), F32)] * 2,
        in_specs=[pl.BlockSpec(memory_space=pltpu.VMEM)] * 4,
        out_specs=[pl.BlockSpec(memory_space=pltpu.VMEM)] * 2,
        compiler_params=_params(48),
    )(ctx, vecs, wi, dm)


def _ctx_backward(ctx, vecs, wi, dm, dscf, dscb):
    def body(ctx_ref, v_ref, wi_ref, dm_ref, gf_ref, gb_ref, gw_ref, gv_ref, gdm_ref):
        wk, wv = _ctx_kv_weights(wi_ref)
        mats = [[dm_ref[d, kind, h] for h in range(NH)] for d in range(2) for kind in (2, 3)]
        ctx0, ctx1 = ctx_ref[0:CH, :], ctx_ref[CH:2 * CH, :]

        def fn(n1, csh, csc, wk_, wv_, zf, zb, ef, eb):
            return _ctx_states(ctx0, ctx1, n1, csh, csc, wk_, wv_, zf, zb, ef, eb)

        _, vjp = jax.vjp(fn, v_ref[0:1, :], v_ref[1:2, :], v_ref[2:3, :], wk, wv,
                         mats[0], mats[2], mats[1], mats[3])
        cot = ([gf_ref[h] for h in range(NH)], [gb_ref[h] for h in range(NH)])
        dn1, dcsh, dcsc, dwk, dwv, dzf, dzb, def_, deb = vjp(cot)
        for h in range(NH):
            gw_ref[:, h * HD:(h + 1) * HD] = dwk[h]
            gw_ref[:, AW + h * HD:AW + (h + 1) * HD] = dwv[h]
        gv_ref[...] = jnp.zeros_like(gv_ref)
        gv_ref[0:1, :] = dn1
        gv_ref[1:2, :] = dcsh
        gv_ref[2:3, :] = dcsc
        for h in range(NH):
            gdm_ref[0, 0, h] = dzf[h]
            gdm_ref[0, 1, h] = def_[h]
            gdm_ref[1, 0, h] = dzb[h]
            gdm_ref[1, 1, h] = deb[h]

    return pl.pallas_call(
        body, name="ctx_backward",
        out_shape=[jax.ShapeDtypeStruct((D, 2 * AW), F32), jax.ShapeDtypeStruct((8, D), F32),
                   jax.ShapeDtypeStruct((2, 2, NH, CH, CH), F32)],
        in_specs=[pl.BlockSpec(memory_space=pltpu.VMEM)] * 6,
        out_specs=[pl.BlockSpec(memory_space=pltpu.VMEM)] * 3,
        compiler_params=_params(56),
    )(ctx, vecs, wi, dm, dscf, dscb)


def _in_proj(x, vecs, wi, gbufs):
    ln = x.shape[0]
    t = min(512, ln)
    nt = len(gbufs)
    steps = ln // t

    def body(x_ref, v_ref, wi_ref, *refs):
        z_ref, hx_ref = refs[nt:nt + 2]
        bufs = refs[nt + 2:2 * nt + 2]
        send_sems, recv_sems = refs[2 * nt + 2:]
        i = pl.program_id(0)

        @pl.when(i == 0)
        def _():
            for cp in _gather_ici_copies(bufs, send_sems, recv_sems)[0]:
                cp.start()

        xv = x_ref[...]
        hx = (xv * _rms(xv) * v_ref[0:1, :]) * (1.0 + v_ref[2:3, :]) + v_ref[1:2, :]
        hb = hx.astype(BF)
        hx_ref[...] = hb
        for j in range(NCHIP):
            z_ref[:, j * WI_C:(j + 1) * WI_C] = _dot(hb, wi_ref[j], NN)

        @pl.when(i == steps - 1)
        def _():
            out_cp, in_cp = _gather_ici_copies(bufs, send_sems, recv_sems)
            for cp in in_cp:
                cp.wait_recv()
            for cp in out_cp:
                cp.wait_send()

    hbm = pl.BlockSpec(memory_space=pl.ANY)
    out = pl.pallas_call(
        body, name="in_proj", grid=(steps,),
        in_specs=[pl.BlockSpec((t, D), lambda i: (i, 0)), _const((8, D)), _const((NCHIP, D, WI_C))] + [hbm] * nt,
        out_specs=[pl.BlockSpec((t, IN_COLS), lambda i: (i, 0)), pl.BlockSpec((t, D), lambda i: (i, 0))] + [hbm] * nt,
        out_shape=[jax.ShapeDtypeStruct((ln, IN_COLS), F32), jax.ShapeDtypeStruct((ln, D), BF)]
        + [jax.ShapeDtypeStruct(g.shape, g.dtype) for g in gbufs],
        input_output_aliases={3 + k: 2 + k for k in range(nt)},
        scratch_shapes=[pltpu.SemaphoreType.DMA(((NCHIP - 1) * nt,))] * 2,
        compiler_params=_params(56, ("arbitrary",)),
    )(x, vecs, wi, *gbufs)
    return out[0], out[1], out[2:]


def _allgather_copies(src, out, send_sems, recv_sems, local_sem):
    x, y, c = _pos()
    me = 4 * x + 2 * y + c
    sends, recvs = [], []
    for k in range(1, NDEV):
        kx, ky, kc = (k >> 2) & 1, (k >> 1) & 1, k & 1
        peer = (x ^ kx, y ^ ky, c ^ kc)
        frm = 4 * (x ^ kx) + 2 * (y ^ ky) + (c ^ kc)
        sends.append(pltpu.make_async_remote_copy(src_ref=src, dst_ref=out.at[me], send_sem=send_sems.at[k - 1],
                                                  recv_sem=recv_sems.at[k - 1], device_id=peer, device_id_type=MESH))
        recvs.append(pltpu.make_async_remote_copy(src_ref=src, dst_ref=out.at[frm], send_sem=send_sems.at[k - 1],
                                                  recv_sem=recv_sems.at[k - 1], device_id=peer, device_id_type=MESH))
    return sends, recvs, pltpu.make_async_copy(src, out.at[me], local_sem)


def _in_proj_bwd(dz, x, dx1, vecs, wi, parts, early):
    ln = x.shape[0]
    t = min(512, ln)
    nt = len(parts)
    steps = ln // t

    def body(dz_ref, x_ref, dx1_ref, v_ref, wi_ref, *refs):
        ps = refs[:nt]
        early_ref = refs[nt]
        gx_ref, acc_ref = refs[nt + 1:nt + 3]
        got = refs[nt + 3:2 * nt + 3]
        early_all = refs[2 * nt + 3]
        send_sems, recv_sems, ag_send, ag_recv, ag_local = refs[2 * nt + 4:]

        @pl.when(pl.program_id(0) == 0)
        def _():
            acc_ref[...] = jnp.zeros_like(acc_ref)
            for cp in _scatter_ici_copies(ps, got, send_sems, recv_sems):
                cp.start()
            sends, _, own = _allgather_copies(early_ref, early_all, ag_send, ag_recv, ag_local)
            own.start()
            for cp in sends:
                cp.start()

        dhx = jnp.zeros((t, D), F32)
        for j in range(NCHIP):
            dhx = dhx + _dot(dz_ref[:, j * WI_C:(j + 1) * WI_C], wi_ref[j], NT)
        xv = x_ref[...]
        r = _rms(xv)
        xn = xv * r
        n1, sc = v_ref[0:1, :], v_ref[2:3, :]
        acc_ref[0:1, :] += jnp.sum(dhx * xn * (1.0 + sc), axis=0, keepdims=True)
        acc_ref[1:2, :] += jnp.sum(dhx, axis=0, keepdims=True)
        acc_ref[2:3, :] += jnp.sum(dhx * xn * n1, axis=0, keepdims=True)
        g = dhx * n1 * (1.0 + sc)
        gx_ref[...] = dx1_ref[...] + r * (g - xn * jnp.mean(g * xn, axis=-1, keepdims=True))

        @pl.when(pl.program_id(0) == steps - 1)
        def _():
            cps = _scatter_ici_copies(ps, got, send_sems, recv_sems)
            sends, recvs, own = _allgather_copies(early_ref, early_all, ag_send, ag_recv, ag_local)
            for cp in cps + recvs:
                cp.wait_recv()
            for cp in cps + sends:
                cp.wait_send()
            own.wait()

    hbm = pl.BlockSpec(memory_space=pl.ANY)
    out = pl.pallas_call(
        body, name="in_proj_bwd", grid=(steps,),
        in_specs=[pl.BlockSpec((t, IN_COLS), lambda i: (i, 0)), pl.BlockSpec((t, D), lambda i: (i, 0)),
                  pl.BlockSpec((t, D), lambda i: (i, 0)), _const((8, D)), _const((NCHIP, D, WI_C))]
        + [hbm] * (nt + 1),
        out_specs=[pl.BlockSpec((t, D), lambda i: (i, 0)), pl.BlockSpec((8, D), lambda i: (0, 0))]
        + [hbm] * (nt + 1),
        out_shape=[jax.ShapeDtypeStruct((ln, D), F32), jax.ShapeDtypeStruct((8, D), F32)]
        + [jax.ShapeDtypeStruct((NCHIP - 1,) + p.shape[1:], BF) for p in parts]
        + [jax.ShapeDtypeStruct((NDEV,) + early.shape, F32)],
        scratch_shapes=[pltpu.SemaphoreType.DMA(((NCHIP - 1) * nt,))] * 2
        + [pltpu.SemaphoreType.DMA((NDEV - 1,))] * 2 + [pltpu.SemaphoreType.DMA],
        compiler_params=_params(56, ("arbitrary",)),
    )(dz, x, dx1, vecs, wi, *parts, early)
    return out[0], out[1], out[2:2 + nt], out[2 + nt]


def _post_mixer(x, ycat, tgt, vecs, wo, wg, wu, wd):
    ln = x.shape[0]
    t = min(256, ln)

    def body(x_ref, y_ref, t_ref, v_ref, wo_ref, wg_ref, wu_ref, wd_ref,
             dx1_ref, dyc_ref, h2_ref, dy_ref, df_ref, act_ref, da_ref, db_ref, acc_ref, a_st, b_st):
        @pl.when(pl.program_id(0) == 0)
        def _():
            acc_ref[...] = jnp.zeros_like(acc_ref)

        g1, n2, sh2, sc2 = v_ref[0:1, :], v_ref[1:2, :], v_ref[2:3, :], v_ref[3:4, :]
        g2, nf = v_ref[4:5, :], v_ref[5:6, :]
        y = _dot(y_ref[...], wo_ref[...], NN)
        x1 = x_ref[...] + g1 * y
        r2 = _rms(x1)
        xn2 = x1 * r2
        t2 = xn2 * n2
        h2b = (t2 * (1.0 + sc2) + sh2).astype(BF)
        h2_ref[...] = h2b
        a = _dot(h2b, wg_ref[...], NT)
        b = _dot(h2b, wu_ref[...], NT)
        a_st[...] = a
        b_st[...] = b
        act = (_silu(a) * b).astype(BF)
        act_ref[...] = act
        f = _dot(act, wd_ref[...], NN)
        x2 = x1 + g2 * f
        r3 = _rms(x2)
        xn3 = x2 * r3
        e = xn3 * nf - t_ref[...]
        acc_ref[6:7, :] += jnp.sum(e * e, axis=0, keepdims=True) * (0.5 / D)
        dout = e * (1.0 / D)
        acc_ref[5:6, :] += jnp.sum(dout * xn3, axis=0, keepdims=True)
        gg = dout * nf
        dx2 = r3 * (gg - xn3 * jnp.mean(gg * xn3, axis=-1, keepdims=True))
        acc_ref[4:5, :] += jnp.sum(dx2 * f, axis=0, keepdims=True)
        dfb = (g2 * dx2).astype(BF)
        df_ref[...] = dfb
        dact = _dot(dfb, wd_ref[...], NT)
        a = a_st[...]
        b = b_st[...]
        s = jax.nn.sigmoid(a)
        da = (dact * b * (s * (1.0 + a * (1.0 - s)))).astype(BF)
        db = (dact * (a * s)).astype(BF)
        da_ref[...] = da
        db_ref[...] = db
        dh2 = _dot(da, wg_ref[...], NN) + _dot(db, wu_ref[...], NN)
        acc_ref[2:3, :] += jnp.sum(dh2, axis=0, keepdims=True)
        acc_ref[3:4, :] += jnp.sum(dh2 * t2, axis=0, keepdims=True)
        acc_ref[1:2, :] += jnp.sum(dh2 * xn2 * (1.0 + sc2), axis=0, keepdims=True)
        gx = dh2 * n2 * (1.0 + sc2)
        dx1 = dx2 + r2 * (gx - xn2 * jnp.mean(gx * xn2, axis=-1, keepdims=True))
        dx1_ref[...] = dx1
        acc_ref[0:1, :] += jnp.sum(dx1 * y, axis=0, keepdims=True)
        dyb = (g1 * dx1).astype(BF)
        dy_ref[...] = dyb
        dyc_ref[...] = _dot(dyb, wo_ref[...], NT)

    tok = pl.BlockSpec((t, D), lambda i: (i, 0))
    ffb = pl.BlockSpec((t, DFF), lambda i: (i, 0))
    return pl.pallas_call(
        body, name="post_mixer", grid=(ln // t,),
        in_specs=[tok, tok, tok, _const((8, D)), _const((D, D)), _const((DFF, D)), _const((DFF, D)),
                  _const((DFF, D))],
        out_specs=[tok, tok, tok, tok, tok, ffb, ffb, ffb, pl.BlockSpec((16, D), lambda i: (0, 0))],
        out_shape=[jax.ShapeDtypeStruct((ln, D), F32)] * 2 + [jax.ShapeDtypeStruct((ln, D), BF)] * 3
        + [jax.ShapeDtypeStruct((ln, DFF), BF)] * 3 + [jax.ShapeDtypeStruct((16, D), F32)],
        scratch_shapes=[pltpu.VMEM((t, DFF), F32)] * 2,
        compiler_params=_params(60, ("arbitrary",)),
    )(x, ycat, tgt, vecs, wo, wg, wu, wd)


def _exchange_copies(g, out, send_sems, recv_sems):
    x, y, c = _pos()
    return [pltpu.make_async_remote_copy(src_ref=g.at[j, 1 - c], dst_ref=out.at[j], send_sem=send_sems.at[j],
                                         recv_sem=recv_sems.at[j], device_id=(x, y, 1 - c), device_id_type=MESH)
            for j in range(NCHIP)]


def _tn_matmul(xa, dy, name, nb, k1, n, x_batched, dy_mode, tt, ctx_kv=None, carry=None):
    ln = xa.shape[-2]
    tt = min(tt, ln)
    steps = ln // tt
    n_in = 2 + (ctx_kv is not None) + (carry is not None)

    def body(x_ref, dy_ref, *refs):
        o_ref = refs[n_in - 2]
        if carry is not None:
            g_ref, got_ref = refs[n_in - 3], refs[n_in - 1]
            send_sems, recv_sems = refs[n_in:]

        @pl.when(pl.program_id(0) == 0)
        def _():
            if carry is not None:
                for cp in _exchange_copies(g_ref, got_ref, send_sems, recv_sems):
                    cp.start()
            o_ref[...] = jnp.zeros_like(o_ref)
            if ctx_kv is not None:
                for g in range(0, 2 * AW, HD):
                    col = 3 * AW + g
                    o_ref[col // n, :, col % n: col % n + HD] = refs[0][:, g:g + HD]

        xt = None if x_batched else jnp.transpose(x_ref[...])
        for b in range(nb):
            lhs = jnp.transpose(x_ref[b]) if x_batched else xt
            if dy_mode == "batched":
                rhs = dy_ref[b]
            elif dy_mode == "cols":
                rhs = dy_ref[:, b * n:(b + 1) * n]
            else:
                rhs = dy_ref[...]
            o_ref[b] += _dot(lhs, rhs, NN)

        if carry is not None:
            @pl.when(pl.program_id(0) == steps - 1)
            def _():
                cps = _exchange_copies(g_ref, got_ref, send_sems, recv_sems)
                for cp in cps:
                    cp.wait_recv()
                for cp in cps:
                    cp.wait_send()

    x_spec = (pl.BlockSpec((nb, tt, k1), lambda t: (0, t, 0)) if x_batched
              else pl.BlockSpec((tt, k1), lambda t: (t, 0)))
    if dy_mode == "batched":
        dy_spec = pl.BlockSpec((nb, tt, n), lambda t: (0, t, 0))
    elif dy_mode == "cols":
        dy_spec = pl.BlockSpec((tt, nb * n), lambda t: (t, 0))
    else:
        dy_spec = pl.BlockSpec((tt, n), lambda t: (t, 0))
    hbm = pl.BlockSpec(memory_space=pl.ANY)
    extra = [] if ctx_kv is None else [ctx_kv]
    in_specs = [x_spec, dy_spec] + [_const(e.shape) for e in extra]
    out_specs = [pl.BlockSpec((nb, k1, n), lambda t: (0, 0, 0))]
    out_shape = [jax.ShapeDtypeStruct((nb, k1, n), F32)]
    scratch = []
    if carry is not None:
        extra = extra + [carry]
        in_specs.append(hbm)
        out_specs.append(hbm)
        out_shape.append(jax.ShapeDtypeStruct((NCHIP,) + carry.shape[2:], F32))
        scratch = [pltpu.SemaphoreType.DMA((NCHIP,))] * 2
    out = pl.pallas_call(
        body, name=name, grid=(steps,),
        in_specs=in_specs, out_specs=out_specs, out_shape=out_shape, scratch_shapes=scratch,
        compiler_params=_params(60, ("arbitrary",)),
    )(xa, dy, *extra)
    return out[0] if carry is None else (out[0], out[1])


def _add_ctx_cols(gwi, gwkv):
    first = 1536 // HD
    per = WI_C // HD

    def body(g_ref, a_ref, o_ref):
        o_ref[...] = g_ref[...] + a_ref[...]

    spec = pl.BlockSpec((None, D, HD), lambda i: ((first + i) // per, 0, (first + i) % per))
    return pl.pallas_call(
        body, name="add_ctx_cols", grid=(2 * AW // HD,),
        in_specs=[spec, pl.BlockSpec((D, HD), lambda i: (0, i))],
        out_specs=spec,
        out_shape=jax.ShapeDtypeStruct(gwi.shape, F32),
        input_output_aliases={0: 0},
        compiler_params=_params(32, ("arbitrary",)),
    )(gwi, gwkv)


FWD_CHUNKS_PER_STEP = 4
BWD_CHUNKS_PER_STEP = 4


def _chunks_per_step(nc, want):
    return want if nc % want == 0 else 1


def _mixer_fwd(z, cos_t, sin_t, dm, sgw, gain, bfull, scf, scb, gbufs_a, gbufs_b):
    ln = z.shape[0]
    nc = ln // CH
    na = len(gbufs_a)
    gbufs = list(gbufs_a) + list(gbufs_b)
    nt = len(gbufs)
    cps = _chunks_per_step(nc, FWD_CHUNKS_PER_STEP)
    nb = nc // cps
    rows = cps * CH
    mid = nb // 2

    def rev(p, n):
        return p * n + (1 - p) * (nb - 1 - n)

    def col(j, both):
        if both:
            return pl.BlockSpec((rows, AW), lambda p, n: (rev(p, n), j))
        return pl.BlockSpec((rows, AW), lambda p, n: (p * n, j))

    def body(u_ref, v_ref, q_ref, k_ref, vr_ref, gf_ref, gb_ref, cos_ref, sin_ref, dm_ref, sgw_ref, gain_ref,
             bfull_ref, scf_ref, scb_ref, *refs):
        y_ref, sf_ref, sb_ref, of_ref = refs[nt:nt + 4]
        bufs = refs[nt + 4:2 * nt + 4]
        bufs_a, bufs_b = bufs[:na], bufs[na:]
        sb_all, st, a_send, a_recv, bi_send, bi_recv, bd_send, bd_recv = refs[2 * nt + 4:]
        p, n = pl.program_id(0), pl.program_id(1)

        @pl.when((p == 0) & (n == 0))
        def _():
            for cp in _gather_d2d_copies(bufs_a, a_send, a_recv)[0]:
                cp.start()
            for cp in _gather_ici_copies(bufs_b, bi_send, bi_recv)[0]:
                cp.start()

        @pl.when((p == 1) & (n == mid))
        def _():
            for cp in _gather_ici_copies(bufs_b, bi_send, bi_recv)[1]:
                cp.wait_recv()
            for cp in _gather_d2d_copies(bufs_b, bd_send, bd_recv)[0]:
                cp.start()

        def roped_k(r0):
            cos, sin = cos_ref[r0:r0 + CH, :], sin_ref[r0:r0 + CH, :]
            return [_rope(t, cos, sin) * K_SCALE for t in _heads(k_ref, r0)]

        @pl.when(p == 0)
        def _():
            @pl.when(n == 0)
            def _():
                st[...] = scb_ref[...]

            for s in reversed(range(cps)):
                m = (nb - 1 - n) * cps + s
                k, vr = roped_k(s * CH), _heads(vr_ref, s * CH)
                for h in range(NH):
                    sb_all[m, h] = st[h]
                    st[h] = dm_ref[1, 3, h] * st[h] + _mm_tn(k[h], dm_ref[1, 2, h] * vr[h])

        @pl.when(p == 1)
        def _():
            @pl.when(n == 0)
            def _():
                st[...] = scf_ref[...]

            mats = [[dm_ref[d, kind, h] for h in range(NH)] for d in range(2) for kind in range(3)]
            for s in range(cps):
                r0 = s * CH
                m = n * cps + s
                cos, sin = cos_ref[r0:r0 + CH, :], sin_ref[r0:r0 + CH, :]
                q = [_rope(t, cos, sin) for t in _heads(q_ref, r0)]
                k, vr = roped_k(r0), _heads(vr_ref, r0)
                u, v, gf, gb = _heads(u_ref, r0), _heads(v_ref, r0), _heads(gf_ref, r0), _heads(gb_ref, r0)
                cols = [slice(h * HD, (h + 1) * HD) for h in range(NH)]
                ya = [_gate_group(u[h], v[h], sgw_ref[h], gain_ref[:, cols[h]], bfull_ref[h]) for h in range(NH)]
                sf = [st[h] for h in range(NH)]
                sb = [sb_all[m, h] for h in range(NH)]
                ret = [_ret_head(q[h], k[h], vr[h], gf[h], gb[h], sf[h], sb[h], mats[0][h], mats[1][h], mats[2][h],
                                 mats[3][h], mats[4][h], mats[5][h]) for h in range(NH)]
                for h in range(NH):
                    yr, uf, _, of = ret[h]
                    y_ref[r0:r0 + CH, cols[h]] = ya[h].astype(BF)
                    y_ref[r0:r0 + CH, AW + h * HD:AW + (h + 1) * HD] = yr.astype(BF)
                    of_ref[r0:r0 + CH, cols[h]] = of
                    sf_ref[s, h] = sf[h]
                    sb_ref[s, h] = sb[h]
                    st[h] = dm_ref[0, 3, h] * sf[h] + uf

        @pl.when((p == 1) & (n == nb - 1))
        def _():
            a_out, a_in = _gather_d2d_copies(bufs_a, a_send, a_recv)
            b_out, b_in = _gather_d2d_copies(bufs_b, bd_send, bd_recv)
            for cp in a_in + b_in:
                cp.wait_recv()
            for cp in a_out + b_out + _gather_ici_copies(bufs_b, bi_send, bi_recv)[0]:
                cp.wait_send()

    hbm = pl.BlockSpec(memory_space=pl.ANY)
    tab = pl.BlockSpec((rows, HD), lambda p, n: (rev(p, n), 0))
    st_spec = pl.BlockSpec((cps, NH, HD, HD), lambda p, n: (p * n, 0, 0, 0))
    out = pl.pallas_call(
        body, name="mixer_fwd", grid=(2, nb),
        in_specs=[col(0, False), col(1, False), col(2, False), col(3, True), col(4, True), col(5, False),
                  col(6, False), tab, tab, _const((2, 4, NH, CH, CH)), _const((NH, CH, CH)), _const((1, AW)),
                  _const((NH, CH, CH)), _const((NH, HD, HD)), _const((NH, HD, HD))] + [hbm] * nt,
        out_specs=[pl.BlockSpec((rows, D), lambda p, n: (p * n, 0)), st_spec, st_spec,
                   pl.BlockSpec((rows, AW), lambda p, n: (p * n, 0))] + [hbm] * nt,
        out_shape=[jax.ShapeDtypeStruct((ln, D), BF), jax.ShapeDtypeStruct((nc, NH, HD, HD), F32),
                   jax.ShapeDtypeStruct((nc, NH, HD, HD), F32), jax.ShapeDtypeStruct((ln, AW), F32)]
        + [jax.ShapeDtypeStruct(g.shape, g.dtype) for g in gbufs],
        input_output_aliases={15 + k: 4 + k for k in range(nt)},
        scratch_shapes=[pltpu.VMEM((nc, NH, HD, HD), F32), pltpu.VMEM((NH, HD, HD), F32)]
        + [pltpu.SemaphoreType.DMA(((NCHIP - 1) * na,))] * 2
        + [pltpu.SemaphoreType.DMA(((NCHIP - 1) * (nt - na),))] * 4,
        compiler_params=_params(56, ("arbitrary", "arbitrary")),
    )(z, z, z, z, z, z, z, cos_t, sin_t, dm, sgw, gain, bfull, scf, scb, *gbufs)
    return out[0], out[1], out[2], out[3], out[4:]


def _mixer_bwd(z, dycat, of_all, cos_t, sin_t, dm, sgw, gain, bfull, sf_all, sb_all, parts):
    ln = z.shape[0]
    nc = ln // CH
    cps = _chunks_per_step(nc, BWD_CHUNKS_PER_STEP)
    nb = nc // cps
    rows = cps * CH

    def rev(p, n):
        return p * n + (1 - p) * (nb - 1 - n)

    def col(j, both):
        if both:
            return pl.BlockSpec((rows, AW), lambda p, n: (rev(p, n), j))
        return pl.BlockSpec((rows, AW), lambda p, n: (p * n, j))

    nt = len(parts)

    def body(u_ref, v_ref, q_ref, k_ref, vr_ref, gf_ref, gb_ref, dya_ref, dyr_ref, of_ref, cos_ref, sin_ref,
             dm_ref, sgw_ref, gain_ref, bfull_ref, sf_ref, sb_ref, *refs):
        ps = refs[:nt]
        dz_ref, ddm_ref, dsgw_ref, dgain_ref, dbf_ref, dscf_ref, dscb_ref = refs[nt:nt + 7]
        got = refs[nt + 7:2 * nt + 7]
        gf_all, run, send_sems, recv_sems = refs[2 * nt + 7:]
        p, n = pl.program_id(0), pl.program_id(1)

        @pl.when((p == 0) & (n == 0))
        def _():
            for cp in _scatter_ici_copies(ps, got, send_sems, recv_sems):
                cp.start()

        mats = [[dm_ref[d, kind, h] for h in range(NH)] for d in range(2) for kind in range(3)]

        @pl.when(p == 0)
        def _():
            @pl.when(n == 0)
            def _():
                run[...] = jnp.zeros_like(run)
                ddm_ref[...] = jnp.zeros_like(ddm_ref)
                dsgw_ref[...] = jnp.zeros_like(dsgw_ref)
                dgain_ref[...] = jnp.zeros_like(dgain_ref)
                dbf_ref[...] = jnp.zeros_like(dbf_ref)

            for s in reversed(range(cps)):
                r0 = s * CH
                m = (nb - 1 - n) * cps + s
                cos, sin = cos_ref[r0:r0 + CH, :], sin_ref[r0:r0 + CH, :]
                q = [_rope(t, cos, sin) for t in _heads(q_ref, r0)]
                gf, dyr, of = _heads(gf_ref, r0), _heads(dyr_ref, r0), _heads(of_ref, r0)
                for h in range(NH):
                    _, vjp = jax.vjp(functools.partial(_gated_norm, gf[h]), of[h])
                    (dof,) = vjp(dyr[h])
                    dsf = _mm_tn(q[h], mats[1][h] * dof)
                    g_next = run[h]
                    gf_all[m, h] = g_next.astype(BF)
                    ddm_ref[0, 3, h] += sf_ref[s, h] * g_next
                    run[h] = dsf + dm_ref[0, 3, h] * g_next

            @pl.when(n == nb - 1)
            def _():
                dscf_ref[...] = run[...]

        @pl.when(p == 1)
        def _():
            @pl.when(n == 0)
            def _():
                run[...] = jnp.zeros_like(run)

            for s in range(cps):
                r0 = s * CH
                m = n * cps + s
                rw = slice(r0, r0 + CH)
                cos, sin = cos_ref[r0:r0 + CH, :], sin_ref[r0:r0 + CH, :]
                q = [_rope(t, cos, sin) for t in _heads(q_ref, r0)]
                k = [_rope(t, cos, sin) * K_SCALE for t in _heads(k_ref, r0)]
                vr, gf, gb, dyr = _heads(vr_ref, r0), _heads(gf_ref, r0), _heads(gb_ref, r0), _heads(dyr_ref, r0)
                u, v, dya = _heads(u_ref, r0), _heads(v_ref, r0), _heads(dya_ref, r0)
                sf = [sf_ref[s, h] for h in range(NH)]
                sb = [sb_ref[s, h] for h in range(NH)]
                g_f = [gf_all[m, h].astype(F32) for h in range(NH)]
                g_b = [run[h] for h in range(NH)]
                cols = [slice(h * HD, (h + 1) * HD) for h in range(NH)]

                def chunk(u_, v_, sgw_, gain_, bfull_, q_, k_, vr_, gf_, gb_, sb_, df, xf, zf, db, xb, zb, sf=sf):
                    ya = [_gate_group(u_[h], v_[h], sgw_[h], gain_[h], bfull_[h]) for h in range(NH)]
                    ret = [_ret_head(q_[h], k_[h], vr_[h], gf_[h], gb_[h], sf[h], sb_[h], df[h], xf[h], zf[h],
                                     db[h], xb[h], zb[h])[:3] for h in range(NH)]
                    return ya, ret

                _, vjp = jax.vjp(chunk, u, v, [sgw_ref[h] for h in range(NH)], [gain_ref[:, c] for c in cols],
                                 [bfull_ref[h] for h in range(NH)], q, k, vr, gf, gb, sb, *mats)
                (du, dv, dsgw, dgain, dbf, dq, dk, dvr, dgf, dgb, dsb, ddf, dxf, dzf, ddb, dxb, dzb) = vjp(
                    (dya, [(dyr[h], g_f[h], g_b[h]) for h in range(NH)]))
                for h in range(NH):
                    dz_ref[rw, h * HD:(h + 1) * HD] = du[h].astype(BF)
                    dz_ref[rw, AW + h * HD:AW + (h + 1) * HD] = dv[h].astype(BF)
                    dz_ref[rw, 2 * AW + h * HD:2 * AW + (h + 1) * HD] = _rope_bwd(dq[h], cos, sin).astype(BF)
                    dz_ref[rw, 3 * AW + h * HD:3 * AW + (h + 1) * HD] = _rope_bwd(dk[h] * K_SCALE, cos,
                                                                                  sin).astype(BF)
                    dz_ref[rw, 4 * AW + h * HD:4 * AW + (h + 1) * HD] = dvr[h].astype(BF)
                    dz_ref[rw, 5 * AW + h * HD:5 * AW + (h + 1) * HD] = dgf[h].astype(BF)
                    dz_ref[rw, 6 * AW + h * HD:6 * AW + (h + 1) * HD] = dgb[h].astype(BF)
                    ddm_ref[0, 0, h] += ddf[h]
                    ddm_ref[0, 1, h] += dxf[h]
                    ddm_ref[0, 2, h] += dzf[h]
                    ddm_ref[1, 0, h] += ddb[h]
                    ddm_ref[1, 1, h] += dxb[h]
                    ddm_ref[1, 2, h] += dzb[h]
                    ddm_ref[1, 3, h] += sb[h] * g_b[h]
                    dsgw_ref[h] += dsgw[h]
                    dgain_ref[:, cols[h]] += dgain[h]
                    dbf_ref[h] += dbf[h]
                    run[h] = dsb[h] + dm_ref[1, 3, h] * g_b[h]

            @pl.when(n == nb - 1)
            def _():
                dscb_ref[...] = run[...]

        @pl.when((p == 1) & (n == nb - 1))
        def _():
            cps_ = _scatter_ici_copies(ps, got, send_sems, recv_sems)
            for cp in cps_:
                cp.wait_recv()
            for cp in cps_:
                cp.wait_send()

    hbm = pl.BlockSpec(memory_space=pl.ANY)
    tab = pl.BlockSpec((rows, HD), lambda p, n: (rev(p, n), 0))
    tile4 = jax.ShapeDtypeStruct((NH, CH, CH), F32)
    out = pl.pallas_call(
        body, name="mixer_bwd", grid=(2, nb),
        in_specs=[col(0, False), col(1, False), col(2, True), col(3, False), col(4, False), col(5, True),
                  col(6, False),
                  pl.BlockSpec((rows, AW), lambda p, n: (p * n, 0)),
                  pl.BlockSpec((rows, AW), lambda p, n: (rev(p, n), 1)),
                  pl.BlockSpec((rows, AW), lambda p, n: ((1 - p) * (nb - 1 - n), 0)),
                  tab, tab, _const((2, 4, NH, CH, CH)), _const((NH, CH, CH)), _const((1, AW)),
                  _const((NH, CH, CH)),
                  pl.BlockSpec((cps, NH, HD, HD), lambda p, n: (rev(p, n), 0, 0, 0)),
                  pl.BlockSpec((cps, NH, HD, HD), lambda p, n: (p * n, 0, 0, 0))] + [hbm] * nt,
        out_specs=[pl.BlockSpec((rows, IN_COLS), lambda p, n: (p * n, 0)),
                   pl.BlockSpec((2, 4, NH, CH, CH), lambda p, n: (0, 0, 0, 0, 0)),
                   pl.BlockSpec((NH, CH, CH), lambda p, n: (0, 0, 0)),
                   pl.BlockSpec((1, AW), lambda p, n: (0, 0)),
                   pl.BlockSpec((NH, CH, CH), lambda p, n: (0, 0, 0)),
                   pl.BlockSpec((NH, HD, HD), lambda p, n: (0, 0, 0)),
                   pl.BlockSpec((NH, HD, HD), lambda p, n: (0, 0, 0))] + [hbm] * nt,
        out_shape=[jax.ShapeDtypeStruct((ln, IN_COLS), BF), jax.ShapeDtypeStruct((2, 4, NH, CH, CH), F32),
                   tile4, jax.ShapeDtypeStruct((1, AW), F32), tile4, tile4, tile4]
        + [jax.ShapeDtypeStruct((NCHIP - 1,) + p.shape[1:], BF) for p in parts],
        scratch_shapes=[pltpu.VMEM((nc, NH, HD, HD), BF), pltpu.VMEM((NH, HD, HD), F32)]
        + [pltpu.SemaphoreType.DMA(((NCHIP - 1) * nt,))] * 2,
        compiler_params=_params(60, ("arbitrary", "arbitrary")),
    )(z, z, z, z, z, z, z, dycat, dycat, of_all, cos_t, sin_t, dm, sgw, gain, bfull, sf_all, sb_all, *parts)
    return out[:7], out[7:]


def _small_reduce(ddm, ddm_ctx, dm, dbf):
    def body(ddm_ref, dctx_ref, dm_ref, dbf_ref, lg_ref, sgb_ref):
        ex = _decay_exponents()
        ones = jnp.ones((8, CH), F32)
        for d in range(2):
            for h in range(NH):
                tot = jnp.zeros((CH, CH), F32)
                for kind in range(4):
                    g = ddm_ref[d, kind, h]
                    if kind >= 2:
                        g = g + dctx_ref[d, kind - 2, h]
                    tot = tot + g * dm_ref[d, kind, h] * ex[d][kind]
                lg_ref[d * NH + h: d * NH + h + 1, :] = jnp.sum(tot, axis=0, keepdims=True)
        sgb_ref[...] = jnp.zeros_like(sgb_ref)
        for g in range(NH):
            r = lax.dot_general(ones, dbf_ref[g], (NT, ((), ())), precision=HI, preferred_element_type=F32)
            sgb_ref[g:g + 1, :] = r[0:1, :]

    return pl.pallas_call(
        body, name="small_reduce",
        out_shape=[jax.ShapeDtypeStruct((8, CH), F32), jax.ShapeDtypeStruct((8, CH), F32)],
        in_specs=[pl.BlockSpec(memory_space=pltpu.VMEM)] * 4,
        out_specs=[pl.BlockSpec(memory_space=pltpu.VMEM)] * 2,
        compiler_params=_params(32),
    )(ddm, ddm_ctx, dm, dbf)


def _mod_backward(ct_pad_t, cctx_col, dmod_pad, dcmod_cols, w_mod_s):
    def body(ct_ref, cc_ref, dm_ref, dc_ref, w_ref, gw_ref, part_ref):
        dcm = dc_ref[0:1, :]
        for d in range(1, NDEV):
            dcm = dcm + dc_ref[d:d + 1, :]
        gw_ref[...] = (jnp.dot(_silu(ct_ref[...]), dm_ref[...], precision=HI, preferred_element_type=F32)
                       + _silu(cc_ref[...]) * dcm)
        part_ref[...] = lax.dot_general(jnp.broadcast_to(dcm, (8, dcm.shape[1])), w_ref[...], (NT, ((), ())),
                                        precision=HI, preferred_element_type=F32)

    return pl.pallas_call(
        body, name="mod_backward",
        out_shape=[jax.ShapeDtypeStruct(w_mod_s.shape, F32), jax.ShapeDtypeStruct((8, D), F32)],
        in_specs=[pl.BlockSpec(memory_space=pltpu.VMEM)] * 5,
        out_specs=[pl.BlockSpec(memory_space=pltpu.VMEM)] * 2,
        compiler_params=_params(48),
    )(ct_pad_t, cctx_col, dmod_pad, dcmod_cols, w_mod_s)


def _cctx_update(parts, c_ctx, m, v):
    def body(p_ref, c_ref, m_ref, v_ref, g_ref, d_ref, mo_ref, vo_ref):
        tot = ((p_ref[0] + p_ref[2]) + p_ref[4]) + p_ref[6]
        cv = c_ref[...]
        s = jax.nn.sigmoid(cv)
        g = tot * (s * (1.0 + cv * (1.0 - s)))
        g_ref[...] = g
        d_ref[...], mo_ref[...], vo_ref[...] = _adamw_math(cv, g, m_ref[...], v_ref[...])

    return pl.pallas_call(
        body, name="cctx_update",
        out_shape=[jax.ShapeDtypeStruct((1, D), F32)] * 4,
        in_specs=[pl.BlockSpec(memory_space=pltpu.VMEM)] * 4,
        out_specs=[pl.BlockSpec(memory_space=pltpu.VMEM)] * 4,
        compiler_params=_params(16),
    )(parts, c_ctx, m, v)


def _small_update(gathered, wp, mp, vp):
    def body(g_ref, w_ref, m_ref, v_ref, go_ref, d_ref, mo_ref, vo_ref, loss_ref):
        tot = g_ref[0]
        for d in range(1, NDEV):
            tot = tot + g_ref[d]
        go_ref[Q_BMOD:Q_N1, :] = tot[P_DMOD:P_N1, :] + tot[P_DCMOD:P_DMOD, :]
        go_ref[Q_N1:Q_LG, :] = tot[P_N1:P_LG, :]
        lg = jnp.sum(tot[P_LG:P_N2, :], axis=1, keepdims=True)
        go_ref[Q_LG:Q_N2, :] = lg * jax.nn.sigmoid(-w_ref[Q_LG:Q_N2, :])
        go_ref[Q_N2:Q_ROWS, :] = tot[P_N2:P_LOSS, :]
        d_ref[...], mo_ref[...], vo_ref[...] = _adamw_math(w_ref[...], go_ref[...], m_ref[...], v_ref[...])
        ls = jnp.sum(jnp.sum(tot[P_LOSS:P_ROWS, :], axis=1, keepdims=True), axis=0, keepdims=True)
        loss_ref[...] = jnp.broadcast_to(ls, (8, CH))

    return pl.pallas_call(
        body, name="small_update",
        out_shape=[jax.ShapeDtypeStruct((Q_ROWS, CH), F32)] * 4 + [jax.ShapeDtypeStruct((8, CH), F32)],
        in_specs=[pl.BlockSpec(memory_space=pltpu.VMEM)] * 4,
        out_specs=[pl.BlockSpec(memory_space=pltpu.VMEM)] * 5,
        compiler_params=_params(32),
    )(gathered, wp, mp, vp)


def _rows(a):
    r = a.reshape(-1, CH)
    return jnp.pad(r, ((0, -r.shape[0] % 8), (0, 0)))


def _pack_small(b_mod, norm1, sg_gain, sg_w, sg_b, lf, lb, norm2, norm_f):
    lg = jnp.broadcast_to(jnp.concatenate([lf.reshape(NH), lb.reshape(NH)])[:, None], (2 * NH, CH))
    return jnp.concatenate([_rows(b_mod), _rows(norm1), _rows(sg_gain), _rows(sg_w), _rows(sg_b), lg,
                            _rows(norm2), _rows(norm_f)], axis=0)


def _unpack_small(p):
    return (p[Q_BMOD:Q_N1].reshape(1, 6 * D), p[Q_N1:Q_GAIN].reshape(1, D), p[Q_GAIN:Q_GAIN + NH].reshape(1, AW),
            p[Q_SGW:Q_SGB].reshape(1, NH, CH, CH), p[Q_SGB:Q_SGB + NH].reshape(1, NH, CH),
            p[Q_LG:Q_LG + NH, 0].reshape(1, NH), p[Q_LG + NH:Q_N2, 0].reshape(1, NH),
            p[Q_N2:Q_NF].reshape(1, D), p[Q_NF:Q_ROWS].reshape(D))


def _rope_tables(ln):
    pos = np.arange(ln)
    rows = (pos // GRID_W).astype(np.float32)
    cols = (pos % GRID_W).astype(np.float32)
    n_freq = HD // 4
    inv = (np.float32(ROPE_BASE) ** (-np.arange(n_freq, dtype=np.float32) / np.float32(n_freq))).astype(np.float32)
    ar = rows[:, None] * inv[None, :]
    ac = cols[:, None] * inv[None, :]
    cos_t = np.concatenate([np.cos(ar), np.cos(ar), np.cos(ac), np.cos(ac)], axis=1).astype(np.float32)
    sin_t = np.concatenate([-np.sin(ar), np.sin(ar), -np.sin(ac), np.sin(ac)], axis=1).astype(np.float32)
    return jnp.asarray(cos_t), jnp.asarray(sin_t)


def kernel(x, c, ctx, c_ctx, w_mod, b_mod, norm1, w_in, sg_gain, sg_w, sg_b, ret_logit_f, ret_logit_b, w_out, norm2, w_gate, w_up, w_down, norm_f, loss_target, m_c_ctx, m_w_mod, m_b_mod, m_norm1, m_w_in, m_sg_gain, m_sg_w, m_sg_b, m_ret_logit_f, m_ret_logit_b, m_w_out, m_norm2, m_w_gate, m_w_up, m_w_down, m_norm_f, v_c_ctx, v_w_mod, v_b_mod, v_norm1, v_w_in, v_sg_gain, v_sg_w, v_sg_b, v_ret_logit_f, v_ret_logit_b, v_w_out, v_norm2, v_w_gate, v_w_up, v_w_down, v_norm_f):
    ln = x.shape[1]
    xi, yi, ci = _pos()
    chip = 2 * xi + yi
    me = 4 * xi + 2 * yi + ci
    x2d = x.reshape(ln, D)
    tgt = loss_target.reshape(ln, D)
    mod_c = w_mod.shape[2]

    tr = lambda a: jnp.swapaxes(a[0], 0, 1)
    gbufs, c_all, prod_all = _prologue(c, c_ctx.reshape(1, D), w_mod[0],
                                       [w_in[0], w_out[0], tr(w_gate), tr(w_up), w_down[0]])
    wi = gbufs[0].reshape(NCHIP, D, WI_C)
    gbufs_a, gbufs_b = gbufs[1:3], gbufs[3:5]
    c_all = c_all.reshape(NDEV, D)
    prod_chips = prod_all[0::2]
    mod_rows = jnp.transpose(prod_chips, (1, 0, 2)).reshape(16, NCHIP * mod_c) + b_mod
    mod = lax.dynamic_slice_in_dim(mod_rows, me, 1, axis=0)
    cmod = mod_rows[8:9]
    sh1, sc1, g1, sh2, sc2, g2 = [mod[:, i * D:(i + 1) * D] for i in range(6)]
    csh1, csc1 = cmod[:, 0:D], cmod[:, D:2 * D]
    zrow = jnp.zeros((1, D), F32)
    vec_in = jnp.concatenate([norm1, sh1, sc1] + [zrow] * 5, axis=0)
    vec_ctx = jnp.concatenate([norm1, csh1, csc1] + [zrow] * 5, axis=0)
    vec_post = jnp.concatenate([g1, norm2, sh2, sc2, g2, norm_f.reshape(1, D), zrow, zrow], axis=0)

    logits = jnp.concatenate([ret_logit_f.reshape(NH), ret_logit_b.reshape(NH)])
    dm = _decay_mats(jnp.broadcast_to(logits[:, None, None], (2 * NH, CH, CH)))
    ctx2d = ctx.reshape(ctx.shape[1], D)
    scf, scb = _ctx_forward(ctx2d, vec_ctx, wi, dm)

    cos_t, sin_t = _rope_tables(ln)
    z, hx, gbufs_a = _in_proj(x2d, vec_in, wi, gbufs_a)
    bfull = jnp.broadcast_to(sg_b[0][:, :, None], (NH, CH, CH))
    ycat, sf_all, sb_all, of_all, gbufs = _mixer_fwd(z, cos_t, sin_t, dm, sg_w[0], sg_gain, bfull, scf, scb,
                                             gbufs_a, gbufs_b)
    wo, wg_t, wu_t, wd = [g.reshape(-1, D) for g in gbufs]

    dx1, dycat, h2, dy, df, act, da, db, acc_post = _post_mixer(x2d, ycat, tgt, vec_post, wo, wg_t, wu_t, wd)

    cidx = ci.reshape(1).astype(jnp.int32)
    chipidx = chip.reshape(1).astype(jnp.int32)

    def halves_summed(full, names):
        full = [g.reshape(NCHIP, 2, g.shape[1] // 2, g.shape[2]) for g in full]
        from_sib = _rs_exchange_halves(full, "rs_exchange_" + names[0])
        return [_rs_add_halves(g, r, cidx, "rs_add_halves_" + nm) for g, r, nm in zip(full, from_sib, names)]

    def split(g):
        return g.reshape(NCHIP, 2, g.shape[1] // (2 * NCHIP), g.shape[2])

    g_wd = split(_tn_matmul(act, df, "grad_w_down", 1, DFF, D, False, "shared", 1024))
    g_wu, x_wd = _tn_matmul(db, h2, "grad_w_up", 1, DFF, D, False, "shared", 1024, carry=g_wd)
    g_wu = split(g_wu)
    g_wg, x_wu = _tn_matmul(da, h2, "grad_w_gate", 1, DFF, D, False, "shared", 1024, carry=g_wu)
    g_wg = split(g_wg)
    g_wo, x_wg = _tn_matmul(ycat, dy, "grad_w_out", 1, D, D, False, "shared", 1024, carry=g_wg)
    g_wo = split(g_wo)
    x_wo = _rs_exchange_halves([g_wo], "rs_exchange_w_out")[0]
    names = ["w_in", "w_out", "w_gate", "w_up", "w_down"]
    sums_b = [_rs_add_halves(g, r, cidx, "rs_add_halves_" + nm)
              for g, r, nm in zip([g_wo, g_wg, g_wu, g_wd], [x_wo, x_wg, x_wu, x_wd], names[1:])]

    (dz, ddm, dsgw, dgain, dbf, dscf, dscb), from_chips_b = _mixer_bwd(
        z, dycat, of_all, cos_t, sin_t, dm, sg_w[0], sg_gain, bfull, sf_all, sb_all, [s[1] for s in sums_b])
    gwkv, acc_ctx, ddm_ctx = _ctx_backward(ctx2d, vec_ctx, wi, dm, dscf, dscb)
    g_wi = _tn_matmul(hx, dz, "grad_w_in", NCHIP, D, WI_C, False, "cols", 512, ctx_kv=gwkv)
    sums_a = halves_summed([g_wi], names[:1])
    lg_part, dsgb = _small_reduce(ddm, ddm_ctx, dm, dbf)
    dcmod = jnp.concatenate([acc_ctx[1:2], acc_ctx[2:3], jnp.zeros((1, 4 * D), F32)], axis=1)
    dmod_rest = jnp.concatenate([acc_post[0:1], acc_post[2:3], acc_post[3:4], acc_post[4:5]], axis=1)
    early = jnp.concatenate([_rows(dcmod), _rows(dmod_rest), _rows(dgain), _rows(dsgw), dsgb, lg_part,
                             _rows(acc_post[1:2]), _rows(acc_post[5:6]), _rows(acc_post[6:7])], axis=0)
    gx, acc_in, from_chips_a, early_all = _in_proj_bwd(dz, x2d, dx1, vec_in, wi, [s[1] for s in sums_a], early)

    sums = sums_a + sums_b
    from_chips = list(from_chips_a) + list(from_chips_b)
    finals = [_rs_add_chips(s[0], r, chipidx, "rs_add_chips_" + nm) for s, r, nm in zip(sums, from_chips, names)]
    others = _rs_share_final(finals)

    late = jnp.concatenate([_rows(acc_in[1:2]), _rows(acc_in[2:3]), _rows(acc_in[0:1] + acc_ctx[0:1])], axis=0)
    late_all = _allgather_small(late, "gather_small")
    n_dc, n_l = P_DMOD - P_DCMOD, 16
    gathered = jnp.concatenate([early_all[:, :n_dc], late_all[:, :n_l], early_all[:, n_dc:n_dc + 32],
                                late_all[:, n_l:], early_all[:, n_dc + 32:]], axis=1)
    dmod_all = gathered[:, P_DMOD:P_N1].reshape(NDEV, 6 * D)
    dcmod_all = gathered[:, P_DCMOD:P_DMOD].reshape(NDEV, 6 * D)
    dmod_cols = lax.dynamic_slice_in_dim(dmod_all, chip * mod_c, mod_c, axis=1)
    dcmod_cols = lax.dynamic_slice_in_dim(dcmod_all, chip * mod_c, mod_c, axis=1)
    dmod_pad = jnp.concatenate([dmod_cols, jnp.zeros((CH - NDEV, mod_c), F32)], axis=0)
    ct_pad_t = jnp.concatenate([jnp.transpose(c_all), jnp.zeros((D, CH - NDEV), F32)], axis=1)
    g_wmod, cctx_part = _mod_backward(ct_pad_t, c_ctx.reshape(D, 1), dmod_pad, dcmod_cols, w_mod[0])
    parts = _allgather_small(cctx_part[0:1], "gather_cctx")
    g_cctx, d_cctx, nm_cctx, nv_cctx = _cctx_update(parts, c_ctx.reshape(1, D), m_c_ctx.reshape(1, D),
                                                    v_c_ctx.reshape(1, D))

    wp = _pack_small(b_mod, norm1, sg_gain, sg_w, sg_b, ret_logit_f, ret_logit_b, norm2, norm_f)
    mp = _pack_small(m_b_mod, m_norm1, m_sg_gain, m_sg_w, m_sg_b, m_ret_logit_f, m_ret_logit_b, m_norm2, m_norm_f)
    vp = _pack_small(v_b_mod, v_norm1, v_sg_gain, v_sg_w, v_sg_b, v_ret_logit_f, v_ret_logit_b, v_norm2, v_norm_f)
    gp, dp, mp2, vp2, loss_t = _small_update(gathered, wp, mp, vp)

    big_w = [w_in[0], w_out[0], tr(w_gate), tr(w_up), w_down[0]]
    big_m = [m_w_in[0], m_w_out[0], tr(m_w_gate), tr(m_w_up), m_w_down[0]]
    big_v = [v_w_in[0], v_w_out[0], tr(v_w_gate), tr(v_w_up), v_w_down[0]]
    upd = [_adamw_halves(w, own, oth, m, v, cidx, "adamw_" + nm) for w, own, oth, m, v, nm in
           zip(big_w, finals, others, big_m, big_v, names)]
    big_g = [g_wmod] + [u[0] for u in upd]
    big = [_adamw(w_mod[0], g_wmod, m_w_mod[0], v_w_mod[0], "adamw_w_mod")] + [u[1:] for u in upd]

    def assemble(small, cctx, bigs):
        b_mod_, norm1_, gain_, sgw_, sgb_, lf_, lb_, norm2_, normf_ = _unpack_small(small)
        wm, wi_, wo_, wg_, wu_, wd_ = [b[None] for b in bigs]
        wg_, wu_ = jnp.swapaxes(wg_, 1, 2), jnp.swapaxes(wu_, 1, 2)
        return [cctx.reshape(D), wm, b_mod_, norm1_, wi_, gain_, sgw_, sgb_, lf_, lb_, wo_, norm2_, wg_, wu_, wd_,
                normf_]

    out = [loss_t[0, 0], gx.reshape(1, ln, D)]
    out += assemble(gp, g_cctx, big_g)
    out += assemble(dp, d_cctx, [b[0] for b in big])
    out += assemble(mp2, nm_cctx, [b[1] for b in big])
    out += assemble(vp2, nv_cctx, [b[2] for b in big])
    return tuple(out)
```

```python
import functools

import jax
import jax.numpy as jnp
import numpy as np
from jax import lax
from jax.experimental import pallas as pl
from jax.experimental.pallas import tpu as pltpu

F32 = jnp.float32
BF = jnp.bfloat16
MESH = pl.DeviceIdType.MESH

D = 1024
CH = 128
HD = 128
NH = 4
AW = 512
IN_COLS = 3584
DFF = 2816
NCHIP = 4
NDEV = 8
WI_C = IN_COLS // NCHIP
FF_C = DFF // NCHIP
WO_R = D // NCHIP
EPS = 1e-6
GRID_W = 64
ROPE_BASE = 10000.0
K_SCALE = HD ** -0.5
LR, B1, B2, AEPS, WD, STEP = 0.001, 0.9, 0.999, 1e-08, 0.01, 10
VMEM_MB = 1 << 20
HI = lax.Precision.HIGHEST

P_DCMOD, P_DMOD, P_N1, P_GAIN, P_SGW, P_SGB, P_LG, P_N2, P_NF, P_LOSS = 0, 48, 96, 104, 112, 624, 632, 640, 648, 656
P_ROWS = 664
Q_BMOD, Q_N1, Q_GAIN, Q_SGW, Q_SGB, Q_LG, Q_N2, Q_NF = 0, 48, 56, 64, 576, 584, 592, 600
Q_ROWS = 608


def _params(vmem_mb, sem=None):
    return pltpu.CompilerParams(vmem_limit_bytes=vmem_mb * VMEM_MB, dimension_semantics=sem)


def _const(shape):
    nd = len(shape)
    return pl.BlockSpec(shape, lambda *_: (0,) * nd, pipeline_mode=pl.Buffered(1))


def _pos():
    return lax.axis_index("x"), lax.axis_index("y"), lax.axis_index("c")


def _dot(a, b, dims):
    return lax.dot_general(a, b, (dims, ((), ())), preferred_element_type=F32)


NN = ((1,), (0,))
NT = ((1,), (1,))
TN = ((0,), (0,))


@jax.custom_vjp
def _mm(a, b):
    return _dot(a.astype(BF), b.astype(BF), NN)


def _mm_f(a, b):
    return _mm(a, b), (a.astype(BF), b.astype(BF))


def _mm_b(res, g):
    a, b = res
    gb = g.astype(BF)
    return _dot(gb, b, NT), _dot(a, gb, TN)


_mm.defvjp(_mm_f, _mm_b)


@jax.custom_vjp
def _mm_nt(a, b):
    return _dot(a.astype(BF), b.astype(BF), NT)


def _mm_nt_f(a, b):
    return _mm_nt(a, b), (a.astype(BF), b.astype(BF))


def _mm_nt_b(res, g):
    a, b = res
    gb = g.astype(BF)
    return _dot(gb, b, NN), _dot(gb, a, TN)


_mm_nt.defvjp(_mm_nt_f, _mm_nt_b)


@jax.custom_vjp
def _mm_tn(a, b):
    return _dot(a.astype(BF), b.astype(BF), TN)


def _mm_tn_f(a, b):
    return _mm_tn(a, b), (a.astype(BF), b.astype(BF))


def _mm_tn_b(res, g):
    a, b = res
    gb = g.astype(BF)
    return _dot(b, gb, NT), _dot(a, gb, NN)


_mm_tn.defvjp(_mm_tn_f, _mm_tn_b)


def _gelu(x):
    return x * (0.5 * (1.0 + jnp.tanh(0.7978845608028654 * (x + 0.044715 * (x * x * x)))))


def _silu(x):
    return x * jax.nn.sigmoid(x)


def _rms(x):
    return lax.rsqrt(jnp.mean(x * x, axis=-1, keepdims=True) + EPS)


def _swap32(t):
    lane = lax.broadcasted_iota(jnp.int32, t.shape, 1)
    first = (lane % 64) < 32
    return jnp.where(first, pltpu.roll(t, 96, 1), pltpu.roll(t, 32, 1))


def _rope(t, cos, sin):
    return t * cos + _swap32(t) * sin


def _rope_bwd(d, cos, sin):
    return d * cos + _swap32(d * sin)


def _heads(ref, r0=0):
    return [ref[r0:r0 + CH, h * HD:(h + 1) * HD].astype(F32) for h in range(NH)]


def _gate_group(u, v, sgw, gain, bfull):
    gv = _gelu(v)
    return _gelu(u) * (_mm(sgw, gv * _rms(gv) * gain) + bfull)


def _gated_norm(gate, o):
    return _silu(gate) * (o * _rms(o))


def _ret_head(q, k, vr, gf, gb, sf, sb, df, xf, zf, db, xb, zb):
    a = _mm_nt(q, k)
    of = _mm(a * df, vr) + xf * _mm(q, sf)
    ob = _mm(a * db, vr) + xb * _mm(q, sb)
    return _gated_norm(gf, of) + _gated_norm(gb, ob), _mm_tn(k, zf * vr), _mm_tn(k, zb * vr), of


def _ctx_states(ctx0, ctx1, n1, csh, csc, wk, wv, zf, zb, ef, eb):
    hc0 = (ctx0 * _rms(ctx0) * n1) * (1.0 + csc) + csh
    hc1 = (ctx1 * _rms(ctx1) * n1) * (1.0 + csc) + csh
    scf, scb = [], []
    for h in range(NH):
        k0, k1 = _mm(hc0, wk[h]) * K_SCALE, _mm(hc1, wk[h]) * K_SCALE
        v0, v1 = _mm(hc0, wv[h]), _mm(hc1, wv[h])
        scf.append(ef[h] * _mm_tn(k0, zf[h] * v0) + _mm_tn(k1, zf[h] * v1))
        scb.append(eb[h] * _mm_tn(k1, zb[h] * v1) + _mm_tn(k0, zb[h] * v0))
    return scf, scb


def _allgather_small(v, name):
    r, n = v.shape

    def body(v_ref, out_ref, send_sems, recv_sems, local_sem):
        x, y, c = _pos()
        me = 4 * x + 2 * y + c
        mine = pltpu.make_async_copy(v_ref, out_ref.at[me], local_sem)
        mine.start()
        sent = []
        for k in range(1, NDEV):
            kx, ky, kc = (k >> 2) & 1, (k >> 1) & 1, k & 1
            peer = (x ^ kx, y ^ ky, c ^ kc)
            cp = pltpu.make_async_remote_copy(src_ref=v_ref, dst_ref=out_ref.at[me], send_sem=send_sems.at[k - 1],
                                              recv_sem=recv_sems.at[k - 1], device_id=peer, device_id_type=MESH)
            cp.start()
            sent.append(cp)
        for k in range(1, NDEV):
            kx, ky, kc = (k >> 2) & 1, (k >> 1) & 1, k & 1
            peer = (x ^ kx, y ^ ky, c ^ kc)
            src = 4 * (x ^ kx) + 2 * (y ^ ky) + (c ^ kc)
            pltpu.make_async_remote_copy(src_ref=v_ref, dst_ref=out_ref.at[src], send_sem=send_sems.at[k - 1],
                                         recv_sem=recv_sems.at[k - 1], device_id=peer, device_id_type=MESH).wait_recv()
        for cp in sent:
            cp.wait_send()
        mine.wait()

    return pl.pallas_call(
        body, name=name,
        out_shape=jax.ShapeDtypeStruct((NDEV, r, n), F32),
        in_specs=[pl.BlockSpec(memory_space=pltpu.VMEM)],
        out_specs=pl.BlockSpec(memory_space=pltpu.VMEM),
        scratch_shapes=[pltpu.SemaphoreType.DMA((NDEV - 1,)), pltpu.SemaphoreType.DMA((NDEV - 1,)),
                        pltpu.SemaphoreType.DMA],
        compiler_params=_params(16),
    )(v)


def _chip_offsets():
    return [((k >> 1) & 1, k & 1) for k in range(1, NCHIP)]


def _prologue(c, c_ctx, w_mod_s, shards):
    nt = len(shards)
    shapes = [s.shape for s in shards]
    mod_c = w_mod_s.shape[1]

    def body(*refs):
        c_ref, cc_ref, wm_ref = refs[:3]
        srcs = refs[3:3 + nt]
        outs = refs[3 + nt:3 + 2 * nt]
        call_ref, prod_ref = refs[3 + 2 * nt:5 + 2 * nt]
        stages = refs[5 + 2 * nt:5 + 3 * nt]
        ct = refs[5 + 3 * nt]
        c_send, c_recv, p_send, p_recv, ici_send, ici_recv, d2d_send, d2d_recv, local_sems = refs[6 + 3 * nt:]
        x, y, c = _pos()
        chip = 2 * x + y
        me = 4 * x + 2 * y + c
        sib = (x, y, 1 - c)
        pending = []
        for t in range(nt):
            half = shapes[t][0] // 2
            stages[t][0] = srcs[t][0:half, :].astype(BF)
            stages[t][1] = srcs[t][half:2 * half, :].astype(BF)
            cp = pltpu.make_async_copy(stages[t], outs[t].at[chip], local_sems.at[t])
            cp.start()
            pending.append(cp)
        sends = []
        for k, (kx, ky) in enumerate(_chip_offsets()):
            cp = pltpu.make_async_remote_copy(src_ref=stages[0].at[c], dst_ref=outs[0].at[chip, c],
                                              send_sem=ici_send.at[k], recv_sem=ici_recv.at[k],
                                              device_id=(x ^ kx, y ^ ky, c), device_id_type=MESH)
            cp.start()
            sends.append(cp)

        def to_all(src, dst_of, send_sems, recv_sems):
            for k in range(1, NDEV):
                kx, ky, kc = (k >> 2) & 1, (k >> 1) & 1, k & 1
                cp = pltpu.make_async_remote_copy(src_ref=src, dst_ref=dst_of(me), send_sem=send_sems.at[k - 1],
                                                  recv_sem=recv_sems.at[k - 1], device_id=(x ^ kx, y ^ ky, c ^ kc),
                                                  device_id_type=MESH)
                cp.start()
                sends.append(cp)
            for k in range(1, NDEV):
                kx, ky, kc = (k >> 2) & 1, (k >> 1) & 1, k & 1
                frm = 4 * (x ^ kx) + 2 * (y ^ ky) + (c ^ kc)
                pltpu.make_async_remote_copy(src_ref=src, dst_ref=dst_of(frm), send_sem=send_sems.at[k - 1],
                                             recv_sem=recv_sems.at[k - 1], device_id=(x ^ kx, y ^ ky, c ^ kc),
                                             device_id_type=MESH).wait_recv()

        call_ref[me] = c_ref[...]
        to_all(c_ref, lambda d: call_ref.at[d], c_send, c_recv)
        ct[...] = jnp.zeros_like(ct)
        for d in range(NDEV):
            ct[d:d + 1, :] = call_ref[d]
        ct[NDEV:NDEV + 1, :] = cc_ref[...]
        prod_ref[me] = jnp.dot(_silu(ct[...]), wm_ref[...], precision=HI, preferred_element_type=F32)
        to_all(prod_ref.at[me], lambda d: prod_ref.at[d], p_send, p_recv)

        for k, (kx, ky) in enumerate(_chip_offsets()):
            frm = 2 * (x ^ kx) + (y ^ ky)
            pltpu.make_async_remote_copy(src_ref=stages[0].at[c], dst_ref=outs[0].at[frm, c],
                                         send_sem=ici_send.at[k], recv_sem=ici_recv.at[k],
                                         device_id=(x ^ kx, y ^ ky, c), device_id_type=MESH).wait_recv()
            cp = pltpu.make_async_remote_copy(src_ref=outs[0].at[frm, c], dst_ref=outs[0].at[frm, c],
                                              send_sem=d2d_send.at[k], recv_sem=d2d_recv.at[k],
                                              device_id=sib, device_id_type=MESH)
            cp.start()
            sends.append(cp)
        for k, (kx, ky) in enumerate(_chip_offsets()):
            frm = 2 * (x ^ kx) + (y ^ ky)
            pltpu.make_async_remote_copy(src_ref=stages[0].at[c], dst_ref=outs[0].at[frm, 1 - c],
                                         send_sem=d2d_send.at[k], recv_sem=d2d_recv.at[k],
                                         device_id=sib, device_id_type=MESH).wait_recv()
        for cp in sends:
            cp.wait_send()
        for cp in pending:
            cp.wait()

    vm = pl.BlockSpec(memory_space=pltpu.VMEM)
    out = pl.pallas_call(
        body, name="prologue",
        out_shape=[jax.ShapeDtypeStruct((NCHIP, 2, r // 2, cc), BF) for r, cc in shapes]
        + [jax.ShapeDtypeStruct((NDEV, 1, D), F32), jax.ShapeDtypeStruct((NDEV, 16, mod_c), F32)],
        in_specs=[vm] * (3 + nt),
        out_specs=[pl.BlockSpec(memory_space=pl.ANY)] * nt + [vm, vm],
        scratch_shapes=[pltpu.VMEM((2, r // 2, cc), BF) for r, cc in shapes] + [pltpu.VMEM((16, D), F32)]
        + [pltpu.SemaphoreType.DMA((NDEV - 1,))] * 4 + [pltpu.SemaphoreType.DMA((NCHIP - 1,))] * 4
        + [pltpu.SemaphoreType.DMA((nt,))],
        compiler_params=_params(56),
    )(c, c_ctx, w_mod_s, *shards)
    return out[:nt], out[nt], out[nt + 1]


def _gather_ici_copies(bufs, send_sems, recv_sems):
    x, y, c = _pos()
    chip = 2 * x + y
    nt = len(bufs)
    out_cp, in_cp = [], []
    for k, (kx, ky) in enumerate(_chip_offsets()):
        frm = 2 * (x ^ kx) + (y ^ ky)
        for t in range(nt):
            s = k * nt + t
            peer = (x ^ kx, y ^ ky, c)
            out_cp.append(pltpu.make_async_remote_copy(
                src_ref=bufs[t].at[chip, c], dst_ref=bufs[t].at[chip, c], send_sem=send_sems.at[s],
                recv_sem=recv_sems.at[s], device_id=peer, device_id_type=MESH))
            in_cp.append(pltpu.make_async_remote_copy(
                src_ref=bufs[t].at[chip, c], dst_ref=bufs[t].at[frm, c], send_sem=send_sems.at[s],
                recv_sem=recv_sems.at[s], device_id=peer, device_id_type=MESH))
    return out_cp, in_cp


def _gather_d2d_copies(bufs, send_sems, recv_sems):
    x, y, c = _pos()
    nt = len(bufs)
    out_cp, in_cp = [], []
    for k, (kx, ky) in enumerate(_chip_offsets()):
        frm = 2 * (x ^ kx) + (y ^ ky)
        for t in range(nt):
            s = k * nt + t
            out_cp.append(pltpu.make_async_remote_copy(
                src_ref=bufs[t].at[frm, c], dst_ref=bufs[t].at[frm, c], send_sem=send_sems.at[s],
                recv_sem=recv_sems.at[s], device_id=(x, y, 1 - c), device_id_type=MESH))
            in_cp.append(pltpu.make_async_remote_copy(
                src_ref=bufs[t].at[frm, c], dst_ref=bufs[t].at[frm, 1 - c], send_sem=send_sems.at[s],
                recv_sem=recv_sems.at[s], device_id=(x, y, 1 - c), device_id_type=MESH))
    return out_cp, in_cp


def _scatter_ici_copies(parts, outs, send_sems, recv_sems):
    x, y, c = _pos()
    nt = len(parts)
    cps = []
    for k, (kx, ky) in enumerate(_chip_offsets()):
        dst_chip = 2 * (x ^ kx) + (y ^ ky)
        for t in range(nt):
            s = k * nt + t
            cps.append(pltpu.make_async_remote_copy(
                src_ref=parts[t].at[dst_chip], dst_ref=outs[t].at[k], send_sem=send_sems.at[s],
                recv_sem=recv_sems.at[s], device_id=(x ^ kx, y ^ ky, c), device_id_type=MESH))
    return cps


def _rs_exchange_halves(grads, name):
    nt = len(grads)
    shapes = [g.shape for g in grads]

    def body(*refs):
        gs, outs = refs[:nt], refs[nt:2 * nt]
        send_sems, recv_sems = refs[2 * nt:]
        x, y, c = _pos()
        sib = (x, y, 1 - c)
        sent = []
        for t in range(nt):
            for j in range(NCHIP):
                s = t * NCHIP + j
                cp = pltpu.make_async_remote_copy(src_ref=gs[t].at[j, 1 - c], dst_ref=outs[t].at[j],
                                                  send_sem=send_sems.at[s], recv_sem=recv_sems.at[s],
                                                  device_id=sib, device_id_type=MESH)
                cp.start()
                sent.append(cp)
        for cp in sent:
            cp.wait_recv()
        for cp in sent:
            cp.wait_send()

    return pl.pallas_call(
        body, name=name,
        out_shape=[jax.ShapeDtypeStruct((NCHIP, s[2], s[3]), F32) for s in shapes],
        in_specs=[pl.BlockSpec(memory_space=pl.ANY)] * nt,
        out_specs=[pl.BlockSpec(memory_space=pl.ANY)] * nt,
        scratch_shapes=[pltpu.SemaphoreType.DMA((nt * NCHIP,))] * 2,
    )(*grads)


def _rs_share_final(finals):
    nt = len(finals)
    shapes = [f.shape for f in finals]

    def body(*refs):
        fs, outs = refs[:nt], refs[nt:2 * nt]
        send_sems, recv_sems = refs[2 * nt:]
        x, y, c = _pos()
        sent = []
        for t in range(nt):
            cp = pltpu.make_async_remote_copy(src_ref=fs[t], dst_ref=outs[t], send_sem=send_sems.at[t],
                                              recv_sem=recv_sems.at[t], device_id=(x, y, 1 - c), device_id_type=MESH)
            cp.start()
            sent.append(cp)
        for cp in sent:
            cp.wait_recv()
        for cp in sent:
            cp.wait_send()

    return pl.pallas_call(
        body, name="rs_share_final",
        out_shape=[jax.ShapeDtypeStruct(s, F32) for s in shapes],
        in_specs=[pl.BlockSpec(memory_space=pl.ANY)] * nt,
        out_specs=[pl.BlockSpec(memory_space=pl.ANY)] * nt,
        scratch_shapes=[pltpu.SemaphoreType.DMA((nt,))] * 2,
    )(*finals)


def _row_tile(h, cc=D):
    for t in (512, 384, 352, 256, 176, 128, 64, 32, 16):
        if h % t == 0 and t * cc * 4 <= (5 * VMEM_MB) // 4:
            return t
    return h


def _rs_add_halves(g, recv, where, name):
    _, _, h, cc = g.shape
    th = _row_tile(h, cc)

    def body(w_ref, g_ref, r_ref, own_ref, ob_ref):
        s = g_ref[...] + r_ref[...]
        ob_ref[...] = s.astype(BF)

        @pl.when(pl.program_id(1) == w_ref[1])
        def _():
            own_ref[...] = s

    return pl.pallas_call(
        body, name=name,
        grid_spec=pltpu.PrefetchScalarGridSpec(
            num_scalar_prefetch=1, grid=(h // th, NCHIP),
            in_specs=[pl.BlockSpec((None, None, th, cc), lambda i, j, w_ref: (j, w_ref[0], i, 0)),
                      pl.BlockSpec((None, th, cc), lambda i, j, w_ref: (j, i, 0))],
            out_specs=[pl.BlockSpec((th, cc), lambda i, j, w_ref: (i, 0)),
                       pl.BlockSpec((None, th, cc), lambda i, j, w_ref: (j, i, 0))]),
        out_shape=[jax.ShapeDtypeStruct((h, cc), F32), jax.ShapeDtypeStruct((NCHIP, h, cc), BF)],
        compiler_params=_params(48, ("arbitrary", "arbitrary")),
    )(where, g, recv)


def _rs_add_chips(own, recv, name):
    h, cc = own.shape
    th = _row_tile(h, cc)

    def body(o_ref, r_ref, out_ref):
        out_ref[...] = ((o_ref[...] + r_ref[0].astype(F32)) + r_ref[1].astype(F32)) + r_ref[2].astype(F32)

    return pl.pallas_call(
        body, name=name, grid=(h // th,),
        in_specs=[pl.BlockSpec((th, cc), lambda i: (i, 0)), pl.BlockSpec((NCHIP - 1, th, cc), lambda i: (0, i, 0))],
        out_specs=pl.BlockSpec((th, cc), lambda i: (i, 0)),
        out_shape=jax.ShapeDtypeStruct((h, cc), F32),
        compiler_params=_params(48, ("parallel",)),
    )(own, recv)


def _adamw_math(w, g, m, v):
    m2 = B1 * m + (1.0 - B1) * g
    v2 = B2 * v + (1.0 - B2) * (g * g)
    m_hat = m2 / (1.0 - B1 ** STEP)
    v_hat = v2 / (1.0 - B2 ** STEP)
    delta = -LR * (m_hat / (jnp.sqrt(v_hat) + AEPS) + WD * w)
    return delta, m2, v2


def _adamw(w, g, m, v, name):
    r, cc = w.shape
    tr = _row_tile(r, cc)

    def body(w_ref, g_ref, m_ref, v_ref, d_ref, mo_ref, vo_ref):
        d, m2, v2 = _adamw_math(w_ref[...], g_ref[...], m_ref[...], v_ref[...])
        d_ref[...] = d
        mo_ref[...] = m2
        vo_ref[...] = v2

    spec = pl.BlockSpec((tr, cc), lambda i: (i, 0))
    return pl.pallas_call(
        body, name=name, grid=(r // tr,), in_specs=[spec] * 4, out_specs=[spec] * 3,
        out_shape=[jax.ShapeDtypeStruct((r, cc), F32)] * 3,
        compiler_params=_params(48, ("parallel",)),
    )(w, g, m, v)


def _adamw_halves(w, own, other, m, v, cidx, name):
    r, cc = w.shape
    h = r // 2
    tr = _row_tile(h, cc)
    per = h // tr

    def body(c_ref, w_ref, own_ref, oth_ref, m_ref, v_ref, g_ref, d_ref, mo_ref, vo_ref):
        mine = (pl.program_id(0) // per) == c_ref[0]
        g = jnp.where(mine, own_ref[...], oth_ref[...])
        g_ref[...] = g
        d, m2, v2 = _adamw_math(w_ref[...], g, m_ref[...], v_ref[...])
        d_ref[...] = d
        mo_ref[...] = m2
        vo_ref[...] = v2

    full = pl.BlockSpec((tr, cc), lambda i, c_ref: (i, 0))
    half = pl.BlockSpec((tr, cc), lambda i, c_ref: (i % per, 0))
    return pl.pallas_call(
        body, name=name,
        grid_spec=pltpu.PrefetchScalarGridSpec(
            num_scalar_prefetch=1, grid=(r // tr,),
            in_specs=[full, half, half, full, full], out_specs=[full] * 4),
        out_shape=[jax.ShapeDtypeStruct((r, cc), F32)] * 4,
        compiler_params=_params(48, ("parallel",)),
    )(cidx, w, own, other, m, v)


def _decay_exponents():
    ri = lax.broadcasted_iota(jnp.int32, (CH, CH), 0).astype(F32)
    ci = lax.broadcasted_iota(jnp.int32, (CH, CH), 1).astype(F32)
    full = jnp.full((CH, CH), float(CH), F32)
    return [[ri - ci, ri + 1.0, (CH - 1.0) - ri, full], [ci - ri, CH - ri, ri, full]]


def _decay_mats(logit_full):
    def body(l_ref, o_ref):
        ex = _decay_exponents()
        for d in range(2):
            for h in range(NH):
                lv = l_ref[d * NH + h]
                lg = jnp.minimum(lv, 0.0) - jnp.log(1.0 + jnp.exp(-jnp.abs(lv)))
                for kind in range(4):
                    m = jnp.exp(lg * ex[d][kind])
                    if kind == 0:
                        m = jnp.where(ex[d][0] >= 0.0, jnp.exp(lg * jnp.maximum(ex[d][0], 0.0)), 0.0)
                    o_ref[d, kind, h] = m

    return pl.pallas_call(
        body, name="decay_mats",
        out_shape=jax.ShapeDtypeStruct((2, 4, NH, CH, CH), F32),
        in_specs=[pl.BlockSpec(memory_space=pltpu.VMEM)],
        out_specs=pl.BlockSpec(memory_space=pltpu.VMEM),
        compiler_params=_params(32),
    )(logit_full)


def _ctx_kv_weights(wi_ref):
    def cols(g):
        return wi_ref[g // WI_C, :, g % WI_C: g % WI_C + HD].astype(F32)

    wk = [cols(3 * AW + h * HD) for h in range(NH)]
    wv = [cols(4 * AW + h * HD) for h in range(NH)]
    return wk, wv


def _ctx_forward(ctx, vecs, wi, dm):
    def body(ctx_ref, v_ref, wi_ref, dm_ref, scf_ref, scb_ref):
        wk, wv = _ctx_kv_weights(wi_ref)
        mats = [[dm_ref[d, kind, h] for h in range(NH)] for d in range(2) for kind in (2, 3)]
        scf, scb = _ctx_states(ctx_ref[0:CH, :], ctx_ref[CH:2 * CH, :], v_ref[0:1, :], v_ref[1:2, :],
                               v_ref[2:3, :], wk, wv, mats[0], mats[2], mats[1], mats[3])
        for h in range(NH):
            scf_ref[h] = scf[h]
            scb_ref[h] = scb[h]

    return pl.pallas_call(
        body, name="ctx_forward",
        out_shape=[jax.ShapeDtypeStruct((NH, HD, HD), F32)] * 2,
        in_specs=[pl.BlockSpec(memory_space=pltpu.VMEM)] * 4,
        out_specs=[pl.BlockSpec(memory_space=pltpu.VMEM)] * 2,
        compiler_params=_params(48),
    )(ctx, vecs, wi, dm)


def _ctx_backward(ctx, vecs, wi, dm, dscf, dscb):
    def body(ctx_ref, v_ref, wi_ref, dm_ref, gf_ref, gb_ref, gw_ref, gv_ref, gdm_ref):
        wk, wv = _ctx_kv_weights(wi_ref)
        mats = [[dm_ref[d, kind, h] for h in range(NH)] for d in range(2) for kind in (2, 3)]
        ctx0, ctx1 = ctx_ref[0:CH, :], ctx_ref[CH:2 * CH, :]

        def fn(n1, csh, csc, wk_, wv_, zf, zb, ef, eb):
            return _ctx_states(ctx0, ctx1, n1, csh, csc, wk_, wv_, zf, zb, ef, eb)

        _, vjp = jax.vjp(fn, v_ref[0:1, :], v_ref[1:2, :], v_ref[2:3, :], wk, wv,
                         mats[0], mats[2], mats[1], mats[3])
        cot = ([gf_ref[h] for h in range(NH)], [gb_ref[h] for h in range(NH)])
        dn1, dcsh, dcsc, dwk, dwv, dzf, dzb, def_, deb = vjp(cot)
        for h in range(NH):
            gw_ref[:, h * HD:(h + 1) * HD] = dwk[h]
            gw_ref[:, AW + h * HD:AW + (h + 1) * HD] = dwv[h]
        gv_ref[...] = jnp.zeros_like(gv_ref)
        gv_ref[0:1, :] = dn1
        gv_ref[1:2, :] = dcsh
        gv_ref[2:3, :] = dcsc
        for h in range(NH):
            gdm_ref[0, 0, h] = dzf[h]
            gdm_ref[0, 1, h] = def_[h]
            gdm_ref[1, 0, h] = dzb[h]
            gdm_ref[1, 1, h] = deb[h]

    return pl.pallas_call(
        body, name="ctx_backward",
        out_shape=[jax.ShapeDtypeStruct((D, 2 * AW), F32), jax.ShapeDtypeStruct((8, D), F32),
                   jax.ShapeDtypeStruct((2, 2, NH, CH, CH), F32)],
        in_specs=[pl.BlockSpec(memory_space=pltpu.VMEM)] * 6,
        out_specs=[pl.BlockSpec(memory_space=pltpu.VMEM)] * 3,
        compiler_params=_params(56),
    )(ctx, vecs, wi, dm, dscf, dscb)


def _in_proj(x, vecs, wi, gbufs):
    ln = x.shape[0]
    t = min(512, ln)
    nt = len(gbufs)
    steps = ln // t

    def body(x_ref, v_ref, wi_ref, *refs):
        z_ref, hx_ref = refs[nt:nt + 2]
        bufs = refs[nt + 2:2 * nt + 2]
        send_sems, recv_sems = refs[2 * nt + 2:]
        i = pl.program_id(0)

        @pl.when(i == 0)
        def _():
            for cp in _gather_ici_copies(bufs, send_sems, recv_sems)[0]:
                cp.start()

        xv = x_ref[...]
        hx = (xv * _rms(xv) * v_ref[0:1, :]) * (1.0 + v_ref[2:3, :]) + v_ref[1:2, :]
        hb = hx.astype(BF)
        hx_ref[...] = hb
        for j in range(NCHIP):
            z_ref[:, j * WI_C:(j + 1) * WI_C] = _dot(hb, wi_ref[j], NN)

        @pl.when(i == steps - 1)
        def _():
            out_cp, in_cp = _gather_ici_copies(bufs, send_sems, recv_sems)
            for cp in in_cp:
                cp.wait_recv()
            for cp in out_cp:
                cp.wait_send()

    hbm = pl.BlockSpec(memory_space=pl.ANY)
    out = pl.pallas_call(
        body, name="in_proj", grid=(steps,),
        in_specs=[pl.BlockSpec((t, D), lambda i: (i, 0)), _const((8, D)), _const((NCHIP, D, WI_C))] + [hbm] * nt,
        out_specs=[pl.BlockSpec((t, IN_COLS), lambda i: (i, 0)), pl.BlockSpec((t, D), lambda i: (i, 0))] + [hbm] * nt,
        out_shape=[jax.ShapeDtypeStruct((ln, IN_COLS), F32), jax.ShapeDtypeStruct((ln, D), BF)]
        + [jax.ShapeDtypeStruct(g.shape, g.dtype) for g in gbufs],
        input_output_aliases={3 + k: 2 + k for k in range(nt)},
        scratch_shapes=[pltpu.SemaphoreType.DMA(((NCHIP - 1) * nt,))] * 2,
        compiler_params=_params(56, ("arbitrary",)),
    )(x, vecs, wi, *gbufs)
    return out[0], out[1], out[2:]


def _allgather_copies(src, out, send_sems, recv_sems, local_sem):
    x, y, c = _pos()
    me = 4 * x + 2 * y + c
    sends, recvs = [], []
    for k in range(1, NDEV):
        kx, ky, kc = (k >> 2) & 1, (k >> 1) & 1, k & 1
        peer = (x ^ kx, y ^ ky, c ^ kc)
        frm = 4 * (x ^ kx) + 2 * (y ^ ky) + (c ^ kc)
        sends.append(pltpu.make_async_remote_copy(src_ref=src, dst_ref=out.at[me], send_sem=send_sems.at[k - 1],
                                                  recv_sem=recv_sems.at[k - 1], device_id=peer, device_id_type=MESH))
        recvs.append(pltpu.make_async_remote_copy(src_ref=src, dst_ref=out.at[frm], send_sem=send_sems.at[k - 1],
                                                  recv_sem=recv_sems.at[k - 1], device_id=peer, device_id_type=MESH))
    return sends, recvs, pltpu.make_async_copy(src, out.at[me], local_sem)


def _in_proj_bwd(dz, x, dx1, vecs, wi, parts, early):
    ln = x.shape[0]
    t = min(512, ln)
    nt = len(parts)
    steps = ln // t

    def body(dz_ref, x_ref, dx1_ref, v_ref, wi_ref, *refs):
        ps = refs[:nt]
        early_ref = refs[nt]
        gx_ref, acc_ref = refs[nt + 1:nt + 3]
        got = refs[nt + 3:2 * nt + 3]
        early_all = refs[2 * nt + 3]
        send_sems, recv_sems, ag_send, ag_recv, ag_local = refs[2 * nt + 4:]

        @pl.when(pl.program_id(0) == 0)
        def _():
            acc_ref[...] = jnp.zeros_like(acc_ref)
            for cp in _scatter_ici_copies(ps, got, send_sems, recv_sems):
                cp.start()
            sends, _, own = _allgather_copies(early_ref, early_all, ag_send, ag_recv, ag_local)
            own.start()
            for cp in sends:
                cp.start()

        dhx = jnp.zeros((t, D), F32)
        for j in range(NCHIP):
            dhx = dhx + _dot(dz_ref[:, j * WI_C:(j + 1) * WI_C], wi_ref[j], NT)
        xv = x_ref[...]
        r = _rms(xv)
        xn = xv * r
        n1, sc = v_ref[0:1, :], v_ref[2:3, :]
        acc_ref[0:1, :] += jnp.sum(dhx * xn * (1.0 + sc), axis=0, keepdims=True)
        acc_ref[1:2, :] += jnp.sum(dhx, axis=0, keepdims=True)
        acc_ref[2:3, :] += jnp.sum(dhx * xn * n1, axis=0, keepdims=True)
        g = dhx * n1 * (1.0 + sc)
        gx_ref[...] = dx1_ref[...] + r * (g - xn * jnp.mean(g * xn, axis=-1, keepdims=True))

        @pl.when(pl.program_id(0) == steps - 1)
        def _():
            cps = _scatter_ici_copies(ps, got, send_sems, recv_sems)
            sends, recvs, own = _allgather_copies(early_ref, early_all, ag_send, ag_recv, ag_local)
            for cp in cps + recvs:
                cp.wait_recv()
            for cp in cps + sends:
                cp.wait_send()
            own.wait()

    hbm = pl.BlockSpec(memory_space=pl.ANY)
    out = pl.pallas_call(
        body, name="in_proj_bwd", grid=(steps,),
        in_specs=[pl.BlockSpec((t, IN_COLS), lambda i: (i, 0)), pl.BlockSpec((t, D), lambda i: (i, 0)),
                  pl.BlockSpec((t, D), lambda i: (i, 0)), _const((8, D)), _const((NCHIP, D, WI_C))]
        + [hbm] * (nt + 1),
        out_specs=[pl.BlockSpec((t, D), lambda i: (i, 0)), pl.BlockSpec((8, D), lambda i: (0, 0))]
        + [hbm] * (nt + 1),
        out_shape=[jax.ShapeDtypeStruct((ln, D), F32), jax.ShapeDtypeStruct((8, D), F32)]
        + [jax.ShapeDtypeStruct((NCHIP - 1,) + p.shape[1:], BF) for p in parts]
        + [jax.ShapeDtypeStruct((NDEV,) + early.shape, F32)],
        scratch_shapes=[pltpu.SemaphoreType.DMA(((NCHIP - 1) * nt,))] * 2
        + [pltpu.SemaphoreType.DMA((NDEV - 1,))] * 2 + [pltpu.SemaphoreType.DMA],
        compiler_params=_params(56, ("arbitrary",)),
    )(dz, x, dx1, vecs, wi, *parts, early)
    return out[0], out[1], out[2:2 + nt], out[2 + nt]


def _post_mixer(x, ycat, tgt, vecs, wo, wg, wu, wd):
    ln = x.shape[0]
    t = min(256, ln)

    def body(x_ref, y_ref, t_ref, v_ref, wo_ref, wg_ref, wu_ref, wd_ref,
             dx1_ref, dyc_ref, h2_ref, dy_ref, df_ref, act_ref, da_ref, db_ref, acc_ref, a_st, b_st):
        @pl.when(pl.program_id(0) == 0)
        def _():
            acc_ref[...] = jnp.zeros_like(acc_ref)

        g1, n2, sh2, sc2 = v_ref[0:1, :], v_ref[1:2, :], v_ref[2:3, :], v_ref[3:4, :]
        g2, nf = v_ref[4:5, :], v_ref[5:6, :]
        y = _dot(y_ref[...], wo_ref[...], NN)
        x1 = x_ref[...] + g1 * y
        r2 = _rms(x1)
        xn2 = x1 * r2
        t2 = xn2 * n2
        h2b = (t2 * (1.0 + sc2) + sh2).astype(BF)
        h2_ref[...] = h2b
        a = _dot(h2b, wg_ref[...], NT)
        b = _dot(h2b, wu_ref[...], NT)
        a_st[...] = a
        b_st[...] = b
        act = (_silu(a) * b).astype(BF)
        act_ref[...] = act
        f = _dot(act, wd_ref[...], NN)
        x2 = x1 + g2 * f
        r3 = _rms(x2)
        xn3 = x2 * r3
        e = xn3 * nf - t_ref[...]
        acc_ref[6:7, :] += jnp.sum(e * e, axis=0, keepdims=True) * (0.5 / D)
        dout = e * (1.0 / D)
        acc_ref[5:6, :] += jnp.sum(dout * xn3, axis=0, keepdims=True)
        gg = dout * nf
        dx2 = r3 * (gg - xn3 * jnp.mean(gg * xn3, axis=-1, keepdims=True))
        acc_ref[4:5, :] += jnp.sum(dx2 * f, axis=0, keepdims=True)
        dfb = (g2 * dx2).astype(BF)
        df_ref[...] = dfb
        dact = _dot(dfb, wd_ref[...], NT)
        a = a_st[...]
        b = b_st[...]
        s = jax.nn.sigmoid(a)
        da = (dact * b * (s * (1.0 + a * (1.0 - s)))).astype(BF)
        db = (dact * (a * s)).astype(BF)
        da_ref[...] = da
        db_ref[...] = db
        dh2 = _dot(da, wg_ref[...], NN) + _dot(db, wu_ref[...], NN)
        acc_ref[2:3, :] += jnp.sum(dh2, axis=0, keepdims=True)
        acc_ref[3:4, :] += jnp.sum(dh2 * t2, axis=0, keepdims=True)
        acc_ref[1:2, :] += jnp.sum(dh2 * xn2 * (1.0 + sc2), axis=0, keepdims=True)
        gx = dh2 * n2 * (1.0 + sc2)
        dx1 = dx2 + r2 * (gx - xn2 * jnp.mean(gx * xn2, axis=-1, keepdims=True))
        dx1_ref[...] = dx1
        acc_ref[0:1, :] += jnp.sum(dx1 * y, axis=0, keepdims=True)
        dyb = (g1 * dx1).astype(BF)
        dy_ref[...] = dyb
        dyc_ref[...] = _dot(dyb, wo_ref[...], NT)

    tok = pl.BlockSpec((t, D), lambda i: (i, 0))
    ffb = pl.BlockSpec((t, DFF), lambda i: (i, 0))
    return pl.pallas_call(
        body, name="post_mixer", grid=(ln // t,),
        in_specs=[tok, tok, tok, _const((8, D)), _const((D, D)), _const((DFF, D)), _const((DFF, D)),
                  _const((DFF, D))],
        out_specs=[tok, tok, tok, tok, tok, ffb, ffb, ffb, pl.BlockSpec((16, D), lambda i: (0, 0))],
        out_shape=[jax.ShapeDtypeStruct((ln, D), F32)] * 2 + [jax.ShapeDtypeStruct((ln, D), BF)] * 3
        + [jax.ShapeDtypeStruct((ln, DFF), BF)] * 3 + [jax.ShapeDtypeStruct((16, D), F32)],
        scratch_shapes=[pltpu.VMEM((t, DFF), F32)] * 2,
        compiler_params=_params(60, ("arbitrary",)),
    )(x, ycat, tgt, vecs, wo, wg, wu, wd)


def _exchange_copies(g, out, send_sems, recv_sems):
    x, y, c = _pos()
    return [pltpu.make_async_remote_copy(src_ref=g.at[j, 1 - c], dst_ref=out.at[j], send_sem=send_sems.at[j],
                                         recv_sem=recv_sems.at[j], device_id=(x, y, 1 - c), device_id_type=MESH)
            for j in range(NCHIP)]


def _tn_matmul(xa, dy, name, nb, k1, n, x_batched, dy_mode, tt, ctx_kv=None, carry=None):
    ln = xa.shape[-2]
    tt = min(tt, ln)
    steps = ln // tt
    n_in = 2 + (ctx_kv is not None) + (carry is not None)

    def body(x_ref, dy_ref, *refs):
        o_ref = refs[n_in - 2]
        if carry is not None:
            g_ref, got_ref = refs[n_in - 3], refs[n_in - 1]
            send_sems, recv_sems = refs[n_in:]

        @pl.when(pl.program_id(0) == 0)
        def _():
            if carry is not None:
                for cp in _exchange_copies(g_ref, got_ref, send_sems, recv_sems):
                    cp.start()
            o_ref[...] = jnp.zeros_like(o_ref)
            if ctx_kv is not None:
                for g in range(0, 2 * AW, HD):
                    col = 3 * AW + g
                    o_ref[col // n, :, col % n: col % n + HD] = refs[0][:, g:g + HD]

        xt = None if x_batched else jnp.transpose(x_ref[...])
        for b in range(nb):
            lhs = jnp.transpose(x_ref[b]) if x_batched else xt
            if dy_mode == "batched":
                rhs = dy_ref[b]
            elif dy_mode == "cols":
                rhs = dy_ref[:, b * n:(b + 1) * n]
            else:
                rhs = dy_ref[...]
            o_ref[b] += _dot(lhs, rhs, NN)

        if carry is not None:
            @pl.when(pl.program_id(0) == steps - 1)
            def _():
                cps = _exchange_copies(g_ref, got_ref, send_sems, recv_sems)
                for cp in cps:
                    cp.wait_recv()
                for cp in cps:
                    cp.wait_send()

    x_spec = (pl.BlockSpec((nb, tt, k1), lambda t: (0, t, 0)) if x_batched
              else pl.BlockSpec((tt, k1), lambda t: (t, 0)))
    if dy_mode == "batched":
        dy_spec = pl.BlockSpec((nb, tt, n), lambda t: (0, t, 0))
    elif dy_mode == "cols":
        dy_spec = pl.BlockSpec((tt, nb * n), lambda t: (t, 0))
    else:
        dy_spec = pl.BlockSpec((tt, n), lambda t: (t, 0))
    hbm = pl.BlockSpec(memory_space=pl.ANY)
    extra = [] if ctx_kv is None else [ctx_kv]
    in_specs = [x_spec, dy_spec] + [_const(e.shape) for e in extra]
    out_specs = [pl.BlockSpec((nb, k1, n), lambda t: (0, 0, 0))]
    out_shape = [jax.ShapeDtypeStruct((nb, k1, n), F32)]
    scratch = []
    if carry is not None:
        extra = extra + [carry]
        in_specs.append(hbm)
        out_specs.append(hbm)
        out_shape.append(jax.ShapeDtypeStruct((NCHIP,) + carry.shape[2:], F32))
        scratch = [pltpu.SemaphoreType.DMA((NCHIP,))] * 2
    out = pl.pallas_call(
        body, name=name, grid=(steps,),
        in_specs=in_specs, out_specs=out_specs, out_shape=out_shape, scratch_shapes=scratch,
        compiler_params=_params(60, ("arbitrary",)),
    )(xa, dy, *extra)
    return out[0] if carry is None else (out[0], out[1])


FWD_CHUNKS_PER_STEP = 4
BWD_CHUNKS_PER_STEP = 4


def _chunks_per_step(nc, want):
    return want if nc % want == 0 else 1


def _mixer_fwd(z, cos_t, sin_t, dm, sgw, gain, bfull, scf, scb, gbufs_a, gbufs_b):
    ln = z.shape[0]
    nc = ln // CH
    na = len(gbufs_a)
    gbufs = list(gbufs_a) + list(gbufs_b)
    nt = len(gbufs)
    cps = _chunks_per_step(nc, FWD_CHUNKS_PER_STEP)
    nb = nc // cps
    rows = cps * CH
    mid = nb // 2

    def rev(p, n):
        return p * n + (1 - p) * (nb - 1 - n)

    def col(j, both):
        if both:
            return pl.BlockSpec((rows, AW), lambda p, n: (rev(p, n), j))
        return pl.BlockSpec((rows, AW), lambda p, n: (p * n, j))

    def body(u_ref, v_ref, q_ref, k_ref, vr_ref, gf_ref, gb_ref, cos_ref, sin_ref, dm_ref, sgw_ref, gain_ref,
             bfull_ref, scf_ref, scb_ref, *refs):
        y_ref, sf_ref, sb_ref, of_ref = refs[nt:nt + 4]
        bufs = refs[nt + 4:2 * nt + 4]
        bufs_a, bufs_b = bufs[:na], bufs[na:]
        sb_all, st, a_send, a_recv, bi_send, bi_recv, bd_send, bd_recv = refs[2 * nt + 4:]
        p, n = pl.program_id(0), pl.program_id(1)

        @pl.when((p == 0) & (n == 0))
        def _():
            for cp in _gather_d2d_copies(bufs_a, a_send, a_recv)[0]:
                cp.start()
            for cp in _gather_ici_copies(bufs_b, bi_send, bi_recv)[0]:
                cp.start()

        @pl.when((p == 1) & (n == mid))
        def _():
            for cp in _gather_ici_copies(bufs_b, bi_send, bi_recv)[1]:
                cp.wait_recv()
            for cp in _gather_d2d_copies(bufs_b, bd_send, bd_recv)[0]:
                cp.start()

        def roped_k(r0):
            cos, sin = cos_ref[r0:r0 + CH, :], sin_ref[r0:r0 + CH, :]
            return [_rope(t, cos, sin) * K_SCALE for t in _heads(k_ref, r0)]

        @pl.when(p == 0)
        def _():
            @pl.when(n == 0)
            def _():
                st[...] = scb_ref[...]

            for s in reversed(range(cps)):
                m = (nb - 1 - n) * cps + s
                k, vr = roped_k(s * CH), _heads(vr_ref, s * CH)
                for h in range(NH):
                    sb_all[m, h] = st[h]
                    st[h] = dm_ref[1, 3, h] * st[h] + _mm_tn(k[h], dm_ref[1, 2, h] * vr[h])

        @pl.when(p == 1)
        def _():
            @pl.when(n == 0)
            def _():
                st[...] = scf_ref[...]

            mats = [[dm_ref[d, kind, h] for h in range(NH)] for d in range(2) for kind in range(3)]
            for s in range(cps):
                r0 = s * CH
                m = n * cps + s
                cos, sin = cos_ref[r0:r0 + CH, :], sin_ref[r0:r0 + CH, :]
                q = [_rope(t, cos, sin) for t in _heads(q_ref, r0)]
                k, vr = roped_k(r0), _heads(vr_ref, r0)
                u, v, gf, gb = _heads(u_ref, r0), _heads(v_ref, r0), _heads(gf_ref, r0), _heads(gb_ref, r0)
                cols = [slice(h * HD, (h + 1) * HD) for h in range(NH)]
                ya = [_gate_group(u[h], v[h], sgw_ref[h], gain_ref[:, cols[h]], bfull_ref[h]) for h in range(NH)]
                sf = [st[h] for h in range(NH)]
                sb = [sb_all[m, h] for h in range(NH)]
                ret = [_ret_head(q[h], k[h], vr[h], gf[h], gb[h], sf[h], sb[h], mats[0][h], mats[1][h], mats[2][h],
                                 mats[3][h], mats[4][h], mats[5][h]) for h in range(NH)]
                for h in range(NH):
                    yr, uf, _, of = ret[h]
                    y_ref[r0:r0 + CH, cols[h]] = ya[h].astype(BF)
                    y_ref[r0:r0 + CH, AW + h * HD:AW + (h + 1) * HD] = yr.astype(BF)
                    of_ref[r0:r0 + CH, cols[h]] = of
                    sf_ref[s, h] = sf[h]
                    sb_ref[s, h] = sb[h]
                    st[h] = dm_ref[0, 3, h] * sf[h] + uf

        @pl.when((p == 1) & (n == nb - 1))
        def _():
            a_out, a_in = _gather_d2d_copies(bufs_a, a_send, a_recv)
            b_out, b_in = _gather_d2d_copies(bufs_b, bd_send, bd_recv)
            for cp in a_in + b_in:
                cp.wait_recv()
            for cp in a_out + b_out + _gather_ici_copies(bufs_b, bi_send, bi_recv)[0]:
                cp.wait_send()

    hbm = pl.BlockSpec(memory_space=pl.ANY)
    tab = pl.BlockSpec((rows, HD), lambda p, n: (rev(p, n), 0))
    st_spec = pl.BlockSpec((cps, NH, HD, HD), lambda p, n: (p * n, 0, 0, 0))
    out = pl.pallas_call(
        body, name="mixer_fwd", grid=(2, nb),
        in_specs=[col(0, False), col(1, False), col(2, False), col(3, True), col(4, True), col(5, False),
                  col(6, False), tab, tab, _const((2, 4, NH, CH, CH)), _const((NH, CH, CH)), _const((1, AW)),
                  _const((NH, CH, CH)), _const((NH, HD, HD)), _const((NH, HD, HD))] + [hbm] * nt,
        out_specs=[pl.BlockSpec((rows, D), lambda p, n: (p * n, 0)), st_spec, st_spec,
                   pl.BlockSpec((rows, AW), lambda p, n: (p * n, 0))] + [hbm] * nt,
        out_shape=[jax.ShapeDtypeStruct((ln, D), BF), jax.ShapeDtypeStruct((nc, NH, HD, HD), F32),
                   jax.ShapeDtypeStruct((nc, NH, HD, HD), F32), jax.ShapeDtypeStruct((ln, AW), F32)]
        + [jax.ShapeDtypeStruct(g.shape, g.dtype) for g in gbufs],
        input_output_aliases={15 + k: 4 + k for k in range(nt)},
        scratch_shapes=[pltpu.VMEM((nc, NH, HD, HD), F32), pltpu.VMEM((NH, HD, HD), F32)]
        + [pltpu.SemaphoreType.DMA(((NCHIP - 1) * na,))] * 2
        + [pltpu.SemaphoreType.DMA(((NCHIP - 1) * (nt - na),))] * 4,
        compiler_params=_params(56, ("arbitrary", "arbitrary")),
    )(z, z, z, z, z, z, z, cos_t, sin_t, dm, sgw, gain, bfull, scf, scb, *gbufs)
    return out[0], out[1], out[2], out[3], out[4:]


def _mixer_bwd(z, dycat, of_all, cos_t, sin_t, dm, sgw, gain, bfull, sf_all, sb_all, parts):
    ln = z.shape[0]
    nc = ln // CH
    cps = _chunks_per_step(nc, BWD_CHUNKS_PER_STEP)
    nb = nc // cps
    rows = cps * CH

    def rev(p, n):
        return p * n + (1 - p) * (nb - 1 - n)

    def col(j, both):
        if both:
            return pl.BlockSpec((rows, AW), lambda p, n: (rev(p, n), j))
        return pl.BlockSpec((rows, AW), lambda p, n: (p * n, j))

    nt = len(parts)

    def body(u_ref, v_ref, q_ref, k_ref, vr_ref, gf_ref, gb_ref, dya_ref, dyr_ref, of_ref, cos_ref, sin_ref,
             dm_ref, sgw_ref, gain_ref, bfull_ref, sf_ref, sb_ref, *refs):
        ps = refs[:nt]
        dz_ref, ddm_ref, dsgw_ref, dgain_ref, dbf_ref, dscf_ref, dscb_ref = refs[nt:nt + 7]
        got = refs[nt + 7:2 * nt + 7]
        gf_all, run, send_sems, recv_sems = refs[2 * nt + 7:]
        p, n = pl.program_id(0), pl.program_id(1)

        @pl.when((p == 0) & (n == 0))
        def _():
            for cp in _scatter_ici_copies(ps, got, send_sems, recv_sems):
                cp.start()

        mats = [[dm_ref[d, kind, h] for h in range(NH)] for d in range(2) for kind in range(3)]

        @pl.when(p == 0)
        def _():
            @pl.when(n == 0)
            def _():
                run[...] = jnp.zeros_like(run)
                ddm_ref[...] = jnp.zeros_like(ddm_ref)
                dsgw_ref[...] = jnp.zeros_like(dsgw_ref)
                dgain_ref[...] = jnp.zeros_like(dgain_ref)
                dbf_ref[...] = jnp.zeros_like(dbf_ref)

            for s in reversed(range(cps)):
                r0 = s * CH
                m = (nb - 1 - n) * cps + s
                cos, sin = cos_ref[r0:r0 + CH, :], sin_ref[r0:r0 + CH, :]
                q = [_rope(t, cos, sin) for t in _heads(q_ref, r0)]
                gf, dyr, of = _heads(gf_ref, r0), _heads(dyr_ref, r0), _heads(of_ref, r0)
                for h in range(NH):
                    _, vjp = jax.vjp(functools.partial(_gated_norm, gf[h]), of[h])
                    (dof,) = vjp(dyr[h])
                    dsf = _mm_tn(q[h], mats[1][h] * dof)
                    g_next = run[h]
                    gf_all[m, h] = g_next.astype(BF)
                    ddm_ref[0, 3, h] += sf_ref[s, h] * g_next
                    run[h] = dsf + dm_ref[0, 3, h] * g_next

            @pl.when(n == nb - 1)
            def _():
                dscf_ref[...] = run[...]

        @pl.when(p == 1)
        def _():
            @pl.when(n == 0)
            def _():
                run[...] = jnp.zeros_like(run)

            for s in range(cps):
                r0 = s * CH
                m = n * cps + s
                rw = slice(r0, r0 + CH)
                cos, sin = cos_ref[r0:r0 + CH, :], sin_ref[r0:r0 + CH, :]
                q = [_rope(t, cos, sin) for t in _heads(q_ref, r0)]
                k = [_rope(t, cos, sin) * K_SCALE for t in _heads(k_ref, r0)]
                vr, gf, gb, dyr = _heads(vr_ref, r0), _heads(gf_ref, r0), _heads(gb_ref, r0), _heads(dyr_ref, r0)
                u, v, dya = _heads(u_ref, r0), _heads(v_ref, r0), _heads(dya_ref, r0)
                sf = [sf_ref[s, h] for h in range(NH)]
                sb = [sb_ref[s, h] for h in range(NH)]
                g_f = [gf_all[m, h].astype(F32) for h in range(NH)]
                g_b = [run[h] for h in range(NH)]
                cols = [slice(h * HD, (h + 1) * HD) for h in range(NH)]

                def chunk(u_, v_, sgw_, gain_, bfull_, q_, k_, vr_, gf_, gb_, sb_, df, xf, zf, db, xb, zb, sf=sf):
                    ya = [_gate_group(u_[h], v_[h], sgw_[h], gain_[h], bfull_[h]) for h in range(NH)]
                    ret = [_ret_head(q_[h], k_[h], vr_[h], gf_[h], gb_[h], sf[h], sb_[h], df[h], xf[h], zf[h],
                                     db[h], xb[h], zb[h])[:3] for h in range(NH)]
                    return ya, ret

                _, vjp = jax.vjp(chunk, u, v, [sgw_ref[h] for h in range(NH)], [gain_ref[:, c] for c in cols],
                                 [bfull_ref[h] for h in range(NH)], q, k, vr, gf, gb, sb, *mats)
                (du, dv, dsgw, dgain, dbf, dq, dk, dvr, dgf, dgb, dsb, ddf, dxf, dzf, ddb, dxb, dzb) = vjp(
                    (dya, [(dyr[h], g_f[h], g_b[h]) for h in range(NH)]))
                for h in range(NH):
                    dz_ref[rw, h * HD:(h + 1) * HD] = du[h].astype(BF)
                    dz_ref[rw, AW + h * HD:AW + (h + 1) * HD] = dv[h].astype(BF)
                    dz_ref[rw, 2 * AW + h * HD:2 * AW + (h + 1) * HD] = _rope_bwd(dq[h], cos, sin).astype(BF)
                    dz_ref[rw, 3 * AW + h * HD:3 * AW + (h + 1) * HD] = _rope_bwd(dk[h] * K_SCALE, cos,
                                                                                  sin).astype(BF)
                    dz_ref[rw, 4 * AW + h * HD:4 * AW + (h + 1) * HD] = dvr[h].astype(BF)
                    dz_ref[rw, 5 * AW + h * HD:5 * AW + (h + 1) * HD] = dgf[h].astype(BF)
                    dz_ref[rw, 6 * AW + h * HD:6 * AW + (h + 1) * HD] = dgb[h].astype(BF)
                    ddm_ref[0, 0, h] += ddf[h]
                    ddm_ref[0, 1, h] += dxf[h]
                    ddm_ref[0, 2, h] += dzf[h]
                    ddm_ref[1, 0, h] += ddb[h]
                    ddm_ref[1, 1, h] += dxb[h]
                    ddm_ref[1, 2, h] += dzb[h]
                    ddm_ref[1, 3, h] += sb[h] * g_b[h]
                    dsgw_ref[h] += dsgw[h]
                    dgain_ref[:, cols[h]] += dgain[h]
                    dbf_ref[h] += dbf[h]
                    run[h] = dsb[h] + dm_ref[1, 3, h] * g_b[h]

            @pl.when(n == nb - 1)
            def _():
                dscb_ref[...] = run[...]

        @pl.when((p == 1) & (n == nb - 1))
        def _():
            cps_ = _scatter_ici_copies(ps, got, send_sems, recv_sems)
            for cp in cps_:
                cp.wait_recv()
            for cp in cps_:
                cp.wait_send()

    hbm = pl.BlockSpec(memory_space=pl.ANY)
    tab = pl.BlockSpec((rows, HD), lambda p, n: (rev(p, n), 0))
    tile4 = jax.ShapeDtypeStruct((NH, CH, CH), F32)
    out = pl.pallas_call(
        body, name="mixer_bwd", grid=(2, nb),
        in_specs=[col(0, False), col(1, False), col(2, True), col(3, False), col(4, False), col(5, True),
                  col(6, False),
                  pl.BlockSpec((rows, AW), lambda p, n: (p * n, 0)),
                  pl.BlockSpec((rows, AW), lambda p, n: (rev(p, n), 1)),
                  pl.BlockSpec((rows, AW), lambda p, n: ((1 - p) * (nb - 1 - n), 0)),
                  tab, tab, _const((2, 4, NH, CH, CH)), _const((NH, CH, CH)), _const((1, AW)),
                  _const((NH, CH, CH)),
                  pl.BlockSpec((cps, NH, HD, HD), lambda p, n: (rev(p, n), 0, 0, 0)),
                  pl.BlockSpec((cps, NH, HD, HD), lambda p, n: (p * n, 0, 0, 0))] + [hbm] * nt,
        out_specs=[pl.BlockSpec((rows, IN_COLS), lambda p, n: (p * n, 0)),
                   pl.BlockSpec((2, 4, NH, CH, CH), lambda p, n: (0, 0, 0, 0, 0)),
                   pl.BlockSpec((NH, CH, CH), lambda p, n: (0, 0, 0)),
                   pl.BlockSpec((1, AW), lambda p, n: (0, 0)),
                   pl.BlockSpec((NH, CH, CH), lambda p, n: (0, 0, 0)),
                   pl.BlockSpec((NH, HD, HD), lambda p, n: (0, 0, 0)),
                   pl.BlockSpec((NH, HD, HD), lambda p, n: (0, 0, 0))] + [hbm] * nt,
        out_shape=[jax.ShapeDtypeStruct((ln, IN_COLS), BF), jax.ShapeDtypeStruct((2, 4, NH, CH, CH), F32),
                   tile4, jax.ShapeDtypeStruct((1, AW), F32), tile4, tile4, tile4]
        + [jax.ShapeDtypeStruct((NCHIP - 1,) + p.shape[1:], BF) for p in parts],
        scratch_shapes=[pltpu.VMEM((nc, NH, HD, HD), BF), pltpu.VMEM((NH, HD, HD), F32)]
        + [pltpu.SemaphoreType.DMA(((NCHIP - 1) * nt,))] * 2,
        compiler_params=_params(60, ("arbitrary", "arbitrary")),
    )(z, z, z, z, z, z, z, dycat, dycat, of_all, cos_t, sin_t, dm, sgw, gain, bfull, sf_all, sb_all, *parts)
    return out[:7], out[7:]


def _small_reduce(ddm, ddm_ctx, dm, dbf):
    def body(ddm_ref, dctx_ref, dm_ref, dbf_ref, lg_ref, sgb_ref):
        ex = _decay_exponents()
        ones = jnp.ones((8, CH), F32)
        for d in range(2):
            for h in range(NH):
                tot = jnp.zeros((CH, CH), F32)
                for kind in range(4):
                    g = ddm_ref[d, kind, h]
                    if kind >= 2:
                        g = g + dctx_ref[d, kind - 2, h]
                    tot = tot + g * dm_ref[d, kind, h] * ex[d][kind]
                lg_ref[d * NH + h: d * NH + h + 1, :] = jnp.sum(tot, axis=0, keepdims=True)
        sgb_ref[...] = jnp.zeros_like(sgb_ref)
        for g in range(NH):
            r = lax.dot_general(ones, dbf_ref[g], (NT, ((), ())), precision=HI, preferred_element_type=F32)
            sgb_ref[g:g + 1, :] = r[0:1, :]

    return pl.pallas_call(
        body, name="small_reduce",
        out_shape=[jax.ShapeDtypeStruct((8, CH), F32), jax.ShapeDtypeStruct((8, CH), F32)],
        in_specs=[pl.BlockSpec(memory_space=pltpu.VMEM)] * 4,
        out_specs=[pl.BlockSpec(memory_space=pltpu.VMEM)] * 2,
        compiler_params=_params(32),
    )(ddm, ddm_ctx, dm, dbf)


def _mod_backward(ct_pad_t, cctx_col, dmod_pad, dcmod_cols, w_mod_s):
    def body(ct_ref, cc_ref, dm_ref, dc_ref, w_ref, gw_ref, part_ref):
        dcm = dc_ref[0:1, :]
        for d in range(1, NDEV):
            dcm = dcm + dc_ref[d:d + 1, :]
        gw_ref[...] = (jnp.dot(_silu(ct_ref[...]), dm_ref[...], precision=HI, preferred_element_type=F32)
                       + _silu(cc_ref[...]) * dcm)
        part_ref[...] = lax.dot_general(jnp.broadcast_to(dcm, (8, dcm.shape[1])), w_ref[...], (NT, ((), ())),
                                        precision=HI, preferred_element_type=F32)

    return pl.pallas_call(
        body, name="mod_backward",
        out_shape=[jax.ShapeDtypeStruct(w_mod_s.shape, F32), jax.ShapeDtypeStruct((8, D), F32)],
        in_specs=[pl.BlockSpec(memory_space=pltpu.VMEM)] * 5,
        out_specs=[pl.BlockSpec(memory_space=pltpu.VMEM)] * 2,
        compiler_params=_params(48),
    )(ct_pad_t, cctx_col, dmod_pad, dcmod_cols, w_mod_s)


def _cctx_update(parts, c_ctx, m, v):
    def body(p_ref, c_ref, m_ref, v_ref, g_ref, d_ref, mo_ref, vo_ref):
        tot = ((p_ref[0] + p_ref[2]) + p_ref[4]) + p_ref[6]
        cv = c_ref[...]
        s = jax.nn.sigmoid(cv)
        g = tot * (s * (1.0 + cv * (1.0 - s)))
        g_ref[...] = g
        d_ref[...], mo_ref[...], vo_ref[...] = _adamw_math(cv, g, m_ref[...], v_ref[...])

    return pl.pallas_call(
        body, name="cctx_update",
        out_shape=[jax.ShapeDtypeStruct((1, D), F32)] * 4,
        in_specs=[pl.BlockSpec(memory_space=pltpu.VMEM)] * 4,
        out_specs=[pl.BlockSpec(memory_space=pltpu.VMEM)] * 4,
        compiler_params=_params(16),
    )(parts, c_ctx, m, v)


def _small_update(gathered, wp, mp, vp):
    def body(g_ref, w_ref, m_ref, v_ref, go_ref, d_ref, mo_ref, vo_ref, loss_ref):
        tot = g_ref[0]
        for d in range(1, NDEV):
            tot = tot + g_ref[d]
        go_ref[Q_BMOD:Q_N1, :] = tot[P_DMOD:P_N1, :] + tot[P_DCMOD:P_DMOD, :]
        go_ref[Q_N1:Q_LG, :] = tot[P_N1:P_LG, :]
        lg = jnp.sum(tot[P_LG:P_N2, :], axis=1, keepdims=True)
        go_ref[Q_LG:Q_N2, :] = lg * jax.nn.sigmoid(-w_ref[Q_LG:Q_N2, :])
        go_ref[Q_N2:Q_ROWS, :] = tot[P_N2:P_LOSS, :]
        d_ref[...], mo_ref[...], vo_ref[...] = _adamw_math(w_ref[...], go_ref[...], m_ref[...], v_ref[...])
        ls = jnp.sum(jnp.sum(tot[P_LOSS:P_ROWS, :], axis=1, keepdims=True), axis=0, keepdims=True)
        loss_ref[...] = jnp.broadcast_to(ls, (8, CH))

    return pl.pallas_call(
        body, name="small_update",
        out_shape=[jax.ShapeDtypeStruct((Q_ROWS, CH), F32)] * 4 + [jax.ShapeDtypeStruct((8, CH), F32)],
        in_specs=[pl.BlockSpec(memory_space=pltpu.VMEM)] * 4,
        out_specs=[pl.BlockSpec(memory_space=pltpu.VMEM)] * 5,
        compiler_params=_params(32),
    )(gathered, wp, mp, vp)


def _rows(a):
    r = a.reshape(-1, CH)
    return jnp.pad(r, ((0, -r.shape[0] % 8), (0, 0)))


def _pack_small(b_mod, norm1, sg_gain, sg_w, sg_b, lf, lb, norm2, norm_f):
    lg = jnp.broadcast_to(jnp.concatenate([lf.reshape(NH), lb.reshape(NH)])[:, None], (2 * NH, CH))
    return jnp.concatenate([_rows(b_mod), _rows(norm1), _rows(sg_gain), _rows(sg_w), _rows(sg_b), lg,
                            _rows(norm2), _rows(norm_f)], axis=0)


def _unpack_small(p):
    return (p[Q_BMOD:Q_N1].reshape(1, 6 * D), p[Q_N1:Q_GAIN].reshape(1, D), p[Q_GAIN:Q_GAIN + NH].reshape(1, AW),
            p[Q_SGW:Q_SGB].reshape(1, NH, CH, CH), p[Q_SGB:Q_SGB + NH].reshape(1, NH, CH),
            p[Q_LG:Q_LG + NH, 0].reshape(1, NH), p[Q_LG + NH:Q_N2, 0].reshape(1, NH),
            p[Q_N2:Q_NF].reshape(1, D), p[Q_NF:Q_ROWS].reshape(D))


def _rope_tables(ln):
    pos = np.arange(ln)
    rows = (pos // GRID_W).astype(np.float32)
    cols = (pos % GRID_W).astype(np.float32)
    n_freq = HD // 4
    inv = (np.float32(ROPE_BASE) ** (-np.arange(n_freq, dtype=np.float32) / np.float32(n_freq))).astype(np.float32)
    ar = rows[:, None] * inv[None, :]
    ac = cols[:, None] * inv[None, :]
    cos_t = np.concatenate([np.cos(ar), np.cos(ar), np.cos(ac), np.cos(ac)], axis=1).astype(np.float32)
    sin_t = np.concatenate([-np.sin(ar), np.sin(ar), -np.sin(ac), np.sin(ac)], axis=1).astype(np.float32)
    return jnp.asarray(cos_t), jnp.asarray(sin_t)


def kernel(x, c, ctx, c_ctx, w_mod, b_mod, norm1, w_in, sg_gain, sg_w, sg_b, ret_logit_f, ret_logit_b, w_out, norm2, w_gate, w_up, w_down, norm_f, loss_target, m_c_ctx, m_w_mod, m_b_mod, m_norm1, m_w_in, m_sg_gain, m_sg_w, m_sg_b, m_ret_logit_f, m_ret_logit_b, m_w_out, m_norm2, m_w_gate, m_w_up, m_w_down, m_norm_f, v_c_ctx, v_w_mod, v_b_mod, v_norm1, v_w_in, v_sg_gain, v_sg_w, v_sg_b, v_ret_logit_f, v_ret_logit_b, v_w_out, v_norm2, v_w_gate, v_w_up, v_w_down, v_norm_f):
    ln = x.shape[1]
    xi, yi, ci = _pos()
    chip = 2 * xi + yi
    me = 4 * xi + 2 * yi + ci
    x2d = x.reshape(ln, D)
    tgt = loss_target.reshape(ln, D)
    mod_c = w_mod.shape[2]

    tr = lambda a: jnp.swapaxes(a[0], 0, 1)
    gbufs, c_all, prod_all = _prologue(c, c_ctx.reshape(1, D), w_mod[0],
                                       [w_in[0], w_out[0], tr(w_gate), tr(w_up), w_down[0]])
    wi = gbufs[0].reshape(NCHIP, D, WI_C)
    gbufs_a, gbufs_b = gbufs[1:3], gbufs[3:5]
    c_all = c_all.reshape(NDEV, D)
    prod_chips = prod_all[0::2]
    mod_rows = jnp.transpose(prod_chips, (1, 0, 2)).reshape(16, NCHIP * mod_c) + b_mod
    mod = lax.dynamic_slice_in_dim(mod_rows, me, 1, axis=0)
    cmod = mod_rows[8:9]
    sh1, sc1, g1, sh2, sc2, g2 = [mod[:, i * D:(i + 1) * D] for i in range(6)]
    csh1, csc1 = cmod[:, 0:D], cmod[:, D:2 * D]
    zrow = jnp.zeros((1, D), F32)
    vec_in = jnp.concatenate([norm1, sh1, sc1] + [zrow] * 5, axis=0)
    vec_ctx = jnp.concatenate([norm1, csh1, csc1] + [zrow] * 5, axis=0)
    vec_post = jnp.concatenate([g1, norm2, sh2, sc2, g2, norm_f.reshape(1, D), zrow, zrow], axis=0)

    logits = jnp.concatenate([ret_logit_f.reshape(NH), ret_logit_b.reshape(NH)])
    dm = _decay_mats(jnp.broadcast_to(logits[:, None, None], (2 * NH, CH, CH)))
    ctx2d = ctx.reshape(ctx.shape[1], D)
    scf, scb = _ctx_forward(ctx2d, vec_ctx, wi, dm)

    cos_t, sin_t = _rope_tables(ln)
    z, hx, gbufs_a = _in_proj(x2d, vec_in, wi, gbufs_a)
    bfull = jnp.broadcast_to(sg_b[0][:, :, None], (NH, CH, CH))
    ycat, sf_all, sb_all, of_all, gbufs = _mixer_fwd(z, cos_t, sin_t, dm, sg_w[0], sg_gain, bfull, scf, scb,
                                             gbufs_a, gbufs_b)
    wo, wg_t, wu_t, wd = [g.reshape(-1, D) for g in gbufs]

    dx1, dycat, h2, dy, df, act, da, db, acc_post = _post_mixer(x2d, ycat, tgt, vec_post, wo, wg_t, wu_t, wd)

    cidx = ci.reshape(1).astype(jnp.int32)
    where = jnp.stack([ci, chip]).astype(jnp.int32)

    def halves_summed(full, names):
        full = [g.reshape(NCHIP, 2, g.shape[1] // 2, g.shape[2]) for g in full]
        from_sib = _rs_exchange_halves(full, "rs_exchange_" + names[0])
        return [_rs_add_halves(g, r, where, "rs_add_halves_" + nm) for g, r, nm in zip(full, from_sib, names)]

    def split(g):
        return g.reshape(NCHIP, 2, g.shape[1] // (2 * NCHIP), g.shape[2])

    g_wd = split(_tn_matmul(act, df, "grad_w_down", 1, DFF, D, False, "shared", 1024))
    g_wu, x_wd = _tn_matmul(db, h2, "grad_w_up", 1, DFF, D, False, "shared", 1024, carry=g_wd)
    g_wu = split(g_wu)
    g_wg, x_wu = _tn_matmul(da, h2, "grad_w_gate", 1, DFF, D, False, "shared", 1024, carry=g_wu)
    g_wg = split(g_wg)
    g_wo, x_wg = _tn_matmul(ycat, dy, "grad_w_out", 1, D, D, False, "shared", 1024, carry=g_wg)
    g_wo = split(g_wo)
    x_wo = _rs_exchange_halves([g_wo], "rs_exchange_w_out")[0]
    names = ["w_in", "w_out", "w_gate", "w_up", "w_down"]
    sums_b = [_rs_add_halves(g, r, where, "rs_add_halves_" + nm)
              for g, r, nm in zip([g_wo, g_wg, g_wu, g_wd], [x_wo, x_wg, x_wu, x_wd], names[1:])]

    (dz, ddm, dsgw, dgain, dbf, dscf, dscb), from_chips_b = _mixer_bwd(
        z, dycat, of_all, cos_t, sin_t, dm, sg_w[0], sg_gain, bfull, sf_all, sb_all, [s[1] for s in sums_b])
    gwkv, acc_ctx, ddm_ctx = _ctx_backward(ctx2d, vec_ctx, wi, dm, dscf, dscb)
    g_wi = _tn_matmul(hx, dz, "grad_w_in", NCHIP, D, WI_C, False, "cols", 512, ctx_kv=gwkv)
    sums_a = halves_summed([g_wi], names[:1])
    lg_part, dsgb = _small_reduce(ddm, ddm_ctx, dm, dbf)
    dcmod = jnp.concatenate([acc_ctx[1:2], acc_ctx[2:3], jnp.zeros((1, 4 * D), F32)], axis=1)
    dmod_rest = jnp.concatenate([acc_post[0:1], acc_post[2:3], acc_post[3:4], acc_post[4:5]], axis=1)
    early = jnp.concatenate([_rows(dcmod), _rows(dmod_rest), _rows(dgain), _rows(dsgw), dsgb, lg_part,
                             _rows(acc_post[1:2]), _rows(acc_post[5:6]), _rows(acc_post[6:7])], axis=0)
    gx, acc_in, from_chips_a, early_all = _in_proj_bwd(dz, x2d, dx1, vec_in, wi, [s[1] for s in sums_a], early)

    sums = sums_a + sums_b
    from_chips = list(from_chips_a) + list(from_chips_b)
    finals = [_rs_add_chips(s[0], r, "rs_add_chips_" + nm) for s, r, nm in zip(sums, from_chips, names)]
    others = _rs_share_final(finals)

    late = jnp.concatenate([_rows(acc_in[1:2]), _rows(acc_in[2:3]), _rows(acc_in[0:1] + acc_ctx[0:1])], axis=0)
    late_all = _allgather_small(late, "gather_small")
    n_dc, n_l = P_DMOD - P_DCMOD, 16
    gathered = jnp.concatenate([early_all[:, :n_dc], late_all[:, :n_l], early_all[:, n_dc:n_dc + 32],
                                late_all[:, n_l:], early_all[:, n_dc + 32:]], axis=1)
    dmod_all = gathered[:, P_DMOD:P_N1].reshape(NDEV, 6 * D)
    dcmod_all = gathered[:, P_DCMOD:P_DMOD].reshape(NDEV, 6 * D)
    dmod_cols = lax.dynamic_slice_in_dim(dmod_all, chip * mod_c, mod_c, axis=1)
    dcmod_cols = lax.dynamic_slice_in_dim(dcmod_all, chip * mod_c, mod_c, axis=1)
    dmod_pad = jnp.concatenate([dmod_cols, jnp.zeros((CH - NDEV, mod_c), F32)], axis=0)
    ct_pad_t = jnp.concatenate([jnp.transpose(c_all), jnp.zeros((D, CH - NDEV), F32)], axis=1)
    g_wmod, cctx_part = _mod_backward(ct_pad_t, c_ctx.reshape(D, 1), dmod_pad, dcmod_cols, w_mod[0])
    parts = _allgather_small(cctx_part[0:1], "gather_cctx")
    g_cctx, d_cctx, nm_cctx, nv_cctx = _cctx_update(parts, c_ctx.reshape(1, D), m_c_ctx.reshape(1, D),
                                                    v_c_ctx.reshape(1, D))

    wp = _pack_small(b_mod, norm1, sg_gain, sg_w, sg_b, ret_logit_f, ret_logit_b, norm2, norm_f)
    mp = _pack_small(m_b_mod, m_norm1, m_sg_gain, m_sg_w, m_sg_b, m_ret_logit_f, m_ret_logit_b, m_norm2, m_norm_f)
    vp = _pack_small(v_b_mod, v_norm1, v_sg_gain, v_sg_w, v_sg_b, v_ret_logit_f, v_ret_logit_b, v_norm2, v_norm_f)
    gp, dp, mp2, vp2, loss_t = _small_update(gathered, wp, mp, vp)

    big_w = [w_in[0], w_out[0], tr(w_gate), tr(w_up), w_down[0]]
    big_m = [m_w_in[0], m_w_out[0], tr(m_w_gate), tr(m_w_up), m_w_down[0]]
    big_v = [v_w_in[0], v_w_out[0], tr(v_w_gate), tr(v_w_up), v_w_down[0]]
    upd = [_adamw_halves(w, own, oth, m, v, cidx, "adamw_" + nm) for w, own, oth, m, v, nm in
           zip(big_w, finals, others, big_m, big_v, names)]
    big_g = [g_wmod] + [u[0] for u in upd]
    big = [_adamw(w_mod[0], g_wmod, m_w_mod[0], v_w_mod[0], "adamw_w_mod")] + [u[1:] for u in upd]

    def assemble(small, cctx, bigs):
        b_mod_, norm1_, gain_, sgw_, sgb_, lf_, lb_, norm2_, normf_ = _unpack_small(small)
        wm, wi_, wo_, wg_, wu_, wd_ = [b[None] for b in bigs]
        wg_, wu_ = jnp.swapaxes(wg_, 1, 2), jnp.swapaxes(wu_, 1, 2)
        return [cctx.reshape(D), wm, b_mod_, norm1_, wi_, gain_, sgw_, sgb_, lf_, lb_, wo_, norm2_, wg_, wu_, wd_,
                normf_]

    out = [loss_t[0, 0], gx.reshape(1, ln, D)]
    out += assemble(gp, g_cctx, big_g)
    out += assemble(dp, d_cctx, [b[0] for b in big])
    out += assemble(mp2, nm_cctx, [b[1] for b in big])
    out += assemble(vp2, nv_cctx, [b[2] for b in big])
    return tuple(out)
```

```python
import functools

import jax
import jax.numpy as jnp
import numpy as np
from jax import lax
from jax.experimental import pallas as pl
from jax.experimental.pallas import tpu as pltpu

F32 = jnp.float32
BF = jnp.bfloat16
MESH = pl.DeviceIdType.MESH

D = 1024
CH = 128
HD = 128
NH = 4
AW = 512
IN_COLS = 3584
DFF = 2816
NCHIP = 4
NDEV = 8
WI_C = IN_COLS // NCHIP
FF_C = DFF // NCHIP
WO_R = D // NCHIP
EPS = 1e-6
GRID_W = 64
ROPE_BASE = 10000.0
K_SCALE = HD ** -0.5
LR, B1, B2, AEPS, WD, STEP = 0.001, 0.9, 0.999, 1e-08, 0.01, 10
VMEM_MB = 1 << 20
HI = lax.Precision.HIGHEST

P_DCMOD, P_DMOD, P_N1, P_GAIN, P_SGW, P_SGB, P_LG, P_N2, P_NF, P_LOSS = 0, 48, 96, 104, 112, 624, 632, 640, 648, 656
P_ROWS = 664
Q_BMOD, Q_N1, Q_GAIN, Q_SGW, Q_SGB, Q_LG, Q_N2, Q_NF = 0, 48, 56, 64, 576, 584, 592, 600
Q_ROWS = 608


def _params(vmem_mb, sem=None):
    return pltpu.CompilerParams(vmem_limit_bytes=vmem_mb * VMEM_MB, dimension_semantics=sem)


def _const(shape):
    nd = len(shape)
    return pl.BlockSpec(shape, lambda *_: (0,) * nd, pipeline_mode=pl.Buffered(1))


def _pos():
    return lax.axis_index("x"), lax.axis_index("y"), lax.axis_index("c")


def _dot(a, b, dims):
    return lax.dot_general(a, b, (dims, ((), ())), preferred_element_type=F32)


NN = ((1,), (0,))
NT = ((1,), (1,))
TN = ((0,), (0,))


@jax.custom_vjp
def _mm(a, b):
    return _dot(a.astype(BF), b.astype(BF), NN)


def _mm_f(a, b):
    return _mm(a, b), (a.astype(BF), b.astype(BF))


def _mm_b(res, g):
    a, b = res
    gb = g.astype(BF)
    return _dot(gb, b, NT), _dot(a, gb, TN)


_mm.defvjp(_mm_f, _mm_b)


@jax.custom_vjp
def _mm_nt(a, b):
    return _dot(a.astype(BF), b.astype(BF), NT)


def _mm_nt_f(a, b):
    return _mm_nt(a, b), (a.astype(BF), b.astype(BF))


def _mm_nt_b(res, g):
    a, b = res
    gb = g.astype(BF)
    return _dot(gb, b, NN), _dot(gb, a, TN)


_mm_nt.defvjp(_mm_nt_f, _mm_nt_b)


@jax.custom_vjp
def _mm_tn(a, b):
    return _dot(a.astype(BF), b.astype(BF), TN)


def _mm_tn_f(a, b):
    return _mm_tn(a, b), (a.astype(BF), b.astype(BF))


def _mm_tn_b(res, g):
    a, b = res
    gb = g.astype(BF)
    return _dot(b, gb, NT), _dot(a, gb, NN)


_mm_tn.defvjp(_mm_tn_f, _mm_tn_b)


def _gelu(x):
    return x * (0.5 * (1.0 + jnp.tanh(0.7978845608028654 * (x + 0.044715 * (x * x * x)))))


def _silu(x):
    return x * jax.nn.sigmoid(x)


def _rms(x):
    return lax.rsqrt(jnp.mean(x * x, axis=-1, keepdims=True) + EPS)


def _swap32(t):
    lane = lax.broadcasted_iota(jnp.int32, t.shape, 1)
    first = (lane % 64) < 32
    return jnp.where(first, pltpu.roll(t, 96, 1), pltpu.roll(t, 32, 1))


def _rope(t, cos, sin):
    return t * cos + _swap32(t) * sin


def _rope_bwd(d, cos, sin):
    return d * cos + _swap32(d * sin)


def _heads(ref, r0=0):
    return [ref[r0:r0 + CH, h * HD:(h + 1) * HD].astype(F32) for h in range(NH)]


def _gate_group(u, v, sgw, gain, bfull):
    gv = _gelu(v)
    return _gelu(u) * (_mm(sgw, gv * _rms(gv) * gain) + bfull)


def _gated_norm(gate, o):
    return _silu(gate) * (o * _rms(o))


def _ret_head(q, k, vr, gf, gb, sf, sb, df, xf, zf, db, xb, zb):
    a = _mm_nt(q, k)
    of = _mm(a * df, vr) + xf * _mm(q, sf)
    ob = _mm(a * db, vr) + xb * _mm(q, sb)
    return _gated_norm(gf, of) + _gated_norm(gb, ob), _mm_tn(k, zf * vr), _mm_tn(k, zb * vr), of


def _ctx_states(ctx0, ctx1, n1, csh, csc, wk, wv, zf, zb, ef, eb):
    hc0 = (ctx0 * _rms(ctx0) * n1) * (1.0 + csc) + csh
    hc1 = (ctx1 * _rms(ctx1) * n1) * (1.0 + csc) + csh
    scf, scb = [], []
    for h in range(NH):
        k0, k1 = _mm(hc0, wk[h]) * K_SCALE, _mm(hc1, wk[h]) * K_SCALE
        v0, v1 = _mm(hc0, wv[h]), _mm(hc1, wv[h])
        scf.append(ef[h] * _mm_tn(k0, zf[h] * v0) + _mm_tn(k1, zf[h] * v1))
        scb.append(eb[h] * _mm_tn(k1, zb[h] * v1) + _mm_tn(k0, zb[h] * v0))
    return scf, scb


def _allgather_small(v, name):
    r, n = v.shape

    def body(v_ref, out_ref, send_sems, recv_sems, local_sem):
        x, y, c = _pos()
        me = 4 * x + 2 * y + c
        mine = pltpu.make_async_copy(v_ref, out_ref.at[me], local_sem)
        mine.start()
        sent = []
        for k in range(1, NDEV):
            kx, ky, kc = (k >> 2) & 1, (k >> 1) & 1, k & 1
            peer = (x ^ kx, y ^ ky, c ^ kc)
            cp = pltpu.make_async_remote_copy(src_ref=v_ref, dst_ref=out_ref.at[me], send_sem=send_sems.at[k - 1],
                                              recv_sem=recv_sems.at[k - 1], device_id=peer, device_id_type=MESH)
            cp.start()
            sent.append(cp)
        for k in range(1, NDEV):
            kx, ky, kc = (k >> 2) & 1, (k >> 1) & 1, k & 1
            peer = (x ^ kx, y ^ ky, c ^ kc)
            src = 4 * (x ^ kx) + 2 * (y ^ ky) + (c ^ kc)
            pltpu.make_async_remote_copy(src_ref=v_ref, dst_ref=out_ref.at[src], send_sem=send_sems.at[k - 1],
                                         recv_sem=recv_sems.at[k - 1], device_id=peer, device_id_type=MESH).wait_recv()
        for cp in sent:
            cp.wait_send()
        mine.wait()

    return pl.pallas_call(
        body, name=name,
        out_shape=jax.ShapeDtypeStruct((NDEV, r, n), F32),
        in_specs=[pl.BlockSpec(memory_space=pltpu.VMEM)],
        out_specs=pl.BlockSpec(memory_space=pltpu.VMEM),
        scratch_shapes=[pltpu.SemaphoreType.DMA((NDEV - 1,)), pltpu.SemaphoreType.DMA((NDEV - 1,)),
                        pltpu.SemaphoreType.DMA],
        compiler_params=_params(16),
    )(v)


def _chip_offsets():
    return [((k >> 1) & 1, k & 1) for k in range(1, NCHIP)]


def _prologue(c, c_ctx, w_mod_s, shards):
    nt = len(shards)
    shapes = [s.shape for s in shards]
    mod_c = w_mod_s.shape[1]

    def body(*refs):
        c_ref, cc_ref, wm_ref = refs[:3]
        srcs = refs[3:3 + nt]
        outs = refs[3 + nt:3 + 2 * nt]
        call_ref, prod_ref = refs[3 + 2 * nt:5 + 2 * nt]
        stages = refs[5 + 2 * nt:5 + 3 * nt]
        ct = refs[5 + 3 * nt]
        c_send, c_recv, p_send, p_recv, ici_send, ici_recv, d2d_send, d2d_recv, local_sems = refs[6 + 3 * nt:]
        x, y, c = _pos()
        chip = 2 * x + y
        me = 4 * x + 2 * y + c
        sib = (x, y, 1 - c)
        pending = []
        for t in range(nt):
            half = shapes[t][0] // 2
            stages[t][0] = srcs[t][0:half, :].astype(BF)
            stages[t][1] = srcs[t][half:2 * half, :].astype(BF)
            cp = pltpu.make_async_copy(stages[t], outs[t].at[chip], local_sems.at[t])
            cp.start()
            pending.append(cp)
        sends = []
        for k, (kx, ky) in enumerate(_chip_offsets()):
            cp = pltpu.make_async_remote_copy(src_ref=stages[0].at[c], dst_ref=outs[0].at[chip, c],
                                              send_sem=ici_send.at[k], recv_sem=ici_recv.at[k],
                                              device_id=(x ^ kx, y ^ ky, c), device_id_type=MESH)
            cp.start()
            sends.append(cp)

        def to_all(src, dst_of, send_sems, recv_sems):
            for k in range(1, NDEV):
                kx, ky, kc = (k >> 2) & 1, (k >> 1) & 1, k & 1
                cp = pltpu.make_async_remote_copy(src_ref=src, dst_ref=dst_of(me), send_sem=send_sems.at[k - 1],
                                                  recv_sem=recv_sems.at[k - 1], device_id=(x ^ kx, y ^ ky, c ^ kc),
                                                  device_id_type=MESH)
                cp.start()
                sends.append(cp)
            for k in range(1, NDEV):
                kx, ky, kc = (k >> 2) & 1, (k >> 1) & 1, k & 1
                frm = 4 * (x ^ kx) + 2 * (y ^ ky) + (c ^ kc)
                pltpu.make_async_remote_copy(src_ref=src, dst_ref=dst_of(frm), send_sem=send_sems.at[k - 1],
                                             recv_sem=recv_sems.at[k - 1], device_id=(x ^ kx, y ^ ky, c ^ kc),
                                             device_id_type=MESH).wait_recv()

        call_ref[me] = c_ref[...]
        to_all(c_ref, lambda d: call_ref.at[d], c_send, c_recv)
        ct[...] = jnp.zeros_like(ct)
        for d in range(NDEV):
            ct[d:d + 1, :] = call_ref[d]
        ct[NDEV:NDEV + 1, :] = cc_ref[...]
        prod_ref[me] = jnp.dot(_silu(ct[...]), wm_ref[...], precision=HI, preferred_element_type=F32)
        to_all(prod_ref.at[me], lambda d: prod_ref.at[d], p_send, p_recv)

        for k, (kx, ky) in enumerate(_chip_offsets()):
            frm = 2 * (x ^ kx) + (y ^ ky)
            pltpu.make_async_remote_copy(src_ref=stages[0].at[c], dst_ref=outs[0].at[frm, c],
                                         send_sem=ici_send.at[k], recv_sem=ici_recv.at[k],
                                         device_id=(x ^ kx, y ^ ky, c), device_id_type=MESH).wait_recv()
            cp = pltpu.make_async_remote_copy(src_ref=outs[0].at[frm, c], dst_ref=outs[0].at[frm, c],
                                              send_sem=d2d_send.at[k], recv_sem=d2d_recv.at[k],
                                              device_id=sib, device_id_type=MESH)
            cp.start()
            sends.append(cp)
        for k, (kx, ky) in enumerate(_chip_offsets()):
            frm = 2 * (x ^ kx) + (y ^ ky)
            pltpu.make_async_remote_copy(src_ref=stages[0].at[c], dst_ref=outs[0].at[frm, 1 - c],
                                         send_sem=d2d_send.at[k], recv_sem=d2d_recv.at[k],
                                         device_id=sib, device_id_type=MESH).wait_recv()
        for cp in sends:
            cp.wait_send()
        for cp in pending:
            cp.wait()

    vm = pl.BlockSpec(memory_space=pltpu.VMEM)
    out = pl.pallas_call(
        body, name="prologue",
        out_shape=[jax.ShapeDtypeStruct((NCHIP, 2, r // 2, cc), BF) for r, cc in shapes]
        + [jax.ShapeDtypeStruct((NDEV, 1, D), F32), jax.ShapeDtypeStruct((NDEV, 16, mod_c), F32)],
        in_specs=[vm] * (3 + nt),
        out_specs=[pl.BlockSpec(memory_space=pl.ANY)] * nt + [vm, vm],
        scratch_shapes=[pltpu.VMEM((2, r // 2, cc), BF) for r, cc in shapes] + [pltpu.VMEM((16, D), F32)]
        + [pltpu.SemaphoreType.DMA((NDEV - 1,))] * 4 + [pltpu.SemaphoreType.DMA((NCHIP - 1,))] * 4
        + [pltpu.SemaphoreType.DMA((nt,))],
        compiler_params=_params(56),
    )(c, c_ctx, w_mod_s, *shards)
    return out[:nt], out[nt], out[nt + 1]


def _gather_ici_copies(bufs, send_sems, recv_sems):
    x, y, c = _pos()
    chip = 2 * x + y
    nt = len(bufs)
    out_cp, in_cp = [], []
    for k, (kx, ky) in enumerate(_chip_offsets()):
        frm = 2 * (x ^ kx) + (y ^ ky)
        for t in range(nt):
            s = k * nt + t
            peer = (x ^ kx, y ^ ky, c)
            out_cp.append(pltpu.make_async_remote_copy(
                src_ref=bufs[t].at[chip, c], dst_ref=bufs[t].at[chip, c], send_sem=send_sems.at[s],
                recv_sem=recv_sems.at[s], device_id=peer, device_id_type=MESH))
            in_cp.append(pltpu.make_async_remote_copy(
                src_ref=bufs[t].at[chip, c], dst_ref=bufs[t].at[frm, c], send_sem=send_sems.at[s],
                recv_sem=recv_sems.at[s], device_id=peer, device_id_type=MESH))
    return out_cp, in_cp


def _gather_d2d_copies(bufs, send_sems, recv_sems):
    x, y, c = _pos()
    nt = len(bufs)
    out_cp, in_cp = [], []
    for k, (kx, ky) in enumerate(_chip_offsets()):
        frm = 2 * (x ^ kx) + (y ^ ky)
        for t in range(nt):
            s = k * nt + t
            out_cp.append(pltpu.make_async_remote_copy(
                src_ref=bufs[t].at[frm, c], dst_ref=bufs[t].at[frm, c], send_sem=send_sems.at[s],
                recv_sem=recv_sems.at[s], device_id=(x, y, 1 - c), device_id_type=MESH))
            in_cp.append(pltpu.make_async_remote_copy(
                src_ref=bufs[t].at[frm, c], dst_ref=bufs[t].at[frm, 1 - c], send_sem=send_sems.at[s],
                recv_sem=recv_sems.at[s], device_id=(x, y, 1 - c), device_id_type=MESH))
    return out_cp, in_cp


def _scatter_ici_copies(parts, outs, send_sems, recv_sems):
    x, y, c = _pos()
    nt = len(parts)
    cps = []
    for k, (kx, ky) in enumerate(_chip_offsets()):
        dst_chip = 2 * (x ^ kx) + (y ^ ky)
        for t in range(nt):
            s = k * nt + t
            cps.append(pltpu.make_async_remote_copy(
                src_ref=parts[t].at[dst_chip], dst_ref=outs[t].at[k], send_sem=send_sems.at[s],
                recv_sem=recv_sems.at[s], device_id=(x ^ kx, y ^ ky, c), device_id_type=MESH))
    return cps


def _rs_exchange_halves(grads, name):
    nt = len(grads)
    shapes = [g.shape for g in grads]

    def body(*refs):
        gs, outs = refs[:nt], refs[nt:2 * nt]
        send_sems, recv_sems = refs[2 * nt:]
        x, y, c = _pos()
        sib = (x, y, 1 - c)
        sent = []
        for t in range(nt):
            for j in range(NCHIP):
                s = t * NCHIP + j
                cp = pltpu.make_async_remote_copy(src_ref=gs[t].at[j, 1 - c], dst_ref=outs[t].at[j],
                                                  send_sem=send_sems.at[s], recv_sem=recv_sems.at[s],
                                                  device_id=sib, device_id_type=MESH)
                cp.start()
                sent.append(cp)
        for cp in sent:
            cp.wait_recv()
        for cp in sent:
            cp.wait_send()

    return pl.pallas_call(
        body, name=name,
        out_shape=[jax.ShapeDtypeStruct((NCHIP, s[2], s[3]), F32) for s in shapes],
        in_specs=[pl.BlockSpec(memory_space=pl.ANY)] * nt,
        out_specs=[pl.BlockSpec(memory_space=pl.ANY)] * nt,
        scratch_shapes=[pltpu.SemaphoreType.DMA((nt * NCHIP,))] * 2,
    )(*grads)


def _rs_share_final(finals):
    nt = len(finals)
    shapes = [f.shape for f in finals]

    def body(*refs):
        fs, outs = refs[:nt], refs[nt:2 * nt]
        send_sems, recv_sems = refs[2 * nt:]
        x, y, c = _pos()
        sent = []
        for t in range(nt):
            cp = pltpu.make_async_remote_copy(src_ref=fs[t], dst_ref=outs[t], send_sem=send_sems.at[t],
                                              recv_sem=recv_sems.at[t], device_id=(x, y, 1 - c), device_id_type=MESH)
            cp.start()
            sent.append(cp)
        for cp in sent:
            cp.wait_recv()
        for cp in sent:
            cp.wait_send()

    return pl.pallas_call(
        body, name="rs_share_final",
        out_shape=[jax.ShapeDtypeStruct(s, F32) for s in shapes],
        in_specs=[pl.BlockSpec(memory_space=pl.ANY)] * nt,
        out_specs=[pl.BlockSpec(memory_space=pl.ANY)] * nt,
        scratch_shapes=[pltpu.SemaphoreType.DMA((nt,))] * 2,
    )(*finals)


def _row_tile(h, cc=D):
    for t in (512, 384, 352, 256, 176, 128, 64, 32, 16):
        if h % t == 0 and t * cc * 4 <= (5 * VMEM_MB) // 4:
            return t
    return h


def _rs_add_halves(g, recv, where, name):
    _, _, h, cc = g.shape
    th = _row_tile(h, cc)

    def body(w_ref, g_ref, r_ref, own_ref, ob_ref):
        s = g_ref[...] + r_ref[...]
        ob_ref[...] = s.astype(BF)

        @pl.when(pl.program_id(1) == w_ref[1])
        def _():
            own_ref[...] = s

    return pl.pallas_call(
        body, name=name,
        grid_spec=pltpu.PrefetchScalarGridSpec(
            num_scalar_prefetch=1, grid=(h // th, NCHIP),
            in_specs=[pl.BlockSpec((None, None, th, cc), lambda i, j, w_ref: (j, w_ref[0], i, 0)),
                      pl.BlockSpec((None, th, cc), lambda i, j, w_ref: (j, i, 0))],
            out_specs=[pl.BlockSpec((th, cc), lambda i, j, w_ref: (i, 0)),
                       pl.BlockSpec((None, th, cc), lambda i, j, w_ref: (j, i, 0))]),
        out_shape=[jax.ShapeDtypeStruct((h, cc), F32), jax.ShapeDtypeStruct((NCHIP, h, cc), BF)],
        compiler_params=_params(48, ("arbitrary", "arbitrary")),
    )(where, g, recv)


def _rs_add_chips(own, recv, name):
    h, cc = own.shape
    th = _row_tile(h, cc)

    def body(o_ref, r_ref, out_ref):
        out_ref[...] = ((o_ref[...] + r_ref[0].astype(F32)) + r_ref[1].astype(F32)) + r_ref[2].astype(F32)

    return pl.pallas_call(
        body, name=name, grid=(h // th,),
        in_specs=[pl.BlockSpec((th, cc), lambda i: (i, 0)), pl.BlockSpec((NCHIP - 1, th, cc), lambda i: (0, i, 0))],
        out_specs=pl.BlockSpec((th, cc), lambda i: (i, 0)),
        out_shape=jax.ShapeDtypeStruct((h, cc), F32),
        compiler_params=_params(48, ("parallel",)),
    )(own, recv)


def _adamw_math(w, g, m, v):
    m2 = B1 * m + (1.0 - B1) * g
    v2 = B2 * v + (1.0 - B2) * (g * g)
    m_hat = m2 / (1.0 - B1 ** STEP)
    v_hat = v2 / (1.0 - B2 ** STEP)
    delta = -LR * (m_hat / (jnp.sqrt(v_hat) + AEPS) + WD * w)
    return delta, m2, v2


def _adamw(w, g, m, v, name):
    r, cc = w.shape
    tr = _row_tile(r, cc)

    def body(w_ref, g_ref, m_ref, v_ref, d_ref, mo_ref, vo_ref):
        d, m2, v2 = _adamw_math(w_ref[...], g_ref[...], m_ref[...], v_ref[...])
        d_ref[...] = d
        mo_ref[...] = m2
        vo_ref[...] = v2

    spec = pl.BlockSpec((tr, cc), lambda i: (i, 0))
    return pl.pallas_call(
        body, name=name, grid=(r // tr,), in_specs=[spec] * 4, out_specs=[spec] * 3,
        out_shape=[jax.ShapeDtypeStruct((r, cc), F32)] * 3,
        compiler_params=_params(48, ("parallel",)),
    )(w, g, m, v)


def _adamw_halves(w, own, other, m, v, cidx, name):
    r, cc = w.shape
    h = r // 2
    tr = _row_tile(h, cc)
    per = h // tr

    def body(c_ref, w_ref, own_ref, oth_ref, m_ref, v_ref, g_ref, d_ref, mo_ref, vo_ref):
        mine = (pl.program_id(0) // per) == c_ref[0]
        g = jnp.where(mine, own_ref[...], oth_ref[...])
        g_ref[...] = g
        d, m2, v2 = _adamw_math(w_ref[...], g, m_ref[...], v_ref[...])
        d_ref[...] = d
        mo_ref[...] = m2
        vo_ref[...] = v2

    full = pl.BlockSpec((tr, cc), lambda i, c_ref: (i, 0))
    half = pl.BlockSpec((tr, cc), lambda i, c_ref: (i % per, 0))
    return pl.pallas_call(
        body, name=name,
        grid_spec=pltpu.PrefetchScalarGridSpec(
            num_scalar_prefetch=1, grid=(r // tr,),
            in_specs=[full, half, half, full, full], out_specs=[full] * 4),
        out_shape=[jax.ShapeDtypeStruct((r, cc), F32)] * 4,
        compiler_params=_params(48, ("parallel",)),
    )(cidx, w, own, other, m, v)


def _decay_exponents():
    ri = lax.broadcasted_iota(jnp.int32, (CH, CH), 0).astype(F32)
    ci = lax.broadcasted_iota(jnp.int32, (CH, CH), 1).astype(F32)
    full = jnp.full((CH, CH), float(CH), F32)
    return [[ri - ci, ri + 1.0, (CH - 1.0) - ri, full], [ci - ri, CH - ri, ri, full]]


def _decay_mats(logit_full):
    def body(l_ref, o_ref):
        ex = _decay_exponents()
        for d in range(2):
            for h in range(NH):
                lv = l_ref[d * NH + h]
                lg = jnp.minimum(lv, 0.0) - jnp.log(1.0 + jnp.exp(-jnp.abs(lv)))
                for kind in range(4):
                    m = jnp.exp(lg * ex[d][kind])
                    if kind == 0:
                        m = jnp.where(ex[d][0] >= 0.0, jnp.exp(lg * jnp.maximum(ex[d][0], 0.0)), 0.0)
                    o_ref[d, kind, h] = m

    return pl.pallas_call(
        body, name="decay_mats",
        out_shape=jax.ShapeDtypeStruct((2, 4, NH, CH, CH), F32),
        in_specs=[pl.BlockSpec(memory_space=pltpu.VMEM)],
        out_specs=pl.BlockSpec(memory_space=pltpu.VMEM),
        compiler_params=_params(32),
    )(logit_full)


def _ctx_kv_weights(wi_ref):
    def cols(g):
        return wi_ref[g // WI_C, :, g % WI_C: g % WI_C + HD].astype(F32)

    wk = [cols(3 * AW + h * HD) for h in range(NH)]
    wv = [cols(4 * AW + h * HD) for h in range(NH)]
    return wk, wv


def _ctx_forward(ctx, vecs, wi, dm):
    def body(ctx_ref, v_ref, wi_ref, dm_ref, scf_ref, scb_ref):
        wk, wv = _ctx_kv_weights(wi_ref)
        mats = [[dm_ref[d, kind, h] for h in range(NH)] for d in range(2) for kind in (2, 3)]
        scf, scb = _ctx_states(ctx_ref[0:CH, :], ctx_ref[CH:2 * CH, :], v_ref[0:1, :], v_ref[1:2, :],
                               v_ref[2:3, :], wk, wv, mats[0], mats[2], mats[1], mats[3])
        for h in range(NH):
            scf_ref[h] = scf[h]
            scb_ref[h] = scb[h]

    return pl.pallas_call(
        body, name="ctx_forward",
        out_shape=[jax.ShapeDtypeStruct((NH, HD, HD), F32)] * 2,
        in_specs=[pl.BlockSpec(memory_space=pltpu.VMEM)] * 4,
        out_specs=[pl.BlockSpec(memory_space=pltpu.VMEM)] * 2,
        compiler_params=_params(48),
    )(ctx, vecs, wi, dm)


def _ctx_backward(ctx, vecs, wi, dm, dscf, dscb):
    def body(ctx_ref, v_ref, wi_ref, dm_ref, gf_ref, gb_ref, gw_ref, gv_ref, gdm_ref):
        wk, wv = _ctx_kv_weights(wi_ref)
        mats = [[dm_ref[d, kind, h] for h in range(NH)] for d in range(2) for kind in (2, 3)]
        ctx0, ctx1 = ctx_ref[0:CH, :], ctx_ref[CH:2 * CH, :]

        def fn(n1, csh, csc, wk_, wv_, zf, zb, ef, eb):
            return _ctx_states(ctx0, ctx1, n1, csh, csc, wk_, wv_, zf, zb, ef, eb)

        _, vjp = jax.vjp(fn, v_ref[0:1, :], v_ref[1:2, :], v_ref[2:3, :], wk, wv,
                         mats[0], mats[2], mats[1], mats[3])
        cot = ([gf_ref[h] for h in range(NH)], [gb_ref[h] for h in range(NH)])
        dn1, dcsh, dcsc, dwk, dwv, dzf, dzb, def_, deb = vjp(cot)
        for h in range(NH):
            gw_ref[:, h * HD:(h + 1) * HD] = dwk[h]
            gw_ref[:, AW + h * HD:AW + (h + 1) * HD] = dwv[h]
        gv_ref[...] = jnp.zeros_like(gv_ref)
        gv_ref[0:1, :] = dn1
        gv_ref[1:2, :] = dcsh
        gv_ref[2:3, :] = dcsc
        for h in range(NH):
            gdm_ref[0, 0, h] = dzf[h]
            gdm_ref[0, 1, h] = def_[h]
            gdm_ref[1, 0, h] = dzb[h]
            gdm_ref[1, 1, h] = deb[h]

    return pl.pallas_call(
        body, name="ctx_backward",
        out_shape=[jax.ShapeDtypeStruct((D, 2 * AW), F32), jax.ShapeDtypeStruct((8, D), F32),
                   jax.ShapeDtypeStruct((2, 2, NH, CH, CH), F32)],
        in_specs=[pl.BlockSpec(memory_space=pltpu.VMEM)] * 6,
        out_specs=[pl.BlockSpec(memory_space=pltpu.VMEM)] * 3,
        compiler_params=_params(56),
    )(ctx, vecs, wi, dm, dscf, dscb)


def _in_proj(x, vecs, wi, gbufs):
    ln = x.shape[0]
    t = min(512, ln)
    nt = len(gbufs)
    steps = ln // t

    def body(x_ref, v_ref, wi_ref, *refs):
        z_ref, hx_ref = refs[nt:nt + 2]
        bufs = refs[nt + 2:2 * nt + 2]
        send_sems, recv_sems = refs[2 * nt + 2:]
        i = pl.program_id(0)

        @pl.when(i == 0)
        def _():
            for cp in _gather_ici_copies(bufs, send_sems, recv_sems)[0]:
                cp.start()

        xv = x_ref[...]
        hx = (xv * _rms(xv) * v_ref[0:1, :]) * (1.0 + v_ref[2:3, :]) + v_ref[1:2, :]
        hb = hx.astype(BF)
        hx_ref[...] = hb
        for j in range(NCHIP):
            z_ref[:, j * WI_C:(j + 1) * WI_C] = _dot(hb, wi_ref[j], NN)

        @pl.when(i == steps - 1)
        def _():
            out_cp, in_cp = _gather_ici_copies(bufs, send_sems, recv_sems)
            for cp in in_cp:
                cp.wait_recv()
            for cp in out_cp:
                cp.wait_send()

    hbm = pl.BlockSpec(memory_space=pl.ANY)
    out = pl.pallas_call(
        body, name="in_proj", grid=(steps,),
        in_specs=[pl.BlockSpec((t, D), lambda i: (i, 0)), _const((8, D)), _const((NCHIP, D, WI_C))] + [hbm] * nt,
        out_specs=[pl.BlockSpec((t, IN_COLS), lambda i: (i, 0)), pl.BlockSpec((t, D), lambda i: (i, 0))] + [hbm] * nt,
        out_shape=[jax.ShapeDtypeStruct((ln, IN_COLS), F32), jax.ShapeDtypeStruct((ln, D), BF)]
        + [jax.ShapeDtypeStruct(g.shape, g.dtype) for g in gbufs],
        input_output_aliases={3 + k: 2 + k for k in range(nt)},
        scratch_shapes=[pltpu.SemaphoreType.DMA(((NCHIP - 1) * nt,))] * 2,
        compiler_params=_params(56, ("arbitrary",)),
    )(x, vecs, wi, *gbufs)
    return out[0], out[1], out[2:]


def _allgather_copies(src, out, send_sems, recv_sems, local_sem):
    x, y, c = _pos()
    me = 4 * x + 2 * y + c
    sends, recvs = [], []
    for k in range(1, NDEV):
        kx, ky, kc = (k >> 2) & 1, (k >> 1) & 1, k & 1
        peer = (x ^ kx, y ^ ky, c ^ kc)
        frm = 4 * (x ^ kx) + 2 * (y ^ ky) + (c ^ kc)
        sends.append(pltpu.make_async_remote_copy(src_ref=src, dst_ref=out.at[me], send_sem=send_sems.at[k - 1],
                                                  recv_sem=recv_sems.at[k - 1], device_id=peer, device_id_type=MESH))
        recvs.append(pltpu.make_async_remote_copy(src_ref=src, dst_ref=out.at[frm], send_sem=send_sems.at[k - 1],
                                                  recv_sem=recv_sems.at[k - 1], device_id=peer, device_id_type=MESH))
    return sends, recvs, pltpu.make_async_copy(src, out.at[me], local_sem)


def _in_proj_bwd(dz, x, dx1, vecs, wi, parts, early):
    ln = x.shape[0]
    t = min(512, ln)
    nt = len(parts)
    steps = ln // t

    def body(dz_ref, x_ref, dx1_ref, v_ref, wi_ref, *refs):
        ps = refs[:nt]
        early_ref = refs[nt]
        gx_ref, acc_ref = refs[nt + 1:nt + 3]
        got = refs[nt + 3:2 * nt + 3]
        early_all = refs[2 * nt + 3]
        send_sems, recv_sems, ag_send, ag_recv, ag_local = refs[2 * nt + 4:]

        @pl.when(pl.program_id(0) == 0)
        def _():
            acc_ref[...] = jnp.zeros_like(acc_ref)
            for cp in _scatter_ici_copies(ps, got, send_sems, recv_sems):
                cp.start()
            sends, _, own = _allgather_copies(early_ref, early_all, ag_send, ag_recv, ag_local)
            own.start()
            for cp in sends:
                cp.start()

        dhx = jnp.zeros((t, D), F32)
        for j in range(NCHIP):
            dhx = dhx + _dot(dz_ref[:, j * WI_C:(j + 1) * WI_C], wi_ref[j], NT)
        xv = x_ref[...]
        r = _rms(xv)
        xn = xv * r
        n1, sc = v_ref[0:1, :], v_ref[2:3, :]
        acc_ref[0:1, :] += jnp.sum(dhx * xn * (1.0 + sc), axis=0, keepdims=True)
        acc_ref[1:2, :] += jnp.sum(dhx, axis=0, keepdims=True)
        acc_ref[2:3, :] += jnp.sum(dhx * xn * n1, axis=0, keepdims=True)
        g = dhx * n1 * (1.0 + sc)
        gx_ref[...] = dx1_ref[...] + r * (g - xn * jnp.mean(g * xn, axis=-1, keepdims=True))

        @pl.when(pl.program_id(0) == steps - 1)
        def _():
            cps = _scatter_ici_copies(ps, got, send_sems, recv_sems)
            sends, recvs, own = _allgather_copies(early_ref, early_all, ag_send, ag_recv, ag_local)
            for cp in cps + recvs:
                cp.wait_recv()
            for cp in cps + sends:
                cp.wait_send()
            own.wait()

    hbm = pl.BlockSpec(memory_space=pl.ANY)
    out = pl.pallas_call(
        body, name="in_proj_bwd", grid=(steps,),
        in_specs=[pl.BlockSpec((t, IN_COLS), lambda i: (i, 0)), pl.BlockSpec((t, D), lambda i: (i, 0)),
                  pl.BlockSpec((t, D), lambda i: (i, 0)), _const((8, D)), _const((NCHIP, D, WI_C))]
        + [hbm] * (nt + 1),
        out_specs=[pl.BlockSpec((t, D), lambda i: (i, 0)), pl.BlockSpec((8, D), lambda i: (0, 0))]
        + [hbm] * (nt + 1),
        out_shape=[jax.ShapeDtypeStruct((ln, D), F32), jax.ShapeDtypeStruct((8, D), F32)]
        + [jax.ShapeDtypeStruct((NCHIP - 1,) + p.shape[1:], BF) for p in parts]
        + [jax.ShapeDtypeStruct((NDEV,) + early.shape, F32)],
        scratch_shapes=[pltpu.SemaphoreType.DMA(((NCHIP - 1) * nt,))] * 2
        + [pltpu.SemaphoreType.DMA((NDEV - 1,))] * 2 + [pltpu.SemaphoreType.DMA],
        compiler_params=_params(56, ("arbitrary",)),
    )(dz, x, dx1, vecs, wi, *parts, early)
    return out[0], out[1], out[2:2 + nt], out[2 + nt]


def _post_mixer(x, ycat, tgt, vecs, wo, wg, wu, wd):
    ln = x.shape[0]
    t = min(256, ln)

    def body(x_ref, y_ref, t_ref, v_ref, wo_ref, wg_ref, wu_ref, wd_ref,
             dx1_ref, dyc_ref, h2_ref, dy_ref, df_ref, act_ref, da_ref, db_ref, acc_ref, a_st, b_st):
        @pl.when(pl.program_id(0) == 0)
        def _():
            acc_ref[...] = jnp.zeros_like(acc_ref)

        g1, n2, sh2, sc2 = v_ref[0:1, :], v_ref[1:2, :], v_ref[2:3, :], v_ref[3:4, :]
        g2, nf = v_ref[4:5, :], v_ref[5:6, :]
        y = _dot(y_ref[...], wo_ref[...], NN)
        x1 = x_ref[...] + g1 * y
        r2 = _rms(x1)
        xn2 = x1 * r2
        t2 = xn2 * n2
        h2b = (t2 * (1.0 + sc2) + sh2).astype(BF)
        h2_ref[...] = h2b
        a = _dot(h2b, wg_ref[...], NT)
        b = _dot(h2b, wu_ref[...], NT)
        a_st[...] = a
        b_st[...] = b
        act = (_silu(a) * b).astype(BF)
        act_ref[...] = act
        f = _dot(act, wd_ref[...], NN)
        x2 = x1 + g2 * f
        r3 = _rms(x2)
        xn3 = x2 * r3
        e = xn3 * nf - t_ref[...]
        acc_ref[6:7, :] += jnp.sum(e * e, axis=0, keepdims=True) * (0.5 / D)
        dout = e * (1.0 / D)
        acc_ref[5:6, :] += jnp.sum(dout * xn3, axis=0, keepdims=True)
        gg = dout * nf
        dx2 = r3 * (gg - xn3 * jnp.mean(gg * xn3, axis=-1, keepdims=True))
        acc_ref[4:5, :] += jnp.sum(dx2 * f, axis=0, keepdims=True)
        dfb = (g2 * dx2).astype(BF)
        df_ref[...] = dfb
        dact = _dot(dfb, wd_ref[...], NT)
        a = a_st[...]
        b = b_st[...]
        s = jax.nn.sigmoid(a)
        da = (dact * b * (s * (1.0 + a * (1.0 - s)))).astype(BF)
        db = (dact * (a * s)).astype(BF)
        da_ref[...] = da
        db_ref[...] = db
        dh2 = _dot(da, wg_ref[...], NN) + _dot(db, wu_ref[...], NN)
        acc_ref[2:3, :] += jnp.sum(dh2, axis=0, keepdims=True)
        acc_ref[3:4, :] += jnp.sum(dh2 * t2, axis=0, keepdims=True)
        acc_ref[1:2, :] += jnp.sum(dh2 * xn2 * (1.0 + sc2), axis=0, keepdims=True)
        gx = dh2 * n2 * (1.0 + sc2)
        dx1 = dx2 + r2 * (gx - xn2 * jnp.mean(gx * xn2, axis=-1, keepdims=True))
        dx1_ref[...] = dx1
        acc_ref[0:1, :] += jnp.sum(dx1 * y, axis=0, keepdims=True)
        dyb = (g1 * dx1).astype(BF)
        dy_ref[...] = dyb
        dyc_ref[...] = _dot(dyb, wo_ref[...], NT)

    tok = pl.BlockSpec((t, D), lambda i: (i, 0))
    ffb = pl.BlockSpec((t, DFF), lambda i: (i, 0))
    return pl.pallas_call(
        body, name="post_mixer", grid=(ln // t,),
        in_specs=[tok, tok, tok, _const((8, D)), _const((D, D)), _const((DFF, D)), _const((DFF, D)),
                  _const((DFF, D))],
        out_specs=[tok, tok, tok, tok, tok, ffb, ffb, ffb, pl.BlockSpec((16, D), lambda i: (0, 0))],
        out_shape=[jax.ShapeDtypeStruct((ln, D), F32)] * 2 + [jax.ShapeDtypeStruct((ln, D), BF)] * 3
        + [jax.ShapeDtypeStruct((ln, DFF), BF)] * 3 + [jax.ShapeDtypeStruct((16, D), F32)],
        scratch_shapes=[pltpu.VMEM((t, DFF), F32)] * 2,
        compiler_params=_params(60, ("arbitrary",)),
    )(x, ycat, tgt, vecs, wo, wg, wu, wd)


def _exchange_copies(g, out, send_sems, recv_sems):
    x, y, c = _pos()
    return [pltpu.make_async_remote_copy(src_ref=g.at[j, 1 - c], dst_ref=out.at[j], send_sem=send_sems.at[j],
                                         recv_sem=recv_sems.at[j], device_id=(x, y, 1 - c), device_id_type=MESH)
            for j in range(NCHIP)]


def _tn_matmul(xa, dy, name, nb, k1, n, x_batched, dy_mode, tt, ctx_kv=None, carry=None):
    ln = xa.shape[-2]
    tt = min(tt, ln)
    steps = ln // tt
    n_in = 2 + (ctx_kv is not None) + (carry is not None)

    def body(x_ref, dy_ref, *refs):
        o_ref = refs[n_in - 2]
        if carry is not None:
            g_ref, got_ref = refs[n_in - 3], refs[n_in - 1]
            send_sems, recv_sems = refs[n_in:]

        @pl.when(pl.program_id(0) == 0)
        def _():
            if carry is not None:
                for cp in _exchange_copies(g_ref, got_ref, send_sems, recv_sems):
                    cp.start()
            o_ref[...] = jnp.zeros_like(o_ref)
            if ctx_kv is not None:
                for g in range(0, 2 * AW, HD):
                    col = 3 * AW + g
                    o_ref[col // n, :, col % n: col % n + HD] = refs[0][:, g:g + HD]

        xt = None if x_batched else jnp.transpose(x_ref[...])
        for b in range(nb):
            lhs = jnp.transpose(x_ref[b]) if x_batched else xt
            if dy_mode == "batched":
                rhs = dy_ref[b]
            elif dy_mode == "cols":
                rhs = dy_ref[:, b * n:(b + 1) * n]
            else:
                rhs = dy_ref[...]
            o_ref[b] += _dot(lhs, rhs, NN)

        if carry is not None:
            @pl.when(pl.program_id(0) == steps - 1)
            def _():
                cps = _exchange_copies(g_ref, got_ref, send_sems, recv_sems)
                for cp in cps:
                    cp.wait_recv()
                for cp in cps:
                    cp.wait_send()

    x_spec = (pl.BlockSpec((nb, tt, k1), lambda t: (0, t, 0)) if x_batched
              else pl.BlockSpec((tt, k1), lambda t: (t, 0)))
    if dy_mode == "batched":
        dy_spec = pl.BlockSpec((nb, tt, n), lambda t: (0, t, 0))
    elif dy_mode == "cols":
        dy_spec = pl.BlockSpec((tt, nb * n), lambda t: (t, 0))
    else:
        dy_spec = pl.BlockSpec((tt, n), lambda t: (t, 0))
    hbm = pl.BlockSpec(memory_space=pl.ANY)
    extra = [] if ctx_kv is None else [ctx_kv]
    in_specs = [x_spec, dy_spec] + [_const(e.shape) for e in extra]
    out_specs = [pl.BlockSpec((nb, k1, n), lambda t: (0, 0, 0))]
    out_shape = [jax.ShapeDtypeStruct((nb, k1, n), F32)]
    scratch = []
    if carry is not None:
        extra = extra + [carry]
        in_specs.append(hbm)
        out_specs.append(hbm)
        out_shape.append(jax.ShapeDtypeStruct((NCHIP,) + carry.shape[2:], F32))
        scratch = [pltpu.SemaphoreType.DMA((NCHIP,))] * 2
    out = pl.pallas_call(
        body, name=name, grid=(steps,),
        in_specs=in_specs, out_specs=out_specs, out_shape=out_shape, scratch_shapes=scratch,
        compiler_params=_params(60, ("arbitrary",)),
    )(xa, dy, *extra)
    return out[0] if carry is None else (out[0], out[1])


FWD_CHUNKS_PER_STEP = 4
BWD_CHUNKS_PER_STEP = 4


def _chunks_per_step(nc, want):
    return want if nc % want == 0 else 1


def _mixer_fwd(z, cos_t, sin_t, dm, sgw, gain, bfull, scf, scb, gbufs_a, gbufs_b):
    ln = z.shape[0]
    nc = ln // CH
    na = len(gbufs_a)
    gbufs = list(gbufs_a) + list(gbufs_b)
    nt = len(gbufs)
    cps = _chunks_per_step(nc, FWD_CHUNKS_PER_STEP)
    nb = nc // cps
    rows = cps * CH
    mid = nb // 2

    def rev(p, n):
        return p * n + (1 - p) * (nb - 1 - n)

    def col(j, both):
        if both:
            return pl.BlockSpec((rows, AW), lambda p, n: (rev(p, n), j))
        return pl.BlockSpec((rows, AW), lambda p, n: (p * n, j))

    def body(u_ref, v_ref, q_ref, k_ref, vr_ref, gf_ref, gb_ref, cos_ref, sin_ref, dm_ref, sgw_ref, gain_ref,
             bfull_ref, scf_ref, scb_ref, *refs):
        y_ref, sf_ref, sb_ref, of_ref = refs[nt:nt + 4]
        bufs = refs[nt + 4:2 * nt + 4]
        bufs_a, bufs_b = bufs[:na], bufs[na:]
        sb_all, st, a_send, a_recv, bi_send, bi_recv, bd_send, bd_recv = refs[2 * nt + 4:]
        p, n = pl.program_id(0), pl.program_id(1)

        @pl.when((p == 0) & (n == 0))
        def _():
            for cp in _gather_d2d_copies(bufs_a, a_send, a_recv)[0]:
                cp.start()
            for cp in _gather_ici_copies(bufs_b, bi_send, bi_recv)[0]:
                cp.start()

        @pl.when((p == 1) & (n == mid))
        def _():
            for cp in _gather_ici_copies(bufs_b, bi_send, bi_recv)[1]:
                cp.wait_recv()
            for cp in _gather_d2d_copies(bufs_b, bd_send, bd_recv)[0]:
                cp.start()

        def roped_k(r0):
            cos, sin = cos_ref[r0:r0 + CH, :], sin_ref[r0:r0 + CH, :]
            return [_rope(t, cos, sin) * K_SCALE for t in _heads(k_ref, r0)]

        @pl.when(p == 0)
        def _():
            @pl.when(n == 0)
            def _():
                st[...] = scb_ref[...]

            for s in reversed(range(cps)):
                m = (nb - 1 - n) * cps + s
                k, vr = roped_k(s * CH), _heads(vr_ref, s * CH)
                for h in range(NH):
                    sb_all[m, h] = st[h]
                    st[h] = dm_ref[1, 3, h] * st[h] + _mm_tn(k[h], dm_ref[1, 2, h] * vr[h])

        @pl.when(p == 1)
        def _():
            @pl.when(n == 0)
            def _():
                st[...] = scf_ref[...]

            mats = [[dm_ref[d, kind, h] for h in range(NH)] for d in range(2) for kind in range(3)]
            for s in range(cps):
                r0 = s * CH
                m = n * cps + s
                cos, sin = cos_ref[r0:r0 + CH, :], sin_ref[r0:r0 + CH, :]
                q = [_rope(t, cos, sin) for t in _heads(q_ref, r0)]
                k, vr = roped_k(r0), _heads(vr_ref, r0)
                u, v, gf, gb = _heads(u_ref, r0), _heads(v_ref, r0), _heads(gf_ref, r0), _heads(gb_ref, r0)
                cols = [slice(h * HD, (h + 1) * HD) for h in range(NH)]
                ya = [_gate_group(u[h], v[h], sgw_ref[h], gain_ref[:, cols[h]], bfull_ref[h]) for h in range(NH)]
                sf = [st[h] for h in range(NH)]
                sb = [sb_all[m, h] for h in range(NH)]
                ret = [_ret_head(q[h], k[h], vr[h], gf[h], gb[h], sf[h], sb[h], mats[0][h], mats[1][h], mats[2][h],
                                 mats[3][h], mats[4][h], mats[5][h]) for h in range(NH)]
                for h in range(NH):
                    yr, uf, _, of = ret[h]
                    y_ref[r0:r0 + CH, cols[h]] = ya[h].astype(BF)
                    y_ref[r0:r0 + CH, AW + h * HD:AW + (h + 1) * HD] = yr.astype(BF)
                    of_ref[r0:r0 + CH, cols[h]] = of
                    sf_ref[s, h] = sf[h]
                    sb_ref[s, h] = sb[h]
                    st[h] = dm_ref[0, 3, h] * sf[h] + uf

        @pl.when((p == 1) & (n == nb - 1))
        def _():
            a_out, a_in = _gather_d2d_copies(bufs_a, a_send, a_recv)
            b_out, b_in = _gather_d2d_copies(bufs_b, bd_send, bd_recv)
            for cp in a_in + b_in:
                cp.wait_recv()
            for cp in a_out + b_out + _gather_ici_copies(bufs_b, bi_send, bi_recv)[0]:
                cp.wait_send()

    hbm = pl.BlockSpec(memory_space=pl.ANY)
    tab = pl.BlockSpec((rows, HD), lambda p, n: (rev(p, n), 0))
    st_spec = pl.BlockSpec((cps, NH, HD, HD), lambda p, n: (p * n, 0, 0, 0))
    out = pl.pallas_call(
        body, name="mixer_fwd", grid=(2, nb),
        in_specs=[col(0, False), col(1, False), col(2, False), col(3, True), col(4, True), col(5, False),
                  col(6, False), tab, tab, _const((2, 4, NH, CH, CH)), _const((NH, CH, CH)), _const((1, AW)),
                  _const((NH, CH, CH)), _const((NH, HD, HD)), _const((NH, HD, HD))] + [hbm] * nt,
        out_specs=[pl.BlockSpec((rows, D), lambda p, n: (p * n, 0)), st_spec, st_spec,
                   pl.BlockSpec((rows, AW), lambda p, n: (p * n, 0))] + [hbm] * nt,
        out_shape=[jax.ShapeDtypeStruct((ln, D), BF), jax.ShapeDtypeStruct((nc, NH, HD, HD), F32),
                   jax.ShapeDtypeStruct((nc, NH, HD, HD), F32), jax.ShapeDtypeStruct((ln, AW), F32)]
        + [jax.ShapeDtypeStruct(g.shape, g.dtype) for g in gbufs],
        input_output_aliases={15 + k: 4 + k for k in range(nt)},
        scratch_shapes=[pltpu.VMEM((nc, NH, HD, HD), F32), pltpu.VMEM((NH, HD, HD), F32)]
        + [pltpu.SemaphoreType.DMA(((NCHIP - 1) * na,))] * 2
        + [pltpu.SemaphoreType.DMA(((NCHIP - 1) * (nt - na),))] * 4,
        compiler_params=_params(56, ("arbitrary", "arbitrary")),
    )(z, z, z, z, z, z, z, cos_t, sin_t, dm, sgw, gain, bfull, scf, scb, *gbufs)
    return out[0], out[1], out[2], out[3], out[4:]


def _mixer_bwd(z, dycat, of_all, cos_t, sin_t, dm, sgw, gain, bfull, sf_all, sb_all, parts):
    ln = z.shape[0]
    nc = ln // CH
    cps = _chunks_per_step(nc, BWD_CHUNKS_PER_STEP)
    nb = nc // cps
    rows = cps * CH

    def rev(p, n):
        return p * n + (1 - p) * (nb - 1 - n)

    def col(j, both):
        if both:
            return pl.BlockSpec((rows, AW), lambda p, n: (rev(p, n), j))
        return pl.BlockSpec((rows, AW), lambda p, n: (p * n, j))

    nt = len(parts)

    def body(u_ref, v_ref, q_ref, k_ref, vr_ref, gf_ref, gb_ref, dya_ref, dyr_ref, of_ref, cos_ref, sin_ref,
             dm_ref, sgw_ref, gain_ref, bfull_ref, sf_ref, sb_ref, *refs):
        ps = refs[:nt]
        dz_ref, ddm_ref, dsgw_ref, dgain_ref, dbf_ref, dscf_ref, dscb_ref = refs[nt:nt + 7]
        got = refs[nt + 7:2 * nt + 7]
        gf_all, run, send_sems, recv_sems = refs[2 * nt + 7:]
        p, n = pl.program_id(0), pl.program_id(1)

        @pl.when((p == 0) & (n == 0))
        def _():
            for cp in _scatter_ici_copies(ps, got, send_sems, recv_sems):
                cp.start()

        mats = [[dm_ref[d, kind, h] for h in range(NH)] for d in range(2) for kind in range(3)]

        @pl.when(p == 0)
        def _():
            @pl.when(n == 0)
            def _():
                run[...] = jnp.zeros_like(run)
                ddm_ref[...] = jnp.zeros_like(ddm_ref)
                dsgw_ref[...] = jnp.zeros_like(dsgw_ref)
                dgain_ref[...] = jnp.zeros_like(dgain_ref)
                dbf_ref[...] = jnp.zeros_like(dbf_ref)

            for s in reversed(range(cps)):
                r0 = s * CH
                m = (nb - 1 - n) * cps + s
                cos, sin = cos_ref[r0:r0 + CH, :], sin_ref[r0:r0 + CH, :]
                q = [_rope(t, cos, sin) for t in _heads(q_ref, r0)]
                gf, dyr, of = _heads(gf_ref, r0), _heads(dyr_ref, r0), _heads(of_ref, r0)
                for h in range(NH):
                    _, vjp = jax.vjp(functools.partial(_gated_norm, gf[h]), of[h])
                    (dof,) = vjp(dyr[h])
                    dsf = _mm_tn(q[h], mats[1][h] * dof)
                    g_next = run[h]
                    gf_all[m, h] = g_next.astype(BF)
                    ddm_ref[0, 3, h] += sf_ref[s, h] * g_next
                    run[h] = dsf + dm_ref[0, 3, h] * g_next

            @pl.when(n == nb - 1)
            def _():
                dscf_ref[...] = run[...]

        @pl.when(p == 1)
        def _():
            @pl.when(n == 0)
            def _():
                run[...] = jnp.zeros_like(run)

            for s in range(cps):
                r0 = s * CH
                m = n * cps + s
                rw = slice(r0, r0 + CH)
                cos, sin = cos_ref[r0:r0 + CH, :], sin_ref[r0:r0 + CH, :]
                q = [_rope(t, cos, sin) for t in _heads(q_ref, r0)]
                k = [_rope(t, cos, sin) * K_SCALE for t in _heads(k_ref, r0)]
                vr, gf, gb, dyr = _heads(vr_ref, r0), _heads(gf_ref, r0), _heads(gb_ref, r0), _heads(dyr_ref, r0)
                u, v, dya = _heads(u_ref, r0), _heads(v_ref, r0), _heads(dya_ref, r0)
                sf = [sf_ref[s, h] for h in range(NH)]
                sb = [sb_ref[s, h] for h in range(NH)]
                g_f = [gf_all[m, h].astype(F32) for h in range(NH)]
                g_b = [run[h] for h in range(NH)]
                cols = [slice(h * HD, (h + 1) * HD) for h in range(NH)]

                def chunk(u_, v_, sgw_, gain_, bfull_, q_, k_, vr_, gf_, gb_, sb_, df, xf, zf, db, xb, zb, sf=sf):
                    ya = [_gate_group(u_[h], v_[h], sgw_[h], gain_[h], bfull_[h]) for h in range(NH)]
                    ret = [_ret_head(q_[h], k_[h], vr_[h], gf_[h], gb_[h], sf[h], sb_[h], df[h], xf[h], zf[h],
                                     db[h], xb[h], zb[h])[:3] for h in range(NH)]
                    return ya, ret

                _, vjp = jax.vjp(chunk, u, v, [sgw_ref[h] for h in range(NH)], [gain_ref[:, c] for c in cols],
                                 [bfull_ref[h] for h in range(NH)], q, k, vr, gf, gb, sb, *mats)
                (du, dv, dsgw, dgain, dbf, dq, dk, dvr, dgf, dgb, dsb, ddf, dxf, dzf, ddb, dxb, dzb) = vjp(
                    (dya, [(dyr[h], g_f[h], g_b[h]) for h in range(NH)]))
                for h in range(NH):
                    dz_ref[rw, h * HD:(h + 1) * HD] = du[h].astype(BF)
                    dz_ref[rw, AW + h * HD:AW + (h + 1) * HD] = dv[h].astype(BF)
                    dz_ref[rw, 2 * AW + h * HD:2 * AW + (h + 1) * HD] = _rope_bwd(dq[h], cos, sin).astype(BF)
                    dz_ref[rw, 3 * AW + h * HD:3 * AW + (h + 1) * HD] = _rope_bwd(dk[h] * K_SCALE, cos,
                                                                                  sin).astype(BF)
                    dz_ref[rw, 4 * AW + h * HD:4 * AW + (h + 1) * HD] = dvr[h].astype(BF)
                    dz_ref[rw, 5 * AW + h * HD:5 * AW + (h + 1) * HD] = dgf[h].astype(BF)
                    dz_ref[rw, 6 * AW + h * HD:6 * AW + (h + 1) * HD] = dgb[h].astype(BF)
                    ddm_ref[0, 0, h] += ddf[h]
                    ddm_ref[0, 1, h] += dxf[h]
                    ddm_ref[0, 2, h] += dzf[h]
                    ddm_ref[1, 0, h] += ddb[h]
                    ddm_ref[1, 1, h] += dxb[h]
                    ddm_ref[1, 2, h] += dzb[h]
                    ddm_ref[1, 3, h] += sb[h] * g_b[h]
                    dsgw_ref[h] += dsgw[h]
                    dgain_ref[:, cols[h]] += dgain[h]
                    dbf_ref[h] += dbf[h]
                    run[h] = dsb[h] + dm_ref[1, 3, h] * g_b[h]

            @pl.when(n == nb - 1)
            def _():
                dscb_ref[...] = run[...]

        @pl.when((p == 1) & (n == nb - 1))
        def _():
            cps_ = _scatter_ici_copies(ps, got, send_sems, recv_sems)
            for cp in cps_:
                cp.wait_recv()
            for cp in cps_:
                cp.wait_send()

    hbm = pl.BlockSpec(memory_space=pl.ANY)
    tab = pl.BlockSpec((rows, HD), lambda p, n: (rev(p, n), 0))
    tile4 = jax.ShapeDtypeStruct((NH, CH, CH), F32)
    out = pl.pallas_call(
        body, name="mixer_bwd", grid=(2, nb),
        in_specs=[col(0, False), col(1, False), col(2, True), col(3, False), col(4, False), col(5, True),
                  col(6, False),
                  pl.BlockSpec((rows, AW), lambda p, n: (p * n, 0)),
                  pl.BlockSpec((rows, AW), lambda p, n: (rev(p, n), 1)),
                  pl.BlockSpec((rows, AW), lambda p, n: ((1 - p) * (nb - 1 - n), 0)),
                  tab, tab, _const((2, 4, NH, CH, CH)), _const((NH, CH, CH)), _const((1, AW)),
                  _const((NH, CH, CH)),
                  pl.BlockSpec((cps, NH, HD, HD), lambda p, n: (rev(p, n), 0, 0, 0)),
                  pl.BlockSpec((cps, NH, HD, HD), lambda p, n: (p * n, 0, 0, 0))] + [hbm] * nt,
        out_specs=[pl.BlockSpec((rows, IN_COLS), lambda p, n: (p * n, 0)),
                   pl.BlockSpec((2, 4, NH, CH, CH), lambda p, n: (0, 0, 0, 0, 0)),
                   pl.BlockSpec((NH, CH, CH), lambda p, n: (0, 0, 0)),
                   pl.BlockSpec((1, AW), lambda p, n: (0, 0)),
                   pl.BlockSpec((NH, CH, CH), lambda p, n: (0, 0, 0)),
                   pl.BlockSpec((NH, HD, HD), lambda p, n: (0, 0, 0)),
                   pl.BlockSpec((NH, HD, HD), lambda p, n: (0, 0, 0))] + [hbm] * nt,
        out_shape=[jax.ShapeDtypeStruct((ln, IN_COLS), BF), jax.ShapeDtypeStruct((2, 4, NH, CH, CH), F32),
                   tile4, jax.ShapeDtypeStruct((1, AW), F32), tile4, tile4, tile4]
        + [jax.ShapeDtypeStruct((NCHIP - 1,) + p.shape[1:], BF) for p in parts],
        scratch_shapes=[pltpu.VMEM((nc, NH, HD, HD), BF), pltpu.VMEM((NH, HD, HD), F32)]
        + [pltpu.SemaphoreType.DMA(((NCHIP - 1) * nt,))] * 2,
        compiler_params=_params(60, ("arbitrary", "arbitrary")),
    )(z, z, z, z, z, z, z, dycat, dycat, of_all, cos_t, sin_t, dm, sgw, gain, bfull, sf_all, sb_all, *parts)
    return out[:7], out[7:]


def _small_reduce(ddm, ddm_ctx, dm, dbf):
    def body(ddm_ref, dctx_ref, dm_ref, dbf_ref, lg_ref, sgb_ref):
        ex = _decay_exponents()
        ones = jnp.ones((8, CH), F32)
        for d in range(2):
            for h in range(NH):
                tot = jnp.zeros((CH, CH), F32)
                for kind in range(4):
                    g = ddm_ref[d, kind, h]
                    if kind >= 2:
                        g = g + dctx_ref[d, kind - 2, h]
                    tot = tot + g * dm_ref[d, kind, h] * ex[d][kind]
                lg_ref[d * NH + h: d * NH + h + 1, :] = jnp.sum(tot, axis=0, keepdims=True)
        sgb_ref[...] = jnp.zeros_like(sgb_ref)
        for g in range(NH):
            r = lax.dot_general(ones, dbf_ref[g], (NT, ((), ())), precision=HI, preferred_element_type=F32)
            sgb_ref[g:g + 1, :] = r[0:1, :]

    return pl.pallas_call(
        body, name="small_reduce",
        out_shape=[jax.ShapeDtypeStruct((8, CH), F32), jax.ShapeDtypeStruct((8, CH), F32)],
        in_specs=[pl.BlockSpec(memory_space=pltpu.VMEM)] * 4,
        out_specs=[pl.BlockSpec(memory_space=pltpu.VMEM)] * 2,
        compiler_params=_params(32),
    )(ddm, ddm_ctx, dm, dbf)


def _mod_backward(ct_pad_t, cctx_col, dmod_pad, dcmod_cols, w_mod_s):
    def body(ct_ref, cc_ref, dm_ref, dc_ref, w_ref, gw_ref, part_ref):
        dcm = dc_ref[0:1, :]
        for d in range(1, NDEV):
            dcm = dcm + dc_ref[d:d + 1, :]
        gw_ref[...] = (jnp.dot(_silu(ct_ref[...]), dm_ref[...], precision=HI, preferred_element_type=F32)
                       + _silu(cc_ref[...]) * dcm)
        part_ref[...] = lax.dot_general(jnp.broadcast_to(dcm, (8, dcm.shape[1])), w_ref[...], (NT, ((), ())),
                                        precision=HI, preferred_element_type=F32)

    return pl.pallas_call(
        body, name="mod_backward",
        out_shape=[jax.ShapeDtypeStruct(w_mod_s.shape, F32), jax.ShapeDtypeStruct((8, D), F32)],
        in_specs=[pl.BlockSpec(memory_space=pltpu.VMEM)] * 5,
        out_specs=[pl.BlockSpec(memory_space=pltpu.VMEM)] * 2,
        compiler_params=_params(48),
    )(ct_pad_t, cctx_col, dmod_pad, dcmod_cols, w_mod_s)


def _cctx_update(parts, c_ctx, m, v):
    def body(p_ref, c_ref, m_ref, v_ref, g_ref, d_ref, mo_ref, vo_ref):
        tot = ((p_ref[0] + p_ref[2]) + p_ref[4]) + p_ref[6]
        cv = c_ref[...]
        s = jax.nn.sigmoid(cv)
        g = tot * (s * (1.0 + cv * (1.0 - s)))
        g_ref[...] = g
        d_ref[...], mo_ref[...], vo_ref[...] = _adamw_math(cv, g, m_ref[...], v_ref[...])

    return pl.pallas_call(
        body, name="cctx_update",
        out_shape=[jax.ShapeDtypeStruct((1, D), F32)] * 4,
        in_specs=[pl.BlockSpec(memory_space=pltpu.VMEM)] * 4,
        out_specs=[pl.BlockSpec(memory_space=pltpu.VMEM)] * 4,
        compiler_params=_params(16),
    )(parts, c_ctx, m, v)


def _small_update(gathered, wp, mp, vp):
    def body(g_ref, w_ref, m_ref, v_ref, go_ref, d_ref, mo_ref, vo_ref, loss_ref):
        tot = g_ref[0]
        for d in range(1, NDEV):
            tot = tot + g_ref[d]
        go_ref[Q_BMOD:Q_N1, :] = tot[P_DMOD:P_N1, :] + tot[P_DCMOD:P_DMOD, :]
        go_ref[Q_N1:Q_LG, :] = tot[P_N1:P_LG, :]
        lg = jnp.sum(tot[P_LG:P_N2, :], axis=1, keepdims=True)
        go_ref[Q_LG:Q_N2, :] = lg * jax.nn.sigmoid(-w_ref[Q_LG:Q_N2, :])
        go_ref[Q_N2:Q_ROWS, :] = tot[P_N2:P_LOSS, :]
        d_ref[...], mo_ref[...], vo_ref[...] = _adamw_math(w_ref[...], go_ref[...], m_ref[...], v_ref[...])
        ls = jnp.sum(jnp.sum(tot[P_LOSS:P_ROWS, :], axis=1, keepdims=True), axis=0, keepdims=True)
        loss_ref[...] = jnp.broadcast_to(ls, (8, CH))

    return pl.pallas_call(
        body, name="small_update",
        out_shape=[jax.ShapeDtypeStruct((Q_ROWS, CH), F32)] * 4 + [jax.ShapeDtypeStruct((8, CH), F32)],
        in_specs=[pl.BlockSpec(memory_space=pltpu.VMEM)] * 4,
        out_specs=[pl.BlockSpec(memory_space=pltpu.VMEM)] * 5,
        compiler_params=_params(32),
    )(gathered, wp, mp, vp)


def _rows(a):
    r = a.reshape(-1, CH)
    return jnp.pad(r, ((0, -r.shape[0] % 8), (0, 0)))


def _pack_small(b_mod, norm1, sg_gain, sg_w, sg_b, lf, lb, norm2, norm_f):
    lg = jnp.broadcast_to(jnp.concatenate([lf.reshape(NH), lb.reshape(NH)])[:, None], (2 * NH, CH))
    return jnp.concatenate([_rows(b_mod), _rows(norm1), _rows(sg_gain), _rows(sg_w), _rows(sg_b), lg,
                            _rows(norm2), _rows(norm_f)], axis=0)


def _unpack_small(p):
    return (p[Q_BMOD:Q_N1].reshape(1, 6 * D), p[Q_N1:Q_GAIN].reshape(1, D), p[Q_GAIN:Q_GAIN + NH].reshape(1, AW),
            p[Q_SGW:Q_SGB].reshape(1, NH, CH, CH), p[Q_SGB:Q_SGB + NH].reshape(1, NH, CH),
            p[Q_LG:Q_LG + NH, 0].reshape(1, NH), p[Q_LG + NH:Q_N2, 0].reshape(1, NH),
            p[Q_N2:Q_NF].reshape(1, D), p[Q_NF:Q_ROWS].reshape(D))


def _rope_tables(ln):
    pos = np.arange(ln)
    rows = (pos // GRID_W).astype(np.float32)
    cols = (pos % GRID_W).astype(np.float32)
    n_freq = HD // 4
    inv = (np.float32(ROPE_BASE) ** (-np.arange(n_freq, dtype=np.float32) / np.float32(n_freq))).astype(np.float32)
    ar = rows[:, None] * inv[None, :]
    ac = cols[:, None] * inv[None, :]
    cos_t = np.concatenate([np.cos(ar), np.cos(ar), np.cos(ac), np.cos(ac)], axis=1).astype(np.float32)
    sin_t = np.concatenate([-np.sin(ar), np.sin(ar), -np.sin(ac), np.sin(ac)], axis=1).astype(np.float32)
    return jnp.asarray(cos_t), jnp.asarray(sin_t)


def kernel(x, c, ctx, c_ctx, w_mod, b_mod, norm1, w_in, sg_gain, sg_w, sg_b, ret_logit_f, ret_logit_b, w_out, norm2, w_gate, w_up, w_down, norm_f, loss_target, m_c_ctx, m_w_mod, m_b_mod, m_norm1, m_w_in, m_sg_gain, m_sg_w, m_sg_b, m_ret_logit_f, m_ret_logit_b, m_w_out, m_norm2, m_w_gate, m_w_up, m_w_down, m_norm_f, v_c_ctx, v_w_mod, v_b_mod, v_norm1, v_w_in, v_sg_gain, v_sg_w, v_sg_b, v_ret_logit_f, v_ret_logit_b, v_w_out, v_norm2, v_w_gate, v_w_up, v_w_down, v_norm_f):
    ln = x.shape[1]
    xi, yi, ci = _pos()
    chip = 2 * xi + yi
    me = 4 * xi + 2 * yi + ci
    x2d = x.reshape(ln, D)
    tgt = loss_target.reshape(ln, D)
    mod_c = w_mod.shape[2]

    tr = lambda a: jnp.swapaxes(a[0], 0, 1)
    gbufs, c_all, prod_all = _prologue(c, c_ctx.reshape(1, D), w_mod[0],
                                       [w_in[0], w_out[0], tr(w_gate), tr(w_up), w_down[0]])
    wi = gbufs[0].reshape(NCHIP, D, WI_C)
    gbufs_a, gbufs_b = gbufs[1:4], gbufs[4:5]
    c_all = c_all.reshape(NDEV, D)
    prod_chips = prod_all[0::2]
    mod_rows = jnp.transpose(prod_chips, (1, 0, 2)).reshape(16, NCHIP * mod_c) + b_mod
    mod = lax.dynamic_slice_in_dim(mod_rows, me, 1, axis=0)
    cmod = mod_rows[8:9]
    sh1, sc1, g1, sh2, sc2, g2 = [mod[:, i * D:(i + 1) * D] for i in range(6)]
    csh1, csc1 = cmod[:, 0:D], cmod[:, D:2 * D]
    zrow = jnp.zeros((1, D), F32)
    vec_in = jnp.concatenate([norm1, sh1, sc1] + [zrow] * 5, axis=0)
    vec_ctx = jnp.concatenate([norm1, csh1, csc1] + [zrow] * 5, axis=0)
    vec_post = jnp.concatenate([g1, norm2, sh2, sc2, g2, norm_f.reshape(1, D), zrow, zrow], axis=0)

    logits = jnp.concatenate([ret_logit_f.reshape(NH), ret_logit_b.reshape(NH)])
    dm = _decay_mats(jnp.broadcast_to(logits[:, None, None], (2 * NH, CH, CH)))
    ctx2d = ctx.reshape(ctx.shape[1], D)
    scf, scb = _ctx_forward(ctx2d, vec_ctx, wi, dm)

    cos_t, sin_t = _rope_tables(ln)
    z, hx, gbufs_a = _in_proj(x2d, vec_in, wi, gbufs_a)
    bfull = jnp.broadcast_to(sg_b[0][:, :, None], (NH, CH, CH))
    ycat, sf_all, sb_all, of_all, gbufs = _mixer_fwd(z, cos_t, sin_t, dm, sg_w[0], sg_gain, bfull, scf, scb,
                                             gbufs_a, gbufs_b)
    wo, wg_t, wu_t, wd = [g.reshape(-1, D) for g in gbufs]

    dx1, dycat, h2, dy, df, act, da, db, acc_post = _post_mixer(x2d, ycat, tgt, vec_post, wo, wg_t, wu_t, wd)

    cidx = ci.reshape(1).astype(jnp.int32)
    where = jnp.stack([ci, chip]).astype(jnp.int32)

    def halves_summed(full, names):
        full = [g.reshape(NCHIP, 2, g.shape[1] // 2, g.shape[2]) for g in full]
        from_sib = _rs_exchange_halves(full, "rs_exchange_" + names[0])
        return [_rs_add_halves(g, r, where, "rs_add_halves_" + nm) for g, r, nm in zip(full, from_sib, names)]

    def split(g):
        return g.reshape(NCHIP, 2, g.shape[1] // (2 * NCHIP), g.shape[2])

    g_wd = split(_tn_matmul(act, df, "grad_w_down", 1, DFF, D, False, "shared", 1024))
    g_wu, x_wd = _tn_matmul(db, h2, "grad_w_up", 1, DFF, D, False, "shared", 1024, carry=g_wd)
    g_wu = split(g_wu)
    g_wg, x_wu = _tn_matmul(da, h2, "grad_w_gate", 1, DFF, D, False, "shared", 1024, carry=g_wu)
    g_wg = split(g_wg)
    g_wo, x_wg = _tn_matmul(ycat, dy, "grad_w_out", 1, D, D, False, "shared", 1024, carry=g_wg)
    g_wo = split(g_wo)
    x_wo = _rs_exchange_halves([g_wo], "rs_exchange_w_out")[0]
    names = ["w_in", "w_out", "w_gate", "w_up", "w_down"]
    sums_b = [_rs_add_halves(g, r, where, "rs_add_halves_" + nm)
              for g, r, nm in zip([g_wo, g_wg, g_wu, g_wd], [x_wo, x_wg, x_wu, x_wd], names[1:])]

    (dz, ddm, dsgw, dgain, dbf, dscf, dscb), from_chips_b = _mixer_bwd(
        z, dycat, of_all, cos_t, sin_t, dm, sg_w[0], sg_gain, bfull, sf_all, sb_all, [s[1] for s in sums_b])
    gwkv, acc_ctx, ddm_ctx = _ctx_backward(ctx2d, vec_ctx, wi, dm, dscf, dscb)
    g_wi = _tn_matmul(hx, dz, "grad_w_in", NCHIP, D, WI_C, False, "cols", 512, ctx_kv=gwkv)
    sums_a = halves_summed([g_wi], names[:1])
    lg_part, dsgb = _small_reduce(ddm, ddm_ctx, dm, dbf)
    dcmod = jnp.concatenate([acc_ctx[1:2], acc_ctx[2:3], jnp.zeros((1, 4 * D), F32)], axis=1)
    dmod_rest = jnp.concatenate([acc_post[0:1], acc_post[2:3], acc_post[3:4], acc_post[4:5]], axis=1)
    early = jnp.concatenate([_rows(dcmod), _rows(dmod_rest), _rows(dgain), _rows(dsgw), dsgb, lg_part,
                             _rows(acc_post[1:2]), _rows(acc_post[5:6]), _rows(acc_post[6:7])], axis=0)
    gx, acc_in, from_chips_a, early_all = _in_proj_bwd(dz, x2d, dx1, vec_in, wi, [s[1] for s in sums_a], early)

    sums = sums_a + sums_b
    from_chips = list(from_chips_a) + list(from_chips_b)
    finals = [_rs_add_chips(s[0], r, "rs_add_chips_" + nm) for s, r, nm in zip(sums, from_chips, names)]
    others = _rs_share_final(finals)

    late = jnp.concatenate([_rows(acc_in[1:2]), _rows(acc_in[2:3]), _rows(acc_in[0:1] + acc_ctx[0:1])], axis=0)
    late_all = _allgather_small(late, "gather_small")
    n_dc, n_l = P_DMOD - P_DCMOD, 16
    gathered = jnp.concatenate([early_all[:, :n_dc], late_all[:, :n_l], early_all[:, n_dc:n_dc + 32],
                                late_all[:, n_l:], early_all[:, n_dc + 32:]], axis=1)
    dmod_all = gathered[:, P_DMOD:P_N1].reshape(NDEV, 6 * D)
    dcmod_all = gathered[:, P_DCMOD:P_DMOD].reshape(NDEV, 6 * D)
    dmod_cols = lax.dynamic_slice_in_dim(dmod_all, chip * mod_c, mod_c, axis=1)
    dcmod_cols = lax.dynamic_slice_in_dim(dcmod_all, chip * mod_c, mod_c, axis=1)
    dmod_pad = jnp.concatenate([dmod_cols, jnp.zeros((CH - NDEV, mod_c), F32)], axis=0)
    ct_pad_t = jnp.concatenate([jnp.transpose(c_all), jnp.zeros((D, CH - NDEV), F32)], axis=1)
    g_wmod, cctx_part = _mod_backward(ct_pad_t, c_ctx.reshape(D, 1), dmod_pad, dcmod_cols, w_mod[0])
    parts = _allgather_small(cctx_part[0:1], "gather_cctx")
    g_cctx, d_cctx, nm_cctx, nv_cctx = _cctx_update(parts, c_ctx.reshape(1, D), m_c_ctx.reshape(1, D),
                                                    v_c_ctx.reshape(1, D))

    wp = _pack_small(b_mod, norm1, sg_gain, sg_w, sg_b, ret_logit_f, ret_logit_b, norm2, norm_f)
    mp = _pack_small(m_b_mod, m_norm1, m_sg_gain, m_sg_w, m_sg_b, m_ret_logit_f, m_ret_logit_b, m_norm2, m_norm_f)
    vp = _pack_small(v_b_mod, v_norm1, v_sg_gain, v_sg_w, v_sg_b, v_ret_logit_f, v_ret_logit_b, v_norm2, v_norm_f)
    gp, dp, mp2, vp2, loss_t = _small_update(gathered, wp, mp, vp)

    big_w = [w_in[0], w_out[0], tr(w_gate), tr(w_up), w_down[0]]
    big_m = [m_w_in[0], m_w_out[0], tr(m_w_gate), tr(m_w_up), m_w_down[0]]
    big_v = [v_w_in[0], v_w_out[0], tr(v_w_gate), tr(v_w_up), v_w_down[0]]
    upd = [_adamw_halves(w, own, oth, m, v, cidx, "adamw_" + nm) for w, own, oth, m, v, nm in
           zip(big_w, finals, others, big_m, big_v, names)]
    big_g = [g_wmod] + [u[0] for u in upd]
    big = [_adamw(w_mod[0], g_wmod, m_w_mod[0], v_w_mod[0], "adamw_w_mod")] + [u[1:] for u in upd]

    def assemble(small, cctx, bigs):
        b_mod_, norm1_, gain_, sgw_, sgb_, lf_, lb_, norm2_, normf_ = _unpack_small(small)
        wm, wi_, wo_, wg_, wu_, wd_ = [b[None] for b in bigs]
        wg_, wu_ = jnp.swapaxes(wg_, 1, 2), jnp.swapaxes(wu_, 1, 2)
        return [cctx.reshape(D), wm, b_mod_, norm1_, wi_, gain_, sgw_, sgb_, lf_, lb_, wo_, norm2_, wg_, wu_, wd_,
                normf_]

    out = [loss_t[0, 0], gx.reshape(1, ln, D)]
    out += assemble(gp, g_cctx, big_g)
    out += assemble(dp, d_cctx, [b[0] for b in big])
    out += assemble(mp2, nm_cctx, [b[1] for b in big])
    out += assemble(vp2, nv_cctx, [b[2] for b in big])
    return tuple(out)
```

```python
import functools

import jax
import jax.numpy as jnp
import numpy as np
from jax import lax
from jax.experimental import pallas as pl
from jax.experimental.pallas import tpu as pltpu

F32 = jnp.float32
BF = jnp.bfloat16
MESH = pl.DeviceIdType.MESH

D = 1024
CH = 128
HD = 128
NH = 4
AW = 512
IN_COLS = 3584
DFF = 2816
NCHIP = 4
NDEV = 8
WI_C = IN_COLS // NCHIP
FF_C = DFF // NCHIP
WO_R = D // NCHIP
EPS = 1e-6
GRID_W = 64
ROPE_BASE = 10000.0
K_SCALE = HD ** -0.5
LR, B1, B2, AEPS, WD, STEP = 0.001, 0.9, 0.999, 1e-08, 0.01, 10
VMEM_MB = 1 << 20
HI = lax.Precision.HIGHEST

P_DCMOD, P_DMOD, P_N1, P_GAIN, P_SGW, P_SGB, P_LG, P_N2, P_NF, P_LOSS = 0, 48, 96, 104, 112, 624, 632, 640, 648, 656
P_ROWS = 664
Q_BMOD, Q_N1, Q_GAIN, Q_SGW, Q_SGB, Q_LG, Q_N2, Q_NF = 0, 48, 56, 64, 576, 584, 592, 600
Q_ROWS = 608


def _params(vmem_mb, sem=None):
    return pltpu.CompilerParams(vmem_limit_bytes=vmem_mb * VMEM_MB, dimension_semantics=sem)


def _const(shape):
    nd = len(shape)
    return pl.BlockSpec(shape, lambda *_: (0,) * nd, pipeline_mode=pl.Buffered(1))


def _pos():
    return lax.axis_index("x"), lax.axis_index("y"), lax.axis_index("c")


def _dot(a, b, dims):
    return lax.dot_general(a, b, (dims, ((), ())), preferred_element_type=F32)


NN = ((1,), (0,))
NT = ((1,), (1,))
TN = ((0,), (0,))


@jax.custom_vjp
def _mm(a, b):
    return _dot(a.astype(BF), b.astype(BF), NN)


def _mm_f(a, b):
    return _mm(a, b), (a.astype(BF), b.astype(BF))


def _mm_b(res, g):
    a, b = res
    gb = g.astype(BF)
    return _dot(gb, b, NT), _dot(a, gb, TN)


_mm.defvjp(_mm_f, _mm_b)


@jax.custom_vjp
def _mm_nt(a, b):
    return _dot(a.astype(BF), b.astype(BF), NT)


def _mm_nt_f(a, b):
    return _mm_nt(a, b), (a.astype(BF), b.astype(BF))


def _mm_nt_b(res, g):
    a, b = res
    gb = g.astype(BF)
    return _dot(gb, b, NN), _dot(gb, a, TN)


_mm_nt.defvjp(_mm_nt_f, _mm_nt_b)


@jax.custom_vjp
def _mm_tn(a, b):
    return _dot(a.astype(BF), b.astype(BF), TN)


def _mm_tn_f(a, b):
    return _mm_tn(a, b), (a.astype(BF), b.astype(BF))


def _mm_tn_b(res, g):
    a, b = res
    gb = g.astype(BF)
    return _dot(b, gb, NT), _dot(a, gb, NN)


_mm_tn.defvjp(_mm_tn_f, _mm_tn_b)


def _gelu(x):
    return x * (0.5 * (1.0 + jnp.tanh(0.7978845608028654 * (x + 0.044715 * (x * x * x)))))


def _silu(x):
    return x * jax.nn.sigmoid(x)


def _rms(x):
    return lax.rsqrt(jnp.mean(x * x, axis=-1, keepdims=True) + EPS)


def _swap32(t):
    lane = lax.broadcasted_iota(jnp.int32, t.shape, 1)
    first = (lane % 64) < 32
    return jnp.where(first, pltpu.roll(t, 96, 1), pltpu.roll(t, 32, 1))


def _rope(t, cos, sin):
    return t * cos + _swap32(t) * sin


def _rope_bwd(d, cos, sin):
    return d * cos + _swap32(d * sin)


def _heads(ref, r0=0):
    return [ref[r0:r0 + CH, h * HD:(h + 1) * HD].astype(F32) for h in range(NH)]


def _gate_group(u, v, sgw, gain, bfull):
    gv = _gelu(v)
    return _gelu(u) * (_mm(sgw, gv * _rms(gv) * gain) + bfull)


def _gated_norm(gate, o):
    return _silu(gate) * (o * _rms(o))


def _ret_head(q, k, vr, gf, gb, sf, sb, df, xf, zf, db, xb, zb):
    a = _mm_nt(q, k)
    of = _mm(a * df, vr) + xf * _mm(q, sf)
    ob = _mm(a * db, vr) + xb * _mm(q, sb)
    return _gated_norm(gf, of) + _gated_norm(gb, ob), _mm_tn(k, zf * vr), _mm_tn(k, zb * vr), of


def _ctx_states(ctx0, ctx1, n1, csh, csc, wk, wv, zf, zb, ef, eb):
    hc0 = (ctx0 * _rms(ctx0) * n1) * (1.0 + csc) + csh
    hc1 = (ctx1 * _rms(ctx1) * n1) * (1.0 + csc) + csh
    scf, scb = [], []
    for h in range(NH):
        k0, k1 = _mm(hc0, wk[h]) * K_SCALE, _mm(hc1, wk[h]) * K_SCALE
        v0, v1 = _mm(hc0, wv[h]), _mm(hc1, wv[h])
        scf.append(ef[h] * _mm_tn(k0, zf[h] * v0) + _mm_tn(k1, zf[h] * v1))
        scb.append(eb[h] * _mm_tn(k1, zb[h] * v1) + _mm_tn(k0, zb[h] * v0))
    return scf, scb


def _allgather_small(v, name):
    r, n = v.shape

    def body(v_ref, out_ref, send_sems, recv_sems, local_sem):
        x, y, c = _pos()
        me = 4 * x + 2 * y + c
        mine = pltpu.make_async_copy(v_ref, out_ref.at[me], local_sem)
        mine.start()
        sent = []
        for k in range(1, NDEV):
            kx, ky, kc = (k >> 2) & 1, (k >> 1) & 1, k & 1
            peer = (x ^ kx, y ^ ky, c ^ kc)
            cp = pltpu.make_async_remote_copy(src_ref=v_ref, dst_ref=out_ref.at[me], send_sem=send_sems.at[k - 1],
                                              recv_sem=recv_sems.at[k - 1], device_id=peer, device_id_type=MESH)
            cp.start()
            sent.append(cp)
        for k in range(1, NDEV):
            kx, ky, kc = (k >> 2) & 1, (k >> 1) & 1, k & 1
            peer = (x ^ kx, y ^ ky, c ^ kc)
            src = 4 * (x ^ kx) + 2 * (y ^ ky) + (c ^ kc)
            pltpu.make_async_remote_copy(src_ref=v_ref, dst_ref=out_ref.at[src], send_sem=send_sems.at[k - 1],
                                         recv_sem=recv_sems.at[k - 1], device_id=peer, device_id_type=MESH).wait_recv()
        for cp in sent:
            cp.wait_send()
        mine.wait()

    return pl.pallas_call(
        body, name=name,
        out_shape=jax.ShapeDtypeStruct((NDEV, r, n), F32),
        in_specs=[pl.BlockSpec(memory_space=pltpu.VMEM)],
        out_specs=pl.BlockSpec(memory_space=pltpu.VMEM),
        scratch_shapes=[pltpu.SemaphoreType.DMA((NDEV - 1,)), pltpu.SemaphoreType.DMA((NDEV - 1,)),
                        pltpu.SemaphoreType.DMA],
        compiler_params=_params(16),
    )(v)


def _chip_offsets():
    return [((k >> 1) & 1, k & 1) for k in range(1, NCHIP)]


def _prologue(c, c_ctx, w_mod_s, shards):
    nt = len(shards)
    shapes = [s.shape for s in shards]
    mod_c = w_mod_s.shape[1]

    def body(*refs):
        c_ref, cc_ref, wm_ref = refs[:3]
        srcs = refs[3:3 + nt]
        outs = refs[3 + nt:3 + 2 * nt]
        call_ref, prod_ref = refs[3 + 2 * nt:5 + 2 * nt]
        stages = refs[5 + 2 * nt:5 + 3 * nt]
        ct = refs[5 + 3 * nt]
        loaded = refs[6 + 3 * nt:6 + 4 * nt]
        (c_send, c_recv, p_send, p_recv, ici_send, ici_recv, d2d_send, d2d_recv, local_sems,
         load_sems) = refs[6 + 4 * nt:]
        x, y, c = _pos()
        chip = 2 * x + y
        me = 4 * x + 2 * y + c
        sib = (x, y, 1 - c)
        loads = [pltpu.make_async_copy(wm_ref, loaded[0], load_sems.at[0])]
        loads += [pltpu.make_async_copy(srcs[t], loaded[t], load_sems.at[t]) for t in range(1, nt)]
        for cp in loads:
            cp.start()
        pending = []

        def stage(t, src):
            half = shapes[t][0] // 2
            stages[t][0] = src[0:half, :].astype(BF)
            stages[t][1] = src[half:2 * half, :].astype(BF)
            cp = pltpu.make_async_copy(stages[t], outs[t].at[chip], local_sems.at[t])
            cp.start()
            pending.append(cp)

        stage(0, srcs[0])
        sends = []
        for k, (kx, ky) in enumerate(_chip_offsets()):
            cp = pltpu.make_async_remote_copy(src_ref=stages[0].at[c], dst_ref=outs[0].at[chip, c],
                                              send_sem=ici_send.at[k], recv_sem=ici_recv.at[k],
                                              device_id=(x ^ kx, y ^ ky, c), device_id_type=MESH)
            cp.start()
            sends.append(cp)

        def to_all(src, dst_of, send_sems, recv_sems):
            for k in range(1, NDEV):
                kx, ky, kc = (k >> 2) & 1, (k >> 1) & 1, k & 1
                cp = pltpu.make_async_remote_copy(src_ref=src, dst_ref=dst_of(me), send_sem=send_sems.at[k - 1],
                                                  recv_sem=recv_sems.at[k - 1], device_id=(x ^ kx, y ^ ky, c ^ kc),
                                                  device_id_type=MESH)
                cp.start()
                sends.append(cp)
            for k in range(1, NDEV):
                kx, ky, kc = (k >> 2) & 1, (k >> 1) & 1, k & 1
                frm = 4 * (x ^ kx) + 2 * (y ^ ky) + (c ^ kc)
                pltpu.make_async_remote_copy(src_ref=src, dst_ref=dst_of(frm), send_sem=send_sems.at[k - 1],
                                             recv_sem=recv_sems.at[k - 1], device_id=(x ^ kx, y ^ ky, c ^ kc),
                                             device_id_type=MESH).wait_recv()

        call_ref[me] = c_ref[...]
        to_all(c_ref, lambda d: call_ref.at[d], c_send, c_recv)
        ct[...] = jnp.zeros_like(ct)
        for d in range(NDEV):
            ct[d:d + 1, :] = call_ref[d]
        ct[NDEV:NDEV + 1, :] = cc_ref[...]
        loads[0].wait()
        prod_ref[me] = jnp.dot(_silu(ct[...]), loaded[0][...], precision=HI, preferred_element_type=F32)
        to_all(prod_ref.at[me], lambda d: prod_ref.at[d], p_send, p_recv)
        for t in range(1, nt):
            loads[t].wait()
            stage(t, loaded[t])

        for k, (kx, ky) in enumerate(_chip_offsets()):
            frm = 2 * (x ^ kx) + (y ^ ky)
            pltpu.make_async_remote_copy(src_ref=stages[0].at[c], dst_ref=outs[0].at[frm, c],
                                         send_sem=ici_send.at[k], recv_sem=ici_recv.at[k],
                                         device_id=(x ^ kx, y ^ ky, c), device_id_type=MESH).wait_recv()
            cp = pltpu.make_async_remote_copy(src_ref=outs[0].at[frm, c], dst_ref=outs[0].at[frm, c],
                                              send_sem=d2d_send.at[k], recv_sem=d2d_recv.at[k],
                                              device_id=sib, device_id_type=MESH)
            cp.start()
            sends.append(cp)
        for k, (kx, ky) in enumerate(_chip_offsets()):
            frm = 2 * (x ^ kx) + (y ^ ky)
            pltpu.make_async_remote_copy(src_ref=stages[0].at[c], dst_ref=outs[0].at[frm, 1 - c],
                                         send_sem=d2d_send.at[k], recv_sem=d2d_recv.at[k],
                                         device_id=sib, device_id_type=MESH).wait_recv()
        for cp in sends:
            cp.wait_send()
        for cp in pending:
            cp.wait()

    vm = pl.BlockSpec(memory_space=pltpu.VMEM)
    hbm = pl.BlockSpec(memory_space=pl.ANY)
    out = pl.pallas_call(
        body, name="prologue",
        out_shape=[jax.ShapeDtypeStruct((NCHIP, 2, r // 2, cc), BF) for r, cc in shapes]
        + [jax.ShapeDtypeStruct((NDEV, 1, D), F32), jax.ShapeDtypeStruct((NDEV, 16, mod_c), F32)],
        in_specs=[vm, vm, hbm, vm] + [hbm] * (nt - 1),
        out_specs=[hbm] * nt + [vm, vm],
        scratch_shapes=[pltpu.VMEM((2, r // 2, cc), BF) for r, cc in shapes] + [pltpu.VMEM((16, D), F32)]
        + [pltpu.VMEM(w_mod_s.shape, F32)] + [pltpu.VMEM(s, F32) for s in shapes[1:]]
        + [pltpu.SemaphoreType.DMA((NDEV - 1,))] * 4 + [pltpu.SemaphoreType.DMA((NCHIP - 1,))] * 4
        + [pltpu.SemaphoreType.DMA((nt,))] * 2,
        compiler_params=_params(56),
    )(c, c_ctx, w_mod_s, *shards)
    return out[:nt], out[nt], out[nt + 1]


def _gather_ici_copies(bufs, send_sems, recv_sems):
    x, y, c = _pos()
    chip = 2 * x + y
    nt = len(bufs)
    out_cp, in_cp = [], []
    for k, (kx, ky) in enumerate(_chip_offsets()):
        frm = 2 * (x ^ kx) + (y ^ ky)
        for t in range(nt):
            s = k * nt + t
            peer = (x ^ kx, y ^ ky, c)
            out_cp.append(pltpu.make_async_remote_copy(
                src_ref=bufs[t].at[chip, c], dst_ref=bufs[t].at[chip, c], send_sem=send_sems.at[s],
                recv_sem=recv_sems.at[s], device_id=peer, device_id_type=MESH))
            in_cp.append(pltpu.make_async_remote_copy(
                src_ref=bufs[t].at[chip, c], dst_ref=bufs[t].at[frm, c], send_sem=send_sems.at[s],
                recv_sem=recv_sems.at[s], device_id=peer, device_id_type=MESH))
    return out_cp, in_cp


def _gather_d2d_copies(bufs, send_sems, recv_sems):
    x, y, c = _pos()
    nt = len(bufs)
    out_cp, in_cp = [], []
    for k, (kx, ky) in enumerate(_chip_offsets()):
        frm = 2 * (x ^ kx) + (y ^ ky)
        for t in range(nt):
            s = k * nt + t
            out_cp.append(pltpu.make_async_remote_copy(
                src_ref=bufs[t].at[frm, c], dst_ref=bufs[t].at[frm, c], send_sem=send_sems.at[s],
                recv_sem=recv_sems.at[s], device_id=(x, y, 1 - c), device_id_type=MESH))
            in_cp.append(pltpu.make_async_remote_copy(
                src_ref=bufs[t].at[frm, c], dst_ref=bufs[t].at[frm, 1 - c], send_sem=send_sems.at[s],
                recv_sem=recv_sems.at[s], device_id=(x, y, 1 - c), device_id_type=MESH))
    return out_cp, in_cp


def _scatter_ici_copies(parts, outs, send_sems, recv_sems):
    x, y, c = _pos()
    nt = len(parts)
    cps = []
    for k, (kx, ky) in enumerate(_chip_offsets()):
        dst_chip = 2 * (x ^ kx) + (y ^ ky)
        for t in range(nt):
            s = k * nt + t
            cps.append(pltpu.make_async_remote_copy(
                src_ref=parts[t].at[dst_chip], dst_ref=outs[t].at[k], send_sem=send_sems.at[s],
                recv_sem=recv_sems.at[s], device_id=(x ^ kx, y ^ ky, c), device_id_type=MESH))
    return cps


def _rs_exchange_halves(grads, name):
    nt = len(grads)
    shapes = [g.shape for g in grads]

    def body(*refs):
        gs, outs = refs[:nt], refs[nt:2 * nt]
        send_sems, recv_sems = refs[2 * nt:]
        x, y, c = _pos()
        sib = (x, y, 1 - c)
        sent = []
        for t in range(nt):
            for j in range(NCHIP):
                s = t * NCHIP + j
                cp = pltpu.make_async_remote_copy(src_ref=gs[t].at[j, 1 - c], dst_ref=outs[t].at[j],
                                                  send_sem=send_sems.at[s], recv_sem=recv_sems.at[s],
                                                  device_id=sib, device_id_type=MESH)
                cp.start()
                sent.append(cp)
        for cp in sent:
            cp.wait_recv()
        for cp in sent:
            cp.wait_send()

    return pl.pallas_call(
        body, name=name,
        out_shape=[jax.ShapeDtypeStruct((NCHIP, s[2], s[3]), F32) for s in shapes],
        in_specs=[pl.BlockSpec(memory_space=pl.ANY)] * nt,
        out_specs=[pl.BlockSpec(memory_space=pl.ANY)] * nt,
        scratch_shapes=[pltpu.SemaphoreType.DMA((nt * NCHIP,))] * 2,
    )(*grads)


def _rs_share_final(finals):
    nt = len(finals)
    shapes = [f.shape for f in finals]

    def body(*refs):
        fs, outs = refs[:nt], refs[nt:2 * nt]
        send_sems, recv_sems = refs[2 * nt:]
        x, y, c = _pos()
        sent = []
        for t in range(nt):
            cp = pltpu.make_async_remote_copy(src_ref=fs[t], dst_ref=outs[t], send_sem=send_sems.at[t],
                                              recv_sem=recv_sems.at[t], device_id=(x, y, 1 - c), device_id_type=MESH)
            cp.start()
            sent.append(cp)
        for cp in sent:
            cp.wait_recv()
        for cp in sent:
            cp.wait_send()

    return pl.pallas_call(
        body, name="rs_share_final",
        out_shape=[jax.ShapeDtypeStruct(s, F32) for s in shapes],
        in_specs=[pl.BlockSpec(memory_space=pl.ANY)] * nt,
        out_specs=[pl.BlockSpec(memory_space=pl.ANY)] * nt,
        scratch_shapes=[pltpu.SemaphoreType.DMA((nt,))] * 2,
    )(*finals)


def _row_tile(h, cc=D):
    for t in (512, 384, 352, 256, 176, 128, 64, 32, 16):
        if h % t == 0 and t * cc * 4 <= (5 * VMEM_MB) // 4:
            return t
    return h


def _rs_add_halves(g, recv, where, name):
    _, _, h, cc = g.shape
    th = _row_tile(h, cc)

    def body(w_ref, g_ref, r_ref, own_ref, ob_ref):
        s = g_ref[...] + r_ref[...]
        ob_ref[...] = s.astype(BF)

        @pl.when(pl.program_id(1) == w_ref[1])
        def _():
            own_ref[...] = s

    return pl.pallas_call(
        body, name=name,
        grid_spec=pltpu.PrefetchScalarGridSpec(
            num_scalar_prefetch=1, grid=(h // th, NCHIP),
            in_specs=[pl.BlockSpec((None, None, th, cc), lambda i, j, w_ref: (j, w_ref[0], i, 0)),
                      pl.BlockSpec((None, th, cc), lambda i, j, w_ref: (j, i, 0))],
            out_specs=[pl.BlockSpec((th, cc), lambda i, j, w_ref: (i, 0)),
                       pl.BlockSpec((None, th, cc), lambda i, j, w_ref: (j, i, 0))]),
        out_shape=[jax.ShapeDtypeStruct((h, cc), F32), jax.ShapeDtypeStruct((NCHIP, h, cc), BF)],
        compiler_params=_params(48, ("arbitrary", "arbitrary")),
    )(where, g, recv)


def _rs_add_chips(own, recv, name):
    h, cc = own.shape
    th = _row_tile(h, cc)

    def body(o_ref, r_ref, out_ref):
        out_ref[...] = ((o_ref[...] + r_ref[0].astype(F32)) + r_ref[1].astype(F32)) + r_ref[2].astype(F32)

    return pl.pallas_call(
        body, name=name, grid=(h // th,),
        in_specs=[pl.BlockSpec((th, cc), lambda i: (i, 0)), pl.BlockSpec((NCHIP - 1, th, cc), lambda i: (0, i, 0))],
        out_specs=pl.BlockSpec((th, cc), lambda i: (i, 0)),
        out_shape=jax.ShapeDtypeStruct((h, cc), F32),
        compiler_params=_params(48, ("parallel",)),
    )(own, recv)


def _adamw_math(w, g, m, v):
    m2 = B1 * m + (1.0 - B1) * g
    v2 = B2 * v + (1.0 - B2) * (g * g)
    m_hat = m2 / (1.0 - B1 ** STEP)
    v_hat = v2 / (1.0 - B2 ** STEP)
    delta = -LR * (m_hat / (jnp.sqrt(v_hat) + AEPS) + WD * w)
    return delta, m2, v2


def _adamw(w, g, m, v, name):
    r, cc = w.shape
    tr = _row_tile(r, cc)

    def body(w_ref, g_ref, m_ref, v_ref, d_ref, mo_ref, vo_ref):
        d, m2, v2 = _adamw_math(w_ref[...], g_ref[...], m_ref[...], v_ref[...])
        d_ref[...] = d
        mo_ref[...] = m2
        vo_ref[...] = v2

    spec = pl.BlockSpec((tr, cc), lambda i: (i, 0))
    return pl.pallas_call(
        body, name=name, grid=(r // tr,), in_specs=[spec] * 4, out_specs=[spec] * 3,
        out_shape=[jax.ShapeDtypeStruct((r, cc), F32)] * 3,
        compiler_params=_params(48, ("parallel",)),
    )(w, g, m, v)


def _adamw_halves(w, own, other, m, v, cidx, name):
    r, cc = w.shape
    h = r // 2
    tr = _row_tile(h, cc)
    per = h // tr

    def body(c_ref, w_ref, own_ref, oth_ref, m_ref, v_ref, g_ref, d_ref, mo_ref, vo_ref):
        mine = (pl.program_id(0) // per) == c_ref[0]
        g = jnp.where(mine, own_ref[...], oth_ref[...])
        g_ref[...] = g
        d, m2, v2 = _adamw_math(w_ref[...], g, m_ref[...], v_ref[...])
        d_ref[...] = d
        mo_ref[...] = m2
        vo_ref[...] = v2

    full = pl.BlockSpec((tr, cc), lambda i, c_ref: (i, 0))
    half = pl.BlockSpec((tr, cc), lambda i, c_ref: (i % per, 0))
    return pl.pallas_call(
        body, name=name,
        grid_spec=pltpu.PrefetchScalarGridSpec(
            num_scalar_prefetch=1, grid=(r // tr,),
            in_specs=[full, half, half, full, full], out_specs=[full] * 4),
        out_shape=[jax.ShapeDtypeStruct((r, cc), F32)] * 4,
        compiler_params=_params(48, ("parallel",)),
    )(cidx, w, own, other, m, v)


def _decay_exponents():
    ri = lax.broadcasted_iota(jnp.int32, (CH, CH), 0).astype(F32)
    ci = lax.broadcasted_iota(jnp.int32, (CH, CH), 1).astype(F32)
    full = jnp.full((CH, CH), float(CH), F32)
    return [[ri - ci, ri + 1.0, (CH - 1.0) - ri, full], [ci - ri, CH - ri, ri, full]]


def _decay_mats(logit_full):
    def body(l_ref, o_ref):
        ex = _decay_exponents()
        for d in range(2):
            for h in range(NH):
                lv = l_ref[d * NH + h]
                lg = jnp.minimum(lv, 0.0) - jnp.log(1.0 + jnp.exp(-jnp.abs(lv)))
                for kind in range(4):
                    m = jnp.exp(lg * ex[d][kind])
                    if kind == 0:
                        m = jnp.where(ex[d][0] >= 0.0, jnp.exp(lg * jnp.maximum(ex[d][0], 0.0)), 0.0)
                    o_ref[d, kind, h] = m

    return pl.pallas_call(
        body, name="decay_mats",
        out_shape=jax.ShapeDtypeStruct((2, 4, NH, CH, CH), F32),
        in_specs=[pl.BlockSpec(memory_space=pltpu.VMEM)],
        out_specs=pl.BlockSpec(memory_space=pltpu.VMEM),
        compiler_params=_params(32),
    )(logit_full)


def _ctx_kv_weights(wi_ref):
    def cols(g):
        return wi_ref[g // WI_C, :, g % WI_C: g % WI_C + HD].astype(F32)

    wk = [cols(3 * AW + h * HD) for h in range(NH)]
    wv = [cols(4 * AW + h * HD) for h in range(NH)]
    return wk, wv


def _ctx_forward(ctx, vecs, wi, dm):
    def body(ctx_ref, v_ref, wi_ref, dm_ref, scf_ref, scb_ref):
        wk, wv = _ctx_kv_weights(wi_ref)
        mats = [[dm_ref[d, kind, h] for h in range(NH)] for d in range(2) for kind in (2, 3)]
        scf, scb = _ctx_states(ctx_ref[0:CH, :], ctx_ref[CH:2 * CH, :], v_ref[0:1, :], v_ref[1:2, :],
                               v_ref[2:3, :], wk, wv, mats[0], mats[2], mats[1], mats[3])
        for h in range(NH):
            scf_ref[h] = scf[h]
            scb_ref[h] = scb[h]

    return pl.pallas_call(
        body, name="ctx_forward",
        out_shape=[jax.ShapeDtypeStruct((NH, HD, HD), F32)] * 2,
        in_specs=[pl.BlockSpec(memory_space=pltpu.VMEM)] * 4,
        out_specs=[pl.BlockSpec(memory_space=pltpu.VMEM)] * 2,
        compiler_params=_params(48),
    )(ctx, vecs, wi, dm)


def _ctx_backward(ctx, vecs, wi, dm, dscf, dscb):
    def body(ctx_ref, v_ref, wi_ref, dm_ref, gf_ref, gb_ref, gw_ref, gv_ref, gdm_ref):
        wk, wv = _ctx_kv_weights(wi_ref)
        mats = [[dm_ref[d, kind, h] for h in range(NH)] for d in range(2) for kind in (2, 3)]
        ctx0, ctx1 = ctx_ref[0:CH, :], ctx_ref[CH:2 * CH, :]

        def fn(n1, csh, csc, wk_, wv_, zf, zb, ef, eb):
            return _ctx_states(ctx0, ctx1, n1, csh, csc, wk_, wv_, zf, zb, ef, eb)

        _, vjp = jax.vjp(fn, v_ref[0:1, :], v_ref[1:2, :], v_ref[2:3, :], wk, wv,
                         mats[0], mats[2], mats[1], mats[3])
        cot = ([gf_ref[h] for h in range(NH)], [gb_ref[h] for h in range(NH)])
        dn1, dcsh, dcsc, dwk, dwv, dzf, dzb, def_, deb = vjp(cot)
        for h in range(NH):
            gw_ref[:, h * HD:(h + 1) * HD] = dwk[h]
            gw_ref[:, AW + h * HD:AW + (h + 1) * HD] = dwv[h]
        gv_ref[...] = jnp.zeros_like(gv_ref)
        gv_ref[0:1, :] = dn1
        gv_ref[1:2, :] = dcsh
        gv_ref[2:3, :] = dcsc
        for h in range(NH):
            gdm_ref[0, 0, h] = dzf[h]
            gdm_ref[0, 1, h] = def_[h]
            gdm_ref[1, 0, h] = dzb[h]
            gdm_ref[1, 1, h] = deb[h]

    return pl.pallas_call(
        body, name="ctx_backward",
        out_shape=[jax.ShapeDtypeStruct((D, 2 * AW), F32), jax.ShapeDtypeStruct((8, D), F32),
                   jax.ShapeDtypeStruct((2, 2, NH, CH, CH), F32)],
        in_specs=[pl.BlockSpec(memory_space=pltpu.VMEM)] * 6,
        out_specs=[pl.BlockSpec(memory_space=pltpu.VMEM)] * 3,
        compiler_params=_params(56),
    )(ctx, vecs, wi, dm, dscf, dscb)


def _in_proj(x, vecs, wi, gbufs):
    ln = x.shape[0]
    t = min(512, ln)
    nt = len(gbufs)
    steps = ln // t

    def body(x_ref, v_ref, wi_ref, *refs):
        z_ref, hx_ref = refs[nt:nt + 2]
        bufs = refs[nt + 2:2 * nt + 2]
        send_sems, recv_sems = refs[2 * nt + 2:]
        i = pl.program_id(0)

        @pl.when(i == 0)
        def _():
            for cp in _gather_ici_copies(bufs, send_sems, recv_sems)[0]:
                cp.start()

        xv = x_ref[...]
        hx = (xv * _rms(xv) * v_ref[0:1, :]) * (1.0 + v_ref[2:3, :]) + v_ref[1:2, :]
        hb = hx.astype(BF)
        hx_ref[...] = hb
        for j in range(NCHIP):
            z_ref[:, j * WI_C:(j + 1) * WI_C] = _dot(hb, wi_ref[j], NN)

        @pl.when(i == steps - 1)
        def _():
            out_cp, in_cp = _gather_ici_copies(bufs, send_sems, recv_sems)
            for cp in in_cp:
                cp.wait_recv()
            for cp in out_cp:
                cp.wait_send()

    hbm = pl.BlockSpec(memory_space=pl.ANY)
    out = pl.pallas_call(
        body, name="in_proj", grid=(steps,),
        in_specs=[pl.BlockSpec((t, D), lambda i: (i, 0)), _const((8, D)), _const((NCHIP, D, WI_C))] + [hbm] * nt,
        out_specs=[pl.BlockSpec((t, IN_COLS), lambda i: (i, 0)), pl.BlockSpec((t, D), lambda i: (i, 0))] + [hbm] * nt,
        out_shape=[jax.ShapeDtypeStruct((ln, IN_COLS), F32), jax.ShapeDtypeStruct((ln, D), BF)]
        + [jax.ShapeDtypeStruct(g.shape, g.dtype) for g in gbufs],
        input_output_aliases={3 + k: 2 + k for k in range(nt)},
        scratch_shapes=[pltpu.SemaphoreType.DMA(((NCHIP - 1) * nt,))] * 2,
        compiler_params=_params(56, ("arbitrary",)),
    )(x, vecs, wi, *gbufs)
    return out[0], out[1], out[2:]


def _allgather_copies(src, out, send_sems, recv_sems, local_sem):
    x, y, c = _pos()
    me = 4 * x + 2 * y + c
    sends, recvs = [], []
    for k in range(1, NDEV):
        kx, ky, kc = (k >> 2) & 1, (k >> 1) & 1, k & 1
        peer = (x ^ kx, y ^ ky, c ^ kc)
        frm = 4 * (x ^ kx) + 2 * (y ^ ky) + (c ^ kc)
        sends.append(pltpu.make_async_remote_copy(src_ref=src, dst_ref=out.at[me], send_sem=send_sems.at[k - 1],
                                                  recv_sem=recv_sems.at[k - 1], device_id=peer, device_id_type=MESH))
        recvs.append(pltpu.make_async_remote_copy(src_ref=src, dst_ref=out.at[frm], send_sem=send_sems.at[k - 1],
                                                  recv_sem=recv_sems.at[k - 1], device_id=peer, device_id_type=MESH))
    return sends, recvs, pltpu.make_async_copy(src, out.at[me], local_sem)


def _in_proj_bwd(dz, x, dx1, vecs, wi, parts, early):
    ln = x.shape[0]
    t = min(512, ln)
    nt = len(parts)
    steps = ln // t

    def body(dz_ref, x_ref, dx1_ref, v_ref, wi_ref, *refs):
        ps = refs[:nt]
        early_ref = refs[nt]
        gx_ref, acc_ref = refs[nt + 1:nt + 3]
        got = refs[nt + 3:2 * nt + 3]
        early_all = refs[2 * nt + 3]
        send_sems, recv_sems, ag_send, ag_recv, ag_local = refs[2 * nt + 4:]

        @pl.when(pl.program_id(0) == 0)
        def _():
            acc_ref[...] = jnp.zeros_like(acc_ref)
            for cp in _scatter_ici_copies(ps, got, send_sems, recv_sems):
                cp.start()
            sends, _, own = _allgather_copies(early_ref, early_all, ag_send, ag_recv, ag_local)
            own.start()
            for cp in sends:
                cp.start()

        dhx = jnp.zeros((t, D), F32)
        for j in range(NCHIP):
            dhx = dhx + _dot(dz_ref[:, j * WI_C:(j + 1) * WI_C], wi_ref[j], NT)
        xv = x_ref[...]
        r = _rms(xv)
        xn = xv * r
        n1, sc = v_ref[0:1, :], v_ref[2:3, :]
        acc_ref[0:1, :] += jnp.sum(dhx * xn * (1.0 + sc), axis=0, keepdims=True)
        acc_ref[1:2, :] += jnp.sum(dhx, axis=0, keepdims=True)
        acc_ref[2:3, :] += jnp.sum(dhx * xn * n1, axis=0, keepdims=True)
        g = dhx * n1 * (1.0 + sc)
        gx_ref[...] = dx1_ref[...] + r * (g - xn * jnp.mean(g * xn, axis=-1, keepdims=True))

        @pl.when(pl.program_id(0) == steps - 1)
        def _():
            cps = _scatter_ici_copies(ps, got, send_sems, recv_sems)
            sends, recvs, own = _allgather_copies(early_ref, early_all, ag_send, ag_recv, ag_local)
            for cp in cps + recvs:
                cp.wait_recv()
            for cp in cps + sends:
                cp.wait_send()
            own.wait()

    hbm = pl.BlockSpec(memory_space=pl.ANY)
    out = pl.pallas_call(
        body, name="in_proj_bwd", grid=(steps,),
        in_specs=[pl.BlockSpec((t, IN_COLS), lambda i: (i, 0)), pl.BlockSpec((t, D), lambda i: (i, 0)),
                  pl.BlockSpec((t, D), lambda i: (i, 0)), _const((8, D)), _const((NCHIP, D, WI_C))]
        + [hbm] * (nt + 1),
        out_specs=[pl.BlockSpec((t, D), lambda i: (i, 0)), pl.BlockSpec((8, D), lambda i: (0, 0))]
        + [hbm] * (nt + 1),
        out_shape=[jax.ShapeDtypeStruct((ln, D), F32), jax.ShapeDtypeStruct((8, D), F32)]
        + [jax.ShapeDtypeStruct((NCHIP - 1,) + p.shape[1:], BF) for p in parts]
        + [jax.ShapeDtypeStruct((NDEV,) + early.shape, F32)],
        scratch_shapes=[pltpu.SemaphoreType.DMA(((NCHIP - 1) * nt,))] * 2
        + [pltpu.SemaphoreType.DMA((NDEV - 1,))] * 2 + [pltpu.SemaphoreType.DMA],
        compiler_params=_params(56, ("arbitrary",)),
    )(dz, x, dx1, vecs, wi, *parts, early)
    return out[0], out[1], out[2:2 + nt], out[2 + nt]


def _post_mixer(x, ycat, tgt, vecs, wo, wg, wu, wd):
    ln = x.shape[0]
    t = min(256, ln)

    def body(x_ref, y_ref, t_ref, v_ref, wo_ref, wg_ref, wu_ref, wd_ref,
             dx1_ref, dyc_ref, h2_ref, dy_ref, df_ref, act_ref, da_ref, db_ref, acc_ref, a_st, b_st):
        @pl.when(pl.program_id(0) == 0)
        def _():
            acc_ref[...] = jnp.zeros_like(acc_ref)

        g1, n2, sh2, sc2 = v_ref[0:1, :], v_ref[1:2, :], v_ref[2:3, :], v_ref[3:4, :]
        g2, nf = v_ref[4:5, :], v_ref[5:6, :]
        y = _dot(y_ref[...], wo_ref[...], NN)
        x1 = x_ref[...] + g1 * y
        r2 = _rms(x1)
        xn2 = x1 * r2
        t2 = xn2 * n2
        h2b = (t2 * (1.0 + sc2) + sh2).astype(BF)
        h2_ref[...] = h2b
        a = _dot(h2b, wg_ref[...], NT)
        b = _dot(h2b, wu_ref[...], NT)
        a_st[...] = a
        b_st[...] = b
        act = (_silu(a) * b).astype(BF)
        act_ref[...] = act
        f = _dot(act, wd_ref[...], NN)
        x2 = x1 + g2 * f
        r3 = _rms(x2)
        xn3 = x2 * r3
        e = xn3 * nf - t_ref[...]
        acc_ref[6:7, :] += jnp.sum(e * e, axis=0, keepdims=True) * (0.5 / D)
        dout = e * (1.0 / D)
        acc_ref[5:6, :] += jnp.sum(dout * xn3, axis=0, keepdims=True)
        gg = dout * nf
        dx2 = r3 * (gg - xn3 * jnp.mean(gg * xn3, axis=-1, keepdims=True))
        acc_ref[4:5, :] += jnp.sum(dx2 * f, axis=0, keepdims=True)
        dfb = (g2 * dx2).astype(BF)
        df_ref[...] = dfb
        dact = _dot(dfb, wd_ref[...], NT)
        a = a_st[...]
        b = b_st[...]
        s = jax.nn.sigmoid(a)
        da = (dact * b * (s * (1.0 + a * (1.0 - s)))).astype(BF)
        db = (dact * (a * s)).astype(BF)
        da_ref[...] = da
        db_ref[...] = db
        dh2 = _dot(da, wg_ref[...], NN) + _dot(db, wu_ref[...], NN)
        acc_ref[2:3, :] += jnp.sum(dh2, axis=0, keepdims=True)
        acc_ref[3:4, :] += jnp.sum(dh2 * t2, axis=0, keepdims=True)
        acc_ref[1:2, :] += jnp.sum(dh2 * xn2 * (1.0 + sc2), axis=0, keepdims=True)
        gx = dh2 * n2 * (1.0 + sc2)
        dx1 = dx2 + r2 * (gx - xn2 * jnp.mean(gx * xn2, axis=-1, keepdims=True))
        dx1_ref[...] = dx1
        acc_ref[0:1, :] += jnp.sum(dx1 * y, axis=0, keepdims=True)
        dyb = (g1 * dx1).astype(BF)
        dy_ref[...] = dyb
        dyc_ref[...] = _dot(dyb, wo_ref[...], NT)

    tok = pl.BlockSpec((t, D), lambda i: (i, 0))
    ffb = pl.BlockSpec((t, DFF), lambda i: (i, 0))
    return pl.pallas_call(
        body, name="post_mixer", grid=(ln // t,),
        in_specs=[tok, tok, tok, _const((8, D)), _const((D, D)), _const((DFF, D)), _const((DFF, D)),
                  _const((DFF, D))],
        out_specs=[tok, tok, tok, tok, tok, ffb, ffb, ffb, pl.BlockSpec((16, D), lambda i: (0, 0))],
        out_shape=[jax.ShapeDtypeStruct((ln, D), F32)] * 2 + [jax.ShapeDtypeStruct((ln, D), BF)] * 3
        + [jax.ShapeDtypeStruct((ln, DFF), BF)] * 3 + [jax.ShapeDtypeStruct((16, D), F32)],
        scratch_shapes=[pltpu.VMEM((t, DFF), F32)] * 2,
        compiler_params=_params(60, ("arbitrary",)),
    )(x, ycat, tgt, vecs, wo, wg, wu, wd)


def _exchange_copies(g, out, send_sems, recv_sems):
    x, y, c = _pos()
    return [pltpu.make_async_remote_copy(src_ref=g.at[j, 1 - c], dst_ref=out.at[j], send_sem=send_sems.at[j],
                                         recv_sem=recv_sems.at[j], device_id=(x, y, 1 - c), device_id_type=MESH)
            for j in range(NCHIP)]


def _tn_matmul(xa, dy, name, nb, k1, n, x_batched, dy_mode, tt, ctx_kv=None, carry=None):
    ln = xa.shape[-2]
    tt = min(tt, ln)
    steps = ln // tt
    n_in = 2 + (ctx_kv is not None) + (carry is not None)

    def body(x_ref, dy_ref, *refs):
        o_ref = refs[n_in - 2]
        if carry is not None:
            g_ref, got_ref = refs[n_in - 3], refs[n_in - 1]
            send_sems, recv_sems = refs[n_in:]

        @pl.when(pl.program_id(0) == 0)
        def _():
            if carry is not None:
                for cp in _exchange_copies(g_ref, got_ref, send_sems, recv_sems):
                    cp.start()
            o_ref[...] = jnp.zeros_like(o_ref)
            if ctx_kv is not None:
                for g in range(0, 2 * AW, HD):
                    col = 3 * AW + g
                    o_ref[col // n, :, col % n: col % n + HD] = refs[0][:, g:g + HD]

        xt = None if x_batched else jnp.transpose(x_ref[...])
        for b in range(nb):
            lhs = jnp.transpose(x_ref[b]) if x_batched else xt
            if dy_mode == "batched":
                rhs = dy_ref[b]
            elif dy_mode == "cols":
                rhs = dy_ref[:, b * n:(b + 1) * n]
            else:
                rhs = dy_ref[...]
            o_ref[b] += _dot(lhs, rhs, NN)

        if carry is not None:
            @pl.when(pl.program_id(0) == steps - 1)
            def _():
                cps = _exchange_copies(g_ref, got_ref, send_sems, recv_sems)
                for cp in cps:
                    cp.wait_recv()
                for cp in cps:
                    cp.wait_send()

    x_spec = (pl.BlockSpec((nb, tt, k1), lambda t: (0, t, 0)) if x_batched
              else pl.BlockSpec((tt, k1), lambda t: (t, 0)))
    if dy_mode == "batched":
        dy_spec = pl.BlockSpec((nb, tt, n), lambda t: (0, t, 0))
    elif dy_mode == "cols":
        dy_spec = pl.BlockSpec((tt, nb * n), lambda t: (t, 0))
    else:
        dy_spec = pl.BlockSpec((tt, n), lambda t: (t, 0))
    hbm = pl.BlockSpec(memory_space=pl.ANY)
    extra = [] if ctx_kv is None else [ctx_kv]
    in_specs = [x_spec, dy_spec] + [_const(e.shape) for e in extra]
    out_specs = [pl.BlockSpec((nb, k1, n), lambda t: (0, 0, 0))]
    out_shape = [jax.ShapeDtypeStruct((nb, k1, n), F32)]
    scratch = []
    if carry is not None:
        extra = extra + [carry]
        in_specs.append(hbm)
        out_specs.append(hbm)
        out_shape.append(jax.ShapeDtypeStruct((NCHIP,) + carry.shape[2:], F32))
        scratch = [pltpu.SemaphoreType.DMA((NCHIP,))] * 2
    out = pl.pallas_call(
        body, name=name, grid=(steps,),
        in_specs=in_specs, out_specs=out_specs, out_shape=out_shape, scratch_shapes=scratch,
        compiler_params=_params(60, ("arbitrary",)),
    )(xa, dy, *extra)
    return out[0] if carry is None else (out[0], out[1])


FWD_CHUNKS_PER_STEP = 4
BWD_CHUNKS_PER_STEP = 4


def _chunks_per_step(nc, want):
    return want if nc % want == 0 else 1


def _mixer_fwd(z, cos_t, sin_t, dm, sgw, gain, bfull, scf, scb, gbufs_a, gbufs_b):
    ln = z.shape[0]
    nc = ln // CH
    na = len(gbufs_a)
    gbufs = list(gbufs_a) + list(gbufs_b)
    nt = len(gbufs)
    cps = _chunks_per_step(nc, FWD_CHUNKS_PER_STEP)
    nb = nc // cps
    rows = cps * CH
    mid = nb // 2

    def rev(p, n):
        return p * n + (1 - p) * (nb - 1 - n)

    def col(j, both):
        if both:
            return pl.BlockSpec((rows, AW), lambda p, n: (rev(p, n), j))
        return pl.BlockSpec((rows, AW), lambda p, n: (p * n, j))

    def body(u_ref, v_ref, q_ref, k_ref, vr_ref, gf_ref, gb_ref, cos_ref, sin_ref, dm_ref, sgw_ref, gain_ref,
             bfull_ref, scf_ref, scb_ref, *refs):
        y_ref, sf_ref, sb_ref, of_ref = refs[nt:nt + 4]
        bufs = refs[nt + 4:2 * nt + 4]
        bufs_a, bufs_b = bufs[:na], bufs[na:]
        sb_all, st, a_send, a_recv, bi_send, bi_recv, bd_send, bd_recv = refs[2 * nt + 4:]
        p, n = pl.program_id(0), pl.program_id(1)

        @pl.when((p == 0) & (n == 0))
        def _():
            for cp in _gather_d2d_copies(bufs_a, a_send, a_recv)[0]:
                cp.start()
            for cp in _gather_ici_copies(bufs_b, bi_send, bi_recv)[0]:
                cp.start()

        @pl.when((p == 1) & (n == mid))
        def _():
            for cp in _gather_ici_copies(bufs_b, bi_send, bi_recv)[1]:
                cp.wait_recv()
            for cp in _gather_d2d_copies(bufs_b, bd_send, bd_recv)[0]:
                cp.start()

        def roped_k(r0):
            cos, sin = cos_ref[r0:r0 + CH, :], sin_ref[r0:r0 + CH, :]
            return [_rope(t, cos, sin) * K_SCALE for t in _heads(k_ref, r0)]

        @pl.when(p == 0)
        def _():
            @pl.when(n == 0)
            def _():
                st[...] = scb_ref[...]

            for s in reversed(range(cps)):
                m = (nb - 1 - n) * cps + s
                k, vr = roped_k(s * CH), _heads(vr_ref, s * CH)
                for h in range(NH):
                    sb_all[m, h] = st[h]
                    st[h] = dm_ref[1, 3, h] * st[h] + _mm_tn(k[h], dm_ref[1, 2, h] * vr[h])

        @pl.when(p == 1)
        def _():
            @pl.when(n == 0)
            def _():
                st[...] = scf_ref[...]

            mats = [[dm_ref[d, kind, h] for h in range(NH)] for d in range(2) for kind in range(3)]
            for s in range(cps):
                r0 = s * CH
                m = n * cps + s
                cos, sin = cos_ref[r0:r0 + CH, :], sin_ref[r0:r0 + CH, :]
                q = [_rope(t, cos, sin) for t in _heads(q_ref, r0)]
                k, vr = roped_k(r0), _heads(vr_ref, r0)
                u, v, gf, gb = _heads(u_ref, r0), _heads(v_ref, r0), _heads(gf_ref, r0), _heads(gb_ref, r0)
                cols = [slice(h * HD, (h + 1) * HD) for h in range(NH)]
                ya = [_gate_group(u[h], v[h], sgw_ref[h], gain_ref[:, cols[h]], bfull_ref[h]) for h in range(NH)]
                sf = [st[h] for h in range(NH)]
                sb = [sb_all[m, h] for h in range(NH)]
                ret = [_ret_head(q[h], k[h], vr[h], gf[h], gb[h], sf[h], sb[h], mats[0][h], mats[1][h], mats[2][h],
                                 mats[3][h], mats[4][h], mats[5][h]) for h in range(NH)]
                for h in range(NH):
                    yr, uf, _, of = ret[h]
                    y_ref[r0:r0 + CH, cols[h]] = ya[h].astype(BF)
                    y_ref[r0:r0 + CH, AW + h * HD:AW + (h + 1) * HD] = yr.astype(BF)
                    of_ref[r0:r0 + CH, cols[h]] = of
                    sf_ref[s, h] = sf[h]
                    sb_ref[s, h] = sb[h]
                    st[h] = dm_ref[0, 3, h] * sf[h] + uf

        @pl.when((p == 1) & (n == nb - 1))
        def _():
            a_out, a_in = _gather_d2d_copies(bufs_a, a_send, a_recv)
            b_out, b_in = _gather_d2d_copies(bufs_b, bd_send, bd_recv)
            for cp in a_in + b_in:
                cp.wait_recv()
            for cp in a_out + b_out + _gather_ici_copies(bufs_b, bi_send, bi_recv)[0]:
                cp.wait_send()

    hbm = pl.BlockSpec(memory_space=pl.ANY)
    tab = pl.BlockSpec((rows, HD), lambda p, n: (rev(p, n), 0))
    st_spec = pl.BlockSpec((cps, NH, HD, HD), lambda p, n: (p * n, 0, 0, 0))
    out = pl.pallas_call(
        body, name="mixer_fwd", grid=(2, nb),
        in_specs=[col(0, False), col(1, False), col(2, False), col(3, True), col(4, True), col(5, False),
                  col(6, False), tab, tab, _const((2, 4, NH, CH, CH)), _const((NH, CH, CH)), _const((1, AW)),
                  _const((NH, CH, CH)), _const((NH, HD, HD)), _const((NH, HD, HD))] + [hbm] * nt,
        out_specs=[pl.BlockSpec((rows, D), lambda p, n: (p * n, 0)), st_spec, st_spec,
                   pl.BlockSpec((rows, AW), lambda p, n: (p * n, 0))] + [hbm] * nt,
        out_shape=[jax.ShapeDtypeStruct((ln, D), BF), jax.ShapeDtypeStruct((nc, NH, HD, HD), F32),
                   jax.ShapeDtypeStruct((nc, NH, HD, HD), F32), jax.ShapeDtypeStruct((ln, AW), F32)]
        + [jax.ShapeDtypeStruct(g.shape, g.dtype) for g in gbufs],
        input_output_aliases={15 + k: 4 + k for k in range(nt)},
        scratch_shapes=[pltpu.VMEM((nc, NH, HD, HD), F32), pltpu.VMEM((NH, HD, HD), F32)]
        + [pltpu.SemaphoreType.DMA(((NCHIP - 1) * na,))] * 2
        + [pltpu.SemaphoreType.DMA(((NCHIP - 1) * (nt - na),))] * 4,
        compiler_params=_params(56, ("arbitrary", "arbitrary")),
    )(z, z, z, z, z, z, z, cos_t, sin_t, dm, sgw, gain, bfull, scf, scb, *gbufs)
    return out[0], out[1], out[2], out[3], out[4:]


def _mixer_bwd(z, dycat, of_all, cos_t, sin_t, dm, sgw, gain, bfull, sf_all, sb_all, parts):
    ln = z.shape[0]
    nc = ln // CH
    cps = _chunks_per_step(nc, BWD_CHUNKS_PER_STEP)
    nb = nc // cps
    rows = cps * CH

    def rev(p, n):
        return p * n + (1 - p) * (nb - 1 - n)

    def col(j, both):
        if both:
            return pl.BlockSpec((rows, AW), lambda p, n: (rev(p, n), j))
        return pl.BlockSpec((rows, AW), lambda p, n: (p * n, j))

    nt = len(parts)

    def body(u_ref, v_ref, q_ref, k_ref, vr_ref, gf_ref, gb_ref, dya_ref, dyr_ref, of_ref, cos_ref, sin_ref,
             dm_ref, sgw_ref, gain_ref, bfull_ref, sf_ref, sb_ref, *refs):
        ps = refs[:nt]
        dz_ref, ddm_ref, dsgw_ref, dgain_ref, dbf_ref, dscf_ref, dscb_ref = refs[nt:nt + 7]
        got = refs[nt + 7:2 * nt + 7]
        gf_all, run, send_sems, recv_sems = refs[2 * nt + 7:]
        p, n = pl.program_id(0), pl.program_id(1)

        @pl.when((p == 0) & (n == 0))
        def _():
            for cp in _scatter_ici_copies(ps, got, send_sems, recv_sems):
                cp.start()

        mats = [[dm_ref[d, kind, h] for h in range(NH)] for d in range(2) for kind in range(3)]

        @pl.when(p == 0)
        def _():
            @pl.when(n == 0)
            def _():
                run[...] = jnp.zeros_like(run)
                ddm_ref[...] = jnp.zeros_like(ddm_ref)
                dsgw_ref[...] = jnp.zeros_like(dsgw_ref)
                dgain_ref[...] = jnp.zeros_like(dgain_ref)
                dbf_ref[...] = jnp.zeros_like(dbf_ref)

            for s in reversed(range(cps)):
                r0 = s * CH
                m = (nb - 1 - n) * cps + s
                cos, sin = cos_ref[r0:r0 + CH, :], sin_ref[r0:r0 + CH, :]
                q = [_rope(t, cos, sin) for t in _heads(q_ref, r0)]
                gf, dyr, of = _heads(gf_ref, r0), _heads(dyr_ref, r0), _heads(of_ref, r0)
                for h in range(NH):
                    _, vjp = jax.vjp(functools.partial(_gated_norm, gf[h]), of[h])
                    (dof,) = vjp(dyr[h])
                    dsf = _mm_tn(q[h], mats[1][h] * dof)
                    g_next = run[h]
                    gf_all[m, h] = g_next.astype(BF)
                    ddm_ref[0, 3, h] += sf_ref[s, h] * g_next
                    run[h] = dsf + dm_ref[0, 3, h] * g_next

            @pl.when(n == nb - 1)
            def _():
                dscf_ref[...] = run[...]

        @pl.when(p == 1)
        def _():
            @pl.when(n == 0)
            def _():
                run[...] = jnp.zeros_like(run)

            for s in range(cps):
                r0 = s * CH
                m = n * cps + s
                rw = slice(r0, r0 + CH)
                cos, sin = cos_ref[r0:r0 + CH, :], sin_ref[r0:r0 + CH, :]
                q = [_rope(t, cos, sin) for t in _heads(q_ref, r0)]
                k = [_rope(t, cos, sin) * K_SCALE for t in _heads(k_ref, r0)]
                vr, gf, gb, dyr = _heads(vr_ref, r0), _heads(gf_ref, r0), _heads(gb_ref, r0), _heads(dyr_ref, r0)
                u, v, dya = _heads(u_ref, r0), _heads(v_ref, r0), _heads(dya_ref, r0)
                sf = [sf_ref[s, h] for h in range(NH)]
                sb = [sb_ref[s, h] for h in range(NH)]
                g_f = [gf_all[m, h].astype(F32) for h in range(NH)]
                g_b = [run[h] for h in range(NH)]
                cols = [slice(h * HD, (h + 1) * HD) for h in range(NH)]

                def chunk(u_, v_, sgw_, gain_, bfull_, q_, k_, vr_, gf_, gb_, sb_, df, xf, zf, db, xb, zb, sf=sf):
                    ya = [_gate_group(u_[h], v_[h], sgw_[h], gain_[h], bfull_[h]) for h in range(NH)]
                    ret = [_ret_head(q_[h], k_[h], vr_[h], gf_[h], gb_[h], sf[h], sb_[h], df[h], xf[h], zf[h],
                                     db[h], xb[h], zb[h])[:3] for h in range(NH)]
                    return ya, ret

                _, vjp = jax.vjp(chunk, u, v, [sgw_ref[h] for h in range(NH)], [gain_ref[:, c] for c in cols],
                                 [bfull_ref[h] for h in range(NH)], q, k, vr, gf, gb, sb, *mats)
                (du, dv, dsgw, dgain, dbf, dq, dk, dvr, dgf, dgb, dsb, ddf, dxf, dzf, ddb, dxb, dzb) = vjp(
                    (dya, [(dyr[h], g_f[h], g_b[h]) for h in range(NH)]))
                for h in range(NH):
                    dz_ref[rw, h * HD:(h + 1) * HD] = du[h].astype(BF)
                    dz_ref[rw, AW + h * HD:AW + (h + 1) * HD] = dv[h].astype(BF)
                    dz_ref[rw, 2 * AW + h * HD:2 * AW + (h + 1) * HD] = _rope_bwd(dq[h], cos, sin).astype(BF)
                    dz_ref[rw, 3 * AW + h * HD:3 * AW + (h + 1) * HD] = _rope_bwd(dk[h] * K_SCALE, cos,
                                                                                  sin).astype(BF)
                    dz_ref[rw, 4 * AW + h * HD:4 * AW + (h + 1) * HD] = dvr[h].astype(BF)
                    dz_ref[rw, 5 * AW + h * HD:5 * AW + (h + 1) * HD] = dgf[h].astype(BF)
                    dz_ref[rw, 6 * AW + h * HD:6 * AW + (h + 1) * HD] = dgb[h].astype(BF)
                    ddm_ref[0, 0, h] += ddf[h]
                    ddm_ref[0, 1, h] += dxf[h]
                    ddm_ref[0, 2, h] += dzf[h]
                    ddm_ref[1, 0, h] += ddb[h]
                    ddm_ref[1, 1, h] += dxb[h]
                    ddm_ref[1, 2, h] += dzb[h]
                    ddm_ref[1, 3, h] += sb[h] * g_b[h]
                    dsgw_ref[h] += dsgw[h]
                    dgain_ref[:, cols[h]] += dgain[h]
                    dbf_ref[h] += dbf[h]
                    run[h] = dsb[h] + dm_ref[1, 3, h] * g_b[h]

            @pl.when(n == nb - 1)
            def _():
                dscb_ref[...] = run[...]

        @pl.when((p == 1) & (n == nb - 1))
        def _():
            cps_ = _scatter_ici_copies(ps, got, send_sems, recv_sems)
            for cp in cps_:
                cp.wait_recv()
            for cp in cps_:
                cp.wait_send()

    hbm = pl.BlockSpec(memory_space=pl.ANY)
    tab = pl.BlockSpec((rows, HD), lambda p, n: (rev(p, n), 0))
    tile4 = jax.ShapeDtypeStruct((NH, CH, CH), F32)
    out = pl.pallas_call(
        body, name="mixer_bwd", grid=(2, nb),
        in_specs=[col(0, False), col(1, False), col(2, True), col(3, False), col(4, False), col(5, True),
                  col(6, False),
                  pl.BlockSpec((rows, AW), lambda p, n: (p * n, 0)),
                  pl.BlockSpec((rows, AW), lambda p, n: (rev(p, n), 1)),
                  pl.BlockSpec((rows, AW), lambda p, n: ((1 - p) * (nb - 1 - n), 0)),
                  tab, tab, _const((2, 4, NH, CH, CH)), _const((NH, CH, CH)), _const((1, AW)),
                  _const((NH, CH, CH)),
                  pl.BlockSpec((cps, NH, HD, HD), lambda p, n: (rev(p, n), 0, 0, 0)),
                  pl.BlockSpec((cps, NH, HD, HD), lambda p, n: (p * n, 0, 0, 0))] + [hbm] * nt,
        out_specs=[pl.BlockSpec((rows, IN_COLS), lambda p, n: (p * n, 0)),
                   pl.BlockSpec((2, 4, NH, CH, CH), lambda p, n: (0, 0, 0, 0, 0)),
                   pl.BlockSpec((NH, CH, CH), lambda p, n: (0, 0, 0)),
                   pl.BlockSpec((1, AW), lambda p, n: (0, 0)),
                   pl.BlockSpec((NH, CH, CH), lambda p, n: (0, 0, 0)),
                   pl.BlockSpec((NH, HD, HD), lambda p, n: (0, 0, 0)),
                   pl.BlockSpec((NH, HD, HD), lambda p, n: (0, 0, 0))] + [hbm] * nt,
        out_shape=[jax.ShapeDtypeStruct((ln, IN_COLS), BF), jax.ShapeDtypeStruct((2, 4, NH, CH, CH), F32),
                   tile4, jax.ShapeDtypeStruct((1, AW), F32), tile4, tile4, tile4]
        + [jax.ShapeDtypeStruct((NCHIP - 1,) + p.shape[1:], BF) for p in parts],
        scratch_shapes=[pltpu.VMEM((nc, NH, HD, HD), BF), pltpu.VMEM((NH, HD, HD), F32)]
        + [pltpu.SemaphoreType.DMA(((NCHIP - 1) * nt,))] * 2,
        compiler_params=_params(60, ("arbitrary", "arbitrary")),
    )(z, z, z, z, z, z, z, dycat, dycat, of_all, cos_t, sin_t, dm, sgw, gain, bfull, sf_all, sb_all, *parts)
    return out[:7], out[7:]


def _small_reduce(ddm, ddm_ctx, dm, dbf):
    def body(ddm_ref, dctx_ref, dm_ref, dbf_ref, lg_ref, sgb_ref):
        ex = _decay_exponents()
        ones = jnp.ones((8, CH), F32)
        for d in range(2):
            for h in range(NH):
                tot = jnp.zeros((CH, CH), F32)
                for kind in range(4):
                    g = ddm_ref[d, kind, h]
                    if kind >= 2:
                        g = g + dctx_ref[d, kind - 2, h]
                    tot = tot + g * dm_ref[d, kind, h] * ex[d][kind]
                lg_ref[d * NH + h: d * NH + h + 1, :] = jnp.sum(tot, axis=0, keepdims=True)
        sgb_ref[...] = jnp.zeros_like(sgb_ref)
        for g in range(NH):
            r = lax.dot_general(ones, dbf_ref[g], (NT, ((), ())), precision=HI, preferred_element_type=F32)
            sgb_ref[g:g + 1, :] = r[0:1, :]

    return pl.pallas_call(
        body, name="small_reduce",
        out_shape=[jax.ShapeDtypeStruct((8, CH), F32), jax.ShapeDtypeStruct((8, CH), F32)],
        in_specs=[pl.BlockSpec(memory_space=pltpu.VMEM)] * 4,
        out_specs=[pl.BlockSpec(memory_space=pltpu.VMEM)] * 2,
        compiler_params=_params(32),
    )(ddm, ddm_ctx, dm, dbf)


def _mod_backward(ct_pad_t, cctx_col, dmod_pad, dcmod_cols, w_mod_s):
    def body(ct_ref, cc_ref, dm_ref, dc_ref, w_ref, gw_ref, part_ref):
        dcm = dc_ref[0:1, :]
        for d in range(1, NDEV):
            dcm = dcm + dc_ref[d:d + 1, :]
        gw_ref[...] = (jnp.dot(_silu(ct_ref[...]), dm_ref[...], precision=HI, preferred_element_type=F32)
                       + _silu(cc_ref[...]) * dcm)
        part_ref[...] = lax.dot_general(jnp.broadcast_to(dcm, (8, dcm.shape[1])), w_ref[...], (NT, ((), ())),
                                        precision=HI, preferred_element_type=F32)

    return pl.pallas_call(
        body, name="mod_backward",
        out_shape=[jax.ShapeDtypeStruct(w_mod_s.shape, F32), jax.ShapeDtypeStruct((8, D), F32)],
        in_specs=[pl.BlockSpec(memory_space=pltpu.VMEM)] * 5,
        out_specs=[pl.BlockSpec(memory_space=pltpu.VMEM)] * 2,
        compiler_params=_params(48),
    )(ct_pad_t, cctx_col, dmod_pad, dcmod_cols, w_mod_s)


def _cctx_update(parts, c_ctx, m, v):
    def body(p_ref, c_ref, m_ref, v_ref, g_ref, d_ref, mo_ref, vo_ref):
        tot = ((p_ref[0] + p_ref[2]) + p_ref[4]) + p_ref[6]
        cv = c_ref[...]
        s = jax.nn.sigmoid(cv)
        g = tot * (s * (1.0 + cv * (1.0 - s)))
        g_ref[...] = g
        d_ref[...], mo_ref[...], vo_ref[...] = _adamw_math(cv, g, m_ref[...], v_ref[...])

    return pl.pallas_call(
        body, name="cctx_update",
        out_shape=[jax.ShapeDtypeStruct((1, D), F32)] * 4,
        in_specs=[pl.BlockSpec(memory_space=pltpu.VMEM)] * 4,
        out_specs=[pl.BlockSpec(memory_space=pltpu.VMEM)] * 4,
        compiler_params=_params(16),
    )(parts, c_ctx, m, v)


def _small_update(gathered, wp, mp, vp):
    def body(g_ref, w_ref, m_ref, v_ref, go_ref, d_ref, mo_ref, vo_ref, loss_ref):
        tot = g_ref[0]
        for d in range(1, NDEV):
            tot = tot + g_ref[d]
        go_ref[Q_BMOD:Q_N1, :] = tot[P_DMOD:P_N1, :] + tot[P_DCMOD:P_DMOD, :]
        go_ref[Q_N1:Q_LG, :] = tot[P_N1:P_LG, :]
        lg = jnp.sum(tot[P_LG:P_N2, :], axis=1, keepdims=True)
        go_ref[Q_LG:Q_N2, :] = lg * jax.nn.sigmoid(-w_ref[Q_LG:Q_N2, :])
        go_ref[Q_N2:Q_ROWS, :] = tot[P_N2:P_LOSS, :]
        d_ref[...], mo_ref[...], vo_ref[...] = _adamw_math(w_ref[...], go_ref[...], m_ref[...], v_ref[...])
        ls = jnp.sum(jnp.sum(tot[P_LOSS:P_ROWS, :], axis=1, keepdims=True), axis=0, keepdims=True)
        loss_ref[...] = jnp.broadcast_to(ls, (8, CH))

    return pl.pallas_call(
        body, name="small_update",
        out_shape=[jax.ShapeDtypeStruct((Q_ROWS, CH), F32)] * 4 + [jax.ShapeDtypeStruct((8, CH), F32)],
        in_specs=[pl.BlockSpec(memory_space=pltpu.VMEM)] * 4,
        out_specs=[pl.BlockSpec(memory_space=pltpu.VMEM)] * 5,
        compiler_params=_params(32),
    )(gathered, wp, mp, vp)


def _rows(a):
    r = a.reshape(-1, CH)
    return jnp.pad(r, ((0, -r.shape[0] % 8), (0, 0)))


def _pack_small(b_mod, norm1, sg_gain, sg_w, sg_b, lf, lb, norm2, norm_f):
    lg = jnp.broadcast_to(jnp.concatenate([lf.reshape(NH), lb.reshape(NH)])[:, None], (2 * NH, CH))
    return jnp.concatenate([_rows(b_mod), _rows(norm1), _rows(sg_gain), _rows(sg_w), _rows(sg_b), lg,
                            _rows(norm2), _rows(norm_f)], axis=0)


def _unpack_small(p):
    return (p[Q_BMOD:Q_N1].reshape(1, 6 * D), p[Q_N1:Q_GAIN].reshape(1, D), p[Q_GAIN:Q_GAIN + NH].reshape(1, AW),
            p[Q_SGW:Q_SGB].reshape(1, NH, CH, CH), p[Q_SGB:Q_SGB + NH].reshape(1, NH, CH),
            p[Q_LG:Q_LG + NH, 0].reshape(1, NH), p[Q_LG + NH:Q_N2, 0].reshape(1, NH),
            p[Q_N2:Q_NF].reshape(1, D), p[Q_NF:Q_ROWS].reshape(D))


def _rope_tables(ln):
    pos = np.arange(ln)
    rows = (pos // GRID_W).astype(np.float32)
    cols = (pos % GRID_W).astype(np.float32)
    n_freq = HD // 4
    inv = (np.float32(ROPE_BASE) ** (-np.arange(n_freq, dtype=np.float32) / np.float32(n_freq))).astype(np.float32)
    ar = rows[:, None] * inv[None, :]
    ac = cols[:, None] * inv[None, :]
    cos_t = np.concatenate([np.cos(ar), np.cos(ar), np.cos(ac), np.cos(ac)], axis=1).astype(np.float32)
    sin_t = np.concatenate([-np.sin(ar), np.sin(ar), -np.sin(ac), np.sin(ac)], axis=1).astype(np.float32)
    return jnp.asarray(cos_t), jnp.asarray(sin_t)


def kernel(x, c, ctx, c_ctx, w_mod, b_mod, norm1, w_in, sg_gain, sg_w, sg_b, ret_logit_f, ret_logit_b, w_out, norm2, w_gate, w_up, w_down, norm_f, loss_target, m_c_ctx, m_w_mod, m_b_mod, m_norm1, m_w_in, m_sg_gain, m_sg_w, m_sg_b, m_ret_logit_f, m_ret_logit_b, m_w_out, m_norm2, m_w_gate, m_w_up, m_w_down, m_norm_f, v_c_ctx, v_w_mod, v_b_mod, v_norm1, v_w_in, v_sg_gain, v_sg_w, v_sg_b, v_ret_logit_f, v_ret_logit_b, v_w_out, v_norm2, v_w_gate, v_w_up, v_w_down, v_norm_f):
    ln = x.shape[1]
    xi, yi, ci = _pos()
    chip = 2 * xi + yi
    me = 4 * xi + 2 * yi + ci
    x2d = x.reshape(ln, D)
    tgt = loss_target.reshape(ln, D)
    mod_c = w_mod.shape[2]

    tr = lambda a: jnp.swapaxes(a[0], 0, 1)
    gbufs, c_all, prod_all = _prologue(c, c_ctx.reshape(1, D), w_mod[0],
                                       [w_in[0], w_out[0], tr(w_gate), tr(w_up), w_down[0]])
    wi = gbufs[0].reshape(NCHIP, D, WI_C)
    gbufs_a, gbufs_b = gbufs[1:3], gbufs[3:5]
    c_all = c_all.reshape(NDEV, D)
    prod_chips = prod_all[0::2]
    mod_rows = jnp.transpose(prod_chips, (1, 0, 2)).reshape(16, NCHIP * mod_c) + b_mod
    mod = lax.dynamic_slice_in_dim(mod_rows, me, 1, axis=0)
    cmod = mod_rows[8:9]
    sh1, sc1, g1, sh2, sc2, g2 = [mod[:, i * D:(i + 1) * D] for i in range(6)]
    csh1, csc1 = cmod[:, 0:D], cmod[:, D:2 * D]
    zrow = jnp.zeros((1, D), F32)
    vec_in = jnp.concatenate([norm1, sh1, sc1] + [zrow] * 5, axis=0)
    vec_ctx = jnp.concatenate([norm1, csh1, csc1] + [zrow] * 5, axis=0)
    vec_post = jnp.concatenate([g1, norm2, sh2, sc2, g2, norm_f.reshape(1, D), zrow, zrow], axis=0)

    logits = jnp.concatenate([ret_logit_f.reshape(NH), ret_logit_b.reshape(NH)])
    dm = _decay_mats(jnp.broadcast_to(logits[:, None, None], (2 * NH, CH, CH)))
    ctx2d = ctx.reshape(ctx.shape[1], D)
    scf, scb = _ctx_forward(ctx2d, vec_ctx, wi, dm)

    cos_t, sin_t = _rope_tables(ln)
    z, hx, gbufs_a = _in_proj(x2d, vec_in, wi, gbufs_a)
    bfull = jnp.broadcast_to(sg_b[0][:, :, None], (NH, CH, CH))
    ycat, sf_all, sb_all, of_all, gbufs = _mixer_fwd(z, cos_t, sin_t, dm, sg_w[0], sg_gain, bfull, scf, scb,
                                             gbufs_a, gbufs_b)
    wo, wg_t, wu_t, wd = [g.reshape(-1, D) for g in gbufs]

    dx1, dycat, h2, dy, df, act, da, db, acc_post = _post_mixer(x2d, ycat, tgt, vec_post, wo, wg_t, wu_t, wd)

    cidx = ci.reshape(1).astype(jnp.int32)
    where = jnp.stack([ci, chip]).astype(jnp.int32)

    def halves_summed(full, names):
        full = [g.reshape(NCHIP, 2, g.shape[1] // 2, g.shape[2]) for g in full]
        from_sib = _rs_exchange_halves(full, "rs_exchange_" + names[0])
        return [_rs_add_halves(g, r, where, "rs_add_halves_" + nm) for g, r, nm in zip(full, from_sib, names)]

    def split(g):
        return g.reshape(NCHIP, 2, g.shape[1] // (2 * NCHIP), g.shape[2])

    g_wd = split(_tn_matmul(act, df, "grad_w_down", 1, DFF, D, False, "shared", 1024))
    g_wu, x_wd = _tn_matmul(db, h2, "grad_w_up", 1, DFF, D, False, "shared", 1024, carry=g_wd)
    g_wu = split(g_wu)
    g_wg, x_wu = _tn_matmul(da, h2, "grad_w_gate", 1, DFF, D, False, "shared", 1024, carry=g_wu)
    g_wg = split(g_wg)
    g_wo, x_wg = _tn_matmul(ycat, dy, "grad_w_out", 1, D, D, False, "shared", 1024, carry=g_wg)
    g_wo = split(g_wo)
    x_wo = _rs_exchange_halves([g_wo], "rs_exchange_w_out")[0]
    names = ["w_in", "w_out", "w_gate", "w_up", "w_down"]
    sums_b = [_rs_add_halves(g, r, where, "rs_add_halves_" + nm)
              for g, r, nm in zip([g_wo, g_wg, g_wu, g_wd], [x_wo, x_wg, x_wu, x_wd], names[1:])]

    (dz, ddm, dsgw, dgain, dbf, dscf, dscb), from_chips_b = _mixer_bwd(
        z, dycat, of_all, cos_t, sin_t, dm, sg_w[0], sg_gain, bfull, sf_all, sb_all, [s[1] for s in sums_b])
    gwkv, acc_ctx, ddm_ctx = _ctx_backward(ctx2d, vec_ctx, wi, dm, dscf, dscb)
    g_wi = _tn_matmul(hx, dz, "grad_w_in", NCHIP, D, WI_C, False, "cols", 1024, ctx_kv=gwkv)
    sums_a = halves_summed([g_wi], names[:1])
    lg_part, dsgb = _small_reduce(ddm, ddm_ctx, dm, dbf)
    dcmod = jnp.concatenate([acc_ctx[1:2], acc_ctx[2:3], jnp.zeros((1, 4 * D), F32)], axis=1)
    dmod_rest = jnp.concatenate([acc_post[0:1], acc_post[2:3], acc_post[3:4], acc_post[4:5]], axis=1)
    early = jnp.concatenate([_rows(dcmod), _rows(dmod_rest), _rows(dgain), _rows(dsgw), dsgb, lg_part,
                             _rows(acc_post[1:2]), _rows(acc_post[5:6]), _rows(acc_post[6:7])], axis=0)
    gx, acc_in, from_chips_a, early_all = _in_proj_bwd(dz, x2d, dx1, vec_in, wi, [s[1] for s in sums_a], early)

    sums = sums_a + sums_b
    from_chips = list(from_chips_a) + list(from_chips_b)
    finals = [_rs_add_chips(s[0], r, "rs_add_chips_" + nm) for s, r, nm in zip(sums, from_chips, names)]
    others = _rs_share_final(finals)

    late = jnp.concatenate([_rows(acc_in[1:2]), _rows(acc_in[2:3]), _rows(acc_in[0:1] + acc_ctx[0:1])], axis=0)
    late_all = _allgather_small(late, "gather_small")
    n_dc, n_l = P_DMOD - P_DCMOD, 16
    gathered = jnp.concatenate([early_all[:, :n_dc], late_all[:, :n_l], early_all[:, n_dc:n_dc + 32],
                                late_all[:, n_l:], early_all[:, n_dc + 32:]], axis=1)
    dmod_all = gathered[:, P_DMOD:P_N1].reshape(NDEV, 6 * D)
    dcmod_all = gathered[:, P_DCMOD:P_DMOD].reshape(NDEV, 6 * D)
    dmod_cols = lax.dynamic_slice_in_dim(dmod_all, chip * mod_c, mod_c, axis=1)
    dcmod_cols = lax.dynamic_slice_in_dim(dcmod_all, chip * mod_c, mod_c, axis=1)
    dmod_pad = jnp.concatenate([dmod_cols, jnp.zeros((CH - NDEV, mod_c), F32)], axis=0)
    ct_pad_t = jnp.concatenate([jnp.transpose(c_all), jnp.zeros((D, CH - NDEV), F32)], axis=1)
    g_wmod, cctx_part = _mod_backward(ct_pad_t, c_ctx.reshape(D, 1), dmod_pad, dcmod_cols, w_mod[0])
    parts = _allgather_small(cctx_part[0:1], "gather_cctx")
    g_cctx, d_cctx, nm_cctx, nv_cctx = _cctx_update(parts, c_ctx.reshape(1, D), m_c_ctx.reshape(1, D),
                                                    v_c_ctx.reshape(1, D))

    wp = _pack_small(b_mod, norm1, sg_gain, sg_w, sg_b, ret_logit_f, ret_logit_b, norm2, norm_f)
    mp = _pack_small(m_b_mod, m_norm1, m_sg_gain, m_sg_w, m_sg_b, m_ret_logit_f, m_ret_logit_b, m_norm2, m_norm_f)
    vp = _pack_small(v_b_mod, v_norm1, v_sg_gain, v_sg_w, v_sg_b, v_ret_logit_f, v_ret_logit_b, v_norm2, v_norm_f)
    gp, dp, mp2, vp2, loss_t = _small_update(gathered, wp, mp, vp)

    big_w = [w_in[0], w_out[0], tr(w_gate), tr(w_up), w_down[0]]
    big_m = [m_w_in[0], m_w_out[0], tr(m_w_gate), tr(m_w_up), m_w_down[0]]
    big_v = [v_w_in[0], v_w_out[0], tr(v_w_gate), tr(v_w_up), v_w_down[0]]
    upd = [_adamw_halves(w, own, oth, m, v, cidx, "adamw_" + nm) for w, own, oth, m, v, nm in
           zip(big_w, finals, others, big_m, big_v, names)]
    big_g = [g_wmod] + [u[0] for u in upd]
    big = [_adamw(w_mod[0], g_wmod, m_w_mod[0], v_w_mod[0], "adamw_w_mod")] + [u[1:] for u in upd]

    def assemble(small, cctx, bigs):
        b_mod_, norm1_, gain_, sgw_, sgb_, lf_, lb_, norm2_, normf_ = _unpack_small(small)
        wm, wi_, wo_, wg_, wu_, wd_ = [b[None] for b in bigs]
        wg_, wu_ = jnp.swapaxes(wg_, 1, 2), jnp.swapaxes(wu_, 1, 2)
        return [cctx.reshape(D), wm, b_mod_, norm1_, wi_, gain_, sgw_, sgb_, lf_, lb_, wo_, norm2_, wg_, wu_, wd_,
                normf_]

    out = [loss_t[0, 0], gx.reshape(1, ln, D)]
    out += assemble(gp, g_cctx, big_g)
    out += assemble(dp, d_cctx, [b[0] for b in big])
    out += assemble(mp2, nm_cctx, [b[1] for b in big])
    out += assemble(vp2, nv_cctx, [b[2] for b in big])
    return tuple(out)
```

```python
import functools

import jax
import jax.numpy as jnp
import numpy as np
from jax import lax
from jax.experimental import pallas as pl
from jax.experimental.pallas import tpu as pltpu

F32 = jnp.float32
BF = jnp.bfloat16
MESH = pl.DeviceIdType.MESH

D = 1024
CH = 128
HD = 128
NH = 4
AW = 512
IN_COLS = 3584
DFF = 2816
NCHIP = 4
NDEV = 8
WI_C = IN_COLS // NCHIP
FF_C = DFF // NCHIP
WO_R = D // NCHIP
EPS = 1e-6
GRID_W = 64
ROPE_BASE = 10000.0
K_SCALE = HD ** -0.5
LR, B1, B2, AEPS, WD, STEP = 0.001, 0.9, 0.999, 1e-08, 0.01, 10
VMEM_MB = 1 << 20
HI = lax.Precision.HIGHEST

P_DCMOD, P_DMOD, P_N1, P_GAIN, P_SGW, P_SGB, P_LG, P_N2, P_NF, P_LOSS = 0, 48, 96, 104, 112, 624, 632, 640, 648, 656
P_ROWS = 664
Q_BMOD, Q_N1, Q_GAIN, Q_SGW, Q_SGB, Q_LG, Q_N2, Q_NF = 0, 48, 56, 64, 576, 584, 592, 600
Q_ROWS = 608


def _params(vmem_mb, sem=None):
    return pltpu.CompilerParams(vmem_limit_bytes=vmem_mb * VMEM_MB, dimension_semantics=sem)


def _const(shape):
    nd = len(shape)
    return pl.BlockSpec(shape, lambda *_: (0,) * nd, pipeline_mode=pl.Buffered(1))


def _pos():
    return lax.axis_index("x"), lax.axis_index("y"), lax.axis_index("c")


def _dot(a, b, dims):
    return lax.dot_general(a, b, (dims, ((), ())), preferred_element_type=F32)


NN = ((1,), (0,))
NT = ((1,), (1,))
TN = ((0,), (0,))


@jax.custom_vjp
def _mm(a, b):
    return _dot(a.astype(BF), b.astype(BF), NN)


def _mm_f(a, b):
    return _mm(a, b), (a.astype(BF), b.astype(BF))


def _mm_b(res, g):
    a, b = res
    gb = g.astype(BF)
    return _dot(gb, b, NT), _dot(a, gb, TN)


_mm.defvjp(_mm_f, _mm_b)


@jax.custom_vjp
def _mm_nt(a, b):
    return _dot(a.astype(BF), b.astype(BF), NT)


def _mm_nt_f(a, b):
    return _mm_nt(a, b), (a.astype(BF), b.astype(BF))


def _mm_nt_b(res, g):
    a, b = res
    gb = g.astype(BF)
    return _dot(gb, b, NN), _dot(gb, a, TN)


_mm_nt.defvjp(_mm_nt_f, _mm_nt_b)


@jax.custom_vjp
def _mm_tn(a, b):
    return _dot(a.astype(BF), b.astype(BF), TN)


def _mm_tn_f(a, b):
    return _mm_tn(a, b), (a.astype(BF), b.astype(BF))


def _mm_tn_b(res, g):
    a, b = res
    gb = g.astype(BF)
    return _dot(b, gb, NT), _dot(a, gb, NN)


_mm_tn.defvjp(_mm_tn_f, _mm_tn_b)


def _gelu(x):
    return x * (0.5 * (1.0 + jnp.tanh(0.7978845608028654 * (x + 0.044715 * (x * x * x)))))


def _silu(x):
    return x * jax.nn.sigmoid(x)


def _rms(x):
    return lax.rsqrt(jnp.mean(x * x, axis=-1, keepdims=True) + EPS)


def _swap32(t):
    lane = lax.broadcasted_iota(jnp.int32, t.shape, 1)
    first = (lane % 64) < 32
    return jnp.where(first, pltpu.roll(t, 96, 1), pltpu.roll(t, 32, 1))


def _rope(t, cos, sin):
    return t * cos + _swap32(t) * sin


def _rope_bwd(d, cos, sin):
    return d * cos + _swap32(d * sin)


def _heads(ref, r0=0):
    return [ref[r0:r0 + CH, h * HD:(h + 1) * HD].astype(F32) for h in range(NH)]


def _gate_group(u, v, sgw, gain, bfull):
    gv = _gelu(v)
    return _gelu(u) * (_mm(sgw, gv * _rms(gv) * gain) + bfull)


def _gated_norm(gate, o):
    return _silu(gate) * (o * _rms(o))


def _ret_head(q, k, vr, gf, gb, sf, sb, df, xf, zf, db, xb, zb):
    a = _mm_nt(q, k)
    of = _mm(a * df, vr) + xf * _mm(q, sf)
    ob = _mm(a * db, vr) + xb * _mm(q, sb)
    return _gated_norm(gf, of) + _gated_norm(gb, ob), _mm_tn(k, zf * vr), _mm_tn(k, zb * vr), of


def _ctx_states(ctx0, ctx1, n1, csh, csc, wk, wv, zf, zb, ef, eb):
    hc0 = (ctx0 * _rms(ctx0) * n1) * (1.0 + csc) + csh
    hc1 = (ctx1 * _rms(ctx1) * n1) * (1.0 + csc) + csh
    scf, scb = [], []
    for h in range(NH):
        k0, k1 = _mm(hc0, wk[h]) * K_SCALE, _mm(hc1, wk[h]) * K_SCALE
        v0, v1 = _mm(hc0, wv[h]), _mm(hc1, wv[h])
        scf.append(ef[h] * _mm_tn(k0, zf[h] * v0) + _mm_tn(k1, zf[h] * v1))
        scb.append(eb[h] * _mm_tn(k1, zb[h] * v1) + _mm_tn(k0, zb[h] * v0))
    return scf, scb


def _allgather_small(v, name):
    r, n = v.shape

    def body(v_ref, out_ref, send_sems, recv_sems, local_sem):
        x, y, c = _pos()
        me = 4 * x + 2 * y + c
        mine = pltpu.make_async_copy(v_ref, out_ref.at[me], local_sem)
        mine.start()
        sent = []
        for k in range(1, NDEV):
            kx, ky, kc = (k >> 2) & 1, (k >> 1) & 1, k & 1
            peer = (x ^ kx, y ^ ky, c ^ kc)
            cp = pltpu.make_async_remote_copy(src_ref=v_ref, dst_ref=out_ref.at[me], send_sem=send_sems.at[k - 1],
                                              recv_sem=recv_sems.at[k - 1], device_id=peer, device_id_type=MESH)
            cp.start()
            sent.append(cp)
        for k in range(1, NDEV):
            kx, ky, kc = (k >> 2) & 1, (k >> 1) & 1, k & 1
            peer = (x ^ kx, y ^ ky, c ^ kc)
            src = 4 * (x ^ kx) + 2 * (y ^ ky) + (c ^ kc)
            pltpu.make_async_remote_copy(src_ref=v_ref, dst_ref=out_ref.at[src], send_sem=send_sems.at[k - 1],
                                         recv_sem=recv_sems.at[k - 1], device_id=peer, device_id_type=MESH).wait_recv()
        for cp in sent:
            cp.wait_send()
        mine.wait()

    return pl.pallas_call(
        body, name=name,
        out_shape=jax.ShapeDtypeStruct((NDEV, r, n), F32),
        in_specs=[pl.BlockSpec(memory_space=pltpu.VMEM)],
        out_specs=pl.BlockSpec(memory_space=pltpu.VMEM),
        scratch_shapes=[pltpu.SemaphoreType.DMA((NDEV - 1,)), pltpu.SemaphoreType.DMA((NDEV - 1,)),
                        pltpu.SemaphoreType.DMA],
        compiler_params=_params(16),
    )(v)


def _chip_offsets():
    return [((k >> 1) & 1, k & 1) for k in range(1, NCHIP)]


def _prologue(c, c_ctx, w_mod_s, shards):
    nt = len(shards)
    shapes = [s.shape for s in shards]
    mod_c = w_mod_s.shape[1]

    def body(*refs):
        c_ref, cc_ref, wm_ref = refs[:3]
        srcs = refs[3:3 + nt]
        outs = refs[3 + nt:3 + 2 * nt]
        call_ref, prod_ref = refs[3 + 2 * nt:5 + 2 * nt]
        stages = refs[5 + 2 * nt:5 + 3 * nt]
        ct = refs[5 + 3 * nt]
        loaded = refs[6 + 3 * nt:6 + 4 * nt]
        (c_send, c_recv, p_send, p_recv, ici_send, ici_recv, d2d_send, d2d_recv, local_sems,
         load_sems) = refs[6 + 4 * nt:]
        x, y, c = _pos()
        chip = 2 * x + y
        me = 4 * x + 2 * y + c
        sib = (x, y, 1 - c)
        loads = [pltpu.make_async_copy(wm_ref, loaded[0], load_sems.at[0])]
        loads += [pltpu.make_async_copy(srcs[t], loaded[t], load_sems.at[t]) for t in range(1, nt)]
        for cp in loads:
            cp.start()
        pending = []

        def stage(t, src):
            half = shapes[t][0] // 2
            stages[t][0] = src[0:half, :].astype(BF)
            stages[t][1] = src[half:2 * half, :].astype(BF)
            cp = pltpu.make_async_copy(stages[t], outs[t].at[chip], local_sems.at[t])
            cp.start()
            pending.append(cp)

        stage(0, srcs[0])
        sends = []
        for k, (kx, ky) in enumerate(_chip_offsets()):
            cp = pltpu.make_async_remote_copy(src_ref=stages[0].at[c], dst_ref=outs[0].at[chip, c],
                                              send_sem=ici_send.at[k], recv_sem=ici_recv.at[k],
                                              device_id=(x ^ kx, y ^ ky, c), device_id_type=MESH)
            cp.start()
            sends.append(cp)

        def to_all(src, dst_of, send_sems, recv_sems):
            for k in range(1, NDEV):
                kx, ky, kc = (k >> 2) & 1, (k >> 1) & 1, k & 1
                cp = pltpu.make_async_remote_copy(src_ref=src, dst_ref=dst_of(me), send_sem=send_sems.at[k - 1],
                                                  recv_sem=recv_sems.at[k - 1], device_id=(x ^ kx, y ^ ky, c ^ kc),
                                                  device_id_type=MESH)
                cp.start()
                sends.append(cp)
            for k in range(1, NDEV):
                kx, ky, kc = (k >> 2) & 1, (k >> 1) & 1, k & 1
                frm = 4 * (x ^ kx) + 2 * (y ^ ky) + (c ^ kc)
                pltpu.make_async_remote_copy(src_ref=src, dst_ref=dst_of(frm), send_sem=send_sems.at[k - 1],
                                             recv_sem=recv_sems.at[k - 1], device_id=(x ^ kx, y ^ ky, c ^ kc),
                                             device_id_type=MESH).wait_recv()

        call_ref[me] = c_ref[...]
        to_all(c_ref, lambda d: call_ref.at[d], c_send, c_recv)
        ct[...] = jnp.zeros_like(ct)
        for d in range(NDEV):
            ct[d:d + 1, :] = call_ref[d]
        ct[NDEV:NDEV + 1, :] = cc_ref[...]
        loads[0].wait()
        prod_ref[me] = jnp.dot(_silu(ct[...]), loaded[0][...], precision=HI, preferred_element_type=F32)
        to_all(prod_ref.at[me], lambda d: prod_ref.at[d], p_send, p_recv)
        for t in range(1, nt):
            loads[t].wait()
            stage(t, loaded[t])

        for k, (kx, ky) in enumerate(_chip_offsets()):
            frm = 2 * (x ^ kx) + (y ^ ky)
            pltpu.make_async_remote_copy(src_ref=stages[0].at[c], dst_ref=outs[0].at[frm, c],
                                         send_sem=ici_send.at[k], recv_sem=ici_recv.at[k],
                                         device_id=(x ^ kx, y ^ ky, c), device_id_type=MESH).wait_recv()
            cp = pltpu.make_async_remote_copy(src_ref=outs[0].at[frm, c], dst_ref=outs[0].at[frm, c],
                                              send_sem=d2d_send.at[k], recv_sem=d2d_recv.at[k],
                                              device_id=sib, device_id_type=MESH)
            cp.start()
            sends.append(cp)
        for k, (kx, ky) in enumerate(_chip_offsets()):
            frm = 2 * (x ^ kx) + (y ^ ky)
            pltpu.make_async_remote_copy(src_ref=stages[0].at[c], dst_ref=outs[0].at[frm, 1 - c],
                                         send_sem=d2d_send.at[k], recv_sem=d2d_recv.at[k],
                                         device_id=sib, device_id_type=MESH).wait_recv()
        for cp in sends:
            cp.wait_send()
        for cp in pending:
            cp.wait()

    vm = pl.BlockSpec(memory_space=pltpu.VMEM)
    hbm = pl.BlockSpec(memory_space=pl.ANY)
    out = pl.pallas_call(
        body, name="prologue",
        out_shape=[jax.ShapeDtypeStruct((NCHIP, 2, r // 2, cc), BF) for r, cc in shapes]
        + [jax.ShapeDtypeStruct((NDEV, 1, D), F32), jax.ShapeDtypeStruct((NDEV, 16, mod_c), F32)],
        in_specs=[vm, vm, hbm, vm] + [hbm] * (nt - 1),
        out_specs=[hbm] * nt + [vm, vm],
        scratch_shapes=[pltpu.VMEM((2, r // 2, cc), BF) for r, cc in shapes] + [pltpu.VMEM((16, D), F32)]
        + [pltpu.VMEM(w_mod_s.shape, F32)] + [pltpu.VMEM(s, F32) for s in shapes[1:]]
        + [pltpu.SemaphoreType.DMA((NDEV - 1,))] * 4 + [pltpu.SemaphoreType.DMA((NCHIP - 1,))] * 4
        + [pltpu.SemaphoreType.DMA((nt,))] * 2,
        compiler_params=_params(56),
    )(c, c_ctx, w_mod_s, *shards)
    return out[:nt], out[nt], out[nt + 1]


def _gather_ici_copies(bufs, send_sems, recv_sems):
    x, y, c = _pos()
    chip = 2 * x + y
    nt = len(bufs)
    out_cp, in_cp = [], []
    for k, (kx, ky) in enumerate(_chip_offsets()):
        frm = 2 * (x ^ kx) + (y ^ ky)
        for t in range(nt):
            s = k * nt + t
            peer = (x ^ kx, y ^ ky, c)
            out_cp.append(pltpu.make_async_remote_copy(
                src_ref=bufs[t].at[chip, c], dst_ref=bufs[t].at[chip, c], send_sem=send_sems.at[s],
                recv_sem=recv_sems.at[s], device_id=peer, device_id_type=MESH))
            in_cp.append(pltpu.make_async_remote_copy(
                src_ref=bufs[t].at[chip, c], dst_ref=bufs[t].at[frm, c], send_sem=send_sems.at[s],
                recv_sem=recv_sems.at[s], device_id=peer, device_id_type=MESH))
    return out_cp, in_cp


def _gather_d2d_copies(bufs, send_sems, recv_sems):
    x, y, c = _pos()
    nt = len(bufs)
    out_cp, in_cp = [], []
    for k, (kx, ky) in enumerate(_chip_offsets()):
        frm = 2 * (x ^ kx) + (y ^ ky)
        for t in range(nt):
            s = k * nt + t
            out_cp.append(pltpu.make_async_remote_copy(
                src_ref=bufs[t].at[frm, c], dst_ref=bufs[t].at[frm, c], send_sem=send_sems.at[s],
                recv_sem=recv_sems.at[s], device_id=(x, y, 1 - c), device_id_type=MESH))
            in_cp.append(pltpu.make_async_remote_copy(
                src_ref=bufs[t].at[frm, c], dst_ref=bufs[t].at[frm, 1 - c], send_sem=send_sems.at[s],
                recv_sem=recv_sems.at[s], device_id=(x, y, 1 - c), device_id_type=MESH))
    return out_cp, in_cp


def _scatter_ici_copies(parts, outs, send_sems, recv_sems):
    x, y, c = _pos()
    nt = len(parts)
    cps = []
    for k, (kx, ky) in enumerate(_chip_offsets()):
        dst_chip = 2 * (x ^ kx) + (y ^ ky)
        for t in range(nt):
            s = k * nt + t
            cps.append(pltpu.make_async_remote_copy(
                src_ref=parts[t].at[dst_chip], dst_ref=outs[t].at[k], send_sem=send_sems.at[s],
                recv_sem=recv_sems.at[s], device_id=(x ^ kx, y ^ ky, c), device_id_type=MESH))
    return cps


def _rs_exchange_halves(grads, name):
    nt = len(grads)
    shapes = [g.shape for g in grads]

    def body(*refs):
        gs, outs = refs[:nt], refs[nt:2 * nt]
        send_sems, recv_sems = refs[2 * nt:]
        x, y, c = _pos()
        sib = (x, y, 1 - c)
        sent = []
        for t in range(nt):
            for j in range(NCHIP):
                s = t * NCHIP + j
                cp = pltpu.make_async_remote_copy(src_ref=gs[t].at[j, 1 - c], dst_ref=outs[t].at[j],
                                                  send_sem=send_sems.at[s], recv_sem=recv_sems.at[s],
                                                  device_id=sib, device_id_type=MESH)
                cp.start()
                sent.append(cp)
        for cp in sent:
            cp.wait_recv()
        for cp in sent:
            cp.wait_send()

    return pl.pallas_call(
        body, name=name,
        out_shape=[jax.ShapeDtypeStruct((NCHIP, s[2], s[3]), F32) for s in shapes],
        in_specs=[pl.BlockSpec(memory_space=pl.ANY)] * nt,
        out_specs=[pl.BlockSpec(memory_space=pl.ANY)] * nt,
        scratch_shapes=[pltpu.SemaphoreType.DMA((nt * NCHIP,))] * 2,
    )(*grads)


def _rs_share_final(finals):
    nt = len(finals)
    shapes = [f.shape for f in finals]

    def body(*refs):
        fs, outs = refs[:nt], refs[nt:2 * nt]
        send_sems, recv_sems = refs[2 * nt:]
        x, y, c = _pos()
        sent = []
        for t in range(nt):
            cp = pltpu.make_async_remote_copy(src_ref=fs[t], dst_ref=outs[t], send_sem=send_sems.at[t],
                                              recv_sem=recv_sems.at[t], device_id=(x, y, 1 - c), device_id_type=MESH)
            cp.start()
            sent.append(cp)
        for cp in sent:
            cp.wait_recv()
        for cp in sent:
            cp.wait_send()

    return pl.pallas_call(
        body, name="rs_share_final",
        out_shape=[jax.ShapeDtypeStruct(s, F32) for s in shapes],
        in_specs=[pl.BlockSpec(memory_space=pl.ANY)] * nt,
        out_specs=[pl.BlockSpec(memory_space=pl.ANY)] * nt,
        scratch_shapes=[pltpu.SemaphoreType.DMA((nt,))] * 2,
    )(*finals)


def _row_tile(h, cc=D):
    for t in (512, 384, 352, 256, 176, 128, 64, 32, 16):
        if h % t == 0 and t * cc * 4 <= (5 * VMEM_MB) // 4:
            return t
    return h


def _rs_add_halves(g, recv, where, name):
    _, _, h, cc = g.shape
    th = _row_tile(h, cc)

    def body(w_ref, g_ref, r_ref, own_ref, ob_ref):
        s = g_ref[...] + r_ref[...]
        ob_ref[...] = s.astype(BF)

        @pl.when(pl.program_id(1) == w_ref[1])
        def _():
            own_ref[...] = s

    return pl.pallas_call(
        body, name=name,
        grid_spec=pltpu.PrefetchScalarGridSpec(
            num_scalar_prefetch=1, grid=(h // th, NCHIP),
            in_specs=[pl.BlockSpec((None, None, th, cc), lambda i, j, w_ref: (j, w_ref[0], i, 0)),
                      pl.BlockSpec((None, th, cc), lambda i, j, w_ref: (j, i, 0))],
            out_specs=[pl.BlockSpec((th, cc), lambda i, j, w_ref: (i, 0)),
                       pl.BlockSpec((None, th, cc), lambda i, j, w_ref: (j, i, 0))]),
        out_shape=[jax.ShapeDtypeStruct((h, cc), F32), jax.ShapeDtypeStruct((NCHIP, h, cc), BF)],
        compiler_params=_params(48, ("arbitrary", "arbitrary")),
    )(where, g, recv)


def _rs_add_chips(own, recv, name):
    h, cc = own.shape
    th = _row_tile(h, cc)

    def body(o_ref, r_ref, out_ref):
        out_ref[...] = ((o_ref[...] + r_ref[0].astype(F32)) + r_ref[1].astype(F32)) + r_ref[2].astype(F32)

    return pl.pallas_call(
        body, name=name, grid=(h // th,),
        in_specs=[pl.BlockSpec((th, cc), lambda i: (i, 0)), pl.BlockSpec((NCHIP - 1, th, cc), lambda i: (0, i, 0))],
        out_specs=pl.BlockSpec((th, cc), lambda i: (i, 0)),
        out_shape=jax.ShapeDtypeStruct((h, cc), F32),
        compiler_params=_params(48, ("parallel",)),
    )(own, recv)


def _adamw_math(w, g, m, v):
    m2 = B1 * m + (1.0 - B1) * g
    v2 = B2 * v + (1.0 - B2) * (g * g)
    m_hat = m2 / (1.0 - B1 ** STEP)
    v_hat = v2 / (1.0 - B2 ** STEP)
    delta = -LR * (m_hat / (jnp.sqrt(v_hat) + AEPS) + WD * w)
    return delta, m2, v2


def _adamw(w, g, m, v, name):
    r, cc = w.shape
    tr = _row_tile(r, cc)

    def body(w_ref, g_ref, m_ref, v_ref, d_ref, mo_ref, vo_ref):
        d, m2, v2 = _adamw_math(w_ref[...], g_ref[...], m_ref[...], v_ref[...])
        d_ref[...] = d
        mo_ref[...] = m2
        vo_ref[...] = v2

    spec = pl.BlockSpec((tr, cc), lambda i: (i, 0))
    return pl.pallas_call(
        body, name=name, grid=(r // tr,), in_specs=[spec] * 4, out_specs=[spec] * 3,
        out_shape=[jax.ShapeDtypeStruct((r, cc), F32)] * 3,
        compiler_params=_params(48, ("parallel",)),
    )(w, g, m, v)


def _adamw_halves(w, own, other, m, v, cidx, name):
    r, cc = w.shape
    h = r // 2
    tr = _row_tile(h, cc)
    per = h // tr

    def body(c_ref, w_ref, own_ref, oth_ref, m_ref, v_ref, g_ref, d_ref, mo_ref, vo_ref):
        mine = (pl.program_id(0) // per) == c_ref[0]
        g = jnp.where(mine, own_ref[...], oth_ref[...])
        g_ref[...] = g
        d, m2, v2 = _adamw_math(w_ref[...], g, m_ref[...], v_ref[...])
        d_ref[...] = d
        mo_ref[...] = m2
        vo_ref[...] = v2

    full = pl.BlockSpec((tr, cc), lambda i, c_ref: (i, 0))
    half = pl.BlockSpec((tr, cc), lambda i, c_ref: (i % per, 0))
    return pl.pallas_call(
        body, name=name,
        grid_spec=pltpu.PrefetchScalarGridSpec(
            num_scalar_prefetch=1, grid=(r // tr,),
            in_specs=[full, half, half, full, full], out_specs=[full] * 4),
        out_shape=[jax.ShapeDtypeStruct((r, cc), F32)] * 4,
        compiler_params=_params(48, ("parallel",)),
    )(cidx, w, own, other, m, v)


def _decay_exponents():
    ri = lax.broadcasted_iota(jnp.int32, (CH, CH), 0).astype(F32)
    ci = lax.broadcasted_iota(jnp.int32, (CH, CH), 1).astype(F32)
    full = jnp.full((CH, CH), float(CH), F32)
    return [[ri - ci, ri + 1.0, (CH - 1.0) - ri, full], [ci - ri, CH - ri, ri, full]]


def _decay_mats(logit_full):
    def body(l_ref, o_ref):
        ex = _decay_exponents()
        for d in range(2):
            for h in range(NH):
                lv = l_ref[d * NH + h]
                lg = jnp.minimum(lv, 0.0) - jnp.log(1.0 + jnp.exp(-jnp.abs(lv)))
                for kind in range(4):
                    m = jnp.exp(lg * ex[d][kind])
                    if kind == 0:
                        m = jnp.where(ex[d][0] >= 0.0, jnp.exp(lg * jnp.maximum(ex[d][0], 0.0)), 0.0)
                    o_ref[d, kind, h] = m

    return pl.pallas_call(
        body, name="decay_mats",
        out_shape=jax.ShapeDtypeStruct((2, 4, NH, CH, CH), F32),
        in_specs=[pl.BlockSpec(memory_space=pltpu.VMEM)],
        out_specs=pl.BlockSpec(memory_space=pltpu.VMEM),
        compiler_params=_params(32),
    )(logit_full)


def _ctx_kv_weights(wi_ref):
    def cols(g):
        return wi_ref[g // WI_C, :, g % WI_C: g % WI_C + HD].astype(F32)

    wk = [cols(3 * AW + h * HD) for h in range(NH)]
    wv = [cols(4 * AW + h * HD) for h in range(NH)]
    return wk, wv


def _ctx_forward(ctx, vecs, wi, dm):
    def body(ctx_ref, v_ref, wi_ref, dm_ref, scf_ref, scb_ref):
        wk, wv = _ctx_kv_weights(wi_ref)
        mats = [[dm_ref[d, kind, h] for h in range(NH)] for d in range(2) for kind in (2, 3)]
        scf, scb = _ctx_states(ctx_ref[0:CH, :], ctx_ref[CH:2 * CH, :], v_ref[0:1, :], v_ref[1:2, :],
                               v_ref[2:3, :], wk, wv, mats[0], mats[2], mats[1], mats[3])
        for h in range(NH):
            scf_ref[h] = scf[h]
            scb_ref[h] = scb[h]

    return pl.pallas_call(
        body, name="ctx_forward",
        out_shape=[jax.ShapeDtypeStruct((NH, HD, HD), F32)] * 2,
        in_specs=[pl.BlockSpec(memory_space=pltpu.VMEM)] * 4,
        out_specs=[pl.BlockSpec(memory_space=pltpu.VMEM)] * 2,
        compiler_params=_params(48),
    )(ctx, vecs, wi, dm)


def _ctx_backward(ctx, vecs, wi, dm, dscf, dscb):
    def body(ctx_ref, v_ref, wi_ref, dm_ref, gf_ref, gb_ref, gw_ref, gv_ref, gdm_ref):
        wk, wv = _ctx_kv_weights(wi_ref)
        mats = [[dm_ref[d, kind, h] for h in range(NH)] for d in range(2) for kind in (2, 3)]
        ctx0, ctx1 = ctx_ref[0:CH, :], ctx_ref[CH:2 * CH, :]

        def fn(n1, csh, csc, wk_, wv_, zf, zb, ef, eb):
            return _ctx_states(ctx0, ctx1, n1, csh, csc, wk_, wv_, zf, zb, ef, eb)

        _, vjp = jax.vjp(fn, v_ref[0:1, :], v_ref[1:2, :], v_ref[2:3, :], wk, wv,
                         mats[0], mats[2], mats[1], mats[3])
        cot = ([gf_ref[h] for h in range(NH)], [gb_ref[h] for h in range(NH)])
        dn1, dcsh, dcsc, dwk, dwv, dzf, dzb, def_, deb = vjp(cot)
        for h in range(NH):
            gw_ref[:, h * HD:(h + 1) * HD] = dwk[h]
            gw_ref[:, AW + h * HD:AW + (h + 1) * HD] = dwv[h]
        gv_ref[...] = jnp.zeros_like(gv_ref)
        gv_ref[0:1, :] = dn1
        gv_ref[1:2, :] = dcsh
        gv_ref[2:3, :] = dcsc
        for h in range(NH):
            gdm_ref[0, 0, h] = dzf[h]
            gdm_ref[0, 1, h] = def_[h]
            gdm_ref[1, 0, h] = dzb[h]
            gdm_ref[1, 1, h] = deb[h]

    return pl.pallas_call(
        body, name="ctx_backward",
        out_shape=[jax.ShapeDtypeStruct((D, 2 * AW), F32), jax.ShapeDtypeStruct((8, D), F32),
                   jax.ShapeDtypeStruct((2, 2, NH, CH, CH), F32)],
        in_specs=[pl.BlockSpec(memory_space=pltpu.VMEM)] * 6,
        out_specs=[pl.BlockSpec(memory_space=pltpu.VMEM)] * 3,
        compiler_params=_params(56),
    )(ctx, vecs, wi, dm, dscf, dscb)


def _load_w_in(wi_hbm, wcat, sems):
    cps = [pltpu.make_async_copy(wi_hbm.at[j], wcat.at[:, pl.ds(j * WI_C, WI_C)], sems.at[j]) for j in range(NCHIP)]
    for cp in cps:
        cp.start()
    for cp in cps:
        cp.wait()


def _in_proj(x, vecs, wi, gbufs):
    ln = x.shape[0]
    t = min(512, ln)
    nt = len(gbufs)
    steps = ln // t

    def body(x_ref, v_ref, wi_ref, *refs):
        z_ref, hx_ref = refs[nt:nt + 2]
        bufs = refs[nt + 2:2 * nt + 2]
        wcat, w_sems, send_sems, recv_sems = refs[2 * nt + 2:]
        i = pl.program_id(0)

        @pl.when(i == 0)
        def _():
            for cp in _gather_ici_copies(bufs, send_sems, recv_sems)[0]:
                cp.start()
            _load_w_in(wi_ref, wcat, w_sems)

        xv = x_ref[...]
        hx = (xv * _rms(xv) * v_ref[0:1, :]) * (1.0 + v_ref[2:3, :]) + v_ref[1:2, :]
        hb = hx.astype(BF)
        hx_ref[...] = hb
        z_ref[...] = _dot(hb, wcat[...], NN)

        @pl.when(i == steps - 1)
        def _():
            out_cp, in_cp = _gather_ici_copies(bufs, send_sems, recv_sems)
            for cp in in_cp:
                cp.wait_recv()
            for cp in out_cp:
                cp.wait_send()

    hbm = pl.BlockSpec(memory_space=pl.ANY)
    out = pl.pallas_call(
        body, name="in_proj", grid=(steps,),
        in_specs=[pl.BlockSpec((t, D), lambda i: (i, 0)), _const((8, D)), hbm] + [hbm] * nt,
        out_specs=[pl.BlockSpec((t, IN_COLS), lambda i: (i, 0)), pl.BlockSpec((t, D), lambda i: (i, 0))] + [hbm] * nt,
        out_shape=[jax.ShapeDtypeStruct((ln, IN_COLS), F32), jax.ShapeDtypeStruct((ln, D), BF)]
        + [jax.ShapeDtypeStruct(g.shape, g.dtype) for g in gbufs],
        input_output_aliases={3 + k: 2 + k for k in range(nt)},
        scratch_shapes=[pltpu.VMEM((D, IN_COLS), BF), pltpu.SemaphoreType.DMA((NCHIP,))]
        + [pltpu.SemaphoreType.DMA(((NCHIP - 1) * nt,))] * 2,
        compiler_params=_params(56, ("arbitrary",)),
    )(x, vecs, wi, *gbufs)
    return out[0], out[1], out[2:]


def _allgather_copies(src, out, send_sems, recv_sems, local_sem):
    x, y, c = _pos()
    me = 4 * x + 2 * y + c
    sends, recvs = [], []
    for k in range(1, NDEV):
        kx, ky, kc = (k >> 2) & 1, (k >> 1) & 1, k & 1
        peer = (x ^ kx, y ^ ky, c ^ kc)
        frm = 4 * (x ^ kx) + 2 * (y ^ ky) + (c ^ kc)
        sends.append(pltpu.make_async_remote_copy(src_ref=src, dst_ref=out.at[me], send_sem=send_sems.at[k - 1],
                                                  recv_sem=recv_sems.at[k - 1], device_id=peer, device_id_type=MESH))
        recvs.append(pltpu.make_async_remote_copy(src_ref=src, dst_ref=out.at[frm], send_sem=send_sems.at[k - 1],
                                                  recv_sem=recv_sems.at[k - 1], device_id=peer, device_id_type=MESH))
    return sends, recvs, pltpu.make_async_copy(src, out.at[me], local_sem)


def _in_proj_bwd(dz, x, dx1, vecs, wi, parts, early):
    ln = x.shape[0]
    t = min(512, ln)
    nt = len(parts)
    steps = ln // t

    def body(dz_ref, x_ref, dx1_ref, v_ref, wi_ref, *refs):
        ps = refs[:nt]
        early_ref = refs[nt]
        gx_ref, acc_ref = refs[nt + 1:nt + 3]
        got = refs[nt + 3:2 * nt + 3]
        early_all = refs[2 * nt + 3]
        wcat, w_sems, send_sems, recv_sems, ag_send, ag_recv, ag_local = refs[2 * nt + 4:]

        @pl.when(pl.program_id(0) == 0)
        def _():
            acc_ref[...] = jnp.zeros_like(acc_ref)
            _load_w_in(wi_ref, wcat, w_sems)
            for cp in _scatter_ici_copies(ps, got, send_sems, recv_sems):
                cp.start()
            sends, _, own = _allgather_copies(early_ref, early_all, ag_send, ag_recv, ag_local)
            own.start()
            for cp in sends:
                cp.start()

        dhx = _dot(dz_ref[...], wcat[...], NT)
        xv = x_ref[...]
        r = _rms(xv)
        xn = xv * r
        n1, sc = v_ref[0:1, :], v_ref[2:3, :]
        acc_ref[0:1, :] += jnp.sum(dhx * xn * (1.0 + sc), axis=0, keepdims=True)
        acc_ref[1:2, :] += jnp.sum(dhx, axis=0, keepdims=True)
        acc_ref[2:3, :] += jnp.sum(dhx * xn * n1, axis=0, keepdims=True)
        g = dhx * n1 * (1.0 + sc)
        gx_ref[...] = dx1_ref[...] + r * (g - xn * jnp.mean(g * xn, axis=-1, keepdims=True))

        @pl.when(pl.program_id(0) == steps - 1)
        def _():
            cps = _scatter_ici_copies(ps, got, send_sems, recv_sems)
            sends, recvs, own = _allgather_copies(early_ref, early_all, ag_send, ag_recv, ag_local)
            for cp in cps + recvs:
                cp.wait_recv()
            for cp in cps + sends:
                cp.wait_send()
            own.wait()

    hbm = pl.BlockSpec(memory_space=pl.ANY)
    out = pl.pallas_call(
        body, name="in_proj_bwd", grid=(steps,),
        in_specs=[pl.BlockSpec((t, IN_COLS), lambda i: (i, 0)), pl.BlockSpec((t, D), lambda i: (i, 0)),
                  pl.BlockSpec((t, D), lambda i: (i, 0)), _const((8, D)), hbm]
        + [hbm] * (nt + 1),
        out_specs=[pl.BlockSpec((t, D), lambda i: (i, 0)), pl.BlockSpec((8, D), lambda i: (0, 0))]
        + [hbm] * (nt + 1),
        out_shape=[jax.ShapeDtypeStruct((ln, D), F32), jax.ShapeDtypeStruct((8, D), F32)]
        + [jax.ShapeDtypeStruct((NCHIP - 1,) + p.shape[1:], BF) for p in parts]
        + [jax.ShapeDtypeStruct((NDEV,) + early.shape, F32)],
        scratch_shapes=[pltpu.VMEM((D, IN_COLS), BF), pltpu.SemaphoreType.DMA((NCHIP,))]
        + [pltpu.SemaphoreType.DMA(((NCHIP - 1) * nt,))] * 2
        + [pltpu.SemaphoreType.DMA((NDEV - 1,))] * 2 + [pltpu.SemaphoreType.DMA],
        compiler_params=_params(56, ("arbitrary",)),
    )(dz, x, dx1, vecs, wi, *parts, early)
    return out[0], out[1], out[2:2 + nt], out[2 + nt]


def _post_mixer(x, ycat, tgt, vecs, wo, wg, wu, wd):
    ln = x.shape[0]
    t = min(256, ln)

    def body(x_ref, y_ref, t_ref, v_ref, wo_ref, wg_ref, wu_ref, wd_ref,
             dx1_ref, dyc_ref, h2_ref, dy_ref, df_ref, act_ref, da_ref, db_ref, acc_ref, a_st, b_st):
        @pl.when(pl.program_id(0) == 0)
        def _():
            acc_ref[...] = jnp.zeros_like(acc_ref)

        g1, n2, sh2, sc2 = v_ref[0:1, :], v_ref[1:2, :], v_ref[2:3, :], v_ref[3:4, :]
        g2, nf = v_ref[4:5, :], v_ref[5:6, :]
        y = _dot(y_ref[...], wo_ref[...], NN)
        x1 = x_ref[...] + g1 * y
        r2 = _rms(x1)
        xn2 = x1 * r2
        t2 = xn2 * n2
        h2b = (t2 * (1.0 + sc2) + sh2).astype(BF)
        h2_ref[...] = h2b
        a = _dot(h2b, wg_ref[...], NT)
        b = _dot(h2b, wu_ref[...], NT)
        a_st[...] = a
        b_st[...] = b
        act = (_silu(a) * b).astype(BF)
        act_ref[...] = act
        f = _dot(act, wd_ref[...], NN)
        x2 = x1 + g2 * f
        r3 = _rms(x2)
        xn3 = x2 * r3
        e = xn3 * nf - t_ref[...]
        acc_ref[6:7, :] += jnp.sum(e * e, axis=0, keepdims=True) * (0.5 / D)
        dout = e * (1.0 / D)
        acc_ref[5:6, :] += jnp.sum(dout * xn3, axis=0, keepdims=True)
        gg = dout * nf
        dx2 = r3 * (gg - xn3 * jnp.mean(gg * xn3, axis=-1, keepdims=True))
        acc_ref[4:5, :] += jnp.sum(dx2 * f, axis=0, keepdims=True)
        dfb = (g2 * dx2).astype(BF)
        df_ref[...] = dfb
        dact = _dot(dfb, wd_ref[...], NT)
        a = a_st[...]
        b = b_st[...]
        s = jax.nn.sigmoid(a)
        da = (dact * b * (s * (1.0 + a * (1.0 - s)))).astype(BF)
        db = (dact * (a * s)).astype(BF)
        da_ref[...] = da
        db_ref[...] = db
        dh2 = _dot(da, wg_ref[...], NN) + _dot(db, wu_ref[...], NN)
        acc_ref[2:3, :] += jnp.sum(dh2, axis=0, keepdims=True)
        acc_ref[3:4, :] += jnp.sum(dh2 * t2, axis=0, keepdims=True)
        acc_ref[1:2, :] += jnp.sum(dh2 * xn2 * (1.0 + sc2), axis=0, keepdims=True)
        gx = dh2 * n2 * (1.0 + sc2)
        dx1 = dx2 + r2 * (gx - xn2 * jnp.mean(gx * xn2, axis=-1, keepdims=True))
        dx1_ref[...] = dx1
        acc_ref[0:1, :] += jnp.sum(dx1 * y, axis=0, keepdims=True)
        dyb = (g1 * dx1).astype(BF)
        dy_ref[...] = dyb
        dyc_ref[...] = _dot(dyb, wo_ref[...], NT)

    tok = pl.BlockSpec((t, D), lambda i: (i, 0))
    ffb = pl.BlockSpec((t, DFF), lambda i: (i, 0))
    return pl.pallas_call(
        body, name="post_mixer", grid=(ln // t,),
        in_specs=[tok, tok, tok, _const((8, D)), _const((D, D)), _const((DFF, D)), _const((DFF, D)),
                  _const((DFF, D))],
        out_specs=[tok, tok, tok, tok, tok, ffb, ffb, ffb, pl.BlockSpec((16, D), lambda i: (0, 0))],
        out_shape=[jax.ShapeDtypeStruct((ln, D), F32)] * 2 + [jax.ShapeDtypeStruct((ln, D), BF)] * 3
        + [jax.ShapeDtypeStruct((ln, DFF), BF)] * 3 + [jax.ShapeDtypeStruct((16, D), F32)],
        scratch_shapes=[pltpu.VMEM((t, DFF), F32)] * 2,
        compiler_params=_params(60, ("arbitrary",)),
    )(x, ycat, tgt, vecs, wo, wg, wu, wd)


def _exchange_copies(g, out, send_sems, recv_sems):
    x, y, c = _pos()
    return [pltpu.make_async_remote_copy(src_ref=g.at[j, 1 - c], dst_ref=out.at[j], send_sem=send_sems.at[j],
                                         recv_sem=recv_sems.at[j], device_id=(x, y, 1 - c), device_id_type=MESH)
            for j in range(NCHIP)]


def _tn_matmul(xa, dy, name, nb, k1, n, x_batched, dy_mode, tt, ctx_kv=None, carry=None):
    ln = xa.shape[-2]
    tt = min(tt, ln)
    steps = ln // tt
    n_in = 2 + (ctx_kv is not None) + (carry is not None)

    def body(x_ref, dy_ref, *refs):
        o_ref = refs[n_in - 2]
        if carry is not None:
            g_ref, got_ref = refs[n_in - 3], refs[n_in - 1]
            send_sems, recv_sems = refs[n_in:]

        @pl.when(pl.program_id(0) == 0)
        def _():
            if carry is not None:
                for cp in _exchange_copies(g_ref, got_ref, send_sems, recv_sems):
                    cp.start()
            o_ref[...] = jnp.zeros_like(o_ref)
            if ctx_kv is not None:
                for g in range(0, 2 * AW, HD):
                    col = 3 * AW + g
                    o_ref[col // n, :, col % n: col % n + HD] = refs[0][:, g:g + HD]

        xt = None if x_batched else jnp.transpose(x_ref[...])
        for b in range(nb):
            lhs = jnp.transpose(x_ref[b]) if x_batched else xt
            if dy_mode == "batched":
                rhs = dy_ref[b]
            elif dy_mode == "cols":
                rhs = dy_ref[:, b * n:(b + 1) * n]
            else:
                rhs = dy_ref[...]
            o_ref[b] += _dot(lhs, rhs, NN)

        if carry is not None:
            @pl.when(pl.program_id(0) == steps - 1)
            def _():
                cps = _exchange_copies(g_ref, got_ref, send_sems, recv_sems)
                for cp in cps:
                    cp.wait_recv()
                for cp in cps:
                    cp.wait_send()

    x_spec = (pl.BlockSpec((nb, tt, k1), lambda t: (0, t, 0)) if x_batched
              else pl.BlockSpec((tt, k1), lambda t: (t, 0)))
    if dy_mode == "batched":
        dy_spec = pl.BlockSpec((nb, tt, n), lambda t: (0, t, 0))
    elif dy_mode == "cols":
        dy_spec = pl.BlockSpec((tt, nb * n), lambda t: (t, 0))
    else:
        dy_spec = pl.BlockSpec((tt, n), lambda t: (t, 0))
    hbm = pl.BlockSpec(memory_space=pl.ANY)
    extra = [] if ctx_kv is None else [ctx_kv]
    in_specs = [x_spec, dy_spec] + [_const(e.shape) for e in extra]
    out_specs = [pl.BlockSpec((nb, k1, n), lambda t: (0, 0, 0))]
    out_shape = [jax.ShapeDtypeStruct((nb, k1, n), F32)]
    scratch = []
    if carry is not None:
        extra = extra + [carry]
        in_specs.append(hbm)
        out_specs.append(hbm)
        out_shape.append(jax.ShapeDtypeStruct((NCHIP,) + carry.shape[2:], F32))
        scratch = [pltpu.SemaphoreType.DMA((NCHIP,))] * 2
    out = pl.pallas_call(
        body, name=name, grid=(steps,),
        in_specs=in_specs, out_specs=out_specs, out_shape=out_shape, scratch_shapes=scratch,
        compiler_params=_params(60, ("arbitrary",)),
    )(xa, dy, *extra)
    return out[0] if carry is None else (out[0], out[1])


FWD_CHUNKS_PER_STEP = 4
BWD_CHUNKS_PER_STEP = 4


def _chunks_per_step(nc, want):
    return want if nc % want == 0 else 1


def _mixer_fwd(z, cos_t, sin_t, dm, sgw, gain, bfull, scf, scb, gbufs_a, gbufs_b):
    ln = z.shape[0]
    nc = ln // CH
    na = len(gbufs_a)
    gbufs = list(gbufs_a) + list(gbufs_b)
    nt = len(gbufs)
    cps = _chunks_per_step(nc, FWD_CHUNKS_PER_STEP)
    nb = nc // cps
    rows = cps * CH
    mid = nb // 2

    def rev(p, n):
        return p * n + (1 - p) * (nb - 1 - n)

    def col(j, both):
        if both:
            return pl.BlockSpec((rows, AW), lambda p, n: (rev(p, n), j))
        return pl.BlockSpec((rows, AW), lambda p, n: (p * n, j))

    def body(u_ref, v_ref, q_ref, k_ref, vr_ref, gf_ref, gb_ref, cos_ref, sin_ref, dm_ref, sgw_ref, gain_ref,
             bfull_ref, scf_ref, scb_ref, *refs):
        y_ref, sf_ref, sb_ref, of_ref = refs[nt:nt + 4]
        bufs = refs[nt + 4:2 * nt + 4]
        bufs_a, bufs_b = bufs[:na], bufs[na:]
        sb_all, st, a_send, a_recv, bi_send, bi_recv, bd_send, bd_recv = refs[2 * nt + 4:]
        p, n = pl.program_id(0), pl.program_id(1)

        @pl.when((p == 0) & (n == 0))
        def _():
            for cp in _gather_d2d_copies(bufs_a, a_send, a_recv)[0]:
                cp.start()
            for cp in _gather_ici_copies(bufs_b, bi_send, bi_recv)[0]:
                cp.start()

        @pl.when((p == 1) & (n == mid))
        def _():
            for cp in _gather_ici_copies(bufs_b, bi_send, bi_recv)[1]:
                cp.wait_recv()
            for cp in _gather_d2d_copies(bufs_b, bd_send, bd_recv)[0]:
                cp.start()

        def roped_k(r0):
            cos, sin = cos_ref[r0:r0 + CH, :], sin_ref[r0:r0 + CH, :]
            return [_rope(t, cos, sin) * K_SCALE for t in _heads(k_ref, r0)]

        @pl.when(p == 0)
        def _():
            @pl.when(n == 0)
            def _():
                st[...] = scb_ref[...]

            for s in reversed(range(cps)):
                m = (nb - 1 - n) * cps + s
                k, vr = roped_k(s * CH), _heads(vr_ref, s * CH)
                for h in range(NH):
                    sb_all[m, h] = st[h]
                    st[h] = dm_ref[1, 3, h] * st[h] + _mm_tn(k[h], dm_ref[1, 2, h] * vr[h])

        @pl.when(p == 1)
        def _():
            @pl.when(n == 0)
            def _():
                st[...] = scf_ref[...]

            mats = [[dm_ref[d, kind, h] for h in range(NH)] for d in range(2) for kind in range(3)]
            for s in range(cps):
                r0 = s * CH
                m = n * cps + s
                cos, sin = cos_ref[r0:r0 + CH, :], sin_ref[r0:r0 + CH, :]
                q = [_rope(t, cos, sin) for t in _heads(q_ref, r0)]
                k, vr = roped_k(r0), _heads(vr_ref, r0)
                u, v, gf, gb = _heads(u_ref, r0), _heads(v_ref, r0), _heads(gf_ref, r0), _heads(gb_ref, r0)
                cols = [slice(h * HD, (h + 1) * HD) for h in range(NH)]
                ya = [_gate_group(u[h], v[h], sgw_ref[h], gain_ref[:, cols[h]], bfull_ref[h]) for h in range(NH)]
                sf = [st[h] for h in range(NH)]
                sb = [sb_all[m, h] for h in range(NH)]
                ret = [_ret_head(q[h], k[h], vr[h], gf[h], gb[h], sf[h], sb[h], mats[0][h], mats[1][h], mats[2][h],
                                 mats[3][h], mats[4][h], mats[5][h]) for h in range(NH)]
                for h in range(NH):
                    yr, uf, _, of = ret[h]
                    y_ref[r0:r0 + CH, cols[h]] = ya[h].astype(BF)
                    y_ref[r0:r0 + CH, AW + h * HD:AW + (h + 1) * HD] = yr.astype(BF)
                    of_ref[r0:r0 + CH, cols[h]] = of
                    sf_ref[s, h] = sf[h]
                    sb_ref[s, h] = sb[h]
                    st[h] = dm_ref[0, 3, h] * sf[h] + uf

        @pl.when((p == 1) & (n == nb - 1))
        def _():
            a_out, a_in = _gather_d2d_copies(bufs_a, a_send, a_recv)
            b_out, b_in = _gather_d2d_copies(bufs_b, bd_send, bd_recv)
            for cp in a_in + b_in:
                cp.wait_recv()
            for cp in a_out + b_out + _gather_ici_copies(bufs_b, bi_send, bi_recv)[0]:
                cp.wait_send()

    hbm = pl.BlockSpec(memory_space=pl.ANY)
    tab = pl.BlockSpec((rows, HD), lambda p, n: (rev(p, n), 0))
    st_spec = pl.BlockSpec((cps, NH, HD, HD), lambda p, n: (p * n, 0, 0, 0))
    out = pl.pallas_call(
        body, name="mixer_fwd", grid=(2, nb),
        in_specs=[col(0, False), col(1, False), col(2, False), col(3, True), col(4, True), col(5, False),
                  col(6, False), tab, tab, _const((2, 4, NH, CH, CH)), _const((NH, CH, CH)), _const((1, AW)),
                  _const((NH, CH, CH)), _const((NH, HD, HD)), _const((NH, HD, HD))] + [hbm] * nt,
        out_specs=[pl.BlockSpec((rows, D), lambda p, n: (p * n, 0)), st_spec, st_spec,
                   pl.BlockSpec((rows, AW), lambda p, n: (p * n, 0))] + [hbm] * nt,
        out_shape=[jax.ShapeDtypeStruct((ln, D), BF), jax.ShapeDtypeStruct((nc, NH, HD, HD), F32),
                   jax.ShapeDtypeStruct((nc, NH, HD, HD), F32), jax.ShapeDtypeStruct((ln, AW), F32)]
        + [jax.ShapeDtypeStruct(g.shape, g.dtype) for g in gbufs],
        input_output_aliases={15 + k: 4 + k for k in range(nt)},
        scratch_shapes=[pltpu.VMEM((nc, NH, HD, HD), F32), pltpu.VMEM((NH, HD, HD), F32)]
        + [pltpu.SemaphoreType.DMA(((NCHIP - 1) * na,))] * 2
        + [pltpu.SemaphoreType.DMA(((NCHIP - 1) * (nt - na),))] * 4,
        compiler_params=_params(56, ("arbitrary", "arbitrary")),
    )(z, z, z, z, z, z, z, cos_t, sin_t, dm, sgw, gain, bfull, scf, scb, *gbufs)
    return out[0], out[1], out[2], out[3], out[4:]


def _mixer_bwd(z, dycat, of_all, cos_t, sin_t, dm, sgw, gain, bfull, sf_all, sb_all, parts):
    ln = z.shape[0]
    nc = ln // CH
    cps = _chunks_per_step(nc, BWD_CHUNKS_PER_STEP)
    nb = nc // cps
    rows = cps * CH

    def rev(p, n):
        return p * n + (1 - p) * (nb - 1 - n)

    def col(j, both):
        if both:
            return pl.BlockSpec((rows, AW), lambda p, n: (rev(p, n), j))
        return pl.BlockSpec((rows, AW), lambda p, n: (p * n, j))

    nt = len(parts)

    def body(u_ref, v_ref, q_ref, k_ref, vr_ref, gf_ref, gb_ref, dya_ref, dyr_ref, of_ref, cos_ref, sin_ref,
             dm_ref, sgw_ref, gain_ref, bfull_ref, sf_ref, sb_ref, *refs):
        ps = refs[:nt]
        dz_ref, ddm_ref, dsgw_ref, dgain_ref, dbf_ref, dscf_ref, dscb_ref = refs[nt:nt + 7]
        got = refs[nt + 7:2 * nt + 7]
        gf_all, run, send_sems, recv_sems = refs[2 * nt + 7:]
        p, n = pl.program_id(0), pl.program_id(1)

        @pl.when((p == 0) & (n == 0))
        def _():
            for cp in _scatter_ici_copies(ps, got, send_sems, recv_sems):
                cp.start()

        mats = [[dm_ref[d, kind, h] for h in range(NH)] for d in range(2) for kind in range(3)]

        @pl.when(p == 0)
        def _():
            @pl.when(n == 0)
            def _():
                run[...] = jnp.zeros_like(run)
                ddm_ref[...] = jnp.zeros_like(ddm_ref)
                dsgw_ref[...] = jnp.zeros_like(dsgw_ref)
                dgain_ref[...] = jnp.zeros_like(dgain_ref)
                dbf_ref[...] = jnp.zeros_like(dbf_ref)

            for s in reversed(range(cps)):
                r0 = s * CH
                m = (nb - 1 - n) * cps + s
                cos, sin = cos_ref[r0:r0 + CH, :], sin_ref[r0:r0 + CH, :]
                q = [_rope(t, cos, sin) for t in _heads(q_ref, r0)]
                gf, dyr, of = _heads(gf_ref, r0), _heads(dyr_ref, r0), _heads(of_ref, r0)
                for h in range(NH):
                    _, vjp = jax.vjp(functools.partial(_gated_norm, gf[h]), of[h])
                    (dof,) = vjp(dyr[h])
                    dsf = _mm_tn(q[h], mats[1][h] * dof)
                    g_next = run[h]
                    gf_all[m, h] = g_next.astype(BF)
                    ddm_ref[0, 3, h] += sf_ref[s, h] * g_next
                    run[h] = dsf + dm_ref[0, 3, h] * g_next

            @pl.when(n == nb - 1)
            def _():
                dscf_ref[...] = run[...]

        @pl.when(p == 1)
        def _():
            @pl.when(n == 0)
            def _():
                run[...] = jnp.zeros_like(run)

            for s in range(cps):
                r0 = s * CH
                m = n * cps + s
                rw = slice(r0, r0 + CH)
                cos, sin = cos_ref[r0:r0 + CH, :], sin_ref[r0:r0 + CH, :]
                q = [_rope(t, cos, sin) for t in _heads(q_ref, r0)]
                k = [_rope(t, cos, sin) * K_SCALE for t in _heads(k_ref, r0)]
                vr, gf, gb, dyr = _heads(vr_ref, r0), _heads(gf_ref, r0), _heads(gb_ref, r0), _heads(dyr_ref, r0)
                u, v, dya = _heads(u_ref, r0), _heads(v_ref, r0), _heads(dya_ref, r0)
                sf = [sf_ref[s, h] for h in range(NH)]
                sb = [sb_ref[s, h] for h in range(NH)]
                g_f = [gf_all[m, h].astype(F32) for h in range(NH)]
                g_b = [run[h] for h in range(NH)]
                cols = [slice(h * HD, (h + 1) * HD) for h in range(NH)]

                def chunk(u_, v_, sgw_, gain_, bfull_, q_, k_, vr_, gf_, gb_, sb_, df, xf, zf, db, xb, zb, sf=sf):
                    ya = [_gate_group(u_[h], v_[h], sgw_[h], gain_[h], bfull_[h]) for h in range(NH)]
                    ret = [_ret_head(q_[h], k_[h], vr_[h], gf_[h], gb_[h], sf[h], sb_[h], df[h], xf[h], zf[h],
                                     db[h], xb[h], zb[h])[:3] for h in range(NH)]
                    return ya, ret

                _, vjp = jax.vjp(chunk, u, v, [sgw_ref[h] for h in range(NH)], [gain_ref[:, c] for c in cols],
                                 [bfull_ref[h] for h in range(NH)], q, k, vr, gf, gb, sb, *mats)
                (du, dv, dsgw, dgain, dbf, dq, dk, dvr, dgf, dgb, dsb, ddf, dxf, dzf, ddb, dxb, dzb) = vjp(
                    (dya, [(dyr[h], g_f[h], g_b[h]) for h in range(NH)]))
                for h in range(NH):
                    dz_ref[rw, h * HD:(h + 1) * HD] = du[h].astype(BF)
                    dz_ref[rw, AW + h * HD:AW + (h + 1) * HD] = dv[h].astype(BF)
                    dz_ref[rw, 2 * AW + h * HD:2 * AW + (h + 1) * HD] = _rope_bwd(dq[h], cos, sin).astype(BF)
                    dz_ref[rw, 3 * AW + h * HD:3 * AW + (h + 1) * HD] = _rope_bwd(dk[h] * K_SCALE, cos,
                                                                                  sin).astype(BF)
                    dz_ref[rw, 4 * AW + h * HD:4 * AW + (h + 1) * HD] = dvr[h].astype(BF)
                    dz_ref[rw, 5 * AW + h * HD:5 * AW + (h + 1) * HD] = dgf[h].astype(BF)
                    dz_ref[rw, 6 * AW + h * HD:6 * AW + (h + 1) * HD] = dgb[h].astype(BF)
                    ddm_ref[0, 0, h] += ddf[h]
                    ddm_ref[0, 1, h] += dxf[h]
                    ddm_ref[0, 2, h] += dzf[h]
                    ddm_ref[1, 0, h] += ddb[h]
                    ddm_ref[1, 1, h] += dxb[h]
                    ddm_ref[1, 2, h] += dzb[h]
                    ddm_ref[1, 3, h] += sb[h] * g_b[h]
                    dsgw_ref[h] += dsgw[h]
                    dgain_ref[:, cols[h]] += dgain[h]
                    dbf_ref[h] += dbf[h]
                    run[h] = dsb[h] + dm_ref[1, 3, h] * g_b[h]

            @pl.when(n == nb - 1)
            def _():
                dscb_ref[...] = run[...]

        @pl.when((p == 1) & (n == nb - 1))
        def _():
            cps_ = _scatter_ici_copies(ps, got, send_sems, recv_sems)
            for cp in cps_:
                cp.wait_recv()
            for cp in cps_:
                cp.wait_send()

    hbm = pl.BlockSpec(memory_space=pl.ANY)
    tab = pl.BlockSpec((rows, HD), lambda p, n: (rev(p, n), 0))
    tile4 = jax.ShapeDtypeStruct((NH, CH, CH), F32)
    out = pl.pallas_call(
        body, name="mixer_bwd", grid=(2, nb),
        in_specs=[col(0, False), col(1, False), col(2, True), col(3, False), col(4, False), col(5, True),
                  col(6, False),
                  pl.BlockSpec((rows, AW), lambda p, n: (p * n, 0)),
                  pl.BlockSpec((rows, AW), lambda p, n: (rev(p, n), 1)),
                  pl.BlockSpec((rows, AW), lambda p, n: ((1 - p) * (nb - 1 - n), 0)),
                  tab, tab, _const((2, 4, NH, CH, CH)), _const((NH, CH, CH)), _const((1, AW)),
                  _const((NH, CH, CH)),
                  pl.BlockSpec((cps, NH, HD, HD), lambda p, n: (rev(p, n), 0, 0, 0)),
                  pl.BlockSpec((cps, NH, HD, HD), lambda p, n: (p * n, 0, 0, 0))] + [hbm] * nt,
        out_specs=[pl.BlockSpec((rows, IN_COLS), lambda p, n: (p * n, 0)),
                   pl.BlockSpec((2, 4, NH, CH, CH), lambda p, n: (0, 0, 0, 0, 0)),
                   pl.BlockSpec((NH, CH, CH), lambda p, n: (0, 0, 0)),
                   pl.BlockSpec((1, AW), lambda p, n: (0, 0)),
                   pl.BlockSpec((NH, CH, CH), lambda p, n: (0, 0, 0)),
                   pl.BlockSpec((NH, HD, HD), lambda p, n: (0, 0, 0)),
                   pl.BlockSpec((NH, HD, HD), lambda p, n: (0, 0, 0))] + [hbm] * nt,
        out_shape=[jax.ShapeDtypeStruct((ln, IN_COLS), BF), jax.ShapeDtypeStruct((2, 4, NH, CH, CH), F32),
                   tile4, jax.ShapeDtypeStruct((1, AW), F32), tile4, tile4, tile4]
        + [jax.ShapeDtypeStruct((NCHIP - 1,) + p.shape[1:], BF) for p in parts],
        scratch_shapes=[pltpu.VMEM((nc, NH, HD, HD), BF), pltpu.VMEM((NH, HD, HD), F32)]
        + [pltpu.SemaphoreType.DMA(((NCHIP - 1) * nt,))] * 2,
        compiler_params=_params(60, ("arbitrary", "arbitrary")),
    )(z, z, z, z, z, z, z, dycat, dycat, of_all, cos_t, sin_t, dm, sgw, gain, bfull, sf_all, sb_all, *parts)
    return out[:7], out[7:]


def _small_reduce(ddm, ddm_ctx, dm, dbf):
    def body(ddm_ref, dctx_ref, dm_ref, dbf_ref, lg_ref, sgb_ref):
        ex = _decay_exponents()
        ones = jnp.ones((8, CH), F32)
        for d in range(2):
            for h in range(NH):
                tot = jnp.zeros((CH, CH), F32)
                for kind in range(4):
                    g = ddm_ref[d, kind, h]
                    if kind >= 2:
                        g = g + dctx_ref[d, kind - 2, h]
                    tot = tot + g * dm_ref[d, kind, h] * ex[d][kind]
                lg_ref[d * NH + h: d * NH + h + 1, :] = jnp.sum(tot, axis=0, keepdims=True)
        sgb_ref[...] = jnp.zeros_like(sgb_ref)
        for g in range(NH):
            r = lax.dot_general(ones, dbf_ref[g], (NT, ((), ())), precision=HI, preferred_element_type=F32)
            sgb_ref[g:g + 1, :] = r[0:1, :]

    return pl.pallas_call(
        body, name="small_reduce",
        out_shape=[jax.ShapeDtypeStruct((8, CH), F32), jax.ShapeDtypeStruct((8, CH), F32)],
        in_specs=[pl.BlockSpec(memory_space=pltpu.VMEM)] * 4,
        out_specs=[pl.BlockSpec(memory_space=pltpu.VMEM)] * 2,
        compiler_params=_params(32),
    )(ddm, ddm_ctx, dm, dbf)


def _mod_backward(ct_pad_t, cctx_col, dmod_pad, dcmod_cols, w_mod_s):
    def body(ct_ref, cc_ref, dm_ref, dc_ref, w_ref, gw_ref, part_ref):
        dcm = dc_ref[0:1, :]
        for d in range(1, NDEV):
            dcm = dcm + dc_ref[d:d + 1, :]
        gw_ref[...] = (jnp.dot(_silu(ct_ref[...]), dm_ref[...], precision=HI, preferred_element_type=F32)
                       + _silu(cc_ref[...]) * dcm)
        part_ref[...] = lax.dot_general(jnp.broadcast_to(dcm, (8, dcm.shape[1])), w_ref[...], (NT, ((), ())),
                                        precision=HI, preferred_element_type=F32)

    return pl.pallas_call(
        body, name="mod_backward",
        out_shape=[jax.ShapeDtypeStruct(w_mod_s.shape, F32), jax.ShapeDtypeStruct((8, D), F32)],
        in_specs=[pl.BlockSpec(memory_space=pltpu.VMEM)] * 5,
        out_specs=[pl.BlockSpec(memory_space=pltpu.VMEM)] * 2,
        compiler_params=_params(48),
    )(ct_pad_t, cctx_col, dmod_pad, dcmod_cols, w_mod_s)


def _cctx_update(parts, c_ctx, m, v):
    def body(p_ref, c_ref, m_ref, v_ref, g_ref, d_ref, mo_ref, vo_ref):
        tot = ((p_ref[0] + p_ref[2]) + p_ref[4]) + p_ref[6]
        cv = c_ref[...]
        s = jax.nn.sigmoid(cv)
        g = tot * (s * (1.0 + cv * (1.0 - s)))
        g_ref[...] = g
        d_ref[...], mo_ref[...], vo_ref[...] = _adamw_math(cv, g, m_ref[...], v_ref[...])

    return pl.pallas_call(
        body, name="cctx_update",
        out_shape=[jax.ShapeDtypeStruct((1, D), F32)] * 4,
        in_specs=[pl.BlockSpec(memory_space=pltpu.VMEM)] * 4,
        out_specs=[pl.BlockSpec(memory_space=pltpu.VMEM)] * 4,
        compiler_params=_params(16),
    )(parts, c_ctx, m, v)


def _small_update(gathered, wp, mp, vp):
    def body(g_ref, w_ref, m_ref, v_ref, go_ref, d_ref, mo_ref, vo_ref, loss_ref):
        tot = g_ref[0]
        for d in range(1, NDEV):
            tot = tot + g_ref[d]
        go_ref[Q_BMOD:Q_N1, :] = tot[P_DMOD:P_N1, :] + tot[P_DCMOD:P_DMOD, :]
        go_ref[Q_N1:Q_LG, :] = tot[P_N1:P_LG, :]
        lg = jnp.sum(tot[P_LG:P_N2, :], axis=1, keepdims=True)
        go_ref[Q_LG:Q_N2, :] = lg * jax.nn.sigmoid(-w_ref[Q_LG:Q_N2, :])
        go_ref[Q_N2:Q_ROWS, :] = tot[P_N2:P_LOSS, :]
        d_ref[...], mo_ref[...], vo_ref[...] = _adamw_math(w_ref[...], go_ref[...], m_ref[...], v_ref[...])
        ls = jnp.sum(jnp.sum(tot[P_LOSS:P_ROWS, :], axis=1, keepdims=True), axis=0, keepdims=True)
        loss_ref[...] = jnp.broadcast_to(ls, (8, CH))

    return pl.pallas_call(
        body, name="small_update",
        out_shape=[jax.ShapeDtypeStruct((Q_ROWS, CH), F32)] * 4 + [jax.ShapeDtypeStruct((8, CH), F32)],
        in_specs=[pl.BlockSpec(memory_space=pltpu.VMEM)] * 4,
        out_specs=[pl.BlockSpec(memory_space=pltpu.VMEM)] * 5,
        compiler_params=_params(32),
    )(gathered, wp, mp, vp)


def _rows(a):
    r = a.reshape(-1, CH)
    return jnp.pad(r, ((0, -r.shape[0] % 8), (0, 0)))


def _pack_small(b_mod, norm1, sg_gain, sg_w, sg_b, lf, lb, norm2, norm_f):
    lg = jnp.broadcast_to(jnp.concatenate([lf.reshape(NH), lb.reshape(NH)])[:, None], (2 * NH, CH))
    return jnp.concatenate([_rows(b_mod), _rows(norm1), _rows(sg_gain), _rows(sg_w), _rows(sg_b), lg,
                            _rows(norm2), _rows(norm_f)], axis=0)


def _unpack_small(p):
    return (p[Q_BMOD:Q_N1].reshape(1, 6 * D), p[Q_N1:Q_GAIN].reshape(1, D), p[Q_GAIN:Q_GAIN + NH].reshape(1, AW),
            p[Q_SGW:Q_SGB].reshape(1, NH, CH, CH), p[Q_SGB:Q_SGB + NH].reshape(1, NH, CH),
            p[Q_LG:Q_LG + NH, 0].reshape(1, NH), p[Q_LG + NH:Q_N2, 0].reshape(1, NH),
            p[Q_N2:Q_NF].reshape(1, D), p[Q_NF:Q_ROWS].reshape(D))


def _rope_tables(ln):
    pos = np.arange(ln)
    rows = (pos // GRID_W).astype(np.float32)
    cols = (pos % GRID_W).astype(np.float32)
    n_freq = HD // 4
    inv = (np.float32(ROPE_BASE) ** (-np.arange(n_freq, dtype=np.float32) / np.float32(n_freq))).astype(np.float32)
    ar = rows[:, None] * inv[None, :]
    ac = cols[:, None] * inv[None, :]
    cos_t = np.concatenate([np.cos(ar), np.cos(ar), np.cos(ac), np.cos(ac)], axis=1).astype(np.float32)
    sin_t = np.concatenate([-np.sin(ar), np.sin(ar), -np.sin(ac), np.sin(ac)], axis=1).astype(np.float32)
    return jnp.asarray(cos_t), jnp.asarray(sin_t)


def kernel(x, c, ctx, c_ctx, w_mod, b_mod, norm1, w_in, sg_gain, sg_w, sg_b, ret_logit_f, ret_logit_b, w_out, norm2, w_gate, w_up, w_down, norm_f, loss_target, m_c_ctx, m_w_mod, m_b_mod, m_norm1, m_w_in, m_sg_gain, m_sg_w, m_sg_b, m_ret_logit_f, m_ret_logit_b, m_w_out, m_norm2, m_w_gate, m_w_up, m_w_down, m_norm_f, v_c_ctx, v_w_mod, v_b_mod, v_norm1, v_w_in, v_sg_gain, v_sg_w, v_sg_b, v_ret_logit_f, v_ret_logit_b, v_w_out, v_norm2, v_w_gate, v_w_up, v_w_down, v_norm_f):
    ln = x.shape[1]
    xi, yi, ci = _pos()
    chip = 2 * xi + yi
    me = 4 * xi + 2 * yi + ci
    x2d = x.reshape(ln, D)
    tgt = loss_target.reshape(ln, D)
    mod_c = w_mod.shape[2]

    tr = lambda a: jnp.swapaxes(a[0], 0, 1)
    gbufs, c_all, prod_all = _prologue(c, c_ctx.reshape(1, D), w_mod[0],
                                       [w_in[0], w_out[0], tr(w_gate), tr(w_up), w_down[0]])
    wi = gbufs[0].reshape(NCHIP, D, WI_C)
    gbufs_a, gbufs_b = gbufs[1:3], gbufs[3:5]
    c_all = c_all.reshape(NDEV, D)
    prod_chips = prod_all[0::2]
    mod_rows = jnp.transpose(prod_chips, (1, 0, 2)).reshape(16, NCHIP * mod_c) + b_mod
    mod = lax.dynamic_slice_in_dim(mod_rows, me, 1, axis=0)
    cmod = mod_rows[8:9]
    sh1, sc1, g1, sh2, sc2, g2 = [mod[:, i * D:(i + 1) * D] for i in range(6)]
    csh1, csc1 = cmod[:, 0:D], cmod[:, D:2 * D]
    zrow = jnp.zeros((1, D), F32)
    vec_in = jnp.concatenate([norm1, sh1, sc1] + [zrow] * 5, axis=0)
    vec_ctx = jnp.concatenate([norm1, csh1, csc1] + [zrow] * 5, axis=0)
    vec_post = jnp.concatenate([g1, norm2, sh2, sc2, g2, norm_f.reshape(1, D), zrow, zrow], axis=0)

    logits = jnp.concatenate([ret_logit_f.reshape(NH), ret_logit_b.reshape(NH)])
    dm = _decay_mats(jnp.broadcast_to(logits[:, None, None], (2 * NH, CH, CH)))
    ctx2d = ctx.reshape(ctx.shape[1], D)
    scf, scb = _ctx_forward(ctx2d, vec_ctx, wi, dm)

    cos_t, sin_t = _rope_tables(ln)
    z, hx, gbufs_a = _in_proj(x2d, vec_in, wi, gbufs_a)
    bfull = jnp.broadcast_to(sg_b[0][:, :, None], (NH, CH, CH))
    ycat, sf_all, sb_all, of_all, gbufs = _mixer_fwd(z, cos_t, sin_t, dm, sg_w[0], sg_gain, bfull, scf, scb,
                                             gbufs_a, gbufs_b)
    wo, wg_t, wu_t, wd = [g.reshape(-1, D) for g in gbufs]

    dx1, dycat, h2, dy, df, act, da, db, acc_post = _post_mixer(x2d, ycat, tgt, vec_post, wo, wg_t, wu_t, wd)

    cidx = ci.reshape(1).astype(jnp.int32)
    where = jnp.stack([ci, chip]).astype(jnp.int32)

    def halves_summed(full, names):
        full = [g.reshape(NCHIP, 2, g.shape[1] // 2, g.shape[2]) for g in full]
        from_sib = _rs_exchange_halves(full, "rs_exchange_" + names[0])
        return [_rs_add_halves(g, r, where, "rs_add_halves_" + nm) for g, r, nm in zip(full, from_sib, names)]

    def split(g):
        return g.reshape(NCHIP, 2, g.shape[1] // (2 * NCHIP), g.shape[2])

    g_wd = split(_tn_matmul(act, df, "grad_w_down", 1, DFF, D, False, "shared", 1024))
    g_wu, x_wd = _tn_matmul(db, h2, "grad_w_up", 1, DFF, D, False, "shared", 1024, carry=g_wd)
    g_wu = split(g_wu)
    g_wg, x_wu = _tn_matmul(da, h2, "grad_w_gate", 1, DFF, D, False, "shared", 1024, carry=g_wu)
    g_wg = split(g_wg)
    g_wo, x_wg = _tn_matmul(ycat, dy, "grad_w_out", 1, D, D, False, "shared", 1024, carry=g_wg)
    g_wo = split(g_wo)
    x_wo = _rs_exchange_halves([g_wo], "rs_exchange_w_out")[0]
    names = ["w_in", "w_out", "w_gate", "w_up", "w_down"]
    sums_b = [_rs_add_halves(g, r, where, "rs_add_halves_" + nm)
              for g, r, nm in zip([g_wo, g_wg, g_wu, g_wd], [x_wo, x_wg, x_wu, x_wd], names[1:])]

    (dz, ddm, dsgw, dgain, dbf, dscf, dscb), from_chips_b = _mixer_bwd(
        z, dycat, of_all, cos_t, sin_t, dm, sg_w[0], sg_gain, bfull, sf_all, sb_all, [s[1] for s in sums_b])
    gwkv, acc_ctx, ddm_ctx = _ctx_backward(ctx2d, vec_ctx, wi, dm, dscf, dscb)
    g_wi = _tn_matmul(hx, dz, "grad_w_in", NCHIP, D, WI_C, False, "cols", 1024, ctx_kv=gwkv)
    sums_a = halves_summed([g_wi], names[:1])
    lg_part, dsgb = _small_reduce(ddm, ddm_ctx, dm, dbf)
    dcmod = jnp.concatenate([acc_ctx[1:2], acc_ctx[2:3], jnp.zeros((1, 4 * D), F32)], axis=1)
    dmod_rest = jnp.concatenate([acc_post[0:1], acc_post[2:3], acc_post[3:4], acc_post[4:5]], axis=1)
    early = jnp.concatenate([_rows(dcmod), _rows(dmod_rest), _rows(dgain), _rows(dsgw), dsgb, lg_part,
                             _rows(acc_post[1:2]), _rows(acc_post[5:6]), _rows(acc_post[6:7])], axis=0)
    gx, acc_in, from_chips_a, early_all = _in_proj_bwd(dz, x2d, dx1, vec_in, wi, [s[1] for s in sums_a], early)

    sums = sums_a + sums_b
    from_chips = list(from_chips_a) + list(from_chips_b)
    finals = [_rs_add_chips(s[0], r, "rs_add_chips_" + nm) for s, r, nm in zip(sums, from_chips, names)]
    others = _rs_share_final(finals)

    late = jnp.concatenate([_rows(acc_in[1:2]), _rows(acc_in[2:3]), _rows(acc_in[0:1] + acc_ctx[0:1])], axis=0)
    late_all = _allgather_small(late, "gather_small")
    n_dc, n_l = P_DMOD - P_DCMOD, 16
    gathered = jnp.concatenate([early_all[:, :n_dc], late_all[:, :n_l], early_all[:, n_dc:n_dc + 32],
                                late_all[:, n_l:], early_all[:, n_dc + 32:]], axis=1)
    dmod_all = gathered[:, P_DMOD:P_N1].reshape(NDEV, 6 * D)
    dcmod_all = gathered[:, P_DCMOD:P_DMOD].reshape(NDEV, 6 * D)
    dmod_cols = lax.dynamic_slice_in_dim(dmod_all, chip * mod_c, mod_c, axis=1)
    dcmod_cols = lax.dynamic_slice_in_dim(dcmod_all, chip * mod_c, mod_c, axis=1)
    dmod_pad = jnp.concatenate([dmod_cols, jnp.zeros((CH - NDEV, mod_c), F32)], axis=0)
    ct_pad_t = jnp.concatenate([jnp.transpose(c_all), jnp.zeros((D, CH - NDEV), F32)], axis=1)
    g_wmod, cctx_part = _mod_backward(ct_pad_t, c_ctx.reshape(D, 1), dmod_pad, dcmod_cols, w_mod[0])
    parts = _allgather_small(cctx_part[0:1], "gather_cctx")
    g_cctx, d_cctx, nm_cctx, nv_cctx = _cctx_update(parts, c_ctx.reshape(1, D), m_c_ctx.reshape(1, D),
                                                    v_c_ctx.reshape(1, D))

    wp = _pack_small(b_mod, norm1, sg_gain, sg_w, sg_b, ret_logit_f, ret_logit_b, norm2, norm_f)
    mp = _pack_small(m_b_mod, m_norm1, m_sg_gain, m_sg_w, m_sg_b, m_ret_logit_f, m_ret_logit_b, m_norm2, m_norm_f)
    vp = _pack_small(v_b_mod, v_norm1, v_sg_gain, v_sg_w, v_sg_b, v_ret_logit_f, v_ret_logit_b, v_norm2, v_norm_f)
    gp, dp, mp2, vp2, loss_t = _small_update(gathered, wp, mp, vp)

    big_w = [w_in[0], w_out[0], tr(w_gate), tr(w_up), w_down[0]]
    big_m = [m_w_in[0], m_w_out[0], tr(m_w_gate), tr(m_w_up), m_w_down[0]]
    big_v = [v_w_in[0], v_w_out[0], tr(v_w_gate), tr(v_w_up), v_w_down[0]]
    upd = [_adamw_halves(w, own, oth, m, v, cidx, "adamw_" + nm) for w, own, oth, m, v, nm in
           zip(big_w, finals, others, big_m, big_v, names)]
    big_g = [g_wmod] + [u[0] for u in upd]
    big = [_adamw(w_mod[0], g_wmod, m_w_mod[0], v_w_mod[0], "adamw_w_mod")] + [u[1:] for u in upd]

    def assemble(small, cctx, bigs):
        b_mod_, norm1_, gain_, sgw_, sgb_, lf_, lb_, norm2_, normf_ = _unpack_small(small)
        wm, wi_, wo_, wg_, wu_, wd_ = [b[None] for b in bigs]
        wg_, wu_ = jnp.swapaxes(wg_, 1, 2), jnp.swapaxes(wu_, 1, 2)
        return [cctx.reshape(D), wm, b_mod_, norm1_, wi_, gain_, sgw_, sgb_, lf_, lb_, wo_, norm2_, wg_, wu_, wd_,
                normf_]

    out = [loss_t[0, 0], gx.reshape(1, ln, D)]
    out += assemble(gp, g_cctx, big_g)
    out += assemble(dp, d_cctx, [b[0] for b in big])
    out += assemble(mp2, nm_cctx, [b[1] for b in big])
    out += assemble(vp2, nv_cctx, [b[2] for b in big])
    return tuple(out)
```

```python
import functools

import jax
import jax.numpy as jnp
import numpy as np
from jax import lax
from jax.experimental import pallas as pl
from jax.experimental.pallas import tpu as pltpu

F32 = jnp.float32
BF = jnp.bfloat16
MESH = pl.DeviceIdType.MESH

D = 1024
CH = 128
HD = 128
NH = 4
AW = 512
IN_COLS = 3584
DFF = 2816
NCHIP = 4
NDEV = 8
WI_C = IN_COLS // NCHIP
FF_C = DFF // NCHIP
WO_R = D // NCHIP
EPS = 1e-6
GRID_W = 64
ROPE_BASE = 10000.0
K_SCALE = HD ** -0.5
LR, B1, B2, AEPS, WD, STEP = 0.001, 0.9, 0.999, 1e-08, 0.01, 10
VMEM_MB = 1 << 20
HI = lax.Precision.HIGHEST

P_DCMOD, P_DMOD, P_N1, P_GAIN, P_SGW, P_SGB, P_LG, P_N2, P_NF, P_LOSS = 0, 48, 96, 104, 112, 624, 632, 640, 648, 656
P_ROWS = 664
Q_BMOD, Q_N1, Q_GAIN, Q_SGW, Q_SGB, Q_LG, Q_N2, Q_NF = 0, 48, 56, 64, 576, 584, 592, 600
Q_ROWS = 608


def _params(vmem_mb, sem=None):
    return pltpu.CompilerParams(vmem_limit_bytes=vmem_mb * VMEM_MB, dimension_semantics=sem)


def _const(shape):
    nd = len(shape)
    return pl.BlockSpec(shape, lambda *_: (0,) * nd, pipeline_mode=pl.Buffered(1))


def _pos():
    return lax.axis_index("x"), lax.axis_index("y"), lax.axis_index("c")


def _dot(a, b, dims):
    return lax.dot_general(a, b, (dims, ((), ())), preferred_element_type=F32)


NN = ((1,), (0,))
NT = ((1,), (1,))
TN = ((0,), (0,))


@jax.custom_vjp
def _mm(a, b):
    return _dot(a.astype(BF), b.astype(BF), NN)


def _mm_f(a, b):
    return _mm(a, b), (a.astype(BF), b.astype(BF))


def _mm_b(res, g):
    a, b = res
    gb = g.astype(BF)
    return _dot(gb, b, NT), _dot(a, gb, TN)


_mm.defvjp(_mm_f, _mm_b)


@jax.custom_vjp
def _mm_nt(a, b):
    return _dot(a.astype(BF), b.astype(BF), NT)


def _mm_nt_f(a, b):
    return _mm_nt(a, b), (a.astype(BF), b.astype(BF))


def _mm_nt_b(res, g):
    a, b = res
    gb = g.astype(BF)
    return _dot(gb, b, NN), _dot(gb, a, TN)


_mm_nt.defvjp(_mm_nt_f, _mm_nt_b)


@jax.custom_vjp
def _mm_tn(a, b):
    return _dot(a.astype(BF), b.astype(BF), TN)


def _mm_tn_f(a, b):
    return _mm_tn(a, b), (a.astype(BF), b.astype(BF))


def _mm_tn_b(res, g):
    a, b = res
    gb = g.astype(BF)
    return _dot(b, gb, NT), _dot(a, gb, NN)


_mm_tn.defvjp(_mm_tn_f, _mm_tn_b)


def _gelu(x):
    return x * (0.5 * (1.0 + jnp.tanh(0.7978845608028654 * (x + 0.044715 * (x * x * x)))))


def _silu(x):
    return x * jax.nn.sigmoid(x)


def _rms(x):
    return lax.rsqrt(jnp.mean(x * x, axis=-1, keepdims=True) + EPS)


def _swap32(t):
    lane = lax.broadcasted_iota(jnp.int32, t.shape, 1)
    first = (lane % 64) < 32
    return jnp.where(first, pltpu.roll(t, 96, 1), pltpu.roll(t, 32, 1))


def _rope(t, cos, sin):
    return t * cos + _swap32(t) * sin


def _rope_bwd(d, cos, sin):
    return d * cos + _swap32(d * sin)


def _heads(ref, r0=0):
    return [ref[r0:r0 + CH, h * HD:(h + 1) * HD].astype(F32) for h in range(NH)]


def _gate_group(u, v, sgw, gain, bfull):
    gv = _gelu(v)
    return _gelu(u) * (_mm(sgw, gv * _rms(gv) * gain) + bfull)


def _gated_norm(gate, o):
    return _silu(gate) * (o * _rms(o))


def _ret_head(q, k, vr, gf, gb, sf, sb, df, xf, zf, db, xb, zb):
    a = _mm_nt(q, k)
    of = _mm(a * df, vr) + xf * _mm(q, sf)
    ob = _mm(a * db, vr) + xb * _mm(q, sb)
    return _gated_norm(gf, of) + _gated_norm(gb, ob), _mm_tn(k, zf * vr), _mm_tn(k, zb * vr), of


def _ctx_states(ctx0, ctx1, n1, csh, csc, wk, wv, zf, zb, ef, eb):
    hc0 = (ctx0 * _rms(ctx0) * n1) * (1.0 + csc) + csh
    hc1 = (ctx1 * _rms(ctx1) * n1) * (1.0 + csc) + csh
    scf, scb = [], []
    for h in range(NH):
        k0, k1 = _mm(hc0, wk[h]) * K_SCALE, _mm(hc1, wk[h]) * K_SCALE
        v0, v1 = _mm(hc0, wv[h]), _mm(hc1, wv[h])
        scf.append(ef[h] * _mm_tn(k0, zf[h] * v0) + _mm_tn(k1, zf[h] * v1))
        scb.append(eb[h] * _mm_tn(k1, zb[h] * v1) + _mm_tn(k0, zb[h] * v0))
    return scf, scb


def _allgather_small(v, name):
    r, n = v.shape

    def body(v_ref, out_ref, send_sems, recv_sems, local_sem):
        x, y, c = _pos()
        me = 4 * x + 2 * y + c
        mine = pltpu.make_async_copy(v_ref, out_ref.at[me], local_sem)
        mine.start()
        sent = []
        for k in range(1, NDEV):
            kx, ky, kc = (k >> 2) & 1, (k >> 1) & 1, k & 1
            peer = (x ^ kx, y ^ ky, c ^ kc)
            cp = pltpu.make_async_remote_copy(src_ref=v_ref, dst_ref=out_ref.at[me], send_sem=send_sems.at[k - 1],
                                              recv_sem=recv_sems.at[k - 1], device_id=peer, device_id_type=MESH)
            cp.start()
            sent.append(cp)
        for k in range(1, NDEV):
            kx, ky, kc = (k >> 2) & 1, (k >> 1) & 1, k & 1
            peer = (x ^ kx, y ^ ky, c ^ kc)
            src = 4 * (x ^ kx) + 2 * (y ^ ky) + (c ^ kc)
            pltpu.make_async_remote_copy(src_ref=v_ref, dst_ref=out_ref.at[src], send_sem=send_sems.at[k - 1],
                                         recv_sem=recv_sems.at[k - 1], device_id=peer, device_id_type=MESH).wait_recv()
        for cp in sent:
            cp.wait_send()
        mine.wait()

    return pl.pallas_call(
        body, name=name,
        out_shape=jax.ShapeDtypeStruct((NDEV, r, n), F32),
        in_specs=[pl.BlockSpec(memory_space=pltpu.VMEM)],
        out_specs=pl.BlockSpec(memory_space=pltpu.VMEM),
        scratch_shapes=[pltpu.SemaphoreType.DMA((NDEV - 1,)), pltpu.SemaphoreType.DMA((NDEV - 1,)),
                        pltpu.SemaphoreType.DMA],
        compiler_params=_params(16),
    )(v)


def _chip_offsets():
    return [((k >> 1) & 1, k & 1) for k in range(1, NCHIP)]


def _prologue(c, c_ctx, w_mod_s, shards):
    nt = len(shards)
    shapes = [s.shape for s in shards]
    mod_c = w_mod_s.shape[1]

    def body(*refs):
        c_ref, cc_ref, wm_ref = refs[:3]
        srcs = refs[3:3 + nt]
        outs = refs[3 + nt:3 + 2 * nt]
        call_ref, prod_ref = refs[3 + 2 * nt:5 + 2 * nt]
        stages = refs[5 + 2 * nt:5 + 3 * nt]
        ct = refs[5 + 3 * nt]
        loaded = refs[6 + 3 * nt:6 + 4 * nt]
        (c_send, c_recv, p_send, p_recv, ici_send, ici_recv, d2d_send, d2d_recv, local_sems,
         load_sems) = refs[6 + 4 * nt:]
        x, y, c = _pos()
        chip = 2 * x + y
        me = 4 * x + 2 * y + c
        sib = (x, y, 1 - c)
        loads = [pltpu.make_async_copy(wm_ref, loaded[0], load_sems.at[0])]
        loads += [pltpu.make_async_copy(srcs[t], loaded[t], load_sems.at[t]) for t in range(1, nt)]
        for cp in loads:
            cp.start()
        pending = []

        def stage(t, src):
            half = shapes[t][0] // 2
            stages[t][0] = src[0:half, :].astype(BF)
            stages[t][1] = src[half:2 * half, :].astype(BF)
            cp = pltpu.make_async_copy(stages[t], outs[t].at[chip], local_sems.at[t])
            cp.start()
            pending.append(cp)

        stage(0, srcs[0])
        sends = []
        for k, (kx, ky) in enumerate(_chip_offsets()):
            cp = pltpu.make_async_remote_copy(src_ref=stages[0].at[c], dst_ref=outs[0].at[chip, c],
                                              send_sem=ici_send.at[k], recv_sem=ici_recv.at[k],
                                              device_id=(x ^ kx, y ^ ky, c), device_id_type=MESH)
            cp.start()
            sends.append(cp)

        def to_all(src, dst_of, send_sems, recv_sems):
            for k in range(1, NDEV):
                kx, ky, kc = (k >> 2) & 1, (k >> 1) & 1, k & 1
                cp = pltpu.make_async_remote_copy(src_ref=src, dst_ref=dst_of(me), send_sem=send_sems.at[k - 1],
                                                  recv_sem=recv_sems.at[k - 1], device_id=(x ^ kx, y ^ ky, c ^ kc),
                                                  device_id_type=MESH)
                cp.start()
                sends.append(cp)
            for k in range(1, NDEV):
                kx, ky, kc = (k >> 2) & 1, (k >> 1) & 1, k & 1
                frm = 4 * (x ^ kx) + 2 * (y ^ ky) + (c ^ kc)
                pltpu.make_async_remote_copy(src_ref=src, dst_ref=dst_of(frm), send_sem=send_sems.at[k - 1],
                                             recv_sem=recv_sems.at[k - 1], device_id=(x ^ kx, y ^ ky, c ^ kc),
                                             device_id_type=MESH).wait_recv()

        call_ref[me] = c_ref[...]
        to_all(c_ref, lambda d: call_ref.at[d], c_send, c_recv)
        ct[...] = jnp.zeros_like(ct)
        for d in range(NDEV):
            ct[d:d + 1, :] = call_ref[d]
        ct[NDEV:NDEV + 1, :] = cc_ref[...]
        loads[0].wait()
        prod_ref[me] = jnp.dot(_silu(ct[...]), loaded[0][...], precision=HI, preferred_element_type=F32)
        to_all(prod_ref.at[me], lambda d: prod_ref.at[d], p_send, p_recv)
        for t in range(1, nt):
            loads[t].wait()
            stage(t, loaded[t])

        for k, (kx, ky) in enumerate(_chip_offsets()):
            frm = 2 * (x ^ kx) + (y ^ ky)
            pltpu.make_async_remote_copy(src_ref=stages[0].at[c], dst_ref=outs[0].at[frm, c],
                                         send_sem=ici_send.at[k], recv_sem=ici_recv.at[k],
                                         device_id=(x ^ kx, y ^ ky, c), device_id_type=MESH).wait_recv()
            cp = pltpu.make_async_remote_copy(src_ref=outs[0].at[frm, c], dst_ref=outs[0].at[frm, c],
                                              send_sem=d2d_send.at[k], recv_sem=d2d_recv.at[k],
                                              device_id=sib, device_id_type=MESH)
            cp.start()
            sends.append(cp)
        for k, (kx, ky) in enumerate(_chip_offsets()):
            frm = 2 * (x ^ kx) + (y ^ ky)
            pltpu.make_async_remote_copy(src_ref=stages[0].at[c], dst_ref=outs[0].at[frm, 1 - c],
                                         send_sem=d2d_send.at[k], recv_sem=d2d_recv.at[k],
                                         device_id=sib, device_id_type=MESH).wait_recv()
        for cp in sends:
            cp.wait_send()
        for cp in pending:
            cp.wait()

    vm = pl.BlockSpec(memory_space=pltpu.VMEM)
    hbm = pl.BlockSpec(memory_space=pl.ANY)
    out = pl.pallas_call(
        body, name="prologue",
        out_shape=[jax.ShapeDtypeStruct((NCHIP, 2, r // 2, cc), BF) for r, cc in shapes]
        + [jax.ShapeDtypeStruct((NDEV, 1, D), F32), jax.ShapeDtypeStruct((NDEV, 16, mod_c), F32)],
        in_specs=[vm, vm, hbm, vm] + [hbm] * (nt - 1),
        out_specs=[hbm] * nt + [vm, vm],
        scratch_shapes=[pltpu.VMEM((2, r // 2, cc), BF) for r, cc in shapes] + [pltpu.VMEM((16, D), F32)]
        + [pltpu.VMEM(w_mod_s.shape, F32)] + [pltpu.VMEM(s, F32) for s in shapes[1:]]
        + [pltpu.SemaphoreType.DMA((NDEV - 1,))] * 4 + [pltpu.SemaphoreType.DMA((NCHIP - 1,))] * 4
        + [pltpu.SemaphoreType.DMA((nt,))] * 2,
        compiler_params=_params(56),
    )(c, c_ctx, w_mod_s, *shards)
    return out[:nt], out[nt], out[nt + 1]


def _gather_ici_copies(bufs, send_sems, recv_sems):
    x, y, c = _pos()
    chip = 2 * x + y
    nt = len(bufs)
    out_cp, in_cp = [], []
    for k, (kx, ky) in enumerate(_chip_offsets()):
        frm = 2 * (x ^ kx) + (y ^ ky)
        for t in range(nt):
            s = k * nt + t
            peer = (x ^ kx, y ^ ky, c)
            out_cp.append(pltpu.make_async_remote_copy(
                src_ref=bufs[t].at[chip, c], dst_ref=bufs[t].at[chip, c], send_sem=send_sems.at[s],
                recv_sem=recv_sems.at[s], device_id=peer, device_id_type=MESH))
            in_cp.append(pltpu.make_async_remote_copy(
                src_ref=bufs[t].at[chip, c], dst_ref=bufs[t].at[frm, c], send_sem=send_sems.at[s],
                recv_sem=recv_sems.at[s], device_id=peer, device_id_type=MESH))
    return out_cp, in_cp


def _gather_d2d_copies(bufs, send_sems, recv_sems):
    x, y, c = _pos()
    nt = len(bufs)
    out_cp, in_cp = [], []
    for k, (kx, ky) in enumerate(_chip_offsets()):
        frm = 2 * (x ^ kx) + (y ^ ky)
        for t in range(nt):
            s = k * nt + t
            out_cp.append(pltpu.make_async_remote_copy(
                src_ref=bufs[t].at[frm, c], dst_ref=bufs[t].at[frm, c], send_sem=send_sems.at[s],
                recv_sem=recv_sems.at[s], device_id=(x, y, 1 - c), device_id_type=MESH))
            in_cp.append(pltpu.make_async_remote_copy(
                src_ref=bufs[t].at[frm, c], dst_ref=bufs[t].at[frm, 1 - c], send_sem=send_sems.at[s],
                recv_sem=recv_sems.at[s], device_id=(x, y, 1 - c), device_id_type=MESH))
    return out_cp, in_cp


def _scatter_ici_copies(parts, outs, send_sems, recv_sems):
    x, y, c = _pos()
    nt = len(parts)
    cps = []
    for k, (kx, ky) in enumerate(_chip_offsets()):
        dst_chip = 2 * (x ^ kx) + (y ^ ky)
        for t in range(nt):
            s = k * nt + t
            cps.append(pltpu.make_async_remote_copy(
                src_ref=parts[t].at[dst_chip], dst_ref=outs[t].at[k], send_sem=send_sems.at[s],
                recv_sem=recv_sems.at[s], device_id=(x ^ kx, y ^ ky, c), device_id_type=MESH))
    return cps


def _rs_exchange_halves(grads, name):
    nt = len(grads)
    shapes = [g.shape for g in grads]

    def body(*refs):
        gs, outs = refs[:nt], refs[nt:2 * nt]
        send_sems, recv_sems = refs[2 * nt:]
        x, y, c = _pos()
        sib = (x, y, 1 - c)
        sent = []
        for t in range(nt):
            for j in range(NCHIP):
                s = t * NCHIP + j
                cp = pltpu.make_async_remote_copy(src_ref=gs[t].at[j, 1 - c], dst_ref=outs[t].at[j],
                                                  send_sem=send_sems.at[s], recv_sem=recv_sems.at[s],
                                                  device_id=sib, device_id_type=MESH)
                cp.start()
                sent.append(cp)
        for cp in sent:
            cp.wait_recv()
        for cp in sent:
            cp.wait_send()

    return pl.pallas_call(
        body, name=name,
        out_shape=[jax.ShapeDtypeStruct((NCHIP, s[2], s[3]), F32) for s in shapes],
        in_specs=[pl.BlockSpec(memory_space=pl.ANY)] * nt,
        out_specs=[pl.BlockSpec(memory_space=pl.ANY)] * nt,
        scratch_shapes=[pltpu.SemaphoreType.DMA((nt * NCHIP,))] * 2,
    )(*grads)


def _rs_share_final(finals):
    nt = len(finals)
    shapes = [f.shape for f in finals]

    def body(*refs):
        fs, outs = refs[:nt], refs[nt:2 * nt]
        send_sems, recv_sems = refs[2 * nt:]
        x, y, c = _pos()
        sent = []
        for t in range(nt):
            cp = pltpu.make_async_remote_copy(src_ref=fs[t], dst_ref=outs[t], send_sem=send_sems.at[t],
                                              recv_sem=recv_sems.at[t], device_id=(x, y, 1 - c), device_id_type=MESH)
            cp.start()
            sent.append(cp)
        for cp in sent:
            cp.wait_recv()
        for cp in sent:
            cp.wait_send()

    return pl.pallas_call(
        body, name="rs_share_final",
        out_shape=[jax.ShapeDtypeStruct(s, F32) for s in shapes],
        in_specs=[pl.BlockSpec(memory_space=pl.ANY)] * nt,
        out_specs=[pl.BlockSpec(memory_space=pl.ANY)] * nt,
        scratch_shapes=[pltpu.SemaphoreType.DMA((nt,))] * 2,
    )(*finals)


def _row_tile(h, cc=D):
    for t in (512, 384, 352, 256, 176, 128, 64, 32, 16):
        if h % t == 0 and t * cc * 4 <= (5 * VMEM_MB) // 4:
            return t
    return h


def _rs_add_halves(g, recv, where, name):
    _, _, h, cc = g.shape
    th = _row_tile(h, cc)

    def body(w_ref, g_ref, r_ref, own_ref, ob_ref):
        s = g_ref[...] + r_ref[...]
        ob_ref[...] = s.astype(BF)

        @pl.when(pl.program_id(1) == w_ref[1])
        def _():
            own_ref[...] = s

    return pl.pallas_call(
        body, name=name,
        grid_spec=pltpu.PrefetchScalarGridSpec(
            num_scalar_prefetch=1, grid=(h // th, NCHIP),
            in_specs=[pl.BlockSpec((None, None, th, cc), lambda i, j, w_ref: (j, w_ref[0], i, 0)),
                      pl.BlockSpec((None, th, cc), lambda i, j, w_ref: (j, i, 0))],
            out_specs=[pl.BlockSpec((th, cc), lambda i, j, w_ref: (i, 0)),
                       pl.BlockSpec((None, th, cc), lambda i, j, w_ref: (j, i, 0))]),
        out_shape=[jax.ShapeDtypeStruct((h, cc), F32), jax.ShapeDtypeStruct((NCHIP, h, cc), BF)],
        compiler_params=_params(48, ("arbitrary", "arbitrary")),
    )(where, g, recv)


def _rs_add_chips(own, recv, name):
    h, cc = own.shape
    th = _row_tile(h, cc)

    def body(o_ref, r_ref, out_ref):
        out_ref[...] = ((o_ref[...] + r_ref[0].astype(F32)) + r_ref[1].astype(F32)) + r_ref[2].astype(F32)

    return pl.pallas_call(
        body, name=name, grid=(h // th,),
        in_specs=[pl.BlockSpec((th, cc), lambda i: (i, 0)), pl.BlockSpec((NCHIP - 1, th, cc), lambda i: (0, i, 0))],
        out_specs=pl.BlockSpec((th, cc), lambda i: (i, 0)),
        out_shape=jax.ShapeDtypeStruct((h, cc), F32),
        compiler_params=_params(48, ("parallel",)),
    )(own, recv)


def _adamw_math(w, g, m, v):
    m2 = B1 * m + (1.0 - B1) * g
    v2 = B2 * v + (1.0 - B2) * (g * g)
    m_hat = m2 / (1.0 - B1 ** STEP)
    v_hat = v2 / (1.0 - B2 ** STEP)
    delta = -LR * (m_hat / (jnp.sqrt(v_hat) + AEPS) + WD * w)
    return delta, m2, v2


def _adamw(w, g, m, v, name):
    r, cc = w.shape
    tr = _row_tile(r, cc)

    def body(w_ref, g_ref, m_ref, v_ref, d_ref, mo_ref, vo_ref):
        d, m2, v2 = _adamw_math(w_ref[...], g_ref[...], m_ref[...], v_ref[...])
        d_ref[...] = d
        mo_ref[...] = m2
        vo_ref[...] = v2

    spec = pl.BlockSpec((tr, cc), lambda i: (i, 0))
    return pl.pallas_call(
        body, name=name, grid=(r // tr,), in_specs=[spec] * 4, out_specs=[spec] * 3,
        out_shape=[jax.ShapeDtypeStruct((r, cc), F32)] * 3,
        compiler_params=_params(48, ("parallel",)),
    )(w, g, m, v)


def _adamw_halves(w, own, other, m, v, cidx, name):
    r, cc = w.shape
    h = r // 2
    tr = _row_tile(h, cc)
    per = h // tr

    def body(c_ref, w_ref, own_ref, oth_ref, m_ref, v_ref, g_ref, d_ref, mo_ref, vo_ref):
        mine = (pl.program_id(0) // per) == c_ref[0]
        g = jnp.where(mine, own_ref[...], oth_ref[...])
        g_ref[...] = g
        d, m2, v2 = _adamw_math(w_ref[...], g, m_ref[...], v_ref[...])
        d_ref[...] = d
        mo_ref[...] = m2
        vo_ref[...] = v2

    full = pl.BlockSpec((tr, cc), lambda i, c_ref: (i, 0))
    half = pl.BlockSpec((tr, cc), lambda i, c_ref: (i % per, 0))
    return pl.pallas_call(
        body, name=name,
        grid_spec=pltpu.PrefetchScalarGridSpec(
            num_scalar_prefetch=1, grid=(r // tr,),
            in_specs=[full, half, half, full, full], out_specs=[full] * 4),
        out_shape=[jax.ShapeDtypeStruct((r, cc), F32)] * 4,
        compiler_params=_params(48, ("parallel",)),
    )(cidx, w, own, other, m, v)


def _decay_exponents():
    ri = lax.broadcasted_iota(jnp.int32, (CH, CH), 0).astype(F32)
    ci = lax.broadcasted_iota(jnp.int32, (CH, CH), 1).astype(F32)
    full = jnp.full((CH, CH), float(CH), F32)
    return [[ri - ci, ri + 1.0, (CH - 1.0) - ri, full], [ci - ri, CH - ri, ri, full]]


def _decay_mats(logit_full):
    def body(l_ref, o_ref):
        ex = _decay_exponents()
        for d in range(2):
            for h in range(NH):
                lv = l_ref[d * NH + h]
                lg = jnp.minimum(lv, 0.0) - jnp.log(1.0 + jnp.exp(-jnp.abs(lv)))
                for kind in range(4):
                    m = jnp.exp(lg * ex[d][kind])
                    if kind == 0:
                        m = jnp.where(ex[d][0] >= 0.0, jnp.exp(lg * jnp.maximum(ex[d][0], 0.0)), 0.0)
                    o_ref[d, kind, h] = m

    return pl.pallas_call(
        body, name="decay_mats",
        out_shape=jax.ShapeDtypeStruct((2, 4, NH, CH, CH), F32),
        in_specs=[pl.BlockSpec(memory_space=pltpu.VMEM)],
        out_specs=pl.BlockSpec(memory_space=pltpu.VMEM),
        compiler_params=_params(32),
    )(logit_full)


def _ctx_kv_weights(wi_ref):
    def cols(g):
        return wi_ref[g // WI_C, :, g % WI_C: g % WI_C + HD].astype(F32)

    wk = [cols(3 * AW + h * HD) for h in range(NH)]
    wv = [cols(4 * AW + h * HD) for h in range(NH)]
    return wk, wv


def _ctx_forward(ctx, vecs, wi, dm):
    def body(ctx_ref, v_ref, wi_ref, dm_ref, scf_ref, scb_ref):
        wk, wv = _ctx_kv_weights(wi_ref)
        mats = [[dm_ref[d, kind, h] for h in range(NH)] for d in range(2) for kind in (2, 3)]
        scf, scb = _ctx_states(ctx_ref[0:CH, :], ctx_ref[CH:2 * CH, :], v_ref[0:1, :], v_ref[1:2, :],
                               v_ref[2:3, :], wk, wv, mats[0], mats[2], mats[1], mats[3])
        for h in range(NH):
            scf_ref[h] = scf[h]
            scb_ref[h] = scb[h]

    return pl.pallas_call(
        body, name="ctx_forward",
        out_shape=[jax.ShapeDtypeStruct((NH, HD, HD), F32)] * 2,
        in_specs=[pl.BlockSpec(memory_space=pltpu.VMEM)] * 4,
        out_specs=[pl.BlockSpec(memory_space=pltpu.VMEM)] * 2,
        compiler_params=_params(48),
    )(ctx, vecs, wi, dm)


def _ctx_backward(ctx, vecs, wi, dm, dscf, dscb):
    def body(ctx_ref, v_ref, wi_ref, dm_ref, gf_ref, gb_ref, gw_ref, gv_ref, gdm_ref):
        wk, wv = _ctx_kv_weights(wi_ref)
        mats = [[dm_ref[d, kind, h] for h in range(NH)] for d in range(2) for kind in (2, 3)]
        ctx0, ctx1 = ctx_ref[0:CH, :], ctx_ref[CH:2 * CH, :]

        def fn(n1, csh, csc, wk_, wv_, zf, zb, ef, eb):
            return _ctx_states(ctx0, ctx1, n1, csh, csc, wk_, wv_, zf, zb, ef, eb)

        _, vjp = jax.vjp(fn, v_ref[0:1, :], v_ref[1:2, :], v_ref[2:3, :], wk, wv,
                         mats[0], mats[2], mats[1], mats[3])
        cot = ([gf_ref[h] for h in range(NH)], [gb_ref[h] for h in range(NH)])
        dn1, dcsh, dcsc, dwk, dwv, dzf, dzb, def_, deb = vjp(cot)
        for h in range(NH):
            gw_ref[:, h * HD:(h + 1) * HD] = dwk[h]
            gw_ref[:, AW + h * HD:AW + (h + 1) * HD] = dwv[h]
        gv_ref[...] = jnp.zeros_like(gv_ref)
        gv_ref[0:1, :] = dn1
        gv_ref[1:2, :] = dcsh
        gv_ref[2:3, :] = dcsc
        for h in range(NH):
            gdm_ref[0, 0, h] = dzf[h]
            gdm_ref[0, 1, h] = def_[h]
            gdm_ref[1, 0, h] = dzb[h]
            gdm_ref[1, 1, h] = deb[h]

    return pl.pallas_call(
        body, name="ctx_backward",
        out_shape=[jax.ShapeDtypeStruct((D, 2 * AW), F32), jax.ShapeDtypeStruct((8, D), F32),
                   jax.ShapeDtypeStruct((2, 2, NH, CH, CH), F32)],
        in_specs=[pl.BlockSpec(memory_space=pltpu.VMEM)] * 6,
        out_specs=[pl.BlockSpec(memory_space=pltpu.VMEM)] * 3,
        compiler_params=_params(56),
    )(ctx, vecs, wi, dm, dscf, dscb)


def _load_w_in(wi_hbm, wcat, sems):
    cps = [pltpu.make_async_copy(wi_hbm.at[j], wcat.at[:, pl.ds(j * WI_C, WI_C)], sems.at[j]) for j in range(NCHIP)]
    for cp in cps:
        cp.start()
    for cp in cps:
        cp.wait()


def _in_proj(x, vecs, wi, gbufs):
    ln = x.shape[0]
    t = min(1024, ln)
    nt = len(gbufs)
    steps = ln // t

    def body(x_ref, v_ref, wi_ref, *refs):
        z_ref, hx_ref = refs[nt:nt + 2]
        bufs = refs[nt + 2:2 * nt + 2]
        wcat, w_sems, send_sems, recv_sems = refs[2 * nt + 2:]
        i = pl.program_id(0)

        @pl.when(i == 0)
        def _():
            for cp in _gather_ici_copies(bufs, send_sems, recv_sems)[0]:
                cp.start()
            _load_w_in(wi_ref, wcat, w_sems)

        xv = x_ref[...]
        hx = (xv * _rms(xv) * v_ref[0:1, :]) * (1.0 + v_ref[2:3, :]) + v_ref[1:2, :]
        hb = hx.astype(BF)
        hx_ref[...] = hb
        z_ref[...] = _dot(hb, wcat[...], NN)

        @pl.when(i == steps - 1)
        def _():
            out_cp, in_cp = _gather_ici_copies(bufs, send_sems, recv_sems)
            for cp in in_cp:
                cp.wait_recv()
            for cp in out_cp:
                cp.wait_send()

    hbm = pl.BlockSpec(memory_space=pl.ANY)
    out = pl.pallas_call(
        body, name="in_proj", grid=(steps,),
        in_specs=[pl.BlockSpec((t, D), lambda i: (i, 0)), _const((8, D)), hbm] + [hbm] * nt,
        out_specs=[pl.BlockSpec((t, IN_COLS), lambda i: (i, 0)), pl.BlockSpec((t, D), lambda i: (i, 0))] + [hbm] * nt,
        out_shape=[jax.ShapeDtypeStruct((ln, IN_COLS), F32), jax.ShapeDtypeStruct((ln, D), BF)]
        + [jax.ShapeDtypeStruct(g.shape, g.dtype) for g in gbufs],
        input_output_aliases={3 + k: 2 + k for k in range(nt)},
        scratch_shapes=[pltpu.VMEM((D, IN_COLS), BF), pltpu.SemaphoreType.DMA((NCHIP,))]
        + [pltpu.SemaphoreType.DMA(((NCHIP - 1) * nt,))] * 2,
        compiler_params=_params(56, ("arbitrary",)),
    )(x, vecs, wi, *gbufs)
    return out[0], out[1], out[2:]


def _allgather_copies(src, out, send_sems, recv_sems, local_sem):
    x, y, c = _pos()
    me = 4 * x + 2 * y + c
    sends, recvs = [], []
    for k in range(1, NDEV):
        kx, ky, kc = (k >> 2) & 1, (k >> 1) & 1, k & 1
        peer = (x ^ kx, y ^ ky, c ^ kc)
        frm = 4 * (x ^ kx) + 2 * (y ^ ky) + (c ^ kc)
        sends.append(pltpu.make_async_remote_copy(src_ref=src, dst_ref=out.at[me], send_sem=send_sems.at[k - 1],
                                                  recv_sem=recv_sems.at[k - 1], device_id=peer, device_id_type=MESH))
        recvs.append(pltpu.make_async_remote_copy(src_ref=src, dst_ref=out.at[frm], send_sem=send_sems.at[k - 1],
                                                  recv_sem=recv_sems.at[k - 1], device_id=peer, device_id_type=MESH))
    return sends, recvs, pltpu.make_async_copy(src, out.at[me], local_sem)


def _in_proj_bwd(dz, x, dx1, vecs, wi, parts, early):
    ln = x.shape[0]
    t = min(1024, ln)
    nt = len(parts)
    steps = ln // t

    def body(dz_ref, x_ref, dx1_ref, v_ref, wi_ref, *refs):
        ps = refs[:nt]
        early_ref = refs[nt]
        gx_ref, acc_ref = refs[nt + 1:nt + 3]
        got = refs[nt + 3:2 * nt + 3]
        early_all = refs[2 * nt + 3]
        wcat, w_sems, send_sems, recv_sems, ag_send, ag_recv, ag_local = refs[2 * nt + 4:]

        @pl.when(pl.program_id(0) == 0)
        def _():
            acc_ref[...] = jnp.zeros_like(acc_ref)
            _load_w_in(wi_ref, wcat, w_sems)
            for cp in _scatter_ici_copies(ps, got, send_sems, recv_sems):
                cp.start()
            sends, _, own = _allgather_copies(early_ref, early_all, ag_send, ag_recv, ag_local)
            own.start()
            for cp in sends:
                cp.start()

        dhx = _dot(dz_ref[...], wcat[...], NT)
        xv = x_ref[...]
        r = _rms(xv)
        xn = xv * r
        n1, sc = v_ref[0:1, :], v_ref[2:3, :]
        acc_ref[0:1, :] += jnp.sum(dhx * xn * (1.0 + sc), axis=0, keepdims=True)
        acc_ref[1:2, :] += jnp.sum(dhx, axis=0, keepdims=True)
        acc_ref[2:3, :] += jnp.sum(dhx * xn * n1, axis=0, keepdims=True)
        g = dhx * n1 * (1.0 + sc)
        gx_ref[...] = dx1_ref[...] + r * (g - xn * jnp.mean(g * xn, axis=-1, keepdims=True))

        @pl.when(pl.program_id(0) == steps - 1)
        def _():
            cps = _scatter_ici_copies(ps, got, send_sems, recv_sems)
            sends, recvs, own = _allgather_copies(early_ref, early_all, ag_send, ag_recv, ag_local)
            for cp in cps + recvs:
                cp.wait_recv()
            for cp in cps + sends:
                cp.wait_send()
            own.wait()

    hbm = pl.BlockSpec(memory_space=pl.ANY)
    out = pl.pallas_call(
        body, name="in_proj_bwd", grid=(steps,),
        in_specs=[pl.BlockSpec((t, IN_COLS), lambda i: (i, 0)), pl.BlockSpec((t, D), lambda i: (i, 0)),
                  pl.BlockSpec((t, D), lambda i: (i, 0)), _const((8, D)), hbm]
        + [hbm] * (nt + 1),
        out_specs=[pl.BlockSpec((t, D), lambda i: (i, 0)), pl.BlockSpec((8, D), lambda i: (0, 0))]
        + [hbm] * (nt + 1),
        out_shape=[jax.ShapeDtypeStruct((ln, D), F32), jax.ShapeDtypeStruct((8, D), F32)]
        + [jax.ShapeDtypeStruct((NCHIP - 1,) + p.shape[1:], BF) for p in parts]
        + [jax.ShapeDtypeStruct((NDEV,) + early.shape, F32)],
        scratch_shapes=[pltpu.VMEM((D, IN_COLS), BF), pltpu.SemaphoreType.DMA((NCHIP,))]
        + [pltpu.SemaphoreType.DMA(((NCHIP - 1) * nt,))] * 2
        + [pltpu.SemaphoreType.DMA((NDEV - 1,))] * 2 + [pltpu.SemaphoreType.DMA],
        compiler_params=_params(56, ("arbitrary",)),
    )(dz, x, dx1, vecs, wi, *parts, early)
    return out[0], out[1], out[2:2 + nt], out[2 + nt]


def _post_mixer(x, ycat, tgt, vecs, wo, wg, wu, wd):
    ln = x.shape[0]
    t = min(256, ln)

    def body(x_ref, y_ref, t_ref, v_ref, wo_ref, wg_ref, wu_ref, wd_ref,
             dx1_ref, dyc_ref, h2_ref, dy_ref, df_ref, act_ref, da_ref, db_ref, acc_ref, a_st, b_st):
        @pl.when(pl.program_id(0) == 0)
        def _():
            acc_ref[...] = jnp.zeros_like(acc_ref)

        g1, n2, sh2, sc2 = v_ref[0:1, :], v_ref[1:2, :], v_ref[2:3, :], v_ref[3:4, :]
        g2, nf = v_ref[4:5, :], v_ref[5:6, :]
        y = _dot(y_ref[...], wo_ref[...], NN)
        x1 = x_ref[...] + g1 * y
        r2 = _rms(x1)
        xn2 = x1 * r2
        t2 = xn2 * n2
        h2b = (t2 * (1.0 + sc2) + sh2).astype(BF)
        h2_ref[...] = h2b
        a = _dot(h2b, wg_ref[...], NT)
        b = _dot(h2b, wu_ref[...], NT)
        a_st[...] = a
        b_st[...] = b
        act = (_silu(a) * b).astype(BF)
        act_ref[...] = act
        f = _dot(act, wd_ref[...], NN)
        x2 = x1 + g2 * f
        r3 = _rms(x2)
        xn3 = x2 * r3
        e = xn3 * nf - t_ref[...]
        acc_ref[6:7, :] += jnp.sum(e * e, axis=0, keepdims=True) * (0.5 / D)
        dout = e * (1.0 / D)
        acc_ref[5:6, :] += jnp.sum(dout * xn3, axis=0, keepdims=True)
        gg = dout * nf
        dx2 = r3 * (gg - xn3 * jnp.mean(gg * xn3, axis=-1, keepdims=True))
        acc_ref[4:5, :] += jnp.sum(dx2 * f, axis=0, keepdims=True)
        dfb = (g2 * dx2).astype(BF)
        df_ref[...] = dfb
        dact = _dot(dfb, wd_ref[...], NT)
        a = a_st[...]
        b = b_st[...]
        s = jax.nn.sigmoid(a)
        da = (dact * b * (s * (1.0 + a * (1.0 - s)))).astype(BF)
        db = (dact * (a * s)).astype(BF)
        da_ref[...] = da
        db_ref[...] = db
        dh2 = _dot(da, wg_ref[...], NN) + _dot(db, wu_ref[...], NN)
        acc_ref[2:3, :] += jnp.sum(dh2, axis=0, keepdims=True)
        acc_ref[3:4, :] += jnp.sum(dh2 * t2, axis=0, keepdims=True)
        acc_ref[1:2, :] += jnp.sum(dh2 * xn2 * (1.0 + sc2), axis=0, keepdims=True)
        gx = dh2 * n2 * (1.0 + sc2)
        dx1 = dx2 + r2 * (gx - xn2 * jnp.mean(gx * xn2, axis=-1, keepdims=True))
        dx1_ref[...] = dx1
        acc_ref[0:1, :] += jnp.sum(dx1 * y, axis=0, keepdims=True)
        dyb = (g1 * dx1).astype(BF)
        dy_ref[...] = dyb
        dyc_ref[...] = _dot(dyb, wo_ref[...], NT)

    tok = pl.BlockSpec((t, D), lambda i: (i, 0))
    ffb = pl.BlockSpec((t, DFF), lambda i: (i, 0))
    return pl.pallas_call(
        body, name="post_mixer", grid=(ln // t,),
        in_specs=[tok, tok, tok, _const((8, D)), _const((D, D)), _const((DFF, D)), _const((DFF, D)),
                  _const((DFF, D))],
        out_specs=[tok, tok, tok, tok, tok, ffb, ffb, ffb, pl.BlockSpec((16, D), lambda i: (0, 0))],
        out_shape=[jax.ShapeDtypeStruct((ln, D), F32)] * 2 + [jax.ShapeDtypeStruct((ln, D), BF)] * 3
        + [jax.ShapeDtypeStruct((ln, DFF), BF)] * 3 + [jax.ShapeDtypeStruct((16, D), F32)],
        scratch_shapes=[pltpu.VMEM((t, DFF), F32)] * 2,
        compiler_params=_params(60, ("arbitrary",)),
    )(x, ycat, tgt, vecs, wo, wg, wu, wd)


def _exchange_copies(g, out, send_sems, recv_sems):
    x, y, c = _pos()
    return [pltpu.make_async_remote_copy(src_ref=g.at[j, 1 - c], dst_ref=out.at[j], send_sem=send_sems.at[j],
                                         recv_sem=recv_sems.at[j], device_id=(x, y, 1 - c), device_id_type=MESH)
            for j in range(NCHIP)]


def _tn_matmul(xa, dy, name, nb, k1, n, x_batched, dy_mode, tt, ctx_kv=None, carry=None):
    ln = xa.shape[-2]
    tt = min(tt, ln)
    steps = ln // tt
    n_in = 2 + (ctx_kv is not None) + (carry is not None)

    def body(x_ref, dy_ref, *refs):
        o_ref = refs[n_in - 2]
        if carry is not None:
            g_ref, got_ref = refs[n_in - 3], refs[n_in - 1]
            send_sems, recv_sems = refs[n_in:]

        @pl.when(pl.program_id(0) == 0)
        def _():
            if carry is not None:
                for cp in _exchange_copies(g_ref, got_ref, send_sems, recv_sems):
                    cp.start()
            o_ref[...] = jnp.zeros_like(o_ref)
            if ctx_kv is not None:
                for g in range(0, 2 * AW, HD):
                    col = 3 * AW + g
                    o_ref[col // n, :, col % n: col % n + HD] = refs[0][:, g:g + HD]

        xt = None if x_batched else jnp.transpose(x_ref[...])
        for b in range(nb):
            lhs = jnp.transpose(x_ref[b]) if x_batched else xt
            if dy_mode == "batched":
                rhs = dy_ref[b]
            elif dy_mode == "cols":
                rhs = dy_ref[:, b * n:(b + 1) * n]
            else:
                rhs = dy_ref[...]
            o_ref[b] += _dot(lhs, rhs, NN)

        if carry is not None:
            @pl.when(pl.program_id(0) == steps - 1)
            def _():
                cps = _exchange_copies(g_ref, got_ref, send_sems, recv_sems)
                for cp in cps:
                    cp.wait_recv()
                for cp in cps:
                    cp.wait_send()

    x_spec = (pl.BlockSpec((nb, tt, k1), lambda t: (0, t, 0)) if x_batched
              else pl.BlockSpec((tt, k1), lambda t: (t, 0)))
    if dy_mode == "batched":
        dy_spec = pl.BlockSpec((nb, tt, n), lambda t: (0, t, 0))
    elif dy_mode == "cols":
        dy_spec = pl.BlockSpec((tt, nb * n), lambda t: (t, 0))
    else:
        dy_spec = pl.BlockSpec((tt, n), lambda t: (t, 0))
    hbm = pl.BlockSpec(memory_space=pl.ANY)
    extra = [] if ctx_kv is None else [ctx_kv]
    in_specs = [x_spec, dy_spec] + [_const(e.shape) for e in extra]
    out_specs = [pl.BlockSpec((nb, k1, n), lambda t: (0, 0, 0))]
    out_shape = [jax.ShapeDtypeStruct((nb, k1, n), F32)]
    scratch = []
    if carry is not None:
        extra = extra + [carry]
        in_specs.append(hbm)
        out_specs.append(hbm)
        out_shape.append(jax.ShapeDtypeStruct((NCHIP,) + carry.shape[2:], F32))
        scratch = [pltpu.SemaphoreType.DMA((NCHIP,))] * 2
    out = pl.pallas_call(
        body, name=name, grid=(steps,),
        in_specs=in_specs, out_specs=out_specs, out_shape=out_shape, scratch_shapes=scratch,
        compiler_params=_params(60, ("arbitrary",)),
    )(xa, dy, *extra)
    return out[0] if carry is None else (out[0], out[1])


FWD_CHUNKS_PER_STEP = 4
BWD_CHUNKS_PER_STEP = 4


def _chunks_per_step(nc, want):
    return want if nc % want == 0 else 1


def _mixer_fwd(z, cos_t, sin_t, dm, sgw, gain, bfull, scf, scb, gbufs_a, gbufs_b):
    ln = z.shape[0]
    nc = ln // CH
    na = len(gbufs_a)
    gbufs = list(gbufs_a) + list(gbufs_b)
    nt = len(gbufs)
    cps = _chunks_per_step(nc, FWD_CHUNKS_PER_STEP)
    nb = nc // cps
    rows = cps * CH
    mid = nb // 2

    def rev(p, n):
        return p * n + (1 - p) * (nb - 1 - n)

    def col(j, both):
        if both:
            return pl.BlockSpec((rows, AW), lambda p, n: (rev(p, n), j))
        return pl.BlockSpec((rows, AW), lambda p, n: (p * n, j))

    def body(u_ref, v_ref, q_ref, k_ref, vr_ref, gf_ref, gb_ref, cos_ref, sin_ref, dm_ref, sgw_ref, gain_ref,
             bfull_ref, scf_ref, scb_ref, *refs):
        y_ref, sf_ref, sb_ref, of_ref = refs[nt:nt + 4]
        bufs = refs[nt + 4:2 * nt + 4]
        bufs_a, bufs_b = bufs[:na], bufs[na:]
        sb_all, st, a_send, a_recv, bi_send, bi_recv, bd_send, bd_recv = refs[2 * nt + 4:]
        p, n = pl.program_id(0), pl.program_id(1)

        @pl.when((p == 0) & (n == 0))
        def _():
            for cp in _gather_d2d_copies(bufs_a, a_send, a_recv)[0]:
                cp.start()
            for cp in _gather_ici_copies(bufs_b, bi_send, bi_recv)[0]:
                cp.start()

        @pl.when((p == 1) & (n == mid))
        def _():
            for cp in _gather_ici_copies(bufs_b, bi_send, bi_recv)[1]:
                cp.wait_recv()
            for cp in _gather_d2d_copies(bufs_b, bd_send, bd_recv)[0]:
                cp.start()

        def roped_k(r0):
            cos, sin = cos_ref[r0:r0 + CH, :], sin_ref[r0:r0 + CH, :]
            return [_rope(t, cos, sin) * K_SCALE for t in _heads(k_ref, r0)]

        @pl.when(p == 0)
        def _():
            @pl.when(n == 0)
            def _():
                st[...] = scb_ref[...]

            for s in reversed(range(cps)):
                m = (nb - 1 - n) * cps + s
                k, vr = roped_k(s * CH), _heads(vr_ref, s * CH)
                for h in range(NH):
                    sb_all[m, h] = st[h]
                    st[h] = dm_ref[1, 3, h] * st[h] + _mm_tn(k[h], dm_ref[1, 2, h] * vr[h])

        @pl.when(p == 1)
        def _():
            @pl.when(n == 0)
            def _():
                st[...] = scf_ref[...]

            mats = [[dm_ref[d, kind, h] for h in range(NH)] for d in range(2) for kind in range(3)]
            for s in range(cps):
                r0 = s * CH
                m = n * cps + s
                cos, sin = cos_ref[r0:r0 + CH, :], sin_ref[r0:r0 + CH, :]
                q = [_rope(t, cos, sin) for t in _heads(q_ref, r0)]
                k, vr = roped_k(r0), _heads(vr_ref, r0)
                u, v, gf, gb = _heads(u_ref, r0), _heads(v_ref, r0), _heads(gf_ref, r0), _heads(gb_ref, r0)
                cols = [slice(h * HD, (h + 1) * HD) for h in range(NH)]
                ya = [_gate_group(u[h], v[h], sgw_ref[h], gain_ref[:, cols[h]], bfull_ref[h]) for h in range(NH)]
                sf = [st[h] for h in range(NH)]
                sb = [sb_all[m, h] for h in range(NH)]
                ret = [_ret_head(q[h], k[h], vr[h], gf[h], gb[h], sf[h], sb[h], mats[0][h], mats[1][h], mats[2][h],
                                 mats[3][h], mats[4][h], mats[5][h]) for h in range(NH)]
                for h in range(NH):
                    yr, uf, _, of = ret[h]
                    y_ref[r0:r0 + CH, cols[h]] = ya[h].astype(BF)
                    y_ref[r0:r0 + CH, AW + h * HD:AW + (h + 1) * HD] = yr.astype(BF)
                    of_ref[r0:r0 + CH, cols[h]] = of
                    sf_ref[s, h] = sf[h]
                    sb_ref[s, h] = sb[h]
                    st[h] = dm_ref[0, 3, h] * sf[h] + uf

        @pl.when((p == 1) & (n == nb - 1))
        def _():
            a_out, a_in = _gather_d2d_copies(bufs_a, a_send, a_recv)
            b_out, b_in = _gather_d2d_copies(bufs_b, bd_send, bd_recv)
            for cp in a_in + b_in:
                cp.wait_recv()
            for cp in a_out + b_out + _gather_ici_copies(bufs_b, bi_send, bi_recv)[0]:
                cp.wait_send()

    hbm = pl.BlockSpec(memory_space=pl.ANY)
    tab = pl.BlockSpec((rows, HD), lambda p, n: (rev(p, n), 0))
    st_spec = pl.BlockSpec((cps, NH, HD, HD), lambda p, n: (p * n, 0, 0, 0))
    out = pl.pallas_call(
        body, name="mixer_fwd", grid=(2, nb),
        in_specs=[col(0, False), col(1, False), col(2, False), col(3, True), col(4, True), col(5, False),
                  col(6, False), tab, tab, _const((2, 4, NH, CH, CH)), _const((NH, CH, CH)), _const((1, AW)),
                  _const((NH, CH, CH)), _const((NH, HD, HD)), _const((NH, HD, HD))] + [hbm] * nt,
        out_specs=[pl.BlockSpec((rows, D), lambda p, n: (p * n, 0)), st_spec, st_spec,
                   pl.BlockSpec((rows, AW), lambda p, n: (p * n, 0))] + [hbm] * nt,
        out_shape=[jax.ShapeDtypeStruct((ln, D), BF), jax.ShapeDtypeStruct((nc, NH, HD, HD), F32),
                   jax.ShapeDtypeStruct((nc, NH, HD, HD), F32), jax.ShapeDtypeStruct((ln, AW), F32)]
        + [jax.ShapeDtypeStruct(g.shape, g.dtype) for g in gbufs],
        input_output_aliases={15 + k: 4 + k for k in range(nt)},
        scratch_shapes=[pltpu.VMEM((nc, NH, HD, HD), F32), pltpu.VMEM((NH, HD, HD), F32)]
        + [pltpu.SemaphoreType.DMA(((NCHIP - 1) * na,))] * 2
        + [pltpu.SemaphoreType.DMA(((NCHIP - 1) * (nt - na),))] * 4,
        compiler_params=_params(56, ("arbitrary", "arbitrary")),
    )(z, z, z, z, z, z, z, cos_t, sin_t, dm, sgw, gain, bfull, scf, scb, *gbufs)
    return out[0], out[1], out[2], out[3], out[4:]


def _mixer_bwd(z, dycat, of_all, cos_t, sin_t, dm, sgw, gain, bfull, sf_all, sb_all, parts):
    ln = z.shape[0]
    nc = ln // CH
    cps = _chunks_per_step(nc, BWD_CHUNKS_PER_STEP)
    nb = nc // cps
    rows = cps * CH

    def rev(p, n):
        return p * n + (1 - p) * (nb - 1 - n)

    def col(j, both):
        if both:
            return pl.BlockSpec((rows, AW), lambda p, n: (rev(p, n), j))
        return pl.BlockSpec((rows, AW), lambda p, n: (p * n, j))

    nt = len(parts)

    def body(u_ref, v_ref, q_ref, k_ref, vr_ref, gf_ref, gb_ref, dya_ref, dyr_ref, of_ref, cos_ref, sin_ref,
             dm_ref, sgw_ref, gain_ref, bfull_ref, sf_ref, sb_ref, *refs):
        ps = refs[:nt]
        dz_ref, ddm_ref, dsgw_ref, dgain_ref, dbf_ref, dscf_ref, dscb_ref = refs[nt:nt + 7]
        got = refs[nt + 7:2 * nt + 7]
        gf_all, run, send_sems, recv_sems = refs[2 * nt + 7:]
        p, n = pl.program_id(0), pl.program_id(1)

        @pl.when((p == 0) & (n == 0))
        def _():
            for cp in _scatter_ici_copies(ps, got, send_sems, recv_sems):
                cp.start()

        mats = [[dm_ref[d, kind, h] for h in range(NH)] for d in range(2) for kind in range(3)]

        @pl.when(p == 0)
        def _():
            @pl.when(n == 0)
            def _():
                run[...] = jnp.zeros_like(run)
                ddm_ref[...] = jnp.zeros_like(ddm_ref)
                dsgw_ref[...] = jnp.zeros_like(dsgw_ref)
                dgain_ref[...] = jnp.zeros_like(dgain_ref)
                dbf_ref[...] = jnp.zeros_like(dbf_ref)

            for s in reversed(range(cps)):
                r0 = s * CH
                m = (nb - 1 - n) * cps + s
                cos, sin = cos_ref[r0:r0 + CH, :], sin_ref[r0:r0 + CH, :]
                q = [_rope(t, cos, sin) for t in _heads(q_ref, r0)]
                gf, dyr, of = _heads(gf_ref, r0), _heads(dyr_ref, r0), _heads(of_ref, r0)
                for h in range(NH):
                    _, vjp = jax.vjp(functools.partial(_gated_norm, gf[h]), of[h])
                    (dof,) = vjp(dyr[h])
                    dsf = _mm_tn(q[h], mats[1][h] * dof)
                    g_next = run[h]
                    gf_all[m, h] = g_next.astype(BF)
                    ddm_ref[0, 3, h] += sf_ref[s, h] * g_next
                    run[h] = dsf + dm_ref[0, 3, h] * g_next

            @pl.when(n == nb - 1)
            def _():
                dscf_ref[...] = run[...]

        @pl.when(p == 1)
        def _():
            @pl.when(n == 0)
            def _():
                run[...] = jnp.zeros_like(run)

            for s in range(cps):
                r0 = s * CH
                m = n * cps + s
                rw = slice(r0, r0 + CH)
                cos, sin = cos_ref[r0:r0 + CH, :], sin_ref[r0:r0 + CH, :]
                q = [_rope(t, cos, sin) for t in _heads(q_ref, r0)]
                k = [_rope(t, cos, sin) * K_SCALE for t in _heads(k_ref, r0)]
                vr, gf, gb, dyr = _heads(vr_ref, r0), _heads(gf_ref, r0), _heads(gb_ref, r0), _heads(dyr_ref, r0)
                u, v, dya = _heads(u_ref, r0), _heads(v_ref, r0), _heads(dya_ref, r0)
                sf = [sf_ref[s, h] for h in range(NH)]
                sb = [sb_ref[s, h] for h in range(NH)]
                g_f = [gf_all[m, h].astype(F32) for h in range(NH)]
                g_b = [run[h] for h in range(NH)]
                cols = [slice(h * HD, (h + 1) * HD) for h in range(NH)]

                def chunk(u_, v_, sgw_, gain_, bfull_, q_, k_, vr_, gf_, gb_, sb_, df, xf, zf, db, xb, zb, sf=sf):
                    ya = [_gate_group(u_[h], v_[h], sgw_[h], gain_[h], bfull_[h]) for h in range(NH)]
                    ret = [_ret_head(q_[h], k_[h], vr_[h], gf_[h], gb_[h], sf[h], sb_[h], df[h], xf[h], zf[h],
                                     db[h], xb[h], zb[h])[:3] for h in range(NH)]
                    return ya, ret

                _, vjp = jax.vjp(chunk, u, v, [sgw_ref[h] for h in range(NH)], [gain_ref[:, c] for c in cols],
                                 [bfull_ref[h] for h in range(NH)], q, k, vr, gf, gb, sb, *mats)
                (du, dv, dsgw, dgain, dbf, dq, dk, dvr, dgf, dgb, dsb, ddf, dxf, dzf, ddb, dxb, dzb) = vjp(
                    (dya, [(dyr[h], g_f[h], g_b[h]) for h in range(NH)]))
                for h in range(NH):
                    dz_ref[rw, h * HD:(h + 1) * HD] = du[h].astype(BF)
                    dz_ref[rw, AW + h * HD:AW + (h + 1) * HD] = dv[h].astype(BF)
                    dz_ref[rw, 2 * AW + h * HD:2 * AW + (h + 1) * HD] = _rope_bwd(dq[h], cos, sin).astype(BF)
                    dz_ref[rw, 3 * AW + h * HD:3 * AW + (h + 1) * HD] = _rope_bwd(dk[h] * K_SCALE, cos,
                                                                                  sin).astype(BF)
                    dz_ref[rw, 4 * AW + h * HD:4 * AW + (h + 1) * HD] = dvr[h].astype(BF)
                    dz_ref[rw, 5 * AW + h * HD:5 * AW + (h + 1) * HD] = dgf[h].astype(BF)
                    dz_ref[rw, 6 * AW + h * HD:6 * AW + (h + 1) * HD] = dgb[h].astype(BF)
                    ddm_ref[0, 0, h] += ddf[h]
                    ddm_ref[0, 1, h] += dxf[h]
                    ddm_ref[0, 2, h] += dzf[h]
                    ddm_ref[1, 0, h] += ddb[h]
                    ddm_ref[1, 1, h] += dxb[h]
                    ddm_ref[1, 2, h] += dzb[h]
                    ddm_ref[1, 3, h] += sb[h] * g_b[h]
                    dsgw_ref[h] += dsgw[h]
                    dgain_ref[:, cols[h]] += dgain[h]
                    dbf_ref[h] += dbf[h]
                    run[h] = dsb[h] + dm_ref[1, 3, h] * g_b[h]

            @pl.when(n == nb - 1)
            def _():
                dscb_ref[...] = run[...]

        @pl.when((p == 1) & (n == nb - 1))
        def _():
            cps_ = _scatter_ici_copies(ps, got, send_sems, recv_sems)
            for cp in cps_:
                cp.wait_recv()
            for cp in cps_:
                cp.wait_send()

    hbm = pl.BlockSpec(memory_space=pl.ANY)
    tab = pl.BlockSpec((rows, HD), lambda p, n: (rev(p, n), 0))
    tile4 = jax.ShapeDtypeStruct((NH, CH, CH), F32)
    out = pl.pallas_call(
        body, name="mixer_bwd", grid=(2, nb),
        in_specs=[col(0, False), col(1, False), col(2, True), col(3, False), col(4, False), col(5, True),
                  col(6, False),
                  pl.BlockSpec((rows, AW), lambda p, n: (p * n, 0)),
                  pl.BlockSpec((rows, AW), lambda p, n: (rev(p, n), 1)),
                  pl.BlockSpec((rows, AW), lambda p, n: ((1 - p) * (nb - 1 - n), 0)),
                  tab, tab, _const((2, 4, NH, CH, CH)), _const((NH, CH, CH)), _const((1, AW)),
                  _const((NH, CH, CH)),
                  pl.BlockSpec((cps, NH, HD, HD), lambda p, n: (rev(p, n), 0, 0, 0)),
                  pl.BlockSpec((cps, NH, HD, HD), lambda p, n: (p * n, 0, 0, 0))] + [hbm] * nt,
        out_specs=[pl.BlockSpec((rows, IN_COLS), lambda p, n: (p * n, 0)),
                   pl.BlockSpec((2, 4, NH, CH, CH), lambda p, n: (0, 0, 0, 0, 0)),
                   pl.BlockSpec((NH, CH, CH), lambda p, n: (0, 0, 0)),
                   pl.BlockSpec((1, AW), lambda p, n: (0, 0)),
                   pl.BlockSpec((NH, CH, CH), lambda p, n: (0, 0, 0)),
                   pl.BlockSpec((NH, HD, HD), lambda p, n: (0, 0, 0)),
                   pl.BlockSpec((NH, HD, HD), lambda p, n: (0, 0, 0))] + [hbm] * nt,
        out_shape=[jax.ShapeDtypeStruct((ln, IN_COLS), BF), jax.ShapeDtypeStruct((2, 4, NH, CH, CH), F32),
                   tile4, jax.ShapeDtypeStruct((1, AW), F32), tile4, tile4, tile4]
        + [jax.ShapeDtypeStruct((NCHIP - 1,) + p.shape[1:], BF) for p in parts],
        scratch_shapes=[pltpu.VMEM((nc, NH, HD, HD), BF), pltpu.VMEM((NH, HD, HD), F32)]
        + [pltpu.SemaphoreType.DMA(((NCHIP - 1) * nt,))] * 2,
        compiler_params=_params(60, ("arbitrary", "arbitrary")),
    )(z, z, z, z, z, z, z, dycat, dycat, of_all, cos_t, sin_t, dm, sgw, gain, bfull, sf_all, sb_all, *parts)
    return out[:7], out[7:]


def _small_reduce(ddm, ddm_ctx, dm, dbf):
    def body(ddm_ref, dctx_ref, dm_ref, dbf_ref, lg_ref, sgb_ref):
        ex = _decay_exponents()
        ones = jnp.ones((8, CH), F32)
        for d in range(2):
            for h in range(NH):
                tot = jnp.zeros((CH, CH), F32)
                for kind in range(4):
                    g = ddm_ref[d, kind, h]
                    if kind >= 2:
                        g = g + dctx_ref[d, kind - 2, h]
                    tot = tot + g * dm_ref[d, kind, h] * ex[d][kind]
                lg_ref[d * NH + h: d * NH + h + 1, :] = jnp.sum(tot, axis=0, keepdims=True)
        sgb_ref[...] = jnp.zeros_like(sgb_ref)
        for g in range(NH):
            r = lax.dot_general(ones, dbf_ref[g], (NT, ((), ())), precision=HI, preferred_element_type=F32)
            sgb_ref[g:g + 1, :] = r[0:1, :]

    return pl.pallas_call(
        body, name="small_reduce",
        out_shape=[jax.ShapeDtypeStruct((8, CH), F32), jax.ShapeDtypeStruct((8, CH), F32)],
        in_specs=[pl.BlockSpec(memory_space=pltpu.VMEM)] * 4,
        out_specs=[pl.BlockSpec(memory_space=pltpu.VMEM)] * 2,
        compiler_params=_params(32),
    )(ddm, ddm_ctx, dm, dbf)


def _mod_backward(ct_pad_t, cctx_col, dmod_pad, dcmod_cols, w_mod_s):
    def body(ct_ref, cc_ref, dm_ref, dc_ref, w_ref, gw_ref, part_ref):
        dcm = dc_ref[0:1, :]
        for d in range(1, NDEV):
            dcm = dcm + dc_ref[d:d + 1, :]
        gw_ref[...] = (jnp.dot(_silu(ct_ref[...]), dm_ref[...], precision=HI, preferred_element_type=F32)
                       + _silu(cc_ref[...]) * dcm)
        part_ref[...] = lax.dot_general(jnp.broadcast_to(dcm, (8, dcm.shape[1])), w_ref[...], (NT, ((), ())),
                                        precision=HI, preferred_element_type=F32)

    return pl.pallas_call(
        body, name="mod_backward",
        out_shape=[jax.ShapeDtypeStruct(w_mod_s.shape, F32), jax.ShapeDtypeStruct((8, D), F32)],
        in_specs=[pl.BlockSpec(memory_space=pltpu.VMEM)] * 5,
        out_specs=[pl.BlockSpec(memory_space=pltpu.VMEM)] * 2,
        compiler_params=_params(48),
    )(ct_pad_t, cctx_col, dmod_pad, dcmod_cols, w_mod_s)


def _cctx_update(parts, c_ctx, m, v):
    def body(p_ref, c_ref, m_ref, v_ref, g_ref, d_ref, mo_ref, vo_ref):
        tot = ((p_ref[0] + p_ref[2]) + p_ref[4]) + p_ref[6]
        cv = c_ref[...]
        s = jax.nn.sigmoid(cv)
        g = tot * (s * (1.0 + cv * (1.0 - s)))
        g_ref[...] = g
        d_ref[...], mo_ref[...], vo_ref[...] = _adamw_math(cv, g, m_ref[...], v_ref[...])

    return pl.pallas_call(
        body, name="cctx_update",
        out_shape=[jax.ShapeDtypeStruct((1, D), F32)] * 4,
        in_specs=[pl.BlockSpec(memory_space=pltpu.VMEM)] * 4,
        out_specs=[pl.BlockSpec(memory_space=pltpu.VMEM)] * 4,
        compiler_params=_params(16),
    )(parts, c_ctx, m, v)


def _small_update(gathered, wp, mp, vp):
    def body(g_ref, w_ref, m_ref, v_ref, go_ref, d_ref, mo_ref, vo_ref, loss_ref):
        tot = g_ref[0]
        for d in range(1, NDEV):
            tot = tot + g_ref[d]
        go_ref[Q_BMOD:Q_N1, :] = tot[P_DMOD:P_N1, :] + tot[P_DCMOD:P_DMOD, :]
        go_ref[Q_N1:Q_LG, :] = tot[P_N1:P_LG, :]
        lg = jnp.sum(tot[P_LG:P_N2, :], axis=1, keepdims=True)
        go_ref[Q_LG:Q_N2, :] = lg * jax.nn.sigmoid(-w_ref[Q_LG:Q_N2, :])
        go_ref[Q_N2:Q_ROWS, :] = tot[P_N2:P_LOSS, :]
        d_ref[...], mo_ref[...], vo_ref[...] = _adamw_math(w_ref[...], go_ref[...], m_ref[...], v_ref[...])
        ls = jnp.sum(jnp.sum(tot[P_LOSS:P_ROWS, :], axis=1, keepdims=True), axis=0, keepdims=True)
        loss_ref[...] = jnp.broadcast_to(ls, (8, CH))

    return pl.pallas_call(
        body, name="small_update",
        out_shape=[jax.ShapeDtypeStruct((Q_ROWS, CH), F32)] * 4 + [jax.ShapeDtypeStruct((8, CH), F32)],
        in_specs=[pl.BlockSpec(memory_space=pltpu.VMEM)] * 4,
        out_specs=[pl.BlockSpec(memory_space=pltpu.VMEM)] * 5,
        compiler_params=_params(32),
    )(gathered, wp, mp, vp)


def _rows(a):
    r = a.reshape(-1, CH)
    return jnp.pad(r, ((0, -r.shape[0] % 8), (0, 0)))


def _pack_small(b_mod, norm1, sg_gain, sg_w, sg_b, lf, lb, norm2, norm_f):
    lg = jnp.broadcast_to(jnp.concatenate([lf.reshape(NH), lb.reshape(NH)])[:, None], (2 * NH, CH))
    return jnp.concatenate([_rows(b_mod), _rows(norm1), _rows(sg_gain), _rows(sg_w), _rows(sg_b), lg,
                            _rows(norm2), _rows(norm_f)], axis=0)


def _unpack_small(p):
    return (p[Q_BMOD:Q_N1].reshape(1, 6 * D), p[Q_N1:Q_GAIN].reshape(1, D), p[Q_GAIN:Q_GAIN + NH].reshape(1, AW),
            p[Q_SGW:Q_SGB].reshape(1, NH, CH, CH), p[Q_SGB:Q_SGB + NH].reshape(1, NH, CH),
            p[Q_LG:Q_LG + NH, 0].reshape(1, NH), p[Q_LG + NH:Q_N2, 0].reshape(1, NH),
            p[Q_N2:Q_NF].reshape(1, D), p[Q_NF:Q_ROWS].reshape(D))


def _rope_tables(ln):
    pos = np.arange(ln)
    rows = (pos // GRID_W).astype(np.float32)
    cols = (pos % GRID_W).astype(np.float32)
    n_freq = HD // 4
    inv = (np.float32(ROPE_BASE) ** (-np.arange(n_freq, dtype=np.float32) / np.float32(n_freq))).astype(np.float32)
    ar = rows[:, None] * inv[None, :]
    ac = cols[:, None] * inv[None, :]
    cos_t = np.concatenate([np.cos(ar), np.cos(ar), np.cos(ac), np.cos(ac)], axis=1).astype(np.float32)
    sin_t = np.concatenate([-np.sin(ar), np.sin(ar), -np.sin(ac), np.sin(ac)], axis=1).astype(np.float32)
    return jnp.asarray(cos_t), jnp.asarray(sin_t)


def kernel(x, c, ctx, c_ctx, w_mod, b_mod, norm1, w_in, sg_gain, sg_w, sg_b, ret_logit_f, ret_logit_b, w_out, norm2, w_gate, w_up, w_down, norm_f, loss_target, m_c_ctx, m_w_mod, m_b_mod, m_norm1, m_w_in, m_sg_gain, m_sg_w, m_sg_b, m_ret_logit_f, m_ret_logit_b, m_w_out, m_norm2, m_w_gate, m_w_up, m_w_down, m_norm_f, v_c_ctx, v_w_mod, v_b_mod, v_norm1, v_w_in, v_sg_gain, v_sg_w, v_sg_b, v_ret_logit_f, v_ret_logit_b, v_w_out, v_norm2, v_w_gate, v_w_up, v_w_down, v_norm_f):
    ln = x.shape[1]
    xi, yi, ci = _pos()
    chip = 2 * xi + yi
    me = 4 * xi + 2 * yi + ci
    x2d = x.reshape(ln, D)
    tgt = loss_target.reshape(ln, D)
    mod_c = w_mod.shape[2]

    tr = lambda a: jnp.swapaxes(a[0], 0, 1)
    gbufs, c_all, prod_all = _prologue(c, c_ctx.reshape(1, D), w_mod[0],
                                       [w_in[0], w_out[0], tr(w_gate), tr(w_up), w_down[0]])
    wi = gbufs[0].reshape(NCHIP, D, WI_C)
    gbufs_a, gbufs_b = gbufs[1:3], gbufs[3:5]
    c_all = c_all.reshape(NDEV, D)
    prod_chips = prod_all[0::2]
    mod_rows = jnp.transpose(prod_chips, (1, 0, 2)).reshape(16, NCHIP * mod_c) + b_mod
    mod = lax.dynamic_slice_in_dim(mod_rows, me, 1, axis=0)
    cmod = mod_rows[8:9]
    sh1, sc1, g1, sh2, sc2, g2 = [mod[:, i * D:(i + 1) * D] for i in range(6)]
    csh1, csc1 = cmod[:, 0:D], cmod[:, D:2 * D]
    zrow = jnp.zeros((1, D), F32)
    vec_in = jnp.concatenate([norm1, sh1, sc1] + [zrow] * 5, axis=0)
    vec_ctx = jnp.concatenate([norm1, csh1, csc1] + [zrow] * 5, axis=0)
    vec_post = jnp.concatenate([g1, norm2, sh2, sc2, g2, norm_f.reshape(1, D), zrow, zrow], axis=0)

    logits = jnp.concatenate([ret_logit_f.reshape(NH), ret_logit_b.reshape(NH)])
    dm = _decay_mats(jnp.broadcast_to(logits[:, None, None], (2 * NH, CH, CH)))
    ctx2d = ctx.reshape(ctx.shape[1], D)
    scf, scb = _ctx_forward(ctx2d, vec_ctx, wi, dm)

    cos_t, sin_t = _rope_tables(ln)
    z, hx, gbufs_a = _in_proj(x2d, vec_in, wi, gbufs_a)
    bfull = jnp.broadcast_to(sg_b[0][:, :, None], (NH, CH, CH))
    ycat, sf_all, sb_all, of_all, gbufs = _mixer_fwd(z, cos_t, sin_t, dm, sg_w[0], sg_gain, bfull, scf, scb,
                                             gbufs_a, gbufs_b)
    wo, wg_t, wu_t, wd = [g.reshape(-1, D) for g in gbufs]

    dx1, dycat, h2, dy, df, act, da, db, acc_post = _post_mixer(x2d, ycat, tgt, vec_post, wo, wg_t, wu_t, wd)

    cidx = ci.reshape(1).astype(jnp.int32)
    where = jnp.stack([ci, chip]).astype(jnp.int32)

    def halves_summed(full, names):
        full = [g.reshape(NCHIP, 2, g.shape[1] // 2, g.shape[2]) for g in full]
        from_sib = _rs_exchange_halves(full, "rs_exchange_" + names[0])
        return [_rs_add_halves(g, r, where, "rs_add_halves_" + nm) for g, r, nm in zip(full, from_sib, names)]

    def split(g):
        return g.reshape(NCHIP, 2, g.shape[1] // (2 * NCHIP), g.shape[2])

    g_wd = split(_tn_matmul(act, df, "grad_w_down", 1, DFF, D, False, "shared", 1024))
    g_wu, x_wd = _tn_matmul(db, h2, "grad_w_up", 1, DFF, D, False, "shared", 1024, carry=g_wd)
    g_wu = split(g_wu)
    g_wg, x_wu = _tn_matmul(da, h2, "grad_w_gate", 1, DFF, D, False, "shared", 1024, carry=g_wu)
    g_wg = split(g_wg)
    g_wo, x_wg = _tn_matmul(ycat, dy, "grad_w_out", 1, D, D, False, "shared", 1024, carry=g_wg)
    g_wo = split(g_wo)
    x_wo = _rs_exchange_halves([g_wo], "rs_exchange_w_out")[0]
    names = ["w_in", "w_out", "w_gate", "w_up", "w_down"]
    sums_b = [_rs_add_halves(g, r, where, "rs_add_halves_" + nm)
              for g, r, nm in zip([g_wo, g_wg, g_wu, g_wd], [x_wo, x_wg, x_wu, x_wd], names[1:])]

    (dz, ddm, dsgw, dgain, dbf, dscf, dscb), from_chips_b = _mixer_bwd(
        z, dycat, of_all, cos_t, sin_t, dm, sg_w[0], sg_gain, bfull, sf_all, sb_all, [s[1] for s in sums_b])
    gwkv, acc_ctx, ddm_ctx = _ctx_backward(ctx2d, vec_ctx, wi, dm, dscf, dscb)
    g_wi = _tn_matmul(hx, dz, "grad_w_in", NCHIP, D, WI_C, False, "cols", 1024, ctx_kv=gwkv)
    sums_a = halves_summed([g_wi], names[:1])
    lg_part, dsgb = _small_reduce(ddm, ddm_ctx, dm, dbf)
    dcmod = jnp.concatenate([acc_ctx[1:2], acc_ctx[2:3], jnp.zeros((1, 4 * D), F32)], axis=1)
    dmod_rest = jnp.concatenate([acc_post[0:1], acc_post[2:3], acc_post[3:4], acc_post[4:5]], axis=1)
    early = jnp.concatenate([_rows(dcmod), _rows(dmod_rest), _rows(dgain), _rows(dsgw), dsgb, lg_part,
                             _rows(acc_post[1:2]), _rows(acc_post[5:6]), _rows(acc_post[6:7])], axis=0)
    gx, acc_in, from_chips_a, early_all = _in_proj_bwd(dz, x2d, dx1, vec_in, wi, [s[1] for s in sums_a], early)

    sums = sums_a + sums_b
    from_chips = list(from_chips_a) + list(from_chips_b)
    finals = [_rs_add_chips(s[0], r, "rs_add_chips_" + nm) for s, r, nm in zip(sums, from_chips, names)]
    others = _rs_share_final(finals)

    late = jnp.concatenate([_rows(acc_in[1:2]), _rows(acc_in[2:3]), _rows(acc_in[0:1] + acc_ctx[0:1])], axis=0)
    late_all = _allgather_small(late, "gather_small")
    n_dc, n_l = P_DMOD - P_DCMOD, 16
    gathered = jnp.concatenate([early_all[:, :n_dc], late_all[:, :n_l], early_all[:, n_dc:n_dc + 32],
                                late_all[:, n_l:], early_all[:, n_dc + 32:]], axis=1)
    dmod_all = gathered[:, P_DMOD:P_N1].reshape(NDEV, 6 * D)
    dcmod_all = gathered[:, P_DCMOD:P_DMOD].reshape(NDEV, 6 * D)
    dmod_cols = lax.dynamic_slice_in_dim(dmod_all, chip * mod_c, mod_c, axis=1)
    dcmod_cols = lax.dynamic_slice_in_dim(dcmod_all, chip * mod_c, mod_c, axis=1)
    dmod_pad = jnp.concatenate([dmod_cols, jnp.zeros((CH - NDEV, mod_c), F32)], axis=0)
    ct_pad_t = jnp.concatenate([jnp.transpose(c_all), jnp.zeros((D, CH - NDEV), F32)], axis=1)
    g_wmod, cctx_part = _mod_backward(ct_pad_t, c_ctx.reshape(D, 1), dmod_pad, dcmod_cols, w_mod[0])
    parts = _allgather_small(cctx_part[0:1], "gather_cctx")
    g_cctx, d_cctx, nm_cctx, nv_cctx = _cctx_update(parts, c_ctx.reshape(1, D), m_c_ctx.reshape(1, D),
                                                    v_c_ctx.reshape(1, D))

    wp = _pack_small(b_mod, norm1, sg_gain, sg_w, sg_b, ret_logit_f, ret_logit_b, norm2, norm_f)
    mp = _pack_small(m_b_mod, m_norm1, m_sg_gain, m_sg_w, m_sg_b, m_ret_logit_f, m_ret_logit_b, m_norm2, m_norm_f)
    vp = _pack_small(v_b_mod, v_norm1, v_sg_gain, v_sg_w, v_sg_b, v_ret_logit_f, v_ret_logit_b, v_norm2, v_norm_f)
    gp, dp, mp2, vp2, loss_t = _small_update(gathered, wp, mp, vp)

    big_w = [w_in[0], w_out[0], tr(w_gate), tr(w_up), w_down[0]]
    big_m = [m_w_in[0], m_w_out[0], tr(m_w_gate), tr(m_w_up), m_w_down[0]]
    big_v = [v_w_in[0], v_w_out[0], tr(v_w_gate), tr(v_w_up), v_w_down[0]]
    upd = [_adamw_halves(w, own, oth, m, v, cidx, "adamw_" + nm) for w, own, oth, m, v, nm in
           zip(big_w, finals, others, big_m, big_v, names)]
    big_g = [g_wmod] + [u[0] for u in upd]
    big = [_adamw(w_mod[0], g_wmod, m_w_mod[0], v_w_mod[0], "adamw_w_mod")] + [u[1:] for u in upd]

    def assemble(small, cctx, bigs):
        b_mod_, norm1_, gain_, sgw_, sgb_, lf_, lb_, norm2_, normf_ = _unpack_small(small)
        wm, wi_, wo_, wg_, wu_, wd_ = [b[None] for b in bigs]
        wg_, wu_ = jnp.swapaxes(wg_, 1, 2), jnp.swapaxes(wu_, 1, 2)
        return [cctx.reshape(D), wm, b_mod_, norm1_, wi_, gain_, sgw_, sgb_, lf_, lb_, wo_, norm2_, wg_, wu_, wd_,
                normf_]

    out = [loss_t[0, 0], gx.reshape(1, ln, D)]
    out += assemble(gp, g_cctx, big_g)
    out += assemble(dp, d_cctx, [b[0] for b in big])
    out += assemble(mp2, nm_cctx, [b[1] for b in big])
    out += assemble(vp2, nv_cctx, [b[2] for b in big])
    return tuple(out)
```

```python
import functools

import jax
import jax.numpy as jnp
import numpy as np
from jax import lax
from jax.experimental import pallas as pl
from jax.experimental.pallas import tpu as pltpu

F32 = jnp.float32
BF = jnp.bfloat16
MESH = pl.DeviceIdType.MESH

D = 1024
CH = 128
HD = 128
NH = 4
AW = 512
IN_COLS = 3584
DFF = 2816
NCHIP = 4
NDEV = 8
WI_C = IN_COLS // NCHIP
FF_C = DFF // NCHIP
WO_R = D // NCHIP
EPS = 1e-6
GRID_W = 64
ROPE_BASE = 10000.0
K_SCALE = HD ** -0.5
LR, B1, B2, AEPS, WD, STEP = 0.001, 0.9, 0.999, 1e-08, 0.01, 10
VMEM_MB = 1 << 20
HI = lax.Precision.HIGHEST

P_DCMOD, P_DMOD, P_N1, P_GAIN, P_SGW, P_SGB, P_LG, P_N2, P_NF, P_LOSS = 0, 48, 96, 104, 112, 624, 632, 640, 648, 656
P_ROWS = 664
Q_BMOD, Q_N1, Q_GAIN, Q_SGW, Q_SGB, Q_LG, Q_N2, Q_NF = 0, 48, 56, 64, 576, 584, 592, 600
Q_ROWS = 608


def _params(vmem_mb, sem=None):
    return pltpu.CompilerParams(vmem_limit_bytes=vmem_mb * VMEM_MB, dimension_semantics=sem)


def _const(shape):
    nd = len(shape)
    return pl.BlockSpec(shape, lambda *_: (0,) * nd, pipeline_mode=pl.Buffered(1))


def _pos():
    return lax.axis_index("x"), lax.axis_index("y"), lax.axis_index("c")


def _dot(a, b, dims):
    return lax.dot_general(a, b, (dims, ((), ())), preferred_element_type=F32)


NN = ((1,), (0,))
NT = ((1,), (1,))
TN = ((0,), (0,))


@jax.custom_vjp
def _mm(a, b):
    return _dot(a.astype(BF), b.astype(BF), NN)


def _mm_f(a, b):
    return _mm(a, b), (a.astype(BF), b.astype(BF))


def _mm_b(res, g):
    a, b = res
    gb = g.astype(BF)
    return _dot(gb, b, NT), _dot(a, gb, TN)


_mm.defvjp(_mm_f, _mm_b)


@jax.custom_vjp
def _mm_nt(a, b):
    return _dot(a.astype(BF), b.astype(BF), NT)


def _mm_nt_f(a, b):
    return _mm_nt(a, b), (a.astype(BF), b.astype(BF))


def _mm_nt_b(res, g):
    a, b = res
    gb = g.astype(BF)
    return _dot(gb, b, NN), _dot(gb, a, TN)


_mm_nt.defvjp(_mm_nt_f, _mm_nt_b)


@jax.custom_vjp
def _mm_tn(a, b):
    return _dot(a.astype(BF), b.astype(BF), TN)


def _mm_tn_f(a, b):
    return _mm_tn(a, b), (a.astype(BF), b.astype(BF))


def _mm_tn_b(res, g):
    a, b = res
    gb = g.astype(BF)
    return _dot(b, gb, NT), _dot(a, gb, NN)


_mm_tn.defvjp(_mm_tn_f, _mm_tn_b)


def _gelu(x):
    return x * (0.5 * (1.0 + jnp.tanh(0.7978845608028654 * (x + 0.044715 * (x * x * x)))))


def _silu(x):
    return x * jax.nn.sigmoid(x)


def _rms(x):
    return lax.rsqrt(jnp.mean(x * x, axis=-1, keepdims=True) + EPS)


def _swap32(t):
    lane = lax.broadcasted_iota(jnp.int32, t.shape, 1)
    first = (lane % 64) < 32
    return jnp.where(first, pltpu.roll(t, 96, 1), pltpu.roll(t, 32, 1))


def _rope(t, cos, sin):
    return t * cos + _swap32(t) * sin


def _rope_bwd(d, cos, sin):
    return d * cos + _swap32(d * sin)


def _heads(ref, r0=0):
    return [ref[r0:r0 + CH, h * HD:(h + 1) * HD].astype(F32) for h in range(NH)]


def _gate_group(u, v, sgw, gain, bfull):
    gv = _gelu(v)
    return _gelu(u) * (_mm(sgw, gv * _rms(gv) * gain) + bfull)


def _gated_norm(gate, o):
    return _silu(gate) * (o * _rms(o))


def _ret_head(q, k, vr, gf, gb, sf, sb, df, xf, zf, db, xb, zb):
    a = _mm_nt(q, k)
    of = _mm(a * df, vr) + xf * _mm(q, sf)
    ob = _mm(a * db, vr) + xb * _mm(q, sb)
    return _gated_norm(gf, of) + _gated_norm(gb, ob), _mm_tn(k, zf * vr), _mm_tn(k, zb * vr), of


def _ctx_states(ctx0, ctx1, n1, csh, csc, wk, wv, zf, zb, ef, eb):
    hc0 = (ctx0 * _rms(ctx0) * n1) * (1.0 + csc) + csh
    hc1 = (ctx1 * _rms(ctx1) * n1) * (1.0 + csc) + csh
    scf, scb = [], []
    for h in range(NH):
        k0, k1 = _mm(hc0, wk[h]) * K_SCALE, _mm(hc1, wk[h]) * K_SCALE
        v0, v1 = _mm(hc0, wv[h]), _mm(hc1, wv[h])
        scf.append(ef[h] * _mm_tn(k0, zf[h] * v0) + _mm_tn(k1, zf[h] * v1))
        scb.append(eb[h] * _mm_tn(k1, zb[h] * v1) + _mm_tn(k0, zb[h] * v0))
    return scf, scb


def _allgather_small(v, name):
    r, n = v.shape

    def body(v_ref, out_ref, send_sems, recv_sems, local_sem):
        x, y, c = _pos()
        me = 4 * x + 2 * y + c
        mine = pltpu.make_async_copy(v_ref, out_ref.at[me], local_sem)
        mine.start()
        sent = []
        for k in range(1, NDEV):
            kx, ky, kc = (k >> 2) & 1, (k >> 1) & 1, k & 1
            peer = (x ^ kx, y ^ ky, c ^ kc)
            cp = pltpu.make_async_remote_copy(src_ref=v_ref, dst_ref=out_ref.at[me], send_sem=send_sems.at[k - 1],
                                              recv_sem=recv_sems.at[k - 1], device_id=peer, device_id_type=MESH)
            cp.start()
            sent.append(cp)
        for k in range(1, NDEV):
            kx, ky, kc = (k >> 2) & 1, (k >> 1) & 1, k & 1
            peer = (x ^ kx, y ^ ky, c ^ kc)
            src = 4 * (x ^ kx) + 2 * (y ^ ky) + (c ^ kc)
            pltpu.make_async_remote_copy(src_ref=v_ref, dst_ref=out_ref.at[src], send_sem=send_sems.at[k - 1],
                                         recv_sem=recv_sems.at[k - 1], device_id=peer, device_id_type=MESH).wait_recv()
        for cp in sent:
            cp.wait_send()
        mine.wait()

    return pl.pallas_call(
        body, name=name,
        out_shape=jax.ShapeDtypeStruct((NDEV, r, n), F32),
        in_specs=[pl.BlockSpec(memory_space=pltpu.VMEM)],
        out_specs=pl.BlockSpec(memory_space=pltpu.VMEM),
        scratch_shapes=[pltpu.SemaphoreType.DMA((NDEV - 1,)), pltpu.SemaphoreType.DMA((NDEV - 1,)),
                        pltpu.SemaphoreType.DMA],
        compiler_params=_params(16),
    )(v)


def _chip_offsets():
    return [((k >> 1) & 1, k & 1) for k in range(1, NCHIP)]


def _prologue(c, c_ctx, w_mod_s, shards):
    nt = len(shards)
    shapes = [s.shape for s in shards]
    mod_c = w_mod_s.shape[1]

    def body(*refs):
        c_ref, cc_ref, wm_ref = refs[:3]
        srcs = refs[3:3 + nt]
        outs = refs[3 + nt:3 + 2 * nt]
        call_ref, prod_ref = refs[3 + 2 * nt:5 + 2 * nt]
        stages = refs[5 + 2 * nt:5 + 3 * nt]
        ct = refs[5 + 3 * nt]
        loaded = refs[6 + 3 * nt:6 + 4 * nt]
        (c_send, c_recv, p_send, p_recv, ici_send, ici_recv, d2d_send, d2d_recv, local_sems,
         load_sems) = refs[6 + 4 * nt:]
        x, y, c = _pos()
        chip = 2 * x + y
        me = 4 * x + 2 * y + c
        sib = (x, y, 1 - c)
        loads = [pltpu.make_async_copy(wm_ref, loaded[0], load_sems.at[0])]
        loads += [pltpu.make_async_copy(srcs[t], loaded[t], load_sems.at[t]) for t in range(1, nt)]
        for cp in loads:
            cp.start()
        pending = []

        def stage(t, src):
            half = shapes[t][0] // 2
            stages[t][0] = src[0:half, :].astype(BF)
            stages[t][1] = src[half:2 * half, :].astype(BF)
            cp = pltpu.make_async_copy(stages[t], outs[t].at[chip], local_sems.at[t])
            cp.start()
            pending.append(cp)

        stage(0, srcs[0])
        sends = []
        for k, (kx, ky) in enumerate(_chip_offsets()):
            cp = pltpu.make_async_remote_copy(src_ref=stages[0].at[c], dst_ref=outs[0].at[chip, c],
                                              send_sem=ici_send.at[k], recv_sem=ici_recv.at[k],
                                              device_id=(x ^ kx, y ^ ky, c), device_id_type=MESH)
            cp.start()
            sends.append(cp)

        def to_all(src, dst_of, send_sems, recv_sems):
            for k in range(1, NDEV):
                kx, ky, kc = (k >> 2) & 1, (k >> 1) & 1, k & 1
                cp = pltpu.make_async_remote_copy(src_ref=src, dst_ref=dst_of(me), send_sem=send_sems.at[k - 1],
                                                  recv_sem=recv_sems.at[k - 1], device_id=(x ^ kx, y ^ ky, c ^ kc),
                                                  device_id_type=MESH)
                cp.start()
                sends.append(cp)
            for k in range(1, NDEV):
                kx, ky, kc = (k >> 2) & 1, (k >> 1) & 1, k & 1
                frm = 4 * (x ^ kx) + 2 * (y ^ ky) + (c ^ kc)
                pltpu.make_async_remote_copy(src_ref=src, dst_ref=dst_of(frm), send_sem=send_sems.at[k - 1],
                                             recv_sem=recv_sems.at[k - 1], device_id=(x ^ kx, y ^ ky, c ^ kc),
                                             device_id_type=MESH).wait_recv()

        call_ref[me] = c_ref[...]
        to_all(c_ref, lambda d: call_ref.at[d], c_send, c_recv)
        ct[...] = jnp.zeros_like(ct)
        for d in range(NDEV):
            ct[d:d + 1, :] = call_ref[d]
        ct[NDEV:NDEV + 1, :] = cc_ref[...]
        loads[0].wait()
        prod_ref[me] = jnp.dot(_silu(ct[...]), loaded[0][...], precision=HI, preferred_element_type=F32)
        to_all(prod_ref.at[me], lambda d: prod_ref.at[d], p_send, p_recv)
        for t in range(1, nt):
            loads[t].wait()
            stage(t, loaded[t])

        for k, (kx, ky) in enumerate(_chip_offsets()):
            frm = 2 * (x ^ kx) + (y ^ ky)
            pltpu.make_async_remote_copy(src_ref=stages[0].at[c], dst_ref=outs[0].at[frm, c],
                                         send_sem=ici_send.at[k], recv_sem=ici_recv.at[k],
                                         device_id=(x ^ kx, y ^ ky, c), device_id_type=MESH).wait_recv()
            cp = pltpu.make_async_remote_copy(src_ref=outs[0].at[frm, c], dst_ref=outs[0].at[frm, c],
                                              send_sem=d2d_send.at[k], recv_sem=d2d_recv.at[k],
                                              device_id=sib, device_id_type=MESH)
            cp.start()
            sends.append(cp)
        for k, (kx, ky) in enumerate(_chip_offsets()):
            frm = 2 * (x ^ kx) + (y ^ ky)
            pltpu.make_async_remote_copy(src_ref=stages[0].at[c], dst_ref=outs[0].at[frm, 1 - c],
                                         send_sem=d2d_send.at[k], recv_sem=d2d_recv.at[k],
                                         device_id=sib, device_id_type=MESH).wait_recv()
        for cp in sends:
            cp.wait_send()
        for cp in pending:
            cp.wait()

    vm = pl.BlockSpec(memory_space=pltpu.VMEM)
    hbm = pl.BlockSpec(memory_space=pl.ANY)
    out = pl.pallas_call(
        body, name="prologue",
        out_shape=[jax.ShapeDtypeStruct((NCHIP, 2, r // 2, cc), BF) for r, cc in shapes]
        + [jax.ShapeDtypeStruct((NDEV, 1, D), F32), jax.ShapeDtypeStruct((NDEV, 16, mod_c), F32)],
        in_specs=[vm, vm, hbm, vm] + [hbm] * (nt - 1),
        out_specs=[hbm] * nt + [vm, vm],
        scratch_shapes=[pltpu.VMEM((2, r // 2, cc), BF) for r, cc in shapes] + [pltpu.VMEM((16, D), F32)]
        + [pltpu.VMEM(w_mod_s.shape, F32)] + [pltpu.VMEM(s, F32) for s in shapes[1:]]
        + [pltpu.SemaphoreType.DMA((NDEV - 1,))] * 4 + [pltpu.SemaphoreType.DMA((NCHIP - 1,))] * 4
        + [pltpu.SemaphoreType.DMA((nt,))] * 2,
        compiler_params=_params(56),
    )(c, c_ctx, w_mod_s, *shards)
    return out[:nt], out[nt], out[nt + 1]


def _gather_ici_copies(bufs, send_sems, recv_sems):
    x, y, c = _pos()
    chip = 2 * x + y
    nt = len(bufs)
    out_cp, in_cp = [], []
    for k, (kx, ky) in enumerate(_chip_offsets()):
        frm = 2 * (x ^ kx) + (y ^ ky)
        for t in range(nt):
            s = k * nt + t
            peer = (x ^ kx, y ^ ky, c)
            out_cp.append(pltpu.make_async_remote_copy(
                src_ref=bufs[t].at[chip, c], dst_ref=bufs[t].at[chip, c], send_sem=send_sems.at[s],
                recv_sem=recv_sems.at[s], device_id=peer, device_id_type=MESH))
            in_cp.append(pltpu.make_async_remote_copy(
                src_ref=bufs[t].at[chip, c], dst_ref=bufs[t].at[frm, c], send_sem=send_sems.at[s],
                recv_sem=recv_sems.at[s], device_id=peer, device_id_type=MESH))
    return out_cp, in_cp


def _gather_d2d_copies(bufs, send_sems, recv_sems):
    x, y, c = _pos()
    nt = len(bufs)
    out_cp, in_cp = [], []
    for k, (kx, ky) in enumerate(_chip_offsets()):
        frm = 2 * (x ^ kx) + (y ^ ky)
        for t in range(nt):
            s = k * nt + t
            out_cp.append(pltpu.make_async_remote_copy(
                src_ref=bufs[t].at[frm, c], dst_ref=bufs[t].at[frm, c], send_sem=send_sems.at[s],
                recv_sem=recv_sems.at[s], device_id=(x, y, 1 - c), device_id_type=MESH))
            in_cp.append(pltpu.make_async_remote_copy(
                src_ref=bufs[t].at[frm, c], dst_ref=bufs[t].at[frm, 1 - c], send_sem=send_sems.at[s],
                recv_sem=recv_sems.at[s], device_id=(x, y, 1 - c), device_id_type=MESH))
    return out_cp, in_cp


def _scatter_ici_copies(parts, outs, send_sems, recv_sems):
    x, y, c = _pos()
    nt = len(parts)
    cps = []
    for k, (kx, ky) in enumerate(_chip_offsets()):
        dst_chip = 2 * (x ^ kx) + (y ^ ky)
        for t in range(nt):
            s = k * nt + t
            cps.append(pltpu.make_async_remote_copy(
                src_ref=parts[t].at[dst_chip], dst_ref=outs[t].at[k], send_sem=send_sems.at[s],
                recv_sem=recv_sems.at[s], device_id=(x ^ kx, y ^ ky, c), device_id_type=MESH))
    return cps


def _rs_exchange_halves(grads, name):
    nt = len(grads)
    shapes = [g.shape for g in grads]

    def body(*refs):
        gs, outs = refs[:nt], refs[nt:2 * nt]
        send_sems, recv_sems = refs[2 * nt:]
        x, y, c = _pos()
        sib = (x, y, 1 - c)
        sent = []
        for t in range(nt):
            for j in range(NCHIP):
                s = t * NCHIP + j
                cp = pltpu.make_async_remote_copy(src_ref=gs[t].at[j, 1 - c], dst_ref=outs[t].at[j],
                                                  send_sem=send_sems.at[s], recv_sem=recv_sems.at[s],
                                                  device_id=sib, device_id_type=MESH)
                cp.start()
                sent.append(cp)
        for cp in sent:
            cp.wait_recv()
        for cp in sent:
            cp.wait_send()

    return pl.pallas_call(
        body, name=name,
        out_shape=[jax.ShapeDtypeStruct((NCHIP, s[2], s[3]), F32) for s in shapes],
        in_specs=[pl.BlockSpec(memory_space=pl.ANY)] * nt,
        out_specs=[pl.BlockSpec(memory_space=pl.ANY)] * nt,
        scratch_shapes=[pltpu.SemaphoreType.DMA((nt * NCHIP,))] * 2,
    )(*grads)


def _rs_share_final(finals):
    nt = len(finals)
    shapes = [f.shape for f in finals]

    def body(*refs):
        fs, outs = refs[:nt], refs[nt:2 * nt]
        send_sems, recv_sems = refs[2 * nt:]
        x, y, c = _pos()
        sent = []
        for t in range(nt):
            cp = pltpu.make_async_remote_copy(src_ref=fs[t], dst_ref=outs[t], send_sem=send_sems.at[t],
                                              recv_sem=recv_sems.at[t], device_id=(x, y, 1 - c), device_id_type=MESH)
            cp.start()
            sent.append(cp)
        for cp in sent:
            cp.wait_recv()
        for cp in sent:
            cp.wait_send()

    return pl.pallas_call(
        body, name="rs_share_final",
        out_shape=[jax.ShapeDtypeStruct(s, F32) for s in shapes],
        in_specs=[pl.BlockSpec(memory_space=pl.ANY)] * nt,
        out_specs=[pl.BlockSpec(memory_space=pl.ANY)] * nt,
        scratch_shapes=[pltpu.SemaphoreType.DMA((nt,))] * 2,
    )(*finals)


def _row_tile(h, cc=D):
    for t in (512, 384, 352, 256, 176, 128, 64, 32, 16):
        if h % t == 0 and t * cc * 4 <= (5 * VMEM_MB) // 4:
            return t
    return h


def _rs_add_halves(g, recv, where, name):
    _, _, h, cc = g.shape
    th = _row_tile(h, cc)

    def body(w_ref, g_ref, r_ref, own_ref, ob_ref):
        s = g_ref[...] + r_ref[...]
        ob_ref[...] = s.astype(BF)

        @pl.when(pl.program_id(1) == w_ref[1])
        def _():
            own_ref[...] = s

    return pl.pallas_call(
        body, name=name,
        grid_spec=pltpu.PrefetchScalarGridSpec(
            num_scalar_prefetch=1, grid=(h // th, NCHIP),
            in_specs=[pl.BlockSpec((None, None, th, cc), lambda i, j, w_ref: (j, w_ref[0], i, 0)),
                      pl.BlockSpec((None, th, cc), lambda i, j, w_ref: (j, i, 0))],
            out_specs=[pl.BlockSpec((th, cc), lambda i, j, w_ref: (i, 0)),
                       pl.BlockSpec((None, th, cc), lambda i, j, w_ref: (j, i, 0))]),
        out_shape=[jax.ShapeDtypeStruct((h, cc), F32), jax.ShapeDtypeStruct((NCHIP, h, cc), BF)],
        compiler_params=_params(48, ("arbitrary", "arbitrary")),
    )(where, g, recv)


def _rs_add_chips(own, recv, name):
    h, cc = own.shape
    th = _row_tile(h, cc)

    def body(o_ref, r_ref, out_ref):
        out_ref[...] = ((o_ref[...] + r_ref[0].astype(F32)) + r_ref[1].astype(F32)) + r_ref[2].astype(F32)

    return pl.pallas_call(
        body, name=name, grid=(h // th,),
        in_specs=[pl.BlockSpec((th, cc), lambda i: (i, 0)), pl.BlockSpec((NCHIP - 1, th, cc), lambda i: (0, i, 0))],
        out_specs=pl.BlockSpec((th, cc), lambda i: (i, 0)),
        out_shape=jax.ShapeDtypeStruct((h, cc), F32),
        compiler_params=_params(48, ("parallel",)),
    )(own, recv)


def _adamw_math(w, g, m, v):
    m2 = B1 * m + (1.0 - B1) * g
    v2 = B2 * v + (1.0 - B2) * (g * g)
    m_hat = m2 / (1.0 - B1 ** STEP)
    v_hat = v2 / (1.0 - B2 ** STEP)
    delta = -LR * (m_hat / (jnp.sqrt(v_hat) + AEPS) + WD * w)
    return delta, m2, v2


def _adamw(w, g, m, v, name):
    r, cc = w.shape
    tr = _row_tile(r, cc)

    def body(w_ref, g_ref, m_ref, v_ref, d_ref, mo_ref, vo_ref):
        d, m2, v2 = _adamw_math(w_ref[...], g_ref[...], m_ref[...], v_ref[...])
        d_ref[...] = d
        mo_ref[...] = m2
        vo_ref[...] = v2

    spec = pl.BlockSpec((tr, cc), lambda i: (i, 0))
    return pl.pallas_call(
        body, name=name, grid=(r // tr,), in_specs=[spec] * 4, out_specs=[spec] * 3,
        out_shape=[jax.ShapeDtypeStruct((r, cc), F32)] * 3,
        compiler_params=_params(48, ("parallel",)),
    )(w, g, m, v)


def _adamw_halves(w, own, other, m, v, cidx, name):
    r, cc = w.shape
    h = r // 2
    tr = _row_tile(h, cc)
    per = h // tr

    def body(c_ref, w_ref, own_ref, oth_ref, m_ref, v_ref, g_ref, d_ref, mo_ref, vo_ref):
        mine = (pl.program_id(0) // per) == c_ref[0]
        g = jnp.where(mine, own_ref[...], oth_ref[...])
        g_ref[...] = g
        d, m2, v2 = _adamw_math(w_ref[...], g, m_ref[...], v_ref[...])
        d_ref[...] = d
        mo_ref[...] = m2
        vo_ref[...] = v2

    full = pl.BlockSpec((tr, cc), lambda i, c_ref: (i, 0))
    half = pl.BlockSpec((tr, cc), lambda i, c_ref: (i % per, 0))
    return pl.pallas_call(
        body, name=name,
        grid_spec=pltpu.PrefetchScalarGridSpec(
            num_scalar_prefetch=1, grid=(r // tr,),
            in_specs=[full, half, half, full, full], out_specs=[full] * 4),
        out_shape=[jax.ShapeDtypeStruct((r, cc), F32)] * 4,
        compiler_params=_params(48, ("parallel",)),
    )(cidx, w, own, other, m, v)


def _decay_exponents():
    ri = lax.broadcasted_iota(jnp.int32, (CH, CH), 0).astype(F32)
    ci = lax.broadcasted_iota(jnp.int32, (CH, CH), 1).astype(F32)
    full = jnp.full((CH, CH), float(CH), F32)
    return [[ri - ci, ri + 1.0, (CH - 1.0) - ri, full], [ci - ri, CH - ri, ri, full]]


def _decay_mats(logit_full):
    def body(l_ref, o_ref):
        ex = _decay_exponents()
        for d in range(2):
            for h in range(NH):
                lv = l_ref[d * NH + h]
                lg = jnp.minimum(lv, 0.0) - jnp.log(1.0 + jnp.exp(-jnp.abs(lv)))
                for kind in range(4):
                    m = jnp.exp(lg * ex[d][kind])
                    if kind == 0:
                        m = jnp.where(ex[d][0] >= 0.0, jnp.exp(lg * jnp.maximum(ex[d][0], 0.0)), 0.0)
                    o_ref[d, kind, h] = m

    return pl.pallas_call(
        body, name="decay_mats",
        out_shape=jax.ShapeDtypeStruct((2, 4, NH, CH, CH), F32),
        in_specs=[pl.BlockSpec(memory_space=pltpu.VMEM)],
        out_specs=pl.BlockSpec(memory_space=pltpu.VMEM),
        compiler_params=_params(32),
    )(logit_full)


def _ctx_kv_weights(wi_ref):
    def cols(g):
        return wi_ref[g // WI_C, :, g % WI_C: g % WI_C + HD].astype(F32)

    wk = [cols(3 * AW + h * HD) for h in range(NH)]
    wv = [cols(4 * AW + h * HD) for h in range(NH)]
    return wk, wv


def _ctx_forward(ctx, vecs, wi, dm):
    def body(ctx_ref, v_ref, wi_ref, dm_ref, scf_ref, scb_ref):
        wk, wv = _ctx_kv_weights(wi_ref)
        mats = [[dm_ref[d, kind, h] for h in range(NH)] for d in range(2) for kind in (2, 3)]
        scf, scb = _ctx_states(ctx_ref[0:CH, :], ctx_ref[CH:2 * CH, :], v_ref[0:1, :], v_ref[1:2, :],
                               v_ref[2:3, :], wk, wv, mats[0], mats[2], mats[1], mats[3])
        for h in range(NH):
            scf_ref[h] = scf[h]
            scb_ref[h] = scb[h]

    return pl.pallas_call(
        body, name="ctx_forward",
        out_shape=[jax.ShapeDtypeStruct((NH, HD, HD), F32)] * 2,
        in_specs=[pl.BlockSpec(memory_space=pltpu.VMEM)] * 4,
        out_specs=[pl.BlockSpec(memory_space=pltpu.VMEM)] * 2,
        compiler_params=_params(48),
    )(ctx, vecs, wi, dm)


def _ctx_backward(ctx, vecs, wi, dm, dscf, dscb):
    def body(ctx_ref, v_ref, wi_ref, dm_ref, gf_ref, gb_ref, gw_ref, gv_ref, gdm_ref):
        wk, wv = _ctx_kv_weights(wi_ref)
        mats = [[dm_ref[d, kind, h] for h in range(NH)] for d in range(2) for kind in (2, 3)]
        ctx0, ctx1 = ctx_ref[0:CH, :], ctx_ref[CH:2 * CH, :]

        def fn(n1, csh, csc, wk_, wv_, zf, zb, ef, eb):
            return _ctx_states(ctx0, ctx1, n1, csh, csc, wk_, wv_, zf, zb, ef, eb)

        _, vjp = jax.vjp(fn, v_ref[0:1, :], v_ref[1:2, :], v_ref[2:3, :], wk, wv,
                         mats[0], mats[2], mats[1], mats[3])
        cot = ([gf_ref[h] for h in range(NH)], [gb_ref[h] for h in range(NH)])
        dn1, dcsh, dcsc, dwk, dwv, dzf, dzb, def_, deb = vjp(cot)
        for h in range(NH):
            gw_ref[:, h * HD:(h + 1) * HD] = dwk[h]
            gw_ref[:, AW + h * HD:AW + (h + 1) * HD] = dwv[h]
        gv_ref[...] = jnp.zeros_like(gv_ref)
        gv_ref[0:1, :] = dn1
        gv_ref[1:2, :] = dcsh
        gv_ref[2:3, :] = dcsc
        for h in range(NH):
            gdm_ref[0, 0, h] = dzf[h]
            gdm_ref[0, 1, h] = def_[h]
            gdm_ref[1, 0, h] = dzb[h]
            gdm_ref[1, 1, h] = deb[h]

    return pl.pallas_call(
        body, name="ctx_backward",
        out_shape=[jax.ShapeDtypeStruct((D, 2 * AW), F32), jax.ShapeDtypeStruct((8, D), F32),
                   jax.ShapeDtypeStruct((2, 2, NH, CH, CH), F32)],
        in_specs=[pl.BlockSpec(memory_space=pltpu.VMEM)] * 6,
        out_specs=[pl.BlockSpec(memory_space=pltpu.VMEM)] * 3,
        compiler_params=_params(56),
    )(ctx, vecs, wi, dm, dscf, dscb)


def _load_w_in(wi_hbm, wcat, sems):
    cps = [pltpu.make_async_copy(wi_hbm.at[j], wcat.at[:, pl.ds(j * WI_C, WI_C)], sems.at[j]) for j in range(NCHIP)]
    for cp in cps:
        cp.start()
    for cp in cps:
        cp.wait()


def _in_proj(x, vecs, wi, gbufs):
    ln = x.shape[0]
    t = min(1024, ln)
    nt = len(gbufs)
    steps = ln // t

    def body(x_ref, v_ref, wi_ref, *refs):
        z_ref, hx_ref = refs[nt:nt + 2]
        bufs = refs[nt + 2:2 * nt + 2]
        wcat, w_sems, send_sems, recv_sems = refs[2 * nt + 2:]
        i = pl.program_id(0)

        @pl.when(i == 0)
        def _():
            for cp in _gather_ici_copies(bufs, send_sems, recv_sems)[0]:
                cp.start()
            _load_w_in(wi_ref, wcat, w_sems)

        xv = x_ref[...]
        hx = (xv * _rms(xv) * v_ref[0:1, :]) * (1.0 + v_ref[2:3, :]) + v_ref[1:2, :]
        hb = hx.astype(BF)
        hx_ref[...] = hb
        z_ref[...] = _dot(hb, wcat[...], NN)

        @pl.when(i == steps - 1)
        def _():
            out_cp, in_cp = _gather_ici_copies(bufs, send_sems, recv_sems)
            for cp in in_cp:
                cp.wait_recv()
            for cp in out_cp:
                cp.wait_send()

    hbm = pl.BlockSpec(memory_space=pl.ANY)
    out = pl.pallas_call(
        body, name="in_proj", grid=(steps,),
        in_specs=[pl.BlockSpec((t, D), lambda i: (i, 0)), _const((8, D)), hbm] + [hbm] * nt,
        out_specs=[pl.BlockSpec((t, IN_COLS), lambda i: (i, 0)), pl.BlockSpec((t, D), lambda i: (i, 0))] + [hbm] * nt,
        out_shape=[jax.ShapeDtypeStruct((ln, IN_COLS), F32), jax.ShapeDtypeStruct((ln, D), BF)]
        + [jax.ShapeDtypeStruct(g.shape, g.dtype) for g in gbufs],
        input_output_aliases={3 + k: 2 + k for k in range(nt)},
        scratch_shapes=[pltpu.VMEM((D, IN_COLS), BF), pltpu.SemaphoreType.DMA((NCHIP,))]
        + [pltpu.SemaphoreType.DMA(((NCHIP - 1) * nt,))] * 2,
        compiler_params=_params(56, ("arbitrary",)),
    )(x, vecs, wi, *gbufs)
    return out[0], out[1], out[2:]


def _allgather_copies(src, out, send_sems, recv_sems, local_sem):
    x, y, c = _pos()
    me = 4 * x + 2 * y + c
    sends, recvs = [], []
    for k in range(1, NDEV):
        kx, ky, kc = (k >> 2) & 1, (k >> 1) & 1, k & 1
        peer = (x ^ kx, y ^ ky, c ^ kc)
        frm = 4 * (x ^ kx) + 2 * (y ^ ky) + (c ^ kc)
        sends.append(pltpu.make_async_remote_copy(src_ref=src, dst_ref=out.at[me], send_sem=send_sems.at[k - 1],
                                                  recv_sem=recv_sems.at[k - 1], device_id=peer, device_id_type=MESH))
        recvs.append(pltpu.make_async_remote_copy(src_ref=src, dst_ref=out.at[frm], send_sem=send_sems.at[k - 1],
                                                  recv_sem=recv_sems.at[k - 1], device_id=peer, device_id_type=MESH))
    return sends, recvs, pltpu.make_async_copy(src, out.at[me], local_sem)


def _in_proj_bwd(dz, x, dx1, vecs, wi, parts, early):
    ln = x.shape[0]
    t = min(1024, ln)
    nt = len(parts)
    steps = ln // t

    def body(dz_ref, x_ref, dx1_ref, v_ref, wi_ref, *refs):
        ps = refs[:nt]
        early_ref = refs[nt]
        gx_ref, acc_ref = refs[nt + 1:nt + 3]
        got = refs[nt + 3:2 * nt + 3]
        early_all = refs[2 * nt + 3]
        wcat, w_sems, send_sems, recv_sems, ag_send, ag_recv, ag_local = refs[2 * nt + 4:]

        @pl.when(pl.program_id(0) == 0)
        def _():
            acc_ref[...] = jnp.zeros_like(acc_ref)
            _load_w_in(wi_ref, wcat, w_sems)
            for cp in _scatter_ici_copies(ps, got, send_sems, recv_sems):
                cp.start()
            sends, _, own = _allgather_copies(early_ref, early_all, ag_send, ag_recv, ag_local)
            own.start()
            for cp in sends:
                cp.start()

        dhx = _dot(dz_ref[...], wcat[...], NT)
        xv = x_ref[...]
        r = _rms(xv)
        xn = xv * r
        n1, sc = v_ref[0:1, :], v_ref[2:3, :]
        acc_ref[0:1, :] += jnp.sum(dhx * xn * (1.0 + sc), axis=0, keepdims=True)
        acc_ref[1:2, :] += jnp.sum(dhx, axis=0, keepdims=True)
        acc_ref[2:3, :] += jnp.sum(dhx * xn * n1, axis=0, keepdims=True)
        g = dhx * n1 * (1.0 + sc)
        gx_ref[...] = dx1_ref[...] + r * (g - xn * jnp.mean(g * xn, axis=-1, keepdims=True))

        @pl.when(pl.program_id(0) == steps - 1)
        def _():
            cps = _scatter_ici_copies(ps, got, send_sems, recv_sems)
            sends, recvs, own = _allgather_copies(early_ref, early_all, ag_send, ag_recv, ag_local)
            for cp in cps + recvs:
                cp.wait_recv()
            for cp in cps + sends:
                cp.wait_send()
            own.wait()

    hbm = pl.BlockSpec(memory_space=pl.ANY)
    out = pl.pallas_call(
        body, name="in_proj_bwd", grid=(steps,),
        in_specs=[pl.BlockSpec((t, IN_COLS), lambda i: (i, 0)), pl.BlockSpec((t, D), lambda i: (i, 0)),
                  pl.BlockSpec((t, D), lambda i: (i, 0)), _const((8, D)), hbm]
        + [hbm] * (nt + 1),
        out_specs=[pl.BlockSpec((t, D), lambda i: (i, 0)), pl.BlockSpec((8, D), lambda i: (0, 0))]
        + [hbm] * (nt + 1),
        out_shape=[jax.ShapeDtypeStruct((ln, D), F32), jax.ShapeDtypeStruct((8, D), F32)]
        + [jax.ShapeDtypeStruct((NCHIP - 1,) + p.shape[1:], BF) for p in parts]
        + [jax.ShapeDtypeStruct((NDEV,) + early.shape, F32)],
        scratch_shapes=[pltpu.VMEM((D, IN_COLS), BF), pltpu.SemaphoreType.DMA((NCHIP,))]
        + [pltpu.SemaphoreType.DMA(((NCHIP - 1) * nt,))] * 2
        + [pltpu.SemaphoreType.DMA((NDEV - 1,))] * 2 + [pltpu.SemaphoreType.DMA],
        compiler_params=_params(56, ("arbitrary",)),
    )(dz, x, dx1, vecs, wi, *parts, early)
    return out[0], out[1], out[2:2 + nt], out[2 + nt]


def _post_mixer(x, ycat, tgt, vecs, wo, wg, wu, wd):
    ln = x.shape[0]
    t = min(256, ln)

    def body(x_ref, y_ref, t_ref, v_ref, wo_ref, wg_ref, wu_ref, wd_ref,
             dx1_ref, dyc_ref, h2_ref, dy_ref, df_ref, act_ref, da_ref, db_ref, acc_ref, a_st, b_st):
        @pl.when(pl.program_id(0) == 0)
        def _():
            acc_ref[...] = jnp.zeros_like(acc_ref)

        g1, n2, sh2, sc2 = v_ref[0:1, :], v_ref[1:2, :], v_ref[2:3, :], v_ref[3:4, :]
        g2, nf = v_ref[4:5, :], v_ref[5:6, :]
        y = _dot(y_ref[...], wo_ref[...], NN)
        x1 = x_ref[...] + g1 * y
        r2 = _rms(x1)
        xn2 = x1 * r2
        t2 = xn2 * n2
        h2b = (t2 * (1.0 + sc2) + sh2).astype(BF)
        h2_ref[...] = h2b
        a = _dot(h2b, wg_ref[...], NT)
        b = _dot(h2b, wu_ref[...], NT)
        a_st[...] = a
        b_st[...] = b
        act = (_silu(a) * b).astype(BF)
        act_ref[...] = act
        f = _dot(act, wd_ref[...], NN)
        x2 = x1 + g2 * f
        r3 = _rms(x2)
        xn3 = x2 * r3
        e = xn3 * nf - t_ref[...]
        acc_ref[6:7, :] += jnp.sum(e * e, axis=0, keepdims=True) * (0.5 / D)
        dout = e * (1.0 / D)
        acc_ref[5:6, :] += jnp.sum(dout * xn3, axis=0, keepdims=True)
        gg = dout * nf
        dx2 = r3 * (gg - xn3 * jnp.mean(gg * xn3, axis=-1, keepdims=True))
        acc_ref[4:5, :] += jnp.sum(dx2 * f, axis=0, keepdims=True)
        dfb = (g2 * dx2).astype(BF)
        df_ref[...] = dfb
        dact = _dot(dfb, wd_ref[...], NT)
        a = a_st[...]
        b = b_st[...]
        s = jax.nn.sigmoid(a)
        da = (dact * b * (s * (1.0 + a * (1.0 - s)))).astype(BF)
        db = (dact * (a * s)).astype(BF)
        da_ref[...] = da
        db_ref[...] = db
        dh2 = _dot(da, wg_ref[...], NN) + _dot(db, wu_ref[...], NN)
        acc_ref[2:3, :] += jnp.sum(dh2, axis=0, keepdims=True)
        acc_ref[3:4, :] += jnp.sum(dh2 * t2, axis=0, keepdims=True)
        acc_ref[1:2, :] += jnp.sum(dh2 * xn2 * (1.0 + sc2), axis=0, keepdims=True)
        gx = dh2 * n2 * (1.0 + sc2)
        dx1 = dx2 + r2 * (gx - xn2 * jnp.mean(gx * xn2, axis=-1, keepdims=True))
        dx1_ref[...] = dx1
        acc_ref[0:1, :] += jnp.sum(dx1 * y, axis=0, keepdims=True)
        dyb = (g1 * dx1).astype(BF)
        dy_ref[...] = dyb
        dyc_ref[...] = _dot(dyb, wo_ref[...], NT)

    tok = pl.BlockSpec((t, D), lambda i: (i, 0))
    ffb = pl.BlockSpec((t, DFF), lambda i: (i, 0))
    return pl.pallas_call(
        body, name="post_mixer", grid=(ln // t,),
        in_specs=[tok, tok, tok, _const((8, D)), _const((D, D)), _const((DFF, D)), _const((DFF, D)),
                  _const((DFF, D))],
        out_specs=[tok, tok, tok, tok, tok, ffb, ffb, ffb, pl.BlockSpec((16, D), lambda i: (0, 0))],
        out_shape=[jax.ShapeDtypeStruct((ln, D), F32)] * 2 + [jax.ShapeDtypeStruct((ln, D), BF)] * 3
        + [jax.ShapeDtypeStruct((ln, DFF), BF)] * 3 + [jax.ShapeDtypeStruct((16, D), F32)],
        scratch_shapes=[pltpu.VMEM((t, DFF), F32)] * 2,
        compiler_params=_params(60, ("arbitrary",)),
    )(x, ycat, tgt, vecs, wo, wg, wu, wd)


def _exchange_copies(g, out, send_sems, recv_sems):
    x, y, c = _pos()
    return [pltpu.make_async_remote_copy(src_ref=g.at[j, 1 - c], dst_ref=out.at[j], send_sem=send_sems.at[j],
                                         recv_sem=recv_sems.at[j], device_id=(x, y, 1 - c), device_id_type=MESH)
            for j in range(NCHIP)]


def _tn_matmul(xa, dy, name, nb, k1, n, x_batched, dy_mode, tt, ctx_kv=None, carry=None):
    ln = xa.shape[-2]
    tt = min(tt, ln)
    steps = ln // tt
    n_in = 2 + (ctx_kv is not None) + (carry is not None)

    def body(x_ref, dy_ref, *refs):
        o_ref = refs[n_in - 2]
        if carry is not None:
            g_ref, got_ref = refs[n_in - 3], refs[n_in - 1]
            send_sems, recv_sems = refs[n_in:]

        @pl.when(pl.program_id(0) == 0)
        def _():
            if carry is not None:
                for cp in _exchange_copies(g_ref, got_ref, send_sems, recv_sems):
                    cp.start()
            o_ref[...] = jnp.zeros_like(o_ref)
            if ctx_kv is not None:
                for g in range(0, 2 * AW, HD):
                    col = 3 * AW + g
                    o_ref[col // n, :, col % n: col % n + HD] = refs[0][:, g:g + HD]

        xt = None if x_batched else jnp.transpose(x_ref[...])
        for b in range(nb):
            lhs = jnp.transpose(x_ref[b]) if x_batched else xt
            if dy_mode == "batched":
                rhs = dy_ref[b]
            elif dy_mode == "cols":
                rhs = dy_ref[:, b * n:(b + 1) * n]
            else:
                rhs = dy_ref[...]
            o_ref[b] += _dot(lhs, rhs, NN)

        if carry is not None:
            @pl.when(pl.program_id(0) == steps - 1)
            def _():
                cps = _exchange_copies(g_ref, got_ref, send_sems, recv_sems)
                for cp in cps:
                    cp.wait_recv()
                for cp in cps:
                    cp.wait_send()

    x_spec = (pl.BlockSpec((nb, tt, k1), lambda t: (0, t, 0)) if x_batched
              else pl.BlockSpec((tt, k1), lambda t: (t, 0)))
    if dy_mode == "batched":
        dy_spec = pl.BlockSpec((nb, tt, n), lambda t: (0, t, 0))
    elif dy_mode == "cols":
        dy_spec = pl.BlockSpec((tt, nb * n), lambda t: (t, 0))
    else:
        dy_spec = pl.BlockSpec((tt, n), lambda t: (t, 0))
    hbm = pl.BlockSpec(memory_space=pl.ANY)
    extra = [] if ctx_kv is None else [ctx_kv]
    in_specs = [x_spec, dy_spec] + [_const(e.shape) for e in extra]
    out_specs = [pl.BlockSpec((nb, k1, n), lambda t: (0, 0, 0))]
    out_shape = [jax.ShapeDtypeStruct((nb, k1, n), F32)]
    scratch = []
    if carry is not None:
        extra = extra + [carry]
        in_specs.append(hbm)
        out_specs.append(hbm)
        out_shape.append(jax.ShapeDtypeStruct((NCHIP,) + carry.shape[2:], F32))
        scratch = [pltpu.SemaphoreType.DMA((NCHIP,))] * 2
    out = pl.pallas_call(
        body, name=name, grid=(steps,),
        in_specs=in_specs, out_specs=out_specs, out_shape=out_shape, scratch_shapes=scratch,
        compiler_params=_params(60, ("arbitrary",)),
    )(xa, dy, *extra)
    return out[0] if carry is None else (out[0], out[1])


FWD_CHUNKS_PER_STEP = 4
BWD_CHUNKS_PER_STEP = 4


def _chunks_per_step(nc, want):
    return want if nc % want == 0 else 1


def _mixer_fwd(z, cos_t, sin_t, dm, sgw, gain, bfull, scf, scb, gbufs_a, gbufs_b):
    ln = z.shape[0]
    nc = ln // CH
    na = len(gbufs_a)
    gbufs = list(gbufs_a) + list(gbufs_b)
    nt = len(gbufs)
    cps = _chunks_per_step(nc, FWD_CHUNKS_PER_STEP)
    nb = nc // cps
    rows = cps * CH
    mid = nb // 2

    def rev(p, n):
        return p * n + (1 - p) * (nb - 1 - n)

    def col(j, both):
        if both:
            return pl.BlockSpec((rows, AW), lambda p, n: (rev(p, n), j))
        return pl.BlockSpec((rows, AW), lambda p, n: (p * n, j))

    def body(u_ref, v_ref, q_ref, k_ref, vr_ref, gf_ref, gb_ref, cos_ref, sin_ref, dm_ref, sgw_ref, gain_ref,
             bfull_ref, scf_ref, scb_ref, *refs):
        y_ref, sf_ref, sb_ref, of_ref = refs[nt:nt + 4]
        bufs = refs[nt + 4:2 * nt + 4]
        bufs_a, bufs_b = bufs[:na], bufs[na:]
        sb_all, st, a_send, a_recv, bi_send, bi_recv, bd_send, bd_recv = refs[2 * nt + 4:]
        p, n = pl.program_id(0), pl.program_id(1)

        @pl.when((p == 0) & (n == 0))
        def _():
            for cp in _gather_d2d_copies(bufs_a, a_send, a_recv)[0]:
                cp.start()
            for cp in _gather_ici_copies(bufs_b, bi_send, bi_recv)[0]:
                cp.start()

        @pl.when((p == 1) & (n == mid))
        def _():
            for cp in _gather_ici_copies(bufs_b, bi_send, bi_recv)[1]:
                cp.wait_recv()
            for cp in _gather_d2d_copies(bufs_b, bd_send, bd_recv)[0]:
                cp.start()

        def roped_k(r0):
            cos, sin = cos_ref[r0:r0 + CH, :], sin_ref[r0:r0 + CH, :]
            return [_rope(t, cos, sin) * K_SCALE for t in _heads(k_ref, r0)]

        @pl.when(p == 0)
        def _():
            @pl.when(n == 0)
            def _():
                st[...] = scb_ref[...]

            for s in reversed(range(cps)):
                m = (nb - 1 - n) * cps + s
                k, vr = roped_k(s * CH), _heads(vr_ref, s * CH)
                for h in range(NH):
                    sb_all[m, h] = st[h]
                    st[h] = dm_ref[1, 3, h] * st[h] + _mm_tn(k[h], dm_ref[1, 2, h] * vr[h])

        @pl.when(p == 1)
        def _():
            @pl.when(n == 0)
            def _():
                st[...] = scf_ref[...]

            mats = [[dm_ref[d, kind, h] for h in range(NH)] for d in range(2) for kind in range(3)]
            for s in range(cps):
                r0 = s * CH
                m = n * cps + s
                cos, sin = cos_ref[r0:r0 + CH, :], sin_ref[r0:r0 + CH, :]
                q = [_rope(t, cos, sin) for t in _heads(q_ref, r0)]
                k, vr = roped_k(r0), _heads(vr_ref, r0)
                u, v, gf, gb = _heads(u_ref, r0), _heads(v_ref, r0), _heads(gf_ref, r0), _heads(gb_ref, r0)
                cols = [slice(h * HD, (h + 1) * HD) for h in range(NH)]
                ya = [_gate_group(u[h], v[h], sgw_ref[h], gain_ref[:, cols[h]], bfull_ref[h]) for h in range(NH)]
                sf = [st[h] for h in range(NH)]
                sb = [sb_all[m, h] for h in range(NH)]
                ret = [_ret_head(q[h], k[h], vr[h], gf[h], gb[h], sf[h], sb[h], mats[0][h], mats[1][h], mats[2][h],
                                 mats[3][h], mats[4][h], mats[5][h]) for h in range(NH)]
                for h in range(NH):
                    yr, uf, _, of = ret[h]
                    y_ref[r0:r0 + CH, cols[h]] = ya[h].astype(BF)
                    y_ref[r0:r0 + CH, AW + h * HD:AW + (h + 1) * HD] = yr.astype(BF)
                    of_ref[r0:r0 + CH, cols[h]] = of
                    sf_ref[s, h] = sf[h]
                    sb_ref[s, h] = sb[h]
                    st[h] = dm_ref[0, 3, h] * sf[h] + uf

        @pl.when((p == 1) & (n == nb - 1))
        def _():
            a_out, a_in = _gather_d2d_copies(bufs_a, a_send, a_recv)
            b_out, b_in = _gather_d2d_copies(bufs_b, bd_send, bd_recv)
            for cp in a_in + b_in:
                cp.wait_recv()
            for cp in a_out + b_out + _gather_ici_copies(bufs_b, bi_send, bi_recv)[0]:
                cp.wait_send()

    hbm = pl.BlockSpec(memory_space=pl.ANY)
    tab = pl.BlockSpec((rows, HD), lambda p, n: (rev(p, n), 0))
    st_spec = pl.BlockSpec((cps, NH, HD, HD), lambda p, n: (p * n, 0, 0, 0))
    out = pl.pallas_call(
        body, name="mixer_fwd", grid=(2, nb),
        in_specs=[col(0, False), col(1, False), col(2, False), col(3, True), col(4, True), col(5, False),
                  col(6, False), tab, tab, _const((2, 4, NH, CH, CH)), _const((NH, CH, CH)), _const((1, AW)),
                  _const((NH, CH, CH)), _const((NH, HD, HD)), _const((NH, HD, HD))] + [hbm] * nt,
        out_specs=[pl.BlockSpec((rows, D), lambda p, n: (p * n, 0)), st_spec, st_spec,
                   pl.BlockSpec((rows, AW), lambda p, n: (p * n, 0))] + [hbm] * nt,
        out_shape=[jax.ShapeDtypeStruct((ln, D), BF), jax.ShapeDtypeStruct((nc, NH, HD, HD), F32),
                   jax.ShapeDtypeStruct((nc, NH, HD, HD), F32), jax.ShapeDtypeStruct((ln, AW), F32)]
        + [jax.ShapeDtypeStruct(g.shape, g.dtype) for g in gbufs],
        input_output_aliases={15 + k: 4 + k for k in range(nt)},
        scratch_shapes=[pltpu.VMEM((nc, NH, HD, HD), F32), pltpu.VMEM((NH, HD, HD), F32)]
        + [pltpu.SemaphoreType.DMA(((NCHIP - 1) * na,))] * 2
        + [pltpu.SemaphoreType.DMA(((NCHIP - 1) * (nt - na),))] * 4,
        compiler_params=_params(56, ("arbitrary", "arbitrary")),
    )(z, z, z, z, z, z, z, cos_t, sin_t, dm, sgw, gain, bfull, scf, scb, *gbufs)
    return out[0], out[1], out[2], out[3], out[4:]


def _mixer_bwd(z, dycat, of_all, cos_t, sin_t, dm, sgw, gain, bfull, sf_all, sb_all, parts):
    ln = z.shape[0]
    nc = ln // CH
    cps = _chunks_per_step(nc, BWD_CHUNKS_PER_STEP)
    nb = nc // cps
    rows = cps * CH

    def rev(p, n):
        return p * n + (1 - p) * (nb - 1 - n)

    def col(j, both):
        if both:
            return pl.BlockSpec((rows, AW), lambda p, n: (rev(p, n), j))
        return pl.BlockSpec((rows, AW), lambda p, n: (p * n, j))

    nt = len(parts)

    def body(u_ref, v_ref, q_ref, k_ref, vr_ref, gf_ref, gb_ref, dya_ref, dyr_ref, of_ref, cos_ref, sin_ref,
             dm_ref, sgw_ref, gain_ref, bfull_ref, sf_ref, sb_ref, *refs):
        ps = refs[:nt]
        dz_ref, ddm_ref, dsgw_ref, dgain_ref, dbf_ref, dscf_ref, dscb_ref = refs[nt:nt + 7]
        got = refs[nt + 7:2 * nt + 7]
        gf_all, run, send_sems, recv_sems = refs[2 * nt + 7:]
        p, n = pl.program_id(0), pl.program_id(1)

        @pl.when((p == 0) & (n == 0))
        def _():
            for cp in _scatter_ici_copies(ps, got, send_sems, recv_sems):
                cp.start()

        mats = [[dm_ref[d, kind, h] for h in range(NH)] for d in range(2) for kind in range(3)]

        @pl.when(p == 0)
        def _():
            @pl.when(n == 0)
            def _():
                run[...] = jnp.zeros_like(run)
                ddm_ref[...] = jnp.zeros_like(ddm_ref)
                dsgw_ref[...] = jnp.zeros_like(dsgw_ref)
                dgain_ref[...] = jnp.zeros_like(dgain_ref)
                dbf_ref[...] = jnp.zeros_like(dbf_ref)

            for s in reversed(range(cps)):
                r0 = s * CH
                m = (nb - 1 - n) * cps + s
                cos, sin = cos_ref[r0:r0 + CH, :], sin_ref[r0:r0 + CH, :]
                q = [_rope(t, cos, sin) for t in _heads(q_ref, r0)]
                gf, dyr, of = _heads(gf_ref, r0), _heads(dyr_ref, r0), _heads(of_ref, r0)
                for h in range(NH):
                    _, vjp = jax.vjp(functools.partial(_gated_norm, gf[h]), of[h])
                    (dof,) = vjp(dyr[h])
                    dsf = _mm_tn(q[h], mats[1][h] * dof)
                    g_next = run[h]
                    gf_all[m, h] = g_next.astype(BF)
                    ddm_ref[0, 3, h] += sf_ref[s, h] * g_next
                    run[h] = dsf + dm_ref[0, 3, h] * g_next

            @pl.when(n == nb - 1)
            def _():
                dscf_ref[...] = run[...]

        @pl.when(p == 1)
        def _():
            @pl.when(n == 0)
            def _():
                run[...] = jnp.zeros_like(run)

            for s in range(cps):
                r0 = s * CH
                m = n * cps + s
                rw = slice(r0, r0 + CH)
                cos, sin = cos_ref[r0:r0 + CH, :], sin_ref[r0:r0 + CH, :]
                q = [_rope(t, cos, sin) for t in _heads(q_ref, r0)]
                k = [_rope(t, cos, sin) * K_SCALE for t in _heads(k_ref, r0)]
                vr, gf, gb, dyr = _heads(vr_ref, r0), _heads(gf_ref, r0), _heads(gb_ref, r0), _heads(dyr_ref, r0)
                u, v, dya = _heads(u_ref, r0), _heads(v_ref, r0), _heads(dya_ref, r0)
                sf = [sf_ref[s, h] for h in range(NH)]
                sb = [sb_ref[s, h] for h in range(NH)]
                g_f = [gf_all[m, h].astype(F32) for h in range(NH)]
                g_b = [run[h] for h in range(NH)]
                cols = [slice(h * HD, (h + 1) * HD) for h in range(NH)]

                def chunk(u_, v_, sgw_, gain_, bfull_, q_, k_, vr_, gf_, gb_, sb_, df, xf, zf, db, xb, zb, sf=sf):
                    ya = [_gate_group(u_[h], v_[h], sgw_[h], gain_[h], bfull_[h]) for h in range(NH)]
                    ret = [_ret_head(q_[h], k_[h], vr_[h], gf_[h], gb_[h], sf[h], sb_[h], df[h], xf[h], zf[h],
                                     db[h], xb[h], zb[h])[:3] for h in range(NH)]
                    return ya, ret

                _, vjp = jax.vjp(chunk, u, v, [sgw_ref[h] for h in range(NH)], [gain_ref[:, c] for c in cols],
                                 [bfull_ref[h] for h in range(NH)], q, k, vr, gf, gb, sb, *mats)
                (du, dv, dsgw, dgain, dbf, dq, dk, dvr, dgf, dgb, dsb, ddf, dxf, dzf, ddb, dxb, dzb) = vjp(
                    (dya, [(dyr[h], g_f[h], g_b[h]) for h in range(NH)]))
                for h in range(NH):
                    dz_ref[rw, h * HD:(h + 1) * HD] = du[h].astype(BF)
                    dz_ref[rw, AW + h * HD:AW + (h + 1) * HD] = dv[h].astype(BF)
                    dz_ref[rw, 2 * AW + h * HD:2 * AW + (h + 1) * HD] = _rope_bwd(dq[h], cos, sin).astype(BF)
                    dz_ref[rw, 3 * AW + h * HD:3 * AW + (h + 1) * HD] = _rope_bwd(dk[h] * K_SCALE, cos,
                                                                                  sin).astype(BF)
                    dz_ref[rw, 4 * AW + h * HD:4 * AW + (h + 1) * HD] = dvr[h].astype(BF)
                    dz_ref[rw, 5 * AW + h * HD:5 * AW + (h + 1) * HD] = dgf[h].astype(BF)
                    dz_ref[rw, 6 * AW + h * HD:6 * AW + (h + 1) * HD] = dgb[h].astype(BF)
                    ddm_ref[0, 0, h] += ddf[h]
                    ddm_ref[0, 1, h] += dxf[h]
                    ddm_ref[0, 2, h] += dzf[h]
                    ddm_ref[1, 0, h] += ddb[h]
                    ddm_ref[1, 1, h] += dxb[h]
                    ddm_ref[1, 2, h] += dzb[h]
                    ddm_ref[1, 3, h] += sb[h] * g_b[h]
                    dsgw_ref[h] += dsgw[h]
                    dgain_ref[:, cols[h]] += dgain[h]
                    dbf_ref[h] += dbf[h]
                    run[h] = dsb[h] + dm_ref[1, 3, h] * g_b[h]

            @pl.when(n == nb - 1)
            def _():
                dscb_ref[...] = run[...]

        @pl.when((p == 1) & (n == nb - 1))
        def _():
            cps_ = _scatter_ici_copies(ps, got, send_sems, recv_sems)
            for cp in cps_:
                cp.wait_recv()
            for cp in cps_:
                cp.wait_send()

    hbm = pl.BlockSpec(memory_space=pl.ANY)
    tab = pl.BlockSpec((rows, HD), lambda p, n: (rev(p, n), 0))
    tile4 = jax.ShapeDtypeStruct((NH, CH, CH), F32)
    out = pl.pallas_call(
        body, name="mixer_bwd", grid=(2, nb),
        in_specs=[col(0, False), col(1, False), col(2, True), col(3, False), col(4, False), col(5, True),
                  col(6, False),
                  pl.BlockSpec((rows, AW), lambda p, n: (p * n, 0)),
                  pl.BlockSpec((rows, AW), lambda p, n: (rev(p, n), 1)),
                  pl.BlockSpec((rows, AW), lambda p, n: ((1 - p) * (nb - 1 - n), 0)),
                  tab, tab, _const((2, 4, NH, CH, CH)), _const((NH, CH, CH)), _const((1, AW)),
                  _const((NH, CH, CH)),
                  pl.BlockSpec((cps, NH, HD, HD), lambda p, n: (rev(p, n), 0, 0, 0)),
                  pl.BlockSpec((cps, NH, HD, HD), lambda p, n: (p * n, 0, 0, 0))] + [hbm] * nt,
        out_specs=[pl.BlockSpec((rows, IN_COLS), lambda p, n: (p * n, 0)),
                   pl.BlockSpec((2, 4, NH, CH, CH), lambda p, n: (0, 0, 0, 0, 0)),
                   pl.BlockSpec((NH, CH, CH), lambda p, n: (0, 0, 0)),
                   pl.BlockSpec((1, AW), lambda p, n: (0, 0)),
                   pl.BlockSpec((NH, CH, CH), lambda p, n: (0, 0, 0)),
                   pl.BlockSpec((NH, HD, HD), lambda p, n: (0, 0, 0)),
                   pl.BlockSpec((NH, HD, HD), lambda p, n: (0, 0, 0))] + [hbm] * nt,
        out_shape=[jax.ShapeDtypeStruct((ln, IN_COLS), BF), jax.ShapeDtypeStruct((2, 4, NH, CH, CH), F32),
                   tile4, jax.ShapeDtypeStruct((1, AW), F32), tile4, tile4, tile4]
        + [jax.ShapeDtypeStruct((NCHIP - 1,) + p.shape[1:], BF) for p in parts],
        scratch_shapes=[pltpu.VMEM((nc, NH, HD, HD), BF), pltpu.VMEM((NH, HD, HD), F32)]
        + [pltpu.SemaphoreType.DMA(((NCHIP - 1) * nt,))] * 2,
        compiler_params=_params(60, ("arbitrary", "arbitrary")),
    )(z, z, z, z, z, z, z, dycat, dycat, of_all, cos_t, sin_t, dm, sgw, gain, bfull, sf_all, sb_all, *parts)
    return out[:7], out[7:]


def _small_reduce(ddm, ddm_ctx, dm, dbf):
    def body(ddm_ref, dctx_ref, dm_ref, dbf_ref, lg_ref, sgb_ref):
        ex = _decay_exponents()
        ones = jnp.ones((8, CH), F32)
        for d in range(2):
            for h in range(NH):
                tot = jnp.zeros((CH, CH), F32)
                for kind in range(4):
                    g = ddm_ref[d, kind, h]
                    if kind >= 2:
                        g = g + dctx_ref[d, kind - 2, h]
                    tot = tot + g * dm_ref[d, kind, h] * ex[d][kind]
                lg_ref[d * NH + h: d * NH + h + 1, :] = jnp.sum(tot, axis=0, keepdims=True)
        sgb_ref[...] = jnp.zeros_like(sgb_ref)
        for g in range(NH):
            r = lax.dot_general(ones, dbf_ref[g], (NT, ((), ())), precision=HI, preferred_element_type=F32)
            sgb_ref[g:g + 1, :] = r[0:1, :]

    return pl.pallas_call(
        body, name="small_reduce",
        out_shape=[jax.ShapeDtypeStruct((8, CH), F32), jax.ShapeDtypeStruct((8, CH), F32)],
        in_specs=[pl.BlockSpec(memory_space=pltpu.VMEM)] * 4,
        out_specs=[pl.BlockSpec(memory_space=pltpu.VMEM)] * 2,
        compiler_params=_params(32),
    )(ddm, ddm_ctx, dm, dbf)


def _mod_backward(ct_pad_t, cctx_col, dmod_pad, dcmod_cols, w_mod_s):
    def body(ct_ref, cc_ref, dm_ref, dc_ref, w_ref, gw_ref, part_ref):
        dcm = dc_ref[0:1, :]
        for d in range(1, NDEV):
            dcm = dcm + dc_ref[d:d + 1, :]
        gw_ref[...] = (jnp.dot(_silu(ct_ref[...]), dm_ref[...], precision=HI, preferred_element_type=F32)
                       + _silu(cc_ref[...]) * dcm)
        part_ref[...] = lax.dot_general(jnp.broadcast_to(dcm, (8, dcm.shape[1])), w_ref[...], (NT, ((), ())),
                                        precision=HI, preferred_element_type=F32)

    return pl.pallas_call(
        body, name="mod_backward",
        out_shape=[jax.ShapeDtypeStruct(w_mod_s.shape, F32), jax.ShapeDtypeStruct((8, D), F32)],
        in_specs=[pl.BlockSpec(memory_space=pltpu.VMEM)] * 5,
        out_specs=[pl.BlockSpec(memory_space=pltpu.VMEM)] * 2,
        compiler_params=_params(48),
    )(ct_pad_t, cctx_col, dmod_pad, dcmod_cols, w_mod_s)


def _cctx_update(parts, c_ctx, m, v):
    def body(p_ref, c_ref, m_ref, v_ref, g_ref, d_ref, mo_ref, vo_ref):
        tot = ((p_ref[0] + p_ref[2]) + p_ref[4]) + p_ref[6]
        cv = c_ref[...]
        s = jax.nn.sigmoid(cv)
        g = tot * (s * (1.0 + cv * (1.0 - s)))
        g_ref[...] = g
        d_ref[...], mo_ref[...], vo_ref[...] = _adamw_math(cv, g, m_ref[...], v_ref[...])

    return pl.pallas_call(
        body, name="cctx_update",
        out_shape=[jax.ShapeDtypeStruct((1, D), F32)] * 4,
        in_specs=[pl.BlockSpec(memory_space=pltpu.VMEM)] * 4,
        out_specs=[pl.BlockSpec(memory_space=pltpu.VMEM)] * 4,
        compiler_params=_params(16),
    )(parts, c_ctx, m, v)


def _small_update(gathered, wp, mp, vp):
    def body(g_ref, w_ref, m_ref, v_ref, go_ref, d_ref, mo_ref, vo_ref, loss_ref):
        tot = g_ref[0]
        for d in range(1, NDEV):
            tot = tot + g_ref[d]
        go_ref[Q_BMOD:Q_N1, :] = tot[P_DMOD:P_N1, :] + tot[P_DCMOD:P_DMOD, :]
        go_ref[Q_N1:Q_LG, :] = tot[P_N1:P_LG, :]
        lg = jnp.sum(tot[P_LG:P_N2, :], axis=1, keepdims=True)
        go_ref[Q_LG:Q_N2, :] = lg * jax.nn.sigmoid(-w_ref[Q_LG:Q_N2, :])
        go_ref[Q_N2:Q_ROWS, :] = tot[P_N2:P_LOSS, :]
        d_ref[...], mo_ref[...], vo_ref[...] = _adamw_math(w_ref[...], go_ref[...], m_ref[...], v_ref[...])
        ls = jnp.sum(jnp.sum(tot[P_LOSS:P_ROWS, :], axis=1, keepdims=True), axis=0, keepdims=True)
        loss_ref[...] = jnp.broadcast_to(ls, (8, CH))

    return pl.pallas_call(
        body, name="small_update",
        out_shape=[jax.ShapeDtypeStruct((Q_ROWS, CH), F32)] * 4 + [jax.ShapeDtypeStruct((8, CH), F32)],
        in_specs=[pl.BlockSpec(memory_space=pltpu.VMEM)] * 4,
        out_specs=[pl.BlockSpec(memory_space=pltpu.VMEM)] * 5,
        compiler_params=_params(32),
    )(gathered, wp, mp, vp)


def _rows(a):
    r = a.reshape(-1, CH)
    return jnp.pad(r, ((0, -r.shape[0] % 8), (0, 0)))


def _pack_small(b_mod, norm1, sg_gain, sg_w, sg_b, lf, lb, norm2, norm_f):
    lg = jnp.broadcast_to(jnp.concatenate([lf.reshape(NH), lb.reshape(NH)])[:, None], (2 * NH, CH))
    return jnp.concatenate([_rows(b_mod), _rows(norm1), _rows(sg_gain), _rows(sg_w), _rows(sg_b), lg,
                            _rows(norm2), _rows(norm_f)], axis=0)


def _unpack_small(p):
    return (p[Q_BMOD:Q_N1].reshape(1, 6 * D), p[Q_N1:Q_GAIN].reshape(1, D), p[Q_GAIN:Q_GAIN + NH].reshape(1, AW),
            p[Q_SGW:Q_SGB].reshape(1, NH, CH, CH), p[Q_SGB:Q_SGB + NH].reshape(1, NH, CH),
            p[Q_LG:Q_LG + NH, 0].reshape(1, NH), p[Q_LG + NH:Q_N2, 0].reshape(1, NH),
            p[Q_N2:Q_NF].reshape(1, D), p[Q_NF:Q_ROWS].reshape(D))


def _rope_tables(ln):
    pos = np.arange(ln)
    rows = (pos // GRID_W).astype(np.float32)
    cols = (pos % GRID_W).astype(np.float32)
    n_freq = HD // 4
    inv = (np.float32(ROPE_BASE) ** (-np.arange(n_freq, dtype=np.float32) / np.float32(n_freq))).astype(np.float32)
    ar = rows[:, None] * inv[None, :]
    ac = cols[:, None] * inv[None, :]
    cos_t = np.concatenate([np.cos(ar), np.cos(ar), np.cos(ac), np.cos(ac)], axis=1).astype(np.float32)
    sin_t = np.concatenate([-np.sin(ar), np.sin(ar), -np.sin(ac), np.sin(ac)], axis=1).astype(np.float32)
    return jnp.asarray(cos_t), jnp.asarray(sin_t)


def kernel(x, c, ctx, c_ctx, w_mod, b_mod, norm1, w_in, sg_gain, sg_w, sg_b, ret_logit_f, ret_logit_b, w_out, norm2, w_gate, w_up, w_down, norm_f, loss_target, m_c_ctx, m_w_mod, m_b_mod, m_norm1, m_w_in, m_sg_gain, m_sg_w, m_sg_b, m_ret_logit_f, m_ret_logit_b, m_w_out, m_norm2, m_w_gate, m_w_up, m_w_down, m_norm_f, v_c_ctx, v_w_mod, v_b_mod, v_norm1, v_w_in, v_sg_gain, v_sg_w, v_sg_b, v_ret_logit_f, v_ret_logit_b, v_w_out, v_norm2, v_w_gate, v_w_up, v_w_down, v_norm_f):
    ln = x.shape[1]
    xi, yi, ci = _pos()
    chip = 2 * xi + yi
    me = 4 * xi + 2 * yi + ci
    x2d = x.reshape(ln, D)
    tgt = loss_target.reshape(ln, D)
    mod_c = w_mod.shape[2]

    tr = lambda a: jnp.swapaxes(a[0], 0, 1)
    gbufs, c_all, prod_all = _prologue(c, c_ctx.reshape(1, D), w_mod[0],
                                       [w_in[0], w_out[0], tr(w_gate), tr(w_up), w_down[0]])
    wi = gbufs[0].reshape(NCHIP, D, WI_C)
    gbufs_a, gbufs_b = gbufs[1:3], gbufs[3:5]
    c_all = c_all.reshape(NDEV, D)
    prod_chips = prod_all[0::2]
    mod_rows = jnp.transpose(prod_chips, (1, 0, 2)).reshape(16, NCHIP * mod_c) + b_mod
    mod = lax.dynamic_slice_in_dim(mod_rows, me, 1, axis=0)
    cmod = mod_rows[8:9]
    sh1, sc1, g1, sh2, sc2, g2 = [mod[:, i * D:(i + 1) * D] for i in range(6)]
    csh1, csc1 = cmod[:, 0:D], cmod[:, D:2 * D]
    zrow = jnp.zeros((1, D), F32)
    vec_in = jnp.concatenate([norm1, sh1, sc1] + [zrow] * 5, axis=0)
    vec_ctx = jnp.concatenate([norm1, csh1, csc1] + [zrow] * 5, axis=0)
    vec_post = jnp.concatenate([g1, norm2, sh2, sc2, g2, norm_f.reshape(1, D), zrow, zrow], axis=0)

    logits = jnp.concatenate([ret_logit_f.reshape(NH), ret_logit_b.reshape(NH)])
    dm = _decay_mats(jnp.broadcast_to(logits[:, None, None], (2 * NH, CH, CH)))
    ctx2d = ctx.reshape(ctx.shape[1], D)
    scf, scb = _ctx_forward(ctx2d, vec_ctx, wi, dm)

    cos_t, sin_t = _rope_tables(ln)
    z, hx, gbufs_a = _in_proj(x2d, vec_in, wi, gbufs_a)
    bfull = jnp.broadcast_to(sg_b[0][:, :, None], (NH, CH, CH))
    ycat, sf_all, sb_all, of_all, gbufs = _mixer_fwd(z, cos_t, sin_t, dm, sg_w[0], sg_gain, bfull, scf, scb,
                                             gbufs_a, gbufs_b)
    wo, wg_t, wu_t, wd = [g.reshape(-1, D) for g in gbufs]

    dx1, dycat, h2, dy, df, act, da, db, acc_post = _post_mixer(x2d, ycat, tgt, vec_post, wo, wg_t, wu_t, wd)

    cidx = ci.reshape(1).astype(jnp.int32)
    where = jnp.stack([ci, chip]).astype(jnp.int32)

    def halves_summed(full, names):
        full = [g.reshape(NCHIP, 2, g.shape[1] // 2, g.shape[2]) for g in full]
        from_sib = _rs_exchange_halves(full, "rs_exchange_" + names[0])
        return [_rs_add_halves(g, r, where, "rs_add_halves_" + nm) for g, r, nm in zip(full, from_sib, names)]

    def split(g):
        return g.reshape(NCHIP, 2, g.shape[1] // (2 * NCHIP), g.shape[2])

    g_wd = split(_tn_matmul(act, df, "grad_w_down", 1, DFF, D, False, "shared", 1024))
    g_wu, x_wd = _tn_matmul(db, h2, "grad_w_up", 1, DFF, D, False, "shared", 1024, carry=g_wd)
    g_wu = split(g_wu)
    g_wg, x_wu = _tn_matmul(da, h2, "grad_w_gate", 1, DFF, D, False, "shared", 1024, carry=g_wu)
    g_wg = split(g_wg)
    g_wo, x_wg = _tn_matmul(ycat, dy, "grad_w_out", 1, D, D, False, "shared", 2048, carry=g_wg)
    g_wo = split(g_wo)
    x_wo = _rs_exchange_halves([g_wo], "rs_exchange_w_out")[0]
    names = ["w_in", "w_out", "w_gate", "w_up", "w_down"]
    sums_b = [_rs_add_halves(g, r, where, "rs_add_halves_" + nm)
              for g, r, nm in zip([g_wo, g_wg, g_wu, g_wd], [x_wo, x_wg, x_wu, x_wd], names[1:])]

    (dz, ddm, dsgw, dgain, dbf, dscf, dscb), from_chips_b = _mixer_bwd(
        z, dycat, of_all, cos_t, sin_t, dm, sg_w[0], sg_gain, bfull, sf_all, sb_all, [s[1] for s in sums_b])
    gwkv, acc_ctx, ddm_ctx = _ctx_backward(ctx2d, vec_ctx, wi, dm, dscf, dscb)
    g_wi = _tn_matmul(hx, dz, "grad_w_in", NCHIP, D, WI_C, False, "cols", 1024, ctx_kv=gwkv)
    sums_a = halves_summed([g_wi], names[:1])
    lg_part, dsgb = _small_reduce(ddm, ddm_ctx, dm, dbf)
    dcmod = jnp.concatenate([acc_ctx[1:2], acc_ctx[2:3], jnp.zeros((1, 4 * D), F32)], axis=1)
    dmod_rest = jnp.concatenate([acc_post[0:1], acc_post[2:3], acc_post[3:4], acc_post[4:5]], axis=1)
    early = jnp.concatenate([_rows(dcmod), _rows(dmod_rest), _rows(dgain), _rows(dsgw), dsgb, lg_part,
                             _rows(acc_post[1:2]), _rows(acc_post[5:6]), _rows(acc_post[6:7])], axis=0)
    gx, acc_in, from_chips_a, early_all = _in_proj_bwd(dz, x2d, dx1, vec_in, wi, [s[1] for s in sums_a], early)

    sums = sums_a + sums_b
    from_chips = list(from_chips_a) + list(from_chips_b)
    finals = [_rs_add_chips(s[0], r, "rs_add_chips_" + nm) for s, r, nm in zip(sums, from_chips, names)]
    others = _rs_share_final(finals)

    late = jnp.concatenate([_rows(acc_in[1:2]), _rows(acc_in[2:3]), _rows(acc_in[0:1] + acc_ctx[0:1])], axis=0)
    late_all = _allgather_small(late, "gather_small")
    n_dc, n_l = P_DMOD - P_DCMOD, 16
    gathered = jnp.concatenate([early_all[:, :n_dc], late_all[:, :n_l], early_all[:, n_dc:n_dc + 32],
                                late_all[:, n_l:], early_all[:, n_dc + 32:]], axis=1)
    dmod_all = gathered[:, P_DMOD:P_N1].reshape(NDEV, 6 * D)
    dcmod_all = gathered[:, P_DCMOD:P_DMOD].reshape(NDEV, 6 * D)
    dmod_cols = lax.dynamic_slice_in_dim(dmod_all, chip * mod_c, mod_c, axis=1)
    dcmod_cols = lax.dynamic_slice_in_dim(dcmod_all, chip * mod_c, mod_c, axis=1)
    dmod_pad = jnp.concatenate([dmod_cols, jnp.zeros((CH - NDEV, mod_c), F32)], axis=0)
    ct_pad_t = jnp.concatenate([jnp.transpose(c_all), jnp.zeros((D, CH - NDEV), F32)], axis=1)
    g_wmod, cctx_part = _mod_backward(ct_pad_t, c_ctx.reshape(D, 1), dmod_pad, dcmod_cols, w_mod[0])
    parts = _allgather_small(cctx_part[0:1], "gather_cctx")
    g_cctx, d_cctx, nm_cctx, nv_cctx = _cctx_update(parts, c_ctx.reshape(1, D), m_c_ctx.reshape(1, D),
                                                    v_c_ctx.reshape(1, D))

    wp = _pack_small(b_mod, norm1, sg_gain, sg_w, sg_b, ret_logit_f, ret_logit_b, norm2, norm_f)
    mp = _pack_small(m_b_mod, m_norm1, m_sg_gain, m_sg_w, m_sg_b, m_ret_logit_f, m_ret_logit_b, m_norm2, m_norm_f)
    vp = _pack_small(v_b_mod, v_norm1, v_sg_gain, v_sg_w, v_sg_b, v_ret_logit_f, v_ret_logit_b, v_norm2, v_norm_f)
    gp, dp, mp2, vp2, loss_t = _small_update(gathered, wp, mp, vp)

    big_w = [w_in[0], w_out[0], tr(w_gate), tr(w_up), w_down[0]]
    big_m = [m_w_in[0], m_w_out[0], tr(m_w_gate), tr(m_w_up), m_w_down[0]]
    big_v = [v_w_in[0], v_w_out[0], tr(v_w_gate), tr(v_w_up), v_w_down[0]]
    upd = [_adamw_halves(w, own, oth, m, v, cidx, "adamw_" + nm) for w, own, oth, m, v, nm in
           zip(big_w, finals, others, big_m, big_v, names)]
    big_g = [g_wmod] + [u[0] for u in upd]
    big = [_adamw(w_mod[0], g_wmod, m_w_mod[0], v_w_mod[0], "adamw_w_mod")] + [u[1:] for u in upd]

    def assemble(small, cctx, bigs):
        b_mod_, norm1_, gain_, sgw_, sgb_, lf_, lb_, norm2_, normf_ = _unpack_small(small)
        wm, wi_, wo_, wg_, wu_, wd_ = [b[None] for b in bigs]
        wg_, wu_ = jnp.swapaxes(wg_, 1, 2), jnp.swapaxes(wu_, 1, 2)
        return [cctx.reshape(D), wm, b_mod_, norm1_, wi_, gain_, sgw_, sgb_, lf_, lb_, wo_, norm2_, wg_, wu_, wd_,
                normf_]

    out = [loss_t[0, 0], gx.reshape(1, ln, D)]
    out += assemble(gp, g_cctx, big_g)
    out += assemble(dp, d_cctx, [b[0] for b in big])
    out += assemble(mp2, nm_cctx, [b[1] for b in big])
    out += assemble(vp2, nv_cctx, [b[2] for b in big])
    return tuple(out)
```
